```python
import math
import jax
import jax.numpy as jnp
from jax import lax
import numpy as np

D_MODEL = 1024
BATCH = 2
SEQ = 16384
DEPTH = 1

MEM_LEN = 256
EPS = 1e-6
N_BRANCH = 3

SSM_WIDTH = D_MODEL // 2
SSM_GROUP = 16
SSM_GROUPS = SSM_WIDTH // SSM_GROUP
SSM_STATE = 64
DT_MIN = 1e-3
DT_MAX = 1e-1

MOBA_HEAD_DIM = 64
MOBA_HEADS = (D_MODEL // 2) // MOBA_HEAD_DIM
MOBA_WIDTH = MOBA_HEADS * MOBA_HEAD_DIM
MOBA_BLOCK = 256
MOBA_TOPK = 3
MOBA_QCHUNK = 64
ROPE_THETA = 10000.0

X_HEADS = 4
X_HEAD_DIM = (D_MODEL // 2) // X_HEADS
X_WIDTH = X_HEADS * X_HEAD_DIM

IN_SPLITS = (SSM_WIDTH, SSM_WIDTH + MOBA_WIDTH, SSM_WIDTH + 2 * MOBA_WIDTH, SSM_WIDTH + 3 * MOBA_WIDTH, SSM_WIDTH + 3 * MOBA_WIDTH + X_WIDTH)
IN_COLS = IN_SPLITS[-1] + N_BRANCH * D_MODEL

N_EXPERTS = 32
TOPK_EXPERTS = 4
D_EXPERT = D_MODEL
SWIGLU_LIMIT = 7.0
SWIGLU_ALPHA = 1.702
EXPERT_ROWS = 512

kernel_name = "hybrid_s5_moba_xattn_moe_layer"


def rms_norm(x, g):
    xf = x.astype(jnp.float32)
    y = xf * lax.rsqrt(jnp.mean(xf * xf, axis=-1, keepdims=True) + EPS)
    return (y * g.astype(jnp.float32)).astype(x.dtype)


def rope(x):
    L, dh = x.shape[1], x.shape[3]
    half = dh // 2
    inv = ROPE_THETA ** (-jnp.arange(half, dtype=jnp.float32) / half)
    ang = jnp.arange(L, dtype=jnp.float32)[:, None] * inv[None, :]
    cos = jnp.cos(ang)[None, :, None, :]
    sin = jnp.sin(ang)[None, :, None, :]
    x1 = x[..., :half].astype(jnp.float32)
    x2 = x[..., half:].astype(jnp.float32)
    return jnp.concatenate([x1 * cos - x2 * sin, x2 * cos + x1 * sin], axis=-1).astype(x.dtype)


def s5_glu(u, lam_re, lam_im, log_dt, b_re, b_im, c_re, c_im, d_skip, w_glu):
    Bsz, L, _ = u.shape
    ug = u.reshape(Bsz, L, SSM_GROUPS, SSM_GROUP)
    dt = jnp.exp(log_dt)[:, None]
    mag = jnp.exp(lam_re * dt)
    ar = mag * jnp.cos(lam_im * dt)
    ai = mag * jnp.sin(lam_im * dt)
    nr = ar - 1.0
    den = lam_re * lam_re + lam_im * lam_im
    cr = (nr * lam_re + ai * lam_im) / den
    ci = (ai * lam_re - nr * lam_im) / den
    bbr = cr[..., None] * b_re - ci[..., None] * b_im
    bbi = cr[..., None] * b_im + ci[..., None] * b_re
    xr = jnp.einsum('blgh,gph->blgp', ug, bbr)
    xi = jnp.einsum('blgh,gph->blgp', ug, bbi)
    a_r = jnp.broadcast_to(ar, (1, L) + ar.shape)
    a_i = jnp.broadcast_to(ai, (1, L) + ai.shape)

    def combine(e1, e2):
        a1r, a1i, s1r, s1i = e1
        a2r, a2i, s2r, s2i = e2
        return (a2r * a1r - a2i * a1i,
                a2r * a1i + a2i * a1r,
                a2r * s1r - a2i * s1i + s2r,
                a2r * s1i + a2i * s1r + s2i)

    _, _, sr, si = lax.associative_scan(combine, (a_r, a_i, xr, xi), axis=1)
    y = jnp.einsum('blgp,ghp->blgh', sr, c_re) - jnp.einsum('blgp,ghp->blgh', si, c_im)
    y = y.reshape(Bsz, L, SSM_WIDTH) + d_skip * u
    z = jax.nn.gelu(y) @ w_glu
    return z[..., :D_MODEL] * jax.nn.sigmoid(z[..., D_MODEL:])


def moba_attention(q, k, v):
    Bsz, H, L, dh = q.shape
    nb = -(-L // MOBA_BLOCK)
    ksel = min(MOBA_TOPK, nb)
    pad = nb * MOBA_BLOCK - L
    kb = jnp.pad(k, ((0, 0), (0, 0), (0, pad), (0, 0))).reshape(Bsz, H, nb, MOBA_BLOCK, dh)
    vb = jnp.pad(v, ((0, 0), (0, 0), (0, pad), (0, 0))).reshape(Bsz, H, nb, MOBA_BLOCK, dh)
    kmean = jnp.mean(kb.astype(jnp.float32), axis=3)
    kb_flat = kb.reshape(Bsz * H * nb, MOBA_BLOCK, dh)
    vb_flat = vb.reshape(Bsz * H * nb, MOBA_BLOCK, dh)
    bh_base = ((jnp.arange(Bsz)[:, None] * H + jnp.arange(H)[None, :]) * nb)[:, :, None, None]
    scale = dh ** -0.5
    blk_ids = jnp.arange(nb)
    kpos_in = jnp.arange(MOBA_BLOCK)
    qoff = jnp.arange(MOBA_QCHUNK)

    def chunk(c):
        q0 = c * MOBA_QCHUNK
        qblk = q0 // MOBA_BLOCK
        qc = lax.dynamic_slice_in_dim(q, q0, MOBA_QCHUNK, axis=2)
        gate = jnp.einsum('bhqd,bhnd->bhqn', qc.astype(jnp.float32), kmean)
        gate = jnp.where(blk_ids < qblk, gate, -jnp.inf)
        _, sel = lax.top_k(gate, ksel)
        sel_ok = sel < qblk
        gidx = bh_base + sel
        kg = kb_flat[gidx]
        vg = vb_flat[gidx]
        s_sel = jnp.einsum('bhqd,bhqskd->bhqsk', qc, kg).astype(jnp.float32) * scale
        s_sel = jnp.where(sel_ok[..., None], s_sel, -jnp.inf).reshape(Bsz, H, MOBA_QCHUNK, ksel * MOBA_BLOCK)
        k_own = lax.dynamic_index_in_dim(kb, qblk, axis=2, keepdims=False)
        v_own = lax.dynamic_index_in_dim(vb, qblk, axis=2, keepdims=False)
        s_own = jnp.einsum('bhqd,bhkd->bhqk', qc, k_own).astype(jnp.float32) * scale
        causal = (qblk * MOBA_BLOCK + kpos_in)[None, :] <= (q0 + qoff)[:, None]
        s_own = jnp.where(causal, s_own, -jnp.inf)
        p = jax.nn.softmax(jnp.concatenate([s_sel, s_own], axis=-1), axis=-1).astype(v.dtype)
        p_sel = p[..., :ksel * MOBA_BLOCK].reshape(Bsz, H, MOBA_QCHUNK, ksel, MOBA_BLOCK)
        p_own = p[..., ksel * MOBA_BLOCK:]
        return (jnp.einsum('bhqsk,bhqskd->bhqd', p_sel, vg)
                + jnp.einsum('bhqk,bhkd->bhqd', p_own, v_own))

    out = lax.map(chunk, jnp.arange(L // MOBA_QCHUNK))
    return out.transpose(1, 0, 3, 2, 4).reshape(Bsz, L, H * dh)


def memory_cross_attention(xq, mem, g_mem, w_kv_mem, g_cq, g_ck):
    Bsz, L, _ = xq.shape
    M = mem.shape[1]
    q = rms_norm(xq.reshape(Bsz, L, X_HEADS, X_HEAD_DIM), g_cq)
    kv = rms_norm(mem, g_mem) @ w_kv_mem
    k = rms_norm(kv[..., :X_WIDTH].reshape(Bsz, M, X_HEADS, X_HEAD_DIM), g_ck)
    v = kv[..., X_WIDTH:].reshape(Bsz, M, X_HEADS, X_HEAD_DIM)
    s = jnp.einsum('blhd,bmhd->bhlm', q, k).astype(jnp.float32) * (X_HEAD_DIM ** -0.5)
    p = jax.nn.softmax(s, axis=-1).astype(v.dtype)
    return jnp.einsum('bhlm,bmhd->blhd', p, v).reshape(Bsz, L, X_WIDTH)


def moe_ffn(h, w_router, b_router, w_gu, b_gu, w_down, b_down):
    Bsz, L, D = h.shape
    T = Bsz * L
    TK = T * TOPK_EXPERTS
    ht = h.reshape(T, D)
    logits = (ht @ w_router + b_router).astype(jnp.float32)
    top_v, top_e = lax.top_k(logits, TOPK_EXPERTS)
    weights = jax.nn.softmax(top_v, axis=-1)
    e_flat = top_e.reshape(TK)
    tok_flat = jnp.arange(TK, dtype=jnp.int32) // TOPK_EXPERTS
    w_flat = weights.reshape(TK)
    order = jnp.argsort(e_flat)
    e_s, tok_s, w_s = e_flat[order], tok_flat[order], w_flat[order]
    counts = jnp.bincount(e_flat, length=N_EXPERTS)
    starts = jnp.cumsum(counts) - counts
    pcounts = ((counts + EXPERT_ROWS - 1) // EXPERT_ROWS) * EXPERT_ROWS
    pends = jnp.cumsum(pcounts)
    pstarts = pends - pcounts
    dest = pstarts[e_s] + (jnp.arange(TK) - starts[e_s])
    nblk = -(-TK // EXPERT_ROWS) + N_EXPERTS
    P = nblk * EXPERT_ROWS
    slot_tok = jnp.full((P,), T, jnp.int32).at[dest].set(tok_s)
    slot_w = jnp.zeros((P,), h.dtype).at[dest].set(w_s.astype(h.dtype))
    blk_e = jnp.minimum(jnp.searchsorted(pends, jnp.arange(nblk) * EXPERT_ROWS, side='right'), N_EXPERTS - 1)
    h_pad = jnp.concatenate([ht, jnp.zeros((1, D), ht.dtype)], axis=0)
    xs = h_pad[slot_tok].reshape(nblk, EXPERT_ROWS, D)

    def expert_block(args):
        xb, e = args
        gu = xb @ w_gu[e] + b_gu[e]
        gate = jnp.minimum(gu[:, :D_EXPERT], SWIGLU_LIMIT)
        up = jnp.clip(gu[:, D_EXPERT:], -SWIGLU_LIMIT, SWIGLU_LIMIT)
        act = gate * jax.nn.sigmoid(SWIGLU_ALPHA * gate) * (up + 1.0)
        return act @ w_down[e] + b_down[e]

    ys = lax.map(expert_block, (xs, blk_e)).reshape(P, D)
    out = jnp.zeros((T + 1, D), h.dtype).at[slot_tok].add(ys * slot_w[:, None])
    return out[:T].reshape(Bsz, L, D)


def setup_inputs(seed: int = 0) -> dict:
    key = jax.random.key(seed)
    ks = jax.random.split(key, 28)
    f32 = jnp.float32

    def nrm(k, shape, scale):
        return jax.random.normal(k, shape, f32) * scale

    def gain(k, shape):
        return 1.0 + 0.01 * jax.random.normal(k, shape, f32)

    NL, D = DEPTH, D_MODEL
    G, P, HG = SSM_GROUPS, SSM_STATE, SSM_GROUP
    n = jnp.arange(P, dtype=f32)
    return {
        "x": nrm(ks[0], (BATCH, SEQ, D), 1.0),
        "mem": nrm(ks[1], (BATCH, MEM_LEN, D), 1.0),
        "g_mix": gain(ks[2], (NL, D)),
        "w_in": nrm(ks[3], (NL, D, IN_COLS), D ** -0.5),
        "lam_re": -0.5 + 0.01 * jax.random.normal(ks[4], (NL, G, P), f32),
        "lam_im": jnp.broadcast_to(math.pi * n, (NL, G, P)),
        "log_dt": jax.random.uniform(ks[5], (NL, G), f32, math.log(DT_MIN), math.log(DT_MAX)),
        "b_re": nrm(ks[6], (NL, G, P, HG), HG ** -0.5),
        "b_im": nrm(ks[7], (NL, G, P, HG), HG ** -0.5),
        "c_re": nrm(ks[8], (NL, G, HG, P), P ** -0.5),
        "c_im": nrm(ks[9], (NL, G, HG, P), P ** -0.5),
        "d_skip": nrm(ks[10], (NL, SSM_WIDTH), 1.0),
        "w_glu": nrm(ks[11], (NL, SSM_WIDTH, 2 * D), SSM_WIDTH ** -0.5),
        "g_q": gain(ks[12], (NL, MOBA_HEAD_DIM)),
        "g_k": gain(ks[13], (NL, MOBA_HEAD_DIM)),
        "w_moba_out": nrm(ks[14], (NL, MOBA_WIDTH, D), MOBA_WIDTH ** -0.5),
        "g_mem": gain(ks[15], (NL, D)),
        "w_kv_mem": nrm(ks[16], (NL, D, 2 * X_WIDTH), D ** -0.5),
        "g_cq": gain(ks[17], (NL, X_HEAD_DIM)),
        "g_ck": gain(ks[18], (NL, X_HEAD_DIM)),
        "w_cross_out": nrm(ks[19], (NL, X_WIDTH, D), X_WIDTH ** -0.5),
        "w_out": nrm(ks[20], (NL, D, D), D ** -0.5),
        "g_ffn": gain(ks[21], (NL, D)),
        "w_router": nrm(ks[22], (NL, D, N_EXPERTS), D ** -0.5),
        "b_router": nrm(ks[23], (NL, N_EXPERTS), 0.01),
        "w_gu": nrm(ks[24], (NL, N_EXPERTS, D, 2 * D_EXPERT), D ** -0.5),
        "b_gu": nrm(ks[25], (NL, N_EXPERTS, 2 * D_EXPERT), 0.01),
        "w_down": nrm(ks[26], (NL, N_EXPERTS, D_EXPERT, D), D_EXPERT ** -0.5),
        "b_down": nrm(ks[27], (NL, N_EXPERTS, D), 0.01),
    }


def reference(x, mem, g_mix, w_in, lam_re, lam_im, log_dt, b_re, b_im, c_re, c_im, d_skip, w_glu,
              g_q, g_k, w_moba_out, g_mem, w_kv_mem, g_cq, g_ck, w_cross_out, w_out, g_ffn,
              w_router, b_router, w_gu, b_gu, w_down, b_down):
    Bsz, L, D = x.shape
    for l in range(DEPTH):
        h = rms_norm(x, g_mix[l])
        u, q, k, v, xq, gate_logits = jnp.split(h @ w_in[l], IN_SPLITS, axis=-1)
        y_ssm = s5_glu(u, lam_re[l], lam_im[l], log_dt[l], b_re[l], b_im[l], c_re[l], c_im[l], d_skip[l], w_glu[l])
        q = rope(rms_norm(q.reshape(Bsz, L, MOBA_HEADS, MOBA_HEAD_DIM), g_q[l])).transpose(0, 2, 1, 3)
        k = rope(rms_norm(k.reshape(Bsz, L, MOBA_HEADS, MOBA_HEAD_DIM), g_k[l])).transpose(0, 2, 1, 3)
        v = v.reshape(Bsz, L, MOBA_HEADS, MOBA_HEAD_DIM).transpose(0, 2, 1, 3)
        y_moba = moba_attention(q, k, v) @ w_moba_out[l]
        y_mem = memory_cross_attention(xq, mem, g_mem[l], w_kv_mem[l], g_cq[l], g_ck[l]) @ w_cross_out[l]
        g = jax.nn.sigmoid(gate_logits.reshape(Bsz, L, N_BRANCH, D))
        merged = g[:, :, 0] * y_ssm + g[:, :, 1] * y_moba + g[:, :, 2] * y_mem
        x = x + merged @ w_out[l]
        x = x + moe_ffn(rms_norm(x, g_ffn[l]), w_router[l], b_router[l], w_gu[l], b_gu[l], w_down[l], b_down[l])
    return x
```

```python
import functools
import math

import jax
import jax.numpy as jnp
from jax import lax
from jax.experimental import pallas as pl
from jax.experimental.pallas import tpu as pltpu

F32 = jnp.float32
BF16 = jnp.bfloat16

EPS = 1e-6
N_BRANCH = 3
SSM_GROUP = 16
SSM_STATE = 64
S5_CHUNK = 16
MOBA_HEAD_DIM = 64
MOBA_BLOCK = 256
MOBA_TOPK = 3
MOBA_MAX_BLOCKS = 64
ROPE_THETA = 10000.0
X_HEADS = 4
N_EXPERTS = 32
TOPK_EXPERTS = 4
SWIGLU_LIMIT = 7.0
SWIGLU_ALPHA = 1.702
EXPERT_ROWS = 512
NEG_BIG = -1e30
LANES = 128
VMEM_LIMIT_BYTES = 56 * 1024 * 1024


def _params(*sem):
    return pltpu.CompilerParams(dimension_semantics=sem, vmem_limit_bytes=VMEM_LIMIT_BYTES)


def _sigmoid(x):
    return 1.0 / (1.0 + jnp.exp(-x))


def _dot(a, b):
    return jnp.dot(a, b, preferred_element_type=F32)


def _dot_nt(a, b):
    return lax.dot_general(a, b, (((1,), (1,)), ((), ())), preferred_element_type=F32)


def _inproj_kernel(x_ref, gmix_ref, wa_ref, wg_ref, e64_ref, gq_ref, gk_ref, gcq_ref, cos_ref, sin_ref,
                   u_ref, q_ref, k_ref, v_ref, xq_ref, g_ref):
    xf = x_ref[...]
    ms = jnp.mean(xf * xf, axis=-1, keepdims=True)
    h = (xf * lax.rsqrt(ms + EPS) * gmix_ref[...]).astype(BF16)
    a = _dot(h, wa_ref[...])
    w = u_ref.shape[1]
    u_ref[...] = a[:, :w].astype(BF16)
    v_ref[...] = a[:, 3 * w:4 * w].astype(BF16)

    cos = jnp.tile(cos_ref[...], (1, w // LANES))
    sin = jnp.tile(sin_ref[...], (1, w // LANES))
    lane = lax.broadcasted_iota(jnp.int32, (xf.shape[0], w), 1)
    first_half = (lane % MOBA_HEAD_DIM) < (MOBA_HEAD_DIM // 2)

    def qk_norm_rope(raw, g):
        ss = _dot((raw * raw).astype(BF16), e64_ref[...])
        n = raw * lax.rsqrt(ss * (1.0 / MOBA_HEAD_DIM) + EPS) * g
        rot = jnp.where(first_half,
                        pltpu.roll(n, w - MOBA_HEAD_DIM // 2, 1),
                        pltpu.roll(n, MOBA_HEAD_DIM // 2, 1))
        return n * cos + rot * sin

    q_ref[...] = (qk_norm_rope(a[:, w:2 * w], gq_ref[...]) * (MOBA_HEAD_DIM ** -0.5)).astype(BF16)
    k_ref[...] = qk_norm_rope(a[:, 2 * w:3 * w], gk_ref[...]).astype(BF16)

    xq = a[:, 4 * w:5 * w]
    hd = w // X_HEADS
    for c in range(X_HEADS):
        chunk = xq[:, c * hd:(c + 1) * hd]
        cms = jnp.mean(chunk * chunk, axis=-1, keepdims=True)
        xq_ref[:, c * hd:(c + 1) * hd] = (chunk * lax.rsqrt(cms + EPS) * gcq_ref[...]).astype(BF16)

    d = xf.shape[1]
    for c in range(N_BRANCH):
        z = _dot(h, wg_ref[:, c * d:(c + 1) * d])
        g_ref[:, c * d:(c + 1) * d] = _sigmoid(z).astype(BF16)


def _in_proj(xt, g_mix, w_in, g_q, g_k, g_cq, seq_len, tm=256):
    t, d = xt.shape
    w = d // 2
    wa = w_in[:, :5 * w].astype(BF16)
    wg = w_in[:, 5 * w:].astype(BF16)
    heads = w // MOBA_HEAD_DIM
    e64 = jnp.kron(jnp.eye(heads, dtype=F32), jnp.ones((MOBA_HEAD_DIM, MOBA_HEAD_DIM), F32)).astype(BF16)
    half = MOBA_HEAD_DIM // 2
    inv = ROPE_THETA ** (-jnp.arange(half, dtype=F32) / half)
    ang = jnp.arange(seq_len, dtype=F32)[:, None] * inv[None, :]
    cos = jnp.tile(jnp.cos(ang), (1, LANES // half))
    sin = jnp.tile(jnp.concatenate([-jnp.sin(ang), jnp.sin(ang)], axis=1), (1, LANES // MOBA_HEAD_DIM))
    nt = seq_len // tm
    row = lambda i: (i, 0)
    const = lambda i: (0, 0)
    out_w = jax.ShapeDtypeStruct((t, w), BF16)
    return pl.pallas_call(
        _inproj_kernel,
        grid=(t // tm,),
        in_specs=[
            pl.BlockSpec((tm, d), row),
            pl.BlockSpec((1, d), const),
            pl.BlockSpec((d, 5 * w), const),
            pl.BlockSpec((d, N_BRANCH * d), const),
            pl.BlockSpec((w, w), const),
            pl.BlockSpec((1, w), const),
            pl.BlockSpec((1, w), const),
            pl.BlockSpec((1, w // X_HEADS), const),
            pl.BlockSpec((tm, LANES), lambda i: (i % nt, 0)),
            pl.BlockSpec((tm, LANES), lambda i: (i % nt, 0)),
        ],
        out_specs=[pl.BlockSpec((tm, w), row)] * 5 + [pl.BlockSpec((tm, N_BRANCH * d), row)],
        out_shape=[out_w] * 5 + [jax.ShapeDtypeStruct((t, N_BRANCH * d), BF16)],
        compiler_params=_params("parallel"),
        name="in_proj",
    )(xt, g_mix.reshape(1, d), wa, wg, e64,
      jnp.tile(g_q, heads).reshape(1, w), jnp.tile(g_k, heads).reshape(1, w), g_cq.reshape(1, -1), cos, sin)


def _s5_matrices(lam_re, lam_im, log_dt, b_re, b_im, c_re, c_im, n_chunks):
    hp = lax.Precision.HIGHEST
    c = S5_CHUNK
    dt = jnp.exp(log_dt)[:, None]
    mag = jnp.exp(lam_re * dt)
    ar = mag * jnp.cos(lam_im * dt)
    ai = mag * jnp.sin(lam_im * dt)
    nr = ar - 1.0
    den = lam_re * lam_re + lam_im * lam_im
    cr = (nr * lam_re + ai * lam_im) / den
    ci = (ai * lam_re - nr * lam_im) / den
    bbr = cr[..., None] * b_re - ci[..., None] * b_im
    bbi = cr[..., None] * b_im + ci[..., None] * b_re

    def power(n):
        nf = n.astype(F32)[None, :, None]
        m = jnp.exp((lam_re * dt)[:, None, :] * nf)
        th = (lam_im * dt)[:, None, :] * nf
        return m * jnp.cos(th), m * jnp.sin(th)

    pr, pi = power(jnp.arange(c + 1))
    kbr = pr[..., None] * bbr[:, None] - pi[..., None] * bbi[:, None]
    kbi = pr[..., None] * bbi[:, None] + pi[..., None] * bbr[:, None]
    kk = (jnp.einsum('ghp,gtpc->gthc', c_re, kbr, precision=hp)
          - jnp.einsum('ghp,gtpc->gthc', c_im, kbi, precision=hp))
    tq = jnp.arange(c)
    lag = tq[None, :] - tq[:, None]
    toep = kk[:, jnp.clip(lag, 0, c)]
    toep = jnp.where((lag >= 0)[None, :, :, None, None], toep, 0.0)
    g = toep.shape[0]
    toep = toep.transpose(0, 1, 4, 2, 3).reshape(g, c * SSM_GROUP, c * SSM_GROUP)
    rev = c - 1 - tq
    w_in = jnp.concatenate([kbr[:, rev].transpose(0, 1, 3, 2), kbi[:, rev].transpose(0, 1, 3, 2)], axis=-1)
    w_in = w_in.reshape(g, c * SSM_GROUP, 2 * SSM_STATE)
    prn, pin = pr[:, 1:], pi[:, 1:]
    wo_r = c_re[:, None] * prn[:, :, None, :] - c_im[:, None] * pin[:, :, None, :]
    wo_i = -c_re[:, None] * pin[:, :, None, :] - c_im[:, None] * prn[:, :, None, :]
    w_out = jnp.concatenate([wo_r, wo_i], axis=-1).transpose(0, 3, 1, 2).reshape(g, 2 * SSM_STATE, c * SSM_GROUP)
    n_steps = max(1, int(math.ceil(math.log2(n_chunks))))
    qr, qi = power(c * (2 ** jnp.arange(n_steps)))
    pa = jnp.concatenate([qr, qr], axis=-1)
    pb = jnp.concatenate([-qi, qi], axis=-1)
    return toep.astype(BF16), w_in.astype(BF16), w_out.astype(BF16), pa, pb


def _s5_kernel(u_ref, toep_ref, win_ref, wout_ref, pa_ref, pb_ref, y_ref):
    u = u_ref[0, 0]
    s = _dot(u, win_ref[0])
    nc = s.shape[0]
    row = lax.broadcasted_iota(jnp.int32, s.shape, 0)
    n_steps = pa_ref.shape[1]
    for k in range(n_steps):
        sh = 1 << k
        if sh >= nc:
            break
        prev = jnp.where(row >= sh, pltpu.roll(s, sh, 0), 0.0)
        s = s + pa_ref[0, k:k + 1, :] * prev + pb_ref[0, k:k + 1, :] * pltpu.roll(prev, SSM_STATE, 1)
    s_start = jnp.where(row >= 1, pltpu.roll(s, 1, 0), 0.0)
    y = _dot(u, toep_ref[0]) + _dot(s_start.astype(BF16), wout_ref[0])
    y_ref[0, 0] = y.astype(BF16)


def _s5(u, mats, bsz, seq_len):
    toep, w_in, w_out, pa, pb = mats
    g = toep.shape[0]
    nc = seq_len // S5_CHUNK
    cw = S5_CHUNK * SSM_GROUP
    ug = u.reshape(bsz, nc, S5_CHUNK, g, SSM_GROUP).transpose(0, 3, 1, 2, 4).reshape(bsz, g, nc, cw)
    per_g = lambda b, j: (j, 0, 0)
    y = pl.pallas_call(
        _s5_kernel,
        grid=(bsz, g),
        in_specs=[
            pl.BlockSpec((1, 1, nc, cw), lambda b, j: (b, j, 0, 0)),
            pl.BlockSpec((1, cw, cw), per_g),
            pl.BlockSpec((1, cw, 2 * SSM_STATE), per_g),
            pl.BlockSpec((1, 2 * SSM_STATE, cw), per_g),
            pl.BlockSpec((1,) + pa.shape[1:], per_g),
            pl.BlockSpec((1,) + pb.shape[1:], per_g),
        ],
        out_specs=pl.BlockSpec((1, 1, nc, cw), lambda b, j: (b, j, 0, 0)),
        out_shape=jax.ShapeDtypeStruct((bsz, g, nc, cw), BF16),
        compiler_params=_params("parallel", "parallel"),
        name="s5_scan",
    )(ug, toep, w_in, w_out, pa, pb)
    return y.reshape(bsz, g, nc, S5_CHUNK, SSM_GROUP).transpose(0, 2, 3, 1, 4).reshape(bsz * seq_len, g * SSM_GROUP)


def _moba_select_kernel(q_ref, k_ref, bias_ref, km_ref):
    t = pl.program_id(2)
    ts = q_ref.shape[0]
    half = MOBA_MAX_BLOCKS

    @pl.when(t == 0)
    def _():
        kk = k_ref[...].astype(F32)
        nb = kk.shape[0] // MOBA_BLOCK
        km = jnp.sum(kk.reshape(nb, MOBA_BLOCK, LANES), axis=1) * (1.0 / MOBA_BLOCK)
        if nb < half:
            km = jnp.concatenate([km, jnp.zeros((half - nb, LANES), F32)], axis=0)
        lane = lax.broadcasted_iota(jnp.int32, (half, LANES), 1)
        km_ref[0:half, :] = jnp.where(lane < MOBA_HEAD_DIM, km, 0.0).astype(BF16)
        km_ref[half:2 * half, :] = jnp.where(lane >= MOBA_HEAD_DIM, km, 0.0).astype(BF16)

    gate = _dot_nt(q_ref[...], km_ref[...])
    lane = lax.broadcasted_iota(jnp.int32, gate.shape, 1)
    blk = lane % half
    qblk = (t * ts + lax.broadcasted_iota(jnp.int32, gate.shape, 0)) // MOBA_BLOCK
    gate = jnp.where(blk < qblk, gate, -jnp.inf)
    selected = jnp.zeros(gate.shape, jnp.bool_)
    for hh in range(2):
        in_half = (lane // half) == hh
        g = jnp.where(in_half, gate, -jnp.inf)
        for _ in range(MOBA_TOPK):
            mx = jnp.max(g, axis=1, keepdims=True)
            idx = jnp.min(jnp.where((g == mx) & in_half, blk, half), axis=1, keepdims=True)
            hit = in_half & (blk == idx)
            selected = selected | (hit & (idx < qblk))
            g = jnp.where(hit, -jnp.inf, g)
    bias_ref[0, 0] = jnp.where(selected, 0.0, NEG_BIG).astype(BF16)


def _moba_select(q2, k2, bsz, seq_len, ts=512):
    ts = min(ts, seq_len)
    npair = q2.shape[1] // LANES
    nt = seq_len // ts
    return pl.pallas_call(
        _moba_select_kernel,
        grid=(bsz, npair, nt),
        in_specs=[
            pl.BlockSpec((ts, LANES), lambda b, p, t: (b * nt + t, p)),
            pl.BlockSpec((seq_len, LANES), lambda b, p, t: (b, p)),
        ],
        out_specs=pl.BlockSpec((1, 1, ts, LANES), lambda b, p, t: (b, p, t, 0)),
        out_shape=jax.ShapeDtypeStruct((bsz, npair, seq_len, LANES), BF16),
        scratch_shapes=[pltpu.VMEM((2 * MOBA_MAX_BLOCKS, LANES), BF16)],
        compiler_params=_params("parallel", "parallel", "arbitrary"),
        name="moba_select",
    )(q2, k2)


def _moba_attn_kernel(q_ref, bias_ref, k_ref, v_ref, o_ref, acc_ref, m_ref):
    i = pl.program_id(2)
    blk = MOBA_BLOCK
    q = q_ref[...]
    bias = bias_ref[0, 0]
    lane = lax.broadcasted_iota(jnp.int32, q.shape, 1)
    is_a = lane < MOBA_HEAD_DIM
    zero = jnp.zeros_like(q)
    q_aug = jnp.concatenate([
        jnp.concatenate([jnp.where(is_a, q, zero), jnp.where(is_a, bias, zero)], axis=1),
        jnp.concatenate([jnp.where(is_a, zero, q), jnp.where(is_a, zero, bias)], axis=1)], axis=0)
    ones = jnp.ones((blk, LANES), BF16)

    def step(kb, vb, onehot, causal):
        s = _dot_nt(q_aug, jnp.concatenate([kb, onehot], axis=1))
        if causal:
            r = lax.broadcasted_iota(jnp.int32, s.shape, 0) % blk
            c = lax.broadcasted_iota(jnp.int32, s.shape, 1)
            s = jnp.where(c <= r, s, NEG_BIG)
        m_prev = m_ref[...]
        m_new = jnp.maximum(m_prev, jnp.max(s, axis=1, keepdims=True))
        p = jnp.exp(s - m_new).astype(BF16)
        pv = _dot(p, jnp.concatenate([vb, ones], axis=1))
        acc_ref[...] = jnp.exp(m_prev - m_new) * acc_ref[...] + pv
        m_ref[...] = m_new

    m_ref[...] = jnp.full(m_ref.shape, NEG_BIG, F32)
    acc_ref[...] = jnp.zeros(acc_ref.shape, F32)
    own = pl.multiple_of(i * blk, blk)
    step(k_ref[pl.ds(own, blk), :], v_ref[pl.ds(own, blk), :], jnp.zeros((blk, LANES), BF16), True)

    def body(j, carry):
        start = pl.multiple_of(j * blk, blk)
        onehot = jnp.where(lane % MOBA_MAX_BLOCKS == j, 1.0, 0.0).astype(BF16)
        step(k_ref[pl.ds(start, blk), :], v_ref[pl.ds(start, blk), :], onehot, False)
        return carry

    lax.fori_loop(0, i, body, 0)
    acc = acc_ref[...]
    o_a = acc[:blk, :LANES] / acc[:blk, LANES:LANES + 1]
    o_b = acc[blk:, :LANES] / acc[blk:, LANES:LANES + 1]
    o_ref[...] = jnp.where(is_a, o_a, o_b).astype(BF16)


def _moba_attn(q2, k2, v2, bias, bsz, seq_len):
    npair = q2.shape[1] // LANES
    nb = seq_len // MOBA_BLOCK
    return pl.pallas_call(
        _moba_attn_kernel,
        grid=(bsz, npair, nb),
        in_specs=[
            pl.BlockSpec((MOBA_BLOCK, LANES), lambda b, p, i: (b * nb + i, p)),
            pl.BlockSpec((1, 1, MOBA_BLOCK, LANES), lambda b, p, i: (b, p, i, 0)),
            pl.BlockSpec((seq_len, LANES), lambda b, p, i: (b, p)),
            pl.BlockSpec((seq_len, LANES), lambda b, p, i: (b, p)),
        ],
        out_specs=pl.BlockSpec((MOBA_BLOCK, LANES), lambda b, p, i: (b * nb + i, p)),
        out_shape=jax.ShapeDtypeStruct(q2.shape, BF16),
        scratch_shapes=[pltpu.VMEM((2 * MOBA_BLOCK, 2 * LANES), F32), pltpu.VMEM((2 * MOBA_BLOCK, 1), F32)],
        compiler_params=_params("parallel", "parallel", "arbitrary"),
        name="moba_attn",
    )(q2, bias, k2, v2)


def _mem_kv_kernel(mem_ref, gmem_ref, w_ref, gck_ref, k_ref, v_ref):
    xf = mem_ref[0]
    ms = jnp.mean(xf * xf, axis=-1, keepdims=True)
    h = (xf * lax.rsqrt(ms + EPS) * gmem_ref[...]).astype(BF16)
    kv = _dot(h, w_ref[...])
    w = k_ref.shape[2]
    hd = w // X_HEADS
    for c in range(X_HEADS):
        chunk = kv[:, c * hd:(c + 1) * hd]
        cms = jnp.mean(chunk * chunk, axis=-1, keepdims=True)
        k_ref[0, :, c * hd:(c + 1) * hd] = (chunk * lax.rsqrt(cms + EPS) * gck_ref[...]).astype(BF16)
    v_ref[0] = kv[:, w:].astype(BF16)


def _mem_kv(mem, g_mem, w_kv_mem, g_ck):
    bsz, m, d = mem.shape
    w = w_kv_mem.shape[1] // 2
    const = lambda b: (0, 0)
    out = jax.ShapeDtypeStruct((bsz, m, w), BF16)
    return pl.pallas_call(
        _mem_kv_kernel,
        grid=(bsz,),
        in_specs=[pl.BlockSpec((1, m, d), lambda b: (b, 0, 0)), pl.BlockSpec((1, d), const),
                  pl.BlockSpec((d, 2 * w), const), pl.BlockSpec((1, w // X_HEADS), const)],
        out_specs=[pl.BlockSpec((1, m, w), lambda b: (b, 0, 0))] * 2,
        out_shape=[out, out],
        compiler_params=_params("parallel"),
        name="mem_kv",
    )(mem, g_mem.reshape(1, d), w_kv_mem.astype(BF16), g_ck.reshape(1, -1))


def _merge_kernel(x_ref, ys_ref, u_ref, dskip_ref, om_ref, xq_ref, kc_ref, vc_ref, g_ref,
                  wglu_ref, wmo_ref, wco_ref, wout_ref, gffn_ref, wr_ref, br_ref,
                  x1_ref, h2_ref, logit_ref):
    d = x_ref.shape[1]
    y = ys_ref[...].astype(F32) + dskip_ref[...] * u_ref[...].astype(F32)
    ge = 0.5 * y * (1.0 + jnp.tanh(math.sqrt(2.0 / math.pi) * (y + 0.044715 * (y * y * y))))
    z = _dot(ge.astype(BF16), wglu_ref[...])
    merged = g_ref[:, 0:d].astype(F32) * (z[:, :d] * _sigmoid(z[:, d:]))
    merged = merged + g_ref[:, d:2 * d].astype(F32) * _dot(om_ref[...], wmo_ref[...])
    w = xq_ref.shape[1]
    hd = w // X_HEADS
    heads = []
    for c in range(X_HEADS):
        s = _dot_nt(xq_ref[:, c * hd:(c + 1) * hd], kc_ref[0, :, c * hd:(c + 1) * hd]) * (hd ** -0.5)
        p = jnp.exp(s - jnp.max(s, axis=1, keepdims=True))
        p = p / jnp.sum(p, axis=1, keepdims=True)
        heads.append(_dot(p.astype(BF16), vc_ref[0, :, c * hd:(c + 1) * hd]))
    oc = jnp.concatenate(heads, axis=1).astype(BF16)
    merged = merged + g_ref[:, 2 * d:3 * d].astype(F32) * _dot(oc, wco_ref[...])
    x1 = x_ref[...] + _dot(merged.astype(BF16), wout_ref[...])
    x1_ref[...] = x1
    ms = jnp.mean(x1 * x1, axis=-1, keepdims=True)
    h2 = (x1 * lax.rsqrt(ms + EPS) * gffn_ref[...]).astype(BF16)
    h2_ref[...] = h2
    logit_ref[...] = _dot(h2, wr_ref[...]) + br_ref[...]


def _merge(xt, ys, u, d_skip, om, xqn, kc, vc, gates, w_glu, w_mo, w_co, w_out, g_ffn, w_router, b_router,
           seq_len, tm=256):
    t, d = xt.shape
    w = d // 2
    m = kc.shape[1]
    ne = w_router.shape[1]
    nt = seq_len // tm
    row = lambda i: (i, 0)
    const = lambda i: (0, 0)
    per_b = lambda i: (i // nt, 0, 0)
    return pl.pallas_call(
        _merge_kernel,
        grid=(t // tm,),
        in_specs=[
            pl.BlockSpec((tm, d), row), pl.BlockSpec((tm, w), row), pl.BlockSpec((tm, w), row),
            pl.BlockSpec((1, w), const), pl.BlockSpec((tm, w), row), pl.BlockSpec((tm, w), row),
            pl.BlockSpec((1, m, w), per_b), pl.BlockSpec((1, m, w), per_b),
            pl.BlockSpec((tm, N_BRANCH * d), row),
            pl.BlockSpec((w, 2 * d), const), pl.BlockSpec((w, d), const), pl.BlockSpec((w, d), const),
            pl.BlockSpec((d, d), const), pl.BlockSpec((1, d), const),
            pl.BlockSpec((d, ne), const), pl.BlockSpec((1, ne), const),
        ],
        out_specs=[pl.BlockSpec((tm, d), row), pl.BlockSpec((tm, d), row), pl.BlockSpec((tm, ne), row)],
        out_shape=[jax.ShapeDtypeStruct((t, d), F32), jax.ShapeDtypeStruct((t, d), BF16),
                   jax.ShapeDtypeStruct((t, ne), F32)],
        compiler_params=_params("parallel"),
        name="merge",
    )(xt, ys, u, d_skip.reshape(1, w), om, xqn, kc, vc, gates,
      w_glu.astype(BF16), w_mo.astype(BF16), w_co.astype(BF16), w_out.astype(BF16),
      g_ffn.reshape(1, d), w_router.astype(BF16), b_router.reshape(1, ne))


def _moe_kernel(blk_e_ref, blk_used_ref, xs_ref, wgu_ref, bgu_ref, wd_ref, bd_ref, y_ref, wgu_bf, wd_bf):
    i = pl.program_id(0)
    prev = blk_e_ref[jnp.maximum(i - 1, 0)]

    @pl.when((i == 0) | (blk_e_ref[i] != prev))
    def _():
        wgu_bf[...] = wgu_ref[0].astype(BF16)
        wd_bf[...] = wd_ref[0].astype(BF16)

    @pl.when(blk_used_ref[i] > 0)
    def _():
        de = wd_bf.shape[0]
        gu = _dot(xs_ref[...], wgu_bf[...]) + bgu_ref[0]
        gate = jnp.minimum(gu[:, :de], SWIGLU_LIMIT)
        up = jnp.clip(gu[:, de:], -SWIGLU_LIMIT, SWIGLU_LIMIT)
        act = gate * _sigmoid(SWIGLU_ALPHA * gate) * (up + 1.0)
        y_ref[...] = _dot(act.astype(BF16), wd_bf[...]) + bd_ref[0]

    @pl.when(blk_used_ref[i] == 0)
    def _():
        y_ref[...] = jnp.zeros(y_ref.shape, y_ref.dtype)


def _moe_experts(xs, blk_e, blk_used, w_gu, b_gu, w_down, b_down):
    p, d = xs.shape
    ne, _, de2 = w_gu.shape
    de = de2 // 2
    nblk = p // EXPERT_ROWS
    grid_spec = pltpu.PrefetchScalarGridSpec(
        num_scalar_prefetch=2,
        grid=(nblk,),
        in_specs=[
            pl.BlockSpec((EXPERT_ROWS, d), lambda i, e, n: (i, 0)),
            pl.BlockSpec((1, d, de2), lambda i, e, n: (e[i], 0, 0)),
            pl.BlockSpec((1, 1, de2), lambda i, e, n: (e[i], 0, 0)),
            pl.BlockSpec((1, de, d), lambda i, e, n: (e[i], 0, 0)),
            pl.BlockSpec((1, 1, d), lambda i, e, n: (e[i], 0, 0)),
        ],
        out_specs=pl.BlockSpec((EXPERT_ROWS, d), lambda i, e, n: (i, 0)),
        scratch_shapes=[pltpu.VMEM((d, de2), BF16), pltpu.VMEM((de, d), BF16)],
    )
    return pl.pallas_call(
        _moe_kernel,
        grid_spec=grid_spec,
        out_shape=jax.ShapeDtypeStruct((p, d), F32),
        compiler_params=_params("arbitrary"),
        name="moe_experts",
    )(blk_e, blk_used, xs, w_gu, b_gu.reshape(ne, 1, de2), w_down, b_down.reshape(ne, 1, d))


def _moe_ffn(x1, h2, logits, w_gu, b_gu, w_down, b_down):
    t, d = x1.shape
    tk = t * TOPK_EXPERTS
    top_v, top_e = lax.top_k(logits, TOPK_EXPERTS)
    weights = jax.nn.softmax(top_v, axis=-1)
    e_flat = top_e.reshape(tk)
    onehot = (e_flat[:, None] == jnp.arange(N_EXPERTS, dtype=e_flat.dtype)[None, :]).astype(jnp.int32)
    csum = jnp.cumsum(onehot, axis=0)
    rank = jnp.take_along_axis(csum, e_flat[:, None], axis=1)[:, 0] - 1
    counts = csum[-1]
    pcounts = ((counts + EXPERT_ROWS - 1) // EXPERT_ROWS) * EXPERT_ROWS
    pends = jnp.cumsum(pcounts)
    pstarts = pends - pcounts
    dest = pstarts[e_flat] + rank
    nblk = -(-tk // EXPERT_ROWS) + N_EXPERTS
    p = nblk * EXPERT_ROWS
    tok = jnp.arange(tk, dtype=jnp.int32) // TOPK_EXPERTS
    slot_tok = jnp.zeros((p,), jnp.int32).at[dest].set(tok)
    blk_start = jnp.arange(nblk, dtype=jnp.int32) * EXPERT_ROWS
    blk_e = jnp.minimum(jnp.searchsorted(pends, blk_start, side='right'), N_EXPERTS - 1).astype(jnp.int32)
    blk_used = (blk_start < pends[-1]).astype(jnp.int32)
    xs = h2[slot_tok]
    ys = _moe_experts(xs, blk_e, blk_used, w_gu, b_gu, w_down, b_down)
    picked = ys[dest].reshape(t, TOPK_EXPERTS, d)
    return x1 + jnp.sum(picked * weights[:, :, None], axis=1)


def kernel(x, mem, g_mix, w_in, lam_re, lam_im, log_dt, b_re, b_im, c_re, c_im, d_skip, w_glu, g_q, g_k, w_moba_out, g_mem, w_kv_mem, g_cq, g_ck, w_cross_out, w_out, g_ffn, w_router, b_router, w_gu, b_gu, w_down, b_down):
    bsz, seq_len, d = x.shape
    xt = x.reshape(bsz * seq_len, d)
    for l in range(g_mix.shape[0]):
        u, q2, k2, v2, xqn, gates = _in_proj(xt, g_mix[l], w_in[l], g_q[l], g_k[l], g_cq[l], seq_len)
        mats = _s5_matrices(lam_re[l], lam_im[l], log_dt[l], b_re[l], b_im[l], c_re[l], c_im[l],
                            seq_len // S5_CHUNK)
        ys = _s5(u, mats, bsz, seq_len)
        bias = _moba_select(q2, k2, bsz, seq_len)
        om = _moba_attn(q2, k2, v2, bias, bsz, seq_len)
        kc, vc = _mem_kv(mem, g_mem[l], w_kv_mem[l], g_ck[l])
        x1, h2, logits = _merge(xt, ys, u, d_skip[l], om, xqn, kc, vc, gates, w_glu[l], w_moba_out[l],
                                w_cross_out[l], w_out[l], g_ffn[l], w_router[l], b_router[l], seq_len)
        xt = _moe_ffn(x1, h2, logits, w_gu[l], b_gu[l], w_down[l], b_down[l])
    return xt.reshape(bsz, seq_len, d)
```

```python
import functools
import math

import jax
import jax.numpy as jnp
from jax import lax
from jax.experimental import pallas as pl
from jax.experimental.pallas import tpu as pltpu

F32 = jnp.float32
BF16 = jnp.bfloat16

EPS = 1e-6
N_BRANCH = 3
SSM_GROUP = 16
SSM_STATE = 64
S5_CHUNK = 16
MOBA_HEAD_DIM = 64
MOBA_BLOCK = 256
MOBA_TOPK = 3
MOBA_MAX_BLOCKS = 64
ROPE_THETA = 10000.0
X_HEADS = 4
N_EXPERTS = 32
TOPK_EXPERTS = 4
SWIGLU_LIMIT = 7.0
SWIGLU_ALPHA = 1.702
EXPERT_ROWS = 512
NEG_BIG = -1e30
LANES = 128
VMEM_LIMIT_BYTES = 56 * 1024 * 1024


def _params(*sem):
    return pltpu.CompilerParams(dimension_semantics=sem, vmem_limit_bytes=VMEM_LIMIT_BYTES)


def _sigmoid(x):
    return 1.0 / (1.0 + jnp.exp(-x))


def _dot(a, b):
    return jnp.dot(a, b, preferred_element_type=F32)


def _dot_nt(a, b):
    return lax.dot_general(a, b, (((1,), (1,)), ((), ())), preferred_element_type=F32)


def _inproj_kernel(x_ref, gmix_ref, wa_ref, wg_ref, e64_ref, gq_ref, gk_ref, gcq_ref, cos_ref, sin_ref,
                   u_ref, q_ref, k_ref, v_ref, xq_ref, g_ref):
    xf = x_ref[...]
    ms = jnp.mean(xf * xf, axis=-1, keepdims=True)
    h = (xf * lax.rsqrt(ms + EPS) * gmix_ref[...]).astype(BF16)
    a = _dot(h, wa_ref[...])
    w = u_ref.shape[1]
    u_ref[...] = a[:, :w].astype(BF16)
    v_ref[...] = a[:, 3 * w:4 * w].astype(BF16)

    cos = jnp.tile(cos_ref[...], (1, w // LANES))
    sin = jnp.tile(sin_ref[...], (1, w // LANES))
    lane = lax.broadcasted_iota(jnp.int32, (xf.shape[0], w), 1)
    first_half = (lane % MOBA_HEAD_DIM) < (MOBA_HEAD_DIM // 2)

    def qk_norm_rope(raw, g):
        ss = _dot((raw * raw).astype(BF16), e64_ref[...])
        n = raw * lax.rsqrt(ss * (1.0 / MOBA_HEAD_DIM) + EPS) * g
        rot = jnp.where(first_half,
                        pltpu.roll(n, w - MOBA_HEAD_DIM // 2, 1),
                        pltpu.roll(n, MOBA_HEAD_DIM // 2, 1))
        return n * cos + rot * sin

    q_ref[...] = (qk_norm_rope(a[:, w:2 * w], gq_ref[...]) * (MOBA_HEAD_DIM ** -0.5)).astype(BF16)
    k_ref[...] = qk_norm_rope(a[:, 2 * w:3 * w], gk_ref[...]).astype(BF16)

    xq = a[:, 4 * w:5 * w]
    hd = w // X_HEADS
    for c in range(X_HEADS):
        chunk = xq[:, c * hd:(c + 1) * hd]
        cms = jnp.mean(chunk * chunk, axis=-1, keepdims=True)
        xq_ref[:, c * hd:(c + 1) * hd] = (chunk * lax.rsqrt(cms + EPS) * gcq_ref[...]).astype(BF16)

    d = xf.shape[1]
    for c in range(N_BRANCH):
        z = _dot(h, wg_ref[:, c * d:(c + 1) * d])
        g_ref[:, c * d:(c + 1) * d] = _sigmoid(z).astype(BF16)


def _in_proj(xt, g_mix, w_in, g_q, g_k, g_cq, seq_len, tm=256):
    t, d = xt.shape
    w = d // 2
    wa = w_in[:, :5 * w].astype(BF16)
    wg = w_in[:, 5 * w:].astype(BF16)
    heads = w // MOBA_HEAD_DIM
    e64 = jnp.kron(jnp.eye(heads, dtype=F32), jnp.ones((MOBA_HEAD_DIM, MOBA_HEAD_DIM), F32)).astype(BF16)
    half = MOBA_HEAD_DIM // 2
    inv = ROPE_THETA ** (-jnp.arange(half, dtype=F32) / half)
    ang = jnp.arange(seq_len, dtype=F32)[:, None] * inv[None, :]
    cos = jnp.tile(jnp.cos(ang), (1, LANES // half))
    sin = jnp.tile(jnp.concatenate([-jnp.sin(ang), jnp.sin(ang)], axis=1), (1, LANES // MOBA_HEAD_DIM))
    nt = seq_len // tm
    row = lambda i: (i, 0)
    const = lambda i: (0, 0)
    out_w = jax.ShapeDtypeStruct((t, w), BF16)
    return pl.pallas_call(
        _inproj_kernel,
        grid=(t // tm,),
        in_specs=[
            pl.BlockSpec((tm, d), row),
            pl.BlockSpec((1, d), const),
            pl.BlockSpec((d, 5 * w), const),
            pl.BlockSpec((d, N_BRANCH * d), const),
            pl.BlockSpec((w, w), const),
            pl.BlockSpec((1, w), const),
            pl.BlockSpec((1, w), const),
            pl.BlockSpec((1, w // X_HEADS), const),
            pl.BlockSpec((tm, LANES), lambda i: (i % nt, 0)),
            pl.BlockSpec((tm, LANES), lambda i: (i % nt, 0)),
        ],
        out_specs=[pl.BlockSpec((tm, w), row)] * 5 + [pl.BlockSpec((tm, N_BRANCH * d), row)],
        out_shape=[out_w] * 5 + [jax.ShapeDtypeStruct((t, N_BRANCH * d), BF16)],
        compiler_params=_params("parallel"),
        name="in_proj",
    )(xt, g_mix.reshape(1, d), wa, wg, e64,
      jnp.tile(g_q, heads).reshape(1, w), jnp.tile(g_k, heads).reshape(1, w), g_cq.reshape(1, -1), cos, sin)


def _s5_matrices(lam_re, lam_im, log_dt, b_re, b_im, c_re, c_im, n_chunks):
    hp = lax.Precision.HIGHEST
    c = S5_CHUNK
    dt = jnp.exp(log_dt)[:, None]
    mag = jnp.exp(lam_re * dt)
    ar = mag * jnp.cos(lam_im * dt)
    ai = mag * jnp.sin(lam_im * dt)
    nr = ar - 1.0
    den = lam_re * lam_re + lam_im * lam_im
    cr = (nr * lam_re + ai * lam_im) / den
    ci = (ai * lam_re - nr * lam_im) / den
    bbr = cr[..., None] * b_re - ci[..., None] * b_im
    bbi = cr[..., None] * b_im + ci[..., None] * b_re

    def power(n):
        nf = n.astype(F32)[None, :, None]
        m = jnp.exp((lam_re * dt)[:, None, :] * nf)
        th = (lam_im * dt)[:, None, :] * nf
        return m * jnp.cos(th), m * jnp.sin(th)

    pr, pi = power(jnp.arange(c + 1))
    kbr = pr[..., None] * bbr[:, None] - pi[..., None] * bbi[:, None]
    kbi = pr[..., None] * bbi[:, None] + pi[..., None] * bbr[:, None]
    kk = (jnp.einsum('ghp,gtpc->gthc', c_re, kbr, precision=hp)
          - jnp.einsum('ghp,gtpc->gthc', c_im, kbi, precision=hp))
    tq = jnp.arange(c)
    lag = tq[None, :] - tq[:, None]
    toep = kk[:, jnp.clip(lag, 0, c)]
    toep = jnp.where((lag >= 0)[None, :, :, None, None], toep, 0.0)
    g = toep.shape[0]
    toep = toep.transpose(0, 1, 4, 2, 3).reshape(g, c * SSM_GROUP, c * SSM_GROUP)
    rev = c - 1 - tq
    w_in = jnp.concatenate([kbr[:, rev].transpose(0, 1, 3, 2), kbi[:, rev].transpose(0, 1, 3, 2)], axis=-1)
    w_in = w_in.reshape(g, c * SSM_GROUP, 2 * SSM_STATE)
    prn, pin = pr[:, 1:], pi[:, 1:]
    wo_r = c_re[:, None] * prn[:, :, None, :] - c_im[:, None] * pin[:, :, None, :]
    wo_i = -c_re[:, None] * pin[:, :, None, :] - c_im[:, None] * prn[:, :, None, :]
    w_out = jnp.concatenate([wo_r, wo_i], axis=-1).transpose(0, 3, 1, 2).reshape(g, 2 * SSM_STATE, c * SSM_GROUP)
    n_steps = max(1, int(math.ceil(math.log2(n_chunks))))
    qr, qi = power(c * (2 ** jnp.arange(n_steps)))
    pa = jnp.concatenate([qr, qr], axis=-1)
    pb = jnp.concatenate([-qi, qi], axis=-1)
    return toep.astype(BF16), w_in.astype(BF16), w_out.astype(BF16), pa, pb


def _s5_kernel(u_ref, toep_ref, win_ref, wout_ref, pa_ref, pb_ref, y_ref):
    u = u_ref[0, 0]
    s = _dot(u, win_ref[0])
    nc = s.shape[0]
    row = lax.broadcasted_iota(jnp.int32, s.shape, 0)
    n_steps = pa_ref.shape[1]
    for k in range(n_steps):
        sh = 1 << k
        if sh >= nc:
            break
        prev = jnp.where(row >= sh, pltpu.roll(s, sh, 0), 0.0)
        s = s + pa_ref[0, k:k + 1, :] * prev + pb_ref[0, k:k + 1, :] * pltpu.roll(prev, SSM_STATE, 1)
    s_start = jnp.where(row >= 1, pltpu.roll(s, 1, 0), 0.0)
    y = _dot(u, toep_ref[0]) + _dot(s_start.astype(BF16), wout_ref[0])
    y_ref[0, 0] = y.astype(BF16)


def _s5(u, mats, bsz, seq_len):
    toep, w_in, w_out, pa, pb = mats
    g = toep.shape[0]
    nc = seq_len // S5_CHUNK
    cw = S5_CHUNK * SSM_GROUP
    ug = u.reshape(bsz, nc, S5_CHUNK, g, SSM_GROUP).transpose(0, 3, 1, 2, 4).reshape(bsz, g, nc, cw)
    per_g = lambda b, j: (j, 0, 0)
    y = pl.pallas_call(
        _s5_kernel,
        grid=(bsz, g),
        in_specs=[
            pl.BlockSpec((1, 1, nc, cw), lambda b, j: (b, j, 0, 0)),
            pl.BlockSpec((1, cw, cw), per_g),
            pl.BlockSpec((1, cw, 2 * SSM_STATE), per_g),
            pl.BlockSpec((1, 2 * SSM_STATE, cw), per_g),
            pl.BlockSpec((1,) + pa.shape[1:], per_g),
            pl.BlockSpec((1,) + pb.shape[1:], per_g),
        ],
        out_specs=pl.BlockSpec((1, 1, nc, cw), lambda b, j: (b, j, 0, 0)),
        out_shape=jax.ShapeDtypeStruct((bsz, g, nc, cw), BF16),
        compiler_params=_params("parallel", "parallel"),
        name="s5_scan",
    )(ug, toep, w_in, w_out, pa, pb)
    return y.reshape(bsz, g, nc, S5_CHUNK, SSM_GROUP).transpose(0, 2, 3, 1, 4).reshape(bsz * seq_len, g * SSM_GROUP)


def _moba_select_kernel(q_ref, k_ref, tri_ref, sel_ref, cnt_out_ref, km_ref, cnt_ref):
    h = pl.program_id(1)
    t = pl.program_id(2)
    ts = q_ref.shape[0]
    nbk = MOBA_MAX_BLOCKS

    @pl.when(t == 0)
    def _():
        kk = k_ref[...].astype(F32)
        nb = kk.shape[0] // MOBA_BLOCK
        km = jnp.sum(kk.reshape(nb, MOBA_BLOCK, LANES), axis=1) * (1.0 / MOBA_BLOCK)
        if nb < nbk:
            km = jnp.concatenate([km, jnp.zeros((nbk - nb, LANES), F32)], axis=0)
        lane = lax.broadcasted_iota(jnp.int32, (nbk, LANES), 1)
        km_ref[...] = jnp.where((lane // MOBA_HEAD_DIM) == (h % 2), km, 0.0).astype(BF16)
        cnt_ref[...] = jnp.zeros(cnt_ref.shape, F32)

    gate = _dot_nt(km_ref[...], q_ref[...])
    blk = lax.broadcasted_iota(jnp.int32, gate.shape, 0)
    qblk = (t * ts + lax.broadcasted_iota(jnp.int32, gate.shape, 1)) // MOBA_BLOCK
    g = jnp.where(blk < qblk, gate, -jnp.inf)
    selected = jnp.zeros(gate.shape, jnp.bool_)
    picks = []
    for _ in range(MOBA_TOPK):
        mx = jnp.max(g, axis=0, keepdims=True)
        idx = jnp.min(jnp.where(g == mx, blk, nbk), axis=0, keepdims=True)
        hit = blk == idx
        ok = (idx[0:1] < qblk[0:1]) & (mx > -jnp.inf)
        picks.append((hit, idx, ok))
        selected = selected | (hit & ok)
        g = jnp.where(hit, -jnp.inf, g)
    sel01 = jnp.where(selected, 1.0, 0.0).astype(BF16)
    prefix = _dot(sel01, tri_ref[...]) + jnp.tile(cnt_ref[...], (1, ts // LANES))
    rows = [jnp.where(ok, idx, -1) for (_, idx, ok) in picks]
    rows += [jnp.sum(jnp.where(hit, prefix, 0.0), axis=0, keepdims=True).astype(jnp.int32) for (hit, _, _) in picks]
    rows += [jnp.zeros((1, ts), jnp.int32)] * (8 - 2 * MOBA_TOPK)
    sel_ref[0, 0] = jnp.concatenate(rows, axis=0)
    cnt_ref[...] = cnt_ref[...] + _dot(sel01, jnp.ones((ts, LANES), BF16))
    cnt_out_ref[0, 0] = cnt_ref[...]


def _moba_select(q2, k2, bsz, seq_len, ts=512):
    ts = min(ts, seq_len)
    heads = q2.shape[1] // MOBA_HEAD_DIM
    nt = seq_len // ts
    tri = (jnp.arange(ts)[:, None] < jnp.arange(ts)[None, :]).astype(BF16)
    return pl.pallas_call(
        _moba_select_kernel,
        grid=(bsz, heads, nt),
        in_specs=[
            pl.BlockSpec((ts, LANES), lambda b, h, t: (b * nt + t, h // 2)),
            pl.BlockSpec((seq_len, LANES), lambda b, h, t: (b, h // 2)),
            pl.BlockSpec((ts, ts), lambda b, h, t: (0, 0)),
        ],
        out_specs=[pl.BlockSpec((1, 1, 8, ts), lambda b, h, t: (b, h, 0, t)),
                   pl.BlockSpec((1, 1, MOBA_MAX_BLOCKS, LANES), lambda b, h, t: (b, h, 0, 0))],
        out_shape=[jax.ShapeDtypeStruct((bsz, heads, 8, seq_len), jnp.int32),
                   jax.ShapeDtypeStruct((bsz, heads, MOBA_MAX_BLOCKS, LANES), F32)],
        scratch_shapes=[pltpu.VMEM((MOBA_MAX_BLOCKS, LANES), BF16), pltpu.VMEM((MOBA_MAX_BLOCKS, LANES), F32)],
        compiler_params=_params("parallel", "arbitrary", "arbitrary"),
        name="moba_select",
    )(q2, k2, tri)


def _softmax_block(q, kb, vb, causal):
    s = _dot_nt(q, kb)
    if causal:
        r = lax.broadcasted_iota(jnp.int32, s.shape, 0) % MOBA_BLOCK
        c = lax.broadcasted_iota(jnp.int32, s.shape, 1)
        s = jnp.where(c <= r, s, NEG_BIG)
    m = jnp.max(s, axis=1, keepdims=True)
    p = jnp.exp(s - m).astype(BF16)
    pv = _dot(p, jnp.concatenate([vb, jnp.ones(vb.shape, BF16)], axis=1))
    l = pv[:, LANES:LANES + 1]
    return pv[:, :LANES] / l, m + jnp.log(l)


def _moba_diag_kernel(q_ref, k_ref, v_ref, o_ref, lse_ref):
    q = q_ref[...]
    lane = lax.broadcasted_iota(jnp.int32, q.shape, 1)
    is_a = lane < MOBA_HEAD_DIM
    zero = jnp.zeros_like(q)
    qs = jnp.concatenate([jnp.where(is_a, q, zero), jnp.where(is_a, zero, q)], axis=0)
    o, lse = _softmax_block(qs, k_ref[...], v_ref[...], True)
    n = q.shape[0]
    o_ref[...] = jnp.where(is_a, o[:n], o[n:])
    lse_ref[...] = jnp.where(is_a, lse[:n], lse[n:])


def _moba_diag(q2, k2, v2, bsz, seq_len):
    npair = q2.shape[1] // LANES
    nb = seq_len // MOBA_BLOCK
    spec = pl.BlockSpec((MOBA_BLOCK, LANES), lambda b, p, i: (b * nb + i, p))
    out = jax.ShapeDtypeStruct(q2.shape, F32)
    return pl.pallas_call(
        _moba_diag_kernel,
        grid=(bsz, npair, nb),
        in_specs=[spec, spec, spec],
        out_specs=[spec, spec],
        out_shape=[out, out],
        compiler_params=_params("parallel", "parallel", "parallel"),
        name="moba_diag",
    )(q2, k2, v2)


MOBA_TILE = 256
MOBA_TILES_PER_STEP = 4


def _moba_grouped_kernel(tile_g_ref, tile_real_ref, qd_ref, *refs):
    n = MOBA_TILES_PER_STEP
    k_refs, v_refs, o_ref = refs[:n], refs[n:2 * n], refs[2 * n]
    i = pl.program_id(0)
    lane = lax.broadcasted_iota(jnp.int32, (MOBA_TILE, LANES), 1)
    for u in range(n):
        tile = i * n + u
        rows = pl.ds(u * MOBA_TILE, MOBA_TILE)
        head_half = (tile_g_ref[tile] // MOBA_MAX_BLOCKS) % 2

        @pl.when(tile_real_ref[tile] > 0)
        def _():
            o, lse = _softmax_block(qd_ref[rows, :], k_refs[u][...], v_refs[u][...], False)
            o_ref[rows, :] = jnp.where((lane // MOBA_HEAD_DIM) == head_half, o, lse)

        @pl.when(tile_real_ref[tile] == 0)
        def _():
            o_ref[rows, :] = jnp.full((MOBA_TILE, LANES), NEG_BIG, F32)


def _moba_grouped(qd, k2, v2, tile_g, tile_real, seq_len):
    n = MOBA_TILES_PER_STEP
    n_tiles = qd.shape[0] // MOBA_TILE
    nb = seq_len // MOBA_BLOCK
    heads = k2.shape[1] // MOBA_HEAD_DIM

    def kv_spec(u):
        def index(i, tg, tr):
            g = tg[i * n + u]
            b = g // (heads * MOBA_MAX_BLOCKS)
            h = (g // MOBA_MAX_BLOCKS) % heads
            return (b * nb + jnp.minimum(g % MOBA_MAX_BLOCKS, nb - 1), h // 2)
        return pl.BlockSpec((MOBA_BLOCK, LANES), index)

    grid_spec = pltpu.PrefetchScalarGridSpec(
        num_scalar_prefetch=2,
        grid=(n_tiles // n,),
        in_specs=[pl.BlockSpec((n * MOBA_TILE, LANES), lambda i, tg, tr: (i, 0))]
        + [kv_spec(u) for u in range(n)] + [kv_spec(u) for u in range(n)],
        out_specs=pl.BlockSpec((n * MOBA_TILE, LANES), lambda i, tg, tr: (i, 0)),
    )
    return pl.pallas_call(
        _moba_grouped_kernel,
        grid_spec=grid_spec,
        out_shape=jax.ShapeDtypeStruct(qd.shape, F32),
        compiler_params=_params("arbitrary"),
        name="moba_grouped",
    )(tile_g, tile_real, qd, *([k2] * n), *([v2] * n))


def _moba_combine_kernel(od_ref, lsed_ref, *refs):
    g_refs, o_ref = refs[:-1], refs[-1]
    lane = lax.broadcasted_iota(jnp.int32, od_ref.shape, 1)
    is_a = lane < MOBA_HEAD_DIM
    parts = [(od_ref[...], lsed_ref[...])]
    for s in range(MOBA_TOPK):
        xa = g_refs[s][0, 0, 0]
        xb = g_refs[MOBA_TOPK + s][0, 0, 0]
        o = jnp.where(is_a, xa, xb)
        lse = pltpu.roll(jnp.where(is_a, xb, xa), MOBA_HEAD_DIM, 1)
        parts.append((o, lse))
    m = parts[0][1]
    for _, lse in parts[1:]:
        m = jnp.maximum(m, lse)
    num = jnp.zeros(od_ref.shape, F32)
    den = jnp.zeros(od_ref.shape, F32)
    for o, lse in parts:
        w = jnp.exp(lse - m)
        num = num + w * o
        den = den + w
    o_ref[...] = (num / den).astype(BF16)


def _moba_combine(od, lsed, gath, bsz, seq_len, tm=256):
    npair = od.shape[1] // LANES
    nt = seq_len // tm
    spec = pl.BlockSpec((tm, LANES), lambda b, p, i: (b * nt + i, p))

    def g_spec(e, s):
        return pl.BlockSpec((1, 1, 1, tm, LANES), lambda b, p, i: (b, 2 * p + e, s, i, 0))

    return pl.pallas_call(
        _moba_combine_kernel,
        grid=(bsz, npair, nt),
        in_specs=[spec, spec] + [g_spec(e, s) for e in range(2) for s in range(MOBA_TOPK)],
        out_specs=spec,
        out_shape=jax.ShapeDtypeStruct(od.shape, BF16),
        compiler_params=_params("parallel", "parallel", "parallel"),
        name="moba_combine",
    )(od, lsed, *([gath] * (2 * MOBA_TOPK)))


def _scatter_rows(rows, dest, n_out):
    return jnp.zeros((n_out, rows.shape[1]), rows.dtype).at[dest].set(rows, mode='drop')


def _gather_rows(table, idx):
    return jnp.take(table, idx, axis=0)


def _moba(q2, k2, v2, bsz, seq_len):
    heads = q2.shape[1] // MOBA_HEAD_DIM
    nbk = MOBA_MAX_BLOCKS
    sel, cnt = _moba_select(q2, k2, bsz, seq_len)
    od, lsed = _moba_diag(q2, k2, v2, bsz, seq_len)
    counts = cnt[..., 0].astype(jnp.int32).reshape(bsz * heads * nbk)
    pcounts = ((counts + MOBA_TILE - 1) // MOBA_TILE) * MOBA_TILE
    pends = jnp.cumsum(pcounts)
    pstarts = (pends - pcounts).reshape(bsz, heads, 1, 1, nbk)
    n_items = bsz * heads * seq_len * MOBA_TOPK
    step_rows = MOBA_TILE * MOBA_TILES_PER_STEP
    n_rows = -(-(n_items + bsz * heads * nbk * MOBA_TILE) // step_rows) * step_rows
    n_null = step_rows
    idx = sel[:, :, 0:MOBA_TOPK, :]
    rank = sel[:, :, MOBA_TOPK:2 * MOBA_TOPK, :]
    start = jnp.sum(jnp.where(idx[..., None] == jnp.arange(nbk), pstarts, 0), axis=-1)
    null_row = n_rows + jnp.arange(seq_len, dtype=jnp.int32) % n_null
    dest = jnp.where(idx >= 0, start + rank, null_row)
    n_tiles = (n_rows + n_null) // MOBA_TILE
    tile_start = jnp.arange(n_tiles, dtype=jnp.int32) * MOBA_TILE
    tile_g = jnp.minimum(jnp.searchsorted(pends, tile_start, side='right'), bsz * heads * nbk - 1).astype(jnp.int32)
    tile_real = (tile_start < pends[-1]).astype(jnp.int32)
    lane = jnp.arange(LANES) // MOBA_HEAD_DIM
    qh = q2.reshape(bsz, seq_len, heads // 2, 1, LANES)
    qh = jnp.where(lane == jnp.arange(2)[:, None], qh, 0).reshape(bsz, seq_len, heads, LANES).transpose(0, 2, 1, 3)
    src = jnp.broadcast_to(qh[:, :, None], (bsz, heads, MOBA_TOPK, seq_len, LANES)).reshape(-1, LANES)
    flat_dest = jnp.where(idx >= 0, dest, n_rows + n_null).reshape(-1)
    qd = _scatter_rows(src, flat_dest, n_rows + n_null)
    part = _moba_grouped(qd, k2, v2, tile_g, tile_real, seq_len)
    gath = _gather_rows(part, dest.reshape(-1)).reshape(bsz, heads, MOBA_TOPK, seq_len, LANES)
    return _moba_combine(od, lsed, gath, bsz, seq_len)


def _mem_kv_kernel(mem_ref, gmem_ref, w_ref, gck_ref, k_ref, v_ref):
    xf = mem_ref[0]
    ms = jnp.mean(xf * xf, axis=-1, keepdims=True)
    h = (xf * lax.rsqrt(ms + EPS) * gmem_ref[...]).astype(BF16)
    kv = _dot(h, w_ref[...])
    w = k_ref.shape[2]
    hd = w // X_HEADS
    for c in range(X_HEADS):
        chunk = kv[:, c * hd:(c + 1) * hd]
        cms = jnp.mean(chunk * chunk, axis=-1, keepdims=True)
        k_ref[0, :, c * hd:(c + 1) * hd] = (chunk * lax.rsqrt(cms + EPS) * gck_ref[...]).astype(BF16)
    v_ref[0] = kv[:, w:].astype(BF16)


def _mem_kv(mem, g_mem, w_kv_mem, g_ck):
    bsz, m, d = mem.shape
    w = w_kv_mem.shape[1] // 2
    const = lambda b: (0, 0)
    out = jax.ShapeDtypeStruct((bsz, m, w), BF16)
    return pl.pallas_call(
        _mem_kv_kernel,
        grid=(bsz,),
        in_specs=[pl.BlockSpec((1, m, d), lambda b: (b, 0, 0)), pl.BlockSpec((1, d), const),
                  pl.BlockSpec((d, 2 * w), const), pl.BlockSpec((1, w // X_HEADS), const)],
        out_specs=[pl.BlockSpec((1, m, w), lambda b: (b, 0, 0))] * 2,
        out_shape=[out, out],
        compiler_params=_params("parallel"),
        name="mem_kv",
    )(mem, g_mem.reshape(1, d), w_kv_mem.astype(BF16), g_ck.reshape(1, -1))


def _merge_kernel(x_ref, ys_ref, u_ref, dskip_ref, om_ref, xq_ref, kc_ref, vc_ref, g_ref,
                  wglu_ref, wmo_ref, wco_ref, wout_ref, gffn_ref, wr_ref, br_ref,
                  x1_ref, h2_ref, logit_ref):
    d = x_ref.shape[1]
    y = ys_ref[...].astype(F32) + dskip_ref[...] * u_ref[...].astype(F32)
    ge = 0.5 * y * (1.0 + jnp.tanh(math.sqrt(2.0 / math.pi) * (y + 0.044715 * (y * y * y))))
    z = _dot(ge.astype(BF16), wglu_ref[...])
    merged = g_ref[:, 0:d].astype(F32) * (z[:, :d] * _sigmoid(z[:, d:]))
    merged = merged + g_ref[:, d:2 * d].astype(F32) * _dot(om_ref[...], wmo_ref[...])
    w = xq_ref.shape[1]
    hd = w // X_HEADS
    heads = []
    for c in range(X_HEADS):
        s = _dot_nt(xq_ref[:, c * hd:(c + 1) * hd], kc_ref[0, :, c * hd:(c + 1) * hd]) * (hd ** -0.5)
        p = jnp.exp(s - jnp.max(s, axis=1, keepdims=True))
        p = p / jnp.sum(p, axis=1, keepdims=True)
        heads.append(_dot(p.astype(BF16), vc_ref[0, :, c * hd:(c + 1) * hd]))
    oc = jnp.concatenate(heads, axis=1).astype(BF16)
    merged = merged + g_ref[:, 2 * d:3 * d].astype(F32) * _dot(oc, wco_ref[...])
    x1 = x_ref[...] + _dot(merged.astype(BF16), wout_ref[...])
    x1_ref[...] = x1
    ms = jnp.mean(x1 * x1, axis=-1, keepdims=True)
    h2 = (x1 * lax.rsqrt(ms + EPS) * gffn_ref[...]).astype(BF16)
    h2_ref[...] = h2
    logit_ref[...] = _dot(h2, wr_ref[...]) + br_ref[...]


def _merge(xt, ys, u, d_skip, om, xqn, kc, vc, gates, w_glu, w_mo, w_co, w_out, g_ffn, w_router, b_router,
           seq_len, tm=256):
    t, d = xt.shape
    w = d // 2
    m = kc.shape[1]
    ne = w_router.shape[1]
    nt = seq_len // tm
    row = lambda i: (i, 0)
    const = lambda i: (0, 0)
    per_b = lambda i: (i // nt, 0, 0)
    return pl.pallas_call(
        _merge_kernel,
        grid=(t // tm,),
        in_specs=[
            pl.BlockSpec((tm, d), row), pl.BlockSpec((tm, w), row), pl.BlockSpec((tm, w), row),
            pl.BlockSpec((1, w), const), pl.BlockSpec((tm, w), row), pl.BlockSpec((tm, w), row),
            pl.BlockSpec((1, m, w), per_b), pl.BlockSpec((1, m, w), per_b),
            pl.BlockSpec((tm, N_BRANCH * d), row),
            pl.BlockSpec((w, 2 * d), const), pl.BlockSpec((w, d), const), pl.BlockSpec((w, d), const),
            pl.BlockSpec((d, d), const), pl.BlockSpec((1, d), const),
            pl.BlockSpec((d, ne), const), pl.BlockSpec((1, ne), const),
        ],
        out_specs=[pl.BlockSpec((tm, d), row), pl.BlockSpec((tm, d), row), pl.BlockSpec((tm, ne), row)],
        out_shape=[jax.ShapeDtypeStruct((t, d), F32), jax.ShapeDtypeStruct((t, d), BF16),
                   jax.ShapeDtypeStruct((t, ne), F32)],
        compiler_params=_params("parallel"),
        name="merge",
    )(xt, ys, u, d_skip.reshape(1, w), om, xqn, kc, vc, gates,
      w_glu.astype(BF16), w_mo.astype(BF16), w_co.astype(BF16), w_out.astype(BF16),
      g_ffn.reshape(1, d), w_router.astype(BF16), b_router.reshape(1, ne))


def _moe_kernel(blk_e_ref, blk_used_ref, xs_ref, wgu_ref, bgu_ref, wd_ref, bd_ref, y_ref, wgu_bf, wd_bf):
    i = pl.program_id(0)
    prev = blk_e_ref[jnp.maximum(i - 1, 0)]

    @pl.when((i == 0) | (blk_e_ref[i] != prev))
    def _():
        wgu_bf[...] = wgu_ref[0].astype(BF16)
        wd_bf[...] = wd_ref[0].astype(BF16)

    @pl.when(blk_used_ref[i] > 0)
    def _():
        de = wd_bf.shape[0]
        gu = _dot(xs_ref[...], wgu_bf[...]) + bgu_ref[0]
        gate = jnp.minimum(gu[:, :de], SWIGLU_LIMIT)
        up = jnp.clip(gu[:, de:], -SWIGLU_LIMIT, SWIGLU_LIMIT)
        act = gate * _sigmoid(SWIGLU_ALPHA * gate) * (up + 1.0)
        y_ref[...] = _dot(act.astype(BF16), wd_bf[...]) + bd_ref[0]

    @pl.when(blk_used_ref[i] == 0)
    def _():
        y_ref[...] = jnp.zeros(y_ref.shape, y_ref.dtype)


def _moe_experts(xs, blk_e, blk_used, w_gu, b_gu, w_down, b_down):
    p, d = xs.shape
    ne, _, de2 = w_gu.shape
    de = de2 // 2
    nblk = p // EXPERT_ROWS
    grid_spec = pltpu.PrefetchScalarGridSpec(
        num_scalar_prefetch=2,
        grid=(nblk,),
        in_specs=[
            pl.BlockSpec((EXPERT_ROWS, d), lambda i, e, n: (i, 0)),
            pl.BlockSpec((1, d, de2), lambda i, e, n: (e[i], 0, 0)),
            pl.BlockSpec((1, 1, de2), lambda i, e, n: (e[i], 0, 0)),
            pl.BlockSpec((1, de, d), lambda i, e, n: (e[i], 0, 0)),
            pl.BlockSpec((1, 1, d), lambda i, e, n: (e[i], 0, 0)),
        ],
        out_specs=pl.BlockSpec((EXPERT_ROWS, d), lambda i, e, n: (i, 0)),
        scratch_shapes=[pltpu.VMEM((d, de2), BF16), pltpu.VMEM((de, d), BF16)],
    )
    return pl.pallas_call(
        _moe_kernel,
        grid_spec=grid_spec,
        out_shape=jax.ShapeDtypeStruct((p, d), F32),
        compiler_params=_params("arbitrary"),
        name="moe_experts",
    )(blk_e, blk_used, xs, w_gu, b_gu.reshape(ne, 1, de2), w_down, b_down.reshape(ne, 1, d))


def _moe_ffn(x1, h2, logits, w_gu, b_gu, w_down, b_down):
    t, d = x1.shape
    tk = t * TOPK_EXPERTS
    top_v, top_e = lax.top_k(logits, TOPK_EXPERTS)
    weights = jax.nn.softmax(top_v, axis=-1)
    e_flat = top_e.reshape(tk)
    onehot = (e_flat[:, None] == jnp.arange(N_EXPERTS, dtype=e_flat.dtype)[None, :]).astype(jnp.int32)
    csum = jnp.cumsum(onehot, axis=0)
    rank = jnp.take_along_axis(csum, e_flat[:, None], axis=1)[:, 0] - 1
    counts = csum[-1]
    pcounts = ((counts + EXPERT_ROWS - 1) // EXPERT_ROWS) * EXPERT_ROWS
    pends = jnp.cumsum(pcounts)
    pstarts = pends - pcounts
    dest = pstarts[e_flat] + rank
    nblk = -(-tk // EXPERT_ROWS) + N_EXPERTS
    p = nblk * EXPERT_ROWS
    tok = jnp.arange(tk, dtype=jnp.int32) // TOPK_EXPERTS
    slot_tok = jnp.zeros((p,), jnp.int32).at[dest].set(tok)
    blk_start = jnp.arange(nblk, dtype=jnp.int32) * EXPERT_ROWS
    blk_e = jnp.minimum(jnp.searchsorted(pends, blk_start, side='right'), N_EXPERTS - 1).astype(jnp.int32)
    blk_used = (blk_start < pends[-1]).astype(jnp.int32)
    xs = h2[slot_tok]
    ys = _moe_experts(xs, blk_e, blk_used, w_gu, b_gu, w_down, b_down)
    picked = ys[dest].reshape(t, TOPK_EXPERTS, d)
    return x1 + jnp.sum(picked * weights[:, :, None], axis=1)


def kernel(x, mem, g_mix, w_in, lam_re, lam_im, log_dt, b_re, b_im, c_re, c_im, d_skip, w_glu, g_q, g_k, w_moba_out, g_mem, w_kv_mem, g_cq, g_ck, w_cross_out, w_out, g_ffn, w_router, b_router, w_gu, b_gu, w_down, b_down):
    bsz, seq_len, d = x.shape
    xt = x.reshape(bsz * seq_len, d)
    for l in range(g_mix.shape[0]):
        u, q2, k2, v2, xqn, gates = _in_proj(xt, g_mix[l], w_in[l], g_q[l], g_k[l], g_cq[l], seq_len)
        mats = _s5_matrices(lam_re[l], lam_im[l], log_dt[l], b_re[l], b_im[l], c_re[l], c_im[l],
                            seq_len // S5_CHUNK)
        ys = _s5(u, mats, bsz, seq_len)
        om = _moba(q2, k2, v2, bsz, seq_len)
        kc, vc = _mem_kv(mem, g_mem[l], w_kv_mem[l], g_ck[l])
        x1, h2, logits = _merge(xt, ys, u, d_skip[l], om, xqn, kc, vc, gates, w_glu[l], w_moba_out[l],
                                w_cross_out[l], w_out[l], g_ffn[l], w_router[l], b_router[l], seq_len)
        xt = _moe_ffn(x1, h2, logits, w_gu[l], b_gu[l], w_down[l], b_down[l])
    return xt.reshape(bsz, seq_len, d)
```

```python
import functools
import math

import jax
import jax.numpy as jnp
from jax import lax
from jax.experimental import pallas as pl
from jax.experimental.pallas import tpu as pltpu

F32 = jnp.float32
BF16 = jnp.bfloat16

EPS = 1e-6
N_BRANCH = 3
SSM_GROUP = 16
SSM_STATE = 64
S5_CHUNK = 16
MOBA_HEAD_DIM = 64
MOBA_BLOCK = 256
MOBA_TOPK = 3
MOBA_MAX_BLOCKS = 64
ROPE_THETA = 10000.0
X_HEADS = 4
N_EXPERTS = 32
TOPK_EXPERTS = 4
SWIGLU_LIMIT = 7.0
SWIGLU_ALPHA = 1.702
EXPERT_ROWS = 512
NEG_BIG = -1e30
LANES = 128
VMEM_LIMIT_BYTES = 56 * 1024 * 1024


def _params(*sem):
    return pltpu.CompilerParams(dimension_semantics=sem, vmem_limit_bytes=VMEM_LIMIT_BYTES)


def _sigmoid(x):
    return 1.0 / (1.0 + jnp.exp(-x))


def _dot(a, b):
    return jnp.dot(a, b, preferred_element_type=F32)


def _dot_nt(a, b):
    return lax.dot_general(a, b, (((1,), (1,)), ((), ())), preferred_element_type=F32)


def _inproj_kernel(x_ref, gmix_ref, wa_ref, wg_ref, e64_ref, gq_ref, gk_ref, gcq_ref, cos_ref, sin_ref,
                   u_ref, q_ref, k_ref, v_ref, xq_ref, g_ref):
    xf = x_ref[...]
    ms = jnp.mean(xf * xf, axis=-1, keepdims=True)
    h = (xf * lax.rsqrt(ms + EPS) * gmix_ref[...]).astype(BF16)
    a = _dot(h, wa_ref[...])
    w = u_ref.shape[1]
    u_ref[...] = a[:, :w].astype(BF16)
    v_ref[...] = a[:, 3 * w:4 * w].astype(BF16)

    cos = jnp.tile(cos_ref[...], (1, w // LANES))
    sin = jnp.tile(sin_ref[...], (1, w // LANES))
    lane = lax.broadcasted_iota(jnp.int32, (xf.shape[0], w), 1)
    first_half = (lane % MOBA_HEAD_DIM) < (MOBA_HEAD_DIM // 2)

    def qk_norm_rope(raw, g):
        ss = _dot((raw * raw).astype(BF16), e64_ref[...])
        n = raw * lax.rsqrt(ss * (1.0 / MOBA_HEAD_DIM) + EPS) * g
        rot = jnp.where(first_half,
                        pltpu.roll(n, w - MOBA_HEAD_DIM // 2, 1),
                        pltpu.roll(n, MOBA_HEAD_DIM // 2, 1))
        return n * cos + rot * sin

    q_ref[...] = (qk_norm_rope(a[:, w:2 * w], gq_ref[...]) * (MOBA_HEAD_DIM ** -0.5)).astype(BF16)
    k_ref[...] = qk_norm_rope(a[:, 2 * w:3 * w], gk_ref[...]).astype(BF16)

    xq = a[:, 4 * w:5 * w]
    hd = w // X_HEADS
    for c in range(X_HEADS):
        chunk = xq[:, c * hd:(c + 1) * hd]
        cms = jnp.mean(chunk * chunk, axis=-1, keepdims=True)
        xq_ref[:, c * hd:(c + 1) * hd] = (chunk * lax.rsqrt(cms + EPS) * gcq_ref[...]).astype(BF16)

    d = xf.shape[1]
    for c in range(N_BRANCH):
        z = _dot(h, wg_ref[:, c * d:(c + 1) * d])
        g_ref[:, c * d:(c + 1) * d] = _sigmoid(z).astype(BF16)


def _in_proj(xt, g_mix, w_in, g_q, g_k, g_cq, seq_len, tm=256):
    t, d = xt.shape
    w = d // 2
    wa = w_in[:, :5 * w].astype(BF16)
    wg = w_in[:, 5 * w:].astype(BF16)
    heads = w // MOBA_HEAD_DIM
    e64 = jnp.kron(jnp.eye(heads, dtype=F32), jnp.ones((MOBA_HEAD_DIM, MOBA_HEAD_DIM), F32)).astype(BF16)
    half = MOBA_HEAD_DIM // 2
    inv = ROPE_THETA ** (-jnp.arange(half, dtype=F32) / half)
    ang = jnp.arange(seq_len, dtype=F32)[:, None] * inv[None, :]
    cos = jnp.tile(jnp.cos(ang), (1, LANES // half))
    sin = jnp.tile(jnp.concatenate([-jnp.sin(ang), jnp.sin(ang)], axis=1), (1, LANES // MOBA_HEAD_DIM))
    nt = seq_len // tm
    row = lambda i: (i, 0)
    const = lambda i: (0, 0)
    out_w = jax.ShapeDtypeStruct((t, w), BF16)
    return pl.pallas_call(
        _inproj_kernel,
        grid=(t // tm,),
        in_specs=[
            pl.BlockSpec((tm, d), row),
            pl.BlockSpec((1, d), const),
            pl.BlockSpec((d, 5 * w), const),
            pl.BlockSpec((d, N_BRANCH * d), const),
            pl.BlockSpec((w, w), const),
            pl.BlockSpec((1, w), const),
            pl.BlockSpec((1, w), const),
            pl.BlockSpec((1, w // X_HEADS), const),
            pl.BlockSpec((tm, LANES), lambda i: (i % nt, 0)),
            pl.BlockSpec((tm, LANES), lambda i: (i % nt, 0)),
        ],
        out_specs=[pl.BlockSpec((tm, w), row)] * 5 + [pl.BlockSpec((tm, N_BRANCH * d), row)],
        out_shape=[out_w] * 5 + [jax.ShapeDtypeStruct((t, N_BRANCH * d), BF16)],
        compiler_params=_params("parallel"),
        name="in_proj",
    )(xt, g_mix.reshape(1, d), wa, wg, e64,
      jnp.tile(g_q, heads).reshape(1, w), jnp.tile(g_k, heads).reshape(1, w), g_cq.reshape(1, -1), cos, sin)


def _s5_matrices(lam_re, lam_im, log_dt, b_re, b_im, c_re, c_im, n_chunks):
    hp = lax.Precision.HIGHEST
    c = S5_CHUNK
    dt = jnp.exp(log_dt)[:, None]
    mag = jnp.exp(lam_re * dt)
    ar = mag * jnp.cos(lam_im * dt)
    ai = mag * jnp.sin(lam_im * dt)
    nr = ar - 1.0
    den = lam_re * lam_re + lam_im * lam_im
    cr = (nr * lam_re + ai * lam_im) / den
    ci = (ai * lam_re - nr * lam_im) / den
    bbr = cr[..., None] * b_re - ci[..., None] * b_im
    bbi = cr[..., None] * b_im + ci[..., None] * b_re

    def power(n):
        nf = n.astype(F32)[None, :, None]
        m = jnp.exp((lam_re * dt)[:, None, :] * nf)
        th = (lam_im * dt)[:, None, :] * nf
        return m * jnp.cos(th), m * jnp.sin(th)

    pr, pi = power(jnp.arange(c + 1))
    kbr = pr[..., None] * bbr[:, None] - pi[..., None] * bbi[:, None]
    kbi = pr[..., None] * bbi[:, None] + pi[..., None] * bbr[:, None]
    kk = (jnp.einsum('ghp,gtpc->gthc', c_re, kbr, precision=hp)
          - jnp.einsum('ghp,gtpc->gthc', c_im, kbi, precision=hp))
    tq = jnp.arange(c)
    lag = tq[None, :] - tq[:, None]
    toep = kk[:, jnp.clip(lag, 0, c)]
    toep = jnp.where((lag >= 0)[None, :, :, None, None], toep, 0.0)
    g = toep.shape[0]
    toep = toep.transpose(0, 1, 4, 2, 3).reshape(g, c * SSM_GROUP, c * SSM_GROUP)
    rev = c - 1 - tq
    w_in = jnp.concatenate([kbr[:, rev].transpose(0, 1, 3, 2), kbi[:, rev].transpose(0, 1, 3, 2)], axis=-1)
    w_in = w_in.reshape(g, c * SSM_GROUP, 2 * SSM_STATE)
    prn, pin = pr[:, 1:], pi[:, 1:]
    wo_r = c_re[:, None] * prn[:, :, None, :] - c_im[:, None] * pin[:, :, None, :]
    wo_i = -c_re[:, None] * pin[:, :, None, :] - c_im[:, None] * prn[:, :, None, :]
    w_out = jnp.concatenate([wo_r, wo_i], axis=-1).transpose(0, 3, 1, 2).reshape(g, 2 * SSM_STATE, c * SSM_GROUP)
    n_steps = max(1, int(math.ceil(math.log2(n_chunks))))
    qr, qi = power(c * (2 ** jnp.arange(n_steps)))
    pa = jnp.concatenate([qr, qr], axis=-1)
    pb = jnp.concatenate([-qi, qi], axis=-1)
    return toep.astype(BF16), w_in.astype(BF16), w_out.astype(BF16), pa, pb


def _s5_kernel(u_ref, toep_ref, win_ref, wout_ref, pa_ref, pb_ref, y_ref):
    u = u_ref[0, 0]
    s = _dot(u, win_ref[0])
    nc = s.shape[0]
    row = lax.broadcasted_iota(jnp.int32, s.shape, 0)
    n_steps = pa_ref.shape[1]
    for k in range(n_steps):
        sh = 1 << k
        if sh >= nc:
            break
        prev = jnp.where(row >= sh, pltpu.roll(s, sh, 0), 0.0)
        s = s + pa_ref[0, k:k + 1, :] * prev + pb_ref[0, k:k + 1, :] * pltpu.roll(prev, SSM_STATE, 1)
    s_start = jnp.where(row >= 1, pltpu.roll(s, 1, 0), 0.0)
    y = _dot(u, toep_ref[0]) + _dot(s_start.astype(BF16), wout_ref[0])
    y_ref[0, 0] = y.astype(BF16)


def _s5(u, mats, bsz, seq_len):
    toep, w_in, w_out, pa, pb = mats
    g = toep.shape[0]
    nc = seq_len // S5_CHUNK
    cw = S5_CHUNK * SSM_GROUP
    ug = u.reshape(bsz, nc, S5_CHUNK, g, SSM_GROUP).transpose(0, 3, 1, 2, 4).reshape(bsz, g, nc, cw)
    per_g = lambda b, j: (j, 0, 0)
    y = pl.pallas_call(
        _s5_kernel,
        grid=(bsz, g),
        in_specs=[
            pl.BlockSpec((1, 1, nc, cw), lambda b, j: (b, j, 0, 0)),
            pl.BlockSpec((1, cw, cw), per_g),
            pl.BlockSpec((1, cw, 2 * SSM_STATE), per_g),
            pl.BlockSpec((1, 2 * SSM_STATE, cw), per_g),
            pl.BlockSpec((1,) + pa.shape[1:], per_g),
            pl.BlockSpec((1,) + pb.shape[1:], per_g),
        ],
        out_specs=pl.BlockSpec((1, 1, nc, cw), lambda b, j: (b, j, 0, 0)),
        out_shape=jax.ShapeDtypeStruct((bsz, g, nc, cw), BF16),
        compiler_params=_params("parallel", "parallel"),
        name="s5_scan",
    )(ug, toep, w_in, w_out, pa, pb)
    return y.reshape(bsz, g, nc, S5_CHUNK, SSM_GROUP).transpose(0, 2, 3, 1, 4).reshape(bsz * seq_len, g * SSM_GROUP)


def _moba_select_kernel(q_ref, k_ref, tri_ref, sel_ref, cnt_out_ref, km_ref, cnt_ref):
    h = pl.program_id(1)
    t = pl.program_id(2)
    ts = q_ref.shape[0]
    nbk = MOBA_MAX_BLOCKS

    @pl.when(t == 0)
    def _():
        kk = k_ref[...].astype(F32)
        nb = kk.shape[0] // MOBA_BLOCK
        km = jnp.sum(kk.reshape(nb, MOBA_BLOCK, LANES), axis=1) * (1.0 / MOBA_BLOCK)
        if nb < nbk:
            km = jnp.concatenate([km, jnp.zeros((nbk - nb, LANES), F32)], axis=0)
        lane = lax.broadcasted_iota(jnp.int32, (nbk, LANES), 1)
        km_ref[...] = jnp.where((lane // MOBA_HEAD_DIM) == (h % 2), km, 0.0).astype(BF16)
        cnt_ref[...] = jnp.zeros(cnt_ref.shape, F32)

    gate = _dot_nt(km_ref[...], q_ref[...])
    blk = lax.broadcasted_iota(jnp.int32, gate.shape, 0)
    qblk = (t * ts + lax.broadcasted_iota(jnp.int32, gate.shape, 1)) // MOBA_BLOCK
    g = jnp.where(blk < qblk, gate, -jnp.inf)
    selected = jnp.zeros(gate.shape, jnp.bool_)
    picks = []
    for _ in range(MOBA_TOPK):
        mx = jnp.max(g, axis=0, keepdims=True)
        idx = jnp.min(jnp.where(g == mx, blk, nbk), axis=0, keepdims=True)
        hit = blk == idx
        ok = (idx[0:1] < qblk[0:1]) & (mx > -jnp.inf)
        picks.append((hit, idx, ok))
        selected = selected | (hit & ok)
        g = jnp.where(hit, -jnp.inf, g)
    sel01 = jnp.where(selected, 1.0, 0.0).astype(BF16)
    prefix = _dot(sel01, tri_ref[...]) + jnp.tile(cnt_ref[...], (1, ts // LANES))
    rows = [jnp.where(ok, idx, -1) for (_, idx, ok) in picks]
    rows += [jnp.sum(jnp.where(hit, prefix, 0.0), axis=0, keepdims=True).astype(jnp.int32) for (hit, _, _) in picks]
    rows += [jnp.zeros((1, ts), jnp.int32)] * (8 - 2 * MOBA_TOPK)
    sel_ref[0, 0] = jnp.concatenate(rows, axis=0)
    cnt_ref[...] = cnt_ref[...] + _dot(sel01, jnp.ones((ts, LANES), BF16))
    cnt_out_ref[0, 0] = cnt_ref[...]


def _moba_select(q2, k2, bsz, seq_len, ts=512):
    ts = min(ts, seq_len)
    heads = q2.shape[1] // MOBA_HEAD_DIM
    nt = seq_len // ts
    tri = (jnp.arange(ts)[:, None] < jnp.arange(ts)[None, :]).astype(BF16)
    return pl.pallas_call(
        _moba_select_kernel,
        grid=(bsz, heads, nt),
        in_specs=[
            pl.BlockSpec((ts, LANES), lambda b, h, t: (b * nt + t, h // 2)),
            pl.BlockSpec((seq_len, LANES), lambda b, h, t: (b, h // 2)),
            pl.BlockSpec((ts, ts), lambda b, h, t: (0, 0)),
        ],
        out_specs=[pl.BlockSpec((1, 1, 8, ts), lambda b, h, t: (b, h, 0, t)),
                   pl.BlockSpec((1, 1, MOBA_MAX_BLOCKS, LANES), lambda b, h, t: (b, h, 0, 0))],
        out_shape=[jax.ShapeDtypeStruct((bsz, heads, 8, seq_len), jnp.int32),
                   jax.ShapeDtypeStruct((bsz, heads, MOBA_MAX_BLOCKS, LANES), F32)],
        scratch_shapes=[pltpu.VMEM((MOBA_MAX_BLOCKS, LANES), BF16), pltpu.VMEM((MOBA_MAX_BLOCKS, LANES), F32)],
        compiler_params=_params("parallel", "arbitrary", "arbitrary"),
        name="moba_select",
    )(q2, k2, tri)


def _softmax_block(q, kb, vb, causal):
    s = _dot_nt(q, kb)
    if causal:
        r = lax.broadcasted_iota(jnp.int32, s.shape, 0) % MOBA_BLOCK
        c = lax.broadcasted_iota(jnp.int32, s.shape, 1)
        s = jnp.where(c <= r, s, NEG_BIG)
    m = jnp.max(s, axis=1, keepdims=True)
    p = jnp.exp(s - m).astype(BF16)
    pv = _dot(p, jnp.concatenate([vb, jnp.ones(vb.shape, BF16)], axis=1))
    l = pv[:, LANES:LANES + 1]
    return pv[:, :LANES] / l, m + jnp.log(l)


def _moba_diag_kernel(q_ref, k_ref, v_ref, o_ref, lse_ref):
    q = q_ref[...]
    lane = lax.broadcasted_iota(jnp.int32, q.shape, 1)
    is_a = lane < MOBA_HEAD_DIM
    zero = jnp.zeros_like(q)
    qs = jnp.concatenate([jnp.where(is_a, q, zero), jnp.where(is_a, zero, q)], axis=0)
    o, lse = _softmax_block(qs, k_ref[...], v_ref[...], True)
    n = q.shape[0]
    o_ref[...] = jnp.where(is_a, o[:n], o[n:])
    lse_ref[...] = jnp.where(is_a, lse[:n], lse[n:])


def _moba_diag(q2, k2, v2, bsz, seq_len):
    npair = q2.shape[1] // LANES
    nb = seq_len // MOBA_BLOCK
    spec = pl.BlockSpec((MOBA_BLOCK, LANES), lambda b, p, i: (b * nb + i, p))
    out = jax.ShapeDtypeStruct(q2.shape, F32)
    return pl.pallas_call(
        _moba_diag_kernel,
        grid=(bsz, npair, nb),
        in_specs=[spec, spec, spec],
        out_specs=[spec, spec],
        out_shape=[out, out],
        compiler_params=_params("parallel", "parallel", "parallel"),
        name="moba_diag",
    )(q2, k2, v2)


MOBA_TILE = 256
MOBA_TILES_PER_STEP = 4


def _moba_grouped_kernel(tile_g_ref, tile_real_ref, qd_ref, *refs):
    n = MOBA_TILES_PER_STEP
    k_refs, v_refs, o_ref = refs[:n], refs[n:2 * n], refs[2 * n]
    i = pl.program_id(0)
    lane = lax.broadcasted_iota(jnp.int32, (MOBA_TILE, LANES), 1)
    for u in range(n):
        tile = i * n + u
        rows = pl.ds(u * MOBA_TILE, MOBA_TILE)
        head_half = (tile_g_ref[tile] // MOBA_MAX_BLOCKS) % 2

        @pl.when(tile_real_ref[tile] > 0)
        def _():
            o, lse = _softmax_block(qd_ref[rows, :], k_refs[u][...], v_refs[u][...], False)
            o_ref[rows, :] = jnp.where((lane // MOBA_HEAD_DIM) == head_half, o, lse)

        @pl.when(tile_real_ref[tile] == 0)
        def _():
            o_ref[rows, :] = jnp.full((MOBA_TILE, LANES), NEG_BIG, F32)


def _moba_grouped(qd, k2, v2, tile_g, tile_real, seq_len):
    n = MOBA_TILES_PER_STEP
    n_tiles = qd.shape[0] // MOBA_TILE
    nb = seq_len // MOBA_BLOCK
    heads = k2.shape[1] // MOBA_HEAD_DIM

    def kv_spec(u):
        def index(i, tg, tr):
            g = tg[i * n + u]
            b = g // (heads * MOBA_MAX_BLOCKS)
            h = (g // MOBA_MAX_BLOCKS) % heads
            return (b * nb + jnp.minimum(g % MOBA_MAX_BLOCKS, nb - 1), h // 2)
        return pl.BlockSpec((MOBA_BLOCK, LANES), index)

    grid_spec = pltpu.PrefetchScalarGridSpec(
        num_scalar_prefetch=2,
        grid=(n_tiles // n,),
        in_specs=[pl.BlockSpec((n * MOBA_TILE, LANES), lambda i, tg, tr: (i, 0))]
        + [kv_spec(u) for u in range(n)] + [kv_spec(u) for u in range(n)],
        out_specs=pl.BlockSpec((n * MOBA_TILE, LANES), lambda i, tg, tr: (i, 0)),
    )
    return pl.pallas_call(
        _moba_grouped_kernel,
        grid_spec=grid_spec,
        out_shape=jax.ShapeDtypeStruct(qd.shape, F32),
        compiler_params=_params("arbitrary"),
        name="moba_grouped",
    )(tile_g, tile_real, qd, *([k2] * n), *([v2] * n))


def _moba_combine_kernel(od_ref, lsed_ref, *refs):
    g_refs, o_ref = refs[:-1], refs[-1]
    lane = lax.broadcasted_iota(jnp.int32, od_ref.shape, 1)
    is_a = lane < MOBA_HEAD_DIM
    parts = [(od_ref[...], lsed_ref[...])]
    for s in range(MOBA_TOPK):
        xa = g_refs[s][0, 0, 0]
        xb = g_refs[MOBA_TOPK + s][0, 0, 0]
        o = jnp.where(is_a, xa, xb)
        lse = pltpu.roll(jnp.where(is_a, xb, xa), MOBA_HEAD_DIM, 1)
        parts.append((o, lse))
    m = parts[0][1]
    for _, lse in parts[1:]:
        m = jnp.maximum(m, lse)
    num = jnp.zeros(od_ref.shape, F32)
    den = jnp.zeros(od_ref.shape, F32)
    for o, lse in parts:
        w = jnp.exp(lse - m)
        num = num + w * o
        den = den + w
    o_ref[...] = (num / den).astype(BF16)


def _moba_combine(od, lsed, gath, bsz, seq_len, tm=256):
    npair = od.shape[1] // LANES
    nt = seq_len // tm
    spec = pl.BlockSpec((tm, LANES), lambda b, p, i: (b * nt + i, p))

    def g_spec(e, s):
        return pl.BlockSpec((1, 1, 1, tm, LANES), lambda b, p, i: (b, 2 * p + e, s, i, 0))

    return pl.pallas_call(
        _moba_combine_kernel,
        grid=(bsz, npair, nt),
        in_specs=[spec, spec] + [g_spec(e, s) for e in range(2) for s in range(MOBA_TOPK)],
        out_specs=spec,
        out_shape=jax.ShapeDtypeStruct(od.shape, BF16),
        compiler_params=_params("parallel", "parallel", "parallel"),
        name="moba_combine",
    )(od, lsed, *([gath] * (2 * MOBA_TOPK)))


def _gather_rows(table, idx):
    return jnp.take(table, idx, axis=0)


def _moba(q2, k2, v2, bsz, seq_len):
    heads = q2.shape[1] // MOBA_HEAD_DIM
    nbk = MOBA_MAX_BLOCKS
    sel, cnt = _moba_select(q2, k2, bsz, seq_len)
    od, lsed = _moba_diag(q2, k2, v2, bsz, seq_len)
    counts = cnt[..., 0].astype(jnp.int32).reshape(bsz * heads * nbk)
    pcounts = ((counts + MOBA_TILE - 1) // MOBA_TILE) * MOBA_TILE
    pends = jnp.cumsum(pcounts)
    pstarts = (pends - pcounts).reshape(bsz, heads, 1, 1, nbk)
    n_items = bsz * heads * seq_len * MOBA_TOPK
    step_rows = MOBA_TILE * MOBA_TILES_PER_STEP
    n_rows = -(-(n_items + bsz * heads * nbk * MOBA_TILE) // step_rows) * step_rows
    n_null = step_rows
    idx = sel[:, :, 0:MOBA_TOPK, :]
    rank = sel[:, :, MOBA_TOPK:2 * MOBA_TOPK, :]
    start = jnp.sum(jnp.where(idx[..., None] == jnp.arange(nbk), pstarts, 0), axis=-1)
    null_row = n_rows + jnp.arange(seq_len, dtype=jnp.int32) % n_null
    dest = jnp.where(idx >= 0, start + rank, null_row)
    n_tiles = (n_rows + n_null) // MOBA_TILE
    n_groups = bsz * heads * nbk
    tile_start = jnp.arange(n_tiles, dtype=jnp.int32) * MOBA_TILE
    tile_g = jnp.minimum(jnp.sum((pends[None, :] <= tile_start[:, None]).astype(jnp.int32), axis=1), n_groups - 1)
    tile_real = (tile_start < pends[-1]).astype(jnp.int32)
    lane = jnp.arange(LANES) // MOBA_HEAD_DIM
    qh = q2.reshape(bsz, seq_len, heads // 2, 1, LANES)
    qh = jnp.where(lane == jnp.arange(2)[:, None], qh, 0).reshape(bsz, seq_len, heads, LANES).transpose(0, 2, 1, 3)
    src_row = (jnp.arange(bsz * heads, dtype=jnp.int32).reshape(bsz, heads, 1, 1) * seq_len
               + jnp.arange(seq_len, dtype=jnp.int32))
    src_row = jnp.broadcast_to(src_row, idx.shape).reshape(-1)
    sort_key = jnp.where(idx >= 0, dest, n_rows + n_null).reshape(-1)
    _, sorted_src = lax.sort((sort_key, src_row), num_keys=1)
    cstarts = jnp.cumsum(counts) - counts
    off = tile_start - (pends - pcounts)[tile_g]
    row_off = off[:, None] + jnp.arange(MOBA_TILE, dtype=jnp.int32)[None, :]
    row_ok = (row_off < counts[tile_g][:, None]) & (tile_real[:, None] > 0)
    row_k = jnp.clip(cstarts[tile_g][:, None] + row_off, 0, n_items - 1)
    slot_src = jnp.where(row_ok, sorted_src[row_k], 0).reshape(-1)
    qd = _gather_rows(qh.reshape(-1, LANES), slot_src)
    part = _moba_grouped(qd, k2, v2, tile_g, tile_real, seq_len)
    gath = _gather_rows(part, dest.reshape(-1)).reshape(bsz, heads, MOBA_TOPK, seq_len, LANES)
    return _moba_combine(od, lsed, gath, bsz, seq_len)


def _mem_kv_kernel(mem_ref, gmem_ref, w_ref, gck_ref, k_ref, v_ref):
    xf = mem_ref[0]
    ms = jnp.mean(xf * xf, axis=-1, keepdims=True)
    h = (xf * lax.rsqrt(ms + EPS) * gmem_ref[...]).astype(BF16)
    kv = _dot(h, w_ref[...])
    w = k_ref.shape[2]
    hd = w // X_HEADS
    for c in range(X_HEADS):
        chunk = kv[:, c * hd:(c + 1) * hd]
        cms = jnp.mean(chunk * chunk, axis=-1, keepdims=True)
        k_ref[0, :, c * hd:(c + 1) * hd] = (chunk * lax.rsqrt(cms + EPS) * gck_ref[...]).astype(BF16)
    v_ref[0] = kv[:, w:].astype(BF16)


def _mem_kv(mem, g_mem, w_kv_mem, g_ck):
    bsz, m, d = mem.shape
    w = w_kv_mem.shape[1] // 2
    const = lambda b: (0, 0)
    out = jax.ShapeDtypeStruct((bsz, m, w), BF16)
    return pl.pallas_call(
        _mem_kv_kernel,
        grid=(bsz,),
        in_specs=[pl.BlockSpec((1, m, d), lambda b: (b, 0, 0)), pl.BlockSpec((1, d), const),
                  pl.BlockSpec((d, 2 * w), const), pl.BlockSpec((1, w // X_HEADS), const)],
        out_specs=[pl.BlockSpec((1, m, w), lambda b: (b, 0, 0))] * 2,
        out_shape=[out, out],
        compiler_params=_params("parallel"),
        name="mem_kv",
    )(mem, g_mem.reshape(1, d), w_kv_mem.astype(BF16), g_ck.reshape(1, -1))


def _merge_kernel(x_ref, ys_ref, u_ref, dskip_ref, om_ref, xq_ref, kc_ref, vc_ref, g_ref,
                  wglu_ref, wmo_ref, wco_ref, wout_ref, gffn_ref, wr_ref, br_ref,
                  x1_ref, h2_ref, logit_ref):
    d = x_ref.shape[1]
    y = ys_ref[...].astype(F32) + dskip_ref[...] * u_ref[...].astype(F32)
    ge = 0.5 * y * (1.0 + jnp.tanh(math.sqrt(2.0 / math.pi) * (y + 0.044715 * (y * y * y))))
    z = _dot(ge.astype(BF16), wglu_ref[...])
    merged = g_ref[:, 0:d].astype(F32) * (z[:, :d] * _sigmoid(z[:, d:]))
    merged = merged + g_ref[:, d:2 * d].astype(F32) * _dot(om_ref[...], wmo_ref[...])
    w = xq_ref.shape[1]
    hd = w // X_HEADS
    heads = []
    for c in range(X_HEADS):
        s = _dot_nt(xq_ref[:, c * hd:(c + 1) * hd], kc_ref[0, :, c * hd:(c + 1) * hd]) * (hd ** -0.5)
        p = jnp.exp(s - jnp.max(s, axis=1, keepdims=True))
        p = p / jnp.sum(p, axis=1, keepdims=True)
        heads.append(_dot(p.astype(BF16), vc_ref[0, :, c * hd:(c + 1) * hd]))
    oc = jnp.concatenate(heads, axis=1).astype(BF16)
    merged = merged + g_ref[:, 2 * d:3 * d].astype(F32) * _dot(oc, wco_ref[...])
    x1 = x_ref[...] + _dot(merged.astype(BF16), wout_ref[...])
    x1_ref[...] = x1
    ms = jnp.mean(x1 * x1, axis=-1, keepdims=True)
    h2 = (x1 * lax.rsqrt(ms + EPS) * gffn_ref[...]).astype(BF16)
    h2_ref[...] = h2
    logit_ref[...] = _dot(h2, wr_ref[...]) + br_ref[...]


def _merge(xt, ys, u, d_skip, om, xqn, kc, vc, gates, w_glu, w_mo, w_co, w_out, g_ffn, w_router, b_router,
           seq_len, tm=256):
    t, d = xt.shape
    w = d // 2
    m = kc.shape[1]
    ne = w_router.shape[1]
    nt = seq_len // tm
    row = lambda i: (i, 0)
    const = lambda i: (0, 0)
    per_b = lambda i: (i // nt, 0, 0)
    return pl.pallas_call(
        _merge_kernel,
        grid=(t // tm,),
        in_specs=[
            pl.BlockSpec((tm, d), row), pl.BlockSpec((tm, w), row), pl.BlockSpec((tm, w), row),
            pl.BlockSpec((1, w), const), pl.BlockSpec((tm, w), row), pl.BlockSpec((tm, w), row),
            pl.BlockSpec((1, m, w), per_b), pl.BlockSpec((1, m, w), per_b),
            pl.BlockSpec((tm, N_BRANCH * d), row),
            pl.BlockSpec((w, 2 * d), const), pl.BlockSpec((w, d), const), pl.BlockSpec((w, d), const),
            pl.BlockSpec((d, d), const), pl.BlockSpec((1, d), const),
            pl.BlockSpec((d, ne), const), pl.BlockSpec((1, ne), const),
        ],
        out_specs=[pl.BlockSpec((tm, d), row), pl.BlockSpec((tm, d), row), pl.BlockSpec((tm, ne), row)],
        out_shape=[jax.ShapeDtypeStruct((t, d), F32), jax.ShapeDtypeStruct((t, d), BF16),
                   jax.ShapeDtypeStruct((t, ne), F32)],
        compiler_params=_params("parallel"),
        name="merge",
    )(xt, ys, u, d_skip.reshape(1, w), om, xqn, kc, vc, gates,
      w_glu.astype(BF16), w_mo.astype(BF16), w_co.astype(BF16), w_out.astype(BF16),
      g_ffn.reshape(1, d), w_router.astype(BF16), b_router.reshape(1, ne))


def _moe_kernel(blk_e_ref, blk_used_ref, xs_ref, wgu_ref, bgu_ref, wd_ref, bd_ref, y_ref, wgu_bf, wd_bf):
    i = pl.program_id(0)
    prev = blk_e_ref[jnp.maximum(i - 1, 0)]

    @pl.when((i == 0) | (blk_e_ref[i] != prev))
    def _():
        wgu_bf[...] = wgu_ref[0].astype(BF16)
        wd_bf[...] = wd_ref[0].astype(BF16)

    @pl.when(blk_used_ref[i] > 0)
    def _():
        de = wd_bf.shape[0]
        gu = _dot(xs_ref[...], wgu_bf[...]) + bgu_ref[0]
        gate = jnp.minimum(gu[:, :de], SWIGLU_LIMIT)
        up = jnp.clip(gu[:, de:], -SWIGLU_LIMIT, SWIGLU_LIMIT)
        act = gate * _sigmoid(SWIGLU_ALPHA * gate) * (up + 1.0)
        y_ref[...] = _dot(act.astype(BF16), wd_bf[...]) + bd_ref[0]

    @pl.when(blk_used_ref[i] == 0)
    def _():
        y_ref[...] = jnp.zeros(y_ref.shape, y_ref.dtype)


def _moe_experts(xs, blk_e, blk_used, w_gu, b_gu, w_down, b_down):
    p, d = xs.shape
    ne, _, de2 = w_gu.shape
    de = de2 // 2
    nblk = p // EXPERT_ROWS
    grid_spec = pltpu.PrefetchScalarGridSpec(
        num_scalar_prefetch=2,
        grid=(nblk,),
        in_specs=[
            pl.BlockSpec((EXPERT_ROWS, d), lambda i, e, n: (i, 0)),
            pl.BlockSpec((1, d, de2), lambda i, e, n: (e[i], 0, 0)),
            pl.BlockSpec((1, 1, de2), lambda i, e, n: (e[i], 0, 0)),
            pl.BlockSpec((1, de, d), lambda i, e, n: (e[i], 0, 0)),
            pl.BlockSpec((1, 1, d), lambda i, e, n: (e[i], 0, 0)),
        ],
        out_specs=pl.BlockSpec((EXPERT_ROWS, d), lambda i, e, n: (i, 0)),
        scratch_shapes=[pltpu.VMEM((d, de2), BF16), pltpu.VMEM((de, d), BF16)],
    )
    return pl.pallas_call(
        _moe_kernel,
        grid_spec=grid_spec,
        out_shape=jax.ShapeDtypeStruct((p, d), F32),
        compiler_params=_params("arbitrary"),
        name="moe_experts",
    )(blk_e, blk_used, xs, w_gu, b_gu.reshape(ne, 1, de2), w_down, b_down.reshape(ne, 1, d))


def _moe_ffn(x1, h2, logits, w_gu, b_gu, w_down, b_down):
    t, d = x1.shape
    tk = t * TOPK_EXPERTS
    top_v, top_e = lax.top_k(logits, TOPK_EXPERTS)
    weights = jax.nn.softmax(top_v, axis=-1)
    e_flat = top_e.reshape(tk)
    expert_ids = jnp.arange(N_EXPERTS, dtype=e_flat.dtype)
    counts = jnp.sum((e_flat[:, None] == expert_ids[None, :]).astype(jnp.int32), axis=0)
    order = jnp.argsort(e_flat, stable=True).astype(jnp.int32)
    pos = jnp.argsort(order).astype(jnp.int32)
    pcounts = ((counts + EXPERT_ROWS - 1) // EXPERT_ROWS) * EXPERT_ROWS
    pends = jnp.cumsum(pcounts)
    pstarts = pends - pcounts
    cstarts = jnp.cumsum(counts) - counts
    dest = (pstarts - cstarts)[e_flat] + pos
    nblk = -(-tk // EXPERT_ROWS) + N_EXPERTS
    blk_start = jnp.arange(nblk, dtype=jnp.int32) * EXPERT_ROWS
    blk_e = jnp.minimum(jnp.sum((pends[None, :] <= blk_start[:, None]).astype(jnp.int32), axis=1), N_EXPERTS - 1)
    blk_used = (blk_start < pends[-1]).astype(jnp.int32)
    row_off = (blk_start - pstarts[blk_e])[:, None] + jnp.arange(EXPERT_ROWS, dtype=jnp.int32)[None, :]
    row_ok = (row_off < counts[blk_e][:, None]) & (blk_used[:, None] > 0)
    row_k = jnp.clip(cstarts[blk_e][:, None] + row_off, 0, tk - 1)
    slot_tok = jnp.where(row_ok, order[row_k] // TOPK_EXPERTS, 0).reshape(-1)
    xs = h2[slot_tok]
    ys = _moe_experts(xs, blk_e, blk_used, w_gu, b_gu, w_down, b_down)
    picked = ys[dest].reshape(t, TOPK_EXPERTS, d)
    return x1 + jnp.sum(picked * weights[:, :, None], axis=1)


def kernel(x, mem, g_mix, w_in, lam_re, lam_im, log_dt, b_re, b_im, c_re, c_im, d_skip, w_glu, g_q, g_k, w_moba_out, g_mem, w_kv_mem, g_cq, g_ck, w_cross_out, w_out, g_ffn, w_router, b_router, w_gu, b_gu, w_down, b_down):
    bsz, seq_len, d = x.shape
    xt = x.reshape(bsz * seq_len, d)
    for l in range(g_mix.shape[0]):
        u, q2, k2, v2, xqn, gates = _in_proj(xt, g_mix[l], w_in[l], g_q[l], g_k[l], g_cq[l], seq_len)
        mats = _s5_matrices(lam_re[l], lam_im[l], log_dt[l], b_re[l], b_im[l], c_re[l], c_im[l],
                            seq_len // S5_CHUNK)
        ys = _s5(u, mats, bsz, seq_len)
        om = _moba(q2, k2, v2, bsz, seq_len)
        kc, vc = _mem_kv(mem, g_mem[l], w_kv_mem[l], g_ck[l])
        x1, h2, logits = _merge(xt, ys, u, d_skip[l], om, xqn, kc, vc, gates, w_glu[l], w_moba_out[l],
                                w_cross_out[l], w_out[l], g_ffn[l], w_router[l], b_router[l], seq_len)
        xt = _moe_ffn(x1, h2, logits, w_gu[l], b_gu[l], w_down[l], b_down[l])
    return xt.reshape(bsz, seq_len, d)
```

```python
import functools
import math

import jax
import jax.numpy as jnp
from jax import lax
from jax.experimental import pallas as pl
from jax.experimental.pallas import tpu as pltpu

F32 = jnp.float32
BF16 = jnp.bfloat16

EPS = 1e-6
N_BRANCH = 3
SSM_GROUP = 16
SSM_STATE = 64
S5_CHUNK = 16
MOBA_HEAD_DIM = 64
MOBA_BLOCK = 256
MOBA_TOPK = 3
MOBA_MAX_BLOCKS = 64
ROPE_THETA = 10000.0
X_HEADS = 4
N_EXPERTS = 32
TOPK_EXPERTS = 4
SWIGLU_LIMIT = 7.0
SWIGLU_ALPHA = 1.702
EXPERT_ROWS = 512
NEG_BIG = -1e30
LANES = 128
VMEM_LIMIT_BYTES = 56 * 1024 * 1024


def _params(*sem):
    return pltpu.CompilerParams(dimension_semantics=sem, vmem_limit_bytes=VMEM_LIMIT_BYTES)


def _sigmoid(x):
    return 1.0 / (1.0 + jnp.exp(-x))


def _dot(a, b):
    return jnp.dot(a, b, preferred_element_type=F32)


def _dot_nt(a, b):
    return lax.dot_general(a, b, (((1,), (1,)), ((), ())), preferred_element_type=F32)


def _inproj_kernel(x_ref, gmix_ref, wa_ref, wg_ref, e64_ref, gq_ref, gk_ref, gcq_ref, cos_ref, sin_ref,
                   u_ref, q_ref, k_ref, v_ref, xq_ref, g_ref, qh_ref):
    xf = x_ref[...]
    ms = jnp.mean(xf * xf, axis=-1, keepdims=True)
    h = (xf * lax.rsqrt(ms + EPS) * gmix_ref[...]).astype(BF16)
    a = _dot(h, wa_ref[...])
    w = u_ref.shape[1]
    u_ref[...] = a[:, :w].astype(BF16)
    v_ref[...] = a[:, 3 * w:4 * w].astype(BF16)

    cos = jnp.tile(cos_ref[...], (1, w // LANES))
    sin = jnp.tile(sin_ref[...], (1, w // LANES))
    lane = lax.broadcasted_iota(jnp.int32, (xf.shape[0], w), 1)
    first_half = (lane % MOBA_HEAD_DIM) < (MOBA_HEAD_DIM // 2)

    def qk_norm_rope(raw, g):
        ss = _dot((raw * raw).astype(BF16), e64_ref[...])
        n = raw * lax.rsqrt(ss * (1.0 / MOBA_HEAD_DIM) + EPS) * g
        rot = jnp.where(first_half,
                        pltpu.roll(n, w - MOBA_HEAD_DIM // 2, 1),
                        pltpu.roll(n, MOBA_HEAD_DIM // 2, 1))
        return n * cos + rot * sin

    q = qk_norm_rope(a[:, w:2 * w], gq_ref[...]) * (MOBA_HEAD_DIM ** -0.5)
    q_ref[...] = q.astype(BF16)
    in_a = lax.broadcasted_iota(jnp.int32, (xf.shape[0], LANES), 1) < MOBA_HEAD_DIM
    for p in range(w // LANES):
        pair = q[:, p * LANES:(p + 1) * LANES].astype(BF16).astype(F32)
        qh_ref[:, (2 * p) * LANES:(2 * p + 1) * LANES] = jnp.where(in_a, pair, 0.0)
        qh_ref[:, (2 * p + 1) * LANES:(2 * p + 2) * LANES] = jnp.where(in_a, 0.0, pair)
    k_ref[...] = qk_norm_rope(a[:, 2 * w:3 * w], gk_ref[...]).astype(BF16)

    xq = a[:, 4 * w:5 * w]
    hd = w // X_HEADS
    for c in range(X_HEADS):
        chunk = xq[:, c * hd:(c + 1) * hd]
        cms = jnp.mean(chunk * chunk, axis=-1, keepdims=True)
        xq_ref[:, c * hd:(c + 1) * hd] = (chunk * lax.rsqrt(cms + EPS) * gcq_ref[...]).astype(BF16)

    d = xf.shape[1]
    for c in range(N_BRANCH):
        z = _dot(h, wg_ref[:, c * d:(c + 1) * d])
        g_ref[:, c * d:(c + 1) * d] = _sigmoid(z).astype(BF16)


def _in_proj(xt, g_mix, w_in, g_q, g_k, g_cq, seq_len, tm=256):
    t, d = xt.shape
    w = d // 2
    wa = w_in[:, :5 * w].astype(BF16)
    wg = w_in[:, 5 * w:].astype(BF16)
    heads = w // MOBA_HEAD_DIM
    e64 = jnp.kron(jnp.eye(heads, dtype=F32), jnp.ones((MOBA_HEAD_DIM, MOBA_HEAD_DIM), F32)).astype(BF16)
    half = MOBA_HEAD_DIM // 2
    inv = ROPE_THETA ** (-jnp.arange(half, dtype=F32) / half)
    ang = jnp.arange(seq_len, dtype=F32)[:, None] * inv[None, :]
    cos = jnp.tile(jnp.cos(ang), (1, LANES // half))
    sin = jnp.tile(jnp.concatenate([-jnp.sin(ang), jnp.sin(ang)], axis=1), (1, LANES // MOBA_HEAD_DIM))
    nt = seq_len // tm
    row = lambda i: (i, 0)
    const = lambda i: (0, 0)
    out_w = jax.ShapeDtypeStruct((t, w), BF16)
    return pl.pallas_call(
        _inproj_kernel,
        grid=(t // tm,),
        in_specs=[
            pl.BlockSpec((tm, d), row),
            pl.BlockSpec((1, d), const),
            pl.BlockSpec((d, 5 * w), const),
            pl.BlockSpec((d, N_BRANCH * d), const),
            pl.BlockSpec((w, w), const),
            pl.BlockSpec((1, w), const),
            pl.BlockSpec((1, w), const),
            pl.BlockSpec((1, w // X_HEADS), const),
            pl.BlockSpec((tm, LANES), lambda i: (i % nt, 0)),
            pl.BlockSpec((tm, LANES), lambda i: (i % nt, 0)),
        ],
        out_specs=[pl.BlockSpec((tm, w), row)] * 5 + [pl.BlockSpec((tm, N_BRANCH * d), row),
                                                      pl.BlockSpec((tm, 2 * w), row)],
        out_shape=[out_w] * 5 + [jax.ShapeDtypeStruct((t, N_BRANCH * d), BF16),
                                 jax.ShapeDtypeStruct((t, 2 * w), F32)],
        compiler_params=_params("parallel"),
        name="in_proj",
    )(xt, g_mix.reshape(1, d), wa, wg, e64,
      jnp.tile(g_q, heads).reshape(1, w), jnp.tile(g_k, heads).reshape(1, w), g_cq.reshape(1, -1), cos, sin)


def _s5_matrices(lam_re, lam_im, log_dt, b_re, b_im, c_re, c_im, n_chunks):
    hp = lax.Precision.HIGHEST
    c = S5_CHUNK
    dt = jnp.exp(log_dt)[:, None]
    mag = jnp.exp(lam_re * dt)
    ar = mag * jnp.cos(lam_im * dt)
    ai = mag * jnp.sin(lam_im * dt)
    nr = ar - 1.0
    den = lam_re * lam_re + lam_im * lam_im
    cr = (nr * lam_re + ai * lam_im) / den
    ci = (ai * lam_re - nr * lam_im) / den
    bbr = cr[..., None] * b_re - ci[..., None] * b_im
    bbi = cr[..., None] * b_im + ci[..., None] * b_re

    def power(n):
        nf = n.astype(F32)[None, :, None]
        m = jnp.exp((lam_re * dt)[:, None, :] * nf)
        th = (lam_im * dt)[:, None, :] * nf
        return m * jnp.cos(th), m * jnp.sin(th)

    pr, pi = power(jnp.arange(c + 1))
    kbr = pr[..., None] * bbr[:, None] - pi[..., None] * bbi[:, None]
    kbi = pr[..., None] * bbi[:, None] + pi[..., None] * bbr[:, None]
    kk = (jnp.einsum('ghp,gtpc->gthc', c_re, kbr, precision=hp)
          - jnp.einsum('ghp,gtpc->gthc', c_im, kbi, precision=hp))
    tq = jnp.arange(c)
    lag = tq[None, :] - tq[:, None]
    toep = kk[:, jnp.clip(lag, 0, c)]
    toep = jnp.where((lag >= 0)[None, :, :, None, None], toep, 0.0)
    g = toep.shape[0]
    toep = toep.transpose(0, 1, 4, 2, 3).reshape(g, c * SSM_GROUP, c * SSM_GROUP)
    rev = c - 1 - tq
    w_in = jnp.concatenate([kbr[:, rev].transpose(0, 1, 3, 2), kbi[:, rev].transpose(0, 1, 3, 2)], axis=-1)
    w_in = w_in.reshape(g, c * SSM_GROUP, 2 * SSM_STATE)
    prn, pin = pr[:, 1:], pi[:, 1:]
    wo_r = c_re[:, None] * prn[:, :, None, :] - c_im[:, None] * pin[:, :, None, :]
    wo_i = -c_re[:, None] * pin[:, :, None, :] - c_im[:, None] * prn[:, :, None, :]
    w_out = jnp.concatenate([wo_r, wo_i], axis=-1).transpose(0, 3, 1, 2).reshape(g, 2 * SSM_STATE, c * SSM_GROUP)
    n_steps = max(1, int(math.ceil(math.log2(n_chunks))))
    qr, qi = power(c * (2 ** jnp.arange(n_steps)))
    pa = jnp.concatenate([qr, qr], axis=-1)
    pb = jnp.concatenate([-qi, qi], axis=-1)
    return toep.astype(BF16), w_in.astype(BF16), w_out.astype(BF16), pa, pb


def _s5_kernel(u_ref, toep_ref, win_ref, wout_ref, pa_ref, pb_ref, y_ref):
    u = u_ref[0, 0]
    s = _dot(u, win_ref[0])
    nc = s.shape[0]
    row = lax.broadcasted_iota(jnp.int32, s.shape, 0)
    n_steps = pa_ref.shape[1]
    for k in range(n_steps):
        sh = 1 << k
        if sh >= nc:
            break
        prev = jnp.where(row >= sh, pltpu.roll(s, sh, 0), 0.0)
        s = s + pa_ref[0, k:k + 1, :] * prev + pb_ref[0, k:k + 1, :] * pltpu.roll(prev, SSM_STATE, 1)
    s_start = jnp.where(row >= 1, pltpu.roll(s, 1, 0), 0.0)
    y = _dot(u, toep_ref[0]) + _dot(s_start.astype(BF16), wout_ref[0])
    y_ref[0, 0] = y.astype(BF16)


def _s5(u, mats, bsz, seq_len):
    toep, w_in, w_out, pa, pb = mats
    g = toep.shape[0]
    nc = seq_len // S5_CHUNK
    cw = S5_CHUNK * SSM_GROUP
    ug = u.reshape(bsz, nc, S5_CHUNK, g, SSM_GROUP).transpose(0, 3, 1, 2, 4).reshape(bsz, g, nc, cw)
    per_g = lambda b, j: (j, 0, 0)
    y = pl.pallas_call(
        _s5_kernel,
        grid=(bsz, g),
        in_specs=[
            pl.BlockSpec((1, 1, nc, cw), lambda b, j: (b, j, 0, 0)),
            pl.BlockSpec((1, cw, cw), per_g),
            pl.BlockSpec((1, cw, 2 * SSM_STATE), per_g),
            pl.BlockSpec((1, 2 * SSM_STATE, cw), per_g),
            pl.BlockSpec((1,) + pa.shape[1:], per_g),
            pl.BlockSpec((1,) + pb.shape[1:], per_g),
        ],
        out_specs=pl.BlockSpec((1, 1, nc, cw), lambda b, j: (b, j, 0, 0)),
        out_shape=jax.ShapeDtypeStruct((bsz, g, nc, cw), BF16),
        compiler_params=_params("parallel", "parallel"),
        name="s5_scan",
    )(ug, toep, w_in, w_out, pa, pb)
    return y.reshape(bsz, g, nc, S5_CHUNK, SSM_GROUP).transpose(0, 2, 3, 1, 4).reshape(bsz * seq_len, g * SSM_GROUP)


def _moba_select_kernel(q_ref, k_ref, tri_ref, sel_ref, cnt_out_ref, km_ref, cnt_ref):
    h = pl.program_id(1)
    t = pl.program_id(2)
    ts = q_ref.shape[0]
    nbk = MOBA_MAX_BLOCKS

    @pl.when(t == 0)
    def _():
        kk = k_ref[...].astype(F32)
        nb = kk.shape[0] // MOBA_BLOCK
        km = jnp.sum(kk.reshape(nb, MOBA_BLOCK, LANES), axis=1) * (1.0 / MOBA_BLOCK)
        if nb < nbk:
            km = jnp.concatenate([km, jnp.zeros((nbk - nb, LANES), F32)], axis=0)
        lane = lax.broadcasted_iota(jnp.int32, (nbk, LANES), 1)
        km_ref[...] = jnp.where((lane // MOBA_HEAD_DIM) == (h % 2), km, 0.0).astype(BF16)
        cnt_ref[...] = jnp.zeros(cnt_ref.shape, F32)

    gate = _dot_nt(km_ref[...], q_ref[...])
    blk = lax.broadcasted_iota(jnp.int32, gate.shape, 0)
    qblk = (t * ts + lax.broadcasted_iota(jnp.int32, gate.shape, 1)) // MOBA_BLOCK
    g = jnp.where(blk < qblk, gate, -jnp.inf)
    selected = jnp.zeros(gate.shape, jnp.bool_)
    picks = []
    for _ in range(MOBA_TOPK):
        mx = jnp.max(g, axis=0, keepdims=True)
        idx = jnp.min(jnp.where(g == mx, blk, nbk), axis=0, keepdims=True)
        hit = blk == idx
        ok = (idx[0:1] < qblk[0:1]) & (mx > -jnp.inf)
        picks.append((hit, idx, ok))
        selected = selected | (hit & ok)
        g = jnp.where(hit, -jnp.inf, g)
    sel01 = jnp.where(selected, 1.0, 0.0).astype(BF16)
    prefix = _dot(sel01, tri_ref[...]) + jnp.tile(cnt_ref[...], (1, ts // LANES))
    rows = [jnp.where(ok, idx, -1) for (_, idx, ok) in picks]
    rows += [jnp.sum(jnp.where(hit, prefix, 0.0), axis=0, keepdims=True).astype(jnp.int32) for (hit, _, _) in picks]
    rows += [jnp.zeros((1, ts), jnp.int32)] * (8 - 2 * MOBA_TOPK)
    sel_ref[0, 0] = jnp.concatenate(rows, axis=0)
    cnt_ref[...] = cnt_ref[...] + _dot(sel01, jnp.ones((ts, LANES), BF16))
    cnt_out_ref[0, 0] = cnt_ref[...]


def _moba_select(q2, k2, bsz, seq_len, ts=512):
    ts = min(ts, seq_len)
    heads = q2.shape[1] // MOBA_HEAD_DIM
    nt = seq_len // ts
    tri = (jnp.arange(ts)[:, None] < jnp.arange(ts)[None, :]).astype(BF16)
    return pl.pallas_call(
        _moba_select_kernel,
        grid=(bsz, heads, nt),
        in_specs=[
            pl.BlockSpec((ts, LANES), lambda b, h, t: (b * nt + t, h // 2)),
            pl.BlockSpec((seq_len, LANES), lambda b, h, t: (b, h // 2)),
            pl.BlockSpec((ts, ts), lambda b, h, t: (0, 0)),
        ],
        out_specs=[pl.BlockSpec((1, 1, 8, ts), lambda b, h, t: (b, h, 0, t)),
                   pl.BlockSpec((1, 1, MOBA_MAX_BLOCKS, LANES), lambda b, h, t: (b, h, 0, 0))],
        out_shape=[jax.ShapeDtypeStruct((bsz, heads, 8, seq_len), jnp.int32),
                   jax.ShapeDtypeStruct((bsz, heads, MOBA_MAX_BLOCKS, LANES), F32)],
        scratch_shapes=[pltpu.VMEM((MOBA_MAX_BLOCKS, LANES), BF16), pltpu.VMEM((MOBA_MAX_BLOCKS, LANES), F32)],
        compiler_params=_params("parallel", "arbitrary", "arbitrary"),
        name="moba_select",
    )(q2, k2, tri)


def _softmax_block(q, kb, vb, causal):
    s = _dot_nt(q, kb)
    if causal:
        r = lax.broadcasted_iota(jnp.int32, s.shape, 0) % MOBA_BLOCK
        c = lax.broadcasted_iota(jnp.int32, s.shape, 1)
        s = jnp.where(c <= r, s, NEG_BIG)
    m = jnp.max(s, axis=1, keepdims=True)
    p = jnp.exp(s - m).astype(BF16)
    pv = _dot(p, jnp.concatenate([vb, jnp.ones(vb.shape, BF16)], axis=1))
    l = pv[:, LANES:LANES + 1]
    return pv[:, :LANES] / l, m + jnp.log(l)


def _moba_diag_kernel(q_ref, k_ref, v_ref, o_ref, lse_ref):
    q = q_ref[...]
    lane = lax.broadcasted_iota(jnp.int32, q.shape, 1)
    is_a = lane < MOBA_HEAD_DIM
    zero = jnp.zeros_like(q)
    qs = jnp.concatenate([jnp.where(is_a, q, zero), jnp.where(is_a, zero, q)], axis=0)
    o, lse = _softmax_block(qs, k_ref[...], v_ref[...], True)
    n = q.shape[0]
    o_ref[...] = jnp.where(is_a, o[:n], o[n:])
    lse_ref[...] = jnp.where(is_a, lse[:n], lse[n:])


def _moba_diag(q2, k2, v2, bsz, seq_len):
    npair = q2.shape[1] // LANES
    nb = seq_len // MOBA_BLOCK
    spec = pl.BlockSpec((MOBA_BLOCK, LANES), lambda b, p, i: (b * nb + i, p))
    out = jax.ShapeDtypeStruct(q2.shape, F32)
    return pl.pallas_call(
        _moba_diag_kernel,
        grid=(bsz, npair, nb),
        in_specs=[spec, spec, spec],
        out_specs=[spec, spec],
        out_shape=[out, out],
        compiler_params=_params("parallel", "parallel", "parallel"),
        name="moba_diag",
    )(q2, k2, v2)


MOBA_TILE = 256
MOBA_TILES_PER_STEP = 4


def _moba_grouped_kernel(tile_g_ref, tile_real_ref, qd_ref, *refs):
    n = MOBA_TILES_PER_STEP
    k_refs, v_refs, o_ref = refs[:n], refs[n:2 * n], refs[2 * n]
    i = pl.program_id(0)
    lane = lax.broadcasted_iota(jnp.int32, (MOBA_TILE, LANES), 1)
    for u in range(n):
        tile = i * n + u
        rows = pl.ds(u * MOBA_TILE, MOBA_TILE)
        head_half = (tile_g_ref[tile] // MOBA_MAX_BLOCKS) % 2

        @pl.when(tile_real_ref[tile] > 0)
        def _():
            o, lse = _softmax_block(qd_ref[rows, :].astype(BF16), k_refs[u][...], v_refs[u][...], False)
            o_ref[rows, :] = jnp.where((lane // MOBA_HEAD_DIM) == head_half, o, lse)

        @pl.when(tile_real_ref[tile] == 0)
        def _():
            o_ref[rows, :] = jnp.full((MOBA_TILE, LANES), NEG_BIG, F32)


def _moba_grouped(qd, k2, v2, tile_g, tile_real, seq_len):
    n = MOBA_TILES_PER_STEP
    n_tiles = qd.shape[0] // MOBA_TILE
    nb = seq_len // MOBA_BLOCK
    heads = k2.shape[1] // MOBA_HEAD_DIM

    def kv_spec(u):
        def index(i, tg, tr):
            g = tg[i * n + u]
            b = g // (heads * MOBA_MAX_BLOCKS)
            h = (g // MOBA_MAX_BLOCKS) % heads
            return (b * nb + jnp.minimum(g % MOBA_MAX_BLOCKS, nb - 1), h // 2)
        return pl.BlockSpec((MOBA_BLOCK, LANES), index)

    grid_spec = pltpu.PrefetchScalarGridSpec(
        num_scalar_prefetch=2,
        grid=(n_tiles // n,),
        in_specs=[pl.BlockSpec((n * MOBA_TILE, LANES), lambda i, tg, tr: (i, 0))]
        + [kv_spec(u) for u in range(n)] + [kv_spec(u) for u in range(n)],
        out_specs=pl.BlockSpec((n * MOBA_TILE, LANES), lambda i, tg, tr: (i, 0)),
    )
    return pl.pallas_call(
        _moba_grouped_kernel,
        grid_spec=grid_spec,
        out_shape=jax.ShapeDtypeStruct(qd.shape, F32),
        compiler_params=_params("arbitrary"),
        name="moba_grouped",
    )(tile_g, tile_real, qd, *([k2] * n), *([v2] * n))


def _moba_combine_kernel(od_ref, lsed_ref, *refs):
    g_refs, o_ref = refs[:-1], refs[-1]
    lane = lax.broadcasted_iota(jnp.int32, od_ref.shape, 1)
    is_a = lane < MOBA_HEAD_DIM
    parts = [(od_ref[...], lsed_ref[...])]
    for s in range(MOBA_TOPK):
        xa = g_refs[s][0, 0, 0]
        xb = g_refs[MOBA_TOPK + s][0, 0, 0]
        o = jnp.where(is_a, xa, xb)
        lse = pltpu.roll(jnp.where(is_a, xb, xa), MOBA_HEAD_DIM, 1)
        parts.append((o, lse))
    m = parts[0][1]
    for _, lse in parts[1:]:
        m = jnp.maximum(m, lse)
    num = jnp.zeros(od_ref.shape, F32)
    den = jnp.zeros(od_ref.shape, F32)
    for o, lse in parts:
        w = jnp.exp(lse - m)
        num = num + w * o
        den = den + w
    o_ref[...] = (num / den).astype(BF16)


def _moba_combine(od, lsed, gath, bsz, seq_len, tm=256):
    npair = od.shape[1] // LANES
    nt = seq_len // tm
    spec = pl.BlockSpec((tm, LANES), lambda b, p, i: (b * nt + i, p))

    def g_spec(e, s):
        return pl.BlockSpec((1, 1, 1, tm, LANES), lambda b, p, i: (b, 2 * p + e, s, i, 0))

    return pl.pallas_call(
        _moba_combine_kernel,
        grid=(bsz, npair, nt),
        in_specs=[spec, spec] + [g_spec(e, s) for e in range(2) for s in range(MOBA_TOPK)],
        out_specs=spec,
        out_shape=jax.ShapeDtypeStruct(od.shape, BF16),
        compiler_params=_params("parallel", "parallel", "parallel"),
        name="moba_combine",
    )(od, lsed, *([gath] * (2 * MOBA_TOPK)))


def _gather_rows(table, idx):
    return jnp.take(table, idx, axis=0)


SORT_LAST = jnp.iinfo(jnp.int32).max


def _rows_by_sort(dest, src, counts, pcounts, tile, n_out):
    fill = counts[:, None] + jnp.arange(tile, dtype=jnp.int32)[None, :]
    first = (jnp.cumsum(pcounts) - pcounts)[:, None]
    fill_key = jnp.where(fill < pcounts[:, None], first + fill, SORT_LAST).reshape(-1)
    keys = jnp.concatenate([dest, fill_key])
    vals = jnp.concatenate([src, jnp.zeros(fill_key.shape, jnp.int32)])
    _, by_row = lax.sort((keys, vals), num_keys=1)
    return jnp.concatenate([by_row, jnp.zeros((n_out - by_row.shape[0],), jnp.int32)])


def _moba(q2, qh, k2, v2, bsz, seq_len):
    heads = q2.shape[1] // MOBA_HEAD_DIM
    nbk = MOBA_MAX_BLOCKS
    sel, cnt = _moba_select(q2, k2, bsz, seq_len)
    od, lsed = _moba_diag(q2, k2, v2, bsz, seq_len)
    counts = cnt[..., 0].astype(jnp.int32).reshape(bsz * heads * nbk)
    pcounts = ((counts + MOBA_TILE - 1) // MOBA_TILE) * MOBA_TILE
    pends = jnp.cumsum(pcounts)
    pstarts = (pends - pcounts).reshape(bsz, heads, 1, 1, nbk)
    n_items = bsz * heads * seq_len * MOBA_TOPK
    step_rows = MOBA_TILE * MOBA_TILES_PER_STEP
    n_rows = -(-(n_items + bsz * heads * nbk * MOBA_TILE) // step_rows) * step_rows
    n_null = step_rows
    idx = sel[:, :, 0:MOBA_TOPK, :]
    rank = sel[:, :, MOBA_TOPK:2 * MOBA_TOPK, :]
    start = jnp.sum(jnp.where(idx[..., None] == jnp.arange(nbk), pstarts, 0), axis=-1)
    null_row = n_rows + jnp.arange(seq_len, dtype=jnp.int32) % n_null
    dest = jnp.where(idx >= 0, start + rank, null_row)
    n_tiles = (n_rows + n_null) // MOBA_TILE
    n_groups = bsz * heads * nbk
    tile_start = jnp.arange(n_tiles, dtype=jnp.int32) * MOBA_TILE
    tile_g = jnp.minimum(jnp.sum((pends[None, :] <= tile_start[:, None]).astype(jnp.int32), axis=1), n_groups - 1)
    tile_real = (tile_start < pends[-1]).astype(jnp.int32)
    src_row = ((jnp.arange(bsz, dtype=jnp.int32).reshape(bsz, 1, 1, 1) * seq_len
                + jnp.arange(seq_len, dtype=jnp.int32)) * heads
               + jnp.arange(heads, dtype=jnp.int32).reshape(1, heads, 1, 1))
    src_row = jnp.broadcast_to(src_row, idx.shape).reshape(-1)
    sort_key = jnp.where(idx >= 0, dest, SORT_LAST).reshape(-1)
    slot_src = _rows_by_sort(sort_key, src_row, counts, pcounts, MOBA_TILE, n_rows + n_null)
    qd = _gather_rows(qh.reshape(-1, LANES), slot_src)
    part = _moba_grouped(qd, k2, v2, tile_g, tile_real, seq_len)
    gath = _gather_rows(part, dest.reshape(-1)).reshape(bsz, heads, MOBA_TOPK, seq_len, LANES)
    return _moba_combine(od, lsed, gath, bsz, seq_len)


def _mem_kv_kernel(mem_ref, gmem_ref, w_ref, gck_ref, k_ref, v_ref):
    xf = mem_ref[0]
    ms = jnp.mean(xf * xf, axis=-1, keepdims=True)
    h = (xf * lax.rsqrt(ms + EPS) * gmem_ref[...]).astype(BF16)
    kv = _dot(h, w_ref[...])
    w = k_ref.shape[2]
    hd = w // X_HEADS
    for c in range(X_HEADS):
        chunk = kv[:, c * hd:(c + 1) * hd]
        cms = jnp.mean(chunk * chunk, axis=-1, keepdims=True)
        k_ref[0, :, c * hd:(c + 1) * hd] = (chunk * lax.rsqrt(cms + EPS) * gck_ref[...]).astype(BF16)
    v_ref[0] = kv[:, w:].astype(BF16)


def _mem_kv(mem, g_mem, w_kv_mem, g_ck):
    bsz, m, d = mem.shape
    w = w_kv_mem.shape[1] // 2
    const = lambda b: (0, 0)
    out = jax.ShapeDtypeStruct((bsz, m, w), BF16)
    return pl.pallas_call(
        _mem_kv_kernel,
        grid=(bsz,),
        in_specs=[pl.BlockSpec((1, m, d), lambda b: (b, 0, 0)), pl.BlockSpec((1, d), const),
                  pl.BlockSpec((d, 2 * w), const), pl.BlockSpec((1, w // X_HEADS), const)],
        out_specs=[pl.BlockSpec((1, m, w), lambda b: (b, 0, 0))] * 2,
        out_shape=[out, out],
        compiler_params=_params("parallel"),
        name="mem_kv",
    )(mem, g_mem.reshape(1, d), w_kv_mem.astype(BF16), g_ck.reshape(1, -1))


def _merge_kernel(x_ref, ys_ref, u_ref, dskip_ref, om_ref, xq_ref, kc_ref, vc_ref, g_ref,
                  wglu_ref, wmo_ref, wco_ref, wout_ref, gffn_ref, wr_ref, br_ref,
                  x1_ref, h2_ref, logit_ref):
    d = x_ref.shape[1]
    y = ys_ref[...].astype(F32) + dskip_ref[...] * u_ref[...].astype(F32)
    ge = 0.5 * y * (1.0 + jnp.tanh(math.sqrt(2.0 / math.pi) * (y + 0.044715 * (y * y * y))))
    z = _dot(ge.astype(BF16), wglu_ref[...])
    merged = g_ref[:, 0:d].astype(F32) * (z[:, :d] * _sigmoid(z[:, d:]))
    merged = merged + g_ref[:, d:2 * d].astype(F32) * _dot(om_ref[...], wmo_ref[...])
    w = xq_ref.shape[1]
    hd = w // X_HEADS
    heads = []
    for c in range(X_HEADS):
        s = _dot_nt(xq_ref[:, c * hd:(c + 1) * hd], kc_ref[0, :, c * hd:(c + 1) * hd]) * (hd ** -0.5)
        p = jnp.exp(s - jnp.max(s, axis=1, keepdims=True))
        p = p / jnp.sum(p, axis=1, keepdims=True)
        heads.append(_dot(p.astype(BF16), vc_ref[0, :, c * hd:(c + 1) * hd]))
    oc = jnp.concatenate(heads, axis=1).astype(BF16)
    merged = merged + g_ref[:, 2 * d:3 * d].astype(F32) * _dot(oc, wco_ref[...])
    x1 = x_ref[...] + _dot(merged.astype(BF16), wout_ref[...])
    x1_ref[...] = x1
    ms = jnp.mean(x1 * x1, axis=-1, keepdims=True)
    h2 = (x1 * lax.rsqrt(ms + EPS) * gffn_ref[...]).astype(BF16)
    h2_ref[...] = h2.astype(F32)
    logit_ref[...] = _dot(h2, wr_ref[...]) + br_ref[...]


def _merge(xt, ys, u, d_skip, om, xqn, kc, vc, gates, w_glu, w_mo, w_co, w_out, g_ffn, w_router, b_router,
           seq_len, tm=256):
    t, d = xt.shape
    w = d // 2
    m = kc.shape[1]
    ne = w_router.shape[1]
    nt = seq_len // tm
    row = lambda i: (i, 0)
    const = lambda i: (0, 0)
    per_b = lambda i: (i // nt, 0, 0)
    return pl.pallas_call(
        _merge_kernel,
        grid=(t // tm,),
        in_specs=[
            pl.BlockSpec((tm, d), row), pl.BlockSpec((tm, w), row), pl.BlockSpec((tm, w), row),
            pl.BlockSpec((1, w), const), pl.BlockSpec((tm, w), row), pl.BlockSpec((tm, w), row),
            pl.BlockSpec((1, m, w), per_b), pl.BlockSpec((1, m, w), per_b),
            pl.BlockSpec((tm, N_BRANCH * d), row),
            pl.BlockSpec((w, 2 * d), const), pl.BlockSpec((w, d), const), pl.BlockSpec((w, d), const),
            pl.BlockSpec((d, d), const), pl.BlockSpec((1, d), const),
            pl.BlockSpec((d, ne), const), pl.BlockSpec((1, ne), const),
        ],
        out_specs=[pl.BlockSpec((tm, d), row), pl.BlockSpec((tm, d), row), pl.BlockSpec((tm, ne), row)],
        out_shape=[jax.ShapeDtypeStruct((t, d), F32), jax.ShapeDtypeStruct((t, d), F32),
                   jax.ShapeDtypeStruct((t, ne), F32)],
        compiler_params=_params("parallel"),
        name="merge",
    )(xt, ys, u, d_skip.reshape(1, w), om, xqn, kc, vc, gates,
      w_glu.astype(BF16), w_mo.astype(BF16), w_co.astype(BF16), w_out.astype(BF16),
      g_ffn.reshape(1, d), w_router.astype(BF16), b_router.reshape(1, ne))


def _moe_kernel(blk_e_ref, blk_used_ref, xs_ref, wgu_ref, bgu_ref, wd_ref, bd_ref, y_ref, wgu_bf, wd_bf):
    i = pl.program_id(0)
    prev = blk_e_ref[jnp.maximum(i - 1, 0)]

    @pl.when((i == 0) | (blk_e_ref[i] != prev))
    def _():
        wgu_bf[...] = wgu_ref[0].astype(BF16)
        wd_bf[...] = wd_ref[0].astype(BF16)

    @pl.when(blk_used_ref[i] > 0)
    def _():
        de = wd_bf.shape[0]
        gu = _dot(xs_ref[...].astype(BF16), wgu_bf[...]) + bgu_ref[0]
        gate = jnp.minimum(gu[:, :de], SWIGLU_LIMIT)
        up = jnp.clip(gu[:, de:], -SWIGLU_LIMIT, SWIGLU_LIMIT)
        act = gate * _sigmoid(SWIGLU_ALPHA * gate) * (up + 1.0)
        y_ref[...] = _dot(act.astype(BF16), wd_bf[...]) + bd_ref[0]

    @pl.when(blk_used_ref[i] == 0)
    def _():
        y_ref[...] = jnp.zeros(y_ref.shape, y_ref.dtype)


def _moe_experts(xs, blk_e, blk_used, w_gu, b_gu, w_down, b_down):
    p, d = xs.shape
    ne, _, de2 = w_gu.shape
    de = de2 // 2
    nblk = p // EXPERT_ROWS
    grid_spec = pltpu.PrefetchScalarGridSpec(
        num_scalar_prefetch=2,
        grid=(nblk,),
        in_specs=[
            pl.BlockSpec((EXPERT_ROWS, d), lambda i, e, n: (i, 0)),
            pl.BlockSpec((1, d, de2), lambda i, e, n: (e[i], 0, 0)),
            pl.BlockSpec((1, 1, de2), lambda i, e, n: (e[i], 0, 0)),
            pl.BlockSpec((1, de, d), lambda i, e, n: (e[i], 0, 0)),
            pl.BlockSpec((1, 1, d), lambda i, e, n: (e[i], 0, 0)),
        ],
        out_specs=pl.BlockSpec((EXPERT_ROWS, d), lambda i, e, n: (i, 0)),
        scratch_shapes=[pltpu.VMEM((d, de2), BF16), pltpu.VMEM((de, d), BF16)],
    )
    return pl.pallas_call(
        _moe_kernel,
        grid_spec=grid_spec,
        out_shape=jax.ShapeDtypeStruct((p, d), F32),
        compiler_params=_params("arbitrary"),
        name="moe_experts",
    )(blk_e, blk_used, xs, w_gu, b_gu.reshape(ne, 1, de2), w_down, b_down.reshape(ne, 1, d))


def _moe_ffn(x1, h2, logits, w_gu, b_gu, w_down, b_down):
    t, d = x1.shape
    tk = t * TOPK_EXPERTS
    top_v, top_e = lax.top_k(logits, TOPK_EXPERTS)
    weights = jax.nn.softmax(top_v, axis=-1)
    e_flat = top_e.reshape(tk)
    expert_ids = jnp.arange(N_EXPERTS, dtype=e_flat.dtype)
    is_e = e_flat[:, None] == expert_ids[None, :]
    counts = jnp.sum(is_e.astype(jnp.int32), axis=0)
    order = jnp.argsort(e_flat, stable=True).astype(jnp.int32)
    pos = jnp.argsort(order).astype(jnp.int32)
    pcounts = ((counts + EXPERT_ROWS - 1) // EXPERT_ROWS) * EXPERT_ROWS
    pends = jnp.cumsum(pcounts)
    pstarts = pends - pcounts
    cstarts = jnp.cumsum(counts) - counts
    dest = jnp.sum(jnp.where(is_e, (pstarts - cstarts)[None, :], 0), axis=1) + pos
    nblk = -(-tk // EXPERT_ROWS) + N_EXPERTS
    blk_start = jnp.arange(nblk, dtype=jnp.int32) * EXPERT_ROWS
    blk_e = jnp.minimum(jnp.sum((pends[None, :] <= blk_start[:, None]).astype(jnp.int32), axis=1), N_EXPERTS - 1)
    blk_used = (blk_start < pends[-1]).astype(jnp.int32)
    tok = jnp.arange(tk, dtype=jnp.int32) // TOPK_EXPERTS
    slot_tok = _rows_by_sort(dest, tok, counts, pcounts, EXPERT_ROWS, nblk * EXPERT_ROWS)
    xs = _gather_rows(h2, slot_tok)
    ys = _moe_experts(xs, blk_e, blk_used, w_gu, b_gu, w_down, b_down)
    picked = _gather_rows(ys, dest).reshape(t, TOPK_EXPERTS, d)
    return x1 + jnp.sum(picked * weights[:, :, None], axis=1)


def kernel(x, mem, g_mix, w_in, lam_re, lam_im, log_dt, b_re, b_im, c_re, c_im, d_skip, w_glu, g_q, g_k, w_moba_out, g_mem, w_kv_mem, g_cq, g_ck, w_cross_out, w_out, g_ffn, w_router, b_router, w_gu, b_gu, w_down, b_down):
    bsz, seq_len, d = x.shape
    xt = x.reshape(bsz * seq_len, d)
    for l in range(g_mix.shape[0]):
        u, q2, k2, v2, xqn, gates, qh = _in_proj(xt, g_mix[l], w_in[l], g_q[l], g_k[l], g_cq[l], seq_len)
        mats = _s5_matrices(lam_re[l], lam_im[l], log_dt[l], b_re[l], b_im[l], c_re[l], c_im[l],
                            seq_len // S5_CHUNK)
        ys = _s5(u, mats, bsz, seq_len)
        om = _moba(q2, qh, k2, v2, bsz, seq_len)
        kc, vc = _mem_kv(mem, g_mem[l], w_kv_mem[l], g_ck[l])
        x1, h2, logits = _merge(xt, ys, u, d_skip[l], om, xqn, kc, vc, gates, w_glu[l], w_moba_out[l],
                                w_cross_out[l], w_out[l], g_ffn[l], w_router[l], b_router[l], seq_len)
        xt = _moe_ffn(x1, h2, logits, w_gu[l], b_gu[l], w_down[l], b_down[l])
    return xt.reshape(bsz, seq_len, d)
```

```python
import functools
import math

import jax
import jax.numpy as jnp
from jax import lax
from jax.experimental import pallas as pl
from jax.experimental.pallas import tpu as pltpu

F32 = jnp.float32
BF16 = jnp.bfloat16

EPS = 1e-6
N_BRANCH = 3
SSM_GROUP = 16
SSM_STATE = 64
S5_CHUNK = 16
MOBA_HEAD_DIM = 64
MOBA_BLOCK = 256
MOBA_TOPK = 3
MOBA_MAX_BLOCKS = 64
ROPE_THETA = 10000.0
X_HEADS = 4
N_EXPERTS = 32
TOPK_EXPERTS = 4
SWIGLU_LIMIT = 7.0
SWIGLU_ALPHA = 1.702
EXPERT_ROWS = 512
NEG_BIG = -1e30
LANES = 128
VMEM_LIMIT_BYTES = 56 * 1024 * 1024


def _params(*sem):
    return pltpu.CompilerParams(dimension_semantics=sem, vmem_limit_bytes=VMEM_LIMIT_BYTES)


def _sigmoid(x):
    return 1.0 / (1.0 + jnp.exp(-x))


def _dot(a, b):
    return jnp.dot(a, b, preferred_element_type=F32)


def _dot_nt(a, b):
    return lax.dot_general(a, b, (((1,), (1,)), ((), ())), preferred_element_type=F32)


def _inproj_kernel(x_ref, gmix_ref, wa_ref, wg_ref, e64_ref, gq_ref, gk_ref, gcq_ref, cos_ref, sin_ref,
                   u_ref, q_ref, k_ref, v_ref, xq_ref, g_ref, qh_ref):
    xf = x_ref[...]
    ms = jnp.mean(xf * xf, axis=-1, keepdims=True)
    h = (xf * lax.rsqrt(ms + EPS) * gmix_ref[...]).astype(BF16)
    a = _dot(h, wa_ref[...])
    w = u_ref.shape[1]
    u_ref[...] = a[:, :w].astype(BF16)
    v_ref[...] = a[:, 3 * w:4 * w].astype(BF16)

    cos = jnp.tile(cos_ref[...], (1, w // LANES))
    sin = jnp.tile(sin_ref[...], (1, w // LANES))
    lane = lax.broadcasted_iota(jnp.int32, (xf.shape[0], w), 1)
    first_half = (lane % MOBA_HEAD_DIM) < (MOBA_HEAD_DIM // 2)

    def qk_norm_rope(raw, g):
        ss = _dot((raw * raw).astype(BF16), e64_ref[...])
        n = raw * lax.rsqrt(ss * (1.0 / MOBA_HEAD_DIM) + EPS) * g
        rot = jnp.where(first_half,
                        pltpu.roll(n, w - MOBA_HEAD_DIM // 2, 1),
                        pltpu.roll(n, MOBA_HEAD_DIM // 2, 1))
        return n * cos + rot * sin

    q = qk_norm_rope(a[:, w:2 * w], gq_ref[...]) * (MOBA_HEAD_DIM ** -0.5)
    q_ref[...] = q.astype(BF16)
    in_a = lax.broadcasted_iota(jnp.int32, (xf.shape[0], LANES), 1) < MOBA_HEAD_DIM
    for p in range(w // LANES):
        pair = q[:, p * LANES:(p + 1) * LANES].astype(BF16).astype(F32)
        qh_ref[2 * p] = jnp.where(in_a, pair, 0.0)
        qh_ref[2 * p + 1] = jnp.where(in_a, 0.0, pair)
    k_ref[...] = qk_norm_rope(a[:, 2 * w:3 * w], gk_ref[...]).astype(BF16)

    xq = a[:, 4 * w:5 * w]
    hd = w // X_HEADS
    for c in range(X_HEADS):
        chunk = xq[:, c * hd:(c + 1) * hd]
        cms = jnp.mean(chunk * chunk, axis=-1, keepdims=True)
        xq_ref[:, c * hd:(c + 1) * hd] = (chunk * lax.rsqrt(cms + EPS) * gcq_ref[...]).astype(BF16)

    d = xf.shape[1]
    for c in range(N_BRANCH):
        z = _dot(h, wg_ref[:, c * d:(c + 1) * d])
        g_ref[:, c * d:(c + 1) * d] = _sigmoid(z).astype(BF16)


def _in_proj(xt, g_mix, w_in, g_q, g_k, g_cq, seq_len, tm=256):
    t, d = xt.shape
    w = d // 2
    wa = w_in[:, :5 * w].astype(BF16)
    wg = w_in[:, 5 * w:].astype(BF16)
    heads = w // MOBA_HEAD_DIM
    e64 = jnp.kron(jnp.eye(heads, dtype=F32), jnp.ones((MOBA_HEAD_DIM, MOBA_HEAD_DIM), F32)).astype(BF16)
    half = MOBA_HEAD_DIM // 2
    inv = ROPE_THETA ** (-jnp.arange(half, dtype=F32) / half)
    ang = jnp.arange(seq_len, dtype=F32)[:, None] * inv[None, :]
    cos = jnp.tile(jnp.cos(ang), (1, LANES // half))
    sin = jnp.tile(jnp.concatenate([-jnp.sin(ang), jnp.sin(ang)], axis=1), (1, LANES // MOBA_HEAD_DIM))
    nt = seq_len // tm
    row = lambda i: (i, 0)
    const = lambda i: (0, 0)
    out_w = jax.ShapeDtypeStruct((t, w), BF16)
    return pl.pallas_call(
        _inproj_kernel,
        grid=(t // tm,),
        in_specs=[
            pl.BlockSpec((tm, d), row),
            pl.BlockSpec((1, d), const),
            pl.BlockSpec((d, 5 * w), const),
            pl.BlockSpec((d, N_BRANCH * d), const),
            pl.BlockSpec((w, w), const),
            pl.BlockSpec((1, w), const),
            pl.BlockSpec((1, w), const),
            pl.BlockSpec((1, w // X_HEADS), const),
            pl.BlockSpec((tm, LANES), lambda i: (i % nt, 0)),
            pl.BlockSpec((tm, LANES), lambda i: (i % nt, 0)),
        ],
        out_specs=[pl.BlockSpec((tm, w), row)] * 5 + [pl.BlockSpec((tm, N_BRANCH * d), row),
                                                      pl.BlockSpec((heads, tm, LANES), lambda i: (0, i, 0))],
        out_shape=[out_w] * 5 + [jax.ShapeDtypeStruct((t, N_BRANCH * d), BF16),
                                 jax.ShapeDtypeStruct((heads, t, LANES), F32)],
        compiler_params=_params("parallel"),
        name="in_proj",
    )(xt, g_mix.reshape(1, d), wa, wg, e64,
      jnp.tile(g_q, heads).reshape(1, w), jnp.tile(g_k, heads).reshape(1, w), g_cq.reshape(1, -1), cos, sin)


def _s5_matrices(lam_re, lam_im, log_dt, b_re, b_im, c_re, c_im, n_chunks):
    hp = lax.Precision.HIGHEST
    c = S5_CHUNK
    dt = jnp.exp(log_dt)[:, None]
    mag = jnp.exp(lam_re * dt)
    ar = mag * jnp.cos(lam_im * dt)
    ai = mag * jnp.sin(lam_im * dt)
    nr = ar - 1.0
    den = lam_re * lam_re + lam_im * lam_im
    cr = (nr * lam_re + ai * lam_im) / den
    ci = (ai * lam_re - nr * lam_im) / den
    bbr = cr[..., None] * b_re - ci[..., None] * b_im
    bbi = cr[..., None] * b_im + ci[..., None] * b_re

    def power(n):
        nf = n.astype(F32)[None, :, None]
        m = jnp.exp((lam_re * dt)[:, None, :] * nf)
        th = (lam_im * dt)[:, None, :] * nf
        return m * jnp.cos(th), m * jnp.sin(th)

    pr, pi = power(jnp.arange(c + 1))
    kbr = pr[..., None] * bbr[:, None] - pi[..., None] * bbi[:, None]
    kbi = pr[..., None] * bbi[:, None] + pi[..., None] * bbr[:, None]
    kk = (jnp.einsum('ghp,gtpc->gthc', c_re, kbr, precision=hp)
          - jnp.einsum('ghp,gtpc->gthc', c_im, kbi, precision=hp))
    tq = jnp.arange(c)
    lag = tq[None, :] - tq[:, None]
    toep = kk[:, jnp.clip(lag, 0, c)]
    toep = jnp.where((lag >= 0)[None, :, :, None, None], toep, 0.0)
    g = toep.shape[0]
    toep = toep.transpose(0, 1, 4, 2, 3).reshape(g, c * SSM_GROUP, c * SSM_GROUP)
    rev = c - 1 - tq
    w_in = jnp.concatenate([kbr[:, rev].transpose(0, 1, 3, 2), kbi[:, rev].transpose(0, 1, 3, 2)], axis=-1)
    w_in = w_in.reshape(g, c * SSM_GROUP, 2 * SSM_STATE)
    prn, pin = pr[:, 1:], pi[:, 1:]
    wo_r = c_re[:, None] * prn[:, :, None, :] - c_im[:, None] * pin[:, :, None, :]
    wo_i = -c_re[:, None] * pin[:, :, None, :] - c_im[:, None] * prn[:, :, None, :]
    w_out = jnp.concatenate([wo_r, wo_i], axis=-1).transpose(0, 3, 1, 2).reshape(g, 2 * SSM_STATE, c * SSM_GROUP)
    n_steps = max(1, int(math.ceil(math.log2(n_chunks))))
    qr, qi = power(c * (2 ** jnp.arange(n_steps)))
    pa = jnp.concatenate([qr, qr], axis=-1)
    pb = jnp.concatenate([-qi, qi], axis=-1)
    return toep.astype(BF16), w_in.astype(BF16), w_out.astype(BF16), pa, pb


def _s5_kernel(u_ref, toep_ref, win_ref, wout_ref, pa_ref, pb_ref, y_ref):
    u = u_ref[0, 0]
    s = _dot(u, win_ref[0])
    nc = s.shape[0]
    row = lax.broadcasted_iota(jnp.int32, s.shape, 0)
    n_steps = pa_ref.shape[1]
    for k in range(n_steps):
        sh = 1 << k
        if sh >= nc:
            break
        prev = jnp.where(row >= sh, pltpu.roll(s, sh, 0), 0.0)
        s = s + pa_ref[0, k:k + 1, :] * prev + pb_ref[0, k:k + 1, :] * pltpu.roll(prev, SSM_STATE, 1)
    s_start = jnp.where(row >= 1, pltpu.roll(s, 1, 0), 0.0)
    y = _dot(u, toep_ref[0]) + _dot(s_start.astype(BF16), wout_ref[0])
    y_ref[0, 0] = y.astype(BF16)


def _s5(u, mats, bsz, seq_len):
    toep, w_in, w_out, pa, pb = mats
    g = toep.shape[0]
    nc = seq_len // S5_CHUNK
    cw = S5_CHUNK * SSM_GROUP
    ug = u.reshape(bsz, nc, S5_CHUNK, g, SSM_GROUP).transpose(0, 3, 1, 2, 4).reshape(bsz, g, nc, cw)
    per_g = lambda b, j: (j, 0, 0)
    y = pl.pallas_call(
        _s5_kernel,
        grid=(bsz, g),
        in_specs=[
            pl.BlockSpec((1, 1, nc, cw), lambda b, j: (b, j, 0, 0)),
            pl.BlockSpec((1, cw, cw), per_g),
            pl.BlockSpec((1, cw, 2 * SSM_STATE), per_g),
            pl.BlockSpec((1, 2 * SSM_STATE, cw), per_g),
            pl.BlockSpec((1,) + pa.shape[1:], per_g),
            pl.BlockSpec((1,) + pb.shape[1:], per_g),
        ],
        out_specs=pl.BlockSpec((1, 1, nc, cw), lambda b, j: (b, j, 0, 0)),
        out_shape=jax.ShapeDtypeStruct((bsz, g, nc, cw), BF16),
        compiler_params=_params("parallel", "parallel"),
        name="s5_scan",
    )(ug, toep, w_in, w_out, pa, pb)
    return y.reshape(bsz, g, nc, S5_CHUNK, SSM_GROUP).transpose(0, 2, 3, 1, 4).reshape(bsz * seq_len, g * SSM_GROUP)


def _moba_select_kernel(q_ref, k_ref, tri_ref, sel_ref, cnt_out_ref, km_ref, cnt_ref):
    h = pl.program_id(1)
    t = pl.program_id(2)
    ts = q_ref.shape[0]
    nbk = MOBA_MAX_BLOCKS

    @pl.when(t == 0)
    def _():
        kk = k_ref[...].astype(F32)
        nb = kk.shape[0] // MOBA_BLOCK
        km = jnp.sum(kk.reshape(nb, MOBA_BLOCK, LANES), axis=1) * (1.0 / MOBA_BLOCK)
        if nb < nbk:
            km = jnp.concatenate([km, jnp.zeros((nbk - nb, LANES), F32)], axis=0)
        lane = lax.broadcasted_iota(jnp.int32, (nbk, LANES), 1)
        km_ref[...] = jnp.where((lane // MOBA_HEAD_DIM) == (h % 2), km, 0.0).astype(BF16)
        cnt_ref[...] = jnp.zeros(cnt_ref.shape, F32)

    gate = _dot_nt(km_ref[...], q_ref[...])
    blk = lax.broadcasted_iota(jnp.int32, gate.shape, 0)
    qblk = (t * ts + lax.broadcasted_iota(jnp.int32, gate.shape, 1)) // MOBA_BLOCK
    g = jnp.where(blk < qblk, gate, -jnp.inf)
    selected = jnp.zeros(gate.shape, jnp.bool_)
    picks = []
    for _ in range(MOBA_TOPK):
        mx = jnp.max(g, axis=0, keepdims=True)
        idx = jnp.min(jnp.where(g == mx, blk, nbk), axis=0, keepdims=True)
        hit = blk == idx
        ok = (idx[0:1] < qblk[0:1]) & (mx > -jnp.inf)
        picks.append((hit, idx, ok))
        selected = selected | (hit & ok)
        g = jnp.where(hit, -jnp.inf, g)
    sel01 = jnp.where(selected, 1.0, 0.0).astype(BF16)
    prefix = _dot(sel01, tri_ref[...]) + jnp.tile(cnt_ref[...], (1, ts // LANES))
    rows = [jnp.where(ok, idx, -1) for (_, idx, ok) in picks]
    rows += [jnp.sum(jnp.where(hit, prefix, 0.0), axis=0, keepdims=True).astype(jnp.int32) for (hit, _, _) in picks]
    rows += [jnp.zeros((1, ts), jnp.int32)] * (8 - 2 * MOBA_TOPK)
    sel_ref[0, 0] = jnp.concatenate(rows, axis=0)
    cnt_ref[...] = cnt_ref[...] + _dot(sel01, jnp.ones((ts, LANES), BF16))
    cnt_out_ref[0, 0] = cnt_ref[...]


def _moba_select(q2, k2, bsz, seq_len, ts=512):
    ts = min(ts, seq_len)
    heads = q2.shape[1] // MOBA_HEAD_DIM
    nt = seq_len // ts
    tri = (jnp.arange(ts)[:, None] < jnp.arange(ts)[None, :]).astype(BF16)
    return pl.pallas_call(
        _moba_select_kernel,
        grid=(bsz, heads, nt),
        in_specs=[
            pl.BlockSpec((ts, LANES), lambda b, h, t: (b * nt + t, h // 2)),
            pl.BlockSpec((seq_len, LANES), lambda b, h, t: (b, h // 2)),
            pl.BlockSpec((ts, ts), lambda b, h, t: (0, 0)),
        ],
        out_specs=[pl.BlockSpec((1, 1, 8, ts), lambda b, h, t: (b, h, 0, t)),
                   pl.BlockSpec((1, 1, MOBA_MAX_BLOCKS, LANES), lambda b, h, t: (b, h, 0, 0))],
        out_shape=[jax.ShapeDtypeStruct((bsz, heads, 8, seq_len), jnp.int32),
                   jax.ShapeDtypeStruct((bsz, heads, MOBA_MAX_BLOCKS, LANES), F32)],
        scratch_shapes=[pltpu.VMEM((MOBA_MAX_BLOCKS, LANES), BF16), pltpu.VMEM((MOBA_MAX_BLOCKS, LANES), F32)],
        compiler_params=_params("parallel", "arbitrary", "arbitrary"),
        name="moba_select",
    )(q2, k2, tri)


def _softmax_block(q, kb, vb, causal):
    s = _dot_nt(q, kb)
    if causal:
        r = lax.broadcasted_iota(jnp.int32, s.shape, 0) % MOBA_BLOCK
        c = lax.broadcasted_iota(jnp.int32, s.shape, 1)
        s = jnp.where(c <= r, s, NEG_BIG)
    m = jnp.max(s, axis=1, keepdims=True)
    p = jnp.exp(s - m).astype(BF16)
    pv = _dot(p, jnp.concatenate([vb, jnp.ones(vb.shape, BF16)], axis=1))
    l = pv[:, LANES:LANES + 1]
    return pv[:, :LANES] / l, m + jnp.log(l)


def _moba_diag_kernel(q_ref, k_ref, v_ref, o_ref, lse_ref):
    q = q_ref[...]
    lane = lax.broadcasted_iota(jnp.int32, q.shape, 1)
    is_a = lane < MOBA_HEAD_DIM
    zero = jnp.zeros_like(q)
    qs = jnp.concatenate([jnp.where(is_a, q, zero), jnp.where(is_a, zero, q)], axis=0)
    o, lse = _softmax_block(qs, k_ref[...], v_ref[...], True)
    n = q.shape[0]
    o_ref[...] = jnp.where(is_a, o[:n], o[n:])
    lse_ref[...] = jnp.where(is_a, lse[:n], lse[n:])


def _moba_diag(q2, k2, v2, bsz, seq_len):
    npair = q2.shape[1] // LANES
    nb = seq_len // MOBA_BLOCK
    spec = pl.BlockSpec((MOBA_BLOCK, LANES), lambda b, p, i: (b * nb + i, p))
    out = jax.ShapeDtypeStruct(q2.shape, F32)
    return pl.pallas_call(
        _moba_diag_kernel,
        grid=(bsz, npair, nb),
        in_specs=[spec, spec, spec],
        out_specs=[spec, spec],
        out_shape=[out, out],
        compiler_params=_params("parallel", "parallel", "parallel"),
        name="moba_diag",
    )(q2, k2, v2)


MOBA_TILE = 256
MOBA_TILES_PER_STEP = 4


def _moba_grouped_kernel(tile_g_ref, tile_real_ref, qd_ref, *refs):
    n = MOBA_TILES_PER_STEP
    k_refs, v_refs, o_ref = refs[:n], refs[n:2 * n], refs[2 * n]
    i = pl.program_id(0)
    lane = lax.broadcasted_iota(jnp.int32, (MOBA_TILE, LANES), 1)
    for u in range(n):
        tile = i * n + u
        rows = pl.ds(u * MOBA_TILE, MOBA_TILE)
        head_half = (tile_g_ref[tile] // MOBA_MAX_BLOCKS) % 2
        o, lse = _softmax_block(qd_ref[rows, :].astype(BF16), k_refs[u][...], v_refs[u][...], False)
        part = jnp.where((lane // MOBA_HEAD_DIM) == head_half, o, lse)
        o_ref[rows, :] = jnp.where(tile_real_ref[tile] > 0, part, NEG_BIG)


def _moba_grouped(qd, k2, v2, tile_g, tile_real, seq_len):
    n = MOBA_TILES_PER_STEP
    n_tiles = qd.shape[0] // MOBA_TILE
    nb = seq_len // MOBA_BLOCK
    heads = k2.shape[1] // MOBA_HEAD_DIM

    def kv_spec(u):
        def index(i, tg, tr):
            g = tg[i * n + u]
            b = g // (heads * MOBA_MAX_BLOCKS)
            h = (g // MOBA_MAX_BLOCKS) % heads
            return (b * nb + jnp.minimum(g % MOBA_MAX_BLOCKS, nb - 1), h // 2)
        return pl.BlockSpec((MOBA_BLOCK, LANES), index)

    grid_spec = pltpu.PrefetchScalarGridSpec(
        num_scalar_prefetch=2,
        grid=(n_tiles // n,),
        in_specs=[pl.BlockSpec((n * MOBA_TILE, LANES), lambda i, tg, tr: (i, 0))]
        + [kv_spec(u) for u in range(n)] + [kv_spec(u) for u in range(n)],
        out_specs=pl.BlockSpec((n * MOBA_TILE, LANES), lambda i, tg, tr: (i, 0)),
    )
    return pl.pallas_call(
        _moba_grouped_kernel,
        grid_spec=grid_spec,
        out_shape=jax.ShapeDtypeStruct(qd.shape, F32),
        compiler_params=_params("arbitrary"),
        name="moba_grouped",
    )(tile_g, tile_real, qd, *([k2] * n), *([v2] * n))


def _moba_combine_kernel(od_ref, lsed_ref, *refs):
    g_refs, o_ref = refs[:-1], refs[-1]
    lane = lax.broadcasted_iota(jnp.int32, od_ref.shape, 1)
    is_a = lane < MOBA_HEAD_DIM
    parts = [(od_ref[...], lsed_ref[...])]
    for s in range(MOBA_TOPK):
        xa = g_refs[s][0, 0, 0]
        xb = g_refs[MOBA_TOPK + s][0, 0, 0]
        o = jnp.where(is_a, xa, xb)
        lse = pltpu.roll(jnp.where(is_a, xb, xa), MOBA_HEAD_DIM, 1)
        parts.append((o, lse))
    m = parts[0][1]
    for _, lse in parts[1:]:
        m = jnp.maximum(m, lse)
    num = jnp.zeros(od_ref.shape, F32)
    den = jnp.zeros(od_ref.shape, F32)
    for o, lse in parts:
        w = jnp.exp(lse - m)
        num = num + w * o
        den = den + w
    o_ref[...] = (num / den).astype(BF16)


def _moba_combine(od, lsed, gath, bsz, seq_len, tm=256):
    npair = od.shape[1] // LANES
    nt = seq_len // tm
    spec = pl.BlockSpec((tm, LANES), lambda b, p, i: (b * nt + i, p))

    def g_spec(e, s):
        return pl.BlockSpec((1, 1, 1, tm, LANES), lambda b, p, i: (b, 2 * p + e, s, i, 0))

    return pl.pallas_call(
        _moba_combine_kernel,
        grid=(bsz, npair, nt),
        in_specs=[spec, spec] + [g_spec(e, s) for e in range(2) for s in range(MOBA_TOPK)],
        out_specs=spec,
        out_shape=jax.ShapeDtypeStruct(od.shape, BF16),
        compiler_params=_params("parallel", "parallel", "parallel"),
        name="moba_combine",
    )(od, lsed, *([gath] * (2 * MOBA_TOPK)))


def _gather_rows(table, idx):
    return table.at[idx].get(mode='promise_in_bounds')


SORT_LAST = jnp.iinfo(jnp.int32).max


def _rows_by_sort(dest, src, counts, pcounts, tile, n_out):
    fill = counts[:, None] + jnp.arange(tile, dtype=jnp.int32)[None, :]
    first = (jnp.cumsum(pcounts) - pcounts)[:, None]
    fill_key = jnp.where(fill < pcounts[:, None], first + fill, SORT_LAST).reshape(-1)
    keys = jnp.concatenate([dest, fill_key])
    vals = jnp.concatenate([src, jnp.zeros(fill_key.shape, jnp.int32)])
    _, by_row = lax.sort((keys, vals), num_keys=1)
    return jnp.concatenate([by_row, jnp.zeros((n_out - by_row.shape[0],), jnp.int32)])


def _moba(q2, qh, k2, v2, bsz, seq_len):
    heads = q2.shape[1] // MOBA_HEAD_DIM
    nbk = MOBA_MAX_BLOCKS
    sel, cnt = _moba_select(q2, k2, bsz, seq_len)
    od, lsed = _moba_diag(q2, k2, v2, bsz, seq_len)
    counts = cnt[..., 0].astype(jnp.int32).reshape(bsz * heads * nbk)
    pcounts = ((counts + MOBA_TILE - 1) // MOBA_TILE) * MOBA_TILE
    pends = jnp.cumsum(pcounts)
    pstarts = (pends - pcounts).reshape(bsz, heads, 1, 1, nbk)
    n_items = bsz * heads * seq_len * MOBA_TOPK
    step_rows = MOBA_TILE * MOBA_TILES_PER_STEP
    n_rows = -(-(n_items + bsz * heads * nbk * MOBA_TILE) // step_rows) * step_rows
    n_null = step_rows
    idx = sel[:, :, 0:MOBA_TOPK, :]
    rank = sel[:, :, MOBA_TOPK:2 * MOBA_TOPK, :]
    start = jnp.sum(jnp.where(idx[..., None] == jnp.arange(nbk), pstarts, 0), axis=-1)
    null_row = n_rows + jnp.arange(seq_len, dtype=jnp.int32) % n_null
    dest = jnp.where(idx >= 0, start + rank, null_row)
    n_tiles = (n_rows + n_null) // MOBA_TILE
    n_groups = bsz * heads * nbk
    tile_start = jnp.arange(n_tiles, dtype=jnp.int32) * MOBA_TILE
    tile_g = jnp.minimum(jnp.sum((pends[None, :] <= tile_start[:, None]).astype(jnp.int32), axis=1), n_groups - 1)
    tile_real = (tile_start < pends[-1]).astype(jnp.int32)
    src_row = (jnp.arange(heads, dtype=jnp.int32).reshape(1, heads, 1, 1) * (bsz * seq_len)
               + jnp.arange(bsz, dtype=jnp.int32).reshape(bsz, 1, 1, 1) * seq_len
               + jnp.arange(seq_len, dtype=jnp.int32))
    src_row = jnp.broadcast_to(src_row, idx.shape).reshape(-1)
    sort_key = jnp.where(idx >= 0, dest, SORT_LAST).reshape(-1)
    slot_src = _rows_by_sort(sort_key, src_row, counts, pcounts, MOBA_TILE, n_rows + n_null)
    qd = _gather_rows(qh.reshape(-1, LANES), slot_src)
    part = _moba_grouped(qd, k2, v2, tile_g, tile_real, seq_len)
    gath = _gather_rows(part, dest.reshape(-1)).reshape(bsz, heads, MOBA_TOPK, seq_len, LANES)
    return _moba_combine(od, lsed, gath, bsz, seq_len)


def _mem_kv_kernel(mem_ref, gmem_ref, w_ref, gck_ref, k_ref, v_ref):
    xf = mem_ref[0]
    ms = jnp.mean(xf * xf, axis=-1, keepdims=True)
    h = (xf * lax.rsqrt(ms + EPS) * gmem_ref[...]).astype(BF16)
    kv = _dot(h, w_ref[...])
    w = k_ref.shape[2]
    hd = w // X_HEADS
    for c in range(X_HEADS):
        chunk = kv[:, c * hd:(c + 1) * hd]
        cms = jnp.mean(chunk * chunk, axis=-1, keepdims=True)
        k_ref[0, :, c * hd:(c + 1) * hd] = (chunk * lax.rsqrt(cms + EPS) * gck_ref[...]).astype(BF16)
    v_ref[0] = kv[:, w:].astype(BF16)


def _mem_kv(mem, g_mem, w_kv_mem, g_ck):
    bsz, m, d = mem.shape
    w = w_kv_mem.shape[1] // 2
    const = lambda b: (0, 0)
    out = jax.ShapeDtypeStruct((bsz, m, w), BF16)
    return pl.pallas_call(
        _mem_kv_kernel,
        grid=(bsz,),
        in_specs=[pl.BlockSpec((1, m, d), lambda b: (b, 0, 0)), pl.BlockSpec((1, d), const),
                  pl.BlockSpec((d, 2 * w), const), pl.BlockSpec((1, w // X_HEADS), const)],
        out_specs=[pl.BlockSpec((1, m, w), lambda b: (b, 0, 0))] * 2,
        out_shape=[out, out],
        compiler_params=_params("parallel"),
        name="mem_kv",
    )(mem, g_mem.reshape(1, d), w_kv_mem.astype(BF16), g_ck.reshape(1, -1))


def _merge_kernel(x_ref, ys_ref, u_ref, dskip_ref, om_ref, xq_ref, kc_ref, vc_ref, g_ref,
                  wglu_ref, wmo_ref, wco_ref, wout_ref, gffn_ref, wr_ref, br_ref,
                  x1_ref, h2_ref, logit_ref):
    d = x_ref.shape[1]
    y = ys_ref[...].astype(F32) + dskip_ref[...] * u_ref[...].astype(F32)
    ge = 0.5 * y * (1.0 + jnp.tanh(math.sqrt(2.0 / math.pi) * (y + 0.044715 * (y * y * y))))
    z = _dot(ge.astype(BF16), wglu_ref[...])
    merged = g_ref[:, 0:d].astype(F32) * (z[:, :d] * _sigmoid(z[:, d:]))
    merged = merged + g_ref[:, d:2 * d].astype(F32) * _dot(om_ref[...], wmo_ref[...])
    w = xq_ref.shape[1]
    hd = w // X_HEADS
    heads = []
    for c in range(X_HEADS):
        s = _dot_nt(xq_ref[:, c * hd:(c + 1) * hd], kc_ref[0, :, c * hd:(c + 1) * hd]) * (hd ** -0.5)
        p = jnp.exp(s - jnp.max(s, axis=1, keepdims=True))
        p = p / jnp.sum(p, axis=1, keepdims=True)
        heads.append(_dot(p.astype(BF16), vc_ref[0, :, c * hd:(c + 1) * hd]))
    oc = jnp.concatenate(heads, axis=1).astype(BF16)
    merged = merged + g_ref[:, 2 * d:3 * d].astype(F32) * _dot(oc, wco_ref[...])
    x1 = x_ref[...] + _dot(merged.astype(BF16), wout_ref[...])
    x1_ref[...] = x1
    ms = jnp.mean(x1 * x1, axis=-1, keepdims=True)
    h2 = (x1 * lax.rsqrt(ms + EPS) * gffn_ref[...]).astype(BF16)
    h2_ref[...] = h2.astype(F32)
    logit_ref[...] = _dot(h2, wr_ref[...]) + br_ref[...]


def _merge(xt, ys, u, d_skip, om, xqn, kc, vc, gates, w_glu, w_mo, w_co, w_out, g_ffn, w_router, b_router,
           seq_len, tm=256):
    t, d = xt.shape
    w = d // 2
    m = kc.shape[1]
    ne = w_router.shape[1]
    nt = seq_len // tm
    row = lambda i: (i, 0)
    const = lambda i: (0, 0)
    per_b = lambda i: (i // nt, 0, 0)
    return pl.pallas_call(
        _merge_kernel,
        grid=(t // tm,),
        in_specs=[
            pl.BlockSpec((tm, d), row), pl.BlockSpec((tm, w), row), pl.BlockSpec((tm, w), row),
            pl.BlockSpec((1, w), const), pl.BlockSpec((tm, w), row), pl.BlockSpec((tm, w), row),
            pl.BlockSpec((1, m, w), per_b), pl.BlockSpec((1, m, w), per_b),
            pl.BlockSpec((tm, N_BRANCH * d), row),
            pl.BlockSpec((w, 2 * d), const), pl.BlockSpec((w, d), const), pl.BlockSpec((w, d), const),
            pl.BlockSpec((d, d), const), pl.BlockSpec((1, d), const),
            pl.BlockSpec((d, ne), const), pl.BlockSpec((1, ne), const),
        ],
        out_specs=[pl.BlockSpec((tm, d), row), pl.BlockSpec((tm, d), row), pl.BlockSpec((tm, ne), row)],
        out_shape=[jax.ShapeDtypeStruct((t, d), F32), jax.ShapeDtypeStruct((t, d), F32),
                   jax.ShapeDtypeStruct((t, ne), F32)],
        compiler_params=_params("parallel"),
        name="merge",
    )(xt, ys, u, d_skip.reshape(1, w), om, xqn, kc, vc, gates,
      w_glu.astype(BF16), w_mo.astype(BF16), w_co.astype(BF16), w_out.astype(BF16),
      g_ffn.reshape(1, d), w_router.astype(BF16), b_router.reshape(1, ne))


def _moe_kernel(blk_e_ref, blk_used_ref, xs_ref, wgu_ref, bgu_ref, wd_ref, bd_ref, y_ref, wgu_bf, wd_bf):
    i = pl.program_id(0)
    prev = blk_e_ref[jnp.maximum(i - 1, 0)]

    @pl.when((i == 0) | (blk_e_ref[i] != prev))
    def _():
        wgu_bf[...] = wgu_ref[0].astype(BF16)
        wd_bf[...] = wd_ref[0].astype(BF16)

    @pl.when(blk_used_ref[i] > 0)
    def _():
        de = wd_bf.shape[0]
        gu = _dot(xs_ref[...].astype(BF16), wgu_bf[...]) + bgu_ref[0]
        gate = jnp.minimum(gu[:, :de], SWIGLU_LIMIT)
        up = jnp.clip(gu[:, de:], -SWIGLU_LIMIT, SWIGLU_LIMIT)
        act = gate * _sigmoid(SWIGLU_ALPHA * gate) * (up + 1.0)
        y_ref[...] = _dot(act.astype(BF16), wd_bf[...]) + bd_ref[0]

    @pl.when(blk_used_ref[i] == 0)
    def _():
        y_ref[...] = jnp.zeros(y_ref.shape, y_ref.dtype)


def _moe_experts(xs, blk_e, blk_used, w_gu, b_gu, w_down, b_down):
    p, d = xs.shape
    ne, _, de2 = w_gu.shape
    de = de2 // 2
    nblk = p // EXPERT_ROWS
    grid_spec = pltpu.PrefetchScalarGridSpec(
        num_scalar_prefetch=2,
        grid=(nblk,),
        in_specs=[
            pl.BlockSpec((EXPERT_ROWS, d), lambda i, e, n: (i, 0)),
            pl.BlockSpec((1, d, de2), lambda i, e, n: (e[i], 0, 0)),
            pl.BlockSpec((1, 1, de2), lambda i, e, n: (e[i], 0, 0)),
            pl.BlockSpec((1, de, d), lambda i, e, n: (e[i], 0, 0)),
            pl.BlockSpec((1, 1, d), lambda i, e, n: (e[i], 0, 0)),
        ],
        out_specs=pl.BlockSpec((EXPERT_ROWS, d), lambda i, e, n: (i, 0)),
        scratch_shapes=[pltpu.VMEM((d, de2), BF16), pltpu.VMEM((de, d), BF16)],
    )
    return pl.pallas_call(
        _moe_kernel,
        grid_spec=grid_spec,
        out_shape=jax.ShapeDtypeStruct((p, d), F32),
        compiler_params=_params("arbitrary"),
        name="moe_experts",
    )(blk_e, blk_used, xs, w_gu, b_gu.reshape(ne, 1, de2), w_down, b_down.reshape(ne, 1, d))


def _moe_ffn(x1, h2, logits, w_gu, b_gu, w_down, b_down):
    t, d = x1.shape
    tk = t * TOPK_EXPERTS
    top_v, top_e = lax.top_k(logits, TOPK_EXPERTS)
    weights = jax.nn.softmax(top_v, axis=-1)
    e_flat = top_e.reshape(tk)
    expert_ids = jnp.arange(N_EXPERTS, dtype=e_flat.dtype)
    is_e = e_flat[:, None] == expert_ids[None, :]
    counts = jnp.sum(is_e.astype(jnp.int32), axis=0)
    order = jnp.argsort(e_flat, stable=True).astype(jnp.int32)
    pos = jnp.argsort(order).astype(jnp.int32)
    pcounts = ((counts + EXPERT_ROWS - 1) // EXPERT_ROWS) * EXPERT_ROWS
    pends = jnp.cumsum(pcounts)
    pstarts = pends - pcounts
    cstarts = jnp.cumsum(counts) - counts
    dest = jnp.sum(jnp.where(is_e, (pstarts - cstarts)[None, :], 0), axis=1) + pos
    nblk = -(-tk // EXPERT_ROWS) + N_EXPERTS
    blk_start = jnp.arange(nblk, dtype=jnp.int32) * EXPERT_ROWS
    blk_e = jnp.minimum(jnp.sum((pends[None, :] <= blk_start[:, None]).astype(jnp.int32), axis=1), N_EXPERTS - 1)
    blk_used = (blk_start < pends[-1]).astype(jnp.int32)
    tok = jnp.arange(tk, dtype=jnp.int32) // TOPK_EXPERTS
    slot_tok = _rows_by_sort(dest, tok, counts, pcounts, EXPERT_ROWS, nblk * EXPERT_ROWS)
    xs = _gather_rows(h2, slot_tok)
    ys = _moe_experts(xs, blk_e, blk_used, w_gu, b_gu, w_down, b_down)
    picked = _gather_rows(ys, dest.reshape(t, TOPK_EXPERTS).T.reshape(-1)).reshape(TOPK_EXPERTS, t, d)
    out = x1
    for k in range(TOPK_EXPERTS):
        out = out + picked[k] * weights[:, k:k + 1]
    return out


def kernel(x, mem, g_mix, w_in, lam_re, lam_im, log_dt, b_re, b_im, c_re, c_im, d_skip, w_glu, g_q, g_k, w_moba_out, g_mem, w_kv_mem, g_cq, g_ck, w_cross_out, w_out, g_ffn, w_router, b_router, w_gu, b_gu, w_down, b_down):
    bsz, seq_len, d = x.shape
    xt = x.reshape(bsz * seq_len, d)
    for l in range(g_mix.shape[0]):
        u, q2, k2, v2, xqn, gates, qh = _in_proj(xt, g_mix[l], w_in[l], g_q[l], g_k[l], g_cq[l], seq_len)
        mats = _s5_matrices(lam_re[l], lam_im[l], log_dt[l], b_re[l], b_im[l], c_re[l], c_im[l],
                            seq_len // S5_CHUNK)
        ys = _s5(u, mats, bsz, seq_len)
        om = _moba(q2, qh, k2, v2, bsz, seq_len)
        kc, vc = _mem_kv(mem, g_mem[l], w_kv_mem[l], g_ck[l])
        x1, h2, logits = _merge(xt, ys, u, d_skip[l], om, xqn, kc, vc, gates, w_glu[l], w_moba_out[l],
                                w_cross_out[l], w_out[l], g_ffn[l], w_router[l], b_router[l], seq_len)
        xt = _moe_ffn(x1, h2, logits, w_gu[l], b_gu[l], w_down[l], b_down[l])
    return xt.reshape(bsz, seq_len, d)
```

```python
import functools
import math

import jax
import jax.numpy as jnp
from jax import lax
from jax.experimental import pallas as pl
from jax.experimental.pallas import tpu as pltpu

F32 = jnp.float32
BF16 = jnp.bfloat16

EPS = 1e-6
N_BRANCH = 3
SSM_GROUP = 16
SSM_STATE = 64
S5_CHUNK = 16
MOBA_HEAD_DIM = 64
MOBA_BLOCK = 256
MOBA_TOPK = 3
MOBA_MAX_BLOCKS = 64
ROPE_THETA = 10000.0
X_HEADS = 4
N_EXPERTS = 32
TOPK_EXPERTS = 4
SWIGLU_LIMIT = 7.0
SWIGLU_ALPHA = 1.702
EXPERT_ROWS = 512
NEG_BIG = -1e30
LANES = 128
VMEM_LIMIT_BYTES = 56 * 1024 * 1024


def _params(*sem):
    return pltpu.CompilerParams(dimension_semantics=sem, vmem_limit_bytes=VMEM_LIMIT_BYTES)


def _sigmoid(x):
    return 1.0 / (1.0 + jnp.exp(-x))


def _dot(a, b):
    return jnp.dot(a, b, preferred_element_type=F32)


def _dot_nt(a, b):
    return lax.dot_general(a, b, (((1,), (1,)), ((), ())), preferred_element_type=F32)


def _inproj_kernel(x_ref, gmix_ref, wa_ref, wg_ref, e64_ref, gq_ref, gk_ref, gcq_ref, cos_ref, sin_ref,
                   u_ref, q_ref, kv_ref, xq_ref, g_ref, qh_ref):
    xf = x_ref[...]
    ms = jnp.mean(xf * xf, axis=-1, keepdims=True)
    h = (xf * lax.rsqrt(ms + EPS) * gmix_ref[...]).astype(BF16)
    a = _dot(h, wa_ref[...])
    w = u_ref.shape[1]
    u_ref[...] = a[:, :w].astype(BF16)

    cos = jnp.tile(cos_ref[...], (1, w // LANES))
    sin = jnp.tile(sin_ref[...], (1, w // LANES))
    lane = lax.broadcasted_iota(jnp.int32, (xf.shape[0], w), 1)
    first_half = (lane % MOBA_HEAD_DIM) < (MOBA_HEAD_DIM // 2)

    def qk_norm_rope(raw, g):
        ss = _dot((raw * raw).astype(BF16), e64_ref[...])
        n = raw * lax.rsqrt(ss * (1.0 / MOBA_HEAD_DIM) + EPS) * g
        rot = jnp.where(first_half,
                        pltpu.roll(n, w - MOBA_HEAD_DIM // 2, 1),
                        pltpu.roll(n, MOBA_HEAD_DIM // 2, 1))
        return n * cos + rot * sin

    q = qk_norm_rope(a[:, w:2 * w], gq_ref[...]) * (MOBA_HEAD_DIM ** -0.5)
    q_ref[...] = q.astype(BF16)
    in_a = lax.broadcasted_iota(jnp.int32, (xf.shape[0], LANES), 1) < MOBA_HEAD_DIM
    for p in range(w // LANES):
        pair = q[:, p * LANES:(p + 1) * LANES].astype(BF16).astype(F32)
        qh_ref[2 * p] = jnp.where(in_a, pair, 0.0)
        qh_ref[2 * p + 1] = jnp.where(in_a, 0.0, pair)
    k = qk_norm_rope(a[:, 2 * w:3 * w], gk_ref[...]).astype(BF16)
    for p in range(w // LANES):
        kv_ref[:, (2 * p) * LANES:(2 * p + 1) * LANES] = k[:, p * LANES:(p + 1) * LANES]
        kv_ref[:, (2 * p + 1) * LANES:(2 * p + 2) * LANES] = a[:, 3 * w + p * LANES:3 * w + (p + 1) * LANES].astype(BF16)

    xq = a[:, 4 * w:5 * w]
    hd = w // X_HEADS
    for c in range(X_HEADS):
        chunk = xq[:, c * hd:(c + 1) * hd]
        cms = jnp.mean(chunk * chunk, axis=-1, keepdims=True)
        xq_ref[:, c * hd:(c + 1) * hd] = (chunk * lax.rsqrt(cms + EPS) * gcq_ref[...]).astype(BF16)

    d = xf.shape[1]
    for c in range(N_BRANCH):
        z = _dot(h, wg_ref[:, c * d:(c + 1) * d])
        g_ref[:, c * d:(c + 1) * d] = _sigmoid(z).astype(BF16)


def _in_proj(xt, g_mix, w_in, g_q, g_k, g_cq, seq_len, tm=256):
    t, d = xt.shape
    w = d // 2
    wa = w_in[:, :5 * w].astype(BF16)
    wg = w_in[:, 5 * w:].astype(BF16)
    heads = w // MOBA_HEAD_DIM
    e64 = jnp.kron(jnp.eye(heads, dtype=F32), jnp.ones((MOBA_HEAD_DIM, MOBA_HEAD_DIM), F32)).astype(BF16)
    half = MOBA_HEAD_DIM // 2
    inv = ROPE_THETA ** (-jnp.arange(half, dtype=F32) / half)
    ang = jnp.arange(seq_len, dtype=F32)[:, None] * inv[None, :]
    cos = jnp.tile(jnp.cos(ang), (1, LANES // half))
    sin = jnp.tile(jnp.concatenate([-jnp.sin(ang), jnp.sin(ang)], axis=1), (1, LANES // MOBA_HEAD_DIM))
    nt = seq_len // tm
    row = lambda i: (i, 0)
    const = lambda i: (0, 0)
    out_w = jax.ShapeDtypeStruct((t, w), BF16)
    return pl.pallas_call(
        _inproj_kernel,
        grid=(t // tm,),
        in_specs=[
            pl.BlockSpec((tm, d), row),
            pl.BlockSpec((1, d), const),
            pl.BlockSpec((d, 5 * w), const),
            pl.BlockSpec((d, N_BRANCH * d), const),
            pl.BlockSpec((w, w), const),
            pl.BlockSpec((1, w), const),
            pl.BlockSpec((1, w), const),
            pl.BlockSpec((1, w // X_HEADS), const),
            pl.BlockSpec((tm, LANES), lambda i: (i % nt, 0)),
            pl.BlockSpec((tm, LANES), lambda i: (i % nt, 0)),
        ],
        out_specs=[pl.BlockSpec((tm, w), row), pl.BlockSpec((tm, w), row), pl.BlockSpec((tm, 2 * w), row),
                   pl.BlockSpec((tm, w), row), pl.BlockSpec((tm, N_BRANCH * d), row),
                   pl.BlockSpec((heads, tm, LANES), lambda i: (0, i, 0))],
        out_shape=[out_w, out_w, jax.ShapeDtypeStruct((t, 2 * w), BF16), out_w,
                   jax.ShapeDtypeStruct((t, N_BRANCH * d), BF16),
                   jax.ShapeDtypeStruct((heads, t, LANES), F32)],
        compiler_params=_params("parallel"),
        name="in_proj",
    )(xt, g_mix.reshape(1, d), wa, wg, e64,
      jnp.tile(g_q, heads).reshape(1, w), jnp.tile(g_k, heads).reshape(1, w), g_cq.reshape(1, -1), cos, sin)


def _s5_matrices(lam_re, lam_im, log_dt, b_re, b_im, c_re, c_im, n_chunks):
    hp = lax.Precision.HIGHEST
    c = S5_CHUNK
    dt = jnp.exp(log_dt)[:, None]
    mag = jnp.exp(lam_re * dt)
    ar = mag * jnp.cos(lam_im * dt)
    ai = mag * jnp.sin(lam_im * dt)
    nr = ar - 1.0
    den = lam_re * lam_re + lam_im * lam_im
    cr = (nr * lam_re + ai * lam_im) / den
    ci = (ai * lam_re - nr * lam_im) / den
    bbr = cr[..., None] * b_re - ci[..., None] * b_im
    bbi = cr[..., None] * b_im + ci[..., None] * b_re

    def power(n):
        nf = n.astype(F32)[None, :, None]
        m = jnp.exp((lam_re * dt)[:, None, :] * nf)
        th = (lam_im * dt)[:, None, :] * nf
        return m * jnp.cos(th), m * jnp.sin(th)

    pr, pi = power(jnp.arange(c + 1))
    kbr = pr[..., None] * bbr[:, None] - pi[..., None] * bbi[:, None]
    kbi = pr[..., None] * bbi[:, None] + pi[..., None] * bbr[:, None]
    kk = (jnp.einsum('ghp,gtpc->gthc', c_re, kbr, precision=hp)
          - jnp.einsum('ghp,gtpc->gthc', c_im, kbi, precision=hp))
    tq = jnp.arange(c)
    lag = tq[None, :] - tq[:, None]
    toep = kk[:, jnp.clip(lag, 0, c)]
    toep = jnp.where((lag >= 0)[None, :, :, None, None], toep, 0.0)
    g = toep.shape[0]
    toep = toep.transpose(0, 1, 4, 2, 3).reshape(g, c * SSM_GROUP, c * SSM_GROUP)
    rev = c - 1 - tq
    w_in = jnp.concatenate([kbr[:, rev].transpose(0, 1, 3, 2), kbi[:, rev].transpose(0, 1, 3, 2)], axis=-1)
    w_in = w_in.reshape(g, c * SSM_GROUP, 2 * SSM_STATE)
    prn, pin = pr[:, 1:], pi[:, 1:]
    wo_r = c_re[:, None] * prn[:, :, None, :] - c_im[:, None] * pin[:, :, None, :]
    wo_i = -c_re[:, None] * pin[:, :, None, :] - c_im[:, None] * prn[:, :, None, :]
    w_out = jnp.concatenate([wo_r, wo_i], axis=-1).transpose(0, 3, 1, 2).reshape(g, 2 * SSM_STATE, c * SSM_GROUP)
    n_steps = max(1, int(math.ceil(math.log2(n_chunks))))
    qr, qi = power(c * (2 ** jnp.arange(n_steps)))
    pa = jnp.concatenate([qr, qr], axis=-1)
    pb = jnp.concatenate([-qi, qi], axis=-1)
    return toep.astype(BF16), w_in.astype(BF16), w_out.astype(BF16), pa, pb


def _s5_kernel(u_ref, toep_ref, win_ref, wout_ref, pa_ref, pb_ref, y_ref):
    u = u_ref[0, 0]
    s = _dot(u, win_ref[0])
    nc = s.shape[0]
    row = lax.broadcasted_iota(jnp.int32, s.shape, 0)
    n_steps = pa_ref.shape[1]
    for k in range(n_steps):
        sh = 1 << k
        if sh >= nc:
            break
        prev = jnp.where(row >= sh, pltpu.roll(s, sh, 0), 0.0)
        s = s + pa_ref[0, k:k + 1, :] * prev + pb_ref[0, k:k + 1, :] * pltpu.roll(prev, SSM_STATE, 1)
    s_start = jnp.where(row >= 1, pltpu.roll(s, 1, 0), 0.0)
    y = _dot(u, toep_ref[0]) + _dot(s_start.astype(BF16), wout_ref[0])
    y_ref[0, 0] = y.astype(BF16)


def _s5(u, mats, bsz, seq_len):
    toep, w_in, w_out, pa, pb = mats
    g = toep.shape[0]
    nc = seq_len // S5_CHUNK
    cw = S5_CHUNK * SSM_GROUP
    ug = u.reshape(bsz, nc, S5_CHUNK, g, SSM_GROUP).transpose(0, 3, 1, 2, 4).reshape(bsz, g, nc, cw)
    per_g = lambda b, j: (j, 0, 0)
    y = pl.pallas_call(
        _s5_kernel,
        grid=(bsz, g),
        in_specs=[
            pl.BlockSpec((1, 1, nc, cw), lambda b, j: (b, j, 0, 0)),
            pl.BlockSpec((1, cw, cw), per_g),
            pl.BlockSpec((1, cw, 2 * SSM_STATE), per_g),
            pl.BlockSpec((1, 2 * SSM_STATE, cw), per_g),
            pl.BlockSpec((1,) + pa.shape[1:], per_g),
            pl.BlockSpec((1,) + pb.shape[1:], per_g),
        ],
        out_specs=pl.BlockSpec((1, 1, nc, cw), lambda b, j: (b, j, 0, 0)),
        out_shape=jax.ShapeDtypeStruct((bsz, g, nc, cw), BF16),
        compiler_params=_params("parallel", "parallel"),
        name="s5_scan",
    )(ug, toep, w_in, w_out, pa, pb)
    return y.reshape(bsz, g, nc, S5_CHUNK, SSM_GROUP).transpose(0, 2, 3, 1, 4).reshape(bsz * seq_len, g * SSM_GROUP)


def _moba_select_kernel(q_ref, k_ref, tri_ref, sel_ref, cnt_out_ref, km_ref, cnt_ref):
    h = pl.program_id(1)
    t = pl.program_id(2)
    ts = q_ref.shape[0]
    nbk = MOBA_MAX_BLOCKS

    @pl.when(t == 0)
    def _():
        kk = k_ref[...].astype(F32)
        nb = kk.shape[0] // MOBA_BLOCK
        km = jnp.sum(kk.reshape(nb, MOBA_BLOCK, LANES), axis=1) * (1.0 / MOBA_BLOCK)
        if nb < nbk:
            km = jnp.concatenate([km, jnp.zeros((nbk - nb, LANES), F32)], axis=0)
        lane = lax.broadcasted_iota(jnp.int32, (nbk, LANES), 1)
        km_ref[...] = jnp.where((lane // MOBA_HEAD_DIM) == (h % 2), km, 0.0).astype(BF16)
        cnt_ref[...] = jnp.zeros(cnt_ref.shape, F32)

    gate = _dot_nt(km_ref[...], q_ref[...])
    blk = lax.broadcasted_iota(jnp.int32, gate.shape, 0)
    qblk = (t * ts + lax.broadcasted_iota(jnp.int32, gate.shape, 1)) // MOBA_BLOCK
    g = jnp.where(blk < qblk, gate, -jnp.inf)
    selected = jnp.zeros(gate.shape, jnp.bool_)
    picks = []
    for _ in range(MOBA_TOPK):
        mx = jnp.max(g, axis=0, keepdims=True)
        idx = jnp.min(jnp.where(g == mx, blk, nbk), axis=0, keepdims=True)
        hit = blk == idx
        ok = (idx[0:1] < qblk[0:1]) & (mx > -jnp.inf)
        picks.append((hit, idx, ok))
        selected = selected | (hit & ok)
        g = jnp.where(hit, -jnp.inf, g)
    sel01 = jnp.where(selected, 1.0, 0.0).astype(BF16)
    prefix = _dot(sel01, tri_ref[...]) + jnp.tile(cnt_ref[...], (1, ts // LANES))
    rows = [jnp.where(ok, idx, -1) for (_, idx, ok) in picks]
    rows += [jnp.sum(jnp.where(hit, prefix, 0.0), axis=0, keepdims=True).astype(jnp.int32) for (hit, _, _) in picks]
    rows += [jnp.zeros((1, ts), jnp.int32)] * (8 - 2 * MOBA_TOPK)
    sel_ref[0, 0] = jnp.concatenate(rows, axis=0)
    cnt_ref[...] = cnt_ref[...] + _dot(sel01, jnp.ones((ts, LANES), BF16))
    cnt_out_ref[0, 0] = cnt_ref[...]


def _moba_select(q2, kv, bsz, seq_len, ts=1024):
    ts = min(ts, seq_len)
    heads = q2.shape[1] // MOBA_HEAD_DIM
    nt = seq_len // ts
    tri = (jnp.arange(ts)[:, None] < jnp.arange(ts)[None, :]).astype(BF16)
    return pl.pallas_call(
        _moba_select_kernel,
        grid=(bsz, heads, nt),
        in_specs=[
            pl.BlockSpec((ts, LANES), lambda b, h, t: (b * nt + t, h // 2)),
            pl.BlockSpec((seq_len, LANES), lambda b, h, t: (b, 2 * (h // 2))),
            pl.BlockSpec((ts, ts), lambda b, h, t: (0, 0)),
        ],
        out_specs=[pl.BlockSpec((1, 1, 8, ts), lambda b, h, t: (b, h, 0, t)),
                   pl.BlockSpec((1, 1, MOBA_MAX_BLOCKS, LANES), lambda b, h, t: (b, h, 0, 0))],
        out_shape=[jax.ShapeDtypeStruct((bsz, heads, 8, seq_len), jnp.int32),
                   jax.ShapeDtypeStruct((bsz, heads, MOBA_MAX_BLOCKS, LANES), F32)],
        scratch_shapes=[pltpu.VMEM((MOBA_MAX_BLOCKS, LANES), BF16), pltpu.VMEM((MOBA_MAX_BLOCKS, LANES), F32)],
        compiler_params=_params("parallel", "arbitrary", "arbitrary"),
        name="moba_select",
    )(q2, kv, tri)


def _softmax_block(q, kb, vb, causal):
    s = _dot_nt(q, kb)
    if causal:
        r = lax.broadcasted_iota(jnp.int32, s.shape, 0) % MOBA_BLOCK
        c = lax.broadcasted_iota(jnp.int32, s.shape, 1)
        s = jnp.where(c <= r, s, NEG_BIG)
    m = jnp.max(s, axis=1, keepdims=True)
    p = jnp.exp(s - m).astype(BF16)
    pv = _dot(p, jnp.concatenate([vb, jnp.ones(vb.shape, BF16)], axis=1))
    l = pv[:, LANES:LANES + 1]
    return pv[:, :LANES] / l, m + jnp.log(l)


MOBA_DIAG_BLOCKS = 2


def _moba_diag_kernel(q_ref, kv_ref, o_ref, lse_ref):
    lane = lax.broadcasted_iota(jnp.int32, (MOBA_BLOCK, LANES), 1)
    is_a = lane < MOBA_HEAD_DIM
    for u in range(MOBA_DIAG_BLOCKS):
        rows = pl.ds(u * MOBA_BLOCK, MOBA_BLOCK)
        q = q_ref[rows, :]
        zero = jnp.zeros_like(q)
        qs = jnp.concatenate([jnp.where(is_a, q, zero), jnp.where(is_a, zero, q)], axis=0)
        o, lse = _softmax_block(qs, kv_ref[rows, :LANES], kv_ref[rows, LANES:], True)
        o_ref[rows, :] = jnp.where(is_a, o[:MOBA_BLOCK], o[MOBA_BLOCK:])
        lse_ref[rows, :] = jnp.where(is_a, lse[:MOBA_BLOCK], lse[MOBA_BLOCK:])


def _moba_diag(q2, kv, bsz, seq_len):
    npair = q2.shape[1] // LANES
    rows = MOBA_BLOCK * MOBA_DIAG_BLOCKS
    nb = seq_len // rows
    spec = pl.BlockSpec((rows, LANES), lambda b, p, i: (b * nb + i, p))
    out = jax.ShapeDtypeStruct(q2.shape, F32)
    return pl.pallas_call(
        _moba_diag_kernel,
        grid=(bsz, npair, nb),
        in_specs=[spec, pl.BlockSpec((rows, 2 * LANES), lambda b, p, i: (b * nb + i, p))],
        out_specs=[spec, spec],
        out_shape=[out, out],
        compiler_params=_params("parallel", "parallel", "parallel"),
        name="moba_diag",
    )(q2, kv)


MOBA_TILE = 256
MOBA_TILES_PER_STEP = 8


def _moba_grouped_kernel(tile_row_ref, tile_pair_ref, tile_half_ref, tile_real_ref, qd_ref, *refs):
    n = MOBA_TILES_PER_STEP
    kv_refs, o_ref = refs[:n], refs[n]
    i = pl.program_id(0)
    lane = lax.broadcasted_iota(jnp.int32, (MOBA_TILE, LANES), 1)
    for u in range(n):
        tile = i * n + u
        rows = pl.ds(u * MOBA_TILE, MOBA_TILE)
        o, lse = _softmax_block(qd_ref[rows, :].astype(BF16), kv_refs[u][:, :LANES], kv_refs[u][:, LANES:], False)
        part = jnp.where((lane // MOBA_HEAD_DIM) == tile_half_ref[tile], o, lse)
        o_ref[rows, :] = jnp.where(tile_real_ref[tile] > 0, part, NEG_BIG)


def _moba_grouped(qd, kv, tile_row, tile_pair, tile_half, tile_real):
    n = MOBA_TILES_PER_STEP
    n_tiles = qd.shape[0] // MOBA_TILE

    def kv_spec(u):
        return pl.BlockSpec((MOBA_BLOCK, 2 * LANES), lambda i, tr, tp, th, tl: (tr[i * n + u], tp[i * n + u]))

    grid_spec = pltpu.PrefetchScalarGridSpec(
        num_scalar_prefetch=4,
        grid=(n_tiles // n,),
        in_specs=[pl.BlockSpec((n * MOBA_TILE, LANES), lambda i, tr, tp, th, tl: (i, 0))]
        + [kv_spec(u) for u in range(n)],
        out_specs=pl.BlockSpec((n * MOBA_TILE, LANES), lambda i, tr, tp, th, tl: (i, 0)),
    )
    return pl.pallas_call(
        _moba_grouped_kernel,
        grid_spec=grid_spec,
        out_shape=jax.ShapeDtypeStruct(qd.shape, F32),
        compiler_params=_params("arbitrary"),
        name="moba_grouped",
    )(tile_row, tile_pair, tile_half, tile_real, qd, *([kv] * n))


def _moba_combine_kernel(od_ref, lsed_ref, *refs):
    g_refs, o_ref = refs[:-1], refs[-1]
    lane = lax.broadcasted_iota(jnp.int32, od_ref.shape, 1)
    is_a = lane < MOBA_HEAD_DIM
    parts = [(od_ref[...], lsed_ref[...])]
    for s in range(MOBA_TOPK):
        xa = g_refs[s][0, 0, 0]
        xb = g_refs[MOBA_TOPK + s][0, 0, 0]
        o = jnp.where(is_a, xa, xb)
        lse = pltpu.roll(jnp.where(is_a, xb, xa), MOBA_HEAD_DIM, 1)
        parts.append((o, lse))
    m = parts[0][1]
    for _, lse in parts[1:]:
        m = jnp.maximum(m, lse)
    num = jnp.zeros(od_ref.shape, F32)
    den = jnp.zeros(od_ref.shape, F32)
    for o, lse in parts:
        w = jnp.exp(lse - m)
        num = num + w * o
        den = den + w
    o_ref[...] = (num / den).astype(BF16)


def _moba_combine(od, lsed, gath, bsz, seq_len, tm=1024):
    tm = min(tm, seq_len)
    npair = od.shape[1] // LANES
    nt = seq_len // tm
    spec = pl.BlockSpec((tm, LANES), lambda b, p, i: (b * nt + i, p))

    def g_spec(e, s):
        return pl.BlockSpec((1, 1, 1, tm, LANES), lambda b, p, i: (b, 2 * p + e, s, i, 0))

    return pl.pallas_call(
        _moba_combine_kernel,
        grid=(bsz, npair, nt),
        in_specs=[spec, spec] + [g_spec(e, s) for e in range(2) for s in range(MOBA_TOPK)],
        out_specs=spec,
        out_shape=jax.ShapeDtypeStruct(od.shape, BF16),
        compiler_params=_params("parallel", "parallel", "parallel"),
        name="moba_combine",
    )(od, lsed, *([gath] * (2 * MOBA_TOPK)))


def _gather_rows(table, idx):
    return table.at[idx].get(mode='promise_in_bounds')


SORT_LAST = jnp.iinfo(jnp.int32).max


def _rows_by_sort(dest, src, counts, pcounts, tile, n_out):
    fill = counts[:, None] + jnp.arange(tile, dtype=jnp.int32)[None, :]
    first = (jnp.cumsum(pcounts) - pcounts)[:, None]
    fill_key = jnp.where(fill < pcounts[:, None], first + fill, SORT_LAST).reshape(-1)
    keys = jnp.concatenate([dest, fill_key])
    vals = jnp.concatenate([src, jnp.zeros(fill_key.shape, jnp.int32)])
    _, by_row = lax.sort((keys, vals), num_keys=1)
    return jnp.concatenate([by_row, jnp.zeros((n_out - by_row.shape[0],), jnp.int32)])


def _moba(q2, qh, kv, bsz, seq_len):
    heads = q2.shape[1] // MOBA_HEAD_DIM
    nbk = MOBA_MAX_BLOCKS
    sel, cnt = _moba_select(q2, kv, bsz, seq_len)
    od, lsed = _moba_diag(q2, kv, bsz, seq_len)
    counts = cnt[..., 0].astype(jnp.int32).reshape(bsz * heads * nbk)
    pcounts = ((counts + MOBA_TILE - 1) // MOBA_TILE) * MOBA_TILE
    pends = jnp.cumsum(pcounts)
    pstarts = (pends - pcounts).reshape(bsz, heads, 1, 1, nbk)
    n_items = bsz * heads * seq_len * MOBA_TOPK
    step_rows = MOBA_TILE * MOBA_TILES_PER_STEP
    n_rows = -(-(n_items + bsz * heads * nbk * MOBA_TILE) // step_rows) * step_rows
    n_null = step_rows
    idx = sel[:, :, 0:MOBA_TOPK, :]
    rank = sel[:, :, MOBA_TOPK:2 * MOBA_TOPK, :]
    start = jnp.sum(jnp.where(idx[..., None] == jnp.arange(nbk), pstarts, 0), axis=-1)
    null_row = n_rows + jnp.arange(seq_len, dtype=jnp.int32) % n_null
    dest = jnp.where(idx >= 0, start + rank, null_row)
    n_tiles = (n_rows + n_null) // MOBA_TILE
    n_groups = bsz * heads * nbk
    tile_start = jnp.arange(n_tiles, dtype=jnp.int32) * MOBA_TILE
    tile_g = jnp.minimum(jnp.sum((pends[None, :] <= tile_start[:, None]).astype(jnp.int32), axis=1), n_groups - 1)
    tile_real = (tile_start < pends[-1]).astype(jnp.int32)
    tile_head = (tile_g // nbk) % heads
    tile_row = (tile_g // (heads * nbk)) * (seq_len // MOBA_BLOCK) + jnp.minimum(tile_g % nbk, seq_len // MOBA_BLOCK - 1)
    src_row = (jnp.arange(heads, dtype=jnp.int32).reshape(1, heads, 1, 1) * (bsz * seq_len)
               + jnp.arange(bsz, dtype=jnp.int32).reshape(bsz, 1, 1, 1) * seq_len
               + jnp.arange(seq_len, dtype=jnp.int32))
    src_row = jnp.broadcast_to(src_row, idx.shape).reshape(-1)
    sort_key = jnp.where(idx >= 0, dest, SORT_LAST).reshape(-1)
    slot_src = _rows_by_sort(sort_key, src_row, counts, pcounts, MOBA_TILE, n_rows + n_null)
    qd = _gather_rows(qh.reshape(-1, LANES), slot_src)
    part = _moba_grouped(qd, kv, tile_row, tile_head // 2, tile_head % 2, tile_real)
    gath = _gather_rows(part, dest.reshape(-1)).reshape(bsz, heads, MOBA_TOPK, seq_len, LANES)
    return _moba_combine(od, lsed, gath, bsz, seq_len)


def _mem_kv_kernel(mem_ref, gmem_ref, w_ref, gck_ref, k_ref, v_ref):
    xf = mem_ref[0]
    ms = jnp.mean(xf * xf, axis=-1, keepdims=True)
    h = (xf * lax.rsqrt(ms + EPS) * gmem_ref[...]).astype(BF16)
    kv = _dot(h, w_ref[...])
    w = k_ref.shape[2]
    hd = w // X_HEADS
    for c in range(X_HEADS):
        chunk = kv[:, c * hd:(c + 1) * hd]
        cms = jnp.mean(chunk * chunk, axis=-1, keepdims=True)
        k_ref[0, :, c * hd:(c + 1) * hd] = (chunk * lax.rsqrt(cms + EPS) * gck_ref[...]).astype(BF16)
    v_ref[0] = kv[:, w:].astype(BF16)


def _mem_kv(mem, g_mem, w_kv_mem, g_ck):
    bsz, m, d = mem.shape
    w = w_kv_mem.shape[1] // 2
    const = lambda b: (0, 0)
    out = jax.ShapeDtypeStruct((bsz, m, w), BF16)
    return pl.pallas_call(
        _mem_kv_kernel,
        grid=(bsz,),
        in_specs=[pl.BlockSpec((1, m, d), lambda b: (b, 0, 0)), pl.BlockSpec((1, d), const),
                  pl.BlockSpec((d, 2 * w), const), pl.BlockSpec((1, w // X_HEADS), const)],
        out_specs=[pl.BlockSpec((1, m, w), lambda b: (b, 0, 0))] * 2,
        out_shape=[out, out],
        compiler_params=_params("parallel"),
        name="mem_kv",
    )(mem, g_mem.reshape(1, d), w_kv_mem.astype(BF16), g_ck.reshape(1, -1))


def _merge_kernel(x_ref, ys_ref, u_ref, dskip_ref, om_ref, xq_ref, kc_ref, vc_ref, g_ref,
                  wglu_ref, wmo_ref, wco_ref, wout_ref, gffn_ref, wr_ref, br_ref,
                  x1_ref, h2_ref, logit_ref):
    d = x_ref.shape[1]
    y = ys_ref[...].astype(F32) + dskip_ref[...] * u_ref[...].astype(F32)
    ge = 0.5 * y * (1.0 + jnp.tanh(math.sqrt(2.0 / math.pi) * (y + 0.044715 * (y * y * y))))
    z = _dot(ge.astype(BF16), wglu_ref[...])
    merged = g_ref[:, 0:d].astype(F32) * (z[:, :d] * _sigmoid(z[:, d:]))
    merged = merged + g_ref[:, d:2 * d].astype(F32) * _dot(om_ref[...], wmo_ref[...])
    w = xq_ref.shape[1]
    hd = w // X_HEADS
    heads = []
    for c in range(X_HEADS):
        s = _dot_nt(xq_ref[:, c * hd:(c + 1) * hd], kc_ref[0, :, c * hd:(c + 1) * hd]) * (hd ** -0.5)
        p = jnp.exp(s - jnp.max(s, axis=1, keepdims=True))
        p = p / jnp.sum(p, axis=1, keepdims=True)
        heads.append(_dot(p.astype(BF16), vc_ref[0, :, c * hd:(c + 1) * hd]))
    oc = jnp.concatenate(heads, axis=1).astype(BF16)
    merged = merged + g_ref[:, 2 * d:3 * d].astype(F32) * _dot(oc, wco_ref[...])
    x1 = x_ref[...] + _dot(merged.astype(BF16), wout_ref[...])
    x1_ref[...] = x1
    ms = jnp.mean(x1 * x1, axis=-1, keepdims=True)
    h2 = (x1 * lax.rsqrt(ms + EPS) * gffn_ref[...]).astype(BF16)
    h2_ref[...] = h2.astype(F32)
    logit_ref[...] = _dot(h2, wr_ref[...]) + br_ref[...]


def _merge(xt, ys, u, d_skip, om, xqn, kc, vc, gates, w_glu, w_mo, w_co, w_out, g_ffn, w_router, b_router,
           seq_len, tm=256):
    t, d = xt.shape
    w = d // 2
    m = kc.shape[1]
    ne = w_router.shape[1]
    nt = seq_len // tm
    row = lambda i: (i, 0)
    const = lambda i: (0, 0)
    per_b = lambda i: (i // nt, 0, 0)
    return pl.pallas_call(
        _merge_kernel,
        grid=(t // tm,),
        in_specs=[
            pl.BlockSpec((tm, d), row), pl.BlockSpec((tm, w), row), pl.BlockSpec((tm, w), row),
            pl.BlockSpec((1, w), const), pl.BlockSpec((tm, w), row), pl.BlockSpec((tm, w), row),
            pl.BlockSpec((1, m, w), per_b), pl.BlockSpec((1, m, w), per_b),
            pl.BlockSpec((tm, N_BRANCH * d), row),
            pl.BlockSpec((w, 2 * d), const), pl.BlockSpec((w, d), const), pl.BlockSpec((w, d), const),
            pl.BlockSpec((d, d), const), pl.BlockSpec((1, d), const),
            pl.BlockSpec((d, ne), const), pl.BlockSpec((1, ne), const),
        ],
        out_specs=[pl.BlockSpec((tm, d), row), pl.BlockSpec((tm, d), row), pl.BlockSpec((tm, ne), row)],
        out_shape=[jax.ShapeDtypeStruct((t, d), F32), jax.ShapeDtypeStruct((t, d), F32),
                   jax.ShapeDtypeStruct((t, ne), F32)],
        compiler_params=_params("parallel"),
        name="merge",
    )(xt, ys, u, d_skip.reshape(1, w), om, xqn, kc, vc, gates,
      w_glu.astype(BF16), w_mo.astype(BF16), w_co.astype(BF16), w_out.astype(BF16),
      g_ffn.reshape(1, d), w_router.astype(BF16), b_router.reshape(1, ne))


def _moe_kernel(blk_e_ref, blk_used_ref, xs_ref, wgu_ref, bgu_ref, wd_ref, bd_ref, y_ref, wgu_bf, wd_bf):
    i = pl.program_id(0)
    prev = blk_e_ref[jnp.maximum(i - 1, 0)]

    @pl.when((i == 0) | (blk_e_ref[i] != prev))
    def _():
        wgu_bf[...] = wgu_ref[0].astype(BF16)
        wd_bf[...] = wd_ref[0].astype(BF16)

    @pl.when(blk_used_ref[i] > 0)
    def _():
        de = wd_bf.shape[0]
        gu = _dot(xs_ref[...].astype(BF16), wgu_bf[...]) + bgu_ref[0]
        gate = jnp.minimum(gu[:, :de], SWIGLU_LIMIT)
        up = jnp.clip(gu[:, de:], -SWIGLU_LIMIT, SWIGLU_LIMIT)
        act = gate * _sigmoid(SWIGLU_ALPHA * gate) * (up + 1.0)
        y_ref[...] = _dot(act.astype(BF16), wd_bf[...]) + bd_ref[0]

    @pl.when(blk_used_ref[i] == 0)
    def _():
        y_ref[...] = jnp.zeros(y_ref.shape, y_ref.dtype)


def _moe_experts(xs, blk_e, blk_used, w_gu, b_gu, w_down, b_down):
    p, d = xs.shape
    ne, _, de2 = w_gu.shape
    de = de2 // 2
    nblk = p // EXPERT_ROWS
    grid_spec = pltpu.PrefetchScalarGridSpec(
        num_scalar_prefetch=2,
        grid=(nblk,),
        in_specs=[
            pl.BlockSpec((EXPERT_ROWS, d), lambda i, e, n: (i, 0)),
            pl.BlockSpec((1, d, de2), lambda i, e, n: (e[i], 0, 0)),
            pl.BlockSpec((1, 1, de2), lambda i, e, n: (e[i], 0, 0)),
            pl.BlockSpec((1, de, d), lambda i, e, n: (e[i], 0, 0)),
            pl.BlockSpec((1, 1, d), lambda i, e, n: (e[i], 0, 0)),
        ],
        out_specs=pl.BlockSpec((EXPERT_ROWS, d), lambda i, e, n: (i, 0)),
        scratch_shapes=[pltpu.VMEM((d, de2), BF16), pltpu.VMEM((de, d), BF16)],
    )
    return pl.pallas_call(
        _moe_kernel,
        grid_spec=grid_spec,
        out_shape=jax.ShapeDtypeStruct((p, d), F32),
        compiler_params=_params("arbitrary"),
        name="moe_experts",
    )(blk_e, blk_used, xs, w_gu, b_gu.reshape(ne, 1, de2), w_down, b_down.reshape(ne, 1, d))


def _moe_ffn(x1, h2, logits, w_gu, b_gu, w_down, b_down):
    t, d = x1.shape
    tk = t * TOPK_EXPERTS
    top_v, top_e = lax.top_k(logits, TOPK_EXPERTS)
    weights = jax.nn.softmax(top_v, axis=-1)
    e_flat = top_e.reshape(tk)
    expert_ids = jnp.arange(N_EXPERTS, dtype=e_flat.dtype)
    is_e = e_flat[:, None] == expert_ids[None, :]
    counts = jnp.sum(is_e.astype(jnp.int32), axis=0)
    order = jnp.argsort(e_flat, stable=True).astype(jnp.int32)
    pos = jnp.argsort(order).astype(jnp.int32)
    pcounts = ((counts + EXPERT_ROWS - 1) // EXPERT_ROWS) * EXPERT_ROWS
    pends = jnp.cumsum(pcounts)
    pstarts = pends - pcounts
    cstarts = jnp.cumsum(counts) - counts
    dest = jnp.sum(jnp.where(is_e, (pstarts - cstarts)[None, :], 0), axis=1) + pos
    nblk = -(-tk // EXPERT_ROWS) + N_EXPERTS
    blk_start = jnp.arange(nblk, dtype=jnp.int32) * EXPERT_ROWS
    blk_e = jnp.minimum(jnp.sum((pends[None, :] <= blk_start[:, None]).astype(jnp.int32), axis=1), N_EXPERTS - 1)
    blk_used = (blk_start < pends[-1]).astype(jnp.int32)
    tok = jnp.arange(tk, dtype=jnp.int32) // TOPK_EXPERTS
    slot_tok = _rows_by_sort(dest, tok, counts, pcounts, EXPERT_ROWS, nblk * EXPERT_ROWS)
    xs = _gather_rows(h2, slot_tok)
    ys = _moe_experts(xs, blk_e, blk_used, w_gu, b_gu, w_down, b_down)
    picked = _gather_rows(ys, dest.reshape(t, TOPK_EXPERTS).T.reshape(-1)).reshape(TOPK_EXPERTS, t, d)
    out = x1
    for k in range(TOPK_EXPERTS):
        out = out + picked[k] * weights[:, k:k + 1]
    return out


def kernel(x, mem, g_mix, w_in, lam_re, lam_im, log_dt, b_re, b_im, c_re, c_im, d_skip, w_glu, g_q, g_k, w_moba_out, g_mem, w_kv_mem, g_cq, g_ck, w_cross_out, w_out, g_ffn, w_router, b_router, w_gu, b_gu, w_down, b_down):
    bsz, seq_len, d = x.shape
    xt = x.reshape(bsz * seq_len, d)
    for l in range(g_mix.shape[0]):
        u, q2, kv, xqn, gates, qh = _in_proj(xt, g_mix[l], w_in[l], g_q[l], g_k[l], g_cq[l], seq_len)
        mats = _s5_matrices(lam_re[l], lam_im[l], log_dt[l], b_re[l], b_im[l], c_re[l], c_im[l],
                            seq_len // S5_CHUNK)
        ys = _s5(u, mats, bsz, seq_len)
        om = _moba(q2, qh, kv, bsz, seq_len)
        kc, vc = _mem_kv(mem, g_mem[l], w_kv_mem[l], g_ck[l])
        x1, h2, logits = _merge(xt, ys, u, d_skip[l], om, xqn, kc, vc, gates, w_glu[l], w_moba_out[l],
                                w_cross_out[l], w_out[l], g_ffn[l], w_router[l], b_router[l], seq_len)
        xt = _moe_ffn(x1, h2, logits, w_gu[l], b_gu[l], w_down[l], b_down[l])
    return xt.reshape(bsz, seq_len, d)
```

```python
import functools
import math

import jax
import jax.numpy as jnp
from jax import lax
from jax.experimental import pallas as pl
from jax.experimental.pallas import tpu as pltpu
from jax.experimental.pallas import tpu_sc as plsc

F32 = jnp.float32
BF16 = jnp.bfloat16

EPS = 1e-6
N_BRANCH = 3
SSM_GROUP = 16
SSM_STATE = 64
S5_CHUNK = 16
MOBA_HEAD_DIM = 64
MOBA_BLOCK = 256
MOBA_TOPK = 3
MOBA_MAX_BLOCKS = 64
ROPE_THETA = 10000.0
X_HEADS = 4
N_EXPERTS = 32
TOPK_EXPERTS = 4
SWIGLU_LIMIT = 7.0
SWIGLU_ALPHA = 1.702
EXPERT_ROWS = 512
NEG_BIG = -1e30
LANES = 128
VMEM_LIMIT_BYTES = 56 * 1024 * 1024


def _params(*sem):
    return pltpu.CompilerParams(dimension_semantics=sem, vmem_limit_bytes=VMEM_LIMIT_BYTES)


def _sigmoid(x):
    return 1.0 / (1.0 + jnp.exp(-x))


def _dot(a, b):
    return jnp.dot(a, b, preferred_element_type=F32)


def _dot_nt(a, b):
    return lax.dot_general(a, b, (((1,), (1,)), ((), ())), preferred_element_type=F32)


def _inproj_kernel(x_ref, gmix_ref, wa_ref, wg_ref, e64_ref, gq_ref, gk_ref, gcq_ref, cos_ref, sin_ref,
                   u_ref, q_ref, kv_ref, xq_ref, g_ref, qh_ref):
    xf = x_ref[...]
    ms = jnp.mean(xf * xf, axis=-1, keepdims=True)
    h = (xf * lax.rsqrt(ms + EPS) * gmix_ref[...]).astype(BF16)
    a = _dot(h, wa_ref[...])
    w = u_ref.shape[1]
    u_ref[...] = a[:, :w].astype(BF16)

    cos = jnp.tile(cos_ref[...], (1, w // LANES))
    sin = jnp.tile(sin_ref[...], (1, w // LANES))
    lane = lax.broadcasted_iota(jnp.int32, (xf.shape[0], w), 1)
    first_half = (lane % MOBA_HEAD_DIM) < (MOBA_HEAD_DIM // 2)

    def qk_norm_rope(raw, g):
        ss = _dot((raw * raw).astype(BF16), e64_ref[...])
        n = raw * lax.rsqrt(ss * (1.0 / MOBA_HEAD_DIM) + EPS) * g
        rot = jnp.where(first_half,
                        pltpu.roll(n, w - MOBA_HEAD_DIM // 2, 1),
                        pltpu.roll(n, MOBA_HEAD_DIM // 2, 1))
        return n * cos + rot * sin

    q = qk_norm_rope(a[:, w:2 * w], gq_ref[...]) * (MOBA_HEAD_DIM ** -0.5)
    q_ref[...] = q.astype(BF16)
    in_a = lax.broadcasted_iota(jnp.int32, (xf.shape[0], LANES), 1) < MOBA_HEAD_DIM
    for p in range(w // LANES):
        pair = q[:, p * LANES:(p + 1) * LANES].astype(BF16).astype(F32)
        qh_ref[2 * p] = jnp.where(in_a, pair, 0.0)
        qh_ref[2 * p + 1] = jnp.where(in_a, 0.0, pair)
    k = qk_norm_rope(a[:, 2 * w:3 * w], gk_ref[...]).astype(BF16)
    for p in range(w // LANES):
        kv_ref[:, (2 * p) * LANES:(2 * p + 1) * LANES] = k[:, p * LANES:(p + 1) * LANES]
        kv_ref[:, (2 * p + 1) * LANES:(2 * p + 2) * LANES] = a[:, 3 * w + p * LANES:3 * w + (p + 1) * LANES].astype(BF16)

    xq = a[:, 4 * w:5 * w]
    hd = w // X_HEADS
    for c in range(X_HEADS):
        chunk = xq[:, c * hd:(c + 1) * hd]
        cms = jnp.mean(chunk * chunk, axis=-1, keepdims=True)
        xq_ref[:, c * hd:(c + 1) * hd] = (chunk * lax.rsqrt(cms + EPS) * gcq_ref[...]).astype(BF16)

    d = xf.shape[1]
    for c in range(N_BRANCH):
        z = _dot(h, wg_ref[:, c * d:(c + 1) * d])
        g_ref[:, c * d:(c + 1) * d] = _sigmoid(z).astype(BF16)


def _in_proj(xt, g_mix, w_in, g_q, g_k, g_cq, seq_len, tm=256):
    t, d = xt.shape
    w = d // 2
    wa = w_in[:, :5 * w].astype(BF16)
    wg = w_in[:, 5 * w:].astype(BF16)
    heads = w // MOBA_HEAD_DIM
    e64 = jnp.kron(jnp.eye(heads, dtype=F32), jnp.ones((MOBA_HEAD_DIM, MOBA_HEAD_DIM), F32)).astype(BF16)
    half = MOBA_HEAD_DIM // 2
    inv = ROPE_THETA ** (-jnp.arange(half, dtype=F32) / half)
    ang = jnp.arange(seq_len, dtype=F32)[:, None] * inv[None, :]
    cos = jnp.tile(jnp.cos(ang), (1, LANES // half))
    sin = jnp.tile(jnp.concatenate([-jnp.sin(ang), jnp.sin(ang)], axis=1), (1, LANES // MOBA_HEAD_DIM))
    nt = seq_len // tm
    row = lambda i: (i, 0)
    const = lambda i: (0, 0)
    out_w = jax.ShapeDtypeStruct((t, w), BF16)
    return pl.pallas_call(
        _inproj_kernel,
        grid=(t // tm,),
        in_specs=[
            pl.BlockSpec((tm, d), row),
            pl.BlockSpec((1, d), const),
            pl.BlockSpec((d, 5 * w), const),
            pl.BlockSpec((d, N_BRANCH * d), const),
            pl.BlockSpec((w, w), const),
            pl.BlockSpec((1, w), const),
            pl.BlockSpec((1, w), const),
            pl.BlockSpec((1, w // X_HEADS), const),
            pl.BlockSpec((tm, LANES), lambda i: (i % nt, 0)),
            pl.BlockSpec((tm, LANES), lambda i: (i % nt, 0)),
        ],
        out_specs=[pl.BlockSpec((tm, w), row), pl.BlockSpec((tm, w), row), pl.BlockSpec((tm, 2 * w), row),
                   pl.BlockSpec((tm, w), row), pl.BlockSpec((tm, N_BRANCH * d), row),
                   pl.BlockSpec((heads, tm, LANES), lambda i: (0, i, 0))],
        out_shape=[out_w, out_w, jax.ShapeDtypeStruct((t, 2 * w), BF16), out_w,
                   jax.ShapeDtypeStruct((t, N_BRANCH * d), BF16),
                   jax.ShapeDtypeStruct((heads, t, LANES), F32)],
        compiler_params=_params("parallel"),
        name="in_proj",
    )(xt, g_mix.reshape(1, d), wa, wg, e64,
      jnp.tile(g_q, heads).reshape(1, w), jnp.tile(g_k, heads).reshape(1, w), g_cq.reshape(1, -1), cos, sin)


def _s5_matrices(lam_re, lam_im, log_dt, b_re, b_im, c_re, c_im, n_chunks):
    hp = lax.Precision.HIGHEST
    c = S5_CHUNK
    dt = jnp.exp(log_dt)[:, None]
    mag = jnp.exp(lam_re * dt)
    ar = mag * jnp.cos(lam_im * dt)
    ai = mag * jnp.sin(lam_im * dt)
    nr = ar - 1.0
    den = lam_re * lam_re + lam_im * lam_im
    cr = (nr * lam_re + ai * lam_im) / den
    ci = (ai * lam_re - nr * lam_im) / den
    bbr = cr[..., None] * b_re - ci[..., None] * b_im
    bbi = cr[..., None] * b_im + ci[..., None] * b_re

    def power(n):
        nf = n.astype(F32)[None, :, None]
        m = jnp.exp((lam_re * dt)[:, None, :] * nf)
        th = (lam_im * dt)[:, None, :] * nf
        return m * jnp.cos(th), m * jnp.sin(th)

    pr, pi = power(jnp.arange(c + 1))
    kbr = pr[..., None] * bbr[:, None] - pi[..., None] * bbi[:, None]
    kbi = pr[..., None] * bbi[:, None] + pi[..., None] * bbr[:, None]
    kk = (jnp.einsum('ghp,gtpc->gthc', c_re, kbr, precision=hp)
          - jnp.einsum('ghp,gtpc->gthc', c_im, kbi, precision=hp))
    tq = jnp.arange(c)
    lag = tq[None, :] - tq[:, None]
    toep = kk[:, jnp.clip(lag, 0, c)]
    toep = jnp.where((lag >= 0)[None, :, :, None, None], toep, 0.0)
    g = toep.shape[0]
    toep = toep.transpose(0, 1, 4, 2, 3).reshape(g, c * SSM_GROUP, c * SSM_GROUP)
    rev = c - 1 - tq
    w_in = jnp.concatenate([kbr[:, rev].transpose(0, 1, 3, 2), kbi[:, rev].transpose(0, 1, 3, 2)], axis=-1)
    w_in = w_in.reshape(g, c * SSM_GROUP, 2 * SSM_STATE)
    prn, pin = pr[:, 1:], pi[:, 1:]
    wo_r = c_re[:, None] * prn[:, :, None, :] - c_im[:, None] * pin[:, :, None, :]
    wo_i = -c_re[:, None] * pin[:, :, None, :] - c_im[:, None] * prn[:, :, None, :]
    w_out = jnp.concatenate([wo_r, wo_i], axis=-1).transpose(0, 3, 1, 2).reshape(g, 2 * SSM_STATE, c * SSM_GROUP)
    n_steps = max(1, int(math.ceil(math.log2(n_chunks))))
    qr, qi = power(c * (2 ** jnp.arange(n_steps)))
    pa = jnp.concatenate([qr, qr], axis=-1)
    pb = jnp.concatenate([-qi, qi], axis=-1)
    return toep.astype(BF16), w_in.astype(BF16), w_out.astype(BF16), pa, pb


def _s5_kernel(u_ref, toep_ref, win_ref, wout_ref, pa_ref, pb_ref, y_ref):
    u = u_ref[0, 0]
    s = _dot(u, win_ref[0])
    nc = s.shape[0]
    row = lax.broadcasted_iota(jnp.int32, s.shape, 0)
    n_steps = pa_ref.shape[1]
    for k in range(n_steps):
        sh = 1 << k
        if sh >= nc:
            break
        prev = jnp.where(row >= sh, pltpu.roll(s, sh, 0), 0.0)
        s = s + pa_ref[0, k:k + 1, :] * prev + pb_ref[0, k:k + 1, :] * pltpu.roll(prev, SSM_STATE, 1)
    s_start = jnp.where(row >= 1, pltpu.roll(s, 1, 0), 0.0)
    y = _dot(u, toep_ref[0]) + _dot(s_start.astype(BF16), wout_ref[0])
    y_ref[0, 0] = y.astype(BF16)


def _s5(u, mats, bsz, seq_len):
    toep, w_in, w_out, pa, pb = mats
    g = toep.shape[0]
    nc = seq_len // S5_CHUNK
    cw = S5_CHUNK * SSM_GROUP
    ug = u.reshape(bsz, nc, S5_CHUNK, g, SSM_GROUP).transpose(0, 3, 1, 2, 4).reshape(bsz, g, nc, cw)
    per_g = lambda b, j: (j, 0, 0)
    y = pl.pallas_call(
        _s5_kernel,
        grid=(bsz, g),
        in_specs=[
            pl.BlockSpec((1, 1, nc, cw), lambda b, j: (b, j, 0, 0)),
            pl.BlockSpec((1, cw, cw), per_g),
            pl.BlockSpec((1, cw, 2 * SSM_STATE), per_g),
            pl.BlockSpec((1, 2 * SSM_STATE, cw), per_g),
            pl.BlockSpec((1,) + pa.shape[1:], per_g),
            pl.BlockSpec((1,) + pb.shape[1:], per_g),
        ],
        out_specs=pl.BlockSpec((1, 1, nc, cw), lambda b, j: (b, j, 0, 0)),
        out_shape=jax.ShapeDtypeStruct((bsz, g, nc, cw), BF16),
        compiler_params=_params("parallel", "parallel"),
        name="s5_scan",
    )(ug, toep, w_in, w_out, pa, pb)
    return y.reshape(bsz, g, nc, S5_CHUNK, SSM_GROUP).transpose(0, 2, 3, 1, 4).reshape(bsz * seq_len, g * SSM_GROUP)


def _moba_select_kernel(q_ref, k_ref, tri_ref, sel_ref, cnt_out_ref, km_ref, cnt_ref):
    h = pl.program_id(1)
    t = pl.program_id(2)
    ts = q_ref.shape[0]
    nbk = MOBA_MAX_BLOCKS

    @pl.when(t == 0)
    def _():
        kk = k_ref[...].astype(F32)
        nb = kk.shape[0] // MOBA_BLOCK
        km = jnp.sum(kk.reshape(nb, MOBA_BLOCK, LANES), axis=1) * (1.0 / MOBA_BLOCK)
        if nb < nbk:
            km = jnp.concatenate([km, jnp.zeros((nbk - nb, LANES), F32)], axis=0)
        lane = lax.broadcasted_iota(jnp.int32, (nbk, LANES), 1)
        km_ref[...] = jnp.where((lane // MOBA_HEAD_DIM) == (h % 2), km, 0.0).astype(BF16)
        cnt_ref[...] = jnp.zeros(cnt_ref.shape, F32)

    gate = _dot_nt(km_ref[...], q_ref[...])
    blk = lax.broadcasted_iota(jnp.int32, gate.shape, 0)
    qblk = (t * ts + lax.broadcasted_iota(jnp.int32, gate.shape, 1)) // MOBA_BLOCK
    g = jnp.where(blk < qblk, gate, -jnp.inf)
    selected = jnp.zeros(gate.shape, jnp.bool_)
    picks = []
    for _ in range(MOBA_TOPK):
        mx = jnp.max(g, axis=0, keepdims=True)
        idx = jnp.min(jnp.where(g == mx, blk, nbk), axis=0, keepdims=True)
        hit = blk == idx
        ok = (idx[0:1] < qblk[0:1]) & (mx > -jnp.inf)
        picks.append((hit, idx, ok))
        selected = selected | (hit & ok)
        g = jnp.where(hit, -jnp.inf, g)
    sel01 = jnp.where(selected, 1.0, 0.0).astype(BF16)
    prefix = _dot(sel01, tri_ref[...]) + jnp.tile(cnt_ref[...], (1, ts // LANES))
    rows = [jnp.where(ok, idx, -1) for (_, idx, ok) in picks]
    rows += [jnp.sum(jnp.where(hit, prefix, 0.0), axis=0, keepdims=True).astype(jnp.int32) for (hit, _, _) in picks]
    rows += [jnp.zeros((1, ts), jnp.int32)] * (8 - 2 * MOBA_TOPK)
    sel_ref[0, 0] = jnp.concatenate(rows, axis=0)
    cnt_ref[...] = cnt_ref[...] + _dot(sel01, jnp.ones((ts, LANES), BF16))
    cnt_out_ref[0, 0] = cnt_ref[...]


def _moba_select(q2, kv, bsz, seq_len, ts=1024):
    ts = min(ts, seq_len)
    heads = q2.shape[1] // MOBA_HEAD_DIM
    nt = seq_len // ts
    tri = (jnp.arange(ts)[:, None] < jnp.arange(ts)[None, :]).astype(BF16)
    return pl.pallas_call(
        _moba_select_kernel,
        grid=(bsz, heads, nt),
        in_specs=[
            pl.BlockSpec((ts, LANES), lambda b, h, t: (b * nt + t, h // 2)),
            pl.BlockSpec((seq_len, LANES), lambda b, h, t: (b, 2 * (h // 2))),
            pl.BlockSpec((ts, ts), lambda b, h, t: (0, 0)),
        ],
        out_specs=[pl.BlockSpec((1, 1, 8, ts), lambda b, h, t: (b, h, 0, t)),
                   pl.BlockSpec((1, 1, MOBA_MAX_BLOCKS, LANES), lambda b, h, t: (b, h, 0, 0))],
        out_shape=[jax.ShapeDtypeStruct((bsz, heads, 8, seq_len), jnp.int32),
                   jax.ShapeDtypeStruct((bsz, heads, MOBA_MAX_BLOCKS, LANES), F32)],
        scratch_shapes=[pltpu.VMEM((MOBA_MAX_BLOCKS, LANES), BF16), pltpu.VMEM((MOBA_MAX_BLOCKS, LANES), F32)],
        compiler_params=_params("parallel", "arbitrary", "arbitrary"),
        name="moba_select",
    )(q2, kv, tri)


def _softmax_block(q, kb, vb, causal):
    s = _dot_nt(q, kb)
    if causal:
        r = lax.broadcasted_iota(jnp.int32, s.shape, 0) % MOBA_BLOCK
        c = lax.broadcasted_iota(jnp.int32, s.shape, 1)
        s = jnp.where(c <= r, s, NEG_BIG)
    m = jnp.max(s, axis=1, keepdims=True)
    p = jnp.exp(s - m).astype(BF16)
    pv = _dot(p, jnp.concatenate([vb, jnp.ones(vb.shape, BF16)], axis=1))
    l = pv[:, LANES:LANES + 1]
    return pv[:, :LANES] / l, m + jnp.log(l)


MOBA_DIAG_BLOCKS = 2


def _moba_diag_kernel(q_ref, kv_ref, o_ref, lse_ref):
    lane = lax.broadcasted_iota(jnp.int32, (MOBA_BLOCK, LANES), 1)
    is_a = lane < MOBA_HEAD_DIM
    for u in range(MOBA_DIAG_BLOCKS):
        rows = pl.ds(u * MOBA_BLOCK, MOBA_BLOCK)
        q = q_ref[rows, :]
        zero = jnp.zeros_like(q)
        qs = jnp.concatenate([jnp.where(is_a, q, zero), jnp.where(is_a, zero, q)], axis=0)
        o, lse = _softmax_block(qs, kv_ref[rows, :LANES], kv_ref[rows, LANES:], True)
        o_ref[rows, :] = jnp.where(is_a, o[:MOBA_BLOCK], o[MOBA_BLOCK:])
        lse_ref[rows, :] = jnp.where(is_a, lse[:MOBA_BLOCK], lse[MOBA_BLOCK:])


def _moba_diag(q2, kv, bsz, seq_len):
    npair = q2.shape[1] // LANES
    rows = MOBA_BLOCK * MOBA_DIAG_BLOCKS
    nb = seq_len // rows
    spec = pl.BlockSpec((rows, LANES), lambda b, p, i: (b * nb + i, p))
    out = jax.ShapeDtypeStruct(q2.shape, F32)
    return pl.pallas_call(
        _moba_diag_kernel,
        grid=(bsz, npair, nb),
        in_specs=[spec, pl.BlockSpec((rows, 2 * LANES), lambda b, p, i: (b * nb + i, p))],
        out_specs=[spec, spec],
        out_shape=[out, out],
        compiler_params=_params("parallel", "parallel", "parallel"),
        name="moba_diag",
    )(q2, kv)


MOBA_TILE = 256
MOBA_TILES_PER_STEP = 8


def _moba_grouped_kernel(tile_row_ref, tile_pair_ref, tile_half_ref, tile_real_ref, qd_ref, *refs):
    n = MOBA_TILES_PER_STEP
    kv_refs, o_ref = refs[:n], refs[n]
    i = pl.program_id(0)
    lane = lax.broadcasted_iota(jnp.int32, (MOBA_TILE, LANES), 1)
    for u in range(n):
        tile = i * n + u
        rows = pl.ds(u * MOBA_TILE, MOBA_TILE)
        o, lse = _softmax_block(qd_ref[rows, :].astype(BF16), kv_refs[u][:, :LANES], kv_refs[u][:, LANES:], False)
        part = jnp.where((lane // MOBA_HEAD_DIM) == tile_half_ref[tile], o, lse)
        o_ref[rows, :] = jnp.where(tile_real_ref[tile] > 0, part, NEG_BIG)


def _moba_grouped(qd, kv, tile_row, tile_pair, tile_half, tile_real):
    n = MOBA_TILES_PER_STEP
    n_tiles = qd.shape[0] // MOBA_TILE

    def kv_spec(u):
        return pl.BlockSpec((MOBA_BLOCK, 2 * LANES), lambda i, tr, tp, th, tl: (tr[i * n + u], tp[i * n + u]))

    grid_spec = pltpu.PrefetchScalarGridSpec(
        num_scalar_prefetch=4,
        grid=(n_tiles // n,),
        in_specs=[pl.BlockSpec((n * MOBA_TILE, LANES), lambda i, tr, tp, th, tl: (i, 0))]
        + [kv_spec(u) for u in range(n)],
        out_specs=pl.BlockSpec((n * MOBA_TILE, LANES), lambda i, tr, tp, th, tl: (i, 0)),
    )
    return pl.pallas_call(
        _moba_grouped_kernel,
        grid_spec=grid_spec,
        out_shape=jax.ShapeDtypeStruct(qd.shape, F32),
        compiler_params=_params("arbitrary"),
        name="moba_grouped",
    )(tile_row, tile_pair, tile_half, tile_real, qd, *([kv] * n))


def _moba_combine_kernel(od_ref, lsed_ref, *refs):
    g_refs, o_ref = refs[:-1], refs[-1]
    lane = lax.broadcasted_iota(jnp.int32, od_ref.shape, 1)
    is_a = lane < MOBA_HEAD_DIM
    parts = [(od_ref[...], lsed_ref[...])]
    for s in range(MOBA_TOPK):
        xa = g_refs[s][0, 0, 0]
        xb = g_refs[MOBA_TOPK + s][0, 0, 0]
        o = jnp.where(is_a, xa, xb)
        lse = pltpu.roll(jnp.where(is_a, xb, xa), MOBA_HEAD_DIM, 1)
        parts.append((o, lse))
    m = parts[0][1]
    for _, lse in parts[1:]:
        m = jnp.maximum(m, lse)
    num = jnp.zeros(od_ref.shape, F32)
    den = jnp.zeros(od_ref.shape, F32)
    for o, lse in parts:
        w = jnp.exp(lse - m)
        num = num + w * o
        den = den + w
    o_ref[...] = (num / den).astype(BF16)


def _moba_combine(od, lsed, gath, bsz, seq_len, tm=1024):
    tm = min(tm, seq_len)
    npair = od.shape[1] // LANES
    nt = seq_len // tm
    spec = pl.BlockSpec((tm, LANES), lambda b, p, i: (b * nt + i, p))

    def g_spec(e, s):
        return pl.BlockSpec((1, 1, 1, tm, LANES), lambda b, p, i: (b, 2 * p + e, s, i, 0))

    return pl.pallas_call(
        _moba_combine_kernel,
        grid=(bsz, npair, nt),
        in_specs=[spec, spec] + [g_spec(e, s) for e in range(2) for s in range(MOBA_TOPK)],
        out_specs=spec,
        out_shape=jax.ShapeDtypeStruct(od.shape, BF16),
        compiler_params=_params("parallel", "parallel", "parallel"),
        name="moba_combine",
    )(od, lsed, *([gath] * (2 * MOBA_TOPK)))


def _gather_rows(table, idx):
    return table.at[idx].get(mode='promise_in_bounds')


SC_WINDOW = 128


def _sc_mesh():
    return plsc.VectorSubcoreMesh(core_axis_name="core", subcore_axis_name="subcore")


def _sc_gather_rows(table, idx):
    n = idx.shape[0]
    d = table.shape[1]
    half = n // SC_WINDOW // 2

    @functools.partial(pl.kernel, out_type=jax.ShapeDtypeStruct((n, d), table.dtype), mesh=_sc_mesh())
    def gather_kernel(x_hbm, i_hbm, o_hbm):
        base = lax.axis_index("core") * half

        def body(i_vmem, o_vmem):
            pltpu.sync_copy(x_hbm.at[i_vmem.at[0]], o_vmem)

        pltpu.emit_pipeline(
            body,
            grid=(half,),
            in_specs=[pl.BlockSpec((1, SC_WINDOW), index_map=lambda i: (0, base + i))],
            out_specs=[pl.BlockSpec((SC_WINDOW, d), index_map=lambda i: (base + i, 0))],
            core_axis_name="subcore",
            dimension_semantics=(pltpu.PARALLEL,),
        )(i_hbm, o_hbm)

    return gather_kernel(table, idx.reshape(1, n))


def _sc_scatter_rows(rows, dest, n_out, repeat):
    n_src, d = rows.shape
    n = dest.shape[0]
    half = n // SC_WINDOW // 2
    src_windows = n_src // SC_WINDOW

    @functools.partial(pl.kernel, out_type=jax.ShapeDtypeStruct((n_out, d), rows.dtype), mesh=_sc_mesh())
    def scatter_kernel(x_hbm, i_hbm, o_hbm):
        base = lax.axis_index("core") * half

        def body(x_vmem, i_vmem):
            pltpu.sync_copy(x_vmem, o_hbm.at[i_vmem.at[0]])

        pltpu.emit_pipeline(
            body,
            grid=(half,),
            in_specs=[pl.BlockSpec((SC_WINDOW, d), index_map=lambda i: ((base + i) % src_windows, 0)),
                      pl.BlockSpec((1, SC_WINDOW), index_map=lambda i: (0, base + i))],
            out_specs=[],
            core_axis_name="subcore",
            dimension_semantics=(pltpu.PARALLEL,),
        )(x_hbm, i_hbm)

    assert n == repeat * n_src
    return scatter_kernel(rows, dest.reshape(1, n))


SORT_LAST = jnp.iinfo(jnp.int32).max


def _rows_by_sort(dest, src, counts, pcounts, tile, n_out):
    fill = counts[:, None] + jnp.arange(tile, dtype=jnp.int32)[None, :]
    first = (jnp.cumsum(pcounts) - pcounts)[:, None]
    fill_key = jnp.where(fill < pcounts[:, None], first + fill, SORT_LAST).reshape(-1)
    keys = jnp.concatenate([dest, fill_key])
    vals = jnp.concatenate([src, jnp.zeros(fill_key.shape, jnp.int32)])
    _, by_row = lax.sort((keys, vals), num_keys=1)
    return jnp.concatenate([by_row, jnp.zeros((n_out - by_row.shape[0],), jnp.int32)])


def _moba(q2, qh, kv, bsz, seq_len):
    heads = q2.shape[1] // MOBA_HEAD_DIM
    nbk = MOBA_MAX_BLOCKS
    sel, cnt = _moba_select(q2, kv, bsz, seq_len)
    od, lsed = _moba_diag(q2, kv, bsz, seq_len)
    counts = cnt[..., 0].astype(jnp.int32).reshape(bsz * heads * nbk)
    pcounts = ((counts + MOBA_TILE - 1) // MOBA_TILE) * MOBA_TILE
    pends = jnp.cumsum(pcounts)
    pstarts = (pends - pcounts).reshape(bsz, heads, 1, 1, nbk)
    n_items = bsz * heads * seq_len * MOBA_TOPK
    step_rows = MOBA_TILE * MOBA_TILES_PER_STEP
    n_rows = -(-(n_items + bsz * heads * nbk * MOBA_TILE) // step_rows) * step_rows
    n_null = step_rows
    idx = sel[:, :, 0:MOBA_TOPK, :]
    rank = sel[:, :, MOBA_TOPK:2 * MOBA_TOPK, :]
    start = jnp.sum(jnp.where(idx[..., None] == jnp.arange(nbk), pstarts, 0), axis=-1)
    null_row = n_rows + jnp.arange(seq_len, dtype=jnp.int32) % n_null
    dest = jnp.where(idx >= 0, start + rank, null_row)
    n_tiles = (n_rows + n_null) // MOBA_TILE
    n_groups = bsz * heads * nbk
    tile_start = jnp.arange(n_tiles, dtype=jnp.int32) * MOBA_TILE
    tile_g = jnp.minimum(jnp.sum((pends[None, :] <= tile_start[:, None]).astype(jnp.int32), axis=1), n_groups - 1)
    tile_real = (tile_start < pends[-1]).astype(jnp.int32)
    tile_head = (tile_g // nbk) % heads
    tile_row = (tile_g // (heads * nbk)) * (seq_len // MOBA_BLOCK) + jnp.minimum(tile_g % nbk, seq_len // MOBA_BLOCK - 1)
    dest_by_slot = dest.transpose(2, 1, 0, 3).reshape(-1)
    qd = _sc_scatter_rows(qh.reshape(-1, LANES), dest_by_slot, n_rows + n_null, MOBA_TOPK)
    part = _moba_grouped(qd, kv, tile_row, tile_head // 2, tile_head % 2, tile_real)
    gath = _sc_gather_rows(part, dest.reshape(-1)).reshape(bsz, heads, MOBA_TOPK, seq_len, LANES)
    return _moba_combine(od, lsed, gath, bsz, seq_len)


def _mem_kv_kernel(mem_ref, gmem_ref, w_ref, gck_ref, k_ref, v_ref):
    xf = mem_ref[0]
    ms = jnp.mean(xf * xf, axis=-1, keepdims=True)
    h = (xf * lax.rsqrt(ms + EPS) * gmem_ref[...]).astype(BF16)
    kv = _dot(h, w_ref[...])
    w = k_ref.shape[2]
    hd = w // X_HEADS
    for c in range(X_HEADS):
        chunk = kv[:, c * hd:(c + 1) * hd]
        cms = jnp.mean(chunk * chunk, axis=-1, keepdims=True)
        k_ref[0, :, c * hd:(c + 1) * hd] = (chunk * lax.rsqrt(cms + EPS) * gck_ref[...]).astype(BF16)
    v_ref[0] = kv[:, w:].astype(BF16)


def _mem_kv(mem, g_mem, w_kv_mem, g_ck):
    bsz, m, d = mem.shape
    w = w_kv_mem.shape[1] // 2
    const = lambda b: (0, 0)
    out = jax.ShapeDtypeStruct((bsz, m, w), BF16)
    return pl.pallas_call(
        _mem_kv_kernel,
        grid=(bsz,),
        in_specs=[pl.BlockSpec((1, m, d), lambda b: (b, 0, 0)), pl.BlockSpec((1, d), const),
                  pl.BlockSpec((d, 2 * w), const), pl.BlockSpec((1, w // X_HEADS), const)],
        out_specs=[pl.BlockSpec((1, m, w), lambda b: (b, 0, 0))] * 2,
        out_shape=[out, out],
        compiler_params=_params("parallel"),
        name="mem_kv",
    )(mem, g_mem.reshape(1, d), w_kv_mem.astype(BF16), g_ck.reshape(1, -1))


def _merge_kernel(x_ref, ys_ref, u_ref, dskip_ref, om_ref, xq_ref, kc_ref, vc_ref, g_ref,
                  wglu_ref, wmo_ref, wco_ref, wout_ref, gffn_ref, wr_ref, br_ref,
                  x1_ref, h2_ref, logit_ref):
    d = x_ref.shape[1]
    y = ys_ref[...].astype(F32) + dskip_ref[...] * u_ref[...].astype(F32)
    ge = 0.5 * y * (1.0 + jnp.tanh(math.sqrt(2.0 / math.pi) * (y + 0.044715 * (y * y * y))))
    z = _dot(ge.astype(BF16), wglu_ref[...])
    merged = g_ref[:, 0:d].astype(F32) * (z[:, :d] * _sigmoid(z[:, d:]))
    merged = merged + g_ref[:, d:2 * d].astype(F32) * _dot(om_ref[...], wmo_ref[...])
    w = xq_ref.shape[1]
    hd = w // X_HEADS
    heads = []
    for c in range(X_HEADS):
        s = _dot_nt(xq_ref[:, c * hd:(c + 1) * hd], kc_ref[0, :, c * hd:(c + 1) * hd]) * (hd ** -0.5)
        p = jnp.exp(s - jnp.max(s, axis=1, keepdims=True))
        p = p / jnp.sum(p, axis=1, keepdims=True)
        heads.append(_dot(p.astype(BF16), vc_ref[0, :, c * hd:(c + 1) * hd]))
    oc = jnp.concatenate(heads, axis=1).astype(BF16)
    merged = merged + g_ref[:, 2 * d:3 * d].astype(F32) * _dot(oc, wco_ref[...])
    x1 = x_ref[...] + _dot(merged.astype(BF16), wout_ref[...])
    x1_ref[...] = x1
    ms = jnp.mean(x1 * x1, axis=-1, keepdims=True)
    h2 = (x1 * lax.rsqrt(ms + EPS) * gffn_ref[...]).astype(BF16)
    h2_ref[...] = h2.astype(F32)
    logit_ref[...] = _dot(h2, wr_ref[...]) + br_ref[...]


def _merge(xt, ys, u, d_skip, om, xqn, kc, vc, gates, w_glu, w_mo, w_co, w_out, g_ffn, w_router, b_router,
           seq_len, tm=256):
    t, d = xt.shape
    w = d // 2
    m = kc.shape[1]
    ne = w_router.shape[1]
    nt = seq_len // tm
    row = lambda i: (i, 0)
    const = lambda i: (0, 0)
    per_b = lambda i: (i // nt, 0, 0)
    return pl.pallas_call(
        _merge_kernel,
        grid=(t // tm,),
        in_specs=[
            pl.BlockSpec((tm, d), row), pl.BlockSpec((tm, w), row), pl.BlockSpec((tm, w), row),
            pl.BlockSpec((1, w), const), pl.BlockSpec((tm, w), row), pl.BlockSpec((tm, w), row),
            pl.BlockSpec((1, m, w), per_b), pl.BlockSpec((1, m, w), per_b),
            pl.BlockSpec((tm, N_BRANCH * d), row),
            pl.BlockSpec((w, 2 * d), const), pl.BlockSpec((w, d), const), pl.BlockSpec((w, d), const),
            pl.BlockSpec((d, d), const), pl.BlockSpec((1, d), const),
            pl.BlockSpec((d, ne), const), pl.BlockSpec((1, ne), const),
        ],
        out_specs=[pl.BlockSpec((tm, d), row), pl.BlockSpec((tm, d), row), pl.BlockSpec((tm, ne), row)],
        out_shape=[jax.ShapeDtypeStruct((t, d), F32), jax.ShapeDtypeStruct((t, d), F32),
                   jax.ShapeDtypeStruct((t, ne), F32)],
        compiler_params=_params("parallel"),
        name="merge",
    )(xt, ys, u, d_skip.reshape(1, w), om, xqn, kc, vc, gates,
      w_glu.astype(BF16), w_mo.astype(BF16), w_co.astype(BF16), w_out.astype(BF16),
      g_ffn.reshape(1, d), w_router.astype(BF16), b_router.reshape(1, ne))


def _moe_kernel(blk_e_ref, blk_used_ref, xs_ref, wgu_ref, bgu_ref, wd_ref, bd_ref, y_ref, wgu_bf, wd_bf):
    i = pl.program_id(0)
    prev = blk_e_ref[jnp.maximum(i - 1, 0)]

    @pl.when((i == 0) | (blk_e_ref[i] != prev))
    def _():
        wgu_bf[...] = wgu_ref[0].astype(BF16)
        wd_bf[...] = wd_ref[0].astype(BF16)

    @pl.when(blk_used_ref[i] > 0)
    def _():
        de = wd_bf.shape[0]
        gu = _dot(xs_ref[...].astype(BF16), wgu_bf[...]) + bgu_ref[0]
        gate = jnp.minimum(gu[:, :de], SWIGLU_LIMIT)
        up = jnp.clip(gu[:, de:], -SWIGLU_LIMIT, SWIGLU_LIMIT)
        act = gate * _sigmoid(SWIGLU_ALPHA * gate) * (up + 1.0)
        y_ref[...] = _dot(act.astype(BF16), wd_bf[...]) + bd_ref[0]

    @pl.when(blk_used_ref[i] == 0)
    def _():
        y_ref[...] = jnp.zeros(y_ref.shape, y_ref.dtype)


def _moe_experts(xs, blk_e, blk_used, w_gu, b_gu, w_down, b_down):
    p, d = xs.shape
    ne, _, de2 = w_gu.shape
    de = de2 // 2
    nblk = p // EXPERT_ROWS
    grid_spec = pltpu.PrefetchScalarGridSpec(
        num_scalar_prefetch=2,
        grid=(nblk,),
        in_specs=[
            pl.BlockSpec((EXPERT_ROWS, d), lambda i, e, n: (i, 0)),
            pl.BlockSpec((1, d, de2), lambda i, e, n: (e[i], 0, 0)),
            pl.BlockSpec((1, 1, de2), lambda i, e, n: (e[i], 0, 0)),
            pl.BlockSpec((1, de, d), lambda i, e, n: (e[i], 0, 0)),
            pl.BlockSpec((1, 1, d), lambda i, e, n: (e[i], 0, 0)),
        ],
        out_specs=pl.BlockSpec((EXPERT_ROWS, d), lambda i, e, n: (i, 0)),
        scratch_shapes=[pltpu.VMEM((d, de2), BF16), pltpu.VMEM((de, d), BF16)],
    )
    return pl.pallas_call(
        _moe_kernel,
        grid_spec=grid_spec,
        out_shape=jax.ShapeDtypeStruct((p, d), F32),
        compiler_params=_params("arbitrary"),
        name="moe_experts",
    )(blk_e, blk_used, xs, w_gu, b_gu.reshape(ne, 1, de2), w_down, b_down.reshape(ne, 1, d))


def _moe_ffn(x1, h2, logits, w_gu, b_gu, w_down, b_down):
    t, d = x1.shape
    tk = t * TOPK_EXPERTS
    top_v, top_e = lax.top_k(logits, TOPK_EXPERTS)
    weights = jax.nn.softmax(top_v, axis=-1)
    e_flat = top_e.reshape(tk)
    expert_ids = jnp.arange(N_EXPERTS, dtype=e_flat.dtype)
    is_e = e_flat[:, None] == expert_ids[None, :]
    counts = jnp.sum(is_e.astype(jnp.int32), axis=0)
    order = jnp.argsort(e_flat, stable=True).astype(jnp.int32)
    pos = jnp.argsort(order).astype(jnp.int32)
    pcounts = ((counts + EXPERT_ROWS - 1) // EXPERT_ROWS) * EXPERT_ROWS
    pends = jnp.cumsum(pcounts)
    pstarts = pends - pcounts
    cstarts = jnp.cumsum(counts) - counts
    dest = jnp.sum(jnp.where(is_e, (pstarts - cstarts)[None, :], 0), axis=1) + pos
    nblk = -(-tk // EXPERT_ROWS) + N_EXPERTS
    blk_start = jnp.arange(nblk, dtype=jnp.int32) * EXPERT_ROWS
    blk_e = jnp.minimum(jnp.sum((pends[None, :] <= blk_start[:, None]).astype(jnp.int32), axis=1), N_EXPERTS - 1)
    blk_used = (blk_start < pends[-1]).astype(jnp.int32)
    tok = jnp.arange(tk, dtype=jnp.int32) // TOPK_EXPERTS
    slot_tok = _rows_by_sort(dest, tok, counts, pcounts, EXPERT_ROWS, nblk * EXPERT_ROWS)
    xs = _gather_rows(h2, slot_tok)
    ys = _moe_experts(xs, blk_e, blk_used, w_gu, b_gu, w_down, b_down)
    picked = _gather_rows(ys, dest.reshape(t, TOPK_EXPERTS).T.reshape(-1)).reshape(TOPK_EXPERTS, t, d)
    out = x1
    for k in range(TOPK_EXPERTS):
        out = out + picked[k] * weights[:, k:k + 1]
    return out


def kernel(x, mem, g_mix, w_in, lam_re, lam_im, log_dt, b_re, b_im, c_re, c_im, d_skip, w_glu, g_q, g_k, w_moba_out, g_mem, w_kv_mem, g_cq, g_ck, w_cross_out, w_out, g_ffn, w_router, b_router, w_gu, b_gu, w_down, b_down):
    bsz, seq_len, d = x.shape
    xt = x.reshape(bsz * seq_len, d)
    for l in range(g_mix.shape[0]):
        u, q2, kv, xqn, gates, qh = _in_proj(xt, g_mix[l], w_in[l], g_q[l], g_k[l], g_cq[l], seq_len)
        mats = _s5_matrices(lam_re[l], lam_im[l], log_dt[l], b_re[l], b_im[l], c_re[l], c_im[l],
                            seq_len // S5_CHUNK)
        ys = _s5(u, mats, bsz, seq_len)
        om = _moba(q2, qh, kv, bsz, seq_len)
        kc, vc = _mem_kv(mem, g_mem[l], w_kv_mem[l], g_ck[l])
        x1, h2, logits = _merge(xt, ys, u, d_skip[l], om, xqn, kc, vc, gates, w_glu[l], w_moba_out[l],
                                w_cross_out[l], w_out[l], g_ffn[l], w_router[l], b_router[l], seq_len)
        xt = _moe_ffn(x1, h2, logits, w_gu[l], b_gu[l], w_down[l], b_down[l])
    return xt.reshape(bsz, seq_len, d)
```

```python
import functools
import math

import jax
import jax.numpy as jnp
from jax import lax
from jax.experimental import pallas as pl
from jax.experimental.pallas import tpu as pltpu
from jax.experimental.pallas import tpu_sc as plsc

F32 = jnp.float32
BF16 = jnp.bfloat16

EPS = 1e-6
N_BRANCH = 3
SSM_GROUP = 16
SSM_STATE = 64
S5_CHUNK = 16
MOBA_HEAD_DIM = 64
MOBA_BLOCK = 256
MOBA_TOPK = 3
MOBA_MAX_BLOCKS = 64
ROPE_THETA = 10000.0
X_HEADS = 4
N_EXPERTS = 32
TOPK_EXPERTS = 4
SWIGLU_LIMIT = 7.0
SWIGLU_ALPHA = 1.702
EXPERT_ROWS = 512
NEG_BIG = -1e30
LANES = 128
VMEM_LIMIT_BYTES = 56 * 1024 * 1024


def _params(*sem):
    return pltpu.CompilerParams(dimension_semantics=sem, vmem_limit_bytes=VMEM_LIMIT_BYTES)


def _sigmoid(x):
    return 1.0 / (1.0 + jnp.exp(-x))


def _dot(a, b):
    return jnp.dot(a, b, preferred_element_type=F32)


def _dot_nt(a, b):
    return lax.dot_general(a, b, (((1,), (1,)), ((), ())), preferred_element_type=F32)


def _inproj_kernel(x_ref, gmix_ref, wa_ref, wg_ref, e64_ref, gq_ref, gk_ref, gcq_ref, cos_ref, sin_ref,
                   u_ref, q_ref, kv_ref, xq_ref, g_ref, qh_ref):
    xf = x_ref[...]
    ms = jnp.mean(xf * xf, axis=-1, keepdims=True)
    h = (xf * lax.rsqrt(ms + EPS) * gmix_ref[...]).astype(BF16)
    a = _dot(h, wa_ref[...])
    w = u_ref.shape[1]
    u_ref[...] = a[:, :w]

    cos = jnp.tile(cos_ref[...], (1, w // LANES))
    sin = jnp.tile(sin_ref[...], (1, w // LANES))
    lane = lax.broadcasted_iota(jnp.int32, (xf.shape[0], w), 1)
    first_half = (lane % MOBA_HEAD_DIM) < (MOBA_HEAD_DIM // 2)

    def qk_norm_rope(raw, g):
        ss = _dot((raw * raw).astype(BF16), e64_ref[...])
        n = raw * lax.rsqrt(ss * (1.0 / MOBA_HEAD_DIM) + EPS) * g
        rot = jnp.where(first_half,
                        pltpu.roll(n, w - MOBA_HEAD_DIM // 2, 1),
                        pltpu.roll(n, MOBA_HEAD_DIM // 2, 1))
        return n * cos + rot * sin

    q = qk_norm_rope(a[:, w:2 * w], gq_ref[...]) * (MOBA_HEAD_DIM ** -0.5)
    q_ref[...] = q.astype(BF16)
    in_a = lax.broadcasted_iota(jnp.int32, (xf.shape[0], LANES), 1) < MOBA_HEAD_DIM
    for p in range(w // LANES):
        pair = q[:, p * LANES:(p + 1) * LANES].astype(BF16).astype(F32)
        qh_ref[2 * p] = jnp.where(in_a, pair, 0.0)
        qh_ref[2 * p + 1] = jnp.where(in_a, 0.0, pair)
    k = qk_norm_rope(a[:, 2 * w:3 * w], gk_ref[...]).astype(BF16)
    for p in range(w // LANES):
        kv_ref[:, (2 * p) * LANES:(2 * p + 1) * LANES] = k[:, p * LANES:(p + 1) * LANES]
        kv_ref[:, (2 * p + 1) * LANES:(2 * p + 2) * LANES] = a[:, 3 * w + p * LANES:3 * w + (p + 1) * LANES].astype(BF16)

    xq = a[:, 4 * w:5 * w]
    hd = w // X_HEADS
    for c in range(X_HEADS):
        chunk = xq[:, c * hd:(c + 1) * hd]
        cms = jnp.mean(chunk * chunk, axis=-1, keepdims=True)
        xq_ref[:, c * hd:(c + 1) * hd] = (chunk * lax.rsqrt(cms + EPS) * gcq_ref[...]).astype(BF16)

    d = xf.shape[1]
    for c in range(N_BRANCH):
        z = _dot(h, wg_ref[:, c * d:(c + 1) * d])
        g_ref[:, c * d:(c + 1) * d] = _sigmoid(z).astype(BF16)


def _in_proj(xt, g_mix, w_in, g_q, g_k, g_cq, seq_len, tm=256):
    t, d = xt.shape
    w = d // 2
    wa = w_in[:, :5 * w].astype(BF16)
    wg = w_in[:, 5 * w:].astype(BF16)
    heads = w // MOBA_HEAD_DIM
    e64 = jnp.kron(jnp.eye(heads, dtype=F32), jnp.ones((MOBA_HEAD_DIM, MOBA_HEAD_DIM), F32)).astype(BF16)
    half = MOBA_HEAD_DIM // 2
    inv = ROPE_THETA ** (-jnp.arange(half, dtype=F32) / half)
    ang = jnp.arange(seq_len, dtype=F32)[:, None] * inv[None, :]
    cos = jnp.tile(jnp.cos(ang), (1, LANES // half))
    sin = jnp.tile(jnp.concatenate([-jnp.sin(ang), jnp.sin(ang)], axis=1), (1, LANES // MOBA_HEAD_DIM))
    nt = seq_len // tm
    row = lambda i: (i, 0)
    const = lambda i: (0, 0)
    out_w = jax.ShapeDtypeStruct((t, w), BF16)
    return pl.pallas_call(
        _inproj_kernel,
        grid=(t // tm,),
        in_specs=[
            pl.BlockSpec((tm, d), row),
            pl.BlockSpec((1, d), const),
            pl.BlockSpec((d, 5 * w), const),
            pl.BlockSpec((d, N_BRANCH * d), const),
            pl.BlockSpec((w, w), const),
            pl.BlockSpec((1, w), const),
            pl.BlockSpec((1, w), const),
            pl.BlockSpec((1, w // X_HEADS), const),
            pl.BlockSpec((tm, LANES), lambda i: (i % nt, 0)),
            pl.BlockSpec((tm, LANES), lambda i: (i % nt, 0)),
        ],
        out_specs=[pl.BlockSpec((tm, w), row), pl.BlockSpec((tm, w), row), pl.BlockSpec((tm, 2 * w), row),
                   pl.BlockSpec((tm, w), row), pl.BlockSpec((tm, N_BRANCH * d), row),
                   pl.BlockSpec((heads, tm, LANES), lambda i: (0, i, 0))],
        out_shape=[jax.ShapeDtypeStruct((t, w), F32), out_w, jax.ShapeDtypeStruct((t, 2 * w), BF16), out_w,
                   jax.ShapeDtypeStruct((t, N_BRANCH * d), BF16),
                   jax.ShapeDtypeStruct((heads, t, LANES), F32)],
        compiler_params=_params("parallel"),
        name="in_proj",
    )(xt, g_mix.reshape(1, d), wa, wg, e64,
      jnp.tile(g_q, heads).reshape(1, w), jnp.tile(g_k, heads).reshape(1, w), g_cq.reshape(1, -1), cos, sin)


def _s5_matrices(lam_re, lam_im, log_dt, b_re, b_im, c_re, c_im, n_chunks):
    hp = lax.Precision.HIGHEST
    c = S5_CHUNK
    dt = jnp.exp(log_dt)[:, None]
    mag = jnp.exp(lam_re * dt)
    ar = mag * jnp.cos(lam_im * dt)
    ai = mag * jnp.sin(lam_im * dt)
    nr = ar - 1.0
    den = lam_re * lam_re + lam_im * lam_im
    cr = (nr * lam_re + ai * lam_im) / den
    ci = (ai * lam_re - nr * lam_im) / den
    bbr = cr[..., None] * b_re - ci[..., None] * b_im
    bbi = cr[..., None] * b_im + ci[..., None] * b_re

    def power(n):
        nf = n.astype(F32)[None, :, None]
        m = jnp.exp((lam_re * dt)[:, None, :] * nf)
        th = (lam_im * dt)[:, None, :] * nf
        return m * jnp.cos(th), m * jnp.sin(th)

    pr, pi = power(jnp.arange(c + 1))
    kbr = pr[..., None] * bbr[:, None] - pi[..., None] * bbi[:, None]
    kbi = pr[..., None] * bbi[:, None] + pi[..., None] * bbr[:, None]
    kk = (jnp.einsum('ghp,gtpc->gthc', c_re, kbr, precision=hp)
          - jnp.einsum('ghp,gtpc->gthc', c_im, kbi, precision=hp))
    tq = jnp.arange(c)
    lag = tq[None, :] - tq[:, None]
    toep = kk[:, jnp.clip(lag, 0, c)]
    toep = jnp.where((lag >= 0)[None, :, :, None, None], toep, 0.0)
    g = toep.shape[0]
    toep = toep.transpose(0, 1, 4, 2, 3).reshape(g, c * SSM_GROUP, c * SSM_GROUP)
    rev = c - 1 - tq
    w_in = jnp.concatenate([kbr[:, rev].transpose(0, 1, 3, 2), kbi[:, rev].transpose(0, 1, 3, 2)], axis=-1)
    w_in = w_in.reshape(g, c * SSM_GROUP, 2 * SSM_STATE)
    prn, pin = pr[:, 1:], pi[:, 1:]
    wo_r = c_re[:, None] * prn[:, :, None, :] - c_im[:, None] * pin[:, :, None, :]
    wo_i = -c_re[:, None] * pin[:, :, None, :] - c_im[:, None] * prn[:, :, None, :]
    w_out = jnp.concatenate([wo_r, wo_i], axis=-1).transpose(0, 3, 1, 2).reshape(g, 2 * SSM_STATE, c * SSM_GROUP)
    n_steps = max(1, int(math.ceil(math.log2(n_chunks))))
    qr, qi = power(c * (2 ** jnp.arange(n_steps)))
    pa = jnp.concatenate([qr, qr], axis=-1)
    pb = jnp.concatenate([-qi, qi], axis=-1)
    return toep.astype(BF16), w_in.astype(BF16), w_out.astype(BF16), pa, pb


S5_SET = LANES // SSM_GROUP
S5_ROWS = 256


def _s5_kernel(u_ref, toep_ref, win_ref, wout_ref, pa_ref, pb_ref, y_ref, s_ref):
    c = S5_CHUNK
    nc = u_ref.shape[0] // c
    rb = min(S5_ROWS, nc)

    def chunk_rows(r0):
        return jnp.concatenate([u_ref[pl.ds(r0 * c + t, rb, stride=c), :] for t in range(c)], axis=1).astype(BF16)

    for blk in range(nc // rb):
        s_ref[blk * rb:(blk + 1) * rb, :] = _dot(chunk_rows(blk * rb), win_ref[0])
    row = lax.broadcasted_iota(jnp.int32, (nc, LANES), 0)
    n_steps = pa_ref.shape[1]
    for g in range(S5_SET):
        lanes = slice(g * LANES, (g + 1) * LANES)
        s = s_ref[:, lanes]
        for k in range(n_steps):
            sh = 1 << k
            if sh >= nc:
                break
            prev = jnp.where(row >= sh, pltpu.roll(s, sh, 0), 0.0)
            s = s + pa_ref[0, k:k + 1, lanes] * prev + pb_ref[0, k:k + 1, lanes] * pltpu.roll(prev, SSM_STATE, 1)
        s_ref[:, lanes] = jnp.where(row >= 1, pltpu.roll(s, 1, 0), 0.0)
    for blk in range(nc // rb):
        rows = slice(blk * rb, (blk + 1) * rb)
        y = _dot(chunk_rows(blk * rb), toep_ref[0]) + _dot(s_ref[rows, :].astype(BF16), wout_ref[0])
        for t in range(c):
            y_ref[pl.ds(blk * rb * c + t, rb, stride=c), :] = y[:, t * LANES:(t + 1) * LANES]


def _s5_block_diag(mats):
    toep, w_in, w_out, pa, pb = mats
    g, c, h, p2 = toep.shape[0], S5_CHUNK, SSM_GROUP, 2 * SSM_STATE
    ns = g // S5_SET
    eye = jnp.eye(S5_SET, dtype=toep.dtype)
    toep_bd = jnp.einsum('sgacbd,gh->sagcbhd', toep.reshape(ns, S5_SET, c, h, c, h), eye)
    win_bd = jnp.einsum('sgacp,gh->sagchp', w_in.reshape(ns, S5_SET, c, h, p2), eye)
    wout_bd = jnp.einsum('sgpbd,gh->sgpbhd', w_out.reshape(ns, S5_SET, p2, c, h), eye)
    k = pa.shape[1]
    lanes_of = lambda x: x.reshape(ns, S5_SET, k, p2).transpose(0, 2, 1, 3).reshape(ns, k, S5_SET * p2)
    return (toep_bd.reshape(ns, c * LANES, c * LANES), win_bd.reshape(ns, c * LANES, S5_SET * p2),
            wout_bd.reshape(ns, S5_SET * p2, c * LANES), lanes_of(pa), lanes_of(pb))


def _s5(u, mats, bsz, seq_len):
    toep, w_in, w_out, pa, pb = _s5_block_diag(mats)
    ns = toep.shape[0]
    nc = seq_len // S5_CHUNK
    per_set = lambda s, b: (s, 0, 0)
    once = dict(pipeline_mode=pl.Buffered(1))
    io_spec = pl.BlockSpec((seq_len, LANES), lambda s, b: (b, s), **once)
    return pl.pallas_call(
        _s5_kernel,
        grid=(ns, bsz),
        in_specs=[
            io_spec,
            pl.BlockSpec((1,) + toep.shape[1:], per_set, **once),
            pl.BlockSpec((1,) + w_in.shape[1:], per_set, **once),
            pl.BlockSpec((1,) + w_out.shape[1:], per_set, **once),
            pl.BlockSpec((1,) + pa.shape[1:], per_set),
            pl.BlockSpec((1,) + pb.shape[1:], per_set),
        ],
        out_specs=io_spec,
        out_shape=jax.ShapeDtypeStruct(u.shape, F32),
        scratch_shapes=[pltpu.VMEM((nc, S5_SET * 2 * SSM_STATE), F32)],
        compiler_params=_params("arbitrary", "arbitrary"),
        name="s5_scan",
    )(u, toep, w_in, w_out, pa, pb)


def _moba_select_kernel(q_ref, k_ref, tri_ref, sel_ref, cnt_out_ref, km_ref, cnt_ref):
    h = pl.program_id(1)
    t = pl.program_id(2)
    ts = q_ref.shape[0]
    nbk = MOBA_MAX_BLOCKS

    @pl.when(t == 0)
    def _():
        kk = k_ref[...].astype(F32)
        nb = kk.shape[0] // MOBA_BLOCK
        km = jnp.sum(kk.reshape(nb, MOBA_BLOCK, LANES), axis=1) * (1.0 / MOBA_BLOCK)
        if nb < nbk:
            km = jnp.concatenate([km, jnp.zeros((nbk - nb, LANES), F32)], axis=0)
        lane = lax.broadcasted_iota(jnp.int32, (nbk, LANES), 1)
        km_ref[...] = jnp.where((lane // MOBA_HEAD_DIM) == (h % 2), km, 0.0).astype(BF16)
        cnt_ref[...] = jnp.zeros(cnt_ref.shape, F32)

    gate = _dot_nt(km_ref[...], q_ref[...])
    blk = lax.broadcasted_iota(jnp.int32, gate.shape, 0)
    qblk = (t * ts + lax.broadcasted_iota(jnp.int32, gate.shape, 1)) // MOBA_BLOCK
    g = jnp.where(blk < qblk, gate, -jnp.inf)
    selected = jnp.zeros(gate.shape, jnp.bool_)
    picks = []
    for _ in range(MOBA_TOPK):
        mx = jnp.max(g, axis=0, keepdims=True)
        idx = jnp.min(jnp.where(g == mx, blk, nbk), axis=0, keepdims=True)
        hit = blk == idx
        ok = (idx[0:1] < qblk[0:1]) & (mx > -jnp.inf)
        picks.append((hit, idx, ok))
        selected = selected | (hit & ok)
        g = jnp.where(hit, -jnp.inf, g)
    sel01 = jnp.where(selected, 1.0, 0.0).astype(BF16)
    prefix = _dot(sel01, tri_ref[...]) + jnp.tile(cnt_ref[...], (1, ts // LANES))
    rows = [jnp.where(ok, idx, -1) for (_, idx, ok) in picks]
    rows += [jnp.sum(jnp.where(hit, prefix, 0.0), axis=0, keepdims=True).astype(jnp.int32) for (hit, _, _) in picks]
    rows += [jnp.zeros((1, ts), jnp.int32)] * (8 - 2 * MOBA_TOPK)
    sel_ref[0, 0] = jnp.concatenate(rows, axis=0)
    cnt_ref[...] = cnt_ref[...] + _dot(sel01, jnp.ones((ts, LANES), BF16))
    cnt_out_ref[0, 0] = cnt_ref[...]


def _moba_select(q2, kv, bsz, seq_len, ts=1024):
    ts = min(ts, seq_len)
    heads = q2.shape[1] // MOBA_HEAD_DIM
    nt = seq_len // ts
    tri = (jnp.arange(ts)[:, None] < jnp.arange(ts)[None, :]).astype(BF16)
    return pl.pallas_call(
        _moba_select_kernel,
        grid=(bsz, heads, nt),
        in_specs=[
            pl.BlockSpec((ts, LANES), lambda b, h, t: (b * nt + t, h // 2)),
            pl.BlockSpec((seq_len, LANES), lambda b, h, t: (b, 2 * (h // 2))),
            pl.BlockSpec((ts, ts), lambda b, h, t: (0, 0)),
        ],
        out_specs=[pl.BlockSpec((1, 1, 8, ts), lambda b, h, t: (b, h, 0, t)),
                   pl.BlockSpec((1, 1, MOBA_MAX_BLOCKS, LANES), lambda b, h, t: (b, h, 0, 0))],
        out_shape=[jax.ShapeDtypeStruct((bsz, heads, 8, seq_len), jnp.int32),
                   jax.ShapeDtypeStruct((bsz, heads, MOBA_MAX_BLOCKS, LANES), F32)],
        scratch_shapes=[pltpu.VMEM((MOBA_MAX_BLOCKS, LANES), BF16), pltpu.VMEM((MOBA_MAX_BLOCKS, LANES), F32)],
        compiler_params=_params("parallel", "arbitrary", "arbitrary"),
        name="moba_select",
    )(q2, kv, tri)


def _softmax_block(q, kb, vb, causal):
    s = _dot_nt(q, kb)
    if causal:
        r = lax.broadcasted_iota(jnp.int32, s.shape, 0) % MOBA_BLOCK
        c = lax.broadcasted_iota(jnp.int32, s.shape, 1)
        s = jnp.where(c <= r, s, NEG_BIG)
    m = jnp.max(s, axis=1, keepdims=True)
    p = jnp.exp(s - m).astype(BF16)
    pv = _dot(p, jnp.concatenate([vb, jnp.ones(vb.shape, BF16)], axis=1))
    l = pv[:, LANES:LANES + 1]
    return pv[:, :LANES] / l, m + jnp.log(l)


MOBA_DIAG_BLOCKS = 2


def _moba_diag_kernel(q_ref, kv_ref, o_ref, lse_ref):
    lane = lax.broadcasted_iota(jnp.int32, (MOBA_BLOCK, LANES), 1)
    is_a = lane < MOBA_HEAD_DIM
    for u in range(MOBA_DIAG_BLOCKS):
        rows = pl.ds(u * MOBA_BLOCK, MOBA_BLOCK)
        q = q_ref[rows, :]
        zero = jnp.zeros_like(q)
        qs = jnp.concatenate([jnp.where(is_a, q, zero), jnp.where(is_a, zero, q)], axis=0)
        o, lse = _softmax_block(qs, kv_ref[rows, :LANES], kv_ref[rows, LANES:], True)
        o_ref[rows, :] = jnp.where(is_a, o[:MOBA_BLOCK], o[MOBA_BLOCK:])
        lse_ref[rows, :] = jnp.where(is_a, lse[:MOBA_BLOCK], lse[MOBA_BLOCK:])


def _moba_diag(q2, kv, bsz, seq_len):
    npair = q2.shape[1] // LANES
    rows = MOBA_BLOCK * MOBA_DIAG_BLOCKS
    nb = seq_len // rows
    spec = pl.BlockSpec((rows, LANES), lambda b, p, i: (b * nb + i, p))
    out = jax.ShapeDtypeStruct(q2.shape, F32)
    return pl.pallas_call(
        _moba_diag_kernel,
        grid=(bsz, npair, nb),
        in_specs=[spec, pl.BlockSpec((rows, 2 * LANES), lambda b, p, i: (b * nb + i, p))],
        out_specs=[spec, spec],
        out_shape=[out, out],
        compiler_params=_params("parallel", "parallel", "parallel"),
        name="moba_diag",
    )(q2, kv)


MOBA_TILE = 256
MOBA_TILES_PER_STEP = 8


def _moba_grouped_kernel(tile_row_ref, tile_pair_ref, tile_half_ref, tile_real_ref, qd_ref, *refs):
    n = MOBA_TILES_PER_STEP
    kv_refs, o_ref = refs[:n], refs[n]
    i = pl.program_id(0)
    lane = lax.broadcasted_iota(jnp.int32, (MOBA_TILE, LANES), 1)
    for u in range(n):
        tile = i * n + u
        rows = pl.ds(u * MOBA_TILE, MOBA_TILE)
        o, lse = _softmax_block(qd_ref[rows, :].astype(BF16), kv_refs[u][:, :LANES], kv_refs[u][:, LANES:], False)
        part = jnp.where((lane // MOBA_HEAD_DIM) == tile_half_ref[tile], o, lse)
        o_ref[rows, :] = jnp.where(tile_real_ref[tile] > 0, part, NEG_BIG)


def _moba_grouped(qd, kv, tile_row, tile_pair, tile_half, tile_real):
    n = MOBA_TILES_PER_STEP
    n_tiles = qd.shape[0] // MOBA_TILE

    def kv_spec(u):
        return pl.BlockSpec((MOBA_BLOCK, 2 * LANES), lambda i, tr, tp, th, tl: (tr[i * n + u], tp[i * n + u]))

    grid_spec = pltpu.PrefetchScalarGridSpec(
        num_scalar_prefetch=4,
        grid=(n_tiles // n,),
        in_specs=[pl.BlockSpec((n * MOBA_TILE, LANES), lambda i, tr, tp, th, tl: (i, 0))]
        + [kv_spec(u) for u in range(n)],
        out_specs=pl.BlockSpec((n * MOBA_TILE, LANES), lambda i, tr, tp, th, tl: (i, 0)),
    )
    return pl.pallas_call(
        _moba_grouped_kernel,
        grid_spec=grid_spec,
        out_shape=jax.ShapeDtypeStruct(qd.shape, F32),
        compiler_params=_params("arbitrary"),
        name="moba_grouped",
    )(tile_row, tile_pair, tile_half, tile_real, qd, *([kv] * n))


def _moba_combine_kernel(od_ref, lsed_ref, *refs):
    g_refs, o_ref = refs[:-1], refs[-1]
    lane = lax.broadcasted_iota(jnp.int32, od_ref.shape, 1)
    is_a = lane < MOBA_HEAD_DIM
    parts = [(od_ref[...], lsed_ref[...])]
    for s in range(MOBA_TOPK):
        xa = g_refs[s][0, 0, 0]
        xb = g_refs[MOBA_TOPK + s][0, 0, 0]
        o = jnp.where(is_a, xa, xb)
        lse = pltpu.roll(jnp.where(is_a, xb, xa), MOBA_HEAD_DIM, 1)
        parts.append((o, lse))
    m = parts[0][1]
    for _, lse in parts[1:]:
        m = jnp.maximum(m, lse)
    num = jnp.zeros(od_ref.shape, F32)
    den = jnp.zeros(od_ref.shape, F32)
    for o, lse in parts:
        w = jnp.exp(lse - m)
        num = num + w * o
        den = den + w
    o_ref[...] = (num / den).astype(BF16)


def _moba_combine(od, lsed, gath, bsz, seq_len, tm=1024):
    tm = min(tm, seq_len)
    npair = od.shape[1] // LANES
    nt = seq_len // tm
    spec = pl.BlockSpec((tm, LANES), lambda b, p, i: (b * nt + i, p))

    def g_spec(e, s):
        return pl.BlockSpec((1, 1, 1, tm, LANES), lambda b, p, i: (b, 2 * p + e, s, i, 0))

    return pl.pallas_call(
        _moba_combine_kernel,
        grid=(bsz, npair, nt),
        in_specs=[spec, spec] + [g_spec(e, s) for e in range(2) for s in range(MOBA_TOPK)],
        out_specs=spec,
        out_shape=jax.ShapeDtypeStruct(od.shape, BF16),
        compiler_params=_params("parallel", "parallel", "parallel"),
        name="moba_combine",
    )(od, lsed, *([gath] * (2 * MOBA_TOPK)))


SC_WINDOW = 128
SC_CORES = 2
SC_SUBCORES = 16


def _sc_mesh():
    return plsc.VectorSubcoreMesh(core_axis_name="core", subcore_axis_name="subcore")


def _sc_gather_rows(table, idx):
    n = idx.shape[0]
    d = table.shape[1]
    window = SC_WINDOW
    assert n % (window * SC_CORES * SC_SUBCORES) == 0
    per_core = n // window // SC_CORES

    @functools.partial(pl.kernel, out_type=jax.ShapeDtypeStruct((n, d), table.dtype), mesh=_sc_mesh())
    def gather_kernel(x_hbm, i_hbm, o_hbm):
        base = lax.axis_index("core") * per_core

        def body(i_vmem, o_vmem):
            pltpu.sync_copy(x_hbm.at[i_vmem.at[0]], o_vmem)

        pltpu.emit_pipeline(
            body,
            grid=(per_core,),
            in_specs=[pl.BlockSpec((1, window), index_map=lambda i: (0, base + i))],
            out_specs=[pl.BlockSpec((window, d), index_map=lambda i: (base + i, 0))],
            core_axis_name="subcore",
            dimension_semantics=(pltpu.PARALLEL,),
            trace_scopes=False,
        )(i_hbm, o_hbm)

    return gather_kernel(table, idx.reshape(1, n))


def _sc_scatter_rows(rows, dest, n_out, repeat):
    n_src, d = rows.shape
    n = dest.shape[0]
    assert n == repeat * n_src
    window = SC_WINDOW
    assert n % (window * SC_CORES * SC_SUBCORES) == 0
    per_core = n // window // SC_CORES
    src_windows = n_src // window

    @functools.partial(pl.kernel, out_type=jax.ShapeDtypeStruct((n_out, d), rows.dtype), mesh=_sc_mesh())
    def scatter_kernel(x_hbm, i_hbm, o_hbm):
        base = lax.axis_index("core") * per_core

        def body(x_vmem, i_vmem):
            pltpu.sync_copy(x_vmem, o_hbm.at[i_vmem.at[0]])

        pltpu.emit_pipeline(
            body,
            grid=(per_core,),
            in_specs=[pl.BlockSpec((window, d), index_map=lambda i: ((base + i) % src_windows, 0)),
                      pl.BlockSpec((1, window), index_map=lambda i: (0, base + i))],
            out_specs=[],
            core_axis_name="subcore",
            dimension_semantics=(pltpu.PARALLEL,),
            trace_scopes=False,
        )(x_hbm, i_hbm)

    return scatter_kernel(rows, dest.reshape(1, n))


def _moba(q2, qh, kv, bsz, seq_len):
    heads = q2.shape[1] // MOBA_HEAD_DIM
    nbk = MOBA_MAX_BLOCKS
    sel, cnt = _moba_select(q2, kv, bsz, seq_len)
    od, lsed = _moba_diag(q2, kv, bsz, seq_len)
    counts = cnt[..., 0].astype(jnp.int32).reshape(bsz * heads * nbk)
    pcounts = ((counts + MOBA_TILE - 1) // MOBA_TILE) * MOBA_TILE
    pends = jnp.cumsum(pcounts)
    pstarts = (pends - pcounts).reshape(bsz, heads, 1, 1, nbk)
    n_items = bsz * heads * seq_len * MOBA_TOPK
    step_rows = MOBA_TILE * MOBA_TILES_PER_STEP
    n_rows = -(-(n_items + bsz * heads * nbk * MOBA_TILE) // step_rows) * step_rows
    n_null = step_rows
    idx = sel[:, :, 0:MOBA_TOPK, :]
    rank = sel[:, :, MOBA_TOPK:2 * MOBA_TOPK, :]
    start = jnp.sum(jnp.where(idx[..., None] == jnp.arange(nbk), pstarts, 0), axis=-1)
    null_row = n_rows + jnp.arange(seq_len, dtype=jnp.int32) % n_null
    dest = jnp.where(idx >= 0, start + rank, null_row)
    n_tiles = (n_rows + n_null) // MOBA_TILE
    n_groups = bsz * heads * nbk
    tile_start = jnp.arange(n_tiles, dtype=jnp.int32) * MOBA_TILE
    tile_g = jnp.minimum(jnp.sum((pends[None, :] <= tile_start[:, None]).astype(jnp.int32), axis=1), n_groups - 1)
    tile_real = (tile_start < pends[-1]).astype(jnp.int32)
    tile_head = (tile_g // nbk) % heads
    tile_row = (tile_g // (heads * nbk)) * (seq_len // MOBA_BLOCK) + jnp.minimum(tile_g % nbk, seq_len // MOBA_BLOCK - 1)
    dest_by_slot = dest.transpose(2, 1, 0, 3).reshape(-1)
    qd = _sc_scatter_rows(qh.reshape(-1, LANES), dest_by_slot, n_rows + n_null, MOBA_TOPK)
    part = _moba_grouped(qd, kv, tile_row, tile_head // 2, tile_head % 2, tile_real)
    gath = _sc_gather_rows(part, dest.reshape(-1)).reshape(bsz, heads, MOBA_TOPK, seq_len, LANES)
    return _moba_combine(od, lsed, gath, bsz, seq_len)


def _mem_kv_kernel(mem_ref, gmem_ref, w_ref, gck_ref, k_ref, v_ref):
    xf = mem_ref[0]
    ms = jnp.mean(xf * xf, axis=-1, keepdims=True)
    h = (xf * lax.rsqrt(ms + EPS) * gmem_ref[...]).astype(BF16)
    kv = _dot(h, w_ref[...])
    w = k_ref.shape[2]
    hd = w // X_HEADS
    for c in range(X_HEADS):
        chunk = kv[:, c * hd:(c + 1) * hd]
        cms = jnp.mean(chunk * chunk, axis=-1, keepdims=True)
        k_ref[0, :, c * hd:(c + 1) * hd] = (chunk * lax.rsqrt(cms + EPS) * gck_ref[...]).astype(BF16)
    v_ref[0] = kv[:, w:].astype(BF16)


def _mem_kv(mem, g_mem, w_kv_mem, g_ck):
    bsz, m, d = mem.shape
    w = w_kv_mem.shape[1] // 2
    const = lambda b: (0, 0)
    out = jax.ShapeDtypeStruct((bsz, m, w), BF16)
    return pl.pallas_call(
        _mem_kv_kernel,
        grid=(bsz,),
        in_specs=[pl.BlockSpec((1, m, d), lambda b: (b, 0, 0)), pl.BlockSpec((1, d), const),
                  pl.BlockSpec((d, 2 * w), const), pl.BlockSpec((1, w // X_HEADS), const)],
        out_specs=[pl.BlockSpec((1, m, w), lambda b: (b, 0, 0))] * 2,
        out_shape=[out, out],
        compiler_params=_params("parallel"),
        name="mem_kv",
    )(mem, g_mem.reshape(1, d), w_kv_mem.astype(BF16), g_ck.reshape(1, -1))


def _merge_kernel(x_ref, ys_ref, u_ref, dskip_ref, om_ref, xq_ref, kc_ref, vc_ref, g_ref,
                  wglu_ref, wmo_ref, wco_ref, wout_ref, gffn_ref, wr_ref, br_ref,
                  x1_ref, h2_ref, logit_ref):
    d = x_ref.shape[1]
    y = ys_ref[...].astype(F32) + dskip_ref[...] * u_ref[...].astype(F32)
    ge = 0.5 * y * (1.0 + jnp.tanh(math.sqrt(2.0 / math.pi) * (y + 0.044715 * (y * y * y))))
    z = _dot(ge.astype(BF16), wglu_ref[...])
    merged = g_ref[:, 0:d].astype(F32) * (z[:, :d] * _sigmoid(z[:, d:]))
    merged = merged + g_ref[:, d:2 * d].astype(F32) * _dot(om_ref[...], wmo_ref[...])
    w = xq_ref.shape[1]
    hd = w // X_HEADS
    heads = []
    for c in range(X_HEADS):
        s = _dot_nt(xq_ref[:, c * hd:(c + 1) * hd], kc_ref[0, :, c * hd:(c + 1) * hd]) * (hd ** -0.5)
        p = jnp.exp(s - jnp.max(s, axis=1, keepdims=True))
        p = p / jnp.sum(p, axis=1, keepdims=True)
        heads.append(_dot(p.astype(BF16), vc_ref[0, :, c * hd:(c + 1) * hd]))
    oc = jnp.concatenate(heads, axis=1).astype(BF16)
    merged = merged + g_ref[:, 2 * d:3 * d].astype(F32) * _dot(oc, wco_ref[...])
    x1 = x_ref[...] + _dot(merged.astype(BF16), wout_ref[...])
    x1_ref[...] = x1
    ms = jnp.mean(x1 * x1, axis=-1, keepdims=True)
    h2 = (x1 * lax.rsqrt(ms + EPS) * gffn_ref[...]).astype(BF16)
    for j in range(d // LANES):
        h2_ref[j] = h2[:, j * LANES:(j + 1) * LANES].astype(F32)
    logit_ref[...] = _dot(h2, wr_ref[...]) + br_ref[...]


def _merge(xt, ys, u, d_skip, om, xqn, kc, vc, gates, w_glu, w_mo, w_co, w_out, g_ffn, w_router, b_router,
           seq_len, tm=256):
    t, d = xt.shape
    w = d // 2
    m = kc.shape[1]
    ne = w_router.shape[1]
    nt = seq_len // tm
    row = lambda i: (i, 0)
    const = lambda i: (0, 0)
    per_b = lambda i: (i // nt, 0, 0)
    return pl.pallas_call(
        _merge_kernel,
        grid=(t // tm,),
        in_specs=[
            pl.BlockSpec((tm, d), row), pl.BlockSpec((tm, w), row), pl.BlockSpec((tm, w), row),
            pl.BlockSpec((1, w), const), pl.BlockSpec((tm, w), row), pl.BlockSpec((tm, w), row),
            pl.BlockSpec((1, m, w), per_b), pl.BlockSpec((1, m, w), per_b),
            pl.BlockSpec((tm, N_BRANCH * d), row),
            pl.BlockSpec((w, 2 * d), const), pl.BlockSpec((w, d), const), pl.BlockSpec((w, d), const),
            pl.BlockSpec((d, d), const), pl.BlockSpec((1, d), const),
            pl.BlockSpec((d, ne), const), pl.BlockSpec((1, ne), const),
        ],
        out_specs=[pl.BlockSpec((tm, d), row), pl.BlockSpec((d // LANES, tm, LANES), lambda i: (0, i, 0)),
                   pl.BlockSpec((tm, ne), row)],
        out_shape=[jax.ShapeDtypeStruct((t, d), F32), jax.ShapeDtypeStruct((d // LANES, t, LANES), F32),
                   jax.ShapeDtypeStruct((t, ne), F32)],
        compiler_params=_params("parallel"),
        name="merge",
    )(xt, ys, u, d_skip.reshape(1, w), om, xqn, kc, vc, gates,
      w_glu.astype(BF16), w_mo.astype(BF16), w_co.astype(BF16), w_out.astype(BF16),
      g_ffn.reshape(1, d), w_router.astype(BF16), b_router.reshape(1, ne))


def _moe_kernel(blk_e_ref, blk_used_ref, xs_ref, wgu_ref, bgu_ref, wd_ref, bd_ref, y_ref, wgu_bf, wd_bf):
    i = pl.program_id(0)
    prev = blk_e_ref[jnp.maximum(i - 1, 0)]

    @pl.when((i == 0) | (blk_e_ref[i] != prev))
    def _():
        wgu_bf[...] = wgu_ref[0].astype(BF16)
        wd_bf[...] = wd_ref[0].astype(BF16)

    @pl.when(blk_used_ref[i] > 0)
    def _():
        de = wd_bf.shape[0]
        xs = jnp.concatenate([xs_ref[j] for j in range(xs_ref.shape[0])], axis=1).astype(BF16)
        gu = _dot(xs, wgu_bf[...]) + bgu_ref[0]
        gate = jnp.minimum(gu[:, :de], SWIGLU_LIMIT)
        up = jnp.clip(gu[:, de:], -SWIGLU_LIMIT, SWIGLU_LIMIT)
        act = gate * _sigmoid(SWIGLU_ALPHA * gate) * (up + 1.0)
        y = _dot(act.astype(BF16), wd_bf[...]) + bd_ref[0]
        for j in range(y_ref.shape[0]):
            y_ref[j] = y[:, j * LANES:(j + 1) * LANES]

    @pl.when(blk_used_ref[i] == 0)
    def _():
        y_ref[...] = jnp.zeros(y_ref.shape, y_ref.dtype)


def _moe_experts(xs, blk_e, blk_used, w_gu, b_gu, w_down, b_down):
    slabs, p, _ = xs.shape
    d = slabs * LANES
    ne, _, de2 = w_gu.shape
    de = de2 // 2
    nblk = p // EXPERT_ROWS
    row_spec = pl.BlockSpec((slabs, EXPERT_ROWS, LANES), lambda i, e, n: (0, i, 0))
    grid_spec = pltpu.PrefetchScalarGridSpec(
        num_scalar_prefetch=2,
        grid=(nblk,),
        in_specs=[
            row_spec,
            pl.BlockSpec((1, d, de2), lambda i, e, n: (e[i], 0, 0)),
            pl.BlockSpec((1, 1, de2), lambda i, e, n: (e[i], 0, 0)),
            pl.BlockSpec((1, de, d), lambda i, e, n: (e[i], 0, 0)),
            pl.BlockSpec((1, 1, d), lambda i, e, n: (e[i], 0, 0)),
        ],
        out_specs=row_spec,
        scratch_shapes=[pltpu.VMEM((d, de2), BF16), pltpu.VMEM((de, d), BF16)],
    )
    return pl.pallas_call(
        _moe_kernel,
        grid_spec=grid_spec,
        out_shape=jax.ShapeDtypeStruct(xs.shape, F32),
        compiler_params=_params("arbitrary"),
        name="moe_experts",
    )(blk_e, blk_used, xs, w_gu, b_gu.reshape(ne, 1, de2), w_down, b_down.reshape(ne, 1, d))


def _moe_ffn(x1, h2, logits, w_gu, b_gu, w_down, b_down):
    t, d = x1.shape
    tk = t * TOPK_EXPERTS
    top_v, top_e = lax.top_k(logits, TOPK_EXPERTS)
    weights = jax.nn.softmax(top_v, axis=-1)
    e_flat = top_e.reshape(tk)
    expert_ids = jnp.arange(N_EXPERTS, dtype=e_flat.dtype)
    is_e = e_flat[:, None] == expert_ids[None, :]
    counts = jnp.sum(is_e.astype(jnp.int32), axis=0)
    order = jnp.argsort(e_flat, stable=True).astype(jnp.int32)
    pos = jnp.argsort(order).astype(jnp.int32)
    pcounts = ((counts + EXPERT_ROWS - 1) // EXPERT_ROWS) * EXPERT_ROWS
    pends = jnp.cumsum(pcounts)
    pstarts = pends - pcounts
    cstarts = jnp.cumsum(counts) - counts
    dest = jnp.sum(jnp.where(is_e, (pstarts - cstarts)[None, :], 0), axis=1) + pos
    nblk = -(-tk // EXPERT_ROWS) + N_EXPERTS
    blk_start = jnp.arange(nblk, dtype=jnp.int32) * EXPERT_ROWS
    blk_e = jnp.minimum(jnp.sum((pends[None, :] <= blk_start[:, None]).astype(jnp.int32), axis=1), N_EXPERTS - 1)
    blk_used = (blk_start < pends[-1]).astype(jnp.int32)
    slabs = d // LANES
    p = nblk * EXPERT_ROWS
    dest_by_k = dest.reshape(t, TOPK_EXPERTS).T
    slab_off = jnp.arange(slabs, dtype=jnp.int32) * p
    dest_kst = (dest_by_k[:, None, :] + slab_off[None, :, None]).reshape(-1)
    dest_skt = (dest_by_k[None, :, :] + slab_off[:, None, None]).reshape(-1)
    xs = _sc_scatter_rows(h2.reshape(slabs * t, LANES), dest_kst, slabs * p, TOPK_EXPERTS)
    ys = _moe_experts(xs.reshape(slabs, p, LANES), blk_e, blk_used, w_gu, b_gu, w_down, b_down)
    picked = _sc_gather_rows(ys.reshape(slabs * p, LANES), dest_skt).reshape(slabs, TOPK_EXPERTS, t, LANES)
    mixed = [sum(picked[j, k] * weights[:, k:k + 1] for k in range(TOPK_EXPERTS)) for j in range(slabs)]
    return x1 + jnp.concatenate(mixed, axis=1)


def kernel(x, mem, g_mix, w_in, lam_re, lam_im, log_dt, b_re, b_im, c_re, c_im, d_skip, w_glu, g_q, g_k, w_moba_out, g_mem, w_kv_mem, g_cq, g_ck, w_cross_out, w_out, g_ffn, w_router, b_router, w_gu, b_gu, w_down, b_down):
    bsz, seq_len, d = x.shape
    xt = x.reshape(bsz * seq_len, d)
    for l in range(g_mix.shape[0]):
        u, q2, kv, xqn, gates, qh = _in_proj(xt, g_mix[l], w_in[l], g_q[l], g_k[l], g_cq[l], seq_len)
        mats = _s5_matrices(lam_re[l], lam_im[l], log_dt[l], b_re[l], b_im[l], c_re[l], c_im[l],
                            seq_len // S5_CHUNK)
        ys = _s5(u, mats, bsz, seq_len)
        om = _moba(q2, qh, kv, bsz, seq_len)
        kc, vc = _mem_kv(mem, g_mem[l], w_kv_mem[l], g_ck[l])
        x1, h2, logits = _merge(xt, ys, u, d_skip[l], om, xqn, kc, vc, gates, w_glu[l], w_moba_out[l],
                                w_cross_out[l], w_out[l], g_ffn[l], w_router[l], b_router[l], seq_len)
        xt = _moe_ffn(x1, h2, logits, w_gu[l], b_gu[l], w_down[l], b_down[l])
    return xt.reshape(bsz, seq_len, d)
```

```python
import functools
import math

import jax
import jax.numpy as jnp
from jax import lax
from jax.experimental import pallas as pl
from jax.experimental.pallas import tpu as pltpu
from jax.experimental.pallas import tpu_sc as plsc

F32 = jnp.float32
BF16 = jnp.bfloat16

EPS = 1e-6
N_BRANCH = 3
SSM_GROUP = 16
SSM_STATE = 64
S5_CHUNK = 16
MOBA_HEAD_DIM = 64
MOBA_BLOCK = 256
MOBA_TOPK = 3
MOBA_MAX_BLOCKS = 64
ROPE_THETA = 10000.0
X_HEADS = 4
N_EXPERTS = 32
TOPK_EXPERTS = 4
SWIGLU_LIMIT = 7.0
SWIGLU_ALPHA = 1.702
EXPERT_ROWS = 512
NEG_BIG = -1e30
LANES = 128
VMEM_LIMIT_BYTES = 56 * 1024 * 1024


def _params(*sem):
    return pltpu.CompilerParams(dimension_semantics=sem, vmem_limit_bytes=VMEM_LIMIT_BYTES)


def _sigmoid(x):
    return 1.0 / (1.0 + jnp.exp(-x))


def _dot(a, b):
    return jnp.dot(a, b, preferred_element_type=F32)


def _dot_nt(a, b):
    return lax.dot_general(a, b, (((1,), (1,)), ((), ())), preferred_element_type=F32)


def _inproj_kernel(x_ref, gmix_ref, wa_ref, wg_ref, e64_ref, gq_ref, gk_ref, gcq_ref, cos_ref, sin_ref,
                   u_ref, q_ref, kv_ref, xq_ref, g_ref, qh_ref):
    xf = x_ref[...]
    ms = jnp.mean(xf * xf, axis=-1, keepdims=True)
    h = (xf * lax.rsqrt(ms + EPS) * gmix_ref[...]).astype(BF16)
    a = _dot(h, wa_ref[...])
    w = u_ref.shape[1]
    u_ref[...] = a[:, :w]

    cos = jnp.tile(cos_ref[...], (1, w // LANES))
    sin = jnp.tile(sin_ref[...], (1, w // LANES))
    lane = lax.broadcasted_iota(jnp.int32, (xf.shape[0], w), 1)
    first_half = (lane % MOBA_HEAD_DIM) < (MOBA_HEAD_DIM // 2)

    def qk_norm_rope(raw, g):
        ss = _dot((raw * raw).astype(BF16), e64_ref[...])
        n = raw * lax.rsqrt(ss * (1.0 / MOBA_HEAD_DIM) + EPS) * g
        rot = jnp.where(first_half,
                        pltpu.roll(n, w - MOBA_HEAD_DIM // 2, 1),
                        pltpu.roll(n, MOBA_HEAD_DIM // 2, 1))
        return n * cos + rot * sin

    q = qk_norm_rope(a[:, w:2 * w], gq_ref[...]) * (MOBA_HEAD_DIM ** -0.5)
    q_ref[...] = q.astype(BF16)
    in_a = lax.broadcasted_iota(jnp.int32, (xf.shape[0], LANES), 1) < MOBA_HEAD_DIM
    for p in range(w // LANES):
        pair = q[:, p * LANES:(p + 1) * LANES].astype(BF16).astype(F32)
        qh_ref[2 * p] = jnp.where(in_a, pair, 0.0)
        qh_ref[2 * p + 1] = jnp.where(in_a, 0.0, pair)
    k = qk_norm_rope(a[:, 2 * w:3 * w], gk_ref[...]).astype(BF16)
    for p in range(w // LANES):
        kv_ref[:, (2 * p) * LANES:(2 * p + 1) * LANES] = k[:, p * LANES:(p + 1) * LANES]
        kv_ref[:, (2 * p + 1) * LANES:(2 * p + 2) * LANES] = a[:, 3 * w + p * LANES:3 * w + (p + 1) * LANES].astype(BF16)

    xq = a[:, 4 * w:5 * w]
    hd = w // X_HEADS
    for c in range(X_HEADS):
        chunk = xq[:, c * hd:(c + 1) * hd]
        cms = jnp.mean(chunk * chunk, axis=-1, keepdims=True)
        xq_ref[:, c * hd:(c + 1) * hd] = (chunk * lax.rsqrt(cms + EPS) * gcq_ref[...]).astype(BF16)

    d = xf.shape[1]
    for c in range(N_BRANCH):
        z = _dot(h, wg_ref[:, c * d:(c + 1) * d])
        g_ref[:, c * d:(c + 1) * d] = _sigmoid(z).astype(BF16)


def _in_proj(xt, g_mix, w_in, g_q, g_k, g_cq, seq_len, tm=256):
    t, d = xt.shape
    w = d // 2
    wa = w_in[:, :5 * w].astype(BF16)
    wg = w_in[:, 5 * w:].astype(BF16)
    heads = w // MOBA_HEAD_DIM
    e64 = jnp.kron(jnp.eye(heads, dtype=F32), jnp.ones((MOBA_HEAD_DIM, MOBA_HEAD_DIM), F32)).astype(BF16)
    half = MOBA_HEAD_DIM // 2
    inv = ROPE_THETA ** (-jnp.arange(half, dtype=F32) / half)
    ang = jnp.arange(seq_len, dtype=F32)[:, None] * inv[None, :]
    cos = jnp.tile(jnp.cos(ang), (1, LANES // half))
    sin = jnp.tile(jnp.concatenate([-jnp.sin(ang), jnp.sin(ang)], axis=1), (1, LANES // MOBA_HEAD_DIM))
    nt = seq_len // tm
    row = lambda i: (i, 0)
    const = lambda i: (0, 0)
    out_w = jax.ShapeDtypeStruct((t, w), BF16)
    return pl.pallas_call(
        _inproj_kernel,
        grid=(t // tm,),
        in_specs=[
            pl.BlockSpec((tm, d), row),
            pl.BlockSpec((1, d), const),
            pl.BlockSpec((d, 5 * w), const),
            pl.BlockSpec((d, N_BRANCH * d), const),
            pl.BlockSpec((w, w), const),
            pl.BlockSpec((1, w), const),
            pl.BlockSpec((1, w), const),
            pl.BlockSpec((1, w // X_HEADS), const),
            pl.BlockSpec((tm, LANES), lambda i: (i % nt, 0)),
            pl.BlockSpec((tm, LANES), lambda i: (i % nt, 0)),
        ],
        out_specs=[pl.BlockSpec((tm, w), row), pl.BlockSpec((tm, w), row), pl.BlockSpec((tm, 2 * w), row),
                   pl.BlockSpec((tm, w), row), pl.BlockSpec((tm, N_BRANCH * d), row),
                   pl.BlockSpec((heads, tm, LANES), lambda i: (0, i, 0))],
        out_shape=[jax.ShapeDtypeStruct((t, w), F32), out_w, jax.ShapeDtypeStruct((t, 2 * w), BF16), out_w,
                   jax.ShapeDtypeStruct((t, N_BRANCH * d), BF16),
                   jax.ShapeDtypeStruct((heads, t, LANES), F32)],
        compiler_params=_params("parallel"),
        name="in_proj",
    )(xt, g_mix.reshape(1, d), wa, wg, e64,
      jnp.tile(g_q, heads).reshape(1, w), jnp.tile(g_k, heads).reshape(1, w), g_cq.reshape(1, -1), cos, sin)


def _s5_matrices(lam_re, lam_im, log_dt, b_re, b_im, c_re, c_im, n_chunks):
    hp = lax.Precision.HIGHEST
    c = S5_CHUNK
    dt = jnp.exp(log_dt)[:, None]
    mag = jnp.exp(lam_re * dt)
    ar = mag * jnp.cos(lam_im * dt)
    ai = mag * jnp.sin(lam_im * dt)
    nr = ar - 1.0
    den = lam_re * lam_re + lam_im * lam_im
    cr = (nr * lam_re + ai * lam_im) / den
    ci = (ai * lam_re - nr * lam_im) / den
    bbr = cr[..., None] * b_re - ci[..., None] * b_im
    bbi = cr[..., None] * b_im + ci[..., None] * b_re

    def power(n):
        nf = n.astype(F32)[None, :, None]
        m = jnp.exp((lam_re * dt)[:, None, :] * nf)
        th = (lam_im * dt)[:, None, :] * nf
        return m * jnp.cos(th), m * jnp.sin(th)

    pr, pi = power(jnp.arange(c + 1))
    kbr = pr[..., None] * bbr[:, None] - pi[..., None] * bbi[:, None]
    kbi = pr[..., None] * bbi[:, None] + pi[..., None] * bbr[:, None]
    kk = (jnp.einsum('ghp,gtpc->gthc', c_re, kbr, precision=hp)
          - jnp.einsum('ghp,gtpc->gthc', c_im, kbi, precision=hp))
    tq = jnp.arange(c)
    g = kk.shape[0]
    rev = c - 1 - tq
    w_in = jnp.concatenate([kbr[:, rev].transpose(0, 1, 3, 2), kbi[:, rev].transpose(0, 1, 3, 2)], axis=-1)
    w_in = w_in.reshape(g, c * SSM_GROUP, 2 * SSM_STATE)
    prn, pin = pr[:, 1:], pi[:, 1:]
    wo_r = c_re[:, None] * prn[:, :, None, :] - c_im[:, None] * pin[:, :, None, :]
    wo_i = -c_re[:, None] * pin[:, :, None, :] - c_im[:, None] * prn[:, :, None, :]
    w_out = jnp.concatenate([wo_r, wo_i], axis=-1).transpose(0, 3, 1, 2).reshape(g, 2 * SSM_STATE, c * SSM_GROUP)
    n_steps = max(1, int(math.ceil(math.log2(n_chunks))))
    qr, qi = power(c * (2 ** jnp.arange(n_steps)))
    pa = jnp.concatenate([qr, qr], axis=-1)
    pb = jnp.concatenate([-qi, qi], axis=-1)
    return kk.astype(BF16), w_in.astype(BF16), w_out.astype(BF16), pa, pb


S5_SET = LANES // SSM_GROUP
S5_ROWS = 256


def _s5_kernel(u_ref, toep_ref, win_ref, wout_ref, pa_ref, pb_ref, y_ref, s_ref):
    c = S5_CHUNK
    nc = u_ref.shape[0] // c
    rb = min(S5_ROWS, nc)

    def chunk_rows(r0):
        return jnp.concatenate([u_ref[pl.ds(r0 * c + t, rb, stride=c), :] for t in range(c)], axis=1).astype(BF16)

    for blk in range(nc // rb):
        s_ref[blk * rb:(blk + 1) * rb, :] = _dot(chunk_rows(blk * rb), win_ref[0])
    row = lax.broadcasted_iota(jnp.int32, (nc, LANES), 0)
    n_steps = pa_ref.shape[1]
    for g in range(S5_SET):
        lanes = slice(g * LANES, (g + 1) * LANES)
        s = s_ref[:, lanes]
        for k in range(n_steps):
            sh = 1 << k
            if sh >= nc:
                break
            prev = jnp.where(row >= sh, pltpu.roll(s, sh, 0), 0.0)
            s = s + pa_ref[0, k:k + 1, lanes] * prev + pb_ref[0, k:k + 1, lanes] * pltpu.roll(prev, SSM_STATE, 1)
        s_ref[:, lanes] = jnp.where(row >= 1, pltpu.roll(s, 1, 0), 0.0)
    for blk in range(nc // rb):
        rows = slice(blk * rb, (blk + 1) * rb)
        y = _dot(chunk_rows(blk * rb), toep_ref[0]) + _dot(s_ref[rows, :].astype(BF16), wout_ref[0])
        for t in range(c):
            y_ref[pl.ds(blk * rb * c + t, rb, stride=c), :] = y[:, t * LANES:(t + 1) * LANES]


def _s5_block_diag(mats):
    kk, w_in, w_out, pa, pb = mats
    g, c, h, p2 = kk.shape[0], S5_CHUNK, SSM_GROUP, 2 * SSM_STATE
    ns = g // S5_SET
    eye = jnp.eye(S5_SET, dtype=kk.dtype)
    lag_bd = jnp.einsum('sglhc,gk->slgckh', kk.reshape(ns, S5_SET, c + 1, h, h), eye).reshape(ns, c + 1, LANES, LANES)
    tq = jnp.arange(c)
    lag = tq[None, :] - tq[:, None]
    toep_bd = jnp.where((lag >= 0)[None, :, :, None, None], lag_bd[:, jnp.clip(lag, 0, c)], 0)
    toep_bd = toep_bd.transpose(0, 1, 3, 2, 4).reshape(ns, c * LANES, c * LANES)

    def rows_to_states(w):
        wc = w.reshape(ns, S5_SET, c, h, p2).transpose(0, 2, 1, 3, 4).reshape(ns, c, LANES, p2)
        same = (jnp.arange(LANES)[:, None] // h) == (jnp.arange(S5_SET * p2)[None, :] // p2)
        return jnp.where(same, jnp.tile(wc, (1, 1, 1, S5_SET)), 0).reshape(ns, c * LANES, S5_SET * p2)

    win_bd = rows_to_states(w_in)
    wout_bd = rows_to_states(w_out.transpose(0, 2, 1)).transpose(0, 2, 1)
    k = pa.shape[1]
    lanes_of = lambda x: x.reshape(ns, S5_SET, k, p2).transpose(0, 2, 1, 3).reshape(ns, k, S5_SET * p2)
    return toep_bd, win_bd, wout_bd, lanes_of(pa), lanes_of(pb)


def _s5(u, mats, bsz, seq_len):
    toep, w_in, w_out, pa, pb = _s5_block_diag(mats)
    ns = toep.shape[0]
    nc = seq_len // S5_CHUNK
    per_set = lambda s, b: (s, 0, 0)
    once = dict(pipeline_mode=pl.Buffered(1))
    io_spec = pl.BlockSpec((seq_len, LANES), lambda s, b: (b, s), **once)
    return pl.pallas_call(
        _s5_kernel,
        grid=(ns, bsz),
        in_specs=[
            io_spec,
            pl.BlockSpec((1,) + toep.shape[1:], per_set, **once),
            pl.BlockSpec((1,) + w_in.shape[1:], per_set, **once),
            pl.BlockSpec((1,) + w_out.shape[1:], per_set, **once),
            pl.BlockSpec((1,) + pa.shape[1:], per_set),
            pl.BlockSpec((1,) + pb.shape[1:], per_set),
        ],
        out_specs=io_spec,
        out_shape=jax.ShapeDtypeStruct(u.shape, F32),
        scratch_shapes=[pltpu.VMEM((nc, S5_SET * 2 * SSM_STATE), F32)],
        compiler_params=_params("arbitrary", "arbitrary"),
        name="s5_scan",
    )(u, toep, w_in, w_out, pa, pb)


def _moba_select_kernel(q_ref, k_ref, tri_ref, sel_ref, cnt_out_ref, km_ref, cnt_ref):
    h = pl.program_id(1)
    t = pl.program_id(2)
    ts = q_ref.shape[0]
    nbk = MOBA_MAX_BLOCKS

    @pl.when(t == 0)
    def _():
        kk = k_ref[...].astype(F32)
        nb = kk.shape[0] // MOBA_BLOCK
        km = jnp.sum(kk.reshape(nb, MOBA_BLOCK, LANES), axis=1) * (1.0 / MOBA_BLOCK)
        if nb < nbk:
            km = jnp.concatenate([km, jnp.zeros((nbk - nb, LANES), F32)], axis=0)
        lane = lax.broadcasted_iota(jnp.int32, (nbk, LANES), 1)
        km_ref[...] = jnp.where((lane // MOBA_HEAD_DIM) == (h % 2), km, 0.0).astype(BF16)
        cnt_ref[...] = jnp.zeros(cnt_ref.shape, F32)

    gate = _dot_nt(km_ref[...], q_ref[...])
    blk = lax.broadcasted_iota(jnp.int32, gate.shape, 0)
    qblk = (t * ts + lax.broadcasted_iota(jnp.int32, gate.shape, 1)) // MOBA_BLOCK
    g = jnp.where(blk < qblk, gate, -jnp.inf)
    selected = jnp.zeros(gate.shape, jnp.bool_)
    picks = []
    for _ in range(MOBA_TOPK):
        mx = jnp.max(g, axis=0, keepdims=True)
        idx = jnp.min(jnp.where(g == mx, blk, nbk), axis=0, keepdims=True)
        hit = blk == idx
        ok = (idx[0:1] < qblk[0:1]) & (mx > -jnp.inf)
        picks.append((hit, idx, ok))
        selected = selected | (hit & ok)
        g = jnp.where(hit, -jnp.inf, g)
    sel01 = jnp.where(selected, 1.0, 0.0).astype(BF16)
    prefix = _dot(sel01, tri_ref[...]) + jnp.tile(cnt_ref[...], (1, ts // LANES))
    rows = [jnp.where(ok, idx, -1) for (_, idx, ok) in picks]
    rows += [jnp.sum(jnp.where(hit, prefix, 0.0), axis=0, keepdims=True).astype(jnp.int32) for (hit, _, _) in picks]
    rows += [jnp.zeros((1, ts), jnp.int32)] * (8 - 2 * MOBA_TOPK)
    sel_ref[0, 0] = jnp.concatenate(rows, axis=0)
    cnt_ref[...] = cnt_ref[...] + _dot(sel01, jnp.ones((ts, LANES), BF16))
    cnt_out_ref[0, 0] = cnt_ref[...]


def _moba_select(q2, kv, bsz, seq_len, ts=1024):
    ts = min(ts, seq_len)
    heads = q2.shape[1] // MOBA_HEAD_DIM
    nt = seq_len // ts
    tri = (jnp.arange(ts)[:, None] < jnp.arange(ts)[None, :]).astype(BF16)
    return pl.pallas_call(
        _moba_select_kernel,
        grid=(bsz, heads, nt),
        in_specs=[
            pl.BlockSpec((ts, LANES), lambda b, h, t: (b * nt + t, h // 2)),
            pl.BlockSpec((seq_len, LANES), lambda b, h, t: (b, 2 * (h // 2))),
            pl.BlockSpec((ts, ts), lambda b, h, t: (0, 0)),
        ],
        out_specs=[pl.BlockSpec((1, 1, 8, ts), lambda b, h, t: (b, h, 0, t)),
                   pl.BlockSpec((1, 1, MOBA_MAX_BLOCKS, LANES), lambda b, h, t: (b, h, 0, 0))],
        out_shape=[jax.ShapeDtypeStruct((bsz, heads, 8, seq_len), jnp.int32),
                   jax.ShapeDtypeStruct((bsz, heads, MOBA_MAX_BLOCKS, LANES), F32)],
        scratch_shapes=[pltpu.VMEM((MOBA_MAX_BLOCKS, LANES), BF16), pltpu.VMEM((MOBA_MAX_BLOCKS, LANES), F32)],
        compiler_params=_params("parallel", "arbitrary", "arbitrary"),
        name="moba_select",
    )(q2, kv, tri)


def _softmax_block(q, kb, vb, causal):
    s = _dot_nt(q, kb)
    if causal:
        r = lax.broadcasted_iota(jnp.int32, s.shape, 0) % MOBA_BLOCK
        c = lax.broadcasted_iota(jnp.int32, s.shape, 1)
        s = jnp.where(c <= r, s, NEG_BIG)
    m = jnp.max(s, axis=1, keepdims=True)
    p = jnp.exp(s - m).astype(BF16)
    pv = _dot(p, jnp.concatenate([vb, jnp.ones(vb.shape, BF16)], axis=1))
    l = pv[:, LANES:LANES + 1]
    return pv[:, :LANES] / l, m + jnp.log(l)


MOBA_DIAG_BLOCKS = 2


def _moba_diag_kernel(q_ref, kv_ref, o_ref, lse_ref):
    lane = lax.broadcasted_iota(jnp.int32, (MOBA_BLOCK, LANES), 1)
    is_a = lane < MOBA_HEAD_DIM
    for u in range(MOBA_DIAG_BLOCKS):
        rows = pl.ds(u * MOBA_BLOCK, MOBA_BLOCK)
        q = q_ref[rows, :]
        zero = jnp.zeros_like(q)
        qs = jnp.concatenate([jnp.where(is_a, q, zero), jnp.where(is_a, zero, q)], axis=0)
        o, lse = _softmax_block(qs, kv_ref[rows, :LANES], kv_ref[rows, LANES:], True)
        o_ref[rows, :] = jnp.where(is_a, o[:MOBA_BLOCK], o[MOBA_BLOCK:])
        lse_ref[rows, :] = jnp.where(is_a, lse[:MOBA_BLOCK], lse[MOBA_BLOCK:])


def _moba_diag(q2, kv, bsz, seq_len):
    npair = q2.shape[1] // LANES
    rows = MOBA_BLOCK * MOBA_DIAG_BLOCKS
    nb = seq_len // rows
    spec = pl.BlockSpec((rows, LANES), lambda b, p, i: (b * nb + i, p))
    out = jax.ShapeDtypeStruct(q2.shape, F32)
    return pl.pallas_call(
        _moba_diag_kernel,
        grid=(bsz, npair, nb),
        in_specs=[spec, pl.BlockSpec((rows, 2 * LANES), lambda b, p, i: (b * nb + i, p))],
        out_specs=[spec, spec],
        out_shape=[out, out],
        compiler_params=_params("parallel", "parallel", "parallel"),
        name="moba_diag",
    )(q2, kv)


MOBA_TILE = 512
MOBA_TILES_PER_STEP = 4


def _moba_grouped_kernel(tile_row_ref, tile_pair_ref, tile_half_ref, tile_real_ref, qd_ref, *refs):
    n = MOBA_TILES_PER_STEP
    kv_refs, o_ref = refs[:n], refs[n]
    i = pl.program_id(0)
    lane = lax.broadcasted_iota(jnp.int32, (MOBA_TILE, LANES), 1)
    for u in range(n):
        tile = i * n + u
        rows = pl.ds(u * MOBA_TILE, MOBA_TILE)
        o, lse = _softmax_block(qd_ref[rows, :].astype(BF16), kv_refs[u][:, :LANES], kv_refs[u][:, LANES:], False)
        part = jnp.where((lane // MOBA_HEAD_DIM) == tile_half_ref[tile], o, lse)
        o_ref[rows, :] = jnp.where(tile_real_ref[tile] > 0, part, NEG_BIG)


def _moba_grouped(qd, kv, tile_row, tile_pair, tile_half, tile_real):
    n = MOBA_TILES_PER_STEP
    n_tiles = qd.shape[0] // MOBA_TILE

    def kv_spec(u):
        return pl.BlockSpec((MOBA_BLOCK, 2 * LANES), lambda i, tr, tp, th, tl: (tr[i * n + u], tp[i * n + u]))

    grid_spec = pltpu.PrefetchScalarGridSpec(
        num_scalar_prefetch=4,
        grid=(n_tiles // n,),
        in_specs=[pl.BlockSpec((n * MOBA_TILE, LANES), lambda i, tr, tp, th, tl: (i, 0))]
        + [kv_spec(u) for u in range(n)],
        out_specs=pl.BlockSpec((n * MOBA_TILE, LANES), lambda i, tr, tp, th, tl: (i, 0)),
    )
    return pl.pallas_call(
        _moba_grouped_kernel,
        grid_spec=grid_spec,
        out_shape=jax.ShapeDtypeStruct(qd.shape, F32),
        compiler_params=_params("arbitrary"),
        name="moba_grouped",
    )(tile_row, tile_pair, tile_half, tile_real, qd, *([kv] * n))


def _moba_combine_kernel(od_ref, lsed_ref, *refs):
    g_refs, o_ref = refs[:-1], refs[-1]
    lane = lax.broadcasted_iota(jnp.int32, od_ref.shape, 1)
    is_a = lane < MOBA_HEAD_DIM
    parts = [(od_ref[...], lsed_ref[...])]
    for s in range(MOBA_TOPK):
        xa = g_refs[s][0, 0, 0]
        xb = g_refs[MOBA_TOPK + s][0, 0, 0]
        o = jnp.where(is_a, xa, xb)
        lse = pltpu.roll(jnp.where(is_a, xb, xa), MOBA_HEAD_DIM, 1)
        parts.append((o, lse))
    m = parts[0][1]
    for _, lse in parts[1:]:
        m = jnp.maximum(m, lse)
    num = jnp.zeros(od_ref.shape, F32)
    den = jnp.zeros(od_ref.shape, F32)
    for o, lse in parts:
        w = jnp.exp(lse - m)
        num = num + w * o
        den = den + w
    o_ref[...] = (num / den).astype(BF16)


def _moba_combine(od, lsed, gath, bsz, seq_len, tm=1024):
    tm = min(tm, seq_len)
    npair = od.shape[1] // LANES
    nt = seq_len // tm
    spec = pl.BlockSpec((tm, LANES), lambda b, p, i: (b * nt + i, p))

    def g_spec(e, s):
        return pl.BlockSpec((1, 1, 1, tm, LANES), lambda b, p, i: (b, 2 * p + e, s, i, 0))

    return pl.pallas_call(
        _moba_combine_kernel,
        grid=(bsz, npair, nt),
        in_specs=[spec, spec] + [g_spec(e, s) for e in range(2) for s in range(MOBA_TOPK)],
        out_specs=spec,
        out_shape=jax.ShapeDtypeStruct(od.shape, BF16),
        compiler_params=_params("parallel", "parallel", "parallel"),
        name="moba_combine",
    )(od, lsed, *([gath] * (2 * MOBA_TOPK)))


SC_WINDOW = 128
SC_CORES = 2
SC_SUBCORES = 16


def _sc_mesh():
    return plsc.VectorSubcoreMesh(core_axis_name="core", subcore_axis_name="subcore")


def _sc_gather_rows(table, idx):
    n = idx.shape[0]
    d = table.shape[1]
    window = SC_WINDOW
    assert n % (window * SC_CORES * SC_SUBCORES) == 0
    per_core = n // window // SC_CORES

    @functools.partial(pl.kernel, out_type=jax.ShapeDtypeStruct((n, d), table.dtype), mesh=_sc_mesh())
    def gather_kernel(x_hbm, i_hbm, o_hbm):
        base = lax.axis_index("core") * per_core

        def body(i_vmem, o_vmem):
            pltpu.sync_copy(x_hbm.at[i_vmem.at[0]], o_vmem)

        pltpu.emit_pipeline(
            body,
            grid=(per_core,),
            in_specs=[pl.BlockSpec((1, window), index_map=lambda i: (0, base + i))],
            out_specs=[pl.BlockSpec((window, d), index_map=lambda i: (base + i, 0))],
            core_axis_name="subcore",
            dimension_semantics=(pltpu.PARALLEL,),
            trace_scopes=False,
        )(i_hbm, o_hbm)

    return gather_kernel(table, idx.reshape(1, n))


def _sc_scatter_rows(rows, dest, n_out, repeat):
    n_src, d = rows.shape
    n = dest.shape[0]
    assert n == repeat * n_src
    window = SC_WINDOW
    assert n % (window * SC_CORES * SC_SUBCORES) == 0
    per_core = n // window // SC_CORES
    src_windows = n_src // window

    @functools.partial(pl.kernel, out_type=jax.ShapeDtypeStruct((n_out, d), rows.dtype), mesh=_sc_mesh())
    def scatter_kernel(x_hbm, i_hbm, o_hbm):
        base = lax.axis_index("core") * per_core

        def body(x_vmem, i_vmem):
            pltpu.sync_copy(x_vmem, o_hbm.at[i_vmem.at[0]])

        pltpu.emit_pipeline(
            body,
            grid=(per_core,),
            in_specs=[pl.BlockSpec((window, d), index_map=lambda i: ((base + i) % src_windows, 0)),
                      pl.BlockSpec((1, window), index_map=lambda i: (0, base + i))],
            out_specs=[],
            core_axis_name="subcore",
            dimension_semantics=(pltpu.PARALLEL,),
            trace_scopes=False,
        )(x_hbm, i_hbm)

    return scatter_kernel(rows, dest.reshape(1, n))


def _moba(q2, qh, kv, bsz, seq_len):
    heads = q2.shape[1] // MOBA_HEAD_DIM
    nbk = MOBA_MAX_BLOCKS
    sel, cnt = _moba_select(q2, kv, bsz, seq_len)
    od, lsed = _moba_diag(q2, kv, bsz, seq_len)
    counts = cnt[..., 0].astype(jnp.int32).reshape(bsz * heads * nbk)
    pcounts = ((counts + MOBA_TILE - 1) // MOBA_TILE) * MOBA_TILE
    pends = jnp.cumsum(pcounts)
    pstarts = (pends - pcounts).reshape(bsz, heads, 1, 1, nbk)
    n_items = bsz * heads * seq_len * MOBA_TOPK
    step_rows = MOBA_TILE * MOBA_TILES_PER_STEP
    n_rows = -(-(n_items + bsz * heads * nbk * MOBA_TILE) // step_rows) * step_rows
    n_null = step_rows
    idx = sel[:, :, 0:MOBA_TOPK, :]
    rank = sel[:, :, MOBA_TOPK:2 * MOBA_TOPK, :]
    start = jnp.sum(jnp.where(idx[..., None] == jnp.arange(nbk), pstarts, 0), axis=-1)
    null_row = n_rows + jnp.arange(seq_len, dtype=jnp.int32) % n_null
    dest = jnp.where(idx >= 0, start + rank, null_row)
    n_tiles = (n_rows + n_null) // MOBA_TILE
    n_groups = bsz * heads * nbk
    tile_start = jnp.arange(n_tiles, dtype=jnp.int32) * MOBA_TILE
    tile_g = jnp.minimum(jnp.sum((pends[None, :] <= tile_start[:, None]).astype(jnp.int32), axis=1), n_groups - 1)
    tile_real = (tile_start < pends[-1]).astype(jnp.int32)
    tile_head = (tile_g // nbk) % heads
    tile_row = (tile_g // (heads * nbk)) * (seq_len // MOBA_BLOCK) + jnp.minimum(tile_g % nbk, seq_len // MOBA_BLOCK - 1)
    dest_by_slot = dest.transpose(2, 1, 0, 3).reshape(-1)
    qd = _sc_scatter_rows(qh.reshape(-1, LANES), dest_by_slot, n_rows + n_null, MOBA_TOPK)
    part = _moba_grouped(qd, kv, tile_row, tile_head // 2, tile_head % 2, tile_real)
    gath = _sc_gather_rows(part, dest.reshape(-1)).reshape(bsz, heads, MOBA_TOPK, seq_len, LANES)
    return _moba_combine(od, lsed, gath, bsz, seq_len)


def _mem_kv_kernel(mem_ref, gmem_ref, w_ref, gck_ref, k_ref, v_ref):
    xf = mem_ref[0]
    ms = jnp.mean(xf * xf, axis=-1, keepdims=True)
    h = (xf * lax.rsqrt(ms + EPS) * gmem_ref[...]).astype(BF16)
    kv = _dot(h, w_ref[...])
    w = k_ref.shape[2]
    hd = w // X_HEADS
    for c in range(X_HEADS):
        chunk = kv[:, c * hd:(c + 1) * hd]
        cms = jnp.mean(chunk * chunk, axis=-1, keepdims=True)
        k_ref[0, :, c * hd:(c + 1) * hd] = (chunk * lax.rsqrt(cms + EPS) * gck_ref[...]).astype(BF16)
    v_ref[0] = kv[:, w:].astype(BF16)


def _mem_kv(mem, g_mem, w_kv_mem, g_ck):
    bsz, m, d = mem.shape
    w = w_kv_mem.shape[1] // 2
    const = lambda b: (0, 0)
    out = jax.ShapeDtypeStruct((bsz, m, w), BF16)
    return pl.pallas_call(
        _mem_kv_kernel,
        grid=(bsz,),
        in_specs=[pl.BlockSpec((1, m, d), lambda b: (b, 0, 0)), pl.BlockSpec((1, d), const),
                  pl.BlockSpec((d, 2 * w), const), pl.BlockSpec((1, w // X_HEADS), const)],
        out_specs=[pl.BlockSpec((1, m, w), lambda b: (b, 0, 0))] * 2,
        out_shape=[out, out],
        compiler_params=_params("parallel"),
        name="mem_kv",
    )(mem, g_mem.reshape(1, d), w_kv_mem.astype(BF16), g_ck.reshape(1, -1))


def _merge_kernel(x_ref, ys_ref, u_ref, dskip_ref, om_ref, xq_ref, kc_ref, vc_ref, g_ref,
                  wglu_ref, wmo_ref, wco_ref, wout_ref, gffn_ref, wr_ref, br_ref,
                  x1_ref, h2_ref, logit_ref):
    d = x_ref.shape[1]
    y = ys_ref[...].astype(F32) + dskip_ref[...] * u_ref[...].astype(F32)
    ge = 0.5 * y * (1.0 + jnp.tanh(math.sqrt(2.0 / math.pi) * (y + 0.044715 * (y * y * y))))
    z = _dot(ge.astype(BF16), wglu_ref[...])
    merged = g_ref[:, 0:d].astype(F32) * (z[:, :d] * _sigmoid(z[:, d:]))
    merged = merged + g_ref[:, d:2 * d].astype(F32) * _dot(om_ref[...], wmo_ref[...])
    w = xq_ref.shape[1]
    hd = w // X_HEADS
    heads = []
    for c in range(X_HEADS):
        s = _dot_nt(xq_ref[:, c * hd:(c + 1) * hd], kc_ref[0, :, c * hd:(c + 1) * hd]) * (hd ** -0.5)
        p = jnp.exp(s - jnp.max(s, axis=1, keepdims=True))
        p = p / jnp.sum(p, axis=1, keepdims=True)
        heads.append(_dot(p.astype(BF16), vc_ref[0, :, c * hd:(c + 1) * hd]))
    oc = jnp.concatenate(heads, axis=1).astype(BF16)
    merged = merged + g_ref[:, 2 * d:3 * d].astype(F32) * _dot(oc, wco_ref[...])
    x1 = x_ref[...] + _dot(merged.astype(BF16), wout_ref[...])
    x1_ref[...] = x1
    ms = jnp.mean(x1 * x1, axis=-1, keepdims=True)
    h2 = (x1 * lax.rsqrt(ms + EPS) * gffn_ref[...]).astype(BF16)
    for j in range(d // LANES):
        h2_ref[j] = h2[:, j * LANES:(j + 1) * LANES].astype(F32)
    logit_ref[...] = _dot(h2, wr_ref[...]) + br_ref[...]


def _merge(xt, ys, u, d_skip, om, xqn, kc, vc, gates, w_glu, w_mo, w_co, w_out, g_ffn, w_router, b_router,
           seq_len, tm=256):
    t, d = xt.shape
    w = d // 2
    m = kc.shape[1]
    ne = w_router.shape[1]
    nt = seq_len // tm
    row = lambda i: (i, 0)
    const = lambda i: (0, 0)
    per_b = lambda i: (i // nt, 0, 0)
    return pl.pallas_call(
        _merge_kernel,
        grid=(t // tm,),
        in_specs=[
            pl.BlockSpec((tm, d), row), pl.BlockSpec((tm, w), row), pl.BlockSpec((tm, w), row),
            pl.BlockSpec((1, w), const), pl.BlockSpec((tm, w), row), pl.BlockSpec((tm, w), row),
            pl.BlockSpec((1, m, w), per_b), pl.BlockSpec((1, m, w), per_b),
            pl.BlockSpec((tm, N_BRANCH * d), row),
            pl.BlockSpec((w, 2 * d), const), pl.BlockSpec((w, d), const), pl.BlockSpec((w, d), const),
            pl.BlockSpec((d, d), const), pl.BlockSpec((1, d), const),
            pl.BlockSpec((d, ne), const), pl.BlockSpec((1, ne), const),
        ],
        out_specs=[pl.BlockSpec((tm, d), row), pl.BlockSpec((d // LANES, tm, LANES), lambda i: (0, i, 0)),
                   pl.BlockSpec((tm, ne), row)],
        out_shape=[jax.ShapeDtypeStruct((t, d), F32), jax.ShapeDtypeStruct((d // LANES, t, LANES), F32),
                   jax.ShapeDtypeStruct((t, ne), F32)],
        compiler_params=_params("parallel"),
        name="merge",
    )(xt, ys, u, d_skip.reshape(1, w), om, xqn, kc, vc, gates,
      w_glu.astype(BF16), w_mo.astype(BF16), w_co.astype(BF16), w_out.astype(BF16),
      g_ffn.reshape(1, d), w_router.astype(BF16), b_router.reshape(1, ne))


def _moe_kernel(blk_e_ref, blk_used_ref, xs_ref, wgu_ref, bgu_ref, wd_ref, bd_ref, y_ref, wgu_bf, wd_bf):
    i = pl.program_id(0)
    prev = blk_e_ref[jnp.maximum(i - 1, 0)]

    @pl.when((i == 0) | (blk_e_ref[i] != prev))
    def _():
        wgu_bf[...] = wgu_ref[0].astype(BF16)
        wd_bf[...] = wd_ref[0].astype(BF16)

    @pl.when(blk_used_ref[i] > 0)
    def _():
        de = wd_bf.shape[0]
        xs = jnp.concatenate([xs_ref[j] for j in range(xs_ref.shape[0])], axis=1).astype(BF16)
        gu = _dot(xs, wgu_bf[...]) + bgu_ref[0]
        gate = jnp.minimum(gu[:, :de], SWIGLU_LIMIT)
        up = jnp.clip(gu[:, de:], -SWIGLU_LIMIT, SWIGLU_LIMIT)
        act = gate * _sigmoid(SWIGLU_ALPHA * gate) * (up + 1.0)
        y = _dot(act.astype(BF16), wd_bf[...]) + bd_ref[0]
        for j in range(y_ref.shape[0]):
            y_ref[j] = y[:, j * LANES:(j + 1) * LANES]

    @pl.when(blk_used_ref[i] == 0)
    def _():
        y_ref[...] = jnp.zeros(y_ref.shape, y_ref.dtype)


def _moe_experts(xs, blk_e, blk_used, w_gu, b_gu, w_down, b_down):
    slabs, p, _ = xs.shape
    d = slabs * LANES
    ne, _, de2 = w_gu.shape
    de = de2 // 2
    nblk = p // EXPERT_ROWS
    row_spec = pl.BlockSpec((slabs, EXPERT_ROWS, LANES), lambda i, e, n: (0, i, 0))
    grid_spec = pltpu.PrefetchScalarGridSpec(
        num_scalar_prefetch=2,
        grid=(nblk,),
        in_specs=[
            row_spec,
            pl.BlockSpec((1, d, de2), lambda i, e, n: (e[i], 0, 0)),
            pl.BlockSpec((1, 1, de2), lambda i, e, n: (e[i], 0, 0)),
            pl.BlockSpec((1, de, d), lambda i, e, n: (e[i], 0, 0)),
            pl.BlockSpec((1, 1, d), lambda i, e, n: (e[i], 0, 0)),
        ],
        out_specs=row_spec,
        scratch_shapes=[pltpu.VMEM((d, de2), BF16), pltpu.VMEM((de, d), BF16)],
    )
    return pl.pallas_call(
        _moe_kernel,
        grid_spec=grid_spec,
        out_shape=jax.ShapeDtypeStruct(xs.shape, F32),
        compiler_params=_params("arbitrary"),
        name="moe_experts",
    )(blk_e, blk_used, xs, w_gu, b_gu.reshape(ne, 1, de2), w_down, b_down.reshape(ne, 1, d))


def _moe_ffn(x1, h2, logits, w_gu, b_gu, w_down, b_down):
    t, d = x1.shape
    tk = t * TOPK_EXPERTS
    top_v, top_e = lax.top_k(logits, TOPK_EXPERTS)
    weights = jax.nn.softmax(top_v, axis=-1)
    e_flat = top_e.reshape(tk)
    expert_ids = jnp.arange(N_EXPERTS, dtype=e_flat.dtype)
    is_e = e_flat[:, None] == expert_ids[None, :]
    counts = jnp.sum(is_e.astype(jnp.int32), axis=0)
    order = jnp.argsort(e_flat, stable=True).astype(jnp.int32)
    pos = jnp.argsort(order).astype(jnp.int32)
    pcounts = ((counts + EXPERT_ROWS - 1) // EXPERT_ROWS) * EXPERT_ROWS
    pends = jnp.cumsum(pcounts)
    pstarts = pends - pcounts
    cstarts = jnp.cumsum(counts) - counts
    dest = jnp.sum(jnp.where(is_e, (pstarts - cstarts)[None, :], 0), axis=1) + pos
    nblk = -(-tk // EXPERT_ROWS) + N_EXPERTS
    blk_start = jnp.arange(nblk, dtype=jnp.int32) * EXPERT_ROWS
    blk_e = jnp.minimum(jnp.sum((pends[None, :] <= blk_start[:, None]).astype(jnp.int32), axis=1), N_EXPERTS - 1)
    blk_used = (blk_start < pends[-1]).astype(jnp.int32)
    slabs = d // LANES
    p = nblk * EXPERT_ROWS
    dest_by_k = dest.reshape(t, TOPK_EXPERTS).T
    slab_off = jnp.arange(slabs, dtype=jnp.int32) * p
    dest_kst = (dest_by_k[:, None, :] + slab_off[None, :, None]).reshape(-1)
    dest_skt = (dest_by_k[None, :, :] + slab_off[:, None, None]).reshape(-1)
    xs = _sc_scatter_rows(h2.reshape(slabs * t, LANES), dest_kst, slabs * p, TOPK_EXPERTS)
    ys = _moe_experts(xs.reshape(slabs, p, LANES), blk_e, blk_used, w_gu, b_gu, w_down, b_down)
    picked = _sc_gather_rows(ys.reshape(slabs * p, LANES), dest_skt).reshape(slabs, TOPK_EXPERTS, t, LANES)
    mixed = [sum(picked[j, k] * weights[:, k:k + 1] for k in range(TOPK_EXPERTS)) for j in range(slabs)]
    return x1 + jnp.concatenate(mixed, axis=1)


def kernel(x, mem, g_mix, w_in, lam_re, lam_im, log_dt, b_re, b_im, c_re, c_im, d_skip, w_glu, g_q, g_k, w_moba_out, g_mem, w_kv_mem, g_cq, g_ck, w_cross_out, w_out, g_ffn, w_router, b_router, w_gu, b_gu, w_down, b_down):
    bsz, seq_len, d = x.shape
    xt = x.reshape(bsz * seq_len, d)
    for l in range(g_mix.shape[0]):
        u, q2, kv, xqn, gates, qh = _in_proj(xt, g_mix[l], w_in[l], g_q[l], g_k[l], g_cq[l], seq_len)
        mats = _s5_matrices(lam_re[l], lam_im[l], log_dt[l], b_re[l], b_im[l], c_re[l], c_im[l],
                            seq_len // S5_CHUNK)
        ys = _s5(u, mats, bsz, seq_len)
        om = _moba(q2, qh, kv, bsz, seq_len)
        kc, vc = _mem_kv(mem, g_mem[l], w_kv_mem[l], g_ck[l])
        x1, h2, logits = _merge(xt, ys, u, d_skip[l], om, xqn, kc, vc, gates, w_glu[l], w_moba_out[l],
                                w_cross_out[l], w_out[l], g_ffn[l], w_router[l], b_router[l], seq_len)
        xt = _moe_ffn(x1, h2, logits, w_gu[l], b_gu[l], w_down[l], b_down[l])
    return xt.reshape(bsz, seq_len, d)
```

```python
import functools
import math

import jax
import jax.numpy as jnp
from jax import lax
from jax.experimental import pallas as pl
from jax.experimental.pallas import tpu as pltpu
from jax.experimental.pallas import tpu_sc as plsc

F32 = jnp.float32
BF16 = jnp.bfloat16

EPS = 1e-6
N_BRANCH = 3
SSM_GROUP = 16
SSM_STATE = 64
S5_CHUNK = 16
MOBA_HEAD_DIM = 64
MOBA_BLOCK = 256
MOBA_TOPK = 3
MOBA_MAX_BLOCKS = 64
ROPE_THETA = 10000.0
X_HEADS = 4
N_EXPERTS = 32
TOPK_EXPERTS = 4
SWIGLU_LIMIT = 7.0
SWIGLU_ALPHA = 1.702
EXPERT_ROWS = 512
NEG_BIG = -1e30
LANES = 128
VMEM_LIMIT_BYTES = 56 * 1024 * 1024


def _params(*sem):
    return pltpu.CompilerParams(dimension_semantics=sem, vmem_limit_bytes=VMEM_LIMIT_BYTES)


def _sigmoid(x):
    return 1.0 / (1.0 + jnp.exp(-x))


def _dot(a, b):
    return jnp.dot(a, b, preferred_element_type=F32)


def _pack_bf16_pairs(x):
    n = x.shape[1] // 2
    lo = lax.bitcast_convert_type(x[:, :n], jnp.uint32) >> 16
    hi = lax.bitcast_convert_type(x[:, n:], jnp.uint32) & jnp.uint32(0xFFFF0000)
    return lo | hi


def _unpack_bf16_pairs(w):
    lo = lax.bitcast_convert_type(w << 16, F32)
    hi = lax.bitcast_convert_type(w & jnp.uint32(0xFFFF0000), F32)
    return lo, hi


def _dot_nt(a, b):
    return lax.dot_general(a, b, (((1,), (1,)), ((), ())), preferred_element_type=F32)


def _inproj_kernel(x_ref, gmix_ref, wa_ref, wg_ref, e64_ref, gq_ref, gk_ref, gcq_ref, cos_ref, sin_ref,
                   u_ref, q_ref, kv_ref, xq_ref, g_ref, qh_ref):
    xf = x_ref[...]
    ms = jnp.mean(xf * xf, axis=-1, keepdims=True)
    h = (xf * lax.rsqrt(ms + EPS) * gmix_ref[...]).astype(BF16)
    a = _dot(h, wa_ref[...])
    w = u_ref.shape[1]
    u_ref[...] = a[:, :w]

    cos = jnp.tile(cos_ref[...], (1, w // LANES))
    sin = jnp.tile(sin_ref[...], (1, w // LANES))
    lane = lax.broadcasted_iota(jnp.int32, (xf.shape[0], w), 1)
    first_half = (lane % MOBA_HEAD_DIM) < (MOBA_HEAD_DIM // 2)

    def qk_norm_rope(raw, g):
        ss = _dot((raw * raw).astype(BF16), e64_ref[...])
        n = raw * lax.rsqrt(ss * (1.0 / MOBA_HEAD_DIM) + EPS) * g
        rot = jnp.where(first_half,
                        pltpu.roll(n, w - MOBA_HEAD_DIM // 2, 1),
                        pltpu.roll(n, MOBA_HEAD_DIM // 2, 1))
        return n * cos + rot * sin

    q = qk_norm_rope(a[:, w:2 * w], gq_ref[...]) * (MOBA_HEAD_DIM ** -0.5)
    q_ref[...] = q.astype(BF16)
    in_a = lax.broadcasted_iota(jnp.int32, (xf.shape[0], LANES), 1) < MOBA_HEAD_DIM
    for p in range(w // LANES):
        pair = q[:, p * LANES:(p + 1) * LANES].astype(BF16).astype(F32)
        qh_ref[2 * p] = jnp.where(in_a, pair, 0.0)
        qh_ref[2 * p + 1] = jnp.where(in_a, 0.0, pair)
    k = qk_norm_rope(a[:, 2 * w:3 * w], gk_ref[...]).astype(BF16)
    for p in range(w // LANES):
        kv_ref[:, (2 * p) * LANES:(2 * p + 1) * LANES] = k[:, p * LANES:(p + 1) * LANES]
        kv_ref[:, (2 * p + 1) * LANES:(2 * p + 2) * LANES] = a[:, 3 * w + p * LANES:3 * w + (p + 1) * LANES].astype(BF16)

    xq = a[:, 4 * w:5 * w]
    hd = w // X_HEADS
    for c in range(X_HEADS):
        chunk = xq[:, c * hd:(c + 1) * hd]
        cms = jnp.mean(chunk * chunk, axis=-1, keepdims=True)
        xq_ref[:, c * hd:(c + 1) * hd] = (chunk * lax.rsqrt(cms + EPS) * gcq_ref[...]).astype(BF16)

    d = xf.shape[1]
    for c in range(N_BRANCH):
        z = _dot(h, wg_ref[:, c * d:(c + 1) * d])
        g_ref[:, c * d:(c + 1) * d] = _sigmoid(z).astype(BF16)


def _in_proj(xt, g_mix, w_in, g_q, g_k, g_cq, seq_len, tm=256):
    t, d = xt.shape
    w = d // 2
    wa = w_in[:, :5 * w].astype(BF16)
    wg = w_in[:, 5 * w:].astype(BF16)
    heads = w // MOBA_HEAD_DIM
    e64 = jnp.kron(jnp.eye(heads, dtype=F32), jnp.ones((MOBA_HEAD_DIM, MOBA_HEAD_DIM), F32)).astype(BF16)
    half = MOBA_HEAD_DIM // 2
    inv = ROPE_THETA ** (-jnp.arange(half, dtype=F32) / half)
    ang = jnp.arange(seq_len, dtype=F32)[:, None] * inv[None, :]
    cos = jnp.tile(jnp.cos(ang), (1, LANES // half))
    sin = jnp.tile(jnp.concatenate([-jnp.sin(ang), jnp.sin(ang)], axis=1), (1, LANES // MOBA_HEAD_DIM))
    nt = seq_len // tm
    row = lambda i: (i, 0)
    const = lambda i: (0, 0)
    out_w = jax.ShapeDtypeStruct((t, w), BF16)
    return pl.pallas_call(
        _inproj_kernel,
        grid=(t // tm,),
        in_specs=[
            pl.BlockSpec((tm, d), row),
            pl.BlockSpec((1, d), const),
            pl.BlockSpec((d, 5 * w), const),
            pl.BlockSpec((d, N_BRANCH * d), const),
            pl.BlockSpec((w, w), const),
            pl.BlockSpec((1, w), const),
            pl.BlockSpec((1, w), const),
            pl.BlockSpec((1, w // X_HEADS), const),
            pl.BlockSpec((tm, LANES), lambda i: (i % nt, 0)),
            pl.BlockSpec((tm, LANES), lambda i: (i % nt, 0)),
        ],
        out_specs=[pl.BlockSpec((tm, w), row), pl.BlockSpec((tm, w), row), pl.BlockSpec((tm, 2 * w), row),
                   pl.BlockSpec((tm, w), row), pl.BlockSpec((tm, N_BRANCH * d), row),
                   pl.BlockSpec((heads, tm, LANES), lambda i: (0, i, 0))],
        out_shape=[jax.ShapeDtypeStruct((t, w), F32), out_w, jax.ShapeDtypeStruct((t, 2 * w), BF16), out_w,
                   jax.ShapeDtypeStruct((t, N_BRANCH * d), BF16),
                   jax.ShapeDtypeStruct((heads, t, LANES), F32)],
        compiler_params=_params("parallel"),
        name="in_proj",
    )(xt, g_mix.reshape(1, d), wa, wg, e64,
      jnp.tile(g_q, heads).reshape(1, w), jnp.tile(g_k, heads).reshape(1, w), g_cq.reshape(1, -1), cos, sin)


def _s5_matrices(lam_re, lam_im, log_dt, b_re, b_im, c_re, c_im, n_chunks):
    hp = lax.Precision.HIGHEST
    c = S5_CHUNK
    dt = jnp.exp(log_dt)[:, None]
    mag = jnp.exp(lam_re * dt)
    ar = mag * jnp.cos(lam_im * dt)
    ai = mag * jnp.sin(lam_im * dt)
    nr = ar - 1.0
    den = lam_re * lam_re + lam_im * lam_im
    cr = (nr * lam_re + ai * lam_im) / den
    ci = (ai * lam_re - nr * lam_im) / den
    bbr = cr[..., None] * b_re - ci[..., None] * b_im
    bbi = cr[..., None] * b_im + ci[..., None] * b_re

    def power(n):
        nf = n.astype(F32)[None, :, None]
        m = jnp.exp((lam_re * dt)[:, None, :] * nf)
        th = (lam_im * dt)[:, None, :] * nf
        return m * jnp.cos(th), m * jnp.sin(th)

    pr, pi = power(jnp.arange(c + 1))
    kbr = pr[..., None] * bbr[:, None] - pi[..., None] * bbi[:, None]
    kbi = pr[..., None] * bbi[:, None] + pi[..., None] * bbr[:, None]
    kk = (jnp.einsum('ghp,gtpc->gthc', c_re, kbr, precision=hp)
          - jnp.einsum('ghp,gtpc->gthc', c_im, kbi, precision=hp))
    tq = jnp.arange(c)
    g = kk.shape[0]
    rev = c - 1 - tq
    w_in = jnp.concatenate([kbr[:, rev].transpose(0, 1, 3, 2), kbi[:, rev].transpose(0, 1, 3, 2)], axis=-1)
    w_in = w_in.reshape(g, c * SSM_GROUP, 2 * SSM_STATE)
    prn, pin = pr[:, 1:], pi[:, 1:]
    wo_r = c_re[:, None] * prn[:, :, None, :] - c_im[:, None] * pin[:, :, None, :]
    wo_i = -c_re[:, None] * pin[:, :, None, :] - c_im[:, None] * prn[:, :, None, :]
    w_out = jnp.concatenate([wo_r, wo_i], axis=-1).transpose(0, 3, 1, 2).reshape(g, 2 * SSM_STATE, c * SSM_GROUP)
    n_steps = max(1, int(math.ceil(math.log2(n_chunks))))
    qr, qi = power(c * (2 ** jnp.arange(n_steps)))
    pa = jnp.concatenate([qr, qr], axis=-1)
    pb = jnp.concatenate([-qi, qi], axis=-1)
    return kk.astype(BF16), w_in.astype(BF16), w_out.astype(BF16), pa, pb


S5_SET = LANES // SSM_GROUP
S5_ROWS = 256


def _s5_kernel(u_ref, toep_ref, win_ref, wout_ref, pa_ref, pb_ref, y_ref, s_ref):
    c = S5_CHUNK
    nc = u_ref.shape[0] // c
    rb = min(S5_ROWS, nc)

    def chunk_rows(r0):
        return jnp.concatenate([u_ref[pl.ds(r0 * c + t, rb, stride=c), :] for t in range(c)], axis=1).astype(BF16)

    for blk in range(nc // rb):
        s_ref[blk * rb:(blk + 1) * rb, :] = _dot(chunk_rows(blk * rb), win_ref[0])
    row = lax.broadcasted_iota(jnp.int32, (nc, LANES), 0)
    n_steps = pa_ref.shape[1]
    for g in range(S5_SET):
        lanes = slice(g * LANES, (g + 1) * LANES)
        s = s_ref[:, lanes]
        for k in range(n_steps):
            sh = 1 << k
            if sh >= nc:
                break
            prev = jnp.where(row >= sh, pltpu.roll(s, sh, 0), 0.0)
            s = s + pa_ref[0, k:k + 1, lanes] * prev + pb_ref[0, k:k + 1, lanes] * pltpu.roll(prev, SSM_STATE, 1)
        s_ref[:, lanes] = jnp.where(row >= 1, pltpu.roll(s, 1, 0), 0.0)
    for blk in range(nc // rb):
        rows = slice(blk * rb, (blk + 1) * rb)
        y = _dot(chunk_rows(blk * rb), toep_ref[0]) + _dot(s_ref[rows, :].astype(BF16), wout_ref[0])
        for t in range(c):
            y_ref[pl.ds(blk * rb * c + t, rb, stride=c), :] = y[:, t * LANES:(t + 1) * LANES]


def _s5_block_diag(mats):
    kk, w_in, w_out, pa, pb = mats
    g, c, h, p2 = kk.shape[0], S5_CHUNK, SSM_GROUP, 2 * SSM_STATE
    ns = g // S5_SET
    eye = jnp.eye(S5_SET, dtype=kk.dtype)
    lag_bd = jnp.einsum('sglhc,gk->slgckh', kk.reshape(ns, S5_SET, c + 1, h, h), eye).reshape(ns, c + 1, LANES, LANES)
    tq = jnp.arange(c)
    lag = tq[None, :] - tq[:, None]
    toep_bd = jnp.where((lag >= 0)[None, :, :, None, None], lag_bd[:, jnp.clip(lag, 0, c)], 0)
    toep_bd = toep_bd.transpose(0, 1, 3, 2, 4).reshape(ns, c * LANES, c * LANES)

    def rows_to_states(w):
        wc = w.reshape(ns, S5_SET, c, h, p2).transpose(0, 2, 1, 3, 4).reshape(ns, c, LANES, p2)
        same = (jnp.arange(LANES)[:, None] // h) == (jnp.arange(S5_SET * p2)[None, :] // p2)
        return jnp.where(same, jnp.tile(wc, (1, 1, 1, S5_SET)), 0).reshape(ns, c * LANES, S5_SET * p2)

    win_bd = rows_to_states(w_in)
    wout_bd = rows_to_states(w_out.transpose(0, 2, 1)).transpose(0, 2, 1)
    k = pa.shape[1]
    lanes_of = lambda x: x.reshape(ns, S5_SET, k, p2).transpose(0, 2, 1, 3).reshape(ns, k, S5_SET * p2)
    return toep_bd, win_bd, wout_bd, lanes_of(pa), lanes_of(pb)


def _s5(u, mats, bsz, seq_len):
    toep, w_in, w_out, pa, pb = _s5_block_diag(mats)
    ns = toep.shape[0]
    nc = seq_len // S5_CHUNK
    per_set = lambda s, b: (s, 0, 0)
    once = dict(pipeline_mode=pl.Buffered(1))
    io_spec = pl.BlockSpec((seq_len, LANES), lambda s, b: (b, s), **once)
    return pl.pallas_call(
        _s5_kernel,
        grid=(ns, bsz),
        in_specs=[
            io_spec,
            pl.BlockSpec((1,) + toep.shape[1:], per_set, **once),
            pl.BlockSpec((1,) + w_in.shape[1:], per_set, **once),
            pl.BlockSpec((1,) + w_out.shape[1:], per_set, **once),
            pl.BlockSpec((1,) + pa.shape[1:], per_set),
            pl.BlockSpec((1,) + pb.shape[1:], per_set),
        ],
        out_specs=io_spec,
        out_shape=jax.ShapeDtypeStruct(u.shape, F32),
        scratch_shapes=[pltpu.VMEM((nc, S5_SET * 2 * SSM_STATE), F32)],
        compiler_params=_params("arbitrary", "arbitrary"),
        name="s5_scan",
    )(u, toep, w_in, w_out, pa, pb)


def _moba_select_kernel(q_ref, k_ref, tri_ref, sel_ref, cnt_out_ref, km_ref, cnt_ref):
    h = pl.program_id(1)
    t = pl.program_id(2)
    ts = q_ref.shape[0]
    nbk = MOBA_MAX_BLOCKS

    @pl.when(t == 0)
    def _():
        kk = k_ref[...].astype(F32)
        nb = kk.shape[0] // MOBA_BLOCK
        km = jnp.sum(kk.reshape(nb, MOBA_BLOCK, LANES), axis=1) * (1.0 / MOBA_BLOCK)
        if nb < nbk:
            km = jnp.concatenate([km, jnp.zeros((nbk - nb, LANES), F32)], axis=0)
        lane = lax.broadcasted_iota(jnp.int32, (nbk, LANES), 1)
        km_ref[...] = jnp.where((lane // MOBA_HEAD_DIM) == (h % 2), km, 0.0).astype(BF16)
        cnt_ref[...] = jnp.zeros(cnt_ref.shape, F32)

    gate = _dot_nt(km_ref[...], q_ref[...])
    blk = lax.broadcasted_iota(jnp.int32, gate.shape, 0)
    qblk = (t * ts + lax.broadcasted_iota(jnp.int32, gate.shape, 1)) // MOBA_BLOCK
    g = jnp.where(blk < qblk, gate, -jnp.inf)
    selected = jnp.zeros(gate.shape, jnp.bool_)
    picks = []
    for _ in range(MOBA_TOPK):
        mx = jnp.max(g, axis=0, keepdims=True)
        idx = jnp.min(jnp.where(g == mx, blk, nbk), axis=0, keepdims=True)
        hit = blk == idx
        ok = (idx[0:1] < qblk[0:1]) & (mx > -jnp.inf)
        picks.append((hit, idx, ok))
        selected = selected | (hit & ok)
        g = jnp.where(hit, -jnp.inf, g)
    sel01 = jnp.where(selected, 1.0, 0.0).astype(BF16)
    prefix = _dot(sel01, tri_ref[...]) + jnp.tile(cnt_ref[...], (1, ts // LANES))
    rows = [jnp.where(ok, idx, -1) for (_, idx, ok) in picks]
    rows += [jnp.sum(jnp.where(hit, prefix, 0.0), axis=0, keepdims=True).astype(jnp.int32) for (hit, _, _) in picks]
    rows += [jnp.zeros((1, ts), jnp.int32)] * (8 - 2 * MOBA_TOPK)
    sel_ref[0, 0] = jnp.concatenate(rows, axis=0)
    cnt_ref[...] = cnt_ref[...] + _dot(sel01, jnp.ones((ts, LANES), BF16))
    cnt_out_ref[0, 0] = cnt_ref[...]


def _moba_select(q2, kv, bsz, seq_len, ts=1024):
    ts = min(ts, seq_len)
    heads = q2.shape[1] // MOBA_HEAD_DIM
    nt = seq_len // ts
    tri = (jnp.arange(ts)[:, None] < jnp.arange(ts)[None, :]).astype(BF16)
    return pl.pallas_call(
        _moba_select_kernel,
        grid=(bsz, heads, nt),
        in_specs=[
            pl.BlockSpec((ts, LANES), lambda b, h, t: (b * nt + t, h // 2)),
            pl.BlockSpec((seq_len, LANES), lambda b, h, t: (b, 2 * (h // 2))),
            pl.BlockSpec((ts, ts), lambda b, h, t: (0, 0)),
        ],
        out_specs=[pl.BlockSpec((1, 1, 8, ts), lambda b, h, t: (b, h, 0, t)),
                   pl.BlockSpec((1, 1, MOBA_MAX_BLOCKS, LANES), lambda b, h, t: (b, h, 0, 0))],
        out_shape=[jax.ShapeDtypeStruct((bsz, heads, 8, seq_len), jnp.int32),
                   jax.ShapeDtypeStruct((bsz, heads, MOBA_MAX_BLOCKS, LANES), F32)],
        scratch_shapes=[pltpu.VMEM((MOBA_MAX_BLOCKS, LANES), BF16), pltpu.VMEM((MOBA_MAX_BLOCKS, LANES), F32)],
        compiler_params=_params("parallel", "arbitrary", "arbitrary"),
        name="moba_select",
    )(q2, kv, tri)


def _softmax_block(q, kb, vb, causal):
    s = _dot_nt(q, kb)
    if causal:
        r = lax.broadcasted_iota(jnp.int32, s.shape, 0) % MOBA_BLOCK
        c = lax.broadcasted_iota(jnp.int32, s.shape, 1)
        s = jnp.where(c <= r, s, NEG_BIG)
    m = jnp.max(s, axis=1, keepdims=True)
    p = jnp.exp(s - m).astype(BF16)
    pv = _dot(p, jnp.concatenate([vb, jnp.ones(vb.shape, BF16)], axis=1))
    l = pv[:, LANES:LANES + 1]
    return pv[:, :LANES] / l, m + jnp.log(l)


MOBA_DIAG_BLOCKS = 2


def _moba_diag_kernel(q_ref, kv_ref, o_ref, lse_ref):
    lane = lax.broadcasted_iota(jnp.int32, (MOBA_BLOCK, LANES), 1)
    is_a = lane < MOBA_HEAD_DIM
    for u in range(MOBA_DIAG_BLOCKS):
        rows = pl.ds(u * MOBA_BLOCK, MOBA_BLOCK)
        q = q_ref[rows, :]
        zero = jnp.zeros_like(q)
        qs = jnp.concatenate([jnp.where(is_a, q, zero), jnp.where(is_a, zero, q)], axis=0)
        o, lse = _softmax_block(qs, kv_ref[rows, :LANES], kv_ref[rows, LANES:], True)
        o_ref[rows, :] = jnp.where(is_a, o[:MOBA_BLOCK], o[MOBA_BLOCK:])
        lse_ref[rows, :] = jnp.where(is_a, lse[:MOBA_BLOCK], lse[MOBA_BLOCK:])


def _moba_diag(q2, kv, bsz, seq_len):
    npair = q2.shape[1] // LANES
    rows = MOBA_BLOCK * MOBA_DIAG_BLOCKS
    nb = seq_len // rows
    spec = pl.BlockSpec((rows, LANES), lambda b, p, i: (b * nb + i, p))
    out = jax.ShapeDtypeStruct(q2.shape, F32)
    return pl.pallas_call(
        _moba_diag_kernel,
        grid=(bsz, npair, nb),
        in_specs=[spec, pl.BlockSpec((rows, 2 * LANES), lambda b, p, i: (b * nb + i, p))],
        out_specs=[spec, spec],
        out_shape=[out, out],
        compiler_params=_params("parallel", "parallel", "parallel"),
        name="moba_diag",
    )(q2, kv)


MOBA_TILE = 256
MOBA_TILES_PER_STEP = 8


def _moba_grouped_kernel(tile_row_ref, tile_pair_ref, tile_half_ref, tile_real_ref, qd_ref, *refs):
    n = MOBA_TILES_PER_STEP
    kv_refs, o_ref = refs[:n], refs[n]
    i = pl.program_id(0)
    lane = lax.broadcasted_iota(jnp.int32, (MOBA_TILE, LANES), 1)
    kv_lane = lax.broadcasted_iota(jnp.int32, (MOBA_BLOCK, LANES), 1)
    for u in range(n):
        tile = i * n + u
        rows = pl.ds(u * MOBA_TILE, MOBA_TILE)
        own = (lane // MOBA_HEAD_DIM) == tile_half_ref[tile]
        s = _dot_nt(qd_ref[rows, :].astype(BF16), kv_refs[u][:, :LANES])
        m = jnp.max(s, axis=1, keepdims=True)
        p = jnp.exp(s - m).astype(BF16)
        vb = jnp.where((kv_lane // MOBA_HEAD_DIM) == tile_half_ref[tile], kv_refs[u][:, LANES:], 1.0)
        pv = _dot(p, vb)
        row_sum = pltpu.roll(pv, MOBA_HEAD_DIM, 1)
        part = jnp.where(own, pv / row_sum, m + jnp.log(pv))
        o_ref[rows, :] = jnp.where(tile_real_ref[tile] > 0, part, NEG_BIG)


def _moba_grouped(qd, kv, tile_row, tile_pair, tile_half, tile_real):
    n = MOBA_TILES_PER_STEP
    n_tiles = qd.shape[0] // MOBA_TILE

    def kv_spec(u):
        return pl.BlockSpec((MOBA_BLOCK, 2 * LANES), lambda i, tr, tp, th, tl: (tr[i * n + u], tp[i * n + u]))

    grid_spec = pltpu.PrefetchScalarGridSpec(
        num_scalar_prefetch=4,
        grid=(n_tiles // n,),
        in_specs=[pl.BlockSpec((n * MOBA_TILE, LANES), lambda i, tr, tp, th, tl: (i, 0))]
        + [kv_spec(u) for u in range(n)],
        out_specs=pl.BlockSpec((n * MOBA_TILE, LANES), lambda i, tr, tp, th, tl: (i, 0)),
    )
    return pl.pallas_call(
        _moba_grouped_kernel,
        grid_spec=grid_spec,
        out_shape=jax.ShapeDtypeStruct(qd.shape, F32),
        compiler_params=_params("arbitrary"),
        name="moba_grouped",
    )(tile_row, tile_pair, tile_half, tile_real, qd, *([kv] * n))


def _moba_combine_kernel(od_ref, lsed_ref, *refs):
    g_refs, o_ref = refs[:-1], refs[-1]
    lane = lax.broadcasted_iota(jnp.int32, od_ref.shape, 1)
    is_a = lane < MOBA_HEAD_DIM
    parts = [(od_ref[...], lsed_ref[...])]
    for s in range(MOBA_TOPK):
        xa = g_refs[s][0, 0, 0]
        xb = g_refs[MOBA_TOPK + s][0, 0, 0]
        o = jnp.where(is_a, xa, xb)
        lse = pltpu.roll(jnp.where(is_a, xb, xa), MOBA_HEAD_DIM, 1)
        parts.append((o, lse))
    m = parts[0][1]
    for _, lse in parts[1:]:
        m = jnp.maximum(m, lse)
    num = jnp.zeros(od_ref.shape, F32)
    den = jnp.zeros(od_ref.shape, F32)
    for o, lse in parts:
        w = jnp.exp(lse - m)
        num = num + w * o
        den = den + w
    o_ref[...] = (num / den).astype(BF16)


def _moba_combine(od, lsed, gath, bsz, seq_len, tm=1024):
    tm = min(tm, seq_len)
    npair = od.shape[1] // LANES
    nt = seq_len // tm
    spec = pl.BlockSpec((tm, LANES), lambda b, p, i: (b * nt + i, p))

    def g_spec(e, s):
        return pl.BlockSpec((1, 1, 1, tm, LANES), lambda b, p, i: (b, 2 * p + e, s, i, 0))

    return pl.pallas_call(
        _moba_combine_kernel,
        grid=(bsz, npair, nt),
        in_specs=[spec, spec] + [g_spec(e, s) for e in range(2) for s in range(MOBA_TOPK)],
        out_specs=spec,
        out_shape=jax.ShapeDtypeStruct(od.shape, BF16),
        compiler_params=_params("parallel", "parallel", "parallel"),
        name="moba_combine",
    )(od, lsed, *([gath] * (2 * MOBA_TOPK)))


SC_WINDOW = 128
SC_CORES = 2
SC_SUBCORES = 16


def _sc_mesh():
    return plsc.VectorSubcoreMesh(core_axis_name="core", subcore_axis_name="subcore")


def _sc_gather_rows(table, idx):
    n = idx.shape[0]
    d = table.shape[1]
    window = SC_WINDOW
    assert n % (window * SC_CORES * SC_SUBCORES) == 0
    per_core = n // window // SC_CORES

    @functools.partial(pl.kernel, out_type=jax.ShapeDtypeStruct((n, d), table.dtype), mesh=_sc_mesh())
    def gather_kernel(x_hbm, i_hbm, o_hbm):
        base = lax.axis_index("core") * per_core

        def body(i_vmem, o_vmem):
            pltpu.sync_copy(x_hbm.at[i_vmem.at[0]], o_vmem)

        pltpu.emit_pipeline(
            body,
            grid=(per_core,),
            in_specs=[pl.BlockSpec((1, window), index_map=lambda i: (0, base + i))],
            out_specs=[pl.BlockSpec((window, d), index_map=lambda i: (base + i, 0))],
            core_axis_name="subcore",
            dimension_semantics=(pltpu.PARALLEL,),
            trace_scopes=False,
        )(i_hbm, o_hbm)

    return gather_kernel(table, idx.reshape(1, n))


def _sc_scatter_rows(rows, dest, n_out, repeat):
    n_src, d = rows.shape
    n = dest.shape[0]
    assert n == repeat * n_src
    window = SC_WINDOW
    assert n % (window * SC_CORES * SC_SUBCORES) == 0
    per_core = n // window // SC_CORES
    src_windows = n_src // window

    @functools.partial(pl.kernel, out_type=jax.ShapeDtypeStruct((n_out, d), rows.dtype), mesh=_sc_mesh())
    def scatter_kernel(x_hbm, i_hbm, o_hbm):
        base = lax.axis_index("core") * per_core

        def body(x_vmem, i_vmem):
            pltpu.sync_copy(x_vmem, o_hbm.at[i_vmem.at[0]])

        pltpu.emit_pipeline(
            body,
            grid=(per_core,),
            in_specs=[pl.BlockSpec((window, d), index_map=lambda i: ((base + i) % src_windows, 0)),
                      pl.BlockSpec((1, window), index_map=lambda i: (0, base + i))],
            out_specs=[],
            core_axis_name="subcore",
            dimension_semantics=(pltpu.PARALLEL,),
            trace_scopes=False,
        )(x_hbm, i_hbm)

    return scatter_kernel(rows, dest.reshape(1, n))


def _moba(q2, qh, kv, bsz, seq_len):
    heads = q2.shape[1] // MOBA_HEAD_DIM
    nbk = MOBA_MAX_BLOCKS
    sel, cnt = _moba_select(q2, kv, bsz, seq_len)
    od, lsed = _moba_diag(q2, kv, bsz, seq_len)
    counts = cnt[..., 0].astype(jnp.int32).reshape(bsz * heads * nbk)
    pcounts = ((counts + MOBA_TILE - 1) // MOBA_TILE) * MOBA_TILE
    pends = jnp.cumsum(pcounts)
    pstarts = (pends - pcounts).reshape(bsz, heads, 1, 1, nbk)
    n_items = bsz * heads * seq_len * MOBA_TOPK
    step_rows = MOBA_TILE * MOBA_TILES_PER_STEP
    n_rows = -(-(n_items + bsz * heads * nbk * MOBA_TILE) // step_rows) * step_rows
    n_null = step_rows
    idx = sel[:, :, 0:MOBA_TOPK, :]
    rank = sel[:, :, MOBA_TOPK:2 * MOBA_TOPK, :]
    start = jnp.sum(jnp.where(idx[..., None] == jnp.arange(nbk), pstarts, 0), axis=-1)
    null_row = n_rows + jnp.arange(seq_len, dtype=jnp.int32) % n_null
    dest = jnp.where(idx >= 0, start + rank, null_row)
    n_tiles = (n_rows + n_null) // MOBA_TILE
    n_groups = bsz * heads * nbk
    tile_start = jnp.arange(n_tiles, dtype=jnp.int32) * MOBA_TILE
    tile_g = jnp.minimum(jnp.sum((pends[None, :] <= tile_start[:, None]).astype(jnp.int32), axis=1), n_groups - 1)
    tile_real = (tile_start < pends[-1]).astype(jnp.int32)
    tile_head = (tile_g // nbk) % heads
    tile_row = (tile_g // (heads * nbk)) * (seq_len // MOBA_BLOCK) + jnp.minimum(tile_g % nbk, seq_len // MOBA_BLOCK - 1)
    dest_by_slot = dest.transpose(2, 1, 0, 3).reshape(-1)
    qd = _sc_scatter_rows(qh.reshape(-1, LANES), dest_by_slot, n_rows + n_null, MOBA_TOPK)
    part = _moba_grouped(qd, kv, tile_row, tile_head // 2, tile_head % 2, tile_real)
    gath = _sc_gather_rows(part, dest.reshape(-1)).reshape(bsz, heads, MOBA_TOPK, seq_len, LANES)
    return _moba_combine(od, lsed, gath, bsz, seq_len)


def _mem_kv_kernel(mem_ref, gmem_ref, w_ref, gck_ref, k_ref, v_ref):
    xf = mem_ref[0]
    ms = jnp.mean(xf * xf, axis=-1, keepdims=True)
    h = (xf * lax.rsqrt(ms + EPS) * gmem_ref[...]).astype(BF16)
    kv = _dot(h, w_ref[...])
    w = k_ref.shape[2]
    hd = w // X_HEADS
    for c in range(X_HEADS):
        chunk = kv[:, c * hd:(c + 1) * hd]
        cms = jnp.mean(chunk * chunk, axis=-1, keepdims=True)
        k_ref[0, :, c * hd:(c + 1) * hd] = (chunk * lax.rsqrt(cms + EPS) * gck_ref[...]).astype(BF16)
    v_ref[0] = kv[:, w:].astype(BF16)


def _mem_kv(mem, g_mem, w_kv_mem, g_ck):
    bsz, m, d = mem.shape
    w = w_kv_mem.shape[1] // 2
    const = lambda b: (0, 0)
    out = jax.ShapeDtypeStruct((bsz, m, w), BF16)
    return pl.pallas_call(
        _mem_kv_kernel,
        grid=(bsz,),
        in_specs=[pl.BlockSpec((1, m, d), lambda b: (b, 0, 0)), pl.BlockSpec((1, d), const),
                  pl.BlockSpec((d, 2 * w), const), pl.BlockSpec((1, w // X_HEADS), const)],
        out_specs=[pl.BlockSpec((1, m, w), lambda b: (b, 0, 0))] * 2,
        out_shape=[out, out],
        compiler_params=_params("parallel"),
        name="mem_kv",
    )(mem, g_mem.reshape(1, d), w_kv_mem.astype(BF16), g_ck.reshape(1, -1))


def _merge_kernel(x_ref, ys_ref, u_ref, dskip_ref, om_ref, xq_ref, kc_ref, vc_ref, g_ref,
                  wglu_ref, wmo_ref, wco_ref, wout_ref, gffn_ref, wr_ref, br_ref,
                  x1_ref, h2_ref, logit_ref):
    d = x_ref.shape[1]
    y = ys_ref[...].astype(F32) + dskip_ref[...] * u_ref[...].astype(F32)
    ge = 0.5 * y * (1.0 + jnp.tanh(math.sqrt(2.0 / math.pi) * (y + 0.044715 * (y * y * y))))
    z = _dot(ge.astype(BF16), wglu_ref[...])
    merged = g_ref[:, 0:d].astype(F32) * (z[:, :d] * _sigmoid(z[:, d:]))
    merged = merged + g_ref[:, d:2 * d].astype(F32) * _dot(om_ref[...], wmo_ref[...])
    w = xq_ref.shape[1]
    hd = w // X_HEADS
    heads = []
    for c in range(X_HEADS):
        s = _dot_nt(xq_ref[:, c * hd:(c + 1) * hd], kc_ref[0, :, c * hd:(c + 1) * hd]) * (hd ** -0.5)
        p = jnp.exp(s - jnp.max(s, axis=1, keepdims=True))
        p = p / jnp.sum(p, axis=1, keepdims=True)
        heads.append(_dot(p.astype(BF16), vc_ref[0, :, c * hd:(c + 1) * hd]))
    oc = jnp.concatenate(heads, axis=1).astype(BF16)
    merged = merged + g_ref[:, 2 * d:3 * d].astype(F32) * _dot(oc, wco_ref[...])
    x1 = x_ref[...] + _dot(merged.astype(BF16), wout_ref[...])
    x1_ref[...] = x1
    ms = jnp.mean(x1 * x1, axis=-1, keepdims=True)
    h2 = (x1 * lax.rsqrt(ms + EPS) * gffn_ref[...]).astype(BF16)
    packed = _pack_bf16_pairs(h2.astype(F32))
    for j in range(h2_ref.shape[0]):
        h2_ref[j] = packed[:, j * LANES:(j + 1) * LANES]
    logit_ref[...] = _dot(h2, wr_ref[...]) + br_ref[...]


def _merge(xt, ys, u, d_skip, om, xqn, kc, vc, gates, w_glu, w_mo, w_co, w_out, g_ffn, w_router, b_router,
           seq_len, tm=256):
    t, d = xt.shape
    w = d // 2
    m = kc.shape[1]
    ne = w_router.shape[1]
    nt = seq_len // tm
    row = lambda i: (i, 0)
    const = lambda i: (0, 0)
    per_b = lambda i: (i // nt, 0, 0)
    return pl.pallas_call(
        _merge_kernel,
        grid=(t // tm,),
        in_specs=[
            pl.BlockSpec((tm, d), row), pl.BlockSpec((tm, w), row), pl.BlockSpec((tm, w), row),
            pl.BlockSpec((1, w), const), pl.BlockSpec((tm, w), row), pl.BlockSpec((tm, w), row),
            pl.BlockSpec((1, m, w), per_b), pl.BlockSpec((1, m, w), per_b),
            pl.BlockSpec((tm, N_BRANCH * d), row),
            pl.BlockSpec((w, 2 * d), const), pl.BlockSpec((w, d), const), pl.BlockSpec((w, d), const),
            pl.BlockSpec((d, d), const), pl.BlockSpec((1, d), const),
            pl.BlockSpec((d, ne), const), pl.BlockSpec((1, ne), const),
        ],
        out_specs=[pl.BlockSpec((tm, d), row), pl.BlockSpec((d // 2 // LANES, tm, LANES), lambda i: (0, i, 0)),
                   pl.BlockSpec((tm, ne), row)],
        out_shape=[jax.ShapeDtypeStruct((t, d), F32), jax.ShapeDtypeStruct((d // 2 // LANES, t, LANES), jnp.uint32),
                   jax.ShapeDtypeStruct((t, ne), F32)],
        compiler_params=_params("parallel"),
        name="merge",
    )(xt, ys, u, d_skip.reshape(1, w), om, xqn, kc, vc, gates,
      w_glu.astype(BF16), w_mo.astype(BF16), w_co.astype(BF16), w_out.astype(BF16),
      g_ffn.reshape(1, d), w_router.astype(BF16), b_router.reshape(1, ne))


def _moe_kernel(blk_e_ref, blk_used_ref, xs_ref, wgu_ref, bgu_ref, wd_ref, bd_ref, y_ref, wgu_bf, wd_bf):
    i = pl.program_id(0)
    prev = blk_e_ref[jnp.maximum(i - 1, 0)]

    @pl.when((i == 0) | (blk_e_ref[i] != prev))
    def _():
        wgu_bf[...] = wgu_ref[0].astype(BF16)
        wd_bf[...] = wd_ref[0].astype(BF16)

    @pl.when(blk_used_ref[i] > 0)
    def _():
        de = wd_bf.shape[0]
        words = jnp.concatenate([xs_ref[j] for j in range(xs_ref.shape[0])], axis=1)
        xs = jnp.concatenate(_unpack_bf16_pairs(words), axis=1).astype(BF16)
        gu = _dot(xs, wgu_bf[...]) + bgu_ref[0]
        gate = jnp.minimum(gu[:, :de], SWIGLU_LIMIT)
        up = jnp.clip(gu[:, de:], -SWIGLU_LIMIT, SWIGLU_LIMIT)
        act = gate * _sigmoid(SWIGLU_ALPHA * gate) * (up + 1.0)
        y = _dot(act.astype(BF16), wd_bf[...]) + bd_ref[0]
        packed = _pack_bf16_pairs(y.astype(BF16).astype(F32))
        for j in range(y_ref.shape[0]):
            y_ref[j] = packed[:, j * LANES:(j + 1) * LANES]

    @pl.when(blk_used_ref[i] == 0)
    def _():
        y_ref[...] = jnp.zeros(y_ref.shape, y_ref.dtype)


def _moe_experts(xs, blk_e, blk_used, w_gu, b_gu, w_down, b_down):
    slabs, p, _ = xs.shape
    d = 2 * slabs * LANES
    ne, _, de2 = w_gu.shape
    de = de2 // 2
    nblk = p // EXPERT_ROWS
    row_spec = pl.BlockSpec((slabs, EXPERT_ROWS, LANES), lambda i, e, n: (0, i, 0))
    grid_spec = pltpu.PrefetchScalarGridSpec(
        num_scalar_prefetch=2,
        grid=(nblk,),
        in_specs=[
            row_spec,
            pl.BlockSpec((1, d, de2), lambda i, e, n: (e[i], 0, 0)),
            pl.BlockSpec((1, 1, de2), lambda i, e, n: (e[i], 0, 0)),
            pl.BlockSpec((1, de, d), lambda i, e, n: (e[i], 0, 0)),
            pl.BlockSpec((1, 1, d), lambda i, e, n: (e[i], 0, 0)),
        ],
        out_specs=row_spec,
        scratch_shapes=[pltpu.VMEM((d, de2), BF16), pltpu.VMEM((de, d), BF16)],
    )
    return pl.pallas_call(
        _moe_kernel,
        grid_spec=grid_spec,
        out_shape=jax.ShapeDtypeStruct(xs.shape, jnp.uint32),
        compiler_params=_params("arbitrary"),
        name="moe_experts",
    )(blk_e, blk_used, xs, w_gu, b_gu.reshape(ne, 1, de2), w_down, b_down.reshape(ne, 1, d))


def _moe_ffn(x1, h2, logits, w_gu, b_gu, w_down, b_down):
    t, d = x1.shape
    tk = t * TOPK_EXPERTS
    top_v, top_e = lax.top_k(logits, TOPK_EXPERTS)
    weights = jax.nn.softmax(top_v, axis=-1)
    e_flat = top_e.reshape(tk)
    expert_ids = jnp.arange(N_EXPERTS, dtype=e_flat.dtype)
    is_e = e_flat[:, None] == expert_ids[None, :]
    counts = jnp.sum(is_e.astype(jnp.int32), axis=0)
    order = jnp.argsort(e_flat, stable=True).astype(jnp.int32)
    pos = jnp.argsort(order).astype(jnp.int32)
    pcounts = ((counts + EXPERT_ROWS - 1) // EXPERT_ROWS) * EXPERT_ROWS
    pends = jnp.cumsum(pcounts)
    pstarts = pends - pcounts
    cstarts = jnp.cumsum(counts) - counts
    dest = jnp.sum(jnp.where(is_e, (pstarts - cstarts)[None, :], 0), axis=1) + pos
    nblk = -(-tk // EXPERT_ROWS) + N_EXPERTS
    blk_start = jnp.arange(nblk, dtype=jnp.int32) * EXPERT_ROWS
    blk_e = jnp.minimum(jnp.sum((pends[None, :] <= blk_start[:, None]).astype(jnp.int32), axis=1), N_EXPERTS - 1)
    blk_used = (blk_start < pends[-1]).astype(jnp.int32)
    slabs = d // 2 // LANES
    p = nblk * EXPERT_ROWS
    dest_by_k = dest.reshape(t, TOPK_EXPERTS).T
    slab_off = jnp.arange(slabs, dtype=jnp.int32) * p
    dest_kst = (dest_by_k[:, None, :] + slab_off[None, :, None]).reshape(-1)
    dest_skt = (dest_by_k[None, :, :] + slab_off[:, None, None]).reshape(-1)
    xs = _sc_scatter_rows(h2.reshape(slabs * t, LANES), dest_kst, slabs * p, TOPK_EXPERTS)
    ys = _moe_experts(xs.reshape(slabs, p, LANES), blk_e, blk_used, w_gu, b_gu, w_down, b_down)
    picked = _sc_gather_rows(ys.reshape(slabs * p, LANES), dest_skt).reshape(slabs, TOPK_EXPERTS, t, LANES)
    lo, hi = _unpack_bf16_pairs(picked)
    mix = lambda part: [sum(part[j, k] * weights[:, k:k + 1] for k in range(TOPK_EXPERTS)) for j in range(slabs)]
    return x1 + jnp.concatenate(mix(lo) + mix(hi), axis=1)


def kernel(x, mem, g_mix, w_in, lam_re, lam_im, log_dt, b_re, b_im, c_re, c_im, d_skip, w_glu, g_q, g_k, w_moba_out, g_mem, w_kv_mem, g_cq, g_ck, w_cross_out, w_out, g_ffn, w_router, b_router, w_gu, b_gu, w_down, b_down):
    bsz, seq_len, d = x.shape
    xt = x.reshape(bsz * seq_len, d)
    for l in range(g_mix.shape[0]):
        u, q2, kv, xqn, gates, qh = _in_proj(xt, g_mix[l], w_in[l], g_q[l], g_k[l], g_cq[l], seq_len)
        mats = _s5_matrices(lam_re[l], lam_im[l], log_dt[l], b_re[l], b_im[l], c_re[l], c_im[l],
                            seq_len // S5_CHUNK)
        ys = _s5(u, mats, bsz, seq_len)
        om = _moba(q2, qh, kv, bsz, seq_len)
        kc, vc = _mem_kv(mem, g_mem[l], w_kv_mem[l], g_ck[l])
        x1, h2, logits = _merge(xt, ys, u, d_skip[l], om, xqn, kc, vc, gates, w_glu[l], w_moba_out[l],
                                w_cross_out[l], w_out[l], g_ffn[l], w_router[l], b_router[l], seq_len)
        xt = _moe_ffn(x1, h2, logits, w_gu[l], b_gu[l], w_down[l], b_down[l])
    return xt.reshape(bsz, seq_len, d)
```

```python
import functools
import math

import jax
import jax.numpy as jnp
from jax import lax
from jax.experimental import pallas as pl
from jax.experimental.pallas import tpu as pltpu
from jax.experimental.pallas import tpu_sc as plsc

F32 = jnp.float32
BF16 = jnp.bfloat16

EPS = 1e-6
N_BRANCH = 3
SSM_GROUP = 16
SSM_STATE = 64
S5_CHUNK = 16
MOBA_HEAD_DIM = 64
MOBA_BLOCK = 256
MOBA_TOPK = 3
MOBA_MAX_BLOCKS = 64
ROPE_THETA = 10000.0
X_HEADS = 4
N_EXPERTS = 32
TOPK_EXPERTS = 4
SWIGLU_LIMIT = 7.0
SWIGLU_ALPHA = 1.702
EXPERT_ROWS = 512
NEG_BIG = -1e30
LANES = 128
VMEM_LIMIT_BYTES = 56 * 1024 * 1024


def _params(*sem):
    return pltpu.CompilerParams(dimension_semantics=sem, vmem_limit_bytes=VMEM_LIMIT_BYTES)


def _sigmoid(x):
    return 1.0 / (1.0 + jnp.exp(-x))


def _dot(a, b):
    return jnp.dot(a, b, preferred_element_type=F32)


def _pack_bf16_pairs(x):
    n = x.shape[1] // 2
    lo = lax.bitcast_convert_type(x[:, :n], jnp.uint32) >> 16
    hi = lax.bitcast_convert_type(x[:, n:], jnp.uint32) & jnp.uint32(0xFFFF0000)
    return lo | hi


def _unpack_bf16_pairs(w):
    lo = lax.bitcast_convert_type(w << 16, F32)
    hi = lax.bitcast_convert_type(w & jnp.uint32(0xFFFF0000), F32)
    return lo, hi


def _dot_nt(a, b):
    return lax.dot_general(a, b, (((1,), (1,)), ((), ())), preferred_element_type=F32)


def _inproj_kernel(x_ref, gmix_ref, wa_ref, wg_ref, e64_ref, gq_ref, gk_ref, gcq_ref, cos_ref, sin_ref,
                   u_ref, q_ref, kv_ref, xq_ref, g_ref, qh_ref):
    xf = x_ref[...]
    ms = jnp.mean(xf * xf, axis=-1, keepdims=True)
    h = (xf * lax.rsqrt(ms + EPS) * gmix_ref[...]).astype(BF16)
    a = _dot(h, wa_ref[...])
    w = u_ref.shape[1]
    u_ref[...] = a[:, :w]

    cos = jnp.tile(cos_ref[...], (1, w // LANES))
    sin = jnp.tile(sin_ref[...], (1, w // LANES))
    lane = lax.broadcasted_iota(jnp.int32, (xf.shape[0], w), 1)
    first_half = (lane % MOBA_HEAD_DIM) < (MOBA_HEAD_DIM // 2)

    def qk_norm_rope(raw, g):
        ss = _dot((raw * raw).astype(BF16), e64_ref[...])
        n = raw * lax.rsqrt(ss * (1.0 / MOBA_HEAD_DIM) + EPS) * g
        rot = jnp.where(first_half,
                        pltpu.roll(n, w - MOBA_HEAD_DIM // 2, 1),
                        pltpu.roll(n, MOBA_HEAD_DIM // 2, 1))
        return n * cos + rot * sin

    q = qk_norm_rope(a[:, w:2 * w], gq_ref[...]) * (MOBA_HEAD_DIM ** -0.5)
    q_ref[...] = q.astype(BF16)
    in_a = lax.broadcasted_iota(jnp.int32, (xf.shape[0], LANES), 1) < MOBA_HEAD_DIM
    for p in range(w // LANES):
        pair = q[:, p * LANES:(p + 1) * LANES].astype(BF16).astype(F32)
        qh_ref[2 * p] = jnp.where(in_a, pair, 0.0)
        qh_ref[2 * p + 1] = jnp.where(in_a, 0.0, pair)
    k = qk_norm_rope(a[:, 2 * w:3 * w], gk_ref[...]).astype(BF16)
    for p in range(w // LANES):
        kv_ref[:, (2 * p) * LANES:(2 * p + 1) * LANES] = k[:, p * LANES:(p + 1) * LANES]
        kv_ref[:, (2 * p + 1) * LANES:(2 * p + 2) * LANES] = a[:, 3 * w + p * LANES:3 * w + (p + 1) * LANES].astype(BF16)

    xq = a[:, 4 * w:5 * w]
    hd = w // X_HEADS
    for c in range(X_HEADS):
        chunk = xq[:, c * hd:(c + 1) * hd]
        cms = jnp.mean(chunk * chunk, axis=-1, keepdims=True)
        xq_ref[:, c * hd:(c + 1) * hd] = (chunk * lax.rsqrt(cms + EPS) * gcq_ref[...]).astype(BF16)

    d = xf.shape[1]
    for c in range(N_BRANCH):
        z = _dot(h, wg_ref[:, c * d:(c + 1) * d])
        g_ref[:, c * d:(c + 1) * d] = _sigmoid(z).astype(BF16)


def _in_proj(xt, g_mix, w_in, g_q, g_k, g_cq, seq_len, tm=256):
    t, d = xt.shape
    w = d // 2
    wa = w_in[:, :5 * w].astype(BF16)
    wg = w_in[:, 5 * w:].astype(BF16)
    heads = w // MOBA_HEAD_DIM
    e64 = jnp.kron(jnp.eye(heads, dtype=F32), jnp.ones((MOBA_HEAD_DIM, MOBA_HEAD_DIM), F32)).astype(BF16)
    half = MOBA_HEAD_DIM // 2
    inv = ROPE_THETA ** (-jnp.arange(half, dtype=F32) / half)
    ang = jnp.arange(seq_len, dtype=F32)[:, None] * inv[None, :]
    cos = jnp.tile(jnp.cos(ang), (1, LANES // half))
    sin = jnp.tile(jnp.concatenate([-jnp.sin(ang), jnp.sin(ang)], axis=1), (1, LANES // MOBA_HEAD_DIM))
    nt = seq_len // tm
    row = lambda i: (i, 0)
    const = lambda i: (0, 0)
    out_w = jax.ShapeDtypeStruct((t, w), BF16)
    return pl.pallas_call(
        _inproj_kernel,
        grid=(t // tm,),
        in_specs=[
            pl.BlockSpec((tm, d), row),
            pl.BlockSpec((1, d), const),
            pl.BlockSpec((d, 5 * w), const),
            pl.BlockSpec((d, N_BRANCH * d), const),
            pl.BlockSpec((w, w), const),
            pl.BlockSpec((1, w), const),
            pl.BlockSpec((1, w), const),
            pl.BlockSpec((1, w // X_HEADS), const),
            pl.BlockSpec((tm, LANES), lambda i: (i % nt, 0)),
            pl.BlockSpec((tm, LANES), lambda i: (i % nt, 0)),
        ],
        out_specs=[pl.BlockSpec((tm, w), row), pl.BlockSpec((tm, w), row), pl.BlockSpec((tm, 2 * w), row),
                   pl.BlockSpec((tm, w), row), pl.BlockSpec((tm, N_BRANCH * d), row),
                   pl.BlockSpec((heads, tm, LANES), lambda i: (0, i, 0))],
        out_shape=[jax.ShapeDtypeStruct((t, w), F32), out_w, jax.ShapeDtypeStruct((t, 2 * w), BF16), out_w,
                   jax.ShapeDtypeStruct((t, N_BRANCH * d), BF16),
                   jax.ShapeDtypeStruct((heads, t, LANES), F32)],
        compiler_params=_params("parallel"),
        name="in_proj",
    )(xt, g_mix.reshape(1, d), wa, wg, e64,
      jnp.tile(g_q, heads).reshape(1, w), jnp.tile(g_k, heads).reshape(1, w), g_cq.reshape(1, -1), cos, sin)


def _s5_matrices(lam_re, lam_im, log_dt, b_re, b_im, c_re, c_im, n_chunks):
    hp = lax.Precision.HIGHEST
    c = S5_CHUNK
    dt = jnp.exp(log_dt)[:, None]
    mag = jnp.exp(lam_re * dt)
    ar = mag * jnp.cos(lam_im * dt)
    ai = mag * jnp.sin(lam_im * dt)
    nr = ar - 1.0
    den = lam_re * lam_re + lam_im * lam_im
    cr = (nr * lam_re + ai * lam_im) / den
    ci = (ai * lam_re - nr * lam_im) / den
    bbr = cr[..., None] * b_re - ci[..., None] * b_im
    bbi = cr[..., None] * b_im + ci[..., None] * b_re

    def power(n):
        nf = n.astype(F32)[None, :, None]
        m = jnp.exp((lam_re * dt)[:, None, :] * nf)
        th = (lam_im * dt)[:, None, :] * nf
        return m * jnp.cos(th), m * jnp.sin(th)

    pr, pi = power(jnp.arange(c + 1))
    kbr = pr[..., None] * bbr[:, None] - pi[..., None] * bbi[:, None]
    kbi = pr[..., None] * bbi[:, None] + pi[..., None] * bbr[:, None]
    kk = (jnp.einsum('ghp,gtpc->gthc', c_re, kbr, precision=hp)
          - jnp.einsum('ghp,gtpc->gthc', c_im, kbi, precision=hp))
    tq = jnp.arange(c)
    g = kk.shape[0]
    rev = c - 1 - tq
    w_in = jnp.concatenate([kbr[:, rev].transpose(0, 1, 3, 2), kbi[:, rev].transpose(0, 1, 3, 2)], axis=-1)
    w_in = w_in.reshape(g, c * SSM_GROUP, 2 * SSM_STATE)
    prn, pin = pr[:, 1:], pi[:, 1:]
    wo_r = c_re[:, None] * prn[:, :, None, :] - c_im[:, None] * pin[:, :, None, :]
    wo_i = -c_re[:, None] * pin[:, :, None, :] - c_im[:, None] * prn[:, :, None, :]
    w_out = jnp.concatenate([wo_r, wo_i], axis=-1).transpose(0, 3, 1, 2).reshape(g, 2 * SSM_STATE, c * SSM_GROUP)
    n_steps = max(1, int(math.ceil(math.log2(n_chunks))))
    qr, qi = power(c * (2 ** jnp.arange(n_steps)))
    pa = jnp.concatenate([qr, qr], axis=-1)
    pb = jnp.concatenate([-qi, qi], axis=-1)
    return kk.astype(BF16), w_in.astype(BF16), w_out.astype(BF16), pa, pb


S5_SET = LANES // SSM_GROUP
S5_ROWS = 256


def _s5_kernel(u_ref, toep_ref, win_ref, wout_ref, pa_ref, pb_ref, y_ref, s_ref):
    c = S5_CHUNK
    nc = u_ref.shape[0] // c
    rb = min(S5_ROWS, nc)

    def chunk_rows(r0):
        return jnp.concatenate([u_ref[pl.ds(r0 * c + t, rb, stride=c), :] for t in range(c)], axis=1).astype(BF16)

    for blk in range(nc // rb):
        s_ref[blk * rb:(blk + 1) * rb, :] = _dot(chunk_rows(blk * rb), win_ref[0])
    row = lax.broadcasted_iota(jnp.int32, (nc, LANES), 0)
    n_steps = pa_ref.shape[1]
    for g in range(S5_SET):
        lanes = slice(g * LANES, (g + 1) * LANES)
        s = s_ref[:, lanes]
        for k in range(n_steps):
            sh = 1 << k
            if sh >= nc:
                break
            prev = jnp.where(row >= sh, pltpu.roll(s, sh, 0), 0.0)
            s = s + pa_ref[0, k:k + 1, lanes] * prev + pb_ref[0, k:k + 1, lanes] * pltpu.roll(prev, SSM_STATE, 1)
        s_ref[:, lanes] = jnp.where(row >= 1, pltpu.roll(s, 1, 0), 0.0)
    for blk in range(nc // rb):
        rows = slice(blk * rb, (blk + 1) * rb)
        y = _dot(chunk_rows(blk * rb), toep_ref[0]) + _dot(s_ref[rows, :].astype(BF16), wout_ref[0])
        for t in range(c):
            y_ref[pl.ds(blk * rb * c + t, rb, stride=c), :] = y[:, t * LANES:(t + 1) * LANES]


def _s5_block_diag(mats):
    kk, w_in, w_out, pa, pb = mats
    g, c, h, p2 = kk.shape[0], S5_CHUNK, SSM_GROUP, 2 * SSM_STATE
    ns = g // S5_SET
    eye = jnp.eye(S5_SET, dtype=kk.dtype)
    lag_bd = jnp.einsum('sglhc,gk->slgckh', kk.reshape(ns, S5_SET, c + 1, h, h), eye).reshape(ns, c + 1, LANES, LANES)
    tq = jnp.arange(c)
    lag = tq[None, :] - tq[:, None]
    toep_bd = jnp.where((lag >= 0)[None, :, :, None, None], lag_bd[:, jnp.clip(lag, 0, c)], 0)
    toep_bd = toep_bd.transpose(0, 1, 3, 2, 4).reshape(ns, c * LANES, c * LANES)

    def rows_to_states(w):
        wc = w.reshape(ns, S5_SET, c, h, p2).transpose(0, 2, 1, 3, 4).reshape(ns, c, LANES, p2)
        same = (jnp.arange(LANES)[:, None] // h) == (jnp.arange(S5_SET * p2)[None, :] // p2)
        return jnp.where(same, jnp.tile(wc, (1, 1, 1, S5_SET)), 0).reshape(ns, c * LANES, S5_SET * p2)

    win_bd = rows_to_states(w_in)
    wout_bd = rows_to_states(w_out.transpose(0, 2, 1)).transpose(0, 2, 1)
    k = pa.shape[1]
    lanes_of = lambda x: x.reshape(ns, S5_SET, k, p2).transpose(0, 2, 1, 3).reshape(ns, k, S5_SET * p2)
    return toep_bd, win_bd, wout_bd, lanes_of(pa), lanes_of(pb)


def _s5(u, mats, bsz, seq_len):
    toep, w_in, w_out, pa, pb = _s5_block_diag(mats)
    ns = toep.shape[0]
    nc = seq_len // S5_CHUNK
    per_set = lambda s, b: (s, 0, 0)
    once = dict(pipeline_mode=pl.Buffered(1))
    io_spec = pl.BlockSpec((seq_len, LANES), lambda s, b: (b, s), **once)
    return pl.pallas_call(
        _s5_kernel,
        grid=(ns, bsz),
        in_specs=[
            io_spec,
            pl.BlockSpec((1,) + toep.shape[1:], per_set, **once),
            pl.BlockSpec((1,) + w_in.shape[1:], per_set, **once),
            pl.BlockSpec((1,) + w_out.shape[1:], per_set, **once),
            pl.BlockSpec((1,) + pa.shape[1:], per_set),
            pl.BlockSpec((1,) + pb.shape[1:], per_set),
        ],
        out_specs=io_spec,
        out_shape=jax.ShapeDtypeStruct(u.shape, F32),
        scratch_shapes=[pltpu.VMEM((nc, S5_SET * 2 * SSM_STATE), F32)],
        compiler_params=_params("arbitrary", "arbitrary"),
        name="s5_scan",
    )(u, toep, w_in, w_out, pa, pb)


def _moba_select_kernel(q_ref, k_ref, tri_ref, sel_ref, cnt_out_ref, km_ref, cnt_ref):
    h = pl.program_id(1)
    t = pl.program_id(2)
    ts = q_ref.shape[0]
    nbk = MOBA_MAX_BLOCKS

    @pl.when(t == 0)
    def _():
        kk = k_ref[...].astype(F32)
        nb = kk.shape[0] // MOBA_BLOCK
        km = jnp.sum(kk.reshape(nb, MOBA_BLOCK, LANES), axis=1) * (1.0 / MOBA_BLOCK)
        if nb < nbk:
            km = jnp.concatenate([km, jnp.zeros((nbk - nb, LANES), F32)], axis=0)
        lane = lax.broadcasted_iota(jnp.int32, (nbk, LANES), 1)
        km_ref[...] = jnp.where((lane // MOBA_HEAD_DIM) == (h % 2), km, 0.0).astype(BF16)
        cnt_ref[...] = jnp.zeros(cnt_ref.shape, F32)

    gate = _dot_nt(km_ref[...], q_ref[...])
    blk = lax.broadcasted_iota(jnp.int32, gate.shape, 0)
    qblk = (t * ts + lax.broadcasted_iota(jnp.int32, gate.shape, 1)) // MOBA_BLOCK
    g = jnp.where(blk < qblk, gate, -jnp.inf)
    selected = jnp.zeros(gate.shape, jnp.bool_)
    picks = []
    for _ in range(MOBA_TOPK):
        mx = jnp.max(g, axis=0, keepdims=True)
        idx = jnp.min(jnp.where(g == mx, blk, nbk), axis=0, keepdims=True)
        hit = blk == idx
        ok = (idx[0:1] < qblk[0:1]) & (mx > -jnp.inf)
        picks.append((hit, idx, ok))
        selected = selected | (hit & ok)
        g = jnp.where(hit, -jnp.inf, g)
    sel01 = jnp.where(selected, 1.0, 0.0).astype(BF16)
    prefix = _dot(sel01, tri_ref[...]) + jnp.tile(cnt_ref[...], (1, ts // LANES))
    rows = [jnp.where(ok, idx, -1) for (_, idx, ok) in picks]
    rows += [jnp.sum(jnp.where(hit, prefix, 0.0), axis=0, keepdims=True).astype(jnp.int32) for (hit, _, _) in picks]
    rows += [jnp.zeros((1, ts), jnp.int32)] * (8 - 2 * MOBA_TOPK)
    sel_ref[0, 0] = jnp.concatenate(rows, axis=0)
    cnt_ref[...] = cnt_ref[...] + _dot(sel01, jnp.ones((ts, LANES), BF16))
    cnt_out_ref[0, 0] = cnt_ref[...]


def _moba_select(q2, kv, bsz, seq_len, ts=1024):
    ts = min(ts, seq_len)
    heads = q2.shape[1] // MOBA_HEAD_DIM
    nt = seq_len // ts
    tri = (jnp.arange(ts)[:, None] < jnp.arange(ts)[None, :]).astype(BF16)
    return pl.pallas_call(
        _moba_select_kernel,
        grid=(bsz, heads, nt),
        in_specs=[
            pl.BlockSpec((ts, LANES), lambda b, h, t: (b * nt + t, h // 2)),
            pl.BlockSpec((seq_len, LANES), lambda b, h, t: (b, 2 * (h // 2))),
            pl.BlockSpec((ts, ts), lambda b, h, t: (0, 0)),
        ],
        out_specs=[pl.BlockSpec((1, 1, 8, ts), lambda b, h, t: (b, h, 0, t)),
                   pl.BlockSpec((1, 1, MOBA_MAX_BLOCKS, LANES), lambda b, h, t: (b, h, 0, 0))],
        out_shape=[jax.ShapeDtypeStruct((bsz, heads, 8, seq_len), jnp.int32),
                   jax.ShapeDtypeStruct((bsz, heads, MOBA_MAX_BLOCKS, LANES), F32)],
        scratch_shapes=[pltpu.VMEM((MOBA_MAX_BLOCKS, LANES), BF16), pltpu.VMEM((MOBA_MAX_BLOCKS, LANES), F32)],
        compiler_params=_params("parallel", "arbitrary", "arbitrary"),
        name="moba_select",
    )(q2, kv, tri)


def _softmax_block(q, kb, vb, causal):
    s = _dot_nt(q, kb)
    if causal:
        r = lax.broadcasted_iota(jnp.int32, s.shape, 0) % MOBA_BLOCK
        c = lax.broadcasted_iota(jnp.int32, s.shape, 1)
        s = jnp.where(c <= r, s, NEG_BIG)
    m = jnp.max(s, axis=1, keepdims=True)
    p = jnp.exp(s - m).astype(BF16)
    pv = _dot(p, jnp.concatenate([vb, jnp.ones(vb.shape, BF16)], axis=1))
    l = pv[:, LANES:LANES + 1]
    return pv[:, :LANES] / l, m + jnp.log(l)


MOBA_DIAG_BLOCKS = 2


def _moba_diag_kernel(q_ref, kv_ref, o_ref, lse_ref):
    lane = lax.broadcasted_iota(jnp.int32, (MOBA_BLOCK, LANES), 1)
    is_a = lane < MOBA_HEAD_DIM
    for u in range(MOBA_DIAG_BLOCKS):
        rows = pl.ds(u * MOBA_BLOCK, MOBA_BLOCK)
        q = q_ref[rows, :]
        zero = jnp.zeros_like(q)
        qs = jnp.concatenate([jnp.where(is_a, q, zero), jnp.where(is_a, zero, q)], axis=0)
        o, lse = _softmax_block(qs, kv_ref[rows, :LANES], kv_ref[rows, LANES:], True)
        o_ref[rows, :] = jnp.where(is_a, o[:MOBA_BLOCK], o[MOBA_BLOCK:])
        lse_ref[rows, :] = jnp.where(is_a, lse[:MOBA_BLOCK], lse[MOBA_BLOCK:])


def _moba_diag(q2, kv, bsz, seq_len):
    npair = q2.shape[1] // LANES
    rows = MOBA_BLOCK * MOBA_DIAG_BLOCKS
    nb = seq_len // rows
    spec = pl.BlockSpec((rows, LANES), lambda b, p, i: (b * nb + i, p))
    out = jax.ShapeDtypeStruct(q2.shape, F32)
    return pl.pallas_call(
        _moba_diag_kernel,
        grid=(bsz, npair, nb),
        in_specs=[spec, pl.BlockSpec((rows, 2 * LANES), lambda b, p, i: (b * nb + i, p))],
        out_specs=[spec, spec],
        out_shape=[out, out],
        compiler_params=_params("parallel", "parallel", "parallel"),
        name="moba_diag",
    )(q2, kv)


MOBA_TILE = 256
MOBA_TILES_PER_STEP = 8


def _moba_grouped_kernel(tile_row_ref, tile_pair_ref, tile_half_ref, tile_real_ref, qd_ref, *refs):
    n = MOBA_TILES_PER_STEP
    kv_refs, o_ref = refs[:n], refs[n]
    i = pl.program_id(0)
    lane = lax.broadcasted_iota(jnp.int32, (MOBA_TILE, LANES), 1)
    kv_lane = lax.broadcasted_iota(jnp.int32, (MOBA_BLOCK, LANES), 1)
    for u in range(n):
        tile = i * n + u
        rows = pl.ds(u * MOBA_TILE, MOBA_TILE)
        own = (lane // MOBA_HEAD_DIM) == tile_half_ref[tile]
        s = _dot_nt(qd_ref[rows, :].astype(BF16), kv_refs[u][:, :LANES])
        m = jnp.max(s, axis=1, keepdims=True)
        p = jnp.exp(s - m).astype(BF16)
        vb = jnp.where((kv_lane // MOBA_HEAD_DIM) == tile_half_ref[tile], kv_refs[u][:, LANES:], 1.0)
        pv = _dot(p, vb)
        row_sum = pltpu.roll(pv, MOBA_HEAD_DIM, 1)
        part = jnp.where(own, pv / row_sum, m + jnp.log(pv))
        o_ref[rows, :] = jnp.where(tile_real_ref[tile] > 0, part, NEG_BIG)


def _moba_grouped(qd, kv, tile_row, tile_pair, tile_half, tile_real):
    n = MOBA_TILES_PER_STEP
    n_tiles = qd.shape[0] // MOBA_TILE

    def kv_spec(u):
        return pl.BlockSpec((MOBA_BLOCK, 2 * LANES), lambda i, tr, tp, th, tl: (tr[i * n + u], tp[i * n + u]))

    grid_spec = pltpu.PrefetchScalarGridSpec(
        num_scalar_prefetch=4,
        grid=(n_tiles // n,),
        in_specs=[pl.BlockSpec((n * MOBA_TILE, LANES), lambda i, tr, tp, th, tl: (i, 0))]
        + [kv_spec(u) for u in range(n)],
        out_specs=pl.BlockSpec((n * MOBA_TILE, LANES), lambda i, tr, tp, th, tl: (i, 0)),
    )
    return pl.pallas_call(
        _moba_grouped_kernel,
        grid_spec=grid_spec,
        out_shape=jax.ShapeDtypeStruct(qd.shape, F32),
        compiler_params=_params("arbitrary"),
        name="moba_grouped",
    )(tile_row, tile_pair, tile_half, tile_real, qd, *([kv] * n))


def _moba_combine_kernel(od_ref, lsed_ref, *refs):
    g_refs, o_ref = refs[:-1], refs[-1]
    lane = lax.broadcasted_iota(jnp.int32, od_ref.shape, 1)
    is_a = lane < MOBA_HEAD_DIM
    parts = [(od_ref[...], lsed_ref[...])]
    for s in range(MOBA_TOPK):
        xa = g_refs[s][0, 0, 0]
        xb = g_refs[MOBA_TOPK + s][0, 0, 0]
        o = jnp.where(is_a, xa, xb)
        lse = pltpu.roll(jnp.where(is_a, xb, xa), MOBA_HEAD_DIM, 1)
        parts.append((o, lse))
    m = parts[0][1]
    for _, lse in parts[1:]:
        m = jnp.maximum(m, lse)
    num = jnp.zeros(od_ref.shape, F32)
    den = jnp.zeros(od_ref.shape, F32)
    for o, lse in parts:
        w = jnp.exp(lse - m)
        num = num + w * o
        den = den + w
    o_ref[...] = (num / den).astype(BF16)


def _moba_combine(od, lsed, gath, bsz, seq_len, tm=1024):
    tm = min(tm, seq_len)
    npair = od.shape[1] // LANES
    nt = seq_len // tm
    spec = pl.BlockSpec((tm, LANES), lambda b, p, i: (b * nt + i, p))

    def g_spec(e, s):
        return pl.BlockSpec((1, 1, 1, tm, LANES), lambda b, p, i: (b, 2 * p + e, s, i, 0))

    return pl.pallas_call(
        _moba_combine_kernel,
        grid=(bsz, npair, nt),
        in_specs=[spec, spec] + [g_spec(e, s) for e in range(2) for s in range(MOBA_TOPK)],
        out_specs=spec,
        out_shape=jax.ShapeDtypeStruct(od.shape, BF16),
        compiler_params=_params("parallel", "parallel", "parallel"),
        name="moba_combine",
    )(od, lsed, *([gath] * (2 * MOBA_TOPK)))


SC_WINDOW = 128
SC_CORES = 2
SC_SUBCORES = 16


def _sc_mesh():
    return plsc.VectorSubcoreMesh(core_axis_name="core", subcore_axis_name="subcore")


def _sc_gather_rows(table, idx):
    n = idx.shape[0]
    d = table.shape[1]
    window = SC_WINDOW
    assert n % (window * SC_CORES * SC_SUBCORES) == 0
    per_core = n // window // SC_CORES

    @functools.partial(pl.kernel, out_type=jax.ShapeDtypeStruct((n, d), table.dtype), mesh=_sc_mesh())
    def gather_kernel(x_hbm, i_hbm, o_hbm):
        base = lax.axis_index("core") * per_core

        def body(i_vmem, o_vmem):
            pltpu.sync_copy(x_hbm.at[i_vmem.at[0]], o_vmem)

        pltpu.emit_pipeline(
            body,
            grid=(per_core,),
            in_specs=[pl.BlockSpec((1, window), index_map=lambda i: (0, base + i))],
            out_specs=[pl.BlockSpec((window, d), index_map=lambda i: (base + i, 0))],
            core_axis_name="subcore",
            dimension_semantics=(pltpu.PARALLEL,),
            trace_scopes=False,
        )(i_hbm, o_hbm)

    return gather_kernel(table, idx.reshape(1, n))


def _sc_scatter_rows(rows, dest, n_out, repeat):
    n_src, d = rows.shape
    n = dest.shape[0]
    assert n == repeat * n_src
    window = SC_WINDOW
    assert n % (window * SC_CORES * SC_SUBCORES) == 0
    per_core = n // window // SC_CORES
    src_windows = n_src // window

    @functools.partial(pl.kernel, out_type=jax.ShapeDtypeStruct((n_out, d), rows.dtype), mesh=_sc_mesh())
    def scatter_kernel(x_hbm, i_hbm, o_hbm):
        base = lax.axis_index("core") * per_core

        def body(x_vmem, i_vmem):
            pltpu.sync_copy(x_vmem, o_hbm.at[i_vmem.at[0]])

        pltpu.emit_pipeline(
            body,
            grid=(per_core,),
            in_specs=[pl.BlockSpec((window, d), index_map=lambda i: ((base + i) % src_windows, 0)),
                      pl.BlockSpec((1, window), index_map=lambda i: (0, base + i))],
            out_specs=[],
            core_axis_name="subcore",
            dimension_semantics=(pltpu.PARALLEL,),
            trace_scopes=False,
        )(x_hbm, i_hbm)

    return scatter_kernel(rows, dest.reshape(1, n))


def _moba(q2, qh, kv, bsz, seq_len):
    heads = q2.shape[1] // MOBA_HEAD_DIM
    nbk = MOBA_MAX_BLOCKS
    sel, cnt = _moba_select(q2, kv, bsz, seq_len)
    od, lsed = _moba_diag(q2, kv, bsz, seq_len)
    counts = cnt[..., 0].astype(jnp.int32).reshape(bsz * heads * nbk)
    pcounts = ((counts + MOBA_TILE - 1) // MOBA_TILE) * MOBA_TILE
    pends = jnp.cumsum(pcounts)
    pstarts = (pends - pcounts).reshape(bsz, heads, 1, 1, nbk)
    n_items = bsz * heads * seq_len * MOBA_TOPK
    step_rows = MOBA_TILE * MOBA_TILES_PER_STEP
    n_rows = -(-(n_items + bsz * heads * nbk * MOBA_TILE) // step_rows) * step_rows
    n_null = step_rows
    idx = sel[:, :, 0:MOBA_TOPK, :]
    rank = sel[:, :, MOBA_TOPK:2 * MOBA_TOPK, :]
    start = jnp.sum(jnp.where(idx[..., None] == jnp.arange(nbk), pstarts, 0), axis=-1)
    null_row = n_rows + jnp.arange(seq_len, dtype=jnp.int32) % n_null
    dest = jnp.where(idx >= 0, start + rank, null_row)
    n_tiles = (n_rows + n_null) // MOBA_TILE
    n_groups = bsz * heads * nbk
    tile_start = jnp.arange(n_tiles, dtype=jnp.int32) * MOBA_TILE
    tile_g = jnp.minimum(jnp.sum((pends[None, :] <= tile_start[:, None]).astype(jnp.int32), axis=1), n_groups - 1)
    tile_real = (tile_start < pends[-1]).astype(jnp.int32)
    tile_head = (tile_g // nbk) % heads
    tile_row = (tile_g // (heads * nbk)) * (seq_len // MOBA_BLOCK) + jnp.minimum(tile_g % nbk, seq_len // MOBA_BLOCK - 1)
    dest_by_slot = dest.transpose(2, 1, 0, 3).reshape(-1)
    qd = _sc_scatter_rows(qh.reshape(-1, LANES), dest_by_slot, n_rows + n_null, MOBA_TOPK)
    part = _moba_grouped(qd, kv, tile_row, tile_head // 2, tile_head % 2, tile_real)
    gath = _sc_gather_rows(part, dest.reshape(-1)).reshape(bsz, heads, MOBA_TOPK, seq_len, LANES)
    return _moba_combine(od, lsed, gath, bsz, seq_len)


def _mem_kv_kernel(mem_ref, gmem_ref, w_ref, gck_ref, k_ref, v_ref):
    xf = mem_ref[0]
    ms = jnp.mean(xf * xf, axis=-1, keepdims=True)
    h = (xf * lax.rsqrt(ms + EPS) * gmem_ref[...]).astype(BF16)
    kv = _dot(h, w_ref[...])
    w = k_ref.shape[2]
    hd = w // X_HEADS
    for c in range(X_HEADS):
        chunk = kv[:, c * hd:(c + 1) * hd]
        cms = jnp.mean(chunk * chunk, axis=-1, keepdims=True)
        k_ref[0, :, c * hd:(c + 1) * hd] = (chunk * lax.rsqrt(cms + EPS) * gck_ref[...]).astype(BF16)
    v_ref[0] = kv[:, w:].astype(BF16)


def _mem_kv(mem, g_mem, w_kv_mem, g_ck):
    bsz, m, d = mem.shape
    w = w_kv_mem.shape[1] // 2
    const = lambda b: (0, 0)
    out = jax.ShapeDtypeStruct((bsz, m, w), BF16)
    return pl.pallas_call(
        _mem_kv_kernel,
        grid=(bsz,),
        in_specs=[pl.BlockSpec((1, m, d), lambda b: (b, 0, 0)), pl.BlockSpec((1, d), const),
                  pl.BlockSpec((d, 2 * w), const), pl.BlockSpec((1, w // X_HEADS), const)],
        out_specs=[pl.BlockSpec((1, m, w), lambda b: (b, 0, 0))] * 2,
        out_shape=[out, out],
        compiler_params=_params("parallel"),
        name="mem_kv",
    )(mem, g_mem.reshape(1, d), w_kv_mem.astype(BF16), g_ck.reshape(1, -1))


def _merge_kernel(x_ref, ys_ref, u_ref, dskip_ref, om_ref, xq_ref, kc_ref, vc_ref, g_ref,
                  wglu_ref, wmo_ref, wco_ref, wout_ref, gffn_ref, wr_ref, br_ref,
                  x1_ref, h2_ref, logit_ref):
    d = x_ref.shape[1]
    y = ys_ref[...].astype(F32) + dskip_ref[...] * u_ref[...].astype(F32)
    ge = 0.5 * y * (1.0 + jnp.tanh(math.sqrt(2.0 / math.pi) * (y + 0.044715 * (y * y * y))))
    z = _dot(ge.astype(BF16), wglu_ref[...])
    merged = g_ref[:, 0:d].astype(F32) * (z[:, :d] * _sigmoid(z[:, d:]))
    merged = merged + g_ref[:, d:2 * d].astype(F32) * _dot(om_ref[...], wmo_ref[...])
    w = xq_ref.shape[1]
    hd = w // X_HEADS
    heads = []
    for c in range(X_HEADS):
        s = _dot_nt(xq_ref[:, c * hd:(c + 1) * hd], kc_ref[0, :, c * hd:(c + 1) * hd]) * (hd ** -0.5)
        p = jnp.exp(s - jnp.max(s, axis=1, keepdims=True))
        p = p / jnp.sum(p, axis=1, keepdims=True)
        heads.append(_dot(p.astype(BF16), vc_ref[0, :, c * hd:(c + 1) * hd]))
    oc = jnp.concatenate(heads, axis=1).astype(BF16)
    merged = merged + g_ref[:, 2 * d:3 * d].astype(F32) * _dot(oc, wco_ref[...])
    x1 = x_ref[...] + _dot(merged.astype(BF16), wout_ref[...])
    x1_ref[...] = x1
    ms = jnp.mean(x1 * x1, axis=-1, keepdims=True)
    h2 = (x1 * lax.rsqrt(ms + EPS) * gffn_ref[...]).astype(BF16)
    packed = _pack_bf16_pairs(h2.astype(F32))
    for j in range(h2_ref.shape[0]):
        h2_ref[j] = packed[:, j * LANES:(j + 1) * LANES]
    logit_ref[...] = _dot_nt(wr_ref[...], h2) + br_ref[...]


def _merge(xt, ys, u, d_skip, om, xqn, kc, vc, gates, w_glu, w_mo, w_co, w_out, g_ffn, w_router, b_router,
           seq_len, tm=256):
    t, d = xt.shape
    w = d // 2
    m = kc.shape[1]
    ne = w_router.shape[1]
    nt = seq_len // tm
    row = lambda i: (i, 0)
    const = lambda i: (0, 0)
    per_b = lambda i: (i // nt, 0, 0)
    return pl.pallas_call(
        _merge_kernel,
        grid=(t // tm,),
        in_specs=[
            pl.BlockSpec((tm, d), row), pl.BlockSpec((tm, w), row), pl.BlockSpec((tm, w), row),
            pl.BlockSpec((1, w), const), pl.BlockSpec((tm, w), row), pl.BlockSpec((tm, w), row),
            pl.BlockSpec((1, m, w), per_b), pl.BlockSpec((1, m, w), per_b),
            pl.BlockSpec((tm, N_BRANCH * d), row),
            pl.BlockSpec((w, 2 * d), const), pl.BlockSpec((w, d), const), pl.BlockSpec((w, d), const),
            pl.BlockSpec((d, d), const), pl.BlockSpec((1, d), const),
            pl.BlockSpec((ne, d), const), pl.BlockSpec((ne, 1), const),
        ],
        out_specs=[pl.BlockSpec((tm, d), row), pl.BlockSpec((d // 2 // LANES, tm, LANES), lambda i: (0, i, 0)),
                   pl.BlockSpec((ne, tm), lambda i: (0, i))],
        out_shape=[jax.ShapeDtypeStruct((t, d), F32), jax.ShapeDtypeStruct((d // 2 // LANES, t, LANES), jnp.uint32),
                   jax.ShapeDtypeStruct((ne, t), F32)],
        compiler_params=_params("parallel"),
        name="merge",
    )(xt, ys, u, d_skip.reshape(1, w), om, xqn, kc, vc, gates,
      w_glu.astype(BF16), w_mo.astype(BF16), w_co.astype(BF16), w_out.astype(BF16),
      g_ffn.reshape(1, d), w_router.T.astype(BF16), b_router.reshape(ne, 1))


def _moe_kernel(blk_e_ref, blk_used_ref, xs_ref, wgu_ref, bgu_ref, wd_ref, bd_ref, y_ref, wgu_bf, wd_bf):
    i = pl.program_id(0)
    prev = blk_e_ref[jnp.maximum(i - 1, 0)]

    @pl.when((i == 0) | (blk_e_ref[i] != prev))
    def _():
        wgu_bf[...] = wgu_ref[0].astype(BF16)
        wd_bf[...] = wd_ref[0].astype(BF16)

    @pl.when(blk_used_ref[i] > 0)
    def _():
        de = wd_bf.shape[0]
        words = jnp.concatenate([xs_ref[j] for j in range(xs_ref.shape[0])], axis=1)
        xs = jnp.concatenate(_unpack_bf16_pairs(words), axis=1).astype(BF16)
        gu = _dot(xs, wgu_bf[...]) + bgu_ref[0]
        gate = jnp.minimum(gu[:, :de], SWIGLU_LIMIT)
        up = jnp.clip(gu[:, de:], -SWIGLU_LIMIT, SWIGLU_LIMIT)
        act = gate * _sigmoid(SWIGLU_ALPHA * gate) * (up + 1.0)
        y = _dot(act.astype(BF16), wd_bf[...]) + bd_ref[0]
        packed = _pack_bf16_pairs(y.astype(BF16).astype(F32))
        for j in range(y_ref.shape[0]):
            y_ref[j] = packed[:, j * LANES:(j + 1) * LANES]

    @pl.when(blk_used_ref[i] == 0)
    def _():
        y_ref[...] = jnp.zeros(y_ref.shape, y_ref.dtype)


def _moe_experts(xs, blk_e, blk_used, w_gu, b_gu, w_down, b_down):
    slabs, p, _ = xs.shape
    d = 2 * slabs * LANES
    ne, _, de2 = w_gu.shape
    de = de2 // 2
    nblk = p // EXPERT_ROWS
    row_spec = pl.BlockSpec((slabs, EXPERT_ROWS, LANES), lambda i, e, n: (0, i, 0))
    grid_spec = pltpu.PrefetchScalarGridSpec(
        num_scalar_prefetch=2,
        grid=(nblk,),
        in_specs=[
            row_spec,
            pl.BlockSpec((1, d, de2), lambda i, e, n: (e[i], 0, 0)),
            pl.BlockSpec((1, 1, de2), lambda i, e, n: (e[i], 0, 0)),
            pl.BlockSpec((1, de, d), lambda i, e, n: (e[i], 0, 0)),
            pl.BlockSpec((1, 1, d), lambda i, e, n: (e[i], 0, 0)),
        ],
        out_specs=row_spec,
        scratch_shapes=[pltpu.VMEM((d, de2), BF16), pltpu.VMEM((de, d), BF16)],
    )
    return pl.pallas_call(
        _moe_kernel,
        grid_spec=grid_spec,
        out_shape=jax.ShapeDtypeStruct(xs.shape, jnp.uint32),
        compiler_params=_params("arbitrary"),
        name="moe_experts",
    )(blk_e, blk_used, xs, w_gu, b_gu.reshape(ne, 1, de2), w_down, b_down.reshape(ne, 1, d))


def _router_kernel(lg_ref, tri_ref, e_ref, w_ref, r_ref, cnt_out_ref, cnt_ref):
    t = pl.program_id(0)
    ne, ts = lg_ref.shape

    @pl.when(t == 0)
    def _():
        cnt_ref[...] = jnp.zeros(cnt_ref.shape, F32)

    g = lg_ref[...]
    eid = lax.broadcasted_iota(jnp.int32, g.shape, 0)
    selected = jnp.zeros(g.shape, jnp.bool_)
    picks = []
    for _ in range(TOPK_EXPERTS):
        mx = jnp.max(g, axis=0, keepdims=True)
        idx = jnp.min(jnp.where(g == mx, eid, ne), axis=0, keepdims=True)
        hit = eid == idx
        picks.append((hit, idx, mx))
        selected = selected | hit
        g = jnp.where(hit, -jnp.inf, g)
    exps = [jnp.exp(mx - picks[0][2]) for (_, _, mx) in picks]
    total = sum(exps)
    sel01 = jnp.where(selected, 1.0, 0.0).astype(BF16)
    prefix = _dot(sel01, tri_ref[...]) + jnp.tile(cnt_ref[...], (1, ts // LANES))
    pad_i = [jnp.zeros((1, ts), jnp.int32)] * (8 - TOPK_EXPERTS)
    e_ref[...] = jnp.concatenate([idx for (_, idx, _) in picks] + pad_i, axis=0)
    w_ref[...] = jnp.concatenate([e / total for e in exps] + [jnp.zeros((1, ts), F32)] * (8 - TOPK_EXPERTS), axis=0)
    r_ref[...] = jnp.concatenate(
        [jnp.sum(jnp.where(hit, prefix, 0.0), axis=0, keepdims=True).astype(jnp.int32) for (hit, _, _) in picks]
        + pad_i, axis=0)
    cnt_ref[...] = cnt_ref[...] + _dot(sel01, jnp.ones((ts, LANES), BF16))
    cnt_out_ref[...] = cnt_ref[...]


def _router(logits_t, ts=1024):
    ne, t = logits_t.shape
    ts = min(ts, t)
    tri = (jnp.arange(ts)[:, None] < jnp.arange(ts)[None, :]).astype(BF16)
    rows = pl.BlockSpec((8, ts), lambda i: (0, i))
    return pl.pallas_call(
        _router_kernel,
        grid=(t // ts,),
        in_specs=[pl.BlockSpec((ne, ts), lambda i: (0, i)), pl.BlockSpec((ts, ts), lambda i: (0, 0))],
        out_specs=[rows, rows, rows, pl.BlockSpec((ne, LANES), lambda i: (0, 0))],
        out_shape=[jax.ShapeDtypeStruct((8, t), jnp.int32), jax.ShapeDtypeStruct((8, t), F32),
                   jax.ShapeDtypeStruct((8, t), jnp.int32), jax.ShapeDtypeStruct((ne, LANES), F32)],
        scratch_shapes=[pltpu.VMEM((ne, LANES), F32)],
        compiler_params=_params("arbitrary"),
        name="moe_router",
    )(logits_t, tri)


def _moe_mix_kernel(x1_ref, w_ref, pk_ref, o_ref):
    slabs = pk_ref.shape[0]
    wts = w_ref[...]
    lo = [jnp.zeros((x1_ref.shape[0], LANES), F32)] * slabs
    hi = [jnp.zeros((x1_ref.shape[0], LANES), F32)] * slabs
    for k in range(TOPK_EXPERTS):
        wk = wts[:, k:k + 1]
        for j in range(slabs):
            a, b = _unpack_bf16_pairs(pk_ref[j, k])
            lo[j] = lo[j] + wk * a
            hi[j] = hi[j] + wk * b
    o_ref[...] = x1_ref[...] + jnp.concatenate(lo + hi, axis=1)


def _moe_mix(x1, weights_tk, picked, tm=512):
    t, d = x1.shape
    slabs = picked.shape[0]
    return pl.pallas_call(
        _moe_mix_kernel,
        grid=(t // tm,),
        in_specs=[pl.BlockSpec((tm, d), lambda i: (i, 0)),
                  pl.BlockSpec((tm, TOPK_EXPERTS), lambda i: (i, 0)),
                  pl.BlockSpec((slabs, TOPK_EXPERTS, tm, LANES), lambda i: (0, 0, i, 0))],
        out_specs=pl.BlockSpec((tm, d), lambda i: (i, 0)),
        out_shape=jax.ShapeDtypeStruct((t, d), F32),
        compiler_params=_params("parallel"),
        name="moe_mix",
    )(x1, weights_tk, picked)


def _moe_ffn(x1, h2, logits_t, w_gu, b_gu, w_down, b_down):
    t, d = x1.shape
    tk = t * TOPK_EXPERTS
    e8, w8, r8, cnt = _router(logits_t)
    top_e, weights, rank = e8[:TOPK_EXPERTS], w8[:TOPK_EXPERTS], r8[:TOPK_EXPERTS]
    counts = cnt[:, 0].astype(jnp.int32)
    pcounts = ((counts + EXPERT_ROWS - 1) // EXPERT_ROWS) * EXPERT_ROWS
    pends = jnp.cumsum(pcounts)
    pstarts = pends - pcounts
    is_e = top_e[..., None] == jnp.arange(N_EXPERTS, dtype=jnp.int32)
    dest_by_k = jnp.sum(jnp.where(is_e, pstarts, 0), axis=-1) + rank
    nblk = -(-tk // EXPERT_ROWS) + N_EXPERTS
    blk_start = jnp.arange(nblk, dtype=jnp.int32) * EXPERT_ROWS
    blk_e = jnp.minimum(jnp.sum((pends[None, :] <= blk_start[:, None]).astype(jnp.int32), axis=1), N_EXPERTS - 1)
    blk_used = (blk_start < pends[-1]).astype(jnp.int32)
    slabs = d // 2 // LANES
    p = nblk * EXPERT_ROWS
    slab_off = jnp.arange(slabs, dtype=jnp.int32) * p
    dest_kst = (dest_by_k[:, None, :] + slab_off[None, :, None]).reshape(-1)
    dest_skt = (dest_by_k[None, :, :] + slab_off[:, None, None]).reshape(-1)
    xs = _sc_scatter_rows(h2.reshape(slabs * t, LANES), dest_kst, slabs * p, TOPK_EXPERTS)
    ys = _moe_experts(xs.reshape(slabs, p, LANES), blk_e, blk_used, w_gu, b_gu, w_down, b_down)
    picked = _sc_gather_rows(ys.reshape(slabs * p, LANES), dest_skt).reshape(slabs, TOPK_EXPERTS, t, LANES)
    return _moe_mix(x1, weights.T, picked)


def kernel(x, mem, g_mix, w_in, lam_re, lam_im, log_dt, b_re, b_im, c_re, c_im, d_skip, w_glu, g_q, g_k, w_moba_out, g_mem, w_kv_mem, g_cq, g_ck, w_cross_out, w_out, g_ffn, w_router, b_router, w_gu, b_gu, w_down, b_down):
    bsz, seq_len, d = x.shape
    xt = x.reshape(bsz * seq_len, d)
    for l in range(g_mix.shape[0]):
        u, q2, kv, xqn, gates, qh = _in_proj(xt, g_mix[l], w_in[l], g_q[l], g_k[l], g_cq[l], seq_len)
        mats = _s5_matrices(lam_re[l], lam_im[l], log_dt[l], b_re[l], b_im[l], c_re[l], c_im[l],
                            seq_len // S5_CHUNK)
        ys = _s5(u, mats, bsz, seq_len)
        om = _moba(q2, qh, kv, bsz, seq_len)
        kc, vc = _mem_kv(mem, g_mem[l], w_kv_mem[l], g_ck[l])
        x1, h2, logits = _merge(xt, ys, u, d_skip[l], om, xqn, kc, vc, gates, w_glu[l], w_moba_out[l],
                                w_cross_out[l], w_out[l], g_ffn[l], w_router[l], b_router[l], seq_len)
        xt = _moe_ffn(x1, h2, logits, w_gu[l], b_gu[l], w_down[l], b_down[l])
    return xt.reshape(bsz, seq_len, d)
```

```python
import functools
import math

import jax
import jax.numpy as jnp
from jax import lax
from jax.experimental import pallas as pl
from jax.experimental.pallas import tpu as pltpu
from jax.experimental.pallas import tpu_sc as plsc

F32 = jnp.float32
BF16 = jnp.bfloat16

EPS = 1e-6
N_BRANCH = 3
SSM_GROUP = 16
SSM_STATE = 64
S5_CHUNK = 16
MOBA_HEAD_DIM = 64
MOBA_BLOCK = 256
MOBA_TOPK = 3
MOBA_MAX_BLOCKS = 64
ROPE_THETA = 10000.0
X_HEADS = 4
N_EXPERTS = 32
TOPK_EXPERTS = 4
SWIGLU_LIMIT = 7.0
SWIGLU_ALPHA = 1.702
EXPERT_ROWS = 512
NEG_BIG = -1e30
LANES = 128
VMEM_LIMIT_BYTES = 56 * 1024 * 1024


def _params(*sem):
    return pltpu.CompilerParams(dimension_semantics=sem, vmem_limit_bytes=VMEM_LIMIT_BYTES)


def _sigmoid(x):
    return 1.0 / (1.0 + jnp.exp(-x))


def _dot(a, b):
    return jnp.dot(a, b, preferred_element_type=F32)


def _pack_bf16_pairs(x):
    n = x.shape[1] // 2
    lo = lax.bitcast_convert_type(x[:, :n], jnp.uint32) >> 16
    hi = lax.bitcast_convert_type(x[:, n:], jnp.uint32) & jnp.uint32(0xFFFF0000)
    return lo | hi


def _unpack_bf16_pairs(w):
    lo = lax.bitcast_convert_type(w << 16, F32)
    hi = lax.bitcast_convert_type(w & jnp.uint32(0xFFFF0000), F32)
    return lo, hi


def _dot_nt(a, b):
    return lax.dot_general(a, b, (((1,), (1,)), ((), ())), preferred_element_type=F32)


def _inproj_kernel(x_ref, gmix_ref, wa_ref, wg_ref, e64_ref, gq_ref, gk_ref, gcq_ref, cos_ref, sin_ref,
                   u_ref, q_ref, kv_ref, xq_ref, g_ref, qh_ref):
    xf = x_ref[...]
    ms = jnp.mean(xf * xf, axis=-1, keepdims=True)
    h = (xf * lax.rsqrt(ms + EPS) * gmix_ref[...]).astype(BF16)
    a = _dot(h, wa_ref[...])
    w = u_ref.shape[1]
    u_ref[...] = a[:, :w]

    cos = jnp.tile(cos_ref[...], (1, w // LANES))
    sin = jnp.tile(sin_ref[...], (1, w // LANES))
    lane = lax.broadcasted_iota(jnp.int32, (xf.shape[0], w), 1)
    first_half = (lane % MOBA_HEAD_DIM) < (MOBA_HEAD_DIM // 2)

    def qk_norm_rope(raw, g):
        ss = _dot((raw * raw).astype(BF16), e64_ref[...])
        n = raw * lax.rsqrt(ss * (1.0 / MOBA_HEAD_DIM) + EPS) * g
        rot = jnp.where(first_half,
                        pltpu.roll(n, w - MOBA_HEAD_DIM // 2, 1),
                        pltpu.roll(n, MOBA_HEAD_DIM // 2, 1))
        return n * cos + rot * sin

    q = qk_norm_rope(a[:, w:2 * w], gq_ref[...]) * (MOBA_HEAD_DIM ** -0.5)
    q_ref[...] = q.astype(BF16)
    in_a = lax.broadcasted_iota(jnp.int32, (xf.shape[0], LANES), 1) < MOBA_HEAD_DIM
    for p in range(w // LANES):
        pair = q[:, p * LANES:(p + 1) * LANES].astype(BF16).astype(F32)
        qh_ref[2 * p] = jnp.where(in_a, pair, 0.0)
        qh_ref[2 * p + 1] = jnp.where(in_a, 0.0, pair)
    k = qk_norm_rope(a[:, 2 * w:3 * w], gk_ref[...]).astype(BF16)
    for p in range(w // LANES):
        kv_ref[:, (2 * p) * LANES:(2 * p + 1) * LANES] = k[:, p * LANES:(p + 1) * LANES]
        kv_ref[:, (2 * p + 1) * LANES:(2 * p + 2) * LANES] = a[:, 3 * w + p * LANES:3 * w + (p + 1) * LANES].astype(BF16)

    xq = a[:, 4 * w:5 * w]
    hd = w // X_HEADS
    for c in range(X_HEADS):
        chunk = xq[:, c * hd:(c + 1) * hd]
        cms = jnp.mean(chunk * chunk, axis=-1, keepdims=True)
        xq_ref[:, c * hd:(c + 1) * hd] = (chunk * lax.rsqrt(cms + EPS) * gcq_ref[...]).astype(BF16)

    d = xf.shape[1]
    for c in range(N_BRANCH):
        z = _dot(h, wg_ref[:, c * d:(c + 1) * d])
        g_ref[:, c * d:(c + 1) * d] = _sigmoid(z).astype(BF16)


def _in_proj(xt, g_mix, w_in, g_q, g_k, g_cq, seq_len, tm=512):
    t, d = xt.shape
    tm = min(tm, seq_len)
    w = d // 2
    wa = w_in[:, :5 * w].astype(BF16)
    wg = w_in[:, 5 * w:].astype(BF16)
    heads = w // MOBA_HEAD_DIM
    e64 = jnp.kron(jnp.eye(heads, dtype=F32), jnp.ones((MOBA_HEAD_DIM, MOBA_HEAD_DIM), F32)).astype(BF16)
    half = MOBA_HEAD_DIM // 2
    inv = ROPE_THETA ** (-jnp.arange(half, dtype=F32) / half)
    ang = jnp.arange(seq_len, dtype=F32)[:, None] * inv[None, :]
    cos = jnp.tile(jnp.cos(ang), (1, LANES // half))
    sin = jnp.tile(jnp.concatenate([-jnp.sin(ang), jnp.sin(ang)], axis=1), (1, LANES // MOBA_HEAD_DIM))
    nt = seq_len // tm
    row = lambda i: (i, 0)
    const = lambda i: (0, 0)
    out_w = jax.ShapeDtypeStruct((t, w), BF16)
    return pl.pallas_call(
        _inproj_kernel,
        grid=(t // tm,),
        in_specs=[
            pl.BlockSpec((tm, d), row),
            pl.BlockSpec((1, d), const),
            pl.BlockSpec((d, 5 * w), const, pipeline_mode=pl.Buffered(1)),
            pl.BlockSpec((d, N_BRANCH * d), const, pipeline_mode=pl.Buffered(1)),
            pl.BlockSpec((w, w), const, pipeline_mode=pl.Buffered(1)),
            pl.BlockSpec((1, w), const),
            pl.BlockSpec((1, w), const),
            pl.BlockSpec((1, w // X_HEADS), const),
            pl.BlockSpec((tm, LANES), lambda i: (i % nt, 0)),
            pl.BlockSpec((tm, LANES), lambda i: (i % nt, 0)),
        ],
        out_specs=[pl.BlockSpec((tm, w), row), pl.BlockSpec((tm, w), row), pl.BlockSpec((tm, 2 * w), row),
                   pl.BlockSpec((tm, w), row), pl.BlockSpec((tm, N_BRANCH * d), row),
                   pl.BlockSpec((heads, tm, LANES), lambda i: (0, i, 0))],
        out_shape=[jax.ShapeDtypeStruct((t, w), F32), out_w, jax.ShapeDtypeStruct((t, 2 * w), BF16), out_w,
                   jax.ShapeDtypeStruct((t, N_BRANCH * d), BF16),
                   jax.ShapeDtypeStruct((heads, t, LANES), F32)],
        compiler_params=_params("parallel"),
        name="in_proj",
    )(xt, g_mix.reshape(1, d), wa, wg, e64,
      jnp.tile(g_q, heads).reshape(1, w), jnp.tile(g_k, heads).reshape(1, w), g_cq.reshape(1, -1), cos, sin)


def _s5_matrices(lam_re, lam_im, log_dt, b_re, b_im, c_re, c_im, n_chunks):
    hp = lax.Precision.HIGHEST
    c = S5_CHUNK
    dt = jnp.exp(log_dt)[:, None]
    mag = jnp.exp(lam_re * dt)
    ar = mag * jnp.cos(lam_im * dt)
    ai = mag * jnp.sin(lam_im * dt)
    nr = ar - 1.0
    den = lam_re * lam_re + lam_im * lam_im
    cr = (nr * lam_re + ai * lam_im) / den
    ci = (ai * lam_re - nr * lam_im) / den
    bbr = cr[..., None] * b_re - ci[..., None] * b_im
    bbi = cr[..., None] * b_im + ci[..., None] * b_re

    def power(n):
        nf = n.astype(F32)[None, :, None]
        m = jnp.exp((lam_re * dt)[:, None, :] * nf)
        th = (lam_im * dt)[:, None, :] * nf
        return m * jnp.cos(th), m * jnp.sin(th)

    pr, pi = power(jnp.arange(c + 1))
    kbr = pr[..., None] * bbr[:, None] - pi[..., None] * bbi[:, None]
    kbi = pr[..., None] * bbi[:, None] + pi[..., None] * bbr[:, None]
    kk = (jnp.einsum('ghp,gtpc->gthc', c_re, kbr, precision=hp)
          - jnp.einsum('ghp,gtpc->gthc', c_im, kbi, precision=hp))
    tq = jnp.arange(c)
    g = kk.shape[0]
    rev = c - 1 - tq
    w_in = jnp.concatenate([kbr[:, rev].transpose(0, 1, 3, 2), kbi[:, rev].transpose(0, 1, 3, 2)], axis=-1)
    w_in = w_in.reshape(g, c * SSM_GROUP, 2 * SSM_STATE)
    prn, pin = pr[:, 1:], pi[:, 1:]
    wo_r = c_re[:, None] * prn[:, :, None, :] - c_im[:, None] * pin[:, :, None, :]
    wo_i = -c_re[:, None] * pin[:, :, None, :] - c_im[:, None] * prn[:, :, None, :]
    w_out = jnp.concatenate([wo_r, wo_i], axis=-1).transpose(0, 3, 1, 2).reshape(g, 2 * SSM_STATE, c * SSM_GROUP)
    n_steps = max(1, int(math.ceil(math.log2(n_chunks))))
    qr, qi = power(c * (2 ** jnp.arange(n_steps)))
    pa = jnp.concatenate([qr, qr], axis=-1)
    pb = jnp.concatenate([-qi, qi], axis=-1)
    return kk.astype(BF16), w_in.astype(BF16), w_out.astype(BF16), pa, pb


S5_SET = LANES // SSM_GROUP
S5_ROWS = 256


def _s5_kernel(u_ref, toep_ref, win_ref, wout_ref, pa_ref, pb_ref, y_ref, s_ref):
    c = S5_CHUNK
    nc = u_ref.shape[0] // c
    rb = min(S5_ROWS, nc)

    def chunk_rows(r0):
        return jnp.concatenate([u_ref[pl.ds(r0 * c + t, rb, stride=c), :] for t in range(c)], axis=1).astype(BF16)

    for blk in range(nc // rb):
        s_ref[blk * rb:(blk + 1) * rb, :] = _dot(chunk_rows(blk * rb), win_ref[0])
    row = lax.broadcasted_iota(jnp.int32, (nc, LANES), 0)
    n_steps = pa_ref.shape[1]
    for g in range(S5_SET):
        lanes = slice(g * LANES, (g + 1) * LANES)
        s = s_ref[:, lanes]
        for k in range(n_steps):
            sh = 1 << k
            if sh >= nc:
                break
            prev = jnp.where(row >= sh, pltpu.roll(s, sh, 0), 0.0)
            s = s + pa_ref[0, k:k + 1, lanes] * prev + pb_ref[0, k:k + 1, lanes] * pltpu.roll(prev, SSM_STATE, 1)
        s_ref[:, lanes] = jnp.where(row >= 1, pltpu.roll(s, 1, 0), 0.0)
    for blk in range(nc // rb):
        rows = slice(blk * rb, (blk + 1) * rb)
        y = _dot(chunk_rows(blk * rb), toep_ref[0]) + _dot(s_ref[rows, :].astype(BF16), wout_ref[0])
        for t in range(c):
            y_ref[pl.ds(blk * rb * c + t, rb, stride=c), :] = y[:, t * LANES:(t + 1) * LANES]


def _s5_block_diag(mats):
    kk, w_in, w_out, pa, pb = mats
    g, c, h, p2 = kk.shape[0], S5_CHUNK, SSM_GROUP, 2 * SSM_STATE
    ns = g // S5_SET
    eye = jnp.eye(S5_SET, dtype=kk.dtype)
    lag_bd = jnp.einsum('sglhc,gk->slgckh', kk.reshape(ns, S5_SET, c + 1, h, h), eye).reshape(ns, c + 1, LANES, LANES)
    tq = jnp.arange(c)
    lag = tq[None, :] - tq[:, None]
    toep_bd = jnp.where((lag >= 0)[None, :, :, None, None], lag_bd[:, jnp.clip(lag, 0, c)], 0)
    toep_bd = toep_bd.transpose(0, 1, 3, 2, 4).reshape(ns, c * LANES, c * LANES)

    def rows_to_states(w):
        wc = w.reshape(ns, S5_SET, c, h, p2).transpose(0, 2, 1, 3, 4).reshape(ns, c, LANES, p2)
        same = (jnp.arange(LANES)[:, None] // h) == (jnp.arange(S5_SET * p2)[None, :] // p2)
        return jnp.where(same, jnp.tile(wc, (1, 1, 1, S5_SET)), 0).reshape(ns, c * LANES, S5_SET * p2)

    win_bd = rows_to_states(w_in)
    wout_bd = rows_to_states(w_out.transpose(0, 2, 1)).transpose(0, 2, 1)
    k = pa.shape[1]
    lanes_of = lambda x: x.reshape(ns, S5_SET, k, p2).transpose(0, 2, 1, 3).reshape(ns, k, S5_SET * p2)
    return toep_bd, win_bd, wout_bd, lanes_of(pa), lanes_of(pb)


def _s5(u, mats, bsz, seq_len):
    toep, w_in, w_out, pa, pb = _s5_block_diag(mats)
    ns = toep.shape[0]
    nc = seq_len // S5_CHUNK
    per_set = lambda s, b: (s, 0, 0)
    once = dict(pipeline_mode=pl.Buffered(1))
    io_spec = pl.BlockSpec((seq_len, LANES), lambda s, b: (b, s), **once)
    return pl.pallas_call(
        _s5_kernel,
        grid=(ns, bsz),
        in_specs=[
            io_spec,
            pl.BlockSpec((1,) + toep.shape[1:], per_set, **once),
            pl.BlockSpec((1,) + w_in.shape[1:], per_set, **once),
            pl.BlockSpec((1,) + w_out.shape[1:], per_set, **once),
            pl.BlockSpec((1,) + pa.shape[1:], per_set),
            pl.BlockSpec((1,) + pb.shape[1:], per_set),
        ],
        out_specs=io_spec,
        out_shape=jax.ShapeDtypeStruct(u.shape, F32),
        scratch_shapes=[pltpu.VMEM((nc, S5_SET * 2 * SSM_STATE), F32)],
        compiler_params=_params("arbitrary", "arbitrary"),
        name="s5_scan",
    )(u, toep, w_in, w_out, pa, pb)


def _moba_select_kernel(q_ref, k_ref, tri_ref, sel_ref, cnt_out_ref, km_ref, cnt_ref):
    h = pl.program_id(1)
    t = pl.program_id(2)
    ts = q_ref.shape[0]
    nbk = MOBA_MAX_BLOCKS

    @pl.when(t == 0)
    def _():
        kk = k_ref[...].astype(F32)
        nb = kk.shape[0] // MOBA_BLOCK
        km = jnp.sum(kk.reshape(nb, MOBA_BLOCK, LANES), axis=1) * (1.0 / MOBA_BLOCK)
        if nb < nbk:
            km = jnp.concatenate([km, jnp.zeros((nbk - nb, LANES), F32)], axis=0)
        lane = lax.broadcasted_iota(jnp.int32, (nbk, LANES), 1)
        km_ref[...] = jnp.where((lane // MOBA_HEAD_DIM) == (h % 2), km, 0.0).astype(BF16)
        cnt_ref[...] = jnp.zeros(cnt_ref.shape, F32)

    gate = _dot_nt(km_ref[...], q_ref[...])
    blk = lax.broadcasted_iota(jnp.int32, gate.shape, 0)
    qblk = (t * ts + lax.broadcasted_iota(jnp.int32, gate.shape, 1)) // MOBA_BLOCK
    g = jnp.where(blk < qblk, gate, -jnp.inf)
    selected = jnp.zeros(gate.shape, jnp.bool_)
    picks = []
    for _ in range(MOBA_TOPK):
        mx = jnp.max(g, axis=0, keepdims=True)
        idx = jnp.min(jnp.where(g == mx, blk, nbk), axis=0, keepdims=True)
        hit = blk == idx
        ok = (idx[0:1] < qblk[0:1]) & (mx > -jnp.inf)
        picks.append((hit, idx, ok))
        selected = selected | (hit & ok)
        g = jnp.where(hit, -jnp.inf, g)
    sel01 = jnp.where(selected, 1.0, 0.0).astype(BF16)
    prefix = _dot(sel01, tri_ref[...]) + jnp.tile(cnt_ref[...], (1, ts // LANES))
    rows = [jnp.where(ok, idx, -1) for (_, idx, ok) in picks]
    rows += [jnp.sum(jnp.where(hit, prefix, 0.0), axis=0, keepdims=True).astype(jnp.int32) for (hit, _, _) in picks]
    rows += [jnp.zeros((1, ts), jnp.int32)] * (8 - 2 * MOBA_TOPK)
    sel_ref[0, 0] = jnp.concatenate(rows, axis=0)
    cnt_ref[...] = cnt_ref[...] + _dot(sel01, jnp.ones((ts, LANES), BF16))
    cnt_out_ref[0, 0] = cnt_ref[...]


def _moba_select(q2, kv, bsz, seq_len, ts=1024):
    ts = min(ts, seq_len)
    heads = q2.shape[1] // MOBA_HEAD_DIM
    nt = seq_len // ts
    tri = (jnp.arange(ts)[:, None] < jnp.arange(ts)[None, :]).astype(BF16)
    return pl.pallas_call(
        _moba_select_kernel,
        grid=(bsz, heads, nt),
        in_specs=[
            pl.BlockSpec((ts, LANES), lambda b, h, t: (b * nt + t, h // 2)),
            pl.BlockSpec((seq_len, LANES), lambda b, h, t: (b, 2 * (h // 2))),
            pl.BlockSpec((ts, ts), lambda b, h, t: (0, 0)),
        ],
        out_specs=[pl.BlockSpec((1, 1, 8, ts), lambda b, h, t: (b, h, 0, t)),
                   pl.BlockSpec((1, 1, MOBA_MAX_BLOCKS, LANES), lambda b, h, t: (b, h, 0, 0))],
        out_shape=[jax.ShapeDtypeStruct((bsz, heads, 8, seq_len), jnp.int32),
                   jax.ShapeDtypeStruct((bsz, heads, MOBA_MAX_BLOCKS, LANES), F32)],
        scratch_shapes=[pltpu.VMEM((MOBA_MAX_BLOCKS, LANES), BF16), pltpu.VMEM((MOBA_MAX_BLOCKS, LANES), F32)],
        compiler_params=_params("parallel", "arbitrary", "arbitrary"),
        name="moba_select",
    )(q2, kv, tri)


MOBA_DIAG_BLOCKS = 4


def _moba_diag_kernel(q_ref, kv_ref, o_ref, lse_ref):
    lane = lax.broadcasted_iota(jnp.int32, (MOBA_BLOCK, LANES), 1)
    is_a = lane < MOBA_HEAD_DIM
    r = lax.broadcasted_iota(jnp.int32, (MOBA_BLOCK, MOBA_BLOCK), 0)
    c = lax.broadcasted_iota(jnp.int32, (MOBA_BLOCK, MOBA_BLOCK), 1)
    for u in range(MOBA_DIAG_BLOCKS):
        rows = pl.ds(u * MOBA_BLOCK, MOBA_BLOCK)
        q = q_ref[rows, :]
        kb = kv_ref[rows, :LANES]
        vb = kv_ref[rows, LANES:]
        pvs, ms = [], []
        for own in (is_a, jnp.logical_not(is_a)):
            s = jnp.where(c <= r, _dot_nt(jnp.where(own, q, jnp.zeros_like(q)), kb), NEG_BIG)
            m = jnp.max(s, axis=1, keepdims=True)
            p = jnp.exp(s - m).astype(BF16)
            pvs.append(_dot(p, jnp.where(own, vb, 1.0)))
            ms.append(m)
        num = jnp.where(is_a, pvs[0], pvs[1])
        den = pltpu.roll(jnp.where(is_a, pvs[1], pvs[0]), MOBA_HEAD_DIM, 1)
        o_ref[rows, :] = num / den
        lse_ref[rows, :] = jnp.where(is_a, ms[0], ms[1]) + jnp.log(den)


def _moba_diag(q2, kv, bsz, seq_len):
    npair = q2.shape[1] // LANES
    rows = MOBA_BLOCK * MOBA_DIAG_BLOCKS
    nb = seq_len // rows
    spec = pl.BlockSpec((rows, LANES), lambda b, p, i: (b * nb + i, p))
    out = jax.ShapeDtypeStruct(q2.shape, F32)
    return pl.pallas_call(
        _moba_diag_kernel,
        grid=(bsz, npair, nb),
        in_specs=[spec, pl.BlockSpec((rows, 2 * LANES), lambda b, p, i: (b * nb + i, p))],
        out_specs=[spec, spec],
        out_shape=[out, out],
        compiler_params=_params("parallel", "parallel", "parallel"),
        name="moba_diag",
    )(q2, kv)


MOBA_TILE = 256
MOBA_TILES_PER_STEP = 8


def _moba_grouped_kernel(tile_row_ref, tile_pair_ref, tile_half_ref, tile_real_ref, qd_ref, *refs):
    n = MOBA_TILES_PER_STEP
    kv_refs, o_ref = refs[:n], refs[n]
    i = pl.program_id(0)
    lane = lax.broadcasted_iota(jnp.int32, (MOBA_TILE, LANES), 1)
    kv_lane = lax.broadcasted_iota(jnp.int32, (MOBA_BLOCK, LANES), 1)
    for u in range(n):
        tile = i * n + u
        rows = pl.ds(u * MOBA_TILE, MOBA_TILE)
        own = (lane // MOBA_HEAD_DIM) == tile_half_ref[tile]
        s = _dot_nt(qd_ref[rows, :].astype(BF16), kv_refs[u][:, :LANES])
        m = jnp.max(s, axis=1, keepdims=True)
        p = jnp.exp(s - m).astype(BF16)
        vb = jnp.where((kv_lane // MOBA_HEAD_DIM) == tile_half_ref[tile], kv_refs[u][:, LANES:], 1.0)
        pv = _dot(p, vb)
        row_sum = pltpu.roll(pv, MOBA_HEAD_DIM, 1)
        part = jnp.where(own, pv / row_sum, m + jnp.log(pv))
        o_ref[rows, :] = jnp.where(tile_real_ref[tile] > 0, part, NEG_BIG)


def _moba_grouped(qd, kv, tile_row, tile_pair, tile_half, tile_real):
    n = MOBA_TILES_PER_STEP
    n_tiles = qd.shape[0] // MOBA_TILE

    def kv_spec(u):
        return pl.BlockSpec((MOBA_BLOCK, 2 * LANES), lambda i, tr, tp, th, tl: (tr[i * n + u], tp[i * n + u]))

    grid_spec = pltpu.PrefetchScalarGridSpec(
        num_scalar_prefetch=4,
        grid=(n_tiles // n,),
        in_specs=[pl.BlockSpec((n * MOBA_TILE, LANES), lambda i, tr, tp, th, tl: (i, 0))]
        + [kv_spec(u) for u in range(n)],
        out_specs=pl.BlockSpec((n * MOBA_TILE, LANES), lambda i, tr, tp, th, tl: (i, 0)),
    )
    return pl.pallas_call(
        _moba_grouped_kernel,
        grid_spec=grid_spec,
        out_shape=jax.ShapeDtypeStruct(qd.shape, F32),
        compiler_params=_params("arbitrary"),
        name="moba_grouped",
    )(tile_row, tile_pair, tile_half, tile_real, qd, *([kv] * n))


def _moba_combine_kernel(od_ref, lsed_ref, *refs):
    g_refs, o_ref = refs[:-1], refs[-1]
    lane = lax.broadcasted_iota(jnp.int32, od_ref.shape, 1)
    is_a = lane < MOBA_HEAD_DIM
    parts = [(od_ref[...], lsed_ref[...])]
    for s in range(MOBA_TOPK):
        xa = g_refs[s][0, 0, 0]
        xb = g_refs[MOBA_TOPK + s][0, 0, 0]
        o = jnp.where(is_a, xa, xb)
        lse = pltpu.roll(jnp.where(is_a, xb, xa), MOBA_HEAD_DIM, 1)
        parts.append((o, lse))
    m = parts[0][1]
    for _, lse in parts[1:]:
        m = jnp.maximum(m, lse)
    num = jnp.zeros(od_ref.shape, F32)
    den = jnp.zeros(od_ref.shape, F32)
    for o, lse in parts:
        w = jnp.exp(lse - m)
        num = num + w * o
        den = den + w
    o_ref[...] = (num / den).astype(BF16)


def _moba_combine(od, lsed, gath, bsz, seq_len, tm=1024):
    tm = min(tm, seq_len)
    npair = od.shape[1] // LANES
    nt = seq_len // tm
    spec = pl.BlockSpec((tm, LANES), lambda b, p, i: (b * nt + i, p))

    def g_spec(e, s):
        return pl.BlockSpec((1, 1, 1, tm, LANES), lambda b, p, i: (b, 2 * p + e, s, i, 0))

    return pl.pallas_call(
        _moba_combine_kernel,
        grid=(bsz, npair, nt),
        in_specs=[spec, spec] + [g_spec(e, s) for e in range(2) for s in range(MOBA_TOPK)],
        out_specs=spec,
        out_shape=jax.ShapeDtypeStruct(od.shape, BF16),
        compiler_params=_params("parallel", "parallel", "parallel"),
        name="moba_combine",
    )(od, lsed, *([gath] * (2 * MOBA_TOPK)))


SC_WINDOW = 128
SC_CORES = 2
SC_SUBCORES = 16


def _sc_mesh():
    return plsc.VectorSubcoreMesh(core_axis_name="core", subcore_axis_name="subcore")


def _sc_gather_rows(table, idx):
    n = idx.shape[0]
    d = table.shape[1]
    window = SC_WINDOW
    assert n % (window * SC_CORES * SC_SUBCORES) == 0
    per_core = n // window // SC_CORES

    @functools.partial(pl.kernel, out_type=jax.ShapeDtypeStruct((n, d), table.dtype), mesh=_sc_mesh())
    def gather_kernel(x_hbm, i_hbm, o_hbm):
        base = lax.axis_index("core") * per_core

        def body(i_vmem, o_vmem):
            pltpu.sync_copy(x_hbm.at[i_vmem.at[0]], o_vmem)

        pltpu.emit_pipeline(
            body,
            grid=(per_core,),
            in_specs=[pl.BlockSpec((1, window), index_map=lambda i: (0, base + i))],
            out_specs=[pl.BlockSpec((window, d), index_map=lambda i: (base + i, 0))],
            core_axis_name="subcore",
            dimension_semantics=(pltpu.PARALLEL,),
            trace_scopes=False,
        )(i_hbm, o_hbm)

    return gather_kernel(table, idx.reshape(1, n))


def _sc_scatter_rows(rows, dest, n_out, repeat):
    n_src, d = rows.shape
    n = dest.shape[0]
    assert n == repeat * n_src
    window = SC_WINDOW
    assert n % (window * SC_CORES * SC_SUBCORES) == 0
    per_core = n // window // SC_CORES
    src_windows = n_src // window

    @functools.partial(pl.kernel, out_type=jax.ShapeDtypeStruct((n_out, d), rows.dtype), mesh=_sc_mesh())
    def scatter_kernel(x_hbm, i_hbm, o_hbm):
        base = lax.axis_index("core") * per_core

        def body(x_vmem, i_vmem):
            pltpu.sync_copy(x_vmem, o_hbm.at[i_vmem.at[0]])

        pltpu.emit_pipeline(
            body,
            grid=(per_core,),
            in_specs=[pl.BlockSpec((window, d), index_map=lambda i: ((base + i) % src_windows, 0)),
                      pl.BlockSpec((1, window), index_map=lambda i: (0, base + i))],
            out_specs=[],
            core_axis_name="subcore",
            dimension_semantics=(pltpu.PARALLEL,),
            trace_scopes=False,
        )(x_hbm, i_hbm)

    return scatter_kernel(rows, dest.reshape(1, n))


def _moba(q2, qh, kv, bsz, seq_len):
    heads = q2.shape[1] // MOBA_HEAD_DIM
    nbk = MOBA_MAX_BLOCKS
    sel, cnt = _moba_select(q2, kv, bsz, seq_len)
    od, lsed = _moba_diag(q2, kv, bsz, seq_len)
    counts = cnt[..., 0].astype(jnp.int32).reshape(bsz * heads * nbk)
    pcounts = ((counts + MOBA_TILE - 1) // MOBA_TILE) * MOBA_TILE
    pends = jnp.cumsum(pcounts)
    pstarts = (pends - pcounts).reshape(bsz, heads, 1, 1, nbk)
    n_items = bsz * heads * seq_len * MOBA_TOPK
    step_rows = MOBA_TILE * MOBA_TILES_PER_STEP
    n_rows = -(-(n_items + bsz * heads * nbk * MOBA_TILE) // step_rows) * step_rows
    n_null = step_rows
    idx = sel[:, :, 0:MOBA_TOPK, :]
    rank = sel[:, :, MOBA_TOPK:2 * MOBA_TOPK, :]
    start = jnp.sum(jnp.where(idx[..., None] == jnp.arange(nbk), pstarts, 0), axis=-1)
    null_row = n_rows + jnp.arange(seq_len, dtype=jnp.int32) % n_null
    dest = jnp.where(idx >= 0, start + rank, null_row)
    n_tiles = (n_rows + n_null) // MOBA_TILE
    n_groups = bsz * heads * nbk
    tile_start = jnp.arange(n_tiles, dtype=jnp.int32) * MOBA_TILE
    tile_g = jnp.minimum(jnp.sum((pends[None, :] <= tile_start[:, None]).astype(jnp.int32), axis=1), n_groups - 1)
    tile_real = (tile_start < pends[-1]).astype(jnp.int32)
    tile_head = (tile_g // nbk) % heads
    tile_row = (tile_g // (heads * nbk)) * (seq_len // MOBA_BLOCK) + jnp.minimum(tile_g % nbk, seq_len // MOBA_BLOCK - 1)
    dest_by_slot = dest.transpose(2, 1, 0, 3).reshape(-1)
    qd = _sc_scatter_rows(qh.reshape(-1, LANES), dest_by_slot, n_rows + n_null, MOBA_TOPK)
    part = _moba_grouped(qd, kv, tile_row, tile_head // 2, tile_head % 2, tile_real)
    gath = _sc_gather_rows(part, dest.reshape(-1)).reshape(bsz, heads, MOBA_TOPK, seq_len, LANES)
    return _moba_combine(od, lsed, gath, bsz, seq_len)


def _mem_kv_kernel(mem_ref, gmem_ref, w_ref, gck_ref, k_ref, v_ref):
    xf = mem_ref[0]
    ms = jnp.mean(xf * xf, axis=-1, keepdims=True)
    h = (xf * lax.rsqrt(ms + EPS) * gmem_ref[...]).astype(BF16)
    kv = _dot(h, w_ref[...])
    w = k_ref.shape[2]
    hd = w // X_HEADS
    for c in range(X_HEADS):
        chunk = kv[:, c * hd:(c + 1) * hd]
        cms = jnp.mean(chunk * chunk, axis=-1, keepdims=True)
        k_ref[0, :, c * hd:(c + 1) * hd] = (chunk * lax.rsqrt(cms + EPS) * gck_ref[...]).astype(BF16)
    v_ref[0] = kv[:, w:].astype(BF16)


def _mem_kv(mem, g_mem, w_kv_mem, g_ck):
    bsz, m, d = mem.shape
    w = w_kv_mem.shape[1] // 2
    const = lambda b: (0, 0)
    out = jax.ShapeDtypeStruct((bsz, m, w), BF16)
    return pl.pallas_call(
        _mem_kv_kernel,
        grid=(bsz,),
        in_specs=[pl.BlockSpec((1, m, d), lambda b: (b, 0, 0)), pl.BlockSpec((1, d), const),
                  pl.BlockSpec((d, 2 * w), const), pl.BlockSpec((1, w // X_HEADS), const)],
        out_specs=[pl.BlockSpec((1, m, w), lambda b: (b, 0, 0))] * 2,
        out_shape=[out, out],
        compiler_params=_params("parallel"),
        name="mem_kv",
    )(mem, g_mem.reshape(1, d), w_kv_mem.astype(BF16), g_ck.reshape(1, -1))


def _merge_kernel(x_ref, ys_ref, u_ref, dskip_ref, om_ref, xq_ref, kc_ref, vc_ref, g_ref,
                  wglu_ref, wmo_ref, wco_ref, wout_ref, gffn_ref, wr_ref, br_ref,
                  x1_ref, h2_ref, logit_ref):
    d = x_ref.shape[1]
    y = ys_ref[...].astype(F32) + dskip_ref[...] * u_ref[...].astype(F32)
    ge = 0.5 * y * (1.0 + jnp.tanh(math.sqrt(2.0 / math.pi) * (y + 0.044715 * (y * y * y))))
    z = _dot(ge.astype(BF16), wglu_ref[...])
    merged = g_ref[:, 0:d].astype(F32) * (z[:, :d] * _sigmoid(z[:, d:]))
    merged = merged + g_ref[:, d:2 * d].astype(F32) * _dot(om_ref[...], wmo_ref[...])
    w = xq_ref.shape[1]
    hd = w // X_HEADS
    heads = []
    for c in range(X_HEADS):
        s = _dot_nt(xq_ref[:, c * hd:(c + 1) * hd], kc_ref[0, :, c * hd:(c + 1) * hd]) * (hd ** -0.5)
        p = jnp.exp(s - jnp.max(s, axis=1, keepdims=True))
        p = p / jnp.sum(p, axis=1, keepdims=True)
        heads.append(_dot(p.astype(BF16), vc_ref[0, :, c * hd:(c + 1) * hd]))
    oc = jnp.concatenate(heads, axis=1).astype(BF16)
    merged = merged + g_ref[:, 2 * d:3 * d].astype(F32) * _dot(oc, wco_ref[...])
    x1 = x_ref[...] + _dot(merged.astype(BF16), wout_ref[...])
    x1_ref[...] = x1
    ms = jnp.mean(x1 * x1, axis=-1, keepdims=True)
    h2 = (x1 * lax.rsqrt(ms + EPS) * gffn_ref[...]).astype(BF16)
    packed = _pack_bf16_pairs(h2.astype(F32))
    for j in range(h2_ref.shape[0]):
        h2_ref[j] = packed[:, j * LANES:(j + 1) * LANES]
    logit_ref[...] = _dot_nt(wr_ref[...], h2) + br_ref[...]


def _merge(xt, ys, u, d_skip, om, xqn, kc, vc, gates, w_glu, w_mo, w_co, w_out, g_ffn, w_router, b_router,
           seq_len, tm=512):
    t, d = xt.shape
    tm = min(tm, seq_len)
    w = d // 2
    m = kc.shape[1]
    ne = w_router.shape[1]
    nt = seq_len // tm
    row = lambda i: (i, 0)
    const = lambda i: (0, 0)
    per_b = lambda i: (i // nt, 0, 0)
    once = dict(pipeline_mode=pl.Buffered(1))
    return pl.pallas_call(
        _merge_kernel,
        grid=(t // tm,),
        in_specs=[
            pl.BlockSpec((tm, d), row), pl.BlockSpec((tm, w), row), pl.BlockSpec((tm, w), row),
            pl.BlockSpec((1, w), const), pl.BlockSpec((tm, w), row), pl.BlockSpec((tm, w), row),
            pl.BlockSpec((1, m, w), per_b), pl.BlockSpec((1, m, w), per_b),
            pl.BlockSpec((tm, N_BRANCH * d), row),
            pl.BlockSpec((w, 2 * d), const, **once), pl.BlockSpec((w, d), const, **once),
            pl.BlockSpec((w, d), const, **once), pl.BlockSpec((d, d), const, **once), pl.BlockSpec((1, d), const),
            pl.BlockSpec((ne, d), const), pl.BlockSpec((ne, 1), const),
        ],
        out_specs=[pl.BlockSpec((tm, d), row), pl.BlockSpec((d // 2 // LANES, tm, LANES), lambda i: (0, i, 0)),
                   pl.BlockSpec((ne, tm), lambda i: (0, i))],
        out_shape=[jax.ShapeDtypeStruct((t, d), F32), jax.ShapeDtypeStruct((d // 2 // LANES, t, LANES), jnp.uint32),
                   jax.ShapeDtypeStruct((ne, t), F32)],
        compiler_params=_params("parallel"),
        name="merge",
    )(xt, ys, u, d_skip.reshape(1, w), om, xqn, kc, vc, gates,
      w_glu.astype(BF16), w_mo.astype(BF16), w_co.astype(BF16), w_out.astype(BF16),
      g_ffn.reshape(1, d), w_router.T.astype(BF16), b_router.reshape(ne, 1))


def _moe_kernel(blk_e_ref, blk_used_ref, xs_ref, wgu_ref, bgu_ref, wd_ref, bd_ref, y_ref, wgu_bf, wd_bf):
    i = pl.program_id(0)
    prev = blk_e_ref[jnp.maximum(i - 1, 0)]

    @pl.when((i == 0) | (blk_e_ref[i] != prev))
    def _():
        wgu_bf[...] = wgu_ref[0].astype(BF16)
        wd_bf[...] = wd_ref[0].astype(BF16)

    @pl.when(blk_used_ref[i] > 0)
    def _():
        de = wd_bf.shape[0]
        words = jnp.concatenate([xs_ref[j] for j in range(xs_ref.shape[0])], axis=1)
        xs = jnp.concatenate(_unpack_bf16_pairs(words), axis=1).astype(BF16)
        gu = _dot(xs, wgu_bf[...]) + bgu_ref[0]
        gate = jnp.minimum(gu[:, :de], SWIGLU_LIMIT)
        up = jnp.clip(gu[:, de:], -SWIGLU_LIMIT, SWIGLU_LIMIT)
        act = gate * _sigmoid(SWIGLU_ALPHA * gate) * (up + 1.0)
        y = _dot(act.astype(BF16), wd_bf[...]) + bd_ref[0]
        packed = _pack_bf16_pairs(y.astype(BF16).astype(F32))
        for j in range(y_ref.shape[0]):
            y_ref[j] = packed[:, j * LANES:(j + 1) * LANES]

    @pl.when(blk_used_ref[i] == 0)
    def _():
        y_ref[...] = jnp.zeros(y_ref.shape, y_ref.dtype)


def _moe_experts(xs, blk_e, blk_used, w_gu, b_gu, w_down, b_down):
    slabs, p, _ = xs.shape
    d = 2 * slabs * LANES
    ne, _, de2 = w_gu.shape
    de = de2 // 2
    nblk = p // EXPERT_ROWS
    row_spec = pl.BlockSpec((slabs, EXPERT_ROWS, LANES), lambda i, e, n: (0, i, 0))
    grid_spec = pltpu.PrefetchScalarGridSpec(
        num_scalar_prefetch=2,
        grid=(nblk,),
        in_specs=[
            row_spec,
            pl.BlockSpec((1, d, de2), lambda i, e, n: (e[i], 0, 0)),
            pl.BlockSpec((1, 1, de2), lambda i, e, n: (e[i], 0, 0)),
            pl.BlockSpec((1, de, d), lambda i, e, n: (e[i], 0, 0)),
            pl.BlockSpec((1, 1, d), lambda i, e, n: (e[i], 0, 0)),
        ],
        out_specs=row_spec,
        scratch_shapes=[pltpu.VMEM((d, de2), BF16), pltpu.VMEM((de, d), BF16)],
    )
    return pl.pallas_call(
        _moe_kernel,
        grid_spec=grid_spec,
        out_shape=jax.ShapeDtypeStruct(xs.shape, jnp.uint32),
        compiler_params=_params("arbitrary"),
        name="moe_experts",
    )(blk_e, blk_used, xs, w_gu, b_gu.reshape(ne, 1, de2), w_down, b_down.reshape(ne, 1, d))


def _router_kernel(lg_ref, tri_ref, e_ref, w_ref, r_ref, cnt_out_ref, cnt_ref):
    t = pl.program_id(0)
    ne, ts = lg_ref.shape

    @pl.when(t == 0)
    def _():
        cnt_ref[...] = jnp.zeros(cnt_ref.shape, F32)

    g = lg_ref[...]
    eid = lax.broadcasted_iota(jnp.int32, g.shape, 0)
    selected = jnp.zeros(g.shape, jnp.bool_)
    picks = []
    for _ in range(TOPK_EXPERTS):
        mx = jnp.max(g, axis=0, keepdims=True)
        idx = jnp.min(jnp.where(g == mx, eid, ne), axis=0, keepdims=True)
        hit = eid == idx
        picks.append((hit, idx, mx))
        selected = selected | hit
        g = jnp.where(hit, -jnp.inf, g)
    exps = [jnp.exp(mx - picks[0][2]) for (_, _, mx) in picks]
    total = sum(exps)
    sel01 = jnp.where(selected, 1.0, 0.0).astype(BF16)
    prefix = _dot(sel01, tri_ref[...]) + jnp.tile(cnt_ref[...], (1, ts // LANES))
    pad_i = [jnp.zeros((1, ts), jnp.int32)] * (8 - TOPK_EXPERTS)
    e_ref[...] = jnp.concatenate([idx for (_, idx, _) in picks] + pad_i, axis=0)
    w_ref[...] = jnp.concatenate([e / total for e in exps] + [jnp.zeros((1, ts), F32)] * (8 - TOPK_EXPERTS), axis=0)
    r_ref[...] = jnp.concatenate(
        [jnp.sum(jnp.where(hit, prefix, 0.0), axis=0, keepdims=True).astype(jnp.int32) for (hit, _, _) in picks]
        + pad_i, axis=0)
    cnt_ref[...] = cnt_ref[...] + _dot(sel01, jnp.ones((ts, LANES), BF16))
    cnt_out_ref[...] = cnt_ref[...]


def _router(logits_t, ts=1024):
    ne, t = logits_t.shape
    ts = min(ts, t)
    tri = (jnp.arange(ts)[:, None] < jnp.arange(ts)[None, :]).astype(BF16)
    rows = pl.BlockSpec((8, ts), lambda i: (0, i))
    return pl.pallas_call(
        _router_kernel,
        grid=(t // ts,),
        in_specs=[pl.BlockSpec((ne, ts), lambda i: (0, i)), pl.BlockSpec((ts, ts), lambda i: (0, 0))],
        out_specs=[rows, rows, rows, pl.BlockSpec((ne, LANES), lambda i: (0, 0))],
        out_shape=[jax.ShapeDtypeStruct((8, t), jnp.int32), jax.ShapeDtypeStruct((8, t), F32),
                   jax.ShapeDtypeStruct((8, t), jnp.int32), jax.ShapeDtypeStruct((ne, LANES), F32)],
        scratch_shapes=[pltpu.VMEM((ne, LANES), F32)],
        compiler_params=_params("arbitrary"),
        name="moe_router",
    )(logits_t, tri)


def _moe_mix_kernel(x1_ref, w_ref, pk_ref, o_ref):
    slabs = pk_ref.shape[0]
    wts = w_ref[...]
    lo = [jnp.zeros((x1_ref.shape[0], LANES), F32)] * slabs
    hi = [jnp.zeros((x1_ref.shape[0], LANES), F32)] * slabs
    for k in range(TOPK_EXPERTS):
        wk = wts[:, k:k + 1]
        for j in range(slabs):
            a, b = _unpack_bf16_pairs(pk_ref[j, k])
            lo[j] = lo[j] + wk * a
            hi[j] = hi[j] + wk * b
    o_ref[...] = x1_ref[...] + jnp.concatenate(lo + hi, axis=1)


def _moe_mix(x1, weights_tk, picked, tm=512):
    t, d = x1.shape
    slabs = picked.shape[0]
    return pl.pallas_call(
        _moe_mix_kernel,
        grid=(t // tm,),
        in_specs=[pl.BlockSpec((tm, d), lambda i: (i, 0)),
                  pl.BlockSpec((tm, TOPK_EXPERTS), lambda i: (i, 0)),
                  pl.BlockSpec((slabs, TOPK_EXPERTS, tm, LANES), lambda i: (0, 0, i, 0))],
        out_specs=pl.BlockSpec((tm, d), lambda i: (i, 0)),
        out_shape=jax.ShapeDtypeStruct((t, d), F32),
        compiler_params=_params("parallel"),
        name="moe_mix",
    )(x1, weights_tk, picked)


def _moe_ffn(x1, h2, logits_t, w_gu, b_gu, w_down, b_down):
    t, d = x1.shape
    tk = t * TOPK_EXPERTS
    e8, w8, r8, cnt = _router(logits_t)
    top_e, weights, rank = e8[:TOPK_EXPERTS], w8[:TOPK_EXPERTS], r8[:TOPK_EXPERTS]
    counts = cnt[:, 0].astype(jnp.int32)
    pcounts = ((counts + EXPERT_ROWS - 1) // EXPERT_ROWS) * EXPERT_ROWS
    pends = jnp.cumsum(pcounts)
    pstarts = pends - pcounts
    is_e = top_e[..., None] == jnp.arange(N_EXPERTS, dtype=jnp.int32)
    dest_by_k = jnp.sum(jnp.where(is_e, pstarts, 0), axis=-1) + rank
    nblk = -(-tk // EXPERT_ROWS) + N_EXPERTS
    blk_start = jnp.arange(nblk, dtype=jnp.int32) * EXPERT_ROWS
    blk_e = jnp.minimum(jnp.sum((pends[None, :] <= blk_start[:, None]).astype(jnp.int32), axis=1), N_EXPERTS - 1)
    blk_used = (blk_start < pends[-1]).astype(jnp.int32)
    slabs = d // 2 // LANES
    p = nblk * EXPERT_ROWS
    slab_off = jnp.arange(slabs, dtype=jnp.int32) * p
    dest_kst = (dest_by_k[:, None, :] + slab_off[None, :, None]).reshape(-1)
    dest_skt = (dest_by_k[None, :, :] + slab_off[:, None, None]).reshape(-1)
    xs = _sc_scatter_rows(h2.reshape(slabs * t, LANES), dest_kst, slabs * p, TOPK_EXPERTS)
    ys = _moe_experts(xs.reshape(slabs, p, LANES), blk_e, blk_used, w_gu, b_gu, w_down, b_down)
    picked = _sc_gather_rows(ys.reshape(slabs * p, LANES), dest_skt).reshape(slabs, TOPK_EXPERTS, t, LANES)
    return _moe_mix(x1, weights.T, picked)


def kernel(x, mem, g_mix, w_in, lam_re, lam_im, log_dt, b_re, b_im, c_re, c_im, d_skip, w_glu, g_q, g_k, w_moba_out, g_mem, w_kv_mem, g_cq, g_ck, w_cross_out, w_out, g_ffn, w_router, b_router, w_gu, b_gu, w_down, b_down):
    bsz, seq_len, d = x.shape
    xt = x.reshape(bsz * seq_len, d)
    for l in range(g_mix.shape[0]):
        u, q2, kv, xqn, gates, qh = _in_proj(xt, g_mix[l], w_in[l], g_q[l], g_k[l], g_cq[l], seq_len)
        mats = _s5_matrices(lam_re[l], lam_im[l], log_dt[l], b_re[l], b_im[l], c_re[l], c_im[l],
                            seq_len // S5_CHUNK)
        ys = _s5(u, mats, bsz, seq_len)
        om = _moba(q2, qh, kv, bsz, seq_len)
        kc, vc = _mem_kv(mem, g_mem[l], w_kv_mem[l], g_ck[l])
        x1, h2, logits = _merge(xt, ys, u, d_skip[l], om, xqn, kc, vc, gates, w_glu[l], w_moba_out[l],
                                w_cross_out[l], w_out[l], g_ffn[l], w_router[l], b_router[l], seq_len)
        xt = _moe_ffn(x1, h2, logits, w_gu[l], b_gu[l], w_down[l], b_down[l])
    return xt.reshape(bsz, seq_len, d)
```

```python
import functools
import math

import jax
import jax.numpy as jnp
from jax import lax
from jax.experimental import pallas as pl
from jax.experimental.pallas import tpu as pltpu
from jax.experimental.pallas import tpu_sc as plsc

F32 = jnp.float32
BF16 = jnp.bfloat16

EPS = 1e-6
N_BRANCH = 3
SSM_GROUP = 16
SSM_STATE = 64
S5_CHUNK = 16
MOBA_HEAD_DIM = 64
MOBA_BLOCK = 256
MOBA_TOPK = 3
MOBA_MAX_BLOCKS = 64
ROPE_THETA = 10000.0
X_HEADS = 4
N_EXPERTS = 32
TOPK_EXPERTS = 4
SWIGLU_LIMIT = 7.0
SWIGLU_ALPHA = 1.702
EXPERT_ROWS = 512
NEG_BIG = -1e30
LANES = 128
VMEM_LIMIT_BYTES = 56 * 1024 * 1024


def _params(*sem):
    return pltpu.CompilerParams(dimension_semantics=sem, vmem_limit_bytes=VMEM_LIMIT_BYTES)


def _sigmoid(x):
    return 1.0 / (1.0 + jnp.exp(-x))


def _dot(a, b):
    return jnp.dot(a, b, preferred_element_type=F32)


def _pack_bf16_pairs(x):
    n = x.shape[1] // 2
    lo = lax.bitcast_convert_type(x[:, :n], jnp.uint32) >> 16
    hi = lax.bitcast_convert_type(x[:, n:], jnp.uint32) & jnp.uint32(0xFFFF0000)
    return lo | hi


def _unpack_bf16_pairs(w):
    lo = lax.bitcast_convert_type(w << 16, F32)
    hi = lax.bitcast_convert_type(w & jnp.uint32(0xFFFF0000), F32)
    return lo, hi


def _dot_nt(a, b):
    return lax.dot_general(a, b, (((1,), (1,)), ((), ())), preferred_element_type=F32)


def _inproj_kernel(x_ref, gmix_ref, wa_ref, wg_ref, e64_ref, gq_ref, gk_ref, gcq_ref, cos_ref, sin_ref,
                   u_ref, q_ref, kv_ref, xq_ref, g_ref, qh_ref):
    xf = x_ref[...]
    ms = jnp.mean(xf * xf, axis=-1, keepdims=True)
    h = (xf * lax.rsqrt(ms + EPS) * gmix_ref[...]).astype(BF16)
    a = _dot(h, wa_ref[...])
    w = u_ref.shape[1]
    u_ref[...] = a[:, :w]

    cos = jnp.tile(cos_ref[...], (1, w // LANES))
    sin = jnp.tile(sin_ref[...], (1, w // LANES))
    lane = lax.broadcasted_iota(jnp.int32, (xf.shape[0], w), 1)
    first_half = (lane % MOBA_HEAD_DIM) < (MOBA_HEAD_DIM // 2)

    def qk_norm_rope(raw, g):
        ss = _dot((raw * raw).astype(BF16), e64_ref[...])
        n = raw * lax.rsqrt(ss * (1.0 / MOBA_HEAD_DIM) + EPS) * g
        rot = jnp.where(first_half,
                        pltpu.roll(n, w - MOBA_HEAD_DIM // 2, 1),
                        pltpu.roll(n, MOBA_HEAD_DIM // 2, 1))
        return n * cos + rot * sin

    q = qk_norm_rope(a[:, w:2 * w], gq_ref[...]) * (MOBA_HEAD_DIM ** -0.5)
    q_ref[...] = q.astype(BF16)
    in_a = lax.broadcasted_iota(jnp.int32, (xf.shape[0], LANES), 1) < MOBA_HEAD_DIM
    for p in range(w // LANES):
        pair = q[:, p * LANES:(p + 1) * LANES].astype(BF16).astype(F32)
        qh_ref[2 * p] = jnp.where(in_a, pair, 0.0)
        qh_ref[2 * p + 1] = jnp.where(in_a, 0.0, pair)
    k = qk_norm_rope(a[:, 2 * w:3 * w], gk_ref[...]).astype(BF16)
    for p in range(w // LANES):
        kv_ref[:, (2 * p) * LANES:(2 * p + 1) * LANES] = k[:, p * LANES:(p + 1) * LANES]
        kv_ref[:, (2 * p + 1) * LANES:(2 * p + 2) * LANES] = a[:, 3 * w + p * LANES:3 * w + (p + 1) * LANES].astype(BF16)

    xq = a[:, 4 * w:5 * w]
    hd = w // X_HEADS
    for c in range(X_HEADS):
        chunk = xq[:, c * hd:(c + 1) * hd]
        cms = jnp.mean(chunk * chunk, axis=-1, keepdims=True)
        xq_ref[:, c * hd:(c + 1) * hd] = (chunk * lax.rsqrt(cms + EPS) * gcq_ref[...]).astype(BF16)

    d = xf.shape[1]
    for c in range(N_BRANCH):
        z = _dot(h, wg_ref[:, c * d:(c + 1) * d])
        g_ref[:, c * d:(c + 1) * d] = _sigmoid(z).astype(BF16)


def _in_proj(xt, g_mix, w_in, g_q, g_k, g_cq, seq_len, tm=512):
    t, d = xt.shape
    tm = min(tm, seq_len)
    w = d // 2
    wa = w_in[:, :5 * w].astype(BF16)
    wg = w_in[:, 5 * w:].astype(BF16)
    heads = w // MOBA_HEAD_DIM
    e64 = jnp.kron(jnp.eye(heads, dtype=F32), jnp.ones((MOBA_HEAD_DIM, MOBA_HEAD_DIM), F32)).astype(BF16)
    half = MOBA_HEAD_DIM // 2
    inv = ROPE_THETA ** (-jnp.arange(half, dtype=F32) / half)
    ang = jnp.arange(seq_len, dtype=F32)[:, None] * inv[None, :]
    cos = jnp.tile(jnp.cos(ang), (1, LANES // half))
    sin = jnp.tile(jnp.concatenate([-jnp.sin(ang), jnp.sin(ang)], axis=1), (1, LANES // MOBA_HEAD_DIM))
    nt = seq_len // tm
    row = lambda i: (i, 0)
    const = lambda i: (0, 0)
    out_w = jax.ShapeDtypeStruct((t, w), BF16)
    return pl.pallas_call(
        _inproj_kernel,
        grid=(t // tm,),
        in_specs=[
            pl.BlockSpec((tm, d), row),
            pl.BlockSpec((1, d), const),
            pl.BlockSpec((d, 5 * w), const, pipeline_mode=pl.Buffered(1)),
            pl.BlockSpec((d, N_BRANCH * d), const, pipeline_mode=pl.Buffered(1)),
            pl.BlockSpec((w, w), const, pipeline_mode=pl.Buffered(1)),
            pl.BlockSpec((1, w), const),
            pl.BlockSpec((1, w), const),
            pl.BlockSpec((1, w // X_HEADS), const),
            pl.BlockSpec((tm, LANES), lambda i: (i % nt, 0)),
            pl.BlockSpec((tm, LANES), lambda i: (i % nt, 0)),
        ],
        out_specs=[pl.BlockSpec((tm, w), row), pl.BlockSpec((tm, w), row), pl.BlockSpec((tm, 2 * w), row),
                   pl.BlockSpec((tm, w), row), pl.BlockSpec((tm, N_BRANCH * d), row),
                   pl.BlockSpec((heads, tm, LANES), lambda i: (0, i, 0))],
        out_shape=[jax.ShapeDtypeStruct((t, w), F32), out_w, jax.ShapeDtypeStruct((t, 2 * w), BF16), out_w,
                   jax.ShapeDtypeStruct((t, N_BRANCH * d), BF16),
                   jax.ShapeDtypeStruct((heads, t, LANES), F32)],
        compiler_params=_params("parallel"),
        name="in_proj",
    )(xt, g_mix.reshape(1, d), wa, wg, e64,
      jnp.tile(g_q, heads).reshape(1, w), jnp.tile(g_k, heads).reshape(1, w), g_cq.reshape(1, -1), cos, sin)


def _s5_matrices(lam_re, lam_im, log_dt, b_re, b_im, c_re, c_im, n_chunks):
    hp = lax.Precision.HIGHEST
    c = S5_CHUNK
    dt = jnp.exp(log_dt)[:, None]
    mag = jnp.exp(lam_re * dt)
    ar = mag * jnp.cos(lam_im * dt)
    ai = mag * jnp.sin(lam_im * dt)
    nr = ar - 1.0
    den = lam_re * lam_re + lam_im * lam_im
    cr = (nr * lam_re + ai * lam_im) / den
    ci = (ai * lam_re - nr * lam_im) / den
    bbr = cr[..., None] * b_re - ci[..., None] * b_im
    bbi = cr[..., None] * b_im + ci[..., None] * b_re

    def power(n):
        nf = n.astype(F32)[None, :, None]
        m = jnp.exp((lam_re * dt)[:, None, :] * nf)
        th = (lam_im * dt)[:, None, :] * nf
        return m * jnp.cos(th), m * jnp.sin(th)

    pr, pi = power(jnp.arange(c + 1))
    kbr = pr[..., None] * bbr[:, None] - pi[..., None] * bbi[:, None]
    kbi = pr[..., None] * bbi[:, None] + pi[..., None] * bbr[:, None]
    kk = (jnp.einsum('ghp,gtpc->gthc', c_re, kbr, precision=hp)
          - jnp.einsum('ghp,gtpc->gthc', c_im, kbi, precision=hp))
    tq = jnp.arange(c)
    g = kk.shape[0]
    rev = c - 1 - tq
    w_in = jnp.concatenate([kbr[:, rev].transpose(0, 1, 3, 2), kbi[:, rev].transpose(0, 1, 3, 2)], axis=-1)
    w_in = w_in.reshape(g, c * SSM_GROUP, 2 * SSM_STATE)
    prn, pin = pr[:, 1:], pi[:, 1:]
    wo_r = c_re[:, None] * prn[:, :, None, :] - c_im[:, None] * pin[:, :, None, :]
    wo_i = -c_re[:, None] * pin[:, :, None, :] - c_im[:, None] * prn[:, :, None, :]
    w_out = jnp.concatenate([wo_r, wo_i], axis=-1).transpose(0, 3, 1, 2).reshape(g, 2 * SSM_STATE, c * SSM_GROUP)
    n_steps = max(1, int(math.ceil(math.log2(n_chunks))))
    qr, qi = power(c * (2 ** jnp.arange(n_steps)))
    pa = jnp.concatenate([qr, qr], axis=-1)
    pb = jnp.concatenate([-qi, qi], axis=-1)
    return kk.astype(BF16), w_in.astype(BF16), w_out.astype(BF16), pa, pb


S5_SET = LANES // SSM_GROUP
S5_ROWS = 256


def _s5_kernel(u_ref, toep_ref, win_ref, wout_ref, pa_ref, pb_ref, y_ref, s_ref):
    c = S5_CHUNK
    nc = u_ref.shape[0] // c
    rb = min(S5_ROWS, nc)

    def chunk_rows(r0):
        return jnp.concatenate([u_ref[pl.ds(r0 * c + t, rb, stride=c), :] for t in range(c)], axis=1).astype(BF16)

    for blk in range(nc // rb):
        s_ref[blk * rb:(blk + 1) * rb, :] = _dot(chunk_rows(blk * rb), win_ref[0])
    row = lax.broadcasted_iota(jnp.int32, (nc, LANES), 0)
    n_steps = pa_ref.shape[1]
    for g in range(S5_SET):
        lanes = slice(g * LANES, (g + 1) * LANES)
        s = s_ref[:, lanes]
        for k in range(n_steps):
            sh = 1 << k
            if sh >= nc:
                break
            prev = jnp.where(row >= sh, pltpu.roll(s, sh, 0), 0.0)
            s = s + pa_ref[0, k:k + 1, lanes] * prev + pb_ref[0, k:k + 1, lanes] * pltpu.roll(prev, SSM_STATE, 1)
        s_ref[:, lanes] = jnp.where(row >= 1, pltpu.roll(s, 1, 0), 0.0)
    for blk in range(nc // rb):
        rows = slice(blk * rb, (blk + 1) * rb)
        a = chunk_rows(blk * rb)
        carried = _dot(s_ref[rows, :].astype(BF16), wout_ref[0])
        for t in range(c):
            y = _dot(a[:, :(t + 1) * LANES], toep_ref[0, (c - 1 - t) * LANES:, :])
            y_ref[pl.ds(blk * rb * c + t, rb, stride=c), :] = y + carried[:, t * LANES:(t + 1) * LANES]


def _s5_block_diag(mats):
    kk, w_in, w_out, pa, pb = mats
    g, c, h, p2 = kk.shape[0], S5_CHUNK, SSM_GROUP, 2 * SSM_STATE
    ns = g // S5_SET
    eye = jnp.eye(S5_SET, dtype=kk.dtype)
    lag_bd = jnp.einsum('sglhc,gk->slgckh', kk.reshape(ns, S5_SET, c + 1, h, h), eye).reshape(ns, c + 1, LANES, LANES)
    toep_bd = lag_bd[:, c - 1::-1].reshape(ns, c * LANES, LANES)

    def rows_to_states(w):
        wc = w.reshape(ns, S5_SET, c, h, p2).transpose(0, 2, 1, 3, 4).reshape(ns, c, LANES, p2)
        same = (jnp.arange(LANES)[:, None] // h) == (jnp.arange(S5_SET * p2)[None, :] // p2)
        return jnp.where(same, jnp.concatenate([wc] * S5_SET, axis=-1), 0).reshape(ns, c * LANES, S5_SET * p2)

    win_bd = rows_to_states(w_in)
    wout_bd = rows_to_states(w_out.transpose(0, 2, 1)).transpose(0, 2, 1)
    k = pa.shape[1]
    lanes_of = lambda x: x.reshape(ns, S5_SET, k, p2).transpose(0, 2, 1, 3).reshape(ns, k, S5_SET * p2)
    return toep_bd, win_bd, wout_bd, lanes_of(pa), lanes_of(pb)


def _s5(u, mats, bsz, seq_len):
    toep, w_in, w_out, pa, pb = _s5_block_diag(mats)
    ns = toep.shape[0]
    nc = seq_len // S5_CHUNK
    per_set = lambda s, b: (s, 0, 0)
    once = dict(pipeline_mode=pl.Buffered(1))
    io_spec = pl.BlockSpec((seq_len, LANES), lambda s, b: (b, s), **once)
    return pl.pallas_call(
        _s5_kernel,
        grid=(ns, bsz),
        in_specs=[
            io_spec,
            pl.BlockSpec((1,) + toep.shape[1:], per_set, **once),
            pl.BlockSpec((1,) + w_in.shape[1:], per_set, **once),
            pl.BlockSpec((1,) + w_out.shape[1:], per_set, **once),
            pl.BlockSpec((1,) + pa.shape[1:], per_set),
            pl.BlockSpec((1,) + pb.shape[1:], per_set),
        ],
        out_specs=io_spec,
        out_shape=jax.ShapeDtypeStruct(u.shape, F32),
        scratch_shapes=[pltpu.VMEM((nc, S5_SET * 2 * SSM_STATE), F32)],
        compiler_params=_params("arbitrary", "arbitrary"),
        name="s5_scan",
    )(u, toep, w_in, w_out, pa, pb)


def _moba_select_kernel(q_ref, k_ref, tri_ref, sel_ref, cnt_out_ref, km_ref, cnt_ref):
    h = pl.program_id(1)
    t = pl.program_id(2)
    ts = q_ref.shape[0]
    nbk = MOBA_MAX_BLOCKS

    @pl.when(t == 0)
    def _():
        kk = k_ref[...].astype(F32)
        nb = kk.shape[0] // MOBA_BLOCK
        km = jnp.sum(kk.reshape(nb, MOBA_BLOCK, LANES), axis=1) * (1.0 / MOBA_BLOCK)
        if nb < nbk:
            km = jnp.concatenate([km, jnp.zeros((nbk - nb, LANES), F32)], axis=0)
        lane = lax.broadcasted_iota(jnp.int32, (nbk, LANES), 1)
        km_ref[...] = jnp.where((lane // MOBA_HEAD_DIM) == (h % 2), km, 0.0).astype(BF16)
        cnt_ref[...] = jnp.zeros(cnt_ref.shape, F32)

    gate = _dot_nt(km_ref[...], q_ref[...])
    blk = lax.broadcasted_iota(jnp.int32, gate.shape, 0)
    qblk = (t * ts + lax.broadcasted_iota(jnp.int32, gate.shape, 1)) // MOBA_BLOCK
    g = jnp.where(blk < qblk, gate, -jnp.inf)
    selected = jnp.zeros(gate.shape, jnp.bool_)
    picks = []
    for _ in range(MOBA_TOPK):
        mx = jnp.max(g, axis=0, keepdims=True)
        idx = jnp.min(jnp.where(g == mx, blk, nbk), axis=0, keepdims=True)
        hit = blk == idx
        ok = (idx[0:1] < qblk[0:1]) & (mx > -jnp.inf)
        picks.append((hit, idx, ok))
        selected = selected | (hit & ok)
        g = jnp.where(hit, -jnp.inf, g)
    sel01 = jnp.where(selected, 1.0, 0.0).astype(BF16)
    prefix = _dot(sel01, tri_ref[...]) + jnp.tile(cnt_ref[...], (1, ts // LANES))
    rows = [jnp.where(ok, idx, -1) for (_, idx, ok) in picks]
    rows += [jnp.sum(jnp.where(hit, prefix, 0.0), axis=0, keepdims=True).astype(jnp.int32) for (hit, _, _) in picks]
    rows += [jnp.zeros((1, ts), jnp.int32)] * (8 - 2 * MOBA_TOPK)
    sel_ref[0, 0] = jnp.concatenate(rows, axis=0)
    cnt_ref[...] = cnt_ref[...] + _dot(sel01, jnp.ones((ts, LANES), BF16))
    cnt_out_ref[0, 0] = cnt_ref[...]


def _moba_select(q2, kv, bsz, seq_len, ts=1024):
    ts = min(ts, seq_len)
    heads = q2.shape[1] // MOBA_HEAD_DIM
    nt = seq_len // ts
    tri = (jnp.arange(ts)[:, None] < jnp.arange(ts)[None, :]).astype(BF16)
    return pl.pallas_call(
        _moba_select_kernel,
        grid=(bsz, heads, nt),
        in_specs=[
            pl.BlockSpec((ts, LANES), lambda b, h, t: (b * nt + t, h // 2)),
            pl.BlockSpec((seq_len, LANES), lambda b, h, t: (b, 2 * (h // 2))),
            pl.BlockSpec((ts, ts), lambda b, h, t: (0, 0)),
        ],
        out_specs=[pl.BlockSpec((1, 1, 8, ts), lambda b, h, t: (b, h, 0, t)),
                   pl.BlockSpec((1, 1, MOBA_MAX_BLOCKS, LANES), lambda b, h, t: (b, h, 0, 0))],
        out_shape=[jax.ShapeDtypeStruct((bsz, heads, 8, seq_len), jnp.int32),
                   jax.ShapeDtypeStruct((bsz, heads, MOBA_MAX_BLOCKS, LANES), F32)],
        scratch_shapes=[pltpu.VMEM((MOBA_MAX_BLOCKS, LANES), BF16), pltpu.VMEM((MOBA_MAX_BLOCKS, LANES), F32)],
        compiler_params=_params("parallel", "arbitrary", "arbitrary"),
        name="moba_select",
    )(q2, kv, tri)


MOBA_DIAG_BLOCKS = 4


def _moba_diag_kernel(q_ref, kv_ref, o_ref, lse_ref):
    lane = lax.broadcasted_iota(jnp.int32, (MOBA_BLOCK, LANES), 1)
    is_a = lane < MOBA_HEAD_DIM
    r = lax.broadcasted_iota(jnp.int32, (MOBA_BLOCK, MOBA_BLOCK), 0)
    c = lax.broadcasted_iota(jnp.int32, (MOBA_BLOCK, MOBA_BLOCK), 1)
    for u in range(MOBA_DIAG_BLOCKS):
        rows = pl.ds(u * MOBA_BLOCK, MOBA_BLOCK)
        q = q_ref[rows, :]
        kb = kv_ref[rows, :LANES]
        vb = kv_ref[rows, LANES:]
        pvs, ms = [], []
        for own in (is_a, jnp.logical_not(is_a)):
            s = jnp.where(c <= r, _dot_nt(jnp.where(own, q, jnp.zeros_like(q)), kb), NEG_BIG)
            m = jnp.max(s, axis=1, keepdims=True)
            p = jnp.exp(s - m).astype(BF16)
            pvs.append(_dot(p, jnp.where(own, vb, 1.0)))
            ms.append(m)
        num = jnp.where(is_a, pvs[0], pvs[1])
        den = pltpu.roll(jnp.where(is_a, pvs[1], pvs[0]), MOBA_HEAD_DIM, 1)
        o_ref[rows, :] = num / den
        lse_ref[rows, :] = jnp.where(is_a, ms[0], ms[1]) + jnp.log(den)


def _moba_diag(q2, kv, bsz, seq_len):
    npair = q2.shape[1] // LANES
    rows = MOBA_BLOCK * MOBA_DIAG_BLOCKS
    nb = seq_len // rows
    spec = pl.BlockSpec((rows, LANES), lambda b, p, i: (b * nb + i, p))
    out = jax.ShapeDtypeStruct(q2.shape, F32)
    return pl.pallas_call(
        _moba_diag_kernel,
        grid=(bsz, npair, nb),
        in_specs=[spec, pl.BlockSpec((rows, 2 * LANES), lambda b, p, i: (b * nb + i, p))],
        out_specs=[spec, spec],
        out_shape=[out, out],
        compiler_params=_params("parallel", "parallel", "parallel"),
        name="moba_diag",
    )(q2, kv)


MOBA_TILE = 256
MOBA_TILES_PER_STEP = 8


def _moba_grouped_kernel(tile_row_ref, tile_pair_ref, tile_half_ref, tile_real_ref, qd_ref, *refs):
    n = MOBA_TILES_PER_STEP
    kv_refs, o_ref = refs[:n], refs[n]
    i = pl.program_id(0)
    lane = lax.broadcasted_iota(jnp.int32, (MOBA_TILE, LANES), 1)
    kv_lane = lax.broadcasted_iota(jnp.int32, (MOBA_BLOCK, LANES), 1)
    for u in range(n):
        tile = i * n + u
        rows = pl.ds(u * MOBA_TILE, MOBA_TILE)
        own = (lane // MOBA_HEAD_DIM) == tile_half_ref[tile]
        s = _dot_nt(qd_ref[rows, :].astype(BF16), kv_refs[u][:, :LANES])
        m = jnp.max(s, axis=1, keepdims=True)
        p = jnp.exp(s - m).astype(BF16)
        vb = jnp.where((kv_lane // MOBA_HEAD_DIM) == tile_half_ref[tile], kv_refs[u][:, LANES:], 1.0)
        pv = _dot(p, vb)
        row_sum = pltpu.roll(pv, MOBA_HEAD_DIM, 1)
        part = jnp.where(own, pv / row_sum, m + jnp.log(pv))
        o_ref[rows, :] = jnp.where(tile_real_ref[tile] > 0, part, NEG_BIG)


def _moba_grouped(qd, kv, tile_row, tile_pair, tile_half, tile_real):
    n = MOBA_TILES_PER_STEP
    n_tiles = qd.shape[0] // MOBA_TILE

    def kv_spec(u):
        return pl.BlockSpec((MOBA_BLOCK, 2 * LANES), lambda i, tr, tp, th, tl: (tr[i * n + u], tp[i * n + u]))

    grid_spec = pltpu.PrefetchScalarGridSpec(
        num_scalar_prefetch=4,
        grid=(n_tiles // n,),
        in_specs=[pl.BlockSpec((n * MOBA_TILE, LANES), lambda i, tr, tp, th, tl: (i, 0))]
        + [kv_spec(u) for u in range(n)],
        out_specs=pl.BlockSpec((n * MOBA_TILE, LANES), lambda i, tr, tp, th, tl: (i, 0)),
    )
    return pl.pallas_call(
        _moba_grouped_kernel,
        grid_spec=grid_spec,
        out_shape=jax.ShapeDtypeStruct(qd.shape, F32),
        compiler_params=_params("arbitrary"),
        name="moba_grouped",
    )(tile_row, tile_pair, tile_half, tile_real, qd, *([kv] * n))


def _moba_combine_kernel(od_ref, lsed_ref, *refs):
    g_refs, o_ref = refs[:-1], refs[-1]
    lane = lax.broadcasted_iota(jnp.int32, od_ref.shape, 1)
    is_a = lane < MOBA_HEAD_DIM
    parts = [(od_ref[...], lsed_ref[...])]
    for s in range(MOBA_TOPK):
        xa = g_refs[s][0, 0, 0]
        xb = g_refs[MOBA_TOPK + s][0, 0, 0]
        o = jnp.where(is_a, xa, xb)
        lse = pltpu.roll(jnp.where(is_a, xb, xa), MOBA_HEAD_DIM, 1)
        parts.append((o, lse))
    m = parts[0][1]
    for _, lse in parts[1:]:
        m = jnp.maximum(m, lse)
    num = jnp.zeros(od_ref.shape, F32)
    den = jnp.zeros(od_ref.shape, F32)
    for o, lse in parts:
        w = jnp.exp(lse - m)
        num = num + w * o
        den = den + w
    o_ref[...] = (num / den).astype(BF16)


def _moba_combine(od, lsed, gath, bsz, seq_len, tm=1024):
    tm = min(tm, seq_len)
    npair = od.shape[1] // LANES
    nt = seq_len // tm
    spec = pl.BlockSpec((tm, LANES), lambda b, p, i: (b * nt + i, p))

    def g_spec(e, s):
        return pl.BlockSpec((1, 1, 1, tm, LANES), lambda b, p, i: (b, 2 * p + e, s, i, 0))

    return pl.pallas_call(
        _moba_combine_kernel,
        grid=(bsz, npair, nt),
        in_specs=[spec, spec] + [g_spec(e, s) for e in range(2) for s in range(MOBA_TOPK)],
        out_specs=spec,
        out_shape=jax.ShapeDtypeStruct(od.shape, BF16),
        compiler_params=_params("parallel", "parallel", "parallel"),
        name="moba_combine",
    )(od, lsed, *([gath] * (2 * MOBA_TOPK)))


SC_WINDOW = 128
SC_CORES = 2
SC_SUBCORES = 16


def _sc_mesh():
    return plsc.VectorSubcoreMesh(core_axis_name="core", subcore_axis_name="subcore")


def _sc_gather_rows(table, idx):
    n = idx.shape[0]
    d = table.shape[1]
    window = SC_WINDOW
    assert n % (window * SC_CORES * SC_SUBCORES) == 0
    per_core = n // window // SC_CORES

    @functools.partial(pl.kernel, out_type=jax.ShapeDtypeStruct((n, d), table.dtype), mesh=_sc_mesh())
    def gather_kernel(x_hbm, i_hbm, o_hbm):
        base = lax.axis_index("core") * per_core

        def body(i_vmem, o_vmem):
            pltpu.sync_copy(x_hbm.at[i_vmem.at[0]], o_vmem)

        pltpu.emit_pipeline(
            body,
            grid=(per_core,),
            in_specs=[pl.BlockSpec((1, window), index_map=lambda i: (0, base + i))],
            out_specs=[pl.BlockSpec((window, d), index_map=lambda i: (base + i, 0))],
            core_axis_name="subcore",
            dimension_semantics=(pltpu.PARALLEL,),
            trace_scopes=False,
        )(i_hbm, o_hbm)

    return gather_kernel(table, idx.reshape(1, n))


def _sc_scatter_rows(rows, dest, n_out, repeat):
    n_src, d = rows.shape
    n = dest.shape[0]
    assert n == repeat * n_src
    window = SC_WINDOW
    assert n % (window * SC_CORES * SC_SUBCORES) == 0
    per_core = n // window // SC_CORES
    src_windows = n_src // window

    @functools.partial(pl.kernel, out_type=jax.ShapeDtypeStruct((n_out, d), rows.dtype), mesh=_sc_mesh())
    def scatter_kernel(x_hbm, i_hbm, o_hbm):
        base = lax.axis_index("core") * per_core

        def body(x_vmem, i_vmem):
            pltpu.sync_copy(x_vmem, o_hbm.at[i_vmem.at[0]])

        pltpu.emit_pipeline(
            body,
            grid=(per_core,),
            in_specs=[pl.BlockSpec((window, d), index_map=lambda i: ((base + i) % src_windows, 0)),
                      pl.BlockSpec((1, window), index_map=lambda i: (0, base + i))],
            out_specs=[],
            core_axis_name="subcore",
            dimension_semantics=(pltpu.PARALLEL,),
            trace_scopes=False,
        )(x_hbm, i_hbm)

    return scatter_kernel(rows, dest.reshape(1, n))


def _moba(q2, qh, kv, bsz, seq_len):
    heads = q2.shape[1] // MOBA_HEAD_DIM
    nbk = MOBA_MAX_BLOCKS
    sel, cnt = _moba_select(q2, kv, bsz, seq_len)
    od, lsed = _moba_diag(q2, kv, bsz, seq_len)
    counts = cnt[..., 0].astype(jnp.int32).reshape(bsz * heads * nbk)
    pcounts = ((counts + MOBA_TILE - 1) // MOBA_TILE) * MOBA_TILE
    pends = jnp.cumsum(pcounts)
    pstarts = (pends - pcounts).reshape(bsz, heads, 1, 1, nbk)
    n_items = bsz * heads * seq_len * MOBA_TOPK
    step_rows = MOBA_TILE * MOBA_TILES_PER_STEP
    n_rows = -(-(n_items + bsz * heads * nbk * MOBA_TILE) // step_rows) * step_rows
    n_null = step_rows
    idx = sel[:, :, 0:MOBA_TOPK, :]
    rank = sel[:, :, MOBA_TOPK:2 * MOBA_TOPK, :]
    start = jnp.sum(jnp.where(idx[..., None] == jnp.arange(nbk), pstarts, 0), axis=-1)
    null_row = n_rows + jnp.arange(seq_len, dtype=jnp.int32) % n_null
    dest = jnp.where(idx >= 0, start + rank, null_row)
    n_tiles = (n_rows + n_null) // MOBA_TILE
    n_groups = bsz * heads * nbk
    tile_start = jnp.arange(n_tiles, dtype=jnp.int32) * MOBA_TILE
    tile_g = jnp.minimum(jnp.sum((pends[None, :] <= tile_start[:, None]).astype(jnp.int32), axis=1), n_groups - 1)
    tile_real = (tile_start < pends[-1]).astype(jnp.int32)
    tile_head = (tile_g // nbk) % heads
    tile_row = (tile_g // (heads * nbk)) * (seq_len // MOBA_BLOCK) + jnp.minimum(tile_g % nbk, seq_len // MOBA_BLOCK - 1)
    dest_by_slot = dest.transpose(2, 1, 0, 3).reshape(-1)
    qd = _sc_scatter_rows(qh.reshape(-1, LANES), dest_by_slot, n_rows + n_null, MOBA_TOPK)
    part = _moba_grouped(qd, kv, tile_row, tile_head // 2, tile_head % 2, tile_real)
    gath = _sc_gather_rows(part, dest.reshape(-1)).reshape(bsz, heads, MOBA_TOPK, seq_len, LANES)
    return _moba_combine(od, lsed, gath, bsz, seq_len)


def _mem_kv_kernel(mem_ref, gmem_ref, w_ref, gck_ref, k_ref, v_ref):
    xf = mem_ref[0]
    ms = jnp.mean(xf * xf, axis=-1, keepdims=True)
    h = (xf * lax.rsqrt(ms + EPS) * gmem_ref[...]).astype(BF16)
    kv = _dot(h, w_ref[...])
    w = k_ref.shape[2]
    hd = w // X_HEADS
    for c in range(X_HEADS):
        chunk = kv[:, c * hd:(c + 1) * hd]
        cms = jnp.mean(chunk * chunk, axis=-1, keepdims=True)
        k_ref[0, :, c * hd:(c + 1) * hd] = (chunk * lax.rsqrt(cms + EPS) * gck_ref[...]).astype(BF16)
    v_ref[0] = kv[:, w:].astype(BF16)


def _mem_kv(mem, g_mem, w_kv_mem, g_ck):
    bsz, m, d = mem.shape
    w = w_kv_mem.shape[1] // 2
    const = lambda b: (0, 0)
    out = jax.ShapeDtypeStruct((bsz, m, w), BF16)
    return pl.pallas_call(
        _mem_kv_kernel,
        grid=(bsz,),
        in_specs=[pl.BlockSpec((1, m, d), lambda b: (b, 0, 0)), pl.BlockSpec((1, d), const),
                  pl.BlockSpec((d, 2 * w), const), pl.BlockSpec((1, w // X_HEADS), const)],
        out_specs=[pl.BlockSpec((1, m, w), lambda b: (b, 0, 0))] * 2,
        out_shape=[out, out],
        compiler_params=_params("parallel"),
        name="mem_kv",
    )(mem, g_mem.reshape(1, d), w_kv_mem.astype(BF16), g_ck.reshape(1, -1))


def _merge_kernel(x_ref, ys_ref, u_ref, dskip_ref, om_ref, xq_ref, kc_ref, vc_ref, g_ref,
                  wglu_ref, wmo_ref, wco_ref, wout_ref, gffn_ref, wr_ref, br_ref,
                  x1_ref, h2_ref, logit_ref):
    d = x_ref.shape[1]
    y = ys_ref[...].astype(F32) + dskip_ref[...] * u_ref[...].astype(F32)
    ge = 0.5 * y * (1.0 + jnp.tanh(math.sqrt(2.0 / math.pi) * (y + 0.044715 * (y * y * y))))
    z = _dot(ge.astype(BF16), wglu_ref[...])
    merged = g_ref[:, 0:d].astype(F32) * (z[:, :d] * _sigmoid(z[:, d:]))
    merged = merged + g_ref[:, d:2 * d].astype(F32) * _dot(om_ref[...], wmo_ref[...])
    w = xq_ref.shape[1]
    hd = w // X_HEADS
    heads = []
    for c in range(X_HEADS):
        s = _dot_nt(xq_ref[:, c * hd:(c + 1) * hd], kc_ref[0, :, c * hd:(c + 1) * hd]) * (hd ** -0.5)
        p = jnp.exp(s - jnp.max(s, axis=1, keepdims=True))
        p = p / jnp.sum(p, axis=1, keepdims=True)
        heads.append(_dot(p.astype(BF16), vc_ref[0, :, c * hd:(c + 1) * hd]))
    oc = jnp.concatenate(heads, axis=1).astype(BF16)
    merged = merged + g_ref[:, 2 * d:3 * d].astype(F32) * _dot(oc, wco_ref[...])
    x1 = x_ref[...] + _dot(merged.astype(BF16), wout_ref[...])
    x1_ref[...] = x1
    ms = jnp.mean(x1 * x1, axis=-1, keepdims=True)
    h2 = (x1 * lax.rsqrt(ms + EPS) * gffn_ref[...]).astype(BF16)
    packed = _pack_bf16_pairs(h2.astype(F32))
    for j in range(h2_ref.shape[0]):
        h2_ref[j] = packed[:, j * LANES:(j + 1) * LANES]
    logit_ref[...] = _dot_nt(wr_ref[...], h2) + br_ref[...]


def _merge(xt, ys, u, d_skip, om, xqn, kc, vc, gates, w_glu, w_mo, w_co, w_out, g_ffn, w_router, b_router,
           seq_len, tm=512):
    t, d = xt.shape
    tm = min(tm, seq_len)
    w = d // 2
    m = kc.shape[1]
    ne = w_router.shape[1]
    nt = seq_len // tm
    row = lambda i: (i, 0)
    const = lambda i: (0, 0)
    per_b = lambda i: (i // nt, 0, 0)
    once = dict(pipeline_mode=pl.Buffered(1))
    return pl.pallas_call(
        _merge_kernel,
        grid=(t // tm,),
        in_specs=[
            pl.BlockSpec((tm, d), row), pl.BlockSpec((tm, w), row), pl.BlockSpec((tm, w), row),
            pl.BlockSpec((1, w), const), pl.BlockSpec((tm, w), row), pl.BlockSpec((tm, w), row),
            pl.BlockSpec((1, m, w), per_b), pl.BlockSpec((1, m, w), per_b),
            pl.BlockSpec((tm, N_BRANCH * d), row),
            pl.BlockSpec((w, 2 * d), const, **once), pl.BlockSpec((w, d), const, **once),
            pl.BlockSpec((w, d), const, **once), pl.BlockSpec((d, d), const, **once), pl.BlockSpec((1, d), const),
            pl.BlockSpec((ne, d), const), pl.BlockSpec((ne, 1), const),
        ],
        out_specs=[pl.BlockSpec((tm, d), row), pl.BlockSpec((d // 2 // LANES, tm, LANES), lambda i: (0, i, 0)),
                   pl.BlockSpec((ne, tm), lambda i: (0, i))],
        out_shape=[jax.ShapeDtypeStruct((t, d), F32), jax.ShapeDtypeStruct((d // 2 // LANES, t, LANES), jnp.uint32),
                   jax.ShapeDtypeStruct((ne, t), F32)],
        compiler_params=_params("parallel"),
        name="merge",
    )(xt, ys, u, d_skip.reshape(1, w), om, xqn, kc, vc, gates,
      w_glu.astype(BF16), w_mo.astype(BF16), w_co.astype(BF16), w_out.astype(BF16),
      g_ffn.reshape(1, d), w_router.T.astype(BF16), b_router.reshape(ne, 1))


def _moe_kernel(blk_e_ref, blk_used_ref, xs_ref, wgu_ref, bgu_ref, wd_ref, bd_ref, y_ref, wgu_bf, wd_bf):
    i = pl.program_id(0)
    prev = blk_e_ref[jnp.maximum(i - 1, 0)]

    @pl.when((i == 0) | (blk_e_ref[i] != prev))
    def _():
        wgu_bf[...] = wgu_ref[0].astype(BF16)
        wd_bf[...] = wd_ref[0].astype(BF16)

    @pl.when(blk_used_ref[i] > 0)
    def _():
        de = wd_bf.shape[0]
        words = jnp.concatenate([xs_ref[j] for j in range(xs_ref.shape[0])], axis=1)
        xs = jnp.concatenate(_unpack_bf16_pairs(words), axis=1).astype(BF16)
        gu = _dot(xs, wgu_bf[...]) + bgu_ref[0]
        gate = jnp.minimum(gu[:, :de], SWIGLU_LIMIT)
        up = jnp.clip(gu[:, de:], -SWIGLU_LIMIT, SWIGLU_LIMIT)
        act = gate * _sigmoid(SWIGLU_ALPHA * gate) * (up + 1.0)
        y = _dot(act.astype(BF16), wd_bf[...]) + bd_ref[0]
        packed = _pack_bf16_pairs(y.astype(BF16).astype(F32))
        for j in range(y_ref.shape[0]):
            y_ref[j] = packed[:, j * LANES:(j + 1) * LANES]

    @pl.when(blk_used_ref[i] == 0)
    def _():
        y_ref[...] = jnp.zeros(y_ref.shape, y_ref.dtype)


def _moe_experts(xs, blk_e, blk_used, w_gu, b_gu, w_down, b_down):
    slabs, p, _ = xs.shape
    d = 2 * slabs * LANES
    ne, _, de2 = w_gu.shape
    de = de2 // 2
    nblk = p // EXPERT_ROWS
    row_spec = pl.BlockSpec((slabs, EXPERT_ROWS, LANES), lambda i, e, n: (0, i, 0))
    grid_spec = pltpu.PrefetchScalarGridSpec(
        num_scalar_prefetch=2,
        grid=(nblk,),
        in_specs=[
            row_spec,
            pl.BlockSpec((1, d, de2), lambda i, e, n: (e[i], 0, 0)),
            pl.BlockSpec((1, 1, de2), lambda i, e, n: (e[i], 0, 0)),
            pl.BlockSpec((1, de, d), lambda i, e, n: (e[i], 0, 0)),
            pl.BlockSpec((1, 1, d), lambda i, e, n: (e[i], 0, 0)),
        ],
        out_specs=row_spec,
        scratch_shapes=[pltpu.VMEM((d, de2), BF16), pltpu.VMEM((de, d), BF16)],
    )
    return pl.pallas_call(
        _moe_kernel,
        grid_spec=grid_spec,
        out_shape=jax.ShapeDtypeStruct(xs.shape, jnp.uint32),
        compiler_params=_params("arbitrary"),
        name="moe_experts",
    )(blk_e, blk_used, xs, w_gu, b_gu.reshape(ne, 1, de2), w_down, b_down.reshape(ne, 1, d))


def _router_kernel(lg_ref, tri_ref, e_ref, w_ref, r_ref, cnt_out_ref, cnt_ref):
    t = pl.program_id(0)
    ne, ts = lg_ref.shape

    @pl.when(t == 0)
    def _():
        cnt_ref[...] = jnp.zeros(cnt_ref.shape, F32)

    g = lg_ref[...]
    eid = lax.broadcasted_iota(jnp.int32, g.shape, 0)
    selected = jnp.zeros(g.shape, jnp.bool_)
    picks = []
    for _ in range(TOPK_EXPERTS):
        mx = jnp.max(g, axis=0, keepdims=True)
        idx = jnp.min(jnp.where(g == mx, eid, ne), axis=0, keepdims=True)
        hit = eid == idx
        picks.append((hit, idx, mx))
        selected = selected | hit
        g = jnp.where(hit, -jnp.inf, g)
    exps = [jnp.exp(mx - picks[0][2]) for (_, _, mx) in picks]
    total = sum(exps)
    sel01 = jnp.where(selected, 1.0, 0.0).astype(BF16)
    prefix = _dot(sel01, tri_ref[...]) + jnp.tile(cnt_ref[...], (1, ts // LANES))
    pad_i = [jnp.zeros((1, ts), jnp.int32)] * (8 - TOPK_EXPERTS)
    e_ref[...] = jnp.concatenate([idx for (_, idx, _) in picks] + pad_i, axis=0)
    w_ref[...] = jnp.concatenate([e / total for e in exps] + [jnp.zeros((1, ts), F32)] * (8 - TOPK_EXPERTS), axis=0)
    r_ref[...] = jnp.concatenate(
        [jnp.sum(jnp.where(hit, prefix, 0.0), axis=0, keepdims=True).astype(jnp.int32) for (hit, _, _) in picks]
        + pad_i, axis=0)
    cnt_ref[...] = cnt_ref[...] + _dot(sel01, jnp.ones((ts, LANES), BF16))
    cnt_out_ref[...] = cnt_ref[...]


def _router(logits_t, ts=1024):
    ne, t = logits_t.shape
    ts = min(ts, t)
    tri = (jnp.arange(ts)[:, None] < jnp.arange(ts)[None, :]).astype(BF16)
    rows = pl.BlockSpec((8, ts), lambda i: (0, i))
    return pl.pallas_call(
        _router_kernel,
        grid=(t // ts,),
        in_specs=[pl.BlockSpec((ne, ts), lambda i: (0, i)), pl.BlockSpec((ts, ts), lambda i: (0, 0))],
        out_specs=[rows, rows, rows, pl.BlockSpec((ne, LANES), lambda i: (0, 0))],
        out_shape=[jax.ShapeDtypeStruct((8, t), jnp.int32), jax.ShapeDtypeStruct((8, t), F32),
                   jax.ShapeDtypeStruct((8, t), jnp.int32), jax.ShapeDtypeStruct((ne, LANES), F32)],
        scratch_shapes=[pltpu.VMEM((ne, LANES), F32)],
        compiler_params=_params("arbitrary"),
        name="moe_router",
    )(logits_t, tri)


def _moe_mix_kernel(x1_ref, w_ref, pk_ref, o_ref):
    slabs = pk_ref.shape[0]
    wts = w_ref[...]
    lo = [jnp.zeros((x1_ref.shape[0], LANES), F32)] * slabs
    hi = [jnp.zeros((x1_ref.shape[0], LANES), F32)] * slabs
    for k in range(TOPK_EXPERTS):
        wk = wts[:, k:k + 1]
        for j in range(slabs):
            a, b = _unpack_bf16_pairs(pk_ref[j, k])
            lo[j] = lo[j] + wk * a
            hi[j] = hi[j] + wk * b
    o_ref[...] = x1_ref[...] + jnp.concatenate(lo + hi, axis=1)


def _moe_mix(x1, weights_tk, picked, tm=512):
    t, d = x1.shape
    slabs = picked.shape[0]
    return pl.pallas_call(
        _moe_mix_kernel,
        grid=(t // tm,),
        in_specs=[pl.BlockSpec((tm, d), lambda i: (i, 0)),
                  pl.BlockSpec((tm, TOPK_EXPERTS), lambda i: (i, 0)),
                  pl.BlockSpec((slabs, TOPK_EXPERTS, tm, LANES), lambda i: (0, 0, i, 0))],
        out_specs=pl.BlockSpec((tm, d), lambda i: (i, 0)),
        out_shape=jax.ShapeDtypeStruct((t, d), F32),
        compiler_params=_params("parallel"),
        name="moe_mix",
    )(x1, weights_tk, picked)


def _moe_ffn(x1, h2, logits_t, w_gu, b_gu, w_down, b_down):
    t, d = x1.shape
    tk = t * TOPK_EXPERTS
    e8, w8, r8, cnt = _router(logits_t)
    top_e, weights, rank = e8[:TOPK_EXPERTS], w8[:TOPK_EXPERTS], r8[:TOPK_EXPERTS]
    counts = cnt[:, 0].astype(jnp.int32)
    pcounts = ((counts + EXPERT_ROWS - 1) // EXPERT_ROWS) * EXPERT_ROWS
    pends = jnp.cumsum(pcounts)
    pstarts = pends - pcounts
    is_e = top_e[..., None] == jnp.arange(N_EXPERTS, dtype=jnp.int32)
    dest_by_k = jnp.sum(jnp.where(is_e, pstarts, 0), axis=-1) + rank
    nblk = -(-tk // EXPERT_ROWS) + N_EXPERTS
    blk_start = jnp.arange(nblk, dtype=jnp.int32) * EXPERT_ROWS
    blk_e = jnp.minimum(jnp.sum((pends[None, :] <= blk_start[:, None]).astype(jnp.int32), axis=1), N_EXPERTS - 1)
    blk_used = (blk_start < pends[-1]).astype(jnp.int32)
    slabs = d // 2 // LANES
    p = nblk * EXPERT_ROWS
    slab_off = jnp.arange(slabs, dtype=jnp.int32) * p
    dest_kst = (dest_by_k[:, None, :] + slab_off[None, :, None]).reshape(-1)
    dest_skt = (dest_by_k[None, :, :] + slab_off[:, None, None]).reshape(-1)
    xs = _sc_scatter_rows(h2.reshape(slabs * t, LANES), dest_kst, slabs * p, TOPK_EXPERTS)
    ys = _moe_experts(xs.reshape(slabs, p, LANES), blk_e, blk_used, w_gu, b_gu, w_down, b_down)
    picked = _sc_gather_rows(ys.reshape(slabs * p, LANES), dest_skt).reshape(slabs, TOPK_EXPERTS, t, LANES)
    return _moe_mix(x1, weights.T, picked)


def kernel(x, mem, g_mix, w_in, lam_re, lam_im, log_dt, b_re, b_im, c_re, c_im, d_skip, w_glu, g_q, g_k, w_moba_out, g_mem, w_kv_mem, g_cq, g_ck, w_cross_out, w_out, g_ffn, w_router, b_router, w_gu, b_gu, w_down, b_down):
    bsz, seq_len, d = x.shape
    xt = x.reshape(bsz * seq_len, d)
    for l in range(g_mix.shape[0]):
        u, q2, kv, xqn, gates, qh = _in_proj(xt, g_mix[l], w_in[l], g_q[l], g_k[l], g_cq[l], seq_len)
        mats = _s5_matrices(lam_re[l], lam_im[l], log_dt[l], b_re[l], b_im[l], c_re[l], c_im[l],
                            seq_len // S5_CHUNK)
        ys = _s5(u, mats, bsz, seq_len)
        om = _moba(q2, qh, kv, bsz, seq_len)
        kc, vc = _mem_kv(mem, g_mem[l], w_kv_mem[l], g_ck[l])
        x1, h2, logits = _merge(xt, ys, u, d_skip[l], om, xqn, kc, vc, gates, w_glu[l], w_moba_out[l],
                                w_cross_out[l], w_out[l], g_ffn[l], w_router[l], b_router[l], seq_len)
        xt = _moe_ffn(x1, h2, logits, w_gu[l], b_gu[l], w_down[l], b_down[l])
    return xt.reshape(bsz, seq_len, d)
```

```python
import functools
import math

import jax
import jax.numpy as jnp
from jax import lax
from jax.experimental import pallas as pl
from jax.experimental.pallas import tpu as pltpu
from jax.experimental.pallas import tpu_sc as plsc

F32 = jnp.float32
BF16 = jnp.bfloat16

EPS = 1e-6
N_BRANCH = 3
SSM_GROUP = 16
SSM_STATE = 64
S5_CHUNK = 16
MOBA_HEAD_DIM = 64
MOBA_BLOCK = 256
MOBA_TOPK = 3
MOBA_MAX_BLOCKS = 64
ROPE_THETA = 10000.0
X_HEADS = 4
N_EXPERTS = 32
TOPK_EXPERTS = 4
SWIGLU_LIMIT = 7.0
SWIGLU_ALPHA = 1.702
EXPERT_ROWS = 512
NEG_BIG = -1e30
LANES = 128
VMEM_LIMIT_BYTES = 56 * 1024 * 1024


def _params(*sem):
    return pltpu.CompilerParams(dimension_semantics=sem, vmem_limit_bytes=VMEM_LIMIT_BYTES)


def _sigmoid(x):
    return 1.0 / (1.0 + jnp.exp(-x))


def _dot(a, b):
    return jnp.dot(a, b, preferred_element_type=F32)


def _pack_bf16_pairs(x):
    n = x.shape[1] // 2
    lo = lax.bitcast_convert_type(x[:, :n], jnp.uint32) >> 16
    hi = lax.bitcast_convert_type(x[:, n:], jnp.uint32) & jnp.uint32(0xFFFF0000)
    return lo | hi


def _unpack_bf16_pairs(w):
    lo = lax.bitcast_convert_type(w << 16, F32)
    hi = lax.bitcast_convert_type(w & jnp.uint32(0xFFFF0000), F32)
    return lo, hi


def _dot_nt(a, b):
    return lax.dot_general(a, b, (((1,), (1,)), ((), ())), preferred_element_type=F32)


def _inproj_kernel(x_ref, gmix_ref, wa_ref, wg_ref, e64_ref, gq_ref, gk_ref, gcq_ref, cos_ref, sin_ref,
                   u_ref, q_ref, kv_ref, xq_ref, g_ref, qh_ref):
    xf = x_ref[...]
    ms = jnp.mean(xf * xf, axis=-1, keepdims=True)
    h = (xf * lax.rsqrt(ms + EPS) * gmix_ref[...]).astype(BF16)
    a = _dot(h, wa_ref[...])
    w = u_ref.shape[1]
    u_ref[...] = a[:, :w]

    cos = jnp.tile(cos_ref[...], (1, w // LANES))
    sin = jnp.tile(sin_ref[...], (1, w // LANES))
    lane = lax.broadcasted_iota(jnp.int32, (xf.shape[0], w), 1)
    first_half = (lane % MOBA_HEAD_DIM) < (MOBA_HEAD_DIM // 2)

    def qk_norm_rope(raw, g):
        ss = _dot((raw * raw).astype(BF16), e64_ref[...])
        n = raw * lax.rsqrt(ss * (1.0 / MOBA_HEAD_DIM) + EPS) * g
        rot = jnp.where(first_half,
                        pltpu.roll(n, w - MOBA_HEAD_DIM // 2, 1),
                        pltpu.roll(n, MOBA_HEAD_DIM // 2, 1))
        return n * cos + rot * sin

    q = qk_norm_rope(a[:, w:2 * w], gq_ref[...]) * (MOBA_HEAD_DIM ** -0.5)
    q_ref[...] = q.astype(BF16)
    in_a = lax.broadcasted_iota(jnp.int32, (xf.shape[0], LANES), 1) < MOBA_HEAD_DIM
    for p in range(w // LANES):
        pair = q[:, p * LANES:(p + 1) * LANES].astype(BF16).astype(F32)
        qh_ref[2 * p] = jnp.where(in_a, pair, 0.0)
        qh_ref[2 * p + 1] = jnp.where(in_a, 0.0, pair)
    k = qk_norm_rope(a[:, 2 * w:3 * w], gk_ref[...]).astype(BF16)
    for p in range(w // LANES):
        kv_ref[:, (2 * p) * LANES:(2 * p + 1) * LANES] = k[:, p * LANES:(p + 1) * LANES]
        kv_ref[:, (2 * p + 1) * LANES:(2 * p + 2) * LANES] = a[:, 3 * w + p * LANES:3 * w + (p + 1) * LANES].astype(BF16)

    xq = a[:, 4 * w:5 * w]
    hd = w // X_HEADS
    for c in range(X_HEADS):
        chunk = xq[:, c * hd:(c + 1) * hd]
        cms = jnp.mean(chunk * chunk, axis=-1, keepdims=True)
        xq_ref[:, c * hd:(c + 1) * hd] = (chunk * lax.rsqrt(cms + EPS) * gcq_ref[...]).astype(BF16)

    d = xf.shape[1]
    for c in range(N_BRANCH):
        z = _dot(h, wg_ref[:, c * d:(c + 1) * d])
        g_ref[:, c * d:(c + 1) * d] = _sigmoid(z).astype(BF16)


def _in_proj(xt, g_mix, w_in, g_q, g_k, g_cq, seq_len, tm=512):
    t, d = xt.shape
    tm = min(tm, seq_len)
    w = d // 2
    wa = w_in[:, :5 * w].astype(BF16)
    wg = w_in[:, 5 * w:].astype(BF16)
    heads = w // MOBA_HEAD_DIM
    e64 = jnp.kron(jnp.eye(heads, dtype=F32), jnp.ones((MOBA_HEAD_DIM, MOBA_HEAD_DIM), F32)).astype(BF16)
    half = MOBA_HEAD_DIM // 2
    inv = ROPE_THETA ** (-jnp.arange(half, dtype=F32) / half)
    ang = jnp.arange(seq_len, dtype=F32)[:, None] * inv[None, :]
    cos = jnp.tile(jnp.cos(ang), (1, LANES // half))
    sin = jnp.tile(jnp.concatenate([-jnp.sin(ang), jnp.sin(ang)], axis=1), (1, LANES // MOBA_HEAD_DIM))
    nt = seq_len // tm
    row = lambda i: (i, 0)
    const = lambda i: (0, 0)
    out_w = jax.ShapeDtypeStruct((t, w), BF16)
    return pl.pallas_call(
        _inproj_kernel,
        grid=(t // tm,),
        in_specs=[
            pl.BlockSpec((tm, d), row),
            pl.BlockSpec((1, d), const),
            pl.BlockSpec((d, 5 * w), const, pipeline_mode=pl.Buffered(1)),
            pl.BlockSpec((d, N_BRANCH * d), const, pipeline_mode=pl.Buffered(1)),
            pl.BlockSpec((w, w), const, pipeline_mode=pl.Buffered(1)),
            pl.BlockSpec((1, w), const),
            pl.BlockSpec((1, w), const),
            pl.BlockSpec((1, w // X_HEADS), const),
            pl.BlockSpec((tm, LANES), lambda i: (i % nt, 0)),
            pl.BlockSpec((tm, LANES), lambda i: (i % nt, 0)),
        ],
        out_specs=[pl.BlockSpec((tm, w), row), pl.BlockSpec((tm, w), row), pl.BlockSpec((tm, 2 * w), row),
                   pl.BlockSpec((tm, w), row), pl.BlockSpec((tm, N_BRANCH * d), row),
                   pl.BlockSpec((heads, tm, LANES), lambda i: (0, i, 0))],
        out_shape=[jax.ShapeDtypeStruct((t, w), F32), out_w, jax.ShapeDtypeStruct((t, 2 * w), BF16), out_w,
                   jax.ShapeDtypeStruct((t, N_BRANCH * d), BF16),
                   jax.ShapeDtypeStruct((heads, t, LANES), F32)],
        compiler_params=_params("parallel"),
        name="in_proj",
    )(xt, g_mix.reshape(1, d), wa, wg, e64,
      jnp.tile(g_q, heads).reshape(1, w), jnp.tile(g_k, heads).reshape(1, w), g_cq.reshape(1, -1), cos, sin)


def _s5_matrices(lam_re, lam_im, log_dt, b_re, b_im, c_re, c_im, n_chunks):
    hp = lax.Precision.HIGHEST
    c = S5_CHUNK
    dt = jnp.exp(log_dt)[:, None]
    mag = jnp.exp(lam_re * dt)
    ar = mag * jnp.cos(lam_im * dt)
    ai = mag * jnp.sin(lam_im * dt)
    nr = ar - 1.0
    den = lam_re * lam_re + lam_im * lam_im
    cr = (nr * lam_re + ai * lam_im) / den
    ci = (ai * lam_re - nr * lam_im) / den
    bbr = cr[..., None] * b_re - ci[..., None] * b_im
    bbi = cr[..., None] * b_im + ci[..., None] * b_re

    def power(n):
        nf = n.astype(F32)[None, :, None]
        m = jnp.exp((lam_re * dt)[:, None, :] * nf)
        th = (lam_im * dt)[:, None, :] * nf
        return m * jnp.cos(th), m * jnp.sin(th)

    pr, pi = power(jnp.arange(c + 1))
    kbr = pr[..., None] * bbr[:, None] - pi[..., None] * bbi[:, None]
    kbi = pr[..., None] * bbi[:, None] + pi[..., None] * bbr[:, None]
    kk = (jnp.einsum('ghp,gtpc->gthc', c_re, kbr, precision=hp)
          - jnp.einsum('ghp,gtpc->gthc', c_im, kbi, precision=hp))
    tq = jnp.arange(c)
    g = kk.shape[0]
    rev = c - 1 - tq
    w_in = jnp.concatenate([kbr[:, rev].transpose(0, 1, 3, 2), kbi[:, rev].transpose(0, 1, 3, 2)], axis=-1)
    w_in = w_in.reshape(g, c * SSM_GROUP, 2 * SSM_STATE)
    prn, pin = pr[:, 1:], pi[:, 1:]
    wo_r = c_re[:, None] * prn[:, :, None, :] - c_im[:, None] * pin[:, :, None, :]
    wo_i = -c_re[:, None] * pin[:, :, None, :] - c_im[:, None] * prn[:, :, None, :]
    w_out = jnp.concatenate([wo_r, wo_i], axis=-1).transpose(0, 3, 1, 2).reshape(g, 2 * SSM_STATE, c * SSM_GROUP)
    n_steps = max(1, int(math.ceil(math.log2(n_chunks))))
    qr, qi = power(c * (2 ** jnp.arange(n_steps)))
    pa = jnp.concatenate([qr, qr], axis=-1)
    pb = jnp.concatenate([-qi, qi], axis=-1)
    return kk.astype(BF16), w_in.astype(BF16), w_out.astype(BF16), pa, pb


S5_SET = LANES // SSM_GROUP
S5_ROWS = 256


def _s5_kernel(u_ref, toep_ref, win_ref, wout_ref, pa_ref, pb_ref, y_ref, s_ref):
    c = S5_CHUNK
    nc = u_ref.shape[0] // c
    rb = min(S5_ROWS, nc)

    def chunk_rows(r0):
        return jnp.concatenate([u_ref[pl.ds(r0 * c + t, rb, stride=c), :] for t in range(c)], axis=1).astype(BF16)

    for blk in range(nc // rb):
        s_ref[blk * rb:(blk + 1) * rb, :] = _dot(chunk_rows(blk * rb), win_ref[0])
    row = lax.broadcasted_iota(jnp.int32, (nc, LANES), 0)
    n_steps = pa_ref.shape[1]
    for g in range(S5_SET):
        lanes = slice(g * LANES, (g + 1) * LANES)
        s = s_ref[:, lanes]
        for k in range(n_steps):
            sh = 1 << k
            if sh >= nc:
                break
            prev = jnp.where(row >= sh, pltpu.roll(s, sh, 0), 0.0)
            s = s + pa_ref[0, k:k + 1, lanes] * prev + pb_ref[0, k:k + 1, lanes] * pltpu.roll(prev, SSM_STATE, 1)
        s_ref[:, lanes] = jnp.where(row >= 1, pltpu.roll(s, 1, 0), 0.0)
    for blk in range(nc // rb):
        rows = slice(blk * rb, (blk + 1) * rb)
        a = chunk_rows(blk * rb)
        carried = _dot(s_ref[rows, :].astype(BF16), wout_ref[0])
        for t in range(c):
            y = _dot(a[:, :(t + 1) * LANES], toep_ref[0, (c - 1 - t) * LANES:, :])
            y_ref[pl.ds(blk * rb * c + t, rb, stride=c), :] = y + carried[:, t * LANES:(t + 1) * LANES]


def _s5_block_diag(mats):
    kk, w_in, w_out, pa, pb = mats
    g, c, h, p2 = kk.shape[0], S5_CHUNK, SSM_GROUP, 2 * SSM_STATE
    ns = g // S5_SET
    eye = jnp.eye(S5_SET, dtype=kk.dtype)
    lag_bd = jnp.einsum('sglhc,gk->slgckh', kk.reshape(ns, S5_SET, c + 1, h, h), eye).reshape(ns, c + 1, LANES, LANES)
    toep_bd = lag_bd[:, c - 1::-1].reshape(ns, c * LANES, LANES)

    def rows_to_states(w):
        wc = w.reshape(ns, S5_SET, c, h, p2).transpose(0, 2, 1, 3, 4).reshape(ns, c, LANES, p2)
        same = (jnp.arange(LANES)[:, None] // h) == (jnp.arange(S5_SET * p2)[None, :] // p2)
        return jnp.where(same, jnp.concatenate([wc] * S5_SET, axis=-1), 0).reshape(ns, c * LANES, S5_SET * p2)

    win_bd = rows_to_states(w_in)
    wout_bd = rows_to_states(w_out.transpose(0, 2, 1)).transpose(0, 2, 1)
    k = pa.shape[1]
    lanes_of = lambda x: x.reshape(ns, S5_SET, k, p2).transpose(0, 2, 1, 3).reshape(ns, k, S5_SET * p2)
    return toep_bd, win_bd, wout_bd, lanes_of(pa), lanes_of(pb)


def _s5(u, mats, bsz, seq_len):
    toep, w_in, w_out, pa, pb = _s5_block_diag(mats)
    ns = toep.shape[0]
    nc = seq_len // S5_CHUNK
    per_set = lambda s, b: (s, 0, 0)
    once = dict(pipeline_mode=pl.Buffered(1))
    io_spec = pl.BlockSpec((seq_len, LANES), lambda s, b: (b, s), **once)
    return pl.pallas_call(
        _s5_kernel,
        grid=(ns, bsz),
        in_specs=[
            io_spec,
            pl.BlockSpec((1,) + toep.shape[1:], per_set, **once),
            pl.BlockSpec((1,) + w_in.shape[1:], per_set, **once),
            pl.BlockSpec((1,) + w_out.shape[1:], per_set, **once),
            pl.BlockSpec((1,) + pa.shape[1:], per_set),
            pl.BlockSpec((1,) + pb.shape[1:], per_set),
        ],
        out_specs=io_spec,
        out_shape=jax.ShapeDtypeStruct(u.shape, F32),
        scratch_shapes=[pltpu.VMEM((nc, S5_SET * 2 * SSM_STATE), F32)],
        compiler_params=_params("arbitrary", "arbitrary"),
        name="s5_scan",
    )(u, toep, w_in, w_out, pa, pb)


def _moba_select_kernel(q_ref, k_ref, tri_ref, sel_ref, cnt_out_ref, km_ref, cnt_ref):
    h = pl.program_id(1)
    t = pl.program_id(2)
    ts = q_ref.shape[0]
    nbk = MOBA_MAX_BLOCKS

    @pl.when(t == 0)
    def _():
        kk = k_ref[...].astype(F32)
        nb = kk.shape[0] // MOBA_BLOCK
        km = jnp.sum(kk.reshape(nb, MOBA_BLOCK, LANES), axis=1) * (1.0 / MOBA_BLOCK)
        if nb < nbk:
            km = jnp.concatenate([km, jnp.zeros((nbk - nb, LANES), F32)], axis=0)
        lane = lax.broadcasted_iota(jnp.int32, (nbk, LANES), 1)
        km_ref[...] = jnp.where((lane // MOBA_HEAD_DIM) == (h % 2), km, 0.0).astype(BF16)
        cnt_ref[...] = jnp.zeros(cnt_ref.shape, F32)

    gate = _dot_nt(km_ref[...], q_ref[...])
    blk = lax.broadcasted_iota(jnp.int32, gate.shape, 0)
    qblk = (t * ts + lax.broadcasted_iota(jnp.int32, gate.shape, 1)) // MOBA_BLOCK
    g = jnp.where(blk < qblk, gate, -jnp.inf)
    selected = jnp.zeros(gate.shape, jnp.bool_)
    picks = []
    for _ in range(MOBA_TOPK):
        mx = jnp.max(g, axis=0, keepdims=True)
        idx = jnp.min(jnp.where(g == mx, blk, nbk), axis=0, keepdims=True)
        hit = blk == idx
        ok = (idx[0:1] < qblk[0:1]) & (mx > -jnp.inf)
        picks.append((hit, idx, ok))
        selected = selected | (hit & ok)
        g = jnp.where(hit, -jnp.inf, g)
    sel01 = jnp.where(selected, 1.0, 0.0).astype(BF16)
    prefix = _dot(sel01, tri_ref[...]) + jnp.tile(cnt_ref[...], (1, ts // LANES))
    rows = [jnp.where(ok, idx, -1) for (_, idx, ok) in picks]
    rows += [jnp.sum(jnp.where(hit, prefix, 0.0), axis=0, keepdims=True).astype(jnp.int32) for (hit, _, _) in picks]
    rows += [jnp.zeros((1, ts), jnp.int32)] * (8 - 2 * MOBA_TOPK)
    sel_ref[0, 0] = jnp.concatenate(rows, axis=0)
    cnt_ref[...] = cnt_ref[...] + _dot(sel01, jnp.ones((ts, LANES), BF16))
    cnt_out_ref[0, 0] = cnt_ref[...]


def _moba_select(q2, kv, bsz, seq_len, ts=1024):
    ts = min(ts, seq_len)
    heads = q2.shape[1] // MOBA_HEAD_DIM
    nt = seq_len // ts
    tri = (jnp.arange(ts)[:, None] < jnp.arange(ts)[None, :]).astype(BF16)
    return pl.pallas_call(
        _moba_select_kernel,
        grid=(bsz, heads, nt),
        in_specs=[
            pl.BlockSpec((ts, LANES), lambda b, h, t: (b * nt + t, h // 2)),
            pl.BlockSpec((seq_len, LANES), lambda b, h, t: (b, 2 * (h // 2))),
            pl.BlockSpec((ts, ts), lambda b, h, t: (0, 0)),
        ],
        out_specs=[pl.BlockSpec((1, 1, 8, ts), lambda b, h, t: (b, h, 0, t)),
                   pl.BlockSpec((1, 1, MOBA_MAX_BLOCKS, LANES), lambda b, h, t: (b, h, 0, 0))],
        out_shape=[jax.ShapeDtypeStruct((bsz, heads, 8, seq_len), jnp.int32),
                   jax.ShapeDtypeStruct((bsz, heads, MOBA_MAX_BLOCKS, LANES), F32)],
        scratch_shapes=[pltpu.VMEM((MOBA_MAX_BLOCKS, LANES), BF16), pltpu.VMEM((MOBA_MAX_BLOCKS, LANES), F32)],
        compiler_params=_params("parallel", "arbitrary", "arbitrary"),
        name="moba_select",
    )(q2, kv, tri)


MOBA_DIAG_BLOCKS = 4


def _moba_diag_kernel(q_ref, kv_ref, o_ref, lse_ref):
    lane = lax.broadcasted_iota(jnp.int32, (MOBA_BLOCK, LANES), 1)
    is_a = lane < MOBA_HEAD_DIM
    r = lax.broadcasted_iota(jnp.int32, (MOBA_BLOCK, MOBA_BLOCK), 0)
    c = lax.broadcasted_iota(jnp.int32, (MOBA_BLOCK, MOBA_BLOCK), 1)
    for u in range(MOBA_DIAG_BLOCKS):
        rows = pl.ds(u * MOBA_BLOCK, MOBA_BLOCK)
        q = q_ref[rows, :]
        kb = kv_ref[rows, :LANES]
        vb = kv_ref[rows, LANES:]
        pvs, ms = [], []
        for own in (is_a, jnp.logical_not(is_a)):
            s = jnp.where(c <= r, _dot_nt(jnp.where(own, q, jnp.zeros_like(q)), kb), NEG_BIG)
            m = jnp.max(s, axis=1, keepdims=True)
            p = jnp.exp(s - m).astype(BF16)
            pvs.append(_dot(p, jnp.where(own, vb, 1.0)))
            ms.append(m)
        num = jnp.where(is_a, pvs[0], pvs[1])
        den = pltpu.roll(jnp.where(is_a, pvs[1], pvs[0]), MOBA_HEAD_DIM, 1)
        o_ref[rows, :] = num / den
        lse_ref[rows, :] = jnp.where(is_a, ms[0], ms[1]) + jnp.log(den)


def _moba_diag(q2, kv, bsz, seq_len):
    npair = q2.shape[1] // LANES
    rows = MOBA_BLOCK * MOBA_DIAG_BLOCKS
    nb = seq_len // rows
    spec = pl.BlockSpec((rows, LANES), lambda b, p, i: (b * nb + i, p))
    out = jax.ShapeDtypeStruct(q2.shape, F32)
    return pl.pallas_call(
        _moba_diag_kernel,
        grid=(bsz, npair, nb),
        in_specs=[spec, pl.BlockSpec((rows, 2 * LANES), lambda b, p, i: (b * nb + i, p))],
        out_specs=[spec, spec],
        out_shape=[out, out],
        compiler_params=_params("parallel", "parallel", "parallel"),
        name="moba_diag",
    )(q2, kv)


MOBA_TILE = 256
MOBA_TILES_PER_STEP = 8


def _moba_grouped_kernel(tile_row_ref, tile_pair_ref, tile_half_ref, tile_real_ref, qd_ref, *refs):
    n = MOBA_TILES_PER_STEP
    kv_refs, o_ref = refs[:n], refs[n]
    i = pl.program_id(0)
    lane = lax.broadcasted_iota(jnp.int32, (MOBA_TILE, LANES), 1)
    kv_lane = lax.broadcasted_iota(jnp.int32, (MOBA_BLOCK, LANES), 1)
    for u in range(n):
        tile = i * n + u
        rows = pl.ds(u * MOBA_TILE, MOBA_TILE)
        own = (lane // MOBA_HEAD_DIM) == tile_half_ref[tile]
        s = _dot_nt(qd_ref[rows, :].astype(BF16), kv_refs[u][:, :LANES])
        m = jnp.max(s, axis=1, keepdims=True)
        p = jnp.exp(s - m).astype(BF16)
        vb = jnp.where((kv_lane // MOBA_HEAD_DIM) == tile_half_ref[tile], kv_refs[u][:, LANES:], 1.0)
        pv = _dot(p, vb)
        row_sum = pltpu.roll(pv, MOBA_HEAD_DIM, 1)
        part = jnp.where(own, pv / row_sum, m + jnp.log(pv))
        o_ref[rows, :] = jnp.where(tile_real_ref[tile] > 0, part, NEG_BIG)


def _moba_grouped(qd, kv, tile_row, tile_pair, tile_half, tile_real):
    n = MOBA_TILES_PER_STEP
    n_tiles = qd.shape[0] // MOBA_TILE

    def kv_spec(u):
        return pl.BlockSpec((MOBA_BLOCK, 2 * LANES), lambda i, tr, tp, th, tl: (tr[i * n + u], tp[i * n + u]))

    grid_spec = pltpu.PrefetchScalarGridSpec(
        num_scalar_prefetch=4,
        grid=(n_tiles // n,),
        in_specs=[pl.BlockSpec((n * MOBA_TILE, LANES), lambda i, tr, tp, th, tl: (i, 0))]
        + [kv_spec(u) for u in range(n)],
        out_specs=pl.BlockSpec((n * MOBA_TILE, LANES), lambda i, tr, tp, th, tl: (i, 0)),
    )
    return pl.pallas_call(
        _moba_grouped_kernel,
        grid_spec=grid_spec,
        out_shape=jax.ShapeDtypeStruct(qd.shape, F32),
        compiler_params=_params("arbitrary"),
        name="moba_grouped",
    )(tile_row, tile_pair, tile_half, tile_real, qd, *([kv] * n))


def _moba_combine_kernel(od_ref, lsed_ref, *refs):
    g_refs, o_ref = refs[:-1], refs[-1]
    lane = lax.broadcasted_iota(jnp.int32, od_ref.shape, 1)
    is_a = lane < MOBA_HEAD_DIM
    parts = [(od_ref[...], lsed_ref[...])]
    for s in range(MOBA_TOPK):
        xa = g_refs[s][0, 0, 0]
        xb = g_refs[MOBA_TOPK + s][0, 0, 0]
        o = jnp.where(is_a, xa, xb)
        lse = pltpu.roll(jnp.where(is_a, xb, xa), MOBA_HEAD_DIM, 1)
        parts.append((o, lse))
    m = parts[0][1]
    for _, lse in parts[1:]:
        m = jnp.maximum(m, lse)
    num = jnp.zeros(od_ref.shape, F32)
    den = jnp.zeros(od_ref.shape, F32)
    for o, lse in parts:
        w = jnp.exp(lse - m)
        num = num + w * o
        den = den + w
    o_ref[...] = (num / den).astype(BF16)


def _moba_combine(od, lsed, gath, bsz, seq_len, tm=1024):
    tm = min(tm, seq_len)
    npair = od.shape[1] // LANES
    nt = seq_len // tm
    spec = pl.BlockSpec((tm, LANES), lambda b, p, i: (b * nt + i, p))

    def g_spec(e, s):
        return pl.BlockSpec((1, 1, 1, tm, LANES), lambda b, p, i: (b, 2 * p + e, s, i, 0))

    return pl.pallas_call(
        _moba_combine_kernel,
        grid=(bsz, npair, nt),
        in_specs=[spec, spec] + [g_spec(e, s) for e in range(2) for s in range(MOBA_TOPK)],
        out_specs=spec,
        out_shape=jax.ShapeDtypeStruct(od.shape, BF16),
        compiler_params=_params("parallel", "parallel", "parallel"),
        name="moba_combine",
    )(od, lsed, *([gath] * (2 * MOBA_TOPK)))


SC_WINDOW = 128
SC_CORES = 2
SC_SUBCORES = 16


def _sc_mesh():
    return plsc.VectorSubcoreMesh(core_axis_name="core", subcore_axis_name="subcore")


def _sc_gather_rows(table, idx):
    n = idx.shape[0]
    d = table.shape[1]
    window = SC_WINDOW
    assert n % (window * SC_CORES * SC_SUBCORES) == 0
    per_core = n // window // SC_CORES

    @functools.partial(pl.kernel, out_type=jax.ShapeDtypeStruct((n, d), table.dtype), mesh=_sc_mesh())
    def gather_kernel(x_hbm, i_hbm, o_hbm):
        base = lax.axis_index("core") * per_core

        def body(i_vmem, o_vmem):
            pltpu.sync_copy(x_hbm.at[i_vmem.at[0]], o_vmem)

        pltpu.emit_pipeline(
            body,
            grid=(per_core,),
            in_specs=[pl.BlockSpec((1, window), index_map=lambda i: (0, base + i))],
            out_specs=[pl.BlockSpec((window, d), index_map=lambda i: (base + i, 0))],
            core_axis_name="subcore",
            dimension_semantics=(pltpu.PARALLEL,),
            trace_scopes=False,
        )(i_hbm, o_hbm)

    return gather_kernel(table, idx.reshape(1, n))


def _sc_scatter_rows(rows, dest, n_out, repeat):
    n_src, d = rows.shape
    n = dest.shape[0]
    assert n == repeat * n_src
    window = SC_WINDOW
    assert n % (window * SC_CORES * SC_SUBCORES) == 0
    per_core = n // window // SC_CORES
    src_windows = n_src // window

    @functools.partial(pl.kernel, out_type=jax.ShapeDtypeStruct((n_out, d), rows.dtype), mesh=_sc_mesh())
    def scatter_kernel(x_hbm, i_hbm, o_hbm):
        base = lax.axis_index("core") * per_core

        def body(x_vmem, i_vmem):
            pltpu.sync_copy(x_vmem, o_hbm.at[i_vmem.at[0]])

        pltpu.emit_pipeline(
            body,
            grid=(per_core,),
            in_specs=[pl.BlockSpec((window, d), index_map=lambda i: ((base + i) % src_windows, 0)),
                      pl.BlockSpec((1, window), index_map=lambda i: (0, base + i))],
            out_specs=[],
            core_axis_name="subcore",
            dimension_semantics=(pltpu.PARALLEL,),
            trace_scopes=False,
        )(x_hbm, i_hbm)

    return scatter_kernel(rows, dest.reshape(1, n))


def _moba_dispatch(q2, qh, kv, bsz, seq_len):
    heads = q2.shape[1] // MOBA_HEAD_DIM
    nbk = MOBA_MAX_BLOCKS
    sel, cnt = _moba_select(q2, kv, bsz, seq_len)
    counts = cnt[..., 0].astype(jnp.int32).reshape(bsz * heads * nbk)
    pcounts = ((counts + MOBA_TILE - 1) // MOBA_TILE) * MOBA_TILE
    pends = jnp.cumsum(pcounts)
    pstarts = (pends - pcounts).reshape(bsz, heads, 1, 1, nbk)
    n_items = bsz * heads * seq_len * MOBA_TOPK
    step_rows = MOBA_TILE * MOBA_TILES_PER_STEP
    n_rows = -(-(n_items + bsz * heads * nbk * MOBA_TILE) // step_rows) * step_rows
    n_null = step_rows
    idx = sel[:, :, 0:MOBA_TOPK, :]
    rank = sel[:, :, MOBA_TOPK:2 * MOBA_TOPK, :]
    start = jnp.sum(jnp.where(idx[..., None] == jnp.arange(nbk), pstarts, 0), axis=-1)
    null_row = n_rows + jnp.arange(seq_len, dtype=jnp.int32) % n_null
    dest = jnp.where(idx >= 0, start + rank, null_row)
    n_tiles = (n_rows + n_null) // MOBA_TILE
    n_groups = bsz * heads * nbk
    tile_start = jnp.arange(n_tiles, dtype=jnp.int32) * MOBA_TILE
    tile_g = jnp.minimum(jnp.sum((pends[None, :] <= tile_start[:, None]).astype(jnp.int32), axis=1), n_groups - 1)
    tile_real = (tile_start < pends[-1]).astype(jnp.int32)
    tile_head = (tile_g // nbk) % heads
    tile_row = (tile_g // (heads * nbk)) * (seq_len // MOBA_BLOCK) + jnp.minimum(tile_g % nbk, seq_len // MOBA_BLOCK - 1)
    dest_by_slot = dest.transpose(2, 1, 0, 3).reshape(-1)
    qd = _sc_scatter_rows(qh.reshape(-1, LANES), dest_by_slot, n_rows + n_null, MOBA_TOPK)
    return qd, dest, (tile_row, tile_head // 2, tile_head % 2, tile_real)


def _moba_finish(qd, dest, tiles, od, lsed, kv, bsz, seq_len):
    heads = dest.shape[1]
    part = _moba_grouped(qd, kv, *tiles)
    gath = _sc_gather_rows(part, dest.reshape(-1)).reshape(bsz, heads, MOBA_TOPK, seq_len, LANES)
    return _moba_combine(od, lsed, gath, bsz, seq_len)


def _mem_kv_kernel(mem_ref, gmem_ref, w_ref, gck_ref, k_ref, v_ref):
    xf = mem_ref[0]
    ms = jnp.mean(xf * xf, axis=-1, keepdims=True)
    h = (xf * lax.rsqrt(ms + EPS) * gmem_ref[...]).astype(BF16)
    kv = _dot(h, w_ref[...])
    w = k_ref.shape[2]
    hd = w // X_HEADS
    for c in range(X_HEADS):
        chunk = kv[:, c * hd:(c + 1) * hd]
        cms = jnp.mean(chunk * chunk, axis=-1, keepdims=True)
        k_ref[0, :, c * hd:(c + 1) * hd] = (chunk * lax.rsqrt(cms + EPS) * gck_ref[...]).astype(BF16)
    v_ref[0] = kv[:, w:].astype(BF16)


def _mem_kv(mem, g_mem, w_kv_mem, g_ck):
    bsz, m, d = mem.shape
    w = w_kv_mem.shape[1] // 2
    const = lambda b: (0, 0)
    out = jax.ShapeDtypeStruct((bsz, m, w), BF16)
    return pl.pallas_call(
        _mem_kv_kernel,
        grid=(bsz,),
        in_specs=[pl.BlockSpec((1, m, d), lambda b: (b, 0, 0)), pl.BlockSpec((1, d), const),
                  pl.BlockSpec((d, 2 * w), const), pl.BlockSpec((1, w // X_HEADS), const)],
        out_specs=[pl.BlockSpec((1, m, w), lambda b: (b, 0, 0))] * 2,
        out_shape=[out, out],
        compiler_params=_params("parallel"),
        name="mem_kv",
    )(mem, g_mem.reshape(1, d), w_kv_mem.astype(BF16), g_ck.reshape(1, -1))


def _merge_kernel(x_ref, ys_ref, u_ref, dskip_ref, om_ref, xq_ref, kc_ref, vc_ref, g_ref,
                  wglu_ref, wmo_ref, wco_ref, wout_ref, gffn_ref, wr_ref, br_ref,
                  x1_ref, h2_ref, logit_ref):
    d = x_ref.shape[1]
    y = ys_ref[...].astype(F32) + dskip_ref[...] * u_ref[...].astype(F32)
    ge = 0.5 * y * (1.0 + jnp.tanh(math.sqrt(2.0 / math.pi) * (y + 0.044715 * (y * y * y))))
    z = _dot(ge.astype(BF16), wglu_ref[...])
    merged = g_ref[:, 0:d].astype(F32) * (z[:, :d] * _sigmoid(z[:, d:]))
    merged = merged + g_ref[:, d:2 * d].astype(F32) * _dot(om_ref[...], wmo_ref[...])
    w = xq_ref.shape[1]
    hd = w // X_HEADS
    heads = []
    for c in range(X_HEADS):
        s = _dot_nt(xq_ref[:, c * hd:(c + 1) * hd], kc_ref[0, :, c * hd:(c + 1) * hd]) * (hd ** -0.5)
        p = jnp.exp(s - jnp.max(s, axis=1, keepdims=True))
        p = p / jnp.sum(p, axis=1, keepdims=True)
        heads.append(_dot(p.astype(BF16), vc_ref[0, :, c * hd:(c + 1) * hd]))
    oc = jnp.concatenate(heads, axis=1).astype(BF16)
    merged = merged + g_ref[:, 2 * d:3 * d].astype(F32) * _dot(oc, wco_ref[...])
    x1 = x_ref[...] + _dot(merged.astype(BF16), wout_ref[...])
    x1_ref[...] = x1
    ms = jnp.mean(x1 * x1, axis=-1, keepdims=True)
    h2 = (x1 * lax.rsqrt(ms + EPS) * gffn_ref[...]).astype(BF16)
    packed = _pack_bf16_pairs(h2.astype(F32))
    for j in range(h2_ref.shape[0]):
        h2_ref[j] = packed[:, j * LANES:(j + 1) * LANES]
    logit_ref[...] = _dot_nt(wr_ref[...], h2) + br_ref[...]


def _merge(xt, ys, u, d_skip, om, xqn, kc, vc, gates, w_glu, w_mo, w_co, w_out, g_ffn, w_router, b_router,
           seq_len, tm=512):
    t, d = xt.shape
    tm = min(tm, seq_len)
    w = d // 2
    m = kc.shape[1]
    ne = w_router.shape[1]
    nt = seq_len // tm
    row = lambda i: (i, 0)
    const = lambda i: (0, 0)
    per_b = lambda i: (i // nt, 0, 0)
    once = dict(pipeline_mode=pl.Buffered(1))
    return pl.pallas_call(
        _merge_kernel,
        grid=(t // tm,),
        in_specs=[
            pl.BlockSpec((tm, d), row), pl.BlockSpec((tm, w), row), pl.BlockSpec((tm, w), row),
            pl.BlockSpec((1, w), const), pl.BlockSpec((tm, w), row), pl.BlockSpec((tm, w), row),
            pl.BlockSpec((1, m, w), per_b), pl.BlockSpec((1, m, w), per_b),
            pl.BlockSpec((tm, N_BRANCH * d), row),
            pl.BlockSpec((w, 2 * d), const, **once), pl.BlockSpec((w, d), const, **once),
            pl.BlockSpec((w, d), const, **once), pl.BlockSpec((d, d), const, **once), pl.BlockSpec((1, d), const),
            pl.BlockSpec((ne, d), const), pl.BlockSpec((ne, 1), const),
        ],
        out_specs=[pl.BlockSpec((tm, d), row), pl.BlockSpec((d // 2 // LANES, tm, LANES), lambda i: (0, i, 0)),
                   pl.BlockSpec((ne, tm), lambda i: (0, i))],
        out_shape=[jax.ShapeDtypeStruct((t, d), F32), jax.ShapeDtypeStruct((d // 2 // LANES, t, LANES), jnp.uint32),
                   jax.ShapeDtypeStruct((ne, t), F32)],
        compiler_params=_params("parallel"),
        name="merge",
    )(xt, ys, u, d_skip.reshape(1, w), om, xqn, kc, vc, gates,
      w_glu.astype(BF16), w_mo.astype(BF16), w_co.astype(BF16), w_out.astype(BF16),
      g_ffn.reshape(1, d), w_router.T.astype(BF16), b_router.reshape(ne, 1))


def _moe_kernel(blk_e_ref, blk_used_ref, xs_ref, wgu_ref, bgu_ref, wd_ref, bd_ref, y_ref, wgu_bf, wd_bf):
    i = pl.program_id(0)
    prev = blk_e_ref[jnp.maximum(i - 1, 0)]

    @pl.when((i == 0) | (blk_e_ref[i] != prev))
    def _():
        wgu_bf[...] = wgu_ref[0].astype(BF16)
        wd_bf[...] = wd_ref[0].astype(BF16)

    @pl.when(blk_used_ref[i] > 0)
    def _():
        de = wd_bf.shape[0]
        words = jnp.concatenate([xs_ref[j] for j in range(xs_ref.shape[0])], axis=1)
        xs = jnp.concatenate(_unpack_bf16_pairs(words), axis=1).astype(BF16)
        gu = _dot(xs, wgu_bf[...]) + bgu_ref[0]
        gate = jnp.minimum(gu[:, :de], SWIGLU_LIMIT)
        up = jnp.clip(gu[:, de:], -SWIGLU_LIMIT, SWIGLU_LIMIT)
        act = gate * _sigmoid(SWIGLU_ALPHA * gate) * (up + 1.0)
        y = _dot(act.astype(BF16), wd_bf[...]) + bd_ref[0]
        packed = _pack_bf16_pairs(y.astype(BF16).astype(F32))
        for j in range(y_ref.shape[0]):
            y_ref[j] = packed[:, j * LANES:(j + 1) * LANES]

    @pl.when(blk_used_ref[i] == 0)
    def _():
        y_ref[...] = jnp.zeros(y_ref.shape, y_ref.dtype)


def _moe_experts(xs, blk_e, blk_used, w_gu, b_gu, w_down, b_down):
    slabs, p, _ = xs.shape
    d = 2 * slabs * LANES
    ne, _, de2 = w_gu.shape
    de = de2 // 2
    nblk = p // EXPERT_ROWS
    row_spec = pl.BlockSpec((slabs, EXPERT_ROWS, LANES), lambda i, e, n: (0, i, 0))
    grid_spec = pltpu.PrefetchScalarGridSpec(
        num_scalar_prefetch=2,
        grid=(nblk,),
        in_specs=[
            row_spec,
            pl.BlockSpec((1, d, de2), lambda i, e, n: (e[i], 0, 0)),
            pl.BlockSpec((1, 1, de2), lambda i, e, n: (e[i], 0, 0)),
            pl.BlockSpec((1, de, d), lambda i, e, n: (e[i], 0, 0)),
            pl.BlockSpec((1, 1, d), lambda i, e, n: (e[i], 0, 0)),
        ],
        out_specs=row_spec,
        scratch_shapes=[pltpu.VMEM((d, de2), BF16), pltpu.VMEM((de, d), BF16)],
    )
    return pl.pallas_call(
        _moe_kernel,
        grid_spec=grid_spec,
        out_shape=jax.ShapeDtypeStruct(xs.shape, jnp.uint32),
        compiler_params=_params("arbitrary"),
        name="moe_experts",
    )(blk_e, blk_used, xs, w_gu, b_gu.reshape(ne, 1, de2), w_down, b_down.reshape(ne, 1, d))


def _router_kernel(lg_ref, tri_ref, e_ref, w_ref, r_ref, cnt_out_ref, cnt_ref):
    t = pl.program_id(0)
    ne, ts = lg_ref.shape

    @pl.when(t == 0)
    def _():
        cnt_ref[...] = jnp.zeros(cnt_ref.shape, F32)

    g = lg_ref[...]
    eid = lax.broadcasted_iota(jnp.int32, g.shape, 0)
    selected = jnp.zeros(g.shape, jnp.bool_)
    picks = []
    for _ in range(TOPK_EXPERTS):
        mx = jnp.max(g, axis=0, keepdims=True)
        idx = jnp.min(jnp.where(g == mx, eid, ne), axis=0, keepdims=True)
        hit = eid == idx
        picks.append((hit, idx, mx))
        selected = selected | hit
        g = jnp.where(hit, -jnp.inf, g)
    exps = [jnp.exp(mx - picks[0][2]) for (_, _, mx) in picks]
    total = sum(exps)
    sel01 = jnp.where(selected, 1.0, 0.0).astype(BF16)
    prefix = _dot(sel01, tri_ref[...]) + jnp.tile(cnt_ref[...], (1, ts // LANES))
    pad_i = [jnp.zeros((1, ts), jnp.int32)] * (8 - TOPK_EXPERTS)
    e_ref[...] = jnp.concatenate([idx for (_, idx, _) in picks] + pad_i, axis=0)
    w_ref[...] = jnp.concatenate([e / total for e in exps] + [jnp.zeros((1, ts), F32)] * (8 - TOPK_EXPERTS), axis=0)
    r_ref[...] = jnp.concatenate(
        [jnp.sum(jnp.where(hit, prefix, 0.0), axis=0, keepdims=True).astype(jnp.int32) for (hit, _, _) in picks]
        + pad_i, axis=0)
    cnt_ref[...] = cnt_ref[...] + _dot(sel01, jnp.ones((ts, LANES), BF16))
    cnt_out_ref[...] = cnt_ref[...]


def _router(logits_t, ts=1024):
    ne, t = logits_t.shape
    ts = min(ts, t)
    tri = (jnp.arange(ts)[:, None] < jnp.arange(ts)[None, :]).astype(BF16)
    rows = pl.BlockSpec((8, ts), lambda i: (0, i))
    return pl.pallas_call(
        _router_kernel,
        grid=(t // ts,),
        in_specs=[pl.BlockSpec((ne, ts), lambda i: (0, i)), pl.BlockSpec((ts, ts), lambda i: (0, 0))],
        out_specs=[rows, rows, rows, pl.BlockSpec((ne, LANES), lambda i: (0, 0))],
        out_shape=[jax.ShapeDtypeStruct((8, t), jnp.int32), jax.ShapeDtypeStruct((8, t), F32),
                   jax.ShapeDtypeStruct((8, t), jnp.int32), jax.ShapeDtypeStruct((ne, LANES), F32)],
        scratch_shapes=[pltpu.VMEM((ne, LANES), F32)],
        compiler_params=_params("arbitrary"),
        name="moe_router",
    )(logits_t, tri)


def _moe_mix_kernel(x1_ref, w_ref, pk_ref, o_ref):
    slabs = pk_ref.shape[0]
    wts = w_ref[...]
    lo = [jnp.zeros((x1_ref.shape[0], LANES), F32)] * slabs
    hi = [jnp.zeros((x1_ref.shape[0], LANES), F32)] * slabs
    for k in range(TOPK_EXPERTS):
        wk = wts[:, k:k + 1]
        for j in range(slabs):
            a, b = _unpack_bf16_pairs(pk_ref[j, k])
            lo[j] = lo[j] + wk * a
            hi[j] = hi[j] + wk * b
    o_ref[...] = x1_ref[...] + jnp.concatenate(lo + hi, axis=1)


def _moe_mix(x1, weights_tk, picked, tm=512):
    t, d = x1.shape
    slabs = picked.shape[0]
    return pl.pallas_call(
        _moe_mix_kernel,
        grid=(t // tm,),
        in_specs=[pl.BlockSpec((tm, d), lambda i: (i, 0)),
                  pl.BlockSpec((tm, TOPK_EXPERTS), lambda i: (i, 0)),
                  pl.BlockSpec((slabs, TOPK_EXPERTS, tm, LANES), lambda i: (0, 0, i, 0))],
        out_specs=pl.BlockSpec((tm, d), lambda i: (i, 0)),
        out_shape=jax.ShapeDtypeStruct((t, d), F32),
        compiler_params=_params("parallel"),
        name="moe_mix",
    )(x1, weights_tk, picked)


def _moe_ffn(x1, h2, logits_t, w_gu, b_gu, w_down, b_down):
    t, d = x1.shape
    tk = t * TOPK_EXPERTS
    e8, w8, r8, cnt = _router(logits_t)
    top_e, weights, rank = e8[:TOPK_EXPERTS], w8[:TOPK_EXPERTS], r8[:TOPK_EXPERTS]
    counts = cnt[:, 0].astype(jnp.int32)
    pcounts = ((counts + EXPERT_ROWS - 1) // EXPERT_ROWS) * EXPERT_ROWS
    pends = jnp.cumsum(pcounts)
    pstarts = pends - pcounts
    is_e = top_e[..., None] == jnp.arange(N_EXPERTS, dtype=jnp.int32)
    dest_by_k = jnp.sum(jnp.where(is_e, pstarts, 0), axis=-1) + rank
    nblk = -(-tk // EXPERT_ROWS) + N_EXPERTS
    blk_start = jnp.arange(nblk, dtype=jnp.int32) * EXPERT_ROWS
    blk_e = jnp.minimum(jnp.sum((pends[None, :] <= blk_start[:, None]).astype(jnp.int32), axis=1), N_EXPERTS - 1)
    blk_used = (blk_start < pends[-1]).astype(jnp.int32)
    slabs = d // 2 // LANES
    p = nblk * EXPERT_ROWS
    slab_off = jnp.arange(slabs, dtype=jnp.int32) * p
    dest_kst = (dest_by_k[:, None, :] + slab_off[None, :, None]).reshape(-1)
    dest_skt = (dest_by_k[None, :, :] + slab_off[:, None, None]).reshape(-1)
    xs = _sc_scatter_rows(h2.reshape(slabs * t, LANES), dest_kst, slabs * p, TOPK_EXPERTS)
    ys = _moe_experts(xs.reshape(slabs, p, LANES), blk_e, blk_used, w_gu, b_gu, w_down, b_down)
    picked = _sc_gather_rows(ys.reshape(slabs * p, LANES), dest_skt).reshape(slabs, TOPK_EXPERTS, t, LANES)
    return _moe_mix(x1, weights.T, picked)


def kernel(x, mem, g_mix, w_in, lam_re, lam_im, log_dt, b_re, b_im, c_re, c_im, d_skip, w_glu, g_q, g_k, w_moba_out, g_mem, w_kv_mem, g_cq, g_ck, w_cross_out, w_out, g_ffn, w_router, b_router, w_gu, b_gu, w_down, b_down):
    bsz, seq_len, d = x.shape
    xt = x.reshape(bsz * seq_len, d)
    for l in range(g_mix.shape[0]):
        u, q2, kv, xqn, gates, qh = _in_proj(xt, g_mix[l], w_in[l], g_q[l], g_k[l], g_cq[l], seq_len)
        qd, dest, tiles = _moba_dispatch(q2, qh, kv, bsz, seq_len)
        mats = _s5_matrices(lam_re[l], lam_im[l], log_dt[l], b_re[l], b_im[l], c_re[l], c_im[l],
                            seq_len // S5_CHUNK)
        ys = _s5(u, mats, bsz, seq_len)
        od, lsed = _moba_diag(q2, kv, bsz, seq_len)
        kc, vc = _mem_kv(mem, g_mem[l], w_kv_mem[l], g_ck[l])
        om = _moba_finish(qd, dest, tiles, od, lsed, kv, bsz, seq_len)
        x1, h2, logits = _merge(xt, ys, u, d_skip[l], om, xqn, kc, vc, gates, w_glu[l], w_moba_out[l],
                                w_cross_out[l], w_out[l], g_ffn[l], w_router[l], b_router[l], seq_len)
        xt = _moe_ffn(x1, h2, logits, w_gu[l], b_gu[l], w_down[l], b_down[l])
    return xt.reshape(bsz, seq_len, d)
```

```python
import functools
import math

import jax
import jax.numpy as jnp
from jax import lax
from jax.experimental import pallas as pl
from jax.experimental.pallas import tpu as pltpu
from jax.experimental.pallas import tpu_sc as plsc

F32 = jnp.float32
BF16 = jnp.bfloat16

EPS = 1e-6
N_BRANCH = 3
SSM_GROUP = 16
SSM_STATE = 64
S5_CHUNK = 16
MOBA_HEAD_DIM = 64
MOBA_BLOCK = 256
MOBA_TOPK = 3
MOBA_MAX_BLOCKS = 64
ROPE_THETA = 10000.0
X_HEADS = 4
N_EXPERTS = 32
TOPK_EXPERTS = 4
SWIGLU_LIMIT = 7.0
SWIGLU_ALPHA = 1.702
EXPERT_ROWS = 512
NEG_BIG = -1e30
LANES = 128
VMEM_LIMIT_BYTES = 56 * 1024 * 1024


def _params(*sem):
    return pltpu.CompilerParams(dimension_semantics=sem, vmem_limit_bytes=VMEM_LIMIT_BYTES)


def _sigmoid(x):
    return 1.0 / (1.0 + jnp.exp(-x))


def _dot(a, b):
    return jnp.dot(a, b, preferred_element_type=F32)


def _pack_bf16_pairs(x):
    n = x.shape[1] // 2
    lo = lax.bitcast_convert_type(x[:, :n], jnp.uint32) >> 16
    hi = lax.bitcast_convert_type(x[:, n:], jnp.uint32) & jnp.uint32(0xFFFF0000)
    return lo | hi


def _unpack_bf16_pairs(w):
    lo = lax.bitcast_convert_type(w << 16, F32)
    hi = lax.bitcast_convert_type(w & jnp.uint32(0xFFFF0000), F32)
    return lo, hi


def _dot_nt(a, b):
    return lax.dot_general(a, b, (((1,), (1,)), ((), ())), preferred_element_type=F32)


def _inproj_kernel(x_ref, gmix_ref, wa_ref, wg_ref, e64_ref, gq_ref, gk_ref, gcq_ref, cos_ref, sin_ref,
                   u_ref, q_ref, kv_ref, xq_ref, g_ref, qh_ref):
    xf = x_ref[...]
    ms = jnp.mean(xf * xf, axis=-1, keepdims=True)
    h = (xf * lax.rsqrt(ms + EPS) * gmix_ref[...]).astype(BF16)
    a = _dot(h, wa_ref[...])
    w = u_ref.shape[1]
    u_ref[...] = a[:, :w]

    cos = jnp.tile(cos_ref[...], (1, w // LANES))
    sin = jnp.tile(sin_ref[...], (1, w // LANES))
    lane = lax.broadcasted_iota(jnp.int32, (xf.shape[0], w), 1)
    first_half = (lane % MOBA_HEAD_DIM) < (MOBA_HEAD_DIM // 2)

    def qk_norm_rope(raw, g):
        ss = _dot((raw * raw).astype(BF16), e64_ref[...])
        n = raw * lax.rsqrt(ss * (1.0 / MOBA_HEAD_DIM) + EPS) * g
        rot = jnp.where(first_half,
                        pltpu.roll(n, w - MOBA_HEAD_DIM // 2, 1),
                        pltpu.roll(n, MOBA_HEAD_DIM // 2, 1))
        return n * cos + rot * sin

    q = qk_norm_rope(a[:, w:2 * w], gq_ref[...]) * (MOBA_HEAD_DIM ** -0.5)
    q_ref[...] = q.astype(BF16)
    in_a = lax.broadcasted_iota(jnp.int32, (xf.shape[0], LANES), 1) < MOBA_HEAD_DIM
    for p in range(w // LANES):
        pair = q[:, p * LANES:(p + 1) * LANES].astype(BF16).astype(F32)
        qh_ref[2 * p] = jnp.where(in_a, pair, 0.0)
        qh_ref[2 * p + 1] = jnp.where(in_a, 0.0, pair)
    k = qk_norm_rope(a[:, 2 * w:3 * w], gk_ref[...]).astype(BF16)
    for p in range(w // LANES):
        kv_ref[:, (2 * p) * LANES:(2 * p + 1) * LANES] = k[:, p * LANES:(p + 1) * LANES]
        kv_ref[:, (2 * p + 1) * LANES:(2 * p + 2) * LANES] = a[:, 3 * w + p * LANES:3 * w + (p + 1) * LANES].astype(BF16)

    xq = a[:, 4 * w:5 * w]
    hd = w // X_HEADS
    for c in range(X_HEADS):
        chunk = xq[:, c * hd:(c + 1) * hd]
        cms = jnp.mean(chunk * chunk, axis=-1, keepdims=True)
        xq_ref[:, c * hd:(c + 1) * hd] = (chunk * lax.rsqrt(cms + EPS) * gcq_ref[...]).astype(BF16)

    d = xf.shape[1]
    for c in range(N_BRANCH):
        z = _dot(h, wg_ref[:, c * d:(c + 1) * d])
        g_ref[:, c * d:(c + 1) * d] = _sigmoid(z).astype(BF16)


def _in_proj(xt, g_mix, w_in, g_q, g_k, g_cq, seq_len, tm=512):
    t, d = xt.shape
    tm = min(tm, seq_len)
    w = d // 2
    wa = w_in[:, :5 * w].astype(BF16)
    wg = w_in[:, 5 * w:].astype(BF16)
    heads = w // MOBA_HEAD_DIM
    e64 = jnp.kron(jnp.eye(heads, dtype=F32), jnp.ones((MOBA_HEAD_DIM, MOBA_HEAD_DIM), F32)).astype(BF16)
    half = MOBA_HEAD_DIM // 2
    inv = ROPE_THETA ** (-jnp.arange(half, dtype=F32) / half)
    ang = jnp.arange(seq_len, dtype=F32)[:, None] * inv[None, :]
    cos = jnp.tile(jnp.cos(ang), (1, LANES // half))
    sin = jnp.tile(jnp.concatenate([-jnp.sin(ang), jnp.sin(ang)], axis=1), (1, LANES // MOBA_HEAD_DIM))
    nt = seq_len // tm
    row = lambda i: (i, 0)
    const = lambda i: (0, 0)
    out_w = jax.ShapeDtypeStruct((t, w), BF16)
    return pl.pallas_call(
        _inproj_kernel,
        grid=(t // tm,),
        in_specs=[
            pl.BlockSpec((tm, d), row),
            pl.BlockSpec((1, d), const),
            pl.BlockSpec((d, 5 * w), const, pipeline_mode=pl.Buffered(1)),
            pl.BlockSpec((d, N_BRANCH * d), const, pipeline_mode=pl.Buffered(1)),
            pl.BlockSpec((w, w), const, pipeline_mode=pl.Buffered(1)),
            pl.BlockSpec((1, w), const),
            pl.BlockSpec((1, w), const),
            pl.BlockSpec((1, w // X_HEADS), const),
            pl.BlockSpec((tm, LANES), lambda i: (i % nt, 0)),
            pl.BlockSpec((tm, LANES), lambda i: (i % nt, 0)),
        ],
        out_specs=[pl.BlockSpec((tm, w), row), pl.BlockSpec((tm, w), row), pl.BlockSpec((tm, 2 * w), row),
                   pl.BlockSpec((tm, w), row), pl.BlockSpec((tm, N_BRANCH * d), row),
                   pl.BlockSpec((heads, tm, LANES), lambda i: (0, i, 0))],
        out_shape=[jax.ShapeDtypeStruct((t, w), F32), out_w, jax.ShapeDtypeStruct((t, 2 * w), BF16), out_w,
                   jax.ShapeDtypeStruct((t, N_BRANCH * d), BF16),
                   jax.ShapeDtypeStruct((heads, t, LANES), F32)],
        compiler_params=_params("parallel"),
        name="in_proj",
    )(xt, g_mix.reshape(1, d), wa, wg, e64,
      jnp.tile(g_q, heads).reshape(1, w), jnp.tile(g_k, heads).reshape(1, w), g_cq.reshape(1, -1), cos, sin)


def _s5_matrices(lam_re, lam_im, log_dt, b_re, b_im, c_re, c_im, n_chunks):
    hp = lax.Precision.HIGHEST
    c = S5_CHUNK
    dt = jnp.exp(log_dt)[:, None]
    mag = jnp.exp(lam_re * dt)
    ar = mag * jnp.cos(lam_im * dt)
    ai = mag * jnp.sin(lam_im * dt)
    nr = ar - 1.0
    den = lam_re * lam_re + lam_im * lam_im
    cr = (nr * lam_re + ai * lam_im) / den
    ci = (ai * lam_re - nr * lam_im) / den
    bbr = cr[..., None] * b_re - ci[..., None] * b_im
    bbi = cr[..., None] * b_im + ci[..., None] * b_re

    def power(n):
        nf = n.astype(F32)[None, :, None]
        m = jnp.exp((lam_re * dt)[:, None, :] * nf)
        th = (lam_im * dt)[:, None, :] * nf
        return m * jnp.cos(th), m * jnp.sin(th)

    pr, pi = power(jnp.arange(c + 1))
    kbr = pr[..., None] * bbr[:, None] - pi[..., None] * bbi[:, None]
    kbi = pr[..., None] * bbi[:, None] + pi[..., None] * bbr[:, None]
    kk = (jnp.einsum('ghp,gtpc->gthc', c_re, kbr, precision=hp)
          - jnp.einsum('ghp,gtpc->gthc', c_im, kbi, precision=hp))
    tq = jnp.arange(c)
    g = kk.shape[0]
    rev = c - 1 - tq
    w_in = jnp.concatenate([kbr[:, rev].transpose(0, 1, 3, 2), kbi[:, rev].transpose(0, 1, 3, 2)], axis=-1)
    w_in = w_in.reshape(g, c * SSM_GROUP, 2 * SSM_STATE)
    prn, pin = pr[:, 1:], pi[:, 1:]
    wo_r = c_re[:, None] * prn[:, :, None, :] - c_im[:, None] * pin[:, :, None, :]
    wo_i = -c_re[:, None] * pin[:, :, None, :] - c_im[:, None] * prn[:, :, None, :]
    w_out = jnp.concatenate([wo_r, wo_i], axis=-1).transpose(0, 3, 1, 2).reshape(g, 2 * SSM_STATE, c * SSM_GROUP)
    n_steps = max(1, int(math.ceil(math.log2(n_chunks))))
    qr, qi = power(c * (2 ** jnp.arange(n_steps)))
    pa = jnp.concatenate([qr, qr], axis=-1)
    pb = jnp.concatenate([-qi, qi], axis=-1)
    return kk.astype(BF16), w_in.astype(BF16), w_out.astype(BF16), pa, pb


S5_SET = LANES // SSM_GROUP
S5_ROWS = 256


def _s5_kernel(u_ref, toep_ref, win_ref, wout_ref, pa_ref, pb_ref, y_ref, s_ref):
    c = S5_CHUNK
    nc = u_ref.shape[0] // c
    rb = min(S5_ROWS, nc)

    def chunk_rows(r0):
        return jnp.concatenate([u_ref[pl.ds(r0 * c + t, rb, stride=c), :] for t in range(c)], axis=1).astype(BF16)

    for blk in range(nc // rb):
        s_ref[blk * rb:(blk + 1) * rb, :] = _dot(chunk_rows(blk * rb), win_ref[0])
    row = lax.broadcasted_iota(jnp.int32, (nc, LANES), 0)
    n_steps = pa_ref.shape[1]
    for g in range(S5_SET):
        lanes = slice(g * LANES, (g + 1) * LANES)
        s = s_ref[:, lanes]
        for k in range(n_steps):
            sh = 1 << k
            if sh >= nc:
                break
            prev = jnp.where(row >= sh, pltpu.roll(s, sh, 0), 0.0)
            s = s + pa_ref[0, k:k + 1, lanes] * prev + pb_ref[0, k:k + 1, lanes] * pltpu.roll(prev, SSM_STATE, 1)
        s_ref[:, lanes] = jnp.where(row >= 1, pltpu.roll(s, 1, 0), 0.0)
    for blk in range(nc // rb):
        rows = slice(blk * rb, (blk + 1) * rb)
        a = chunk_rows(blk * rb)
        carried = _dot(s_ref[rows, :].astype(BF16), wout_ref[0])
        for t in range(c):
            y = _dot(a[:, :(t + 1) * LANES], toep_ref[0, (c - 1 - t) * LANES:, :])
            y_ref[pl.ds(blk * rb * c + t, rb, stride=c), :] = y + carried[:, t * LANES:(t + 1) * LANES]


def _s5_block_diag(mats):
    kk, w_in, w_out, pa, pb = mats
    g, c, h, p2 = kk.shape[0], S5_CHUNK, SSM_GROUP, 2 * SSM_STATE
    ns = g // S5_SET
    eye = jnp.eye(S5_SET, dtype=kk.dtype)
    lag_bd = jnp.einsum('sglhc,gk->slgckh', kk.reshape(ns, S5_SET, c + 1, h, h), eye).reshape(ns, c + 1, LANES, LANES)
    toep_bd = lag_bd[:, c - 1::-1].reshape(ns, c * LANES, LANES)

    def rows_to_states(w):
        wc = w.reshape(ns, S5_SET, c, h, p2).transpose(0, 2, 1, 3, 4).reshape(ns, c, LANES, p2)
        same = (jnp.arange(LANES)[:, None] // h) == (jnp.arange(S5_SET * p2)[None, :] // p2)
        return jnp.where(same, jnp.concatenate([wc] * S5_SET, axis=-1), 0).reshape(ns, c * LANES, S5_SET * p2)

    win_bd = rows_to_states(w_in)
    wout_bd = rows_to_states(w_out.transpose(0, 2, 1)).transpose(0, 2, 1)
    k = pa.shape[1]
    lanes_of = lambda x: x.reshape(ns, S5_SET, k, p2).transpose(0, 2, 1, 3).reshape(ns, k, S5_SET * p2)
    return toep_bd, win_bd, wout_bd, lanes_of(pa), lanes_of(pb)


def _s5(u, mats, bsz, seq_len):
    toep, w_in, w_out, pa, pb = _s5_block_diag(mats)
    ns = toep.shape[0]
    nc = seq_len // S5_CHUNK
    per_set = lambda s, b: (s, 0, 0)
    once = dict(pipeline_mode=pl.Buffered(1))
    io_spec = pl.BlockSpec((seq_len, LANES), lambda s, b: (b, s), **once)
    return pl.pallas_call(
        _s5_kernel,
        grid=(ns, bsz),
        in_specs=[
            io_spec,
            pl.BlockSpec((1,) + toep.shape[1:], per_set, **once),
            pl.BlockSpec((1,) + w_in.shape[1:], per_set, **once),
            pl.BlockSpec((1,) + w_out.shape[1:], per_set, **once),
            pl.BlockSpec((1,) + pa.shape[1:], per_set),
            pl.BlockSpec((1,) + pb.shape[1:], per_set),
        ],
        out_specs=io_spec,
        out_shape=jax.ShapeDtypeStruct(u.shape, F32),
        scratch_shapes=[pltpu.VMEM((nc, S5_SET * 2 * SSM_STATE), F32)],
        compiler_params=_params("arbitrary", "arbitrary"),
        name="s5_scan",
    )(u, toep, w_in, w_out, pa, pb)


def _moba_select_kernel(q_ref, k_ref, tri_ref, sel_ref, cnt_out_ref, km_ref, cnt_ref):
    h = pl.program_id(1)
    t = pl.program_id(2)
    ts = q_ref.shape[0]
    nbk = MOBA_MAX_BLOCKS

    @pl.when(t == 0)
    def _():
        kk = k_ref[...].astype(F32)
        nb = kk.shape[0] // MOBA_BLOCK
        km = jnp.sum(kk.reshape(nb, MOBA_BLOCK, LANES), axis=1) * (1.0 / MOBA_BLOCK)
        if nb < nbk:
            km = jnp.concatenate([km, jnp.zeros((nbk - nb, LANES), F32)], axis=0)
        lane = lax.broadcasted_iota(jnp.int32, (nbk, LANES), 1)
        km_ref[...] = jnp.where((lane // MOBA_HEAD_DIM) == (h % 2), km, 0.0).astype(BF16)
        cnt_ref[...] = jnp.zeros(cnt_ref.shape, F32)

    gate = _dot_nt(km_ref[...], q_ref[...])
    blk = lax.broadcasted_iota(jnp.int32, gate.shape, 0)
    qblk = (t * ts + lax.broadcasted_iota(jnp.int32, gate.shape, 1)) // MOBA_BLOCK
    g = jnp.where(blk < qblk, gate, -jnp.inf)
    selected = jnp.zeros(gate.shape, jnp.bool_)
    picks = []
    for _ in range(MOBA_TOPK):
        mx = jnp.max(g, axis=0, keepdims=True)
        idx = jnp.min(jnp.where(g == mx, blk, nbk), axis=0, keepdims=True)
        hit = blk == idx
        ok = (idx[0:1] < qblk[0:1]) & (mx > -jnp.inf)
        picks.append((hit, idx, ok))
        selected = selected | (hit & ok)
        g = jnp.where(hit, -jnp.inf, g)
    sel01 = jnp.where(selected, 1.0, 0.0).astype(BF16)
    prefix = _dot(sel01, tri_ref[...]) + jnp.tile(cnt_ref[...], (1, ts // LANES))
    rows = [jnp.where(ok, idx, -1) for (_, idx, ok) in picks]
    rows += [jnp.sum(jnp.where(hit, prefix, 0.0), axis=0, keepdims=True).astype(jnp.int32) for (hit, _, _) in picks]
    rows += [jnp.zeros((1, ts), jnp.int32)] * (8 - 2 * MOBA_TOPK)
    sel_ref[0, 0] = jnp.concatenate(rows, axis=0)
    cnt_ref[...] = cnt_ref[...] + _dot(sel01, jnp.ones((ts, LANES), BF16))
    cnt_out_ref[0, 0] = cnt_ref[...]


def _moba_select(q2, kv, bsz, seq_len, ts=1024):
    ts = min(ts, seq_len)
    heads = q2.shape[1] // MOBA_HEAD_DIM
    nt = seq_len // ts
    tri = (jnp.arange(ts)[:, None] < jnp.arange(ts)[None, :]).astype(BF16)
    return pl.pallas_call(
        _moba_select_kernel,
        grid=(bsz, heads, nt),
        in_specs=[
            pl.BlockSpec((ts, LANES), lambda b, h, t: (b * nt + t, h // 2)),
            pl.BlockSpec((seq_len, LANES), lambda b, h, t: (b, 2 * (h // 2))),
            pl.BlockSpec((ts, ts), lambda b, h, t: (0, 0)),
        ],
        out_specs=[pl.BlockSpec((1, 1, 8, ts), lambda b, h, t: (b, h, 0, t)),
                   pl.BlockSpec((1, 1, MOBA_MAX_BLOCKS, LANES), lambda b, h, t: (b, h, 0, 0))],
        out_shape=[jax.ShapeDtypeStruct((bsz, heads, 8, seq_len), jnp.int32),
                   jax.ShapeDtypeStruct((bsz, heads, MOBA_MAX_BLOCKS, LANES), F32)],
        scratch_shapes=[pltpu.VMEM((MOBA_MAX_BLOCKS, LANES), BF16), pltpu.VMEM((MOBA_MAX_BLOCKS, LANES), F32)],
        compiler_params=_params("parallel", "arbitrary", "arbitrary"),
        name="moba_select",
    )(q2, kv, tri)


MOBA_DIAG_BLOCKS = 4


def _moba_diag_kernel(q_ref, kv_ref, o_ref, lse_ref):
    lane = lax.broadcasted_iota(jnp.int32, (MOBA_BLOCK, LANES), 1)
    is_a = lane < MOBA_HEAD_DIM
    r = lax.broadcasted_iota(jnp.int32, (MOBA_BLOCK, MOBA_BLOCK), 0)
    c = lax.broadcasted_iota(jnp.int32, (MOBA_BLOCK, MOBA_BLOCK), 1)
    for u in range(MOBA_DIAG_BLOCKS):
        rows = pl.ds(u * MOBA_BLOCK, MOBA_BLOCK)
        q = q_ref[rows, :]
        kb = kv_ref[rows, :LANES]
        vb = kv_ref[rows, LANES:]
        pvs, ms = [], []
        for own in (is_a, jnp.logical_not(is_a)):
            s = jnp.where(c <= r, _dot_nt(jnp.where(own, q, jnp.zeros_like(q)), kb), NEG_BIG)
            m = jnp.max(s, axis=1, keepdims=True)
            p = jnp.exp(s - m).astype(BF16)
            pvs.append(_dot(p, jnp.where(own, vb, 1.0)))
            ms.append(m)
        num = jnp.where(is_a, pvs[0], pvs[1])
        den = pltpu.roll(jnp.where(is_a, pvs[1], pvs[0]), MOBA_HEAD_DIM, 1)
        o_ref[rows, :] = num / den
        lse_ref[rows, :] = jnp.where(is_a, ms[0], ms[1]) + jnp.log(den)


def _moba_diag(q2, kv, bsz, seq_len):
    npair = q2.shape[1] // LANES
    rows = MOBA_BLOCK * MOBA_DIAG_BLOCKS
    nb = seq_len // rows
    spec = pl.BlockSpec((rows, LANES), lambda b, p, i: (b * nb + i, p))
    out = jax.ShapeDtypeStruct(q2.shape, F32)
    return pl.pallas_call(
        _moba_diag_kernel,
        grid=(bsz, npair, nb),
        in_specs=[spec, pl.BlockSpec((rows, 2 * LANES), lambda b, p, i: (b * nb + i, p))],
        out_specs=[spec, spec],
        out_shape=[out, out],
        compiler_params=_params("parallel", "parallel", "parallel"),
        name="moba_diag",
    )(q2, kv)


MOBA_TILE = 256
MOBA_TILES_PER_STEP = 8


def _moba_grouped_kernel(tile_row_ref, tile_pair_ref, tile_half_ref, tile_real_ref, qd_ref, *refs):
    n = MOBA_TILES_PER_STEP
    kv_refs, o_ref = refs[:n], refs[n + 1]
    i = pl.program_id(0)
    lane = lax.broadcasted_iota(jnp.int32, (MOBA_TILE, LANES), 1)
    kv_lane = lax.broadcasted_iota(jnp.int32, (MOBA_BLOCK, LANES), 1)
    for u in range(n):
        tile = i * n + u
        rows = pl.ds(u * MOBA_TILE, MOBA_TILE)
        own = (lane // MOBA_HEAD_DIM) == tile_half_ref[tile]
        s = _dot_nt(qd_ref[rows, :].astype(BF16), kv_refs[u][:, :LANES])
        m = jnp.max(s, axis=1, keepdims=True)
        p = jnp.exp(s - m).astype(BF16)
        vb = jnp.where((kv_lane // MOBA_HEAD_DIM) == tile_half_ref[tile], kv_refs[u][:, LANES:], 1.0)
        pv = _dot(p, vb)
        row_sum = pltpu.roll(pv, MOBA_HEAD_DIM, 1)
        part = jnp.where(own, pv / row_sum, m + jnp.log(pv))
        o_ref[rows, :] = jnp.where(tile_real_ref[tile] > 0, part, NEG_BIG)


def _moba_grouped(qd, kv, tile_row, tile_pair, tile_half, tile_real, run_after):
    n = MOBA_TILES_PER_STEP
    n_tiles = qd.shape[0] // MOBA_TILE

    def kv_spec(u):
        return pl.BlockSpec((MOBA_BLOCK, 2 * LANES), lambda i, tr, tp, th, tl: (tr[i * n + u], tp[i * n + u]))

    grid_spec = pltpu.PrefetchScalarGridSpec(
        num_scalar_prefetch=4,
        grid=(n_tiles // n,),
        in_specs=[pl.BlockSpec((n * MOBA_TILE, LANES), lambda i, tr, tp, th, tl: (i, 0))]
        + [kv_spec(u) for u in range(n)]
        + [pl.BlockSpec((8, LANES), lambda i, tr, tp, th, tl: (0, 0))],
        out_specs=pl.BlockSpec((n * MOBA_TILE, LANES), lambda i, tr, tp, th, tl: (i, 0)),
    )
    return pl.pallas_call(
        _moba_grouped_kernel,
        grid_spec=grid_spec,
        out_shape=jax.ShapeDtypeStruct(qd.shape, F32),
        compiler_params=_params("arbitrary"),
        name="moba_grouped",
    )(tile_row, tile_pair, tile_half, tile_real, qd, *([kv] * n), run_after)


def _moba_combine_kernel(od_ref, lsed_ref, *refs):
    g_refs, o_ref = refs[:-1], refs[-1]
    lane = lax.broadcasted_iota(jnp.int32, od_ref.shape, 1)
    is_a = lane < MOBA_HEAD_DIM
    parts = [(od_ref[...], lsed_ref[...])]
    for s in range(MOBA_TOPK):
        xa = g_refs[s][0, 0, 0]
        xb = g_refs[MOBA_TOPK + s][0, 0, 0]
        o = jnp.where(is_a, xa, xb)
        lse = pltpu.roll(jnp.where(is_a, xb, xa), MOBA_HEAD_DIM, 1)
        parts.append((o, lse))
    m = parts[0][1]
    for _, lse in parts[1:]:
        m = jnp.maximum(m, lse)
    num = jnp.zeros(od_ref.shape, F32)
    den = jnp.zeros(od_ref.shape, F32)
    for o, lse in parts:
        w = jnp.exp(lse - m)
        num = num + w * o
        den = den + w
    o_ref[...] = (num / den).astype(BF16)


def _moba_combine(od, lsed, gath, bsz, seq_len, tm=1024):
    tm = min(tm, seq_len)
    npair = od.shape[1] // LANES
    nt = seq_len // tm
    spec = pl.BlockSpec((tm, LANES), lambda b, p, i: (b * nt + i, p))

    def g_spec(e, s):
        return pl.BlockSpec((1, 1, 1, tm, LANES), lambda b, p, i: (b, 2 * p + e, s, i, 0))

    return pl.pallas_call(
        _moba_combine_kernel,
        grid=(bsz, npair, nt),
        in_specs=[spec, spec] + [g_spec(e, s) for e in range(2) for s in range(MOBA_TOPK)],
        out_specs=spec,
        out_shape=jax.ShapeDtypeStruct(od.shape, BF16),
        compiler_params=_params("parallel", "parallel", "parallel"),
        name="moba_combine",
    )(od, lsed, *([gath] * (2 * MOBA_TOPK)))


SC_WINDOW = 128
SC_CORES = 2
SC_SUBCORES = 16


def _sc_mesh():
    return plsc.VectorSubcoreMesh(core_axis_name="core", subcore_axis_name="subcore")


def _sc_gather_rows(table, idx):
    n = idx.shape[0]
    d = table.shape[1]
    window = SC_WINDOW
    assert n % (window * SC_CORES * SC_SUBCORES) == 0
    per_core = n // window // SC_CORES

    @functools.partial(pl.kernel, out_type=jax.ShapeDtypeStruct((n, d), table.dtype), mesh=_sc_mesh())
    def gather_kernel(x_hbm, i_hbm, o_hbm):
        base = lax.axis_index("core") * per_core

        def body(i_vmem, o_vmem):
            pltpu.sync_copy(x_hbm.at[i_vmem.at[0]], o_vmem)

        pltpu.emit_pipeline(
            body,
            grid=(per_core,),
            in_specs=[pl.BlockSpec((1, window), index_map=lambda i: (0, base + i))],
            out_specs=[pl.BlockSpec((window, d), index_map=lambda i: (base + i, 0))],
            core_axis_name="subcore",
            dimension_semantics=(pltpu.PARALLEL,),
            trace_scopes=False,
        )(i_hbm, o_hbm)

    return gather_kernel(table, idx.reshape(1, n))


def _sc_scatter_rows(rows, dest, n_out, repeat):
    n_src, d = rows.shape
    n = dest.shape[0]
    assert n == repeat * n_src
    window = SC_WINDOW
    assert n % (window * SC_CORES * SC_SUBCORES) == 0
    per_core = n // window // SC_CORES
    src_windows = n_src // window

    @functools.partial(pl.kernel, out_type=jax.ShapeDtypeStruct((n_out, d), rows.dtype), mesh=_sc_mesh())
    def scatter_kernel(x_hbm, i_hbm, o_hbm):
        base = lax.axis_index("core") * per_core

        def body(x_vmem, i_vmem):
            pltpu.sync_copy(x_vmem, o_hbm.at[i_vmem.at[0]])

        pltpu.emit_pipeline(
            body,
            grid=(per_core,),
            in_specs=[pl.BlockSpec((window, d), index_map=lambda i: ((base + i) % src_windows, 0)),
                      pl.BlockSpec((1, window), index_map=lambda i: (0, base + i))],
            out_specs=[],
            core_axis_name="subcore",
            dimension_semantics=(pltpu.PARALLEL,),
            trace_scopes=False,
        )(x_hbm, i_hbm)

    return scatter_kernel(rows, dest.reshape(1, n))


def _moba_dispatch(q2, qh, kv, bsz, seq_len):
    heads = q2.shape[1] // MOBA_HEAD_DIM
    nbk = MOBA_MAX_BLOCKS
    sel, cnt = _moba_select(q2, kv, bsz, seq_len)
    counts = cnt[..., 0].astype(jnp.int32).reshape(bsz * heads * nbk)
    pcounts = ((counts + MOBA_TILE - 1) // MOBA_TILE) * MOBA_TILE
    pends = jnp.cumsum(pcounts)
    pstarts = (pends - pcounts).reshape(bsz, heads, 1, 1, nbk)
    n_items = bsz * heads * seq_len * MOBA_TOPK
    step_rows = MOBA_TILE * MOBA_TILES_PER_STEP
    n_rows = -(-(n_items + bsz * heads * nbk * MOBA_TILE) // step_rows) * step_rows
    n_null = step_rows
    idx = sel[:, :, 0:MOBA_TOPK, :]
    rank = sel[:, :, MOBA_TOPK:2 * MOBA_TOPK, :]
    start = jnp.sum(jnp.where(idx[..., None] == jnp.arange(nbk), pstarts, 0), axis=-1)
    null_row = n_rows + jnp.arange(seq_len, dtype=jnp.int32) % n_null
    dest = jnp.where(idx >= 0, start + rank, null_row)
    n_tiles = (n_rows + n_null) // MOBA_TILE
    n_groups = bsz * heads * nbk
    tile_start = jnp.arange(n_tiles, dtype=jnp.int32) * MOBA_TILE
    tile_g = jnp.minimum(jnp.sum((pends[None, :] <= tile_start[:, None]).astype(jnp.int32), axis=1), n_groups - 1)
    tile_real = (tile_start < pends[-1]).astype(jnp.int32)
    tile_head = (tile_g // nbk) % heads
    tile_row = (tile_g // (heads * nbk)) * (seq_len // MOBA_BLOCK) + jnp.minimum(tile_g % nbk, seq_len // MOBA_BLOCK - 1)
    dest_by_slot = dest.transpose(2, 1, 0, 3).reshape(-1)
    qd = _sc_scatter_rows(qh.reshape(-1, LANES), dest_by_slot, n_rows + n_null, MOBA_TOPK)
    return qd, dest, (tile_row, tile_head // 2, tile_head % 2, tile_real)


def _moba_finish(qd, dest, tiles, od, lsed, kv, bsz, seq_len):
    heads = dest.shape[1]
    part = _moba_grouped(qd, kv, *tiles, lsed)
    gath = _sc_gather_rows(part, dest.reshape(-1)).reshape(bsz, heads, MOBA_TOPK, seq_len, LANES)
    return _moba_combine(od, lsed, gath, bsz, seq_len)


def _mem_kv_kernel(mem_ref, gmem_ref, w_ref, gck_ref, k_ref, v_ref):
    xf = mem_ref[0]
    ms = jnp.mean(xf * xf, axis=-1, keepdims=True)
    h = (xf * lax.rsqrt(ms + EPS) * gmem_ref[...]).astype(BF16)
    kv = _dot(h, w_ref[...])
    w = k_ref.shape[2]
    hd = w // X_HEADS
    for c in range(X_HEADS):
        chunk = kv[:, c * hd:(c + 1) * hd]
        cms = jnp.mean(chunk * chunk, axis=-1, keepdims=True)
        k_ref[0, :, c * hd:(c + 1) * hd] = (chunk * lax.rsqrt(cms + EPS) * gck_ref[...]).astype(BF16)
    v_ref[0] = kv[:, w:].astype(BF16)


def _mem_kv(mem, g_mem, w_kv_mem, g_ck):
    bsz, m, d = mem.shape
    w = w_kv_mem.shape[1] // 2
    const = lambda b: (0, 0)
    out = jax.ShapeDtypeStruct((bsz, m, w), BF16)
    return pl.pallas_call(
        _mem_kv_kernel,
        grid=(bsz,),
        in_specs=[pl.BlockSpec((1, m, d), lambda b: (b, 0, 0)), pl.BlockSpec((1, d), const),
                  pl.BlockSpec((d, 2 * w), const), pl.BlockSpec((1, w // X_HEADS), const)],
        out_specs=[pl.BlockSpec((1, m, w), lambda b: (b, 0, 0))] * 2,
        out_shape=[out, out],
        compiler_params=_params("parallel"),
        name="mem_kv",
    )(mem, g_mem.reshape(1, d), w_kv_mem.astype(BF16), g_ck.reshape(1, -1))


def _merge_kernel(x_ref, ys_ref, u_ref, dskip_ref, om_ref, xq_ref, kc_ref, vc_ref, g_ref,
                  wglu_ref, wmo_ref, wco_ref, wout_ref, gffn_ref, wr_ref, br_ref,
                  x1_ref, h2_ref, logit_ref):
    d = x_ref.shape[1]
    y = ys_ref[...].astype(F32) + dskip_ref[...] * u_ref[...].astype(F32)
    ge = 0.5 * y * (1.0 + jnp.tanh(math.sqrt(2.0 / math.pi) * (y + 0.044715 * (y * y * y))))
    z = _dot(ge.astype(BF16), wglu_ref[...])
    merged = g_ref[:, 0:d].astype(F32) * (z[:, :d] * _sigmoid(z[:, d:]))
    merged = merged + g_ref[:, d:2 * d].astype(F32) * _dot(om_ref[...], wmo_ref[...])
    w = xq_ref.shape[1]
    hd = w // X_HEADS
    heads = []
    for c in range(X_HEADS):
        s = _dot_nt(xq_ref[:, c * hd:(c + 1) * hd], kc_ref[0, :, c * hd:(c + 1) * hd]) * (hd ** -0.5)
        p = jnp.exp(s - jnp.max(s, axis=1, keepdims=True))
        p = p / jnp.sum(p, axis=1, keepdims=True)
        heads.append(_dot(p.astype(BF16), vc_ref[0, :, c * hd:(c + 1) * hd]))
    oc = jnp.concatenate(heads, axis=1).astype(BF16)
    merged = merged + g_ref[:, 2 * d:3 * d].astype(F32) * _dot(oc, wco_ref[...])
    x1 = x_ref[...] + _dot(merged.astype(BF16), wout_ref[...])
    x1_ref[...] = x1
    ms = jnp.mean(x1 * x1, axis=-1, keepdims=True)
    h2 = (x1 * lax.rsqrt(ms + EPS) * gffn_ref[...]).astype(BF16)
    packed = _pack_bf16_pairs(h2.astype(F32))
    for j in range(h2_ref.shape[0]):
        h2_ref[j] = packed[:, j * LANES:(j + 1) * LANES]
    logit_ref[...] = _dot_nt(wr_ref[...], h2) + br_ref[...]


def _merge(xt, ys, u, d_skip, om, xqn, kc, vc, gates, w_glu, w_mo, w_co, w_out, g_ffn, w_router, b_router,
           seq_len, tm=512):
    t, d = xt.shape
    tm = min(tm, seq_len)
    w = d // 2
    m = kc.shape[1]
    ne = w_router.shape[1]
    nt = seq_len // tm
    row = lambda i: (i, 0)
    const = lambda i: (0, 0)
    per_b = lambda i: (i // nt, 0, 0)
    once = dict(pipeline_mode=pl.Buffered(1))
    return pl.pallas_call(
        _merge_kernel,
        grid=(t // tm,),
        in_specs=[
            pl.BlockSpec((tm, d), row), pl.BlockSpec((tm, w), row), pl.BlockSpec((tm, w), row),
            pl.BlockSpec((1, w), const), pl.BlockSpec((tm, w), row), pl.BlockSpec((tm, w), row),
            pl.BlockSpec((1, m, w), per_b), pl.BlockSpec((1, m, w), per_b),
            pl.BlockSpec((tm, N_BRANCH * d), row),
            pl.BlockSpec((w, 2 * d), const, **once), pl.BlockSpec((w, d), const, **once),
            pl.BlockSpec((w, d), const, **once), pl.BlockSpec((d, d), const, **once), pl.BlockSpec((1, d), const),
            pl.BlockSpec((ne, d), const), pl.BlockSpec((ne, 1), const),
        ],
        out_specs=[pl.BlockSpec((tm, d), row), pl.BlockSpec((d // 2 // LANES, tm, LANES), lambda i: (0, i, 0)),
                   pl.BlockSpec((ne, tm), lambda i: (0, i))],
        out_shape=[jax.ShapeDtypeStruct((t, d), F32), jax.ShapeDtypeStruct((d // 2 // LANES, t, LANES), jnp.uint32),
                   jax.ShapeDtypeStruct((ne, t), F32)],
        compiler_params=_params("parallel"),
        name="merge",
    )(xt, ys, u, d_skip.reshape(1, w), om, xqn, kc, vc, gates,
      w_glu.astype(BF16), w_mo.astype(BF16), w_co.astype(BF16), w_out.astype(BF16),
      g_ffn.reshape(1, d), w_router.T.astype(BF16), b_router.reshape(ne, 1))


def _moe_kernel(blk_e_ref, blk_used_ref, xs_ref, wgu_ref, bgu_ref, wd_ref, bd_ref, y_ref, wgu_bf, wd_bf):
    i = pl.program_id(0)
    prev = blk_e_ref[jnp.maximum(i - 1, 0)]

    @pl.when((i == 0) | (blk_e_ref[i] != prev))
    def _():
        wgu_bf[...] = wgu_ref[0].astype(BF16)
        wd_bf[...] = wd_ref[0].astype(BF16)

    @pl.when(blk_used_ref[i] > 0)
    def _():
        de = wd_bf.shape[0]
        words = jnp.concatenate([xs_ref[j] for j in range(xs_ref.shape[0])], axis=1)
        xs = jnp.concatenate(_unpack_bf16_pairs(words), axis=1).astype(BF16)
        gu = _dot(xs, wgu_bf[...]) + bgu_ref[0]
        gate = jnp.minimum(gu[:, :de], SWIGLU_LIMIT)
        up = jnp.clip(gu[:, de:], -SWIGLU_LIMIT, SWIGLU_LIMIT)
        act = gate * _sigmoid(SWIGLU_ALPHA * gate) * (up + 1.0)
        y = _dot(act.astype(BF16), wd_bf[...]) + bd_ref[0]
        packed = _pack_bf16_pairs(y.astype(BF16).astype(F32))
        for j in range(y_ref.shape[0]):
            y_ref[j] = packed[:, j * LANES:(j + 1) * LANES]

    @pl.when(blk_used_ref[i] == 0)
    def _():
        y_ref[...] = jnp.zeros(y_ref.shape, y_ref.dtype)


def _moe_experts(xs, blk_e, blk_used, w_gu, b_gu, w_down, b_down):
    slabs, p, _ = xs.shape
    d = 2 * slabs * LANES
    ne, _, de2 = w_gu.shape
    de = de2 // 2
    nblk = p // EXPERT_ROWS
    row_spec = pl.BlockSpec((slabs, EXPERT_ROWS, LANES), lambda i, e, n: (0, i, 0))
    grid_spec = pltpu.PrefetchScalarGridSpec(
        num_scalar_prefetch=2,
        grid=(nblk,),
        in_specs=[
            row_spec,
            pl.BlockSpec((1, d, de2), lambda i, e, n: (e[i], 0, 0)),
            pl.BlockSpec((1, 1, de2), lambda i, e, n: (e[i], 0, 0)),
            pl.BlockSpec((1, de, d), lambda i, e, n: (e[i], 0, 0)),
            pl.BlockSpec((1, 1, d), lambda i, e, n: (e[i], 0, 0)),
        ],
        out_specs=row_spec,
        scratch_shapes=[pltpu.VMEM((d, de2), BF16), pltpu.VMEM((de, d), BF16)],
    )
    return pl.pallas_call(
        _moe_kernel,
        grid_spec=grid_spec,
        out_shape=jax.ShapeDtypeStruct(xs.shape, jnp.uint32),
        compiler_params=_params("arbitrary"),
        name="moe_experts",
    )(blk_e, blk_used, xs, w_gu, b_gu.reshape(ne, 1, de2), w_down, b_down.reshape(ne, 1, d))


def _router_kernel(lg_ref, tri_ref, e_ref, w_ref, r_ref, cnt_out_ref, cnt_ref):
    t = pl.program_id(0)
    ne, ts = lg_ref.shape

    @pl.when(t == 0)
    def _():
        cnt_ref[...] = jnp.zeros(cnt_ref.shape, F32)

    g = lg_ref[...]
    eid = lax.broadcasted_iota(jnp.int32, g.shape, 0)
    selected = jnp.zeros(g.shape, jnp.bool_)
    picks = []
    for _ in range(TOPK_EXPERTS):
        mx = jnp.max(g, axis=0, keepdims=True)
        idx = jnp.min(jnp.where(g == mx, eid, ne), axis=0, keepdims=True)
        hit = eid == idx
        picks.append((hit, idx, mx))
        selected = selected | hit
        g = jnp.where(hit, -jnp.inf, g)
    exps = [jnp.exp(mx - picks[0][2]) for (_, _, mx) in picks]
    total = sum(exps)
    sel01 = jnp.where(selected, 1.0, 0.0).astype(BF16)
    prefix = _dot(sel01, tri_ref[...]) + jnp.tile(cnt_ref[...], (1, ts // LANES))
    pad_i = [jnp.zeros((1, ts), jnp.int32)] * (8 - TOPK_EXPERTS)
    e_ref[...] = jnp.concatenate([idx for (_, idx, _) in picks] + pad_i, axis=0)
    w_ref[...] = jnp.concatenate([e / total for e in exps] + [jnp.zeros((1, ts), F32)] * (8 - TOPK_EXPERTS), axis=0)
    r_ref[...] = jnp.concatenate(
        [jnp.sum(jnp.where(hit, prefix, 0.0), axis=0, keepdims=True).astype(jnp.int32) for (hit, _, _) in picks]
        + pad_i, axis=0)
    cnt_ref[...] = cnt_ref[...] + _dot(sel01, jnp.ones((ts, LANES), BF16))
    cnt_out_ref[...] = cnt_ref[...]


def _router(logits_t, ts=1024):
    ne, t = logits_t.shape
    ts = min(ts, t)
    tri = (jnp.arange(ts)[:, None] < jnp.arange(ts)[None, :]).astype(BF16)
    rows = pl.BlockSpec((8, ts), lambda i: (0, i))
    return pl.pallas_call(
        _router_kernel,
        grid=(t // ts,),
        in_specs=[pl.BlockSpec((ne, ts), lambda i: (0, i)), pl.BlockSpec((ts, ts), lambda i: (0, 0))],
        out_specs=[rows, rows, rows, pl.BlockSpec((ne, LANES), lambda i: (0, 0))],
        out_shape=[jax.ShapeDtypeStruct((8, t), jnp.int32), jax.ShapeDtypeStruct((8, t), F32),
                   jax.ShapeDtypeStruct((8, t), jnp.int32), jax.ShapeDtypeStruct((ne, LANES), F32)],
        scratch_shapes=[pltpu.VMEM((ne, LANES), F32)],
        compiler_params=_params("arbitrary"),
        name="moe_router",
    )(logits_t, tri)


def _moe_mix_kernel(x1_ref, w_ref, pk_ref, o_ref):
    slabs = pk_ref.shape[0]
    wts = w_ref[...]
    lo = [jnp.zeros((x1_ref.shape[0], LANES), F32)] * slabs
    hi = [jnp.zeros((x1_ref.shape[0], LANES), F32)] * slabs
    for k in range(TOPK_EXPERTS):
        wk = wts[:, k:k + 1]
        for j in range(slabs):
            a, b = _unpack_bf16_pairs(pk_ref[j, k])
            lo[j] = lo[j] + wk * a
            hi[j] = hi[j] + wk * b
    o_ref[...] = x1_ref[...] + jnp.concatenate(lo + hi, axis=1)


def _moe_mix(x1, weights_tk, picked, tm=512):
    t, d = x1.shape
    slabs = picked.shape[0]
    return pl.pallas_call(
        _moe_mix_kernel,
        grid=(t // tm,),
        in_specs=[pl.BlockSpec((tm, d), lambda i: (i, 0)),
                  pl.BlockSpec((tm, TOPK_EXPERTS), lambda i: (i, 0)),
                  pl.BlockSpec((slabs, TOPK_EXPERTS, tm, LANES), lambda i: (0, 0, i, 0))],
        out_specs=pl.BlockSpec((tm, d), lambda i: (i, 0)),
        out_shape=jax.ShapeDtypeStruct((t, d), F32),
        compiler_params=_params("parallel"),
        name="moe_mix",
    )(x1, weights_tk, picked)


def _moe_ffn(x1, h2, logits_t, w_gu, b_gu, w_down, b_down):
    t, d = x1.shape
    tk = t * TOPK_EXPERTS
    e8, w8, r8, cnt = _router(logits_t)
    top_e, weights, rank = e8[:TOPK_EXPERTS], w8[:TOPK_EXPERTS], r8[:TOPK_EXPERTS]
    counts = cnt[:, 0].astype(jnp.int32)
    pcounts = ((counts + EXPERT_ROWS - 1) // EXPERT_ROWS) * EXPERT_ROWS
    pends = jnp.cumsum(pcounts)
    pstarts = pends - pcounts
    is_e = top_e[..., None] == jnp.arange(N_EXPERTS, dtype=jnp.int32)
    dest_by_k = jnp.sum(jnp.where(is_e, pstarts, 0), axis=-1) + rank
    nblk = -(-tk // EXPERT_ROWS) + N_EXPERTS
    blk_start = jnp.arange(nblk, dtype=jnp.int32) * EXPERT_ROWS
    blk_e = jnp.minimum(jnp.sum((pends[None, :] <= blk_start[:, None]).astype(jnp.int32), axis=1), N_EXPERTS - 1)
    blk_used = (blk_start < pends[-1]).astype(jnp.int32)
    slabs = d // 2 // LANES
    p = nblk * EXPERT_ROWS
    slab_off = jnp.arange(slabs, dtype=jnp.int32) * p
    dest_kst = (dest_by_k[:, None, :] + slab_off[None, :, None]).reshape(-1)
    dest_skt = (dest_by_k[None, :, :] + slab_off[:, None, None]).reshape(-1)
    xs = _sc_scatter_rows(h2.reshape(slabs * t, LANES), dest_kst, slabs * p, TOPK_EXPERTS)
    ys = _moe_experts(xs.reshape(slabs, p, LANES), blk_e, blk_used, w_gu, b_gu, w_down, b_down)
    picked = _sc_gather_rows(ys.reshape(slabs * p, LANES), dest_skt).reshape(slabs, TOPK_EXPERTS, t, LANES)
    return _moe_mix(x1, weights.T, picked)


def kernel(x, mem, g_mix, w_in, lam_re, lam_im, log_dt, b_re, b_im, c_re, c_im, d_skip, w_glu, g_q, g_k, w_moba_out, g_mem, w_kv_mem, g_cq, g_ck, w_cross_out, w_out, g_ffn, w_router, b_router, w_gu, b_gu, w_down, b_down):
    bsz, seq_len, d = x.shape
    xt = x.reshape(bsz * seq_len, d)
    for l in range(g_mix.shape[0]):
        u, q2, kv, xqn, gates, qh = _in_proj(xt, g_mix[l], w_in[l], g_q[l], g_k[l], g_cq[l], seq_len)
        qd, dest, tiles = _moba_dispatch(q2, qh, kv, bsz, seq_len)
        mats = _s5_matrices(lam_re[l], lam_im[l], log_dt[l], b_re[l], b_im[l], c_re[l], c_im[l],
                            seq_len // S5_CHUNK)
        ys = _s5(u, mats, bsz, seq_len)
        od, lsed = _moba_diag(q2, kv, bsz, seq_len)
        kc, vc = _mem_kv(mem, g_mem[l], w_kv_mem[l], g_ck[l])
        om = _moba_finish(qd, dest, tiles, od, lsed, kv, bsz, seq_len)
        x1, h2, logits = _merge(xt, ys, u, d_skip[l], om, xqn, kc, vc, gates, w_glu[l], w_moba_out[l],
                                w_cross_out[l], w_out[l], g_ffn[l], w_router[l], b_router[l], seq_len)
        xt = _moe_ffn(x1, h2, logits, w_gu[l], b_gu[l], w_down[l], b_down[l])
    return xt.reshape(bsz, seq_len, d)
```

```python
import functools
import math

import jax
import jax.numpy as jnp
from jax import lax
from jax.experimental import pallas as pl
from jax.experimental.pallas import tpu as pltpu
from jax.experimental.pallas import tpu_sc as plsc

F32 = jnp.float32
BF16 = jnp.bfloat16

EPS = 1e-6
N_BRANCH = 3
SSM_GROUP = 16
SSM_STATE = 64
S5_CHUNK = 16
MOBA_HEAD_DIM = 64
MOBA_BLOCK = 256
MOBA_TOPK = 3
MOBA_MAX_BLOCKS = 64
ROPE_THETA = 10000.0
X_HEADS = 4
N_EXPERTS = 32
TOPK_EXPERTS = 4
SWIGLU_LIMIT = 7.0
SWIGLU_ALPHA = 1.702
EXPERT_ROWS = 512
NEG_BIG = -1e30
LANES = 128
VMEM_LIMIT_BYTES = 56 * 1024 * 1024


def _params(*sem):
    return pltpu.CompilerParams(dimension_semantics=sem, vmem_limit_bytes=VMEM_LIMIT_BYTES)


def _sigmoid(x):
    return 1.0 / (1.0 + jnp.exp(-x))


def _dot(a, b):
    return jnp.dot(a, b, preferred_element_type=F32)


def _pack_bf16_pairs(x):
    n = x.shape[1] // 2
    lo = lax.bitcast_convert_type(x[:, :n], jnp.uint32) >> 16
    hi = lax.bitcast_convert_type(x[:, n:], jnp.uint32) & jnp.uint32(0xFFFF0000)
    return lo | hi


def _unpack_bf16_pairs(w):
    lo = lax.bitcast_convert_type(w << 16, F32)
    hi = lax.bitcast_convert_type(w & jnp.uint32(0xFFFF0000), F32)
    return lo, hi


def _dot_nt(a, b):
    return lax.dot_general(a, b, (((1,), (1,)), ((), ())), preferred_element_type=F32)


def _inproj_kernel(x_ref, gmix_ref, wa_ref, wg_ref, e64_ref, gq_ref, gk_ref, gcq_ref, cos_ref, sin_ref,
                   u_ref, q_ref, kv_ref, xq_ref, g_ref, qh_ref):
    xf = x_ref[...]
    ms = jnp.mean(xf * xf, axis=-1, keepdims=True)
    h = (xf * lax.rsqrt(ms + EPS) * gmix_ref[...]).astype(BF16)
    a = _dot(h, wa_ref[...])
    w = u_ref.shape[1]
    u_ref[...] = a[:, :w]

    cos = jnp.tile(cos_ref[...], (1, w // LANES))
    sin = jnp.tile(sin_ref[...], (1, w // LANES))
    lane = lax.broadcasted_iota(jnp.int32, (xf.shape[0], w), 1)
    first_half = (lane % MOBA_HEAD_DIM) < (MOBA_HEAD_DIM // 2)

    def qk_norm_rope(raw, g):
        ss = _dot((raw * raw).astype(BF16), e64_ref[...])
        n = raw * lax.rsqrt(ss * (1.0 / MOBA_HEAD_DIM) + EPS) * g
        rot = jnp.where(first_half,
                        pltpu.roll(n, w - MOBA_HEAD_DIM // 2, 1),
                        pltpu.roll(n, MOBA_HEAD_DIM // 2, 1))
        return n * cos + rot * sin

    q = qk_norm_rope(a[:, w:2 * w], gq_ref[...]) * (MOBA_HEAD_DIM ** -0.5)
    q_ref[...] = q.astype(BF16)
    in_a = lax.broadcasted_iota(jnp.int32, (xf.shape[0], LANES), 1) < MOBA_HEAD_DIM
    for p in range(w // LANES):
        pair = q[:, p * LANES:(p + 1) * LANES].astype(BF16).astype(F32)
        qh_ref[2 * p] = jnp.where(in_a, pair, 0.0)
        qh_ref[2 * p + 1] = jnp.where(in_a, 0.0, pair)
    k = qk_norm_rope(a[:, 2 * w:3 * w], gk_ref[...]).astype(BF16)
    for p in range(w // LANES):
        kv_ref[:, (2 * p) * LANES:(2 * p + 1) * LANES] = k[:, p * LANES:(p + 1) * LANES]
        kv_ref[:, (2 * p + 1) * LANES:(2 * p + 2) * LANES] = a[:, 3 * w + p * LANES:3 * w + (p + 1) * LANES].astype(BF16)

    xq = a[:, 4 * w:5 * w]
    hd = w // X_HEADS
    for c in range(X_HEADS):
        chunk = xq[:, c * hd:(c + 1) * hd]
        cms = jnp.mean(chunk * chunk, axis=-1, keepdims=True)
        xq_ref[:, c * hd:(c + 1) * hd] = (chunk * lax.rsqrt(cms + EPS) * gcq_ref[...]).astype(BF16)

    d = xf.shape[1]
    for c in range(N_BRANCH):
        z = _dot(h, wg_ref[:, c * d:(c + 1) * d])
        g_ref[:, c * d:(c + 1) * d] = _sigmoid(z).astype(BF16)


def _in_proj(xt, g_mix, w_in, g_q, g_k, g_cq, seq_len, tm=512):
    t, d = xt.shape
    tm = min(tm, seq_len)
    w = d // 2
    wa = w_in[:, :5 * w].astype(BF16)
    wg = w_in[:, 5 * w:].astype(BF16)
    heads = w // MOBA_HEAD_DIM
    e64 = jnp.kron(jnp.eye(heads, dtype=F32), jnp.ones((MOBA_HEAD_DIM, MOBA_HEAD_DIM), F32)).astype(BF16)
    half = MOBA_HEAD_DIM // 2
    inv = ROPE_THETA ** (-jnp.arange(half, dtype=F32) / half)
    ang = jnp.arange(seq_len, dtype=F32)[:, None] * inv[None, :]
    cos = jnp.tile(jnp.cos(ang), (1, LANES // half))
    sin = jnp.tile(jnp.concatenate([-jnp.sin(ang), jnp.sin(ang)], axis=1), (1, LANES // MOBA_HEAD_DIM))
    nt = seq_len // tm
    row = lambda i: (i, 0)
    const = lambda i: (0, 0)
    out_w = jax.ShapeDtypeStruct((t, w), BF16)
    return pl.pallas_call(
        _inproj_kernel,
        grid=(t // tm,),
        in_specs=[
            pl.BlockSpec((tm, d), row),
            pl.BlockSpec((1, d), const),
            pl.BlockSpec((d, 5 * w), const, pipeline_mode=pl.Buffered(1)),
            pl.BlockSpec((d, N_BRANCH * d), const, pipeline_mode=pl.Buffered(1)),
            pl.BlockSpec((w, w), const, pipeline_mode=pl.Buffered(1)),
            pl.BlockSpec((1, w), const),
            pl.BlockSpec((1, w), const),
            pl.BlockSpec((1, w // X_HEADS), const),
            pl.BlockSpec((tm, LANES), lambda i: (i % nt, 0)),
            pl.BlockSpec((tm, LANES), lambda i: (i % nt, 0)),
        ],
        out_specs=[pl.BlockSpec((tm, w), row), pl.BlockSpec((tm, w), row), pl.BlockSpec((tm, 2 * w), row),
                   pl.BlockSpec((tm, w), row), pl.BlockSpec((tm, N_BRANCH * d), row),
                   pl.BlockSpec((heads, tm, LANES), lambda i: (0, i, 0))],
        out_shape=[jax.ShapeDtypeStruct((t, w), F32), out_w, jax.ShapeDtypeStruct((t, 2 * w), BF16), out_w,
                   jax.ShapeDtypeStruct((t, N_BRANCH * d), BF16),
                   jax.ShapeDtypeStruct((heads, t, LANES), F32)],
        compiler_params=_params("parallel"),
        name="in_proj",
    )(xt, g_mix.reshape(1, d), wa, wg, e64,
      jnp.tile(g_q, heads).reshape(1, w), jnp.tile(g_k, heads).reshape(1, w), g_cq.reshape(1, -1), cos, sin)


def _s5_matrices(lam_re, lam_im, log_dt, b_re, b_im, c_re, c_im, n_chunks):
    hp = lax.Precision.HIGHEST
    c = S5_CHUNK
    dt = jnp.exp(log_dt)[:, None]
    mag = jnp.exp(lam_re * dt)
    ar = mag * jnp.cos(lam_im * dt)
    ai = mag * jnp.sin(lam_im * dt)
    nr = ar - 1.0
    den = lam_re * lam_re + lam_im * lam_im
    cr = (nr * lam_re + ai * lam_im) / den
    ci = (ai * lam_re - nr * lam_im) / den
    bbr = cr[..., None] * b_re - ci[..., None] * b_im
    bbi = cr[..., None] * b_im + ci[..., None] * b_re

    def power(n):
        nf = n.astype(F32)[None, :, None]
        m = jnp.exp((lam_re * dt)[:, None, :] * nf)
        th = (lam_im * dt)[:, None, :] * nf
        return m * jnp.cos(th), m * jnp.sin(th)

    pr, pi = power(jnp.arange(c + 1))
    kbr = pr[..., None] * bbr[:, None] - pi[..., None] * bbi[:, None]
    kbi = pr[..., None] * bbi[:, None] + pi[..., None] * bbr[:, None]
    kk = (jnp.einsum('ghp,gtpc->gthc', c_re, kbr, precision=hp)
          - jnp.einsum('ghp,gtpc->gthc', c_im, kbi, precision=hp))
    tq = jnp.arange(c)
    g = kk.shape[0]
    rev = c - 1 - tq
    w_in = jnp.concatenate([kbr[:, rev].transpose(0, 1, 3, 2), kbi[:, rev].transpose(0, 1, 3, 2)], axis=-1)
    w_in = w_in.reshape(g, c * SSM_GROUP, 2 * SSM_STATE)
    prn, pin = pr[:, 1:], pi[:, 1:]
    wo_r = c_re[:, None] * prn[:, :, None, :] - c_im[:, None] * pin[:, :, None, :]
    wo_i = -c_re[:, None] * pin[:, :, None, :] - c_im[:, None] * prn[:, :, None, :]
    w_out = jnp.concatenate([wo_r, wo_i], axis=-1).transpose(0, 3, 1, 2).reshape(g, 2 * SSM_STATE, c * SSM_GROUP)
    n_steps = max(1, int(math.ceil(math.log2(n_chunks))))
    qr, qi = power(c * (2 ** jnp.arange(n_steps)))
    pa = jnp.concatenate([qr, qr], axis=-1)
    pb = jnp.concatenate([-qi, qi], axis=-1)
    return kk.astype(BF16), w_in.astype(BF16), w_out.astype(BF16), pa, pb


S5_SET = LANES // SSM_GROUP
S5_ROWS = 256


def _s5_kernel(u_ref, toep_ref, win_ref, wout_ref, pa_ref, pb_ref, y_ref, s_ref):
    c = S5_CHUNK
    nc = u_ref.shape[0] // c
    rb = min(S5_ROWS, nc)

    def chunk_rows(r0):
        return jnp.concatenate([u_ref[pl.ds(r0 * c + t, rb, stride=c), :] for t in range(c)], axis=1).astype(BF16)

    for blk in range(nc // rb):
        s_ref[blk * rb:(blk + 1) * rb, :] = _dot(chunk_rows(blk * rb), win_ref[0])
    row = lax.broadcasted_iota(jnp.int32, (nc, LANES), 0)
    n_steps = pa_ref.shape[1]
    for g in range(S5_SET):
        lanes = slice(g * LANES, (g + 1) * LANES)
        s = s_ref[:, lanes]
        for k in range(n_steps):
            sh = 1 << k
            if sh >= nc:
                break
            prev = jnp.where(row >= sh, pltpu.roll(s, sh, 0), 0.0)
            s = s + pa_ref[0, k:k + 1, lanes] * prev + pb_ref[0, k:k + 1, lanes] * pltpu.roll(prev, SSM_STATE, 1)
        s_ref[:, lanes] = jnp.where(row >= 1, pltpu.roll(s, 1, 0), 0.0)
    for blk in range(nc // rb):
        rows = slice(blk * rb, (blk + 1) * rb)
        a = chunk_rows(blk * rb)
        carried = _dot(s_ref[rows, :].astype(BF16), wout_ref[0])
        for t in range(c):
            y = _dot(a[:, :(t + 1) * LANES], toep_ref[0, (c - 1 - t) * LANES:, :])
            y_ref[pl.ds(blk * rb * c + t, rb, stride=c), :] = y + carried[:, t * LANES:(t + 1) * LANES]


def _s5_block_diag(mats):
    kk, w_in, w_out, pa, pb = mats
    g, c, h, p2 = kk.shape[0], S5_CHUNK, SSM_GROUP, 2 * SSM_STATE
    ns = g // S5_SET
    eye = jnp.eye(S5_SET, dtype=kk.dtype)
    lag_bd = jnp.einsum('sglhc,gk->slgckh', kk.reshape(ns, S5_SET, c + 1, h, h), eye).reshape(ns, c + 1, LANES, LANES)
    toep_bd = lag_bd[:, c - 1::-1].reshape(ns, c * LANES, LANES)

    def rows_to_states(w):
        wc = w.reshape(ns, S5_SET, c, h, p2).transpose(0, 2, 1, 3, 4).reshape(ns, c, LANES, p2)
        same = (jnp.arange(LANES)[:, None] // h) == (jnp.arange(S5_SET * p2)[None, :] // p2)
        return jnp.where(same, jnp.concatenate([wc] * S5_SET, axis=-1), 0).reshape(ns, c * LANES, S5_SET * p2)

    win_bd = rows_to_states(w_in)
    wout_bd = rows_to_states(w_out.transpose(0, 2, 1)).transpose(0, 2, 1)
    k = pa.shape[1]
    lanes_of = lambda x: x.reshape(ns, S5_SET, k, p2).transpose(0, 2, 1, 3).reshape(ns, k, S5_SET * p2)
    return toep_bd, win_bd, wout_bd, lanes_of(pa), lanes_of(pb)


def _s5(u, mats, bsz, seq_len):
    toep, w_in, w_out, pa, pb = _s5_block_diag(mats)
    ns = toep.shape[0]
    nc = seq_len // S5_CHUNK
    per_set = lambda s, b: (s, 0, 0)
    once = dict(pipeline_mode=pl.Buffered(1))
    io_spec = pl.BlockSpec((seq_len, LANES), lambda s, b: (b, s), **once)
    return pl.pallas_call(
        _s5_kernel,
        grid=(ns, bsz),
        in_specs=[
            io_spec,
            pl.BlockSpec((1,) + toep.shape[1:], per_set, **once),
            pl.BlockSpec((1,) + w_in.shape[1:], per_set, **once),
            pl.BlockSpec((1,) + w_out.shape[1:], per_set, **once),
            pl.BlockSpec((1,) + pa.shape[1:], per_set),
            pl.BlockSpec((1,) + pb.shape[1:], per_set),
        ],
        out_specs=io_spec,
        out_shape=jax.ShapeDtypeStruct(u.shape, F32),
        scratch_shapes=[pltpu.VMEM((nc, S5_SET * 2 * SSM_STATE), F32)],
        compiler_params=_params("arbitrary", "arbitrary"),
        name="s5_scan",
    )(u, toep, w_in, w_out, pa, pb)


def _moba_select_kernel(q_ref, k_ref, tri_ref, sel_ref, cnt_out_ref, km_ref, cnt_ref):
    h = pl.program_id(1)
    t = pl.program_id(2)
    ts = q_ref.shape[0]
    nbk = MOBA_MAX_BLOCKS

    @pl.when(t == 0)
    def _():
        kk = k_ref[...].astype(F32)
        nb = kk.shape[0] // MOBA_BLOCK
        km = jnp.sum(kk.reshape(nb, MOBA_BLOCK, LANES), axis=1) * (1.0 / MOBA_BLOCK)
        if nb < nbk:
            km = jnp.concatenate([km, jnp.zeros((nbk - nb, LANES), F32)], axis=0)
        lane = lax.broadcasted_iota(jnp.int32, (nbk, LANES), 1)
        km_ref[...] = jnp.where((lane // MOBA_HEAD_DIM) == (h % 2), km, 0.0).astype(BF16)
        cnt_ref[...] = jnp.zeros(cnt_ref.shape, F32)

    gate = _dot_nt(km_ref[...], q_ref[...])
    blk = lax.broadcasted_iota(jnp.int32, gate.shape, 0)
    qblk = (t * ts + lax.broadcasted_iota(jnp.int32, gate.shape, 1)) // MOBA_BLOCK
    g = jnp.where(blk < qblk, gate, -jnp.inf)
    selected = jnp.zeros(gate.shape, jnp.bool_)
    picks = []
    for _ in range(MOBA_TOPK):
        mx = jnp.max(g, axis=0, keepdims=True)
        idx = jnp.min(jnp.where(g == mx, blk, nbk), axis=0, keepdims=True)
        hit = blk == idx
        ok = (idx[0:1] < qblk[0:1]) & (mx > -jnp.inf)
        picks.append((hit, idx, ok))
        selected = selected | (hit & ok)
        g = jnp.where(hit, -jnp.inf, g)
    sel01 = jnp.where(selected, 1.0, 0.0)
    chunks = [sel01[:, c * LANES:(c + 1) * LANES] for c in range(ts // LANES)]
    within = _dot(jnp.concatenate(chunks, axis=0).astype(BF16), tri_ref[...])
    base = cnt_ref[...]
    pieces = []
    for c, chunk in enumerate(chunks):
        pieces.append(within[c * nbk:(c + 1) * nbk] + base)
        base = base + jnp.sum(chunk, axis=1, keepdims=True)
    prefix = jnp.concatenate(pieces, axis=1)
    rows = [jnp.where(ok, idx, -1) for (_, idx, ok) in picks]
    rows += [jnp.sum(jnp.where(hit, prefix, 0.0), axis=0, keepdims=True).astype(jnp.int32) for (hit, _, _) in picks]
    rows += [jnp.zeros((1, ts), jnp.int32)] * (8 - 2 * MOBA_TOPK)
    sel_ref[0, 0] = jnp.concatenate(rows, axis=0)
    cnt_ref[...] = base
    cnt_out_ref[0, 0] = base


def _moba_select(q2, kv, bsz, seq_len, ts=1024):
    ts = min(ts, seq_len)
    heads = q2.shape[1] // MOBA_HEAD_DIM
    nt = seq_len // ts
    tri = (jnp.arange(LANES)[:, None] < jnp.arange(LANES)[None, :]).astype(BF16)
    return pl.pallas_call(
        _moba_select_kernel,
        grid=(bsz, heads, nt),
        in_specs=[
            pl.BlockSpec((ts, LANES), lambda b, h, t: (b * nt + t, h // 2)),
            pl.BlockSpec((seq_len, LANES), lambda b, h, t: (b, 2 * (h // 2))),
            pl.BlockSpec((LANES, LANES), lambda b, h, t: (0, 0)),
        ],
        out_specs=[pl.BlockSpec((1, 1, 8, ts), lambda b, h, t: (b, h, 0, t)),
                   pl.BlockSpec((1, 1, MOBA_MAX_BLOCKS, LANES), lambda b, h, t: (b, h, 0, 0))],
        out_shape=[jax.ShapeDtypeStruct((bsz, heads, 8, seq_len), jnp.int32),
                   jax.ShapeDtypeStruct((bsz, heads, MOBA_MAX_BLOCKS, LANES), F32)],
        scratch_shapes=[pltpu.VMEM((MOBA_MAX_BLOCKS, LANES), BF16), pltpu.VMEM((MOBA_MAX_BLOCKS, LANES), F32)],
        compiler_params=_params("parallel", "arbitrary", "arbitrary"),
        name="moba_select",
    )(q2, kv, tri)


MOBA_DIAG_BLOCKS = 4


def _moba_diag_kernel(q_ref, kv_ref, o_ref, lse_ref):
    lane = lax.broadcasted_iota(jnp.int32, (MOBA_BLOCK, LANES), 1)
    is_a = lane < MOBA_HEAD_DIM
    r = lax.broadcasted_iota(jnp.int32, (MOBA_BLOCK, MOBA_BLOCK), 0)
    c = lax.broadcasted_iota(jnp.int32, (MOBA_BLOCK, MOBA_BLOCK), 1)
    for u in range(MOBA_DIAG_BLOCKS):
        rows = pl.ds(u * MOBA_BLOCK, MOBA_BLOCK)
        q = q_ref[rows, :]
        kb = kv_ref[rows, :LANES]
        vb = kv_ref[rows, LANES:]
        pvs, ms = [], []
        for own in (is_a, jnp.logical_not(is_a)):
            s = jnp.where(c <= r, _dot_nt(jnp.where(own, q, jnp.zeros_like(q)), kb), NEG_BIG)
            m = jnp.max(s, axis=1, keepdims=True)
            p = jnp.exp(s - m).astype(BF16)
            pvs.append(_dot(p, jnp.where(own, vb, 1.0)))
            ms.append(m)
        num = jnp.where(is_a, pvs[0], pvs[1])
        den = pltpu.roll(jnp.where(is_a, pvs[1], pvs[0]), MOBA_HEAD_DIM, 1)
        o_ref[rows, :] = num / den
        lse_ref[rows, :] = jnp.where(is_a, ms[0], ms[1]) + jnp.log(den)


def _moba_diag(q2, kv, bsz, seq_len):
    npair = q2.shape[1] // LANES
    rows = MOBA_BLOCK * MOBA_DIAG_BLOCKS
    nb = seq_len // rows
    spec = pl.BlockSpec((rows, LANES), lambda b, p, i: (b * nb + i, p))
    out = jax.ShapeDtypeStruct(q2.shape, F32)
    return pl.pallas_call(
        _moba_diag_kernel,
        grid=(bsz, npair, nb),
        in_specs=[spec, pl.BlockSpec((rows, 2 * LANES), lambda b, p, i: (b * nb + i, p))],
        out_specs=[spec, spec],
        out_shape=[out, out],
        compiler_params=_params("parallel", "parallel", "parallel"),
        name="moba_diag",
    )(q2, kv)


MOBA_TILE = 256
MOBA_TILES_PER_STEP = 8


def _moba_grouped_kernel(tile_row_ref, tile_pair_ref, tile_half_ref, tile_real_ref, qd_ref, *refs):
    n = MOBA_TILES_PER_STEP
    kv_refs, o_ref = refs[:n], refs[n + 1]
    i = pl.program_id(0)
    lane = lax.broadcasted_iota(jnp.int32, (MOBA_TILE, LANES), 1)
    kv_lane = lax.broadcasted_iota(jnp.int32, (MOBA_BLOCK, LANES), 1)
    for u in range(n):
        tile = i * n + u
        rows = pl.ds(u * MOBA_TILE, MOBA_TILE)
        own = (lane // MOBA_HEAD_DIM) == tile_half_ref[tile]
        s = _dot_nt(qd_ref[rows, :].astype(BF16), kv_refs[u][:, :LANES])
        m = jnp.max(s, axis=1, keepdims=True)
        p = jnp.exp(s - m).astype(BF16)
        vb = jnp.where((kv_lane // MOBA_HEAD_DIM) == tile_half_ref[tile], kv_refs[u][:, LANES:], 1.0)
        pv = _dot(p, vb)
        row_sum = pltpu.roll(pv, MOBA_HEAD_DIM, 1)
        part = jnp.where(own, pv / row_sum, m + jnp.log(pv))
        o_ref[rows, :] = jnp.where(tile_real_ref[tile] > 0, part, NEG_BIG)


def _moba_grouped(qd, kv, tile_row, tile_pair, tile_half, tile_real, run_after):
    n = MOBA_TILES_PER_STEP
    n_tiles = qd.shape[0] // MOBA_TILE

    def kv_spec(u):
        return pl.BlockSpec((MOBA_BLOCK, 2 * LANES), lambda i, tr, tp, th, tl: (tr[i * n + u], tp[i * n + u]))

    grid_spec = pltpu.PrefetchScalarGridSpec(
        num_scalar_prefetch=4,
        grid=(n_tiles // n,),
        in_specs=[pl.BlockSpec((n * MOBA_TILE, LANES), lambda i, tr, tp, th, tl: (i, 0))]
        + [kv_spec(u) for u in range(n)]
        + [pl.BlockSpec((8, LANES), lambda i, tr, tp, th, tl: (0, 0))],
        out_specs=pl.BlockSpec((n * MOBA_TILE, LANES), lambda i, tr, tp, th, tl: (i, 0)),
    )
    return pl.pallas_call(
        _moba_grouped_kernel,
        grid_spec=grid_spec,
        out_shape=jax.ShapeDtypeStruct(qd.shape, F32),
        compiler_params=_params("arbitrary"),
        name="moba_grouped",
    )(tile_row, tile_pair, tile_half, tile_real, qd, *([kv] * n), run_after)


def _moba_combine_kernel(od_ref, lsed_ref, *refs):
    g_refs, o_ref = refs[:-1], refs[-1]
    lane = lax.broadcasted_iota(jnp.int32, od_ref.shape, 1)
    is_a = lane < MOBA_HEAD_DIM
    parts = [(od_ref[...], lsed_ref[...])]
    for s in range(MOBA_TOPK):
        xa = g_refs[s][0, 0, 0]
        xb = g_refs[MOBA_TOPK + s][0, 0, 0]
        o = jnp.where(is_a, xa, xb)
        lse = pltpu.roll(jnp.where(is_a, xb, xa), MOBA_HEAD_DIM, 1)
        parts.append((o, lse))
    m = parts[0][1]
    for _, lse in parts[1:]:
        m = jnp.maximum(m, lse)
    num = jnp.zeros(od_ref.shape, F32)
    den = jnp.zeros(od_ref.shape, F32)
    for o, lse in parts:
        w = jnp.exp(lse - m)
        num = num + w * o
        den = den + w
    o_ref[...] = (num / den).astype(BF16)


def _moba_combine(od, lsed, gath, bsz, seq_len, tm=1024):
    tm = min(tm, seq_len)
    npair = od.shape[1] // LANES
    nt = seq_len // tm
    spec = pl.BlockSpec((tm, LANES), lambda b, p, i: (b * nt + i, p))

    def g_spec(e, s):
        return pl.BlockSpec((1, 1, 1, tm, LANES), lambda b, p, i: (b, 2 * p + e, s, i, 0))

    return pl.pallas_call(
        _moba_combine_kernel,
        grid=(bsz, npair, nt),
        in_specs=[spec, spec] + [g_spec(e, s) for e in range(2) for s in range(MOBA_TOPK)],
        out_specs=spec,
        out_shape=jax.ShapeDtypeStruct(od.shape, BF16),
        compiler_params=_params("parallel", "parallel", "parallel"),
        name="moba_combine",
    )(od, lsed, *([gath] * (2 * MOBA_TOPK)))


SC_WINDOW = 128
SC_CORES = 2
SC_SUBCORES = 16


def _sc_mesh():
    return plsc.VectorSubcoreMesh(core_axis_name="core", subcore_axis_name="subcore")


def _sc_gather_rows(table, idx):
    n = idx.shape[0]
    d = table.shape[1]
    window = SC_WINDOW
    assert n % (window * SC_CORES * SC_SUBCORES) == 0
    per_core = n // window // SC_CORES

    @functools.partial(pl.kernel, out_type=jax.ShapeDtypeStruct((n, d), table.dtype), mesh=_sc_mesh())
    def gather_kernel(x_hbm, i_hbm, o_hbm):
        base = lax.axis_index("core") * per_core

        def body(i_vmem, o_vmem):
            pltpu.sync_copy(x_hbm.at[i_vmem.at[0]], o_vmem)

        pltpu.emit_pipeline(
            body,
            grid=(per_core,),
            in_specs=[pl.BlockSpec((1, window), index_map=lambda i: (0, base + i))],
            out_specs=[pl.BlockSpec((window, d), index_map=lambda i: (base + i, 0))],
            core_axis_name="subcore",
            dimension_semantics=(pltpu.PARALLEL,),
            trace_scopes=False,
        )(i_hbm, o_hbm)

    return gather_kernel(table, idx.reshape(1, n))


def _sc_scatter_rows(rows, dest, n_out, repeat):
    n_src, d = rows.shape
    n = dest.shape[0]
    assert n == repeat * n_src
    window = SC_WINDOW
    assert n % (window * SC_CORES * SC_SUBCORES) == 0
    per_core = n // window // SC_CORES
    src_windows = n_src // window

    @functools.partial(pl.kernel, out_type=jax.ShapeDtypeStruct((n_out, d), rows.dtype), mesh=_sc_mesh())
    def scatter_kernel(x_hbm, i_hbm, o_hbm):
        base = lax.axis_index("core") * per_core

        def body(x_vmem, i_vmem):
            pltpu.sync_copy(x_vmem, o_hbm.at[i_vmem.at[0]])

        pltpu.emit_pipeline(
            body,
            grid=(per_core,),
            in_specs=[pl.BlockSpec((window, d), index_map=lambda i: ((base + i) % src_windows, 0)),
                      pl.BlockSpec((1, window), index_map=lambda i: (0, base + i))],
            out_specs=[],
            core_axis_name="subcore",
            dimension_semantics=(pltpu.PARALLEL,),
            trace_scopes=False,
        )(x_hbm, i_hbm)

    return scatter_kernel(rows, dest.reshape(1, n))


def _moba_dispatch(q2, qh, kv, bsz, seq_len):
    heads = q2.shape[1] // MOBA_HEAD_DIM
    nbk = MOBA_MAX_BLOCKS
    sel, cnt = _moba_select(q2, kv, bsz, seq_len)
    counts = cnt[..., 0].astype(jnp.int32).reshape(bsz * heads * nbk)
    pcounts = ((counts + MOBA_TILE - 1) // MOBA_TILE) * MOBA_TILE
    pends = jnp.cumsum(pcounts)
    pstarts = (pends - pcounts).reshape(bsz, heads, 1, 1, nbk)
    n_items = bsz * heads * seq_len * MOBA_TOPK
    step_rows = MOBA_TILE * MOBA_TILES_PER_STEP
    n_rows = -(-(n_items + bsz * heads * nbk * MOBA_TILE) // step_rows) * step_rows
    n_null = step_rows
    idx = sel[:, :, 0:MOBA_TOPK, :]
    rank = sel[:, :, MOBA_TOPK:2 * MOBA_TOPK, :]
    start = jnp.sum(jnp.where(idx[..., None] == jnp.arange(nbk), pstarts, 0), axis=-1)
    null_row = n_rows + jnp.arange(seq_len, dtype=jnp.int32) % n_null
    dest = jnp.where(idx >= 0, start + rank, null_row)
    n_tiles = (n_rows + n_null) // MOBA_TILE
    n_groups = bsz * heads * nbk
    tile_start = jnp.arange(n_tiles, dtype=jnp.int32) * MOBA_TILE
    tile_g = jnp.minimum(jnp.sum((pends[None, :] <= tile_start[:, None]).astype(jnp.int32), axis=1), n_groups - 1)
    tile_real = (tile_start < pends[-1]).astype(jnp.int32)
    tile_head = (tile_g // nbk) % heads
    tile_row = (tile_g // (heads * nbk)) * (seq_len // MOBA_BLOCK) + jnp.minimum(tile_g % nbk, seq_len // MOBA_BLOCK - 1)
    dest_by_slot = dest.transpose(2, 1, 0, 3).reshape(-1)
    qd = _sc_scatter_rows(qh.reshape(-1, LANES), dest_by_slot, n_rows + n_null, MOBA_TOPK)
    return qd, dest, (tile_row, tile_head // 2, tile_head % 2, tile_real)


def _moba_finish(qd, dest, tiles, od, lsed, kv, bsz, seq_len):
    heads = dest.shape[1]
    part = _moba_grouped(qd, kv, *tiles, lsed)
    gath = _sc_gather_rows(part, dest.reshape(-1)).reshape(bsz, heads, MOBA_TOPK, seq_len, LANES)
    return _moba_combine(od, lsed, gath, bsz, seq_len)


def _mem_kv_kernel(mem_ref, gmem_ref, w_ref, gck_ref, k_ref, v_ref):
    xf = mem_ref[0]
    ms = jnp.mean(xf * xf, axis=-1, keepdims=True)
    h = (xf * lax.rsqrt(ms + EPS) * gmem_ref[...]).astype(BF16)
    kv = _dot(h, w_ref[...])
    w = k_ref.shape[2]
    hd = w // X_HEADS
    for c in range(X_HEADS):
        chunk = kv[:, c * hd:(c + 1) * hd]
        cms = jnp.mean(chunk * chunk, axis=-1, keepdims=True)
        k_ref[0, :, c * hd:(c + 1) * hd] = (chunk * lax.rsqrt(cms + EPS) * gck_ref[...]).astype(BF16)
    v_ref[0] = kv[:, w:].astype(BF16)


def _mem_kv(mem, g_mem, w_kv_mem, g_ck):
    bsz, m, d = mem.shape
    w = w_kv_mem.shape[1] // 2
    const = lambda b: (0, 0)
    out = jax.ShapeDtypeStruct((bsz, m, w), BF16)
    return pl.pallas_call(
        _mem_kv_kernel,
        grid=(bsz,),
        in_specs=[pl.BlockSpec((1, m, d), lambda b: (b, 0, 0)), pl.BlockSpec((1, d), const),
                  pl.BlockSpec((d, 2 * w), const), pl.BlockSpec((1, w // X_HEADS), const)],
        out_specs=[pl.BlockSpec((1, m, w), lambda b: (b, 0, 0))] * 2,
        out_shape=[out, out],
        compiler_params=_params("parallel"),
        name="mem_kv",
    )(mem, g_mem.reshape(1, d), w_kv_mem.astype(BF16), g_ck.reshape(1, -1))


def _merge_kernel(x_ref, ys_ref, u_ref, dskip_ref, om_ref, xq_ref, kc_ref, vc_ref, g_ref,
                  wglu_ref, wmo_ref, wco_ref, wout_ref, gffn_ref, wr_ref, br_ref,
                  x1_ref, h2_ref, logit_ref):
    d = x_ref.shape[1]
    y = ys_ref[...].astype(F32) + dskip_ref[...] * u_ref[...].astype(F32)
    ge = 0.5 * y * (1.0 + jnp.tanh(math.sqrt(2.0 / math.pi) * (y + 0.044715 * (y * y * y))))
    z = _dot(ge.astype(BF16), wglu_ref[...])
    merged = g_ref[:, 0:d].astype(F32) * (z[:, :d] * _sigmoid(z[:, d:]))
    merged = merged + g_ref[:, d:2 * d].astype(F32) * _dot(om_ref[...], wmo_ref[...])
    w = xq_ref.shape[1]
    hd = w // X_HEADS
    heads = []
    for c in range(X_HEADS):
        s = _dot_nt(xq_ref[:, c * hd:(c + 1) * hd], kc_ref[0, :, c * hd:(c + 1) * hd]) * (hd ** -0.5)
        p = jnp.exp(s - jnp.max(s, axis=1, keepdims=True))
        p = p / jnp.sum(p, axis=1, keepdims=True)
        heads.append(_dot(p.astype(BF16), vc_ref[0, :, c * hd:(c + 1) * hd]))
    oc = jnp.concatenate(heads, axis=1).astype(BF16)
    merged = merged + g_ref[:, 2 * d:3 * d].astype(F32) * _dot(oc, wco_ref[...])
    x1 = x_ref[...] + _dot(merged.astype(BF16), wout_ref[...])
    x1_ref[...] = x1
    ms = jnp.mean(x1 * x1, axis=-1, keepdims=True)
    h2 = (x1 * lax.rsqrt(ms + EPS) * gffn_ref[...]).astype(BF16)
    packed = _pack_bf16_pairs(h2.astype(F32))
    for j in range(h2_ref.shape[0]):
        h2_ref[j] = packed[:, j * LANES:(j + 1) * LANES]
    logit_ref[...] = _dot_nt(wr_ref[...], h2) + br_ref[...]


def _merge(xt, ys, u, d_skip, om, xqn, kc, vc, gates, w_glu, w_mo, w_co, w_out, g_ffn, w_router, b_router,
           seq_len, tm=512):
    t, d = xt.shape
    tm = min(tm, seq_len)
    w = d // 2
    m = kc.shape[1]
    ne = w_router.shape[1]
    nt = seq_len // tm
    row = lambda i: (i, 0)
    const = lambda i: (0, 0)
    per_b = lambda i: (i // nt, 0, 0)
    once = dict(pipeline_mode=pl.Buffered(1))
    return pl.pallas_call(
        _merge_kernel,
        grid=(t // tm,),
        in_specs=[
            pl.BlockSpec((tm, d), row), pl.BlockSpec((tm, w), row), pl.BlockSpec((tm, w), row),
            pl.BlockSpec((1, w), const), pl.BlockSpec((tm, w), row), pl.BlockSpec((tm, w), row),
            pl.BlockSpec((1, m, w), per_b), pl.BlockSpec((1, m, w), per_b),
            pl.BlockSpec((tm, N_BRANCH * d), row),
            pl.BlockSpec((w, 2 * d), const, **once), pl.BlockSpec((w, d), const, **once),
            pl.BlockSpec((w, d), const, **once), pl.BlockSpec((d, d), const, **once), pl.BlockSpec((1, d), const),
            pl.BlockSpec((ne, d), const), pl.BlockSpec((ne, 1), const),
        ],
        out_specs=[pl.BlockSpec((tm, d), row), pl.BlockSpec((d // 2 // LANES, tm, LANES), lambda i: (0, i, 0)),
                   pl.BlockSpec((ne, tm), lambda i: (0, i))],
        out_shape=[jax.ShapeDtypeStruct((t, d), F32), jax.ShapeDtypeStruct((d // 2 // LANES, t, LANES), jnp.uint32),
                   jax.ShapeDtypeStruct((ne, t), F32)],
        compiler_params=_params("parallel"),
        name="merge",
    )(xt, ys, u, d_skip.reshape(1, w), om, xqn, kc, vc, gates,
      w_glu.astype(BF16), w_mo.astype(BF16), w_co.astype(BF16), w_out.astype(BF16),
      g_ffn.reshape(1, d), w_router.T.astype(BF16), b_router.reshape(ne, 1))


def _moe_kernel(blk_e_ref, blk_used_ref, xs_ref, wgu_ref, bgu_ref, wd_ref, bd_ref, y_ref, wgu_bf, wd_bf):
    i = pl.program_id(0)
    prev = blk_e_ref[jnp.maximum(i - 1, 0)]

    @pl.when((i == 0) | (blk_e_ref[i] != prev))
    def _():
        wgu_bf[...] = wgu_ref[0].astype(BF16)
        wd_bf[...] = wd_ref[0].astype(BF16)

    @pl.when(blk_used_ref[i] > 0)
    def _():
        de = wd_bf.shape[0]
        words = jnp.concatenate([xs_ref[j] for j in range(xs_ref.shape[0])], axis=1)
        xs = jnp.concatenate(_unpack_bf16_pairs(words), axis=1).astype(BF16)
        gu = _dot(xs, wgu_bf[...]) + bgu_ref[0]
        gate = jnp.minimum(gu[:, :de], SWIGLU_LIMIT)
        up = jnp.clip(gu[:, de:], -SWIGLU_LIMIT, SWIGLU_LIMIT)
        act = gate * _sigmoid(SWIGLU_ALPHA * gate) * (up + 1.0)
        y = _dot(act.astype(BF16), wd_bf[...]) + bd_ref[0]
        packed = _pack_bf16_pairs(y.astype(BF16).astype(F32))
        for j in range(y_ref.shape[0]):
            y_ref[j] = packed[:, j * LANES:(j + 1) * LANES]

    @pl.when(blk_used_ref[i] == 0)
    def _():
        y_ref[...] = jnp.zeros(y_ref.shape, y_ref.dtype)


def _moe_experts(xs, blk_e, blk_used, w_gu, b_gu, w_down, b_down):
    slabs, p, _ = xs.shape
    d = 2 * slabs * LANES
    ne, _, de2 = w_gu.shape
    de = de2 // 2
    nblk = p // EXPERT_ROWS
    row_spec = pl.BlockSpec((slabs, EXPERT_ROWS, LANES), lambda i, e, n: (0, i, 0))
    grid_spec = pltpu.PrefetchScalarGridSpec(
        num_scalar_prefetch=2,
        grid=(nblk,),
        in_specs=[
            row_spec,
            pl.BlockSpec((1, d, de2), lambda i, e, n: (e[i], 0, 0)),
            pl.BlockSpec((1, 1, de2), lambda i, e, n: (e[i], 0, 0)),
            pl.BlockSpec((1, de, d), lambda i, e, n: (e[i], 0, 0)),
            pl.BlockSpec((1, 1, d), lambda i, e, n: (e[i], 0, 0)),
        ],
        out_specs=row_spec,
        scratch_shapes=[pltpu.VMEM((d, de2), BF16), pltpu.VMEM((de, d), BF16)],
    )
    return pl.pallas_call(
        _moe_kernel,
        grid_spec=grid_spec,
        out_shape=jax.ShapeDtypeStruct(xs.shape, jnp.uint32),
        compiler_params=_params("arbitrary"),
        name="moe_experts",
    )(blk_e, blk_used, xs, w_gu, b_gu.reshape(ne, 1, de2), w_down, b_down.reshape(ne, 1, d))


def _router_kernel(lg_ref, tri_ref, e_ref, w_ref, r_ref, cnt_out_ref, cnt_ref):
    t = pl.program_id(0)
    ne, ts = lg_ref.shape

    @pl.when(t == 0)
    def _():
        cnt_ref[...] = jnp.zeros(cnt_ref.shape, F32)

    g = lg_ref[...]
    eid = lax.broadcasted_iota(jnp.int32, g.shape, 0)
    selected = jnp.zeros(g.shape, jnp.bool_)
    picks = []
    for _ in range(TOPK_EXPERTS):
        mx = jnp.max(g, axis=0, keepdims=True)
        idx = jnp.min(jnp.where(g == mx, eid, ne), axis=0, keepdims=True)
        hit = eid == idx
        picks.append((hit, idx, mx))
        selected = selected | hit
        g = jnp.where(hit, -jnp.inf, g)
    exps = [jnp.exp(mx - picks[0][2]) for (_, _, mx) in picks]
    total = sum(exps)
    sel01 = jnp.where(selected, 1.0, 0.0).astype(BF16)
    prefix = _dot(sel01, tri_ref[...]) + jnp.tile(cnt_ref[...], (1, ts // LANES))
    pad_i = [jnp.zeros((1, ts), jnp.int32)] * (8 - TOPK_EXPERTS)
    e_ref[...] = jnp.concatenate([idx for (_, idx, _) in picks] + pad_i, axis=0)
    w_ref[...] = jnp.concatenate([e / total for e in exps] + [jnp.zeros((1, ts), F32)] * (8 - TOPK_EXPERTS), axis=0)
    r_ref[...] = jnp.concatenate(
        [jnp.sum(jnp.where(hit, prefix, 0.0), axis=0, keepdims=True).astype(jnp.int32) for (hit, _, _) in picks]
        + pad_i, axis=0)
    cnt_ref[...] = cnt_ref[...] + _dot(sel01, jnp.ones((ts, LANES), BF16))
    cnt_out_ref[...] = cnt_ref[...]


def _router(logits_t, ts=1024):
    ne, t = logits_t.shape
    ts = min(ts, t)
    tri = (jnp.arange(ts)[:, None] < jnp.arange(ts)[None, :]).astype(BF16)
    rows = pl.BlockSpec((8, ts), lambda i: (0, i))
    return pl.pallas_call(
        _router_kernel,
        grid=(t // ts,),
        in_specs=[pl.BlockSpec((ne, ts), lambda i: (0, i)), pl.BlockSpec((ts, ts), lambda i: (0, 0))],
        out_specs=[rows, rows, rows, pl.BlockSpec((ne, LANES), lambda i: (0, 0))],
        out_shape=[jax.ShapeDtypeStruct((8, t), jnp.int32), jax.ShapeDtypeStruct((8, t), F32),
                   jax.ShapeDtypeStruct((8, t), jnp.int32), jax.ShapeDtypeStruct((ne, LANES), F32)],
        scratch_shapes=[pltpu.VMEM((ne, LANES), F32)],
        compiler_params=_params("arbitrary"),
        name="moe_router",
    )(logits_t, tri)


def _moe_mix_kernel(x1_ref, w_ref, pk_ref, o_ref):
    slabs = pk_ref.shape[0]
    wts = w_ref[...]
    lo = [jnp.zeros((x1_ref.shape[0], LANES), F32)] * slabs
    hi = [jnp.zeros((x1_ref.shape[0], LANES), F32)] * slabs
    for k in range(TOPK_EXPERTS):
        wk = wts[:, k:k + 1]
        for j in range(slabs):
            a, b = _unpack_bf16_pairs(pk_ref[j, k])
            lo[j] = lo[j] + wk * a
            hi[j] = hi[j] + wk * b
    o_ref[...] = x1_ref[...] + jnp.concatenate(lo + hi, axis=1)


def _moe_mix(x1, weights_tk, picked, tm=512):
    t, d = x1.shape
    slabs = picked.shape[0]
    return pl.pallas_call(
        _moe_mix_kernel,
        grid=(t // tm,),
        in_specs=[pl.BlockSpec((tm, d), lambda i: (i, 0)),
                  pl.BlockSpec((tm, TOPK_EXPERTS), lambda i: (i, 0)),
                  pl.BlockSpec((slabs, TOPK_EXPERTS, tm, LANES), lambda i: (0, 0, i, 0))],
        out_specs=pl.BlockSpec((tm, d), lambda i: (i, 0)),
        out_shape=jax.ShapeDtypeStruct((t, d), F32),
        compiler_params=_params("parallel"),
        name="moe_mix",
    )(x1, weights_tk, picked)


def _moe_ffn(x1, h2, logits_t, w_gu, b_gu, w_down, b_down):
    t, d = x1.shape
    tk = t * TOPK_EXPERTS
    e8, w8, r8, cnt = _router(logits_t)
    top_e, weights, rank = e8[:TOPK_EXPERTS], w8[:TOPK_EXPERTS], r8[:TOPK_EXPERTS]
    counts = cnt[:, 0].astype(jnp.int32)
    pcounts = ((counts + EXPERT_ROWS - 1) // EXPERT_ROWS) * EXPERT_ROWS
    pends = jnp.cumsum(pcounts)
    pstarts = pends - pcounts
    is_e = top_e[..., None] == jnp.arange(N_EXPERTS, dtype=jnp.int32)
    dest_by_k = jnp.sum(jnp.where(is_e, pstarts, 0), axis=-1) + rank
    nblk = -(-tk // EXPERT_ROWS) + N_EXPERTS
    blk_start = jnp.arange(nblk, dtype=jnp.int32) * EXPERT_ROWS
    blk_e = jnp.minimum(jnp.sum((pends[None, :] <= blk_start[:, None]).astype(jnp.int32), axis=1), N_EXPERTS - 1)
    blk_used = (blk_start < pends[-1]).astype(jnp.int32)
    slabs = d // 2 // LANES
    p = nblk * EXPERT_ROWS
    slab_off = jnp.arange(slabs, dtype=jnp.int32) * p
    dest_kst = (dest_by_k[:, None, :] + slab_off[None, :, None]).reshape(-1)
    dest_skt = (dest_by_k[None, :, :] + slab_off[:, None, None]).reshape(-1)
    xs = _sc_scatter_rows(h2.reshape(slabs * t, LANES), dest_kst, slabs * p, TOPK_EXPERTS)
    ys = _moe_experts(xs.reshape(slabs, p, LANES), blk_e, blk_used, w_gu, b_gu, w_down, b_down)
    picked = _sc_gather_rows(ys.reshape(slabs * p, LANES), dest_skt).reshape(slabs, TOPK_EXPERTS, t, LANES)
    return _moe_mix(x1, weights.T, picked)


def kernel(x, mem, g_mix, w_in, lam_re, lam_im, log_dt, b_re, b_im, c_re, c_im, d_skip, w_glu, g_q, g_k, w_moba_out, g_mem, w_kv_mem, g_cq, g_ck, w_cross_out, w_out, g_ffn, w_router, b_router, w_gu, b_gu, w_down, b_down):
    bsz, seq_len, d = x.shape
    xt = x.reshape(bsz * seq_len, d)
    for l in range(g_mix.shape[0]):
        u, q2, kv, xqn, gates, qh = _in_proj(xt, g_mix[l], w_in[l], g_q[l], g_k[l], g_cq[l], seq_len)
        qd, dest, tiles = _moba_dispatch(q2, qh, kv, bsz, seq_len)
        mats = _s5_matrices(lam_re[l], lam_im[l], log_dt[l], b_re[l], b_im[l], c_re[l], c_im[l],
                            seq_len // S5_CHUNK)
        ys = _s5(u, mats, bsz, seq_len)
        od, lsed = _moba_diag(q2, kv, bsz, seq_len)
        kc, vc = _mem_kv(mem, g_mem[l], w_kv_mem[l], g_ck[l])
        om = _moba_finish(qd, dest, tiles, od, lsed, kv, bsz, seq_len)
        x1, h2, logits = _merge(xt, ys, u, d_skip[l], om, xqn, kc, vc, gates, w_glu[l], w_moba_out[l],
                                w_cross_out[l], w_out[l], g_ffn[l], w_router[l], b_router[l], seq_len)
        xt = _moe_ffn(x1, h2, logits, w_gu[l], b_gu[l], w_down[l], b_down[l])
    return xt.reshape(bsz, seq_len, d)
```

```python
import functools
import math

import jax
import jax.numpy as jnp
from jax import lax
from jax.experimental import pallas as pl
from jax.experimental.pallas import tpu as pltpu
from jax.experimental.pallas import tpu_sc as plsc

F32 = jnp.float32
BF16 = jnp.bfloat16

EPS = 1e-6
N_BRANCH = 3
SSM_GROUP = 16
SSM_STATE = 64
S5_CHUNK = 16
MOBA_HEAD_DIM = 64
MOBA_BLOCK = 256
MOBA_TOPK = 3
MOBA_MAX_BLOCKS = 64
ROPE_THETA = 10000.0
X_HEADS = 4
N_EXPERTS = 32
TOPK_EXPERTS = 4
SWIGLU_LIMIT = 7.0
SWIGLU_ALPHA = 1.702
EXPERT_ROWS = 512
NEG_BIG = -1e30
LANES = 128
VMEM_LIMIT_BYTES = 56 * 1024 * 1024


def _params(*sem):
    return pltpu.CompilerParams(dimension_semantics=sem, vmem_limit_bytes=VMEM_LIMIT_BYTES)


def _sigmoid(x):
    return 1.0 / (1.0 + jnp.exp(-x))


def _dot(a, b):
    return jnp.dot(a, b, preferred_element_type=F32)


def _pack_bf16_pairs(x):
    n = x.shape[1] // 2
    lo = lax.bitcast_convert_type(x[:, :n], jnp.uint32) >> 16
    hi = lax.bitcast_convert_type(x[:, n:], jnp.uint32) & jnp.uint32(0xFFFF0000)
    return lo | hi


def _unpack_bf16_pairs(w):
    lo = lax.bitcast_convert_type(w << 16, F32)
    hi = lax.bitcast_convert_type(w & jnp.uint32(0xFFFF0000), F32)
    return lo, hi


def _dot_nt(a, b):
    return lax.dot_general(a, b, (((1,), (1,)), ((), ())), preferred_element_type=F32)


def _inproj_kernel(x_ref, gmix_ref, wa_ref, wg_ref, e64_ref, gq_ref, gk_ref, gcq_ref, cos_ref, sin_ref,
                   u_ref, q_ref, kv_ref, xq_ref, g_ref, qh_ref):
    xf = x_ref[...]
    ms = jnp.mean(xf * xf, axis=-1, keepdims=True)
    h = (xf * lax.rsqrt(ms + EPS) * gmix_ref[...]).astype(BF16)
    a = _dot(h, wa_ref[...])
    w = u_ref.shape[1]
    u_ref[...] = a[:, :w]

    cos = jnp.tile(cos_ref[...], (1, w // LANES))
    sin = jnp.tile(sin_ref[...], (1, w // LANES))
    lane = lax.broadcasted_iota(jnp.int32, (xf.shape[0], w), 1)
    first_half = (lane % MOBA_HEAD_DIM) < (MOBA_HEAD_DIM // 2)

    def qk_norm_rope(raw, g):
        ss = _dot((raw * raw).astype(BF16), e64_ref[...])
        n = raw * lax.rsqrt(ss * (1.0 / MOBA_HEAD_DIM) + EPS) * g
        rot = jnp.where(first_half,
                        pltpu.roll(n, w - MOBA_HEAD_DIM // 2, 1),
                        pltpu.roll(n, MOBA_HEAD_DIM // 2, 1))
        return n * cos + rot * sin

    q = qk_norm_rope(a[:, w:2 * w], gq_ref[...]) * (MOBA_HEAD_DIM ** -0.5)
    q_ref[...] = q.astype(BF16)
    in_a = lax.broadcasted_iota(jnp.int32, (xf.shape[0], LANES), 1) < MOBA_HEAD_DIM
    for p in range(w // LANES):
        pair = q[:, p * LANES:(p + 1) * LANES].astype(BF16).astype(F32)
        qh_ref[2 * p] = jnp.where(in_a, pair, 0.0)
        qh_ref[2 * p + 1] = jnp.where(in_a, 0.0, pair)
    k = qk_norm_rope(a[:, 2 * w:3 * w], gk_ref[...]).astype(BF16)
    for p in range(w // LANES):
        kv_ref[:, (2 * p) * LANES:(2 * p + 1) * LANES] = k[:, p * LANES:(p + 1) * LANES]
        kv_ref[:, (2 * p + 1) * LANES:(2 * p + 2) * LANES] = a[:, 3 * w + p * LANES:3 * w + (p + 1) * LANES].astype(BF16)

    xq = a[:, 4 * w:5 * w]
    hd = w // X_HEADS
    for c in range(X_HEADS):
        chunk = xq[:, c * hd:(c + 1) * hd]
        cms = jnp.mean(chunk * chunk, axis=-1, keepdims=True)
        xq_ref[:, c * hd:(c + 1) * hd] = (chunk * lax.rsqrt(cms + EPS) * gcq_ref[...]).astype(BF16)

    d = xf.shape[1]
    for c in range(N_BRANCH):
        z = _dot(h, wg_ref[:, c * d:(c + 1) * d])
        g_ref[:, c * d:(c + 1) * d] = _sigmoid(z).astype(BF16)


def _in_proj(xt, g_mix, w_in, g_q, g_k, g_cq, seq_len, tm=512):
    t, d = xt.shape
    tm = min(tm, seq_len)
    w = d // 2
    wa = w_in[:, :5 * w].astype(BF16)
    wg = w_in[:, 5 * w:].astype(BF16)
    heads = w // MOBA_HEAD_DIM
    e64 = jnp.kron(jnp.eye(heads, dtype=F32), jnp.ones((MOBA_HEAD_DIM, MOBA_HEAD_DIM), F32)).astype(BF16)
    half = MOBA_HEAD_DIM // 2
    inv = ROPE_THETA ** (-jnp.arange(half, dtype=F32) / half)
    ang = jnp.arange(seq_len, dtype=F32)[:, None] * inv[None, :]
    cos = jnp.tile(jnp.cos(ang), (1, LANES // half))
    sin = jnp.tile(jnp.concatenate([-jnp.sin(ang), jnp.sin(ang)], axis=1), (1, LANES // MOBA_HEAD_DIM))
    nt = seq_len // tm
    row = lambda i: (i, 0)
    const = lambda i: (0, 0)
    out_w = jax.ShapeDtypeStruct((t, w), BF16)
    return pl.pallas_call(
        _inproj_kernel,
        grid=(t // tm,),
        in_specs=[
            pl.BlockSpec((tm, d), row),
            pl.BlockSpec((1, d), const),
            pl.BlockSpec((d, 5 * w), const, pipeline_mode=pl.Buffered(1)),
            pl.BlockSpec((d, N_BRANCH * d), const, pipeline_mode=pl.Buffered(1)),
            pl.BlockSpec((w, w), const, pipeline_mode=pl.Buffered(1)),
            pl.BlockSpec((1, w), const),
            pl.BlockSpec((1, w), const),
            pl.BlockSpec((1, w // X_HEADS), const),
            pl.BlockSpec((tm, LANES), lambda i: (i % nt, 0)),
            pl.BlockSpec((tm, LANES), lambda i: (i % nt, 0)),
        ],
        out_specs=[pl.BlockSpec((tm, w), row), pl.BlockSpec((tm, w), row), pl.BlockSpec((tm, 2 * w), row),
                   pl.BlockSpec((tm, w), row), pl.BlockSpec((tm, N_BRANCH * d), row),
                   pl.BlockSpec((heads, tm, LANES), lambda i: (0, i, 0))],
        out_shape=[jax.ShapeDtypeStruct((t, w), F32), out_w, jax.ShapeDtypeStruct((t, 2 * w), BF16), out_w,
                   jax.ShapeDtypeStruct((t, N_BRANCH * d), BF16),
                   jax.ShapeDtypeStruct((heads, t, LANES), F32)],
        compiler_params=_params("parallel"),
        name="in_proj",
    )(xt, g_mix.reshape(1, d), wa, wg, e64,
      jnp.tile(g_q, heads).reshape(1, w), jnp.tile(g_k, heads).reshape(1, w), g_cq.reshape(1, -1), cos, sin)


def _s5_matrices(lam_re, lam_im, log_dt, b_re, b_im, c_re, c_im, n_chunks):
    hp = lax.Precision.HIGHEST
    c = S5_CHUNK
    dt = jnp.exp(log_dt)[:, None]
    mag = jnp.exp(lam_re * dt)
    ar = mag * jnp.cos(lam_im * dt)
    ai = mag * jnp.sin(lam_im * dt)
    nr = ar - 1.0
    den = lam_re * lam_re + lam_im * lam_im
    cr = (nr * lam_re + ai * lam_im) / den
    ci = (ai * lam_re - nr * lam_im) / den
    bbr = cr[..., None] * b_re - ci[..., None] * b_im
    bbi = cr[..., None] * b_im + ci[..., None] * b_re

    def power(n):
        nf = n.astype(F32)[None, :, None]
        m = jnp.exp((lam_re * dt)[:, None, :] * nf)
        th = (lam_im * dt)[:, None, :] * nf
        return m * jnp.cos(th), m * jnp.sin(th)

    pr, pi = power(jnp.arange(c + 1))
    kbr = pr[..., None] * bbr[:, None] - pi[..., None] * bbi[:, None]
    kbi = pr[..., None] * bbi[:, None] + pi[..., None] * bbr[:, None]
    kk = (jnp.einsum('ghp,gtpc->gthc', c_re, kbr, precision=hp)
          - jnp.einsum('ghp,gtpc->gthc', c_im, kbi, precision=hp))
    tq = jnp.arange(c)
    g = kk.shape[0]
    rev = c - 1 - tq
    w_in = jnp.concatenate([kbr[:, rev].transpose(0, 1, 3, 2), kbi[:, rev].transpose(0, 1, 3, 2)], axis=-1)
    w_in = w_in.reshape(g, c * SSM_GROUP, 2 * SSM_STATE)
    prn, pin = pr[:, 1:], pi[:, 1:]
    wo_r = c_re[:, None] * prn[:, :, None, :] - c_im[:, None] * pin[:, :, None, :]
    wo_i = -c_re[:, None] * pin[:, :, None, :] - c_im[:, None] * prn[:, :, None, :]
    w_out = jnp.concatenate([wo_r, wo_i], axis=-1).transpose(0, 3, 1, 2).reshape(g, 2 * SSM_STATE, c * SSM_GROUP)
    n_steps = max(1, int(math.ceil(math.log2(n_chunks))))
    qr, qi = power(c * (2 ** jnp.arange(n_steps)))
    pa = jnp.concatenate([qr, qr], axis=-1)
    pb = jnp.concatenate([-qi, qi], axis=-1)
    return kk.astype(BF16), w_in.astype(BF16), w_out.astype(BF16), pa, pb


S5_SET = LANES // SSM_GROUP
S5_ROWS = 256


def _s5_kernel(u_ref, toep_ref, win_ref, wout_ref, pa_ref, pb_ref, y_ref, s_ref):
    c = S5_CHUNK
    nc = u_ref.shape[0] // c
    rb = min(S5_ROWS, nc)

    def chunk_rows(r0):
        return jnp.concatenate([u_ref[pl.ds(r0 * c + t, rb, stride=c), :] for t in range(c)], axis=1).astype(BF16)

    for blk in range(nc // rb):
        s_ref[blk * rb:(blk + 1) * rb, :] = _dot(chunk_rows(blk * rb), win_ref[0])
    row = lax.broadcasted_iota(jnp.int32, (nc, LANES), 0)
    n_steps = pa_ref.shape[1]
    for g in range(S5_SET):
        lanes = slice(g * LANES, (g + 1) * LANES)
        s = s_ref[:, lanes]
        for k in range(n_steps):
            sh = 1 << k
            if sh >= nc:
                break
            prev = jnp.where(row >= sh, pltpu.roll(s, sh, 0), 0.0)
            s = s + pa_ref[0, k:k + 1, lanes] * prev + pb_ref[0, k:k + 1, lanes] * pltpu.roll(prev, SSM_STATE, 1)
        s_ref[:, lanes] = jnp.where(row >= 1, pltpu.roll(s, 1, 0), 0.0)
    for blk in range(nc // rb):
        rows = slice(blk * rb, (blk + 1) * rb)
        a = chunk_rows(blk * rb)
        carried = _dot(s_ref[rows, :].astype(BF16), wout_ref[0])
        for t in range(c):
            y = _dot(a[:, :(t + 1) * LANES], toep_ref[0, (c - 1 - t) * LANES:, :])
            y_ref[pl.ds(blk * rb * c + t, rb, stride=c), :] = y + carried[:, t * LANES:(t + 1) * LANES]


def _s5_block_diag(mats):
    kk, w_in, w_out, pa, pb = mats
    g, c, h, p2 = kk.shape[0], S5_CHUNK, SSM_GROUP, 2 * SSM_STATE
    ns = g // S5_SET
    eye = jnp.eye(S5_SET, dtype=kk.dtype)
    lag_bd = jnp.einsum('sglhc,gk->slgckh', kk.reshape(ns, S5_SET, c + 1, h, h), eye).reshape(ns, c + 1, LANES, LANES)
    toep_bd = lag_bd[:, c - 1::-1].reshape(ns, c * LANES, LANES)

    def rows_to_states(w):
        wc = w.reshape(ns, S5_SET, c, h, p2).transpose(0, 2, 1, 3, 4).reshape(ns, c, LANES, p2)
        same = (jnp.arange(LANES)[:, None] // h) == (jnp.arange(S5_SET * p2)[None, :] // p2)
        return jnp.where(same, jnp.concatenate([wc] * S5_SET, axis=-1), 0).reshape(ns, c * LANES, S5_SET * p2)

    win_bd = rows_to_states(w_in)
    wout_bd = rows_to_states(w_out.transpose(0, 2, 1)).transpose(0, 2, 1)
    k = pa.shape[1]
    lanes_of = lambda x: x.reshape(ns, S5_SET, k, p2).transpose(0, 2, 1, 3).reshape(ns, k, S5_SET * p2)
    return toep_bd, win_bd, wout_bd, lanes_of(pa), lanes_of(pb)


def _s5(u, mats, bsz, seq_len):
    toep, w_in, w_out, pa, pb = _s5_block_diag(mats)
    ns = toep.shape[0]
    nc = seq_len // S5_CHUNK
    per_set = lambda s, b: (s, 0, 0)
    once = dict(pipeline_mode=pl.Buffered(1))
    io_spec = pl.BlockSpec((seq_len, LANES), lambda s, b: (b, s), **once)
    return pl.pallas_call(
        _s5_kernel,
        grid=(ns, bsz),
        in_specs=[
            io_spec,
            pl.BlockSpec((1,) + toep.shape[1:], per_set, **once),
            pl.BlockSpec((1,) + w_in.shape[1:], per_set, **once),
            pl.BlockSpec((1,) + w_out.shape[1:], per_set, **once),
            pl.BlockSpec((1,) + pa.shape[1:], per_set),
            pl.BlockSpec((1,) + pb.shape[1:], per_set),
        ],
        out_specs=io_spec,
        out_shape=jax.ShapeDtypeStruct(u.shape, F32),
        scratch_shapes=[pltpu.VMEM((nc, S5_SET * 2 * SSM_STATE), F32)],
        compiler_params=_params("arbitrary", "arbitrary"),
        name="s5_scan",
    )(u, toep, w_in, w_out, pa, pb)


def _moba_select_kernel(q_ref, k_ref, tri_ref, sel_ref, cnt_out_ref, km_ref, cnt_ref):
    h = pl.program_id(1)
    t = pl.program_id(2)
    ts = q_ref.shape[0]
    nbk = MOBA_MAX_BLOCKS

    @pl.when(t == 0)
    def _():
        kk = k_ref[...].astype(F32)
        nb = kk.shape[0] // MOBA_BLOCK
        km = jnp.sum(kk.reshape(nb, MOBA_BLOCK, LANES), axis=1) * (1.0 / MOBA_BLOCK)
        if nb < nbk:
            km = jnp.concatenate([km, jnp.zeros((nbk - nb, LANES), F32)], axis=0)
        lane = lax.broadcasted_iota(jnp.int32, (nbk, LANES), 1)
        km_ref[...] = jnp.where((lane // MOBA_HEAD_DIM) == (h % 2), km, 0.0).astype(BF16)
        cnt_ref[...] = jnp.zeros(cnt_ref.shape, F32)

    gate = _dot_nt(km_ref[...], q_ref[...])
    blk = lax.broadcasted_iota(jnp.int32, gate.shape, 0)
    qblk = (t * ts + lax.broadcasted_iota(jnp.int32, gate.shape, 1)) // MOBA_BLOCK
    g = jnp.where(blk < qblk, gate, -jnp.inf)
    selected = jnp.zeros(gate.shape, jnp.bool_)
    picks = []
    for _ in range(MOBA_TOPK):
        mx = jnp.max(g, axis=0, keepdims=True)
        idx = jnp.min(jnp.where(g == mx, blk, nbk), axis=0, keepdims=True)
        hit = blk == idx
        ok = (idx[0:1] < qblk[0:1]) & (mx > -jnp.inf)
        picks.append((hit, idx, ok))
        selected = selected | (hit & ok)
        g = jnp.where(hit, -jnp.inf, g)
    sel01 = jnp.where(selected, 1.0, 0.0)
    chunks = [sel01[:, c * LANES:(c + 1) * LANES] for c in range(ts // LANES)]
    within = _dot(jnp.concatenate(chunks, axis=0).astype(BF16), tri_ref[...])
    base = cnt_ref[...]
    pieces = []
    for c, chunk in enumerate(chunks):
        pieces.append(within[c * nbk:(c + 1) * nbk] + base)
        base = base + jnp.sum(chunk, axis=1, keepdims=True)
    prefix = jnp.concatenate(pieces, axis=1)
    rows = [jnp.where(ok, idx, -1) for (_, idx, ok) in picks]
    rows += [jnp.sum(jnp.where(hit, prefix, 0.0), axis=0, keepdims=True).astype(jnp.int32) for (hit, _, _) in picks]
    rows += [jnp.zeros((1, ts), jnp.int32)] * (8 - 2 * MOBA_TOPK)
    sel_ref[0, 0] = jnp.concatenate(rows, axis=0)
    cnt_ref[...] = base
    cnt_out_ref[0, 0] = base


def _moba_select(q2, kv, bsz, seq_len, ts=1024):
    ts = min(ts, seq_len)
    heads = q2.shape[1] // MOBA_HEAD_DIM
    nt = seq_len // ts
    tri = (jnp.arange(LANES)[:, None] < jnp.arange(LANES)[None, :]).astype(BF16)
    return pl.pallas_call(
        _moba_select_kernel,
        grid=(bsz, heads, nt),
        in_specs=[
            pl.BlockSpec((ts, LANES), lambda b, h, t: (b * nt + t, h // 2)),
            pl.BlockSpec((seq_len, LANES), lambda b, h, t: (b, 2 * (h // 2))),
            pl.BlockSpec((LANES, LANES), lambda b, h, t: (0, 0)),
        ],
        out_specs=[pl.BlockSpec((1, 1, 8, ts), lambda b, h, t: (b, h, 0, t)),
                   pl.BlockSpec((1, 1, MOBA_MAX_BLOCKS, LANES), lambda b, h, t: (b, h, 0, 0))],
        out_shape=[jax.ShapeDtypeStruct((bsz, heads, 8, seq_len), jnp.int32),
                   jax.ShapeDtypeStruct((bsz, heads, MOBA_MAX_BLOCKS, LANES), F32)],
        scratch_shapes=[pltpu.VMEM((MOBA_MAX_BLOCKS, LANES), BF16), pltpu.VMEM((MOBA_MAX_BLOCKS, LANES), F32)],
        compiler_params=_params("parallel", "arbitrary", "arbitrary"),
        name="moba_select",
    )(q2, kv, tri)


MOBA_DIAG_BLOCKS = 4


def _moba_diag_kernel(q_ref, kv_ref, o_ref, lse_ref):
    lane = lax.broadcasted_iota(jnp.int32, (MOBA_BLOCK, LANES), 1)
    is_a = lane < MOBA_HEAD_DIM
    r = lax.broadcasted_iota(jnp.int32, (MOBA_BLOCK, MOBA_BLOCK), 0)
    c = lax.broadcasted_iota(jnp.int32, (MOBA_BLOCK, MOBA_BLOCK), 1)
    for u in range(MOBA_DIAG_BLOCKS):
        rows = pl.ds(u * MOBA_BLOCK, MOBA_BLOCK)
        q = q_ref[rows, :]
        kb = kv_ref[rows, :LANES]
        vb = kv_ref[rows, LANES:]
        pvs, ms = [], []
        for own in (is_a, jnp.logical_not(is_a)):
            s = jnp.where(c <= r, _dot_nt(jnp.where(own, q, jnp.zeros_like(q)), kb), NEG_BIG)
            m = jnp.max(s, axis=1, keepdims=True)
            p = jnp.exp(s - m).astype(BF16)
            pvs.append(_dot(p, jnp.where(own, vb, 1.0)))
            ms.append(m)
        num = jnp.where(is_a, pvs[0], pvs[1])
        den = pltpu.roll(jnp.where(is_a, pvs[1], pvs[0]), MOBA_HEAD_DIM, 1)
        o_ref[rows, :] = num / den
        lse_ref[rows, :] = jnp.where(is_a, ms[0], ms[1]) + jnp.log(den)


def _moba_diag(q2, kv, bsz, seq_len):
    npair = q2.shape[1] // LANES
    rows = MOBA_BLOCK * MOBA_DIAG_BLOCKS
    nb = seq_len // rows
    spec = pl.BlockSpec((rows, LANES), lambda b, p, i: (b * nb + i, p))
    out = jax.ShapeDtypeStruct(q2.shape, F32)
    return pl.pallas_call(
        _moba_diag_kernel,
        grid=(bsz, npair, nb),
        in_specs=[spec, pl.BlockSpec((rows, 2 * LANES), lambda b, p, i: (b * nb + i, p))],
        out_specs=[spec, spec],
        out_shape=[out, out],
        compiler_params=_params("parallel", "parallel", "parallel"),
        name="moba_diag",
    )(q2, kv)


MOBA_TILE = 256
MOBA_TILES_PER_STEP = 8


def _moba_grouped_kernel(tile_row_ref, tile_pair_ref, tile_half_ref, tile_real_ref, qd_ref, *refs):
    n = MOBA_TILES_PER_STEP
    kv_refs, o_ref = refs[:n], refs[n + 1]
    i = pl.program_id(0)
    lane = lax.broadcasted_iota(jnp.int32, (MOBA_TILE, LANES), 1)
    kv_lane = lax.broadcasted_iota(jnp.int32, (MOBA_BLOCK, LANES), 1)

    @pl.when(tile_real_ref[i * n] > 0)
    def _():
        for u in range(n):
            tile = i * n + u
            rows = pl.ds(u * MOBA_TILE, MOBA_TILE)
            own = (lane // MOBA_HEAD_DIM) == tile_half_ref[tile]
            s = _dot_nt(qd_ref[rows, :].astype(BF16), kv_refs[u][:, :LANES])
            m = jnp.max(s, axis=1, keepdims=True)
            p = jnp.exp(s - m).astype(BF16)
            vb = jnp.where((kv_lane // MOBA_HEAD_DIM) == tile_half_ref[tile], kv_refs[u][:, LANES:], 1.0)
            pv = _dot(p, vb)
            row_sum = pltpu.roll(pv, MOBA_HEAD_DIM, 1)
            part = jnp.where(own, pv / row_sum, m + jnp.log(pv))
            o_ref[rows, :] = jnp.where(tile_real_ref[tile] > 0, part, NEG_BIG)

    @pl.when(tile_real_ref[i * n] == 0)
    def _():
        o_ref[...] = jnp.full(o_ref.shape, NEG_BIG, F32)


def _moba_grouped(qd, kv, tile_row, tile_pair, tile_half, tile_real, run_after):
    n = MOBA_TILES_PER_STEP
    n_tiles = qd.shape[0] // MOBA_TILE

    def kv_spec(u):
        return pl.BlockSpec((MOBA_BLOCK, 2 * LANES), lambda i, tr, tp, th, tl: (tr[i * n + u], tp[i * n + u]))

    grid_spec = pltpu.PrefetchScalarGridSpec(
        num_scalar_prefetch=4,
        grid=(n_tiles // n,),
        in_specs=[pl.BlockSpec((n * MOBA_TILE, LANES), lambda i, tr, tp, th, tl: (i, 0))]
        + [kv_spec(u) for u in range(n)]
        + [pl.BlockSpec((8, LANES), lambda i, tr, tp, th, tl: (0, 0))],
        out_specs=pl.BlockSpec((n * MOBA_TILE, LANES), lambda i, tr, tp, th, tl: (i, 0)),
    )
    return pl.pallas_call(
        _moba_grouped_kernel,
        grid_spec=grid_spec,
        out_shape=jax.ShapeDtypeStruct(qd.shape, F32),
        compiler_params=_params("arbitrary"),
        name="moba_grouped",
    )(tile_row, tile_pair, tile_half, tile_real, qd, *([kv] * n), run_after)


def _moba_combine_kernel(od_ref, lsed_ref, *refs):
    g_refs, o_ref = refs[:-1], refs[-1]
    lane = lax.broadcasted_iota(jnp.int32, od_ref.shape, 1)
    is_a = lane < MOBA_HEAD_DIM
    parts = [(od_ref[...], lsed_ref[...])]
    for s in range(MOBA_TOPK):
        xa = g_refs[s][0, 0, 0]
        xb = g_refs[MOBA_TOPK + s][0, 0, 0]
        o = jnp.where(is_a, xa, xb)
        lse = pltpu.roll(jnp.where(is_a, xb, xa), MOBA_HEAD_DIM, 1)
        parts.append((o, lse))
    m = parts[0][1]
    for _, lse in parts[1:]:
        m = jnp.maximum(m, lse)
    num = jnp.zeros(od_ref.shape, F32)
    den = jnp.zeros(od_ref.shape, F32)
    for o, lse in parts:
        w = jnp.exp(lse - m)
        num = num + w * o
        den = den + w
    o_ref[...] = (num / den).astype(BF16)


def _moba_combine(od, lsed, gath, bsz, seq_len, tm=1024):
    tm = min(tm, seq_len)
    npair = od.shape[1] // LANES
    nt = seq_len // tm
    spec = pl.BlockSpec((tm, LANES), lambda b, p, i: (b * nt + i, p))

    def g_spec(e, s):
        return pl.BlockSpec((1, 1, 1, tm, LANES), lambda b, p, i: (b, 2 * p + e, s, i, 0))

    return pl.pallas_call(
        _moba_combine_kernel,
        grid=(bsz, npair, nt),
        in_specs=[spec, spec] + [g_spec(e, s) for e in range(2) for s in range(MOBA_TOPK)],
        out_specs=spec,
        out_shape=jax.ShapeDtypeStruct(od.shape, BF16),
        compiler_params=_params("parallel", "parallel", "parallel"),
        name="moba_combine",
    )(od, lsed, *([gath] * (2 * MOBA_TOPK)))


SC_WINDOW = 128
SC_CORES = 2
SC_SUBCORES = 16


def _sc_mesh():
    return plsc.VectorSubcoreMesh(core_axis_name="core", subcore_axis_name="subcore")


def _sc_gather_rows(table, idx):
    n = idx.shape[0]
    d = table.shape[1]
    window = SC_WINDOW
    assert n % (window * SC_CORES * SC_SUBCORES) == 0
    per_core = n // window // SC_CORES

    @functools.partial(pl.kernel, out_type=jax.ShapeDtypeStruct((n, d), table.dtype), mesh=_sc_mesh())
    def gather_kernel(x_hbm, i_hbm, o_hbm):
        base = lax.axis_index("core") * per_core

        def body(i_vmem, o_vmem):
            pltpu.sync_copy(x_hbm.at[i_vmem.at[0]], o_vmem)

        pltpu.emit_pipeline(
            body,
            grid=(per_core,),
            in_specs=[pl.BlockSpec((1, window), index_map=lambda i: (0, base + i))],
            out_specs=[pl.BlockSpec((window, d), index_map=lambda i: (base + i, 0))],
            core_axis_name="subcore",
            dimension_semantics=(pltpu.PARALLEL,),
            trace_scopes=False,
        )(i_hbm, o_hbm)

    return gather_kernel(table, idx.reshape(1, n))


def _sc_scatter_rows(rows, dest, n_out, repeat):
    n_src, d = rows.shape
    n = dest.shape[0]
    assert n == repeat * n_src
    window = SC_WINDOW
    assert n % (window * SC_CORES * SC_SUBCORES) == 0
    per_core = n // window // SC_CORES
    src_windows = n_src // window

    @functools.partial(pl.kernel, out_type=jax.ShapeDtypeStruct((n_out, d), rows.dtype), mesh=_sc_mesh())
    def scatter_kernel(x_hbm, i_hbm, o_hbm):
        base = lax.axis_index("core") * per_core

        def body(x_vmem, i_vmem):
            pltpu.sync_copy(x_vmem, o_hbm.at[i_vmem.at[0]])

        pltpu.emit_pipeline(
            body,
            grid=(per_core,),
            in_specs=[pl.BlockSpec((window, d), index_map=lambda i: ((base + i) % src_windows, 0)),
                      pl.BlockSpec((1, window), index_map=lambda i: (0, base + i))],
            out_specs=[],
            core_axis_name="subcore",
            dimension_semantics=(pltpu.PARALLEL,),
            trace_scopes=False,
        )(x_hbm, i_hbm)

    return scatter_kernel(rows, dest.reshape(1, n))


def _moba_dispatch(q2, qh, kv, bsz, seq_len):
    heads = q2.shape[1] // MOBA_HEAD_DIM
    nbk = MOBA_MAX_BLOCKS
    sel, cnt = _moba_select(q2, kv, bsz, seq_len)
    counts = cnt[..., 0].astype(jnp.int32).reshape(bsz * heads * nbk)
    pcounts = ((counts + MOBA_TILE - 1) // MOBA_TILE) * MOBA_TILE
    pends = jnp.cumsum(pcounts)
    pstarts = (pends - pcounts).reshape(bsz, heads, 1, 1, nbk)
    n_items = bsz * heads * seq_len * MOBA_TOPK
    step_rows = MOBA_TILE * MOBA_TILES_PER_STEP
    n_rows = -(-(n_items + bsz * heads * nbk * MOBA_TILE) // step_rows) * step_rows
    n_null = step_rows
    idx = sel[:, :, 0:MOBA_TOPK, :]
    rank = sel[:, :, MOBA_TOPK:2 * MOBA_TOPK, :]
    start = jnp.sum(jnp.where(idx[..., None] == jnp.arange(nbk), pstarts, 0), axis=-1)
    null_row = n_rows + jnp.arange(seq_len, dtype=jnp.int32) % n_null
    dest = jnp.where(idx >= 0, start + rank, null_row)
    n_tiles = (n_rows + n_null) // MOBA_TILE
    n_groups = bsz * heads * nbk
    tile_start = jnp.arange(n_tiles, dtype=jnp.int32) * MOBA_TILE
    tile_g = jnp.minimum(jnp.sum((pends[None, :] <= tile_start[:, None]).astype(jnp.int32), axis=1), n_groups - 1)
    tile_real = (tile_start < pends[-1]).astype(jnp.int32)
    tile_head = (tile_g // nbk) % heads
    tile_row = (tile_g // (heads * nbk)) * (seq_len // MOBA_BLOCK) + jnp.minimum(tile_g % nbk, seq_len // MOBA_BLOCK - 1)
    dest_by_slot = dest.transpose(2, 1, 0, 3).reshape(-1)
    qd = _sc_scatter_rows(qh.reshape(-1, LANES), dest_by_slot, n_rows + n_null, MOBA_TOPK)
    return qd, dest, (tile_row, tile_head // 2, tile_head % 2, tile_real)


def _moba_finish(qd, dest, tiles, od, lsed, kv, bsz, seq_len):
    heads = dest.shape[1]
    part = _moba_grouped(qd, kv, *tiles, lsed)
    gath = _sc_gather_rows(part, dest.reshape(-1)).reshape(bsz, heads, MOBA_TOPK, seq_len, LANES)
    return _moba_combine(od, lsed, gath, bsz, seq_len)


def _mem_kv_kernel(mem_ref, gmem_ref, w_ref, gck_ref, k_ref, v_ref):
    xf = mem_ref[0]
    ms = jnp.mean(xf * xf, axis=-1, keepdims=True)
    h = (xf * lax.rsqrt(ms + EPS) * gmem_ref[...]).astype(BF16)
    kv = _dot(h, w_ref[...])
    w = k_ref.shape[2]
    hd = w // X_HEADS
    for c in range(X_HEADS):
        chunk = kv[:, c * hd:(c + 1) * hd]
        cms = jnp.mean(chunk * chunk, axis=-1, keepdims=True)
        k_ref[0, :, c * hd:(c + 1) * hd] = (chunk * lax.rsqrt(cms + EPS) * gck_ref[...]).astype(BF16)
    v_ref[0] = kv[:, w:].astype(BF16)


def _mem_kv(mem, g_mem, w_kv_mem, g_ck):
    bsz, m, d = mem.shape
    w = w_kv_mem.shape[1] // 2
    const = lambda b: (0, 0)
    out = jax.ShapeDtypeStruct((bsz, m, w), BF16)
    return pl.pallas_call(
        _mem_kv_kernel,
        grid=(bsz,),
        in_specs=[pl.BlockSpec((1, m, d), lambda b: (b, 0, 0)), pl.BlockSpec((1, d), const),
                  pl.BlockSpec((d, 2 * w), const), pl.BlockSpec((1, w // X_HEADS), const)],
        out_specs=[pl.BlockSpec((1, m, w), lambda b: (b, 0, 0))] * 2,
        out_shape=[out, out],
        compiler_params=_params("parallel"),
        name="mem_kv",
    )(mem, g_mem.reshape(1, d), w_kv_mem.astype(BF16), g_ck.reshape(1, -1))


def _merge_kernel(x_ref, ys_ref, u_ref, dskip_ref, om_ref, xq_ref, kc_ref, vc_ref, g_ref,
                  wglu_ref, wmo_ref, wco_ref, wout_ref, gffn_ref, wr_ref, br_ref,
                  x1_ref, h2_ref, logit_ref):
    d = x_ref.shape[1]
    y = ys_ref[...].astype(F32) + dskip_ref[...] * u_ref[...].astype(F32)
    ge = 0.5 * y * (1.0 + jnp.tanh(math.sqrt(2.0 / math.pi) * (y + 0.044715 * (y * y * y))))
    z = _dot(ge.astype(BF16), wglu_ref[...])
    merged = g_ref[:, 0:d].astype(F32) * (z[:, :d] * _sigmoid(z[:, d:]))
    merged = merged + g_ref[:, d:2 * d].astype(F32) * _dot(om_ref[...], wmo_ref[...])
    w = xq_ref.shape[1]
    hd = w // X_HEADS
    heads = []
    for c in range(X_HEADS):
        s = _dot_nt(xq_ref[:, c * hd:(c + 1) * hd], kc_ref[0, :, c * hd:(c + 1) * hd]) * (hd ** -0.5)
        p = jnp.exp(s - jnp.max(s, axis=1, keepdims=True))
        p = p / jnp.sum(p, axis=1, keepdims=True)
        heads.append(_dot(p.astype(BF16), vc_ref[0, :, c * hd:(c + 1) * hd]))
    oc = jnp.concatenate(heads, axis=1).astype(BF16)
    merged = merged + g_ref[:, 2 * d:3 * d].astype(F32) * _dot(oc, wco_ref[...])
    x1 = x_ref[...] + _dot(merged.astype(BF16), wout_ref[...])
    x1_ref[...] = x1
    ms = jnp.mean(x1 * x1, axis=-1, keepdims=True)
    h2 = (x1 * lax.rsqrt(ms + EPS) * gffn_ref[...]).astype(BF16)
    packed = _pack_bf16_pairs(h2.astype(F32))
    for j in range(h2_ref.shape[0]):
        h2_ref[j] = packed[:, j * LANES:(j + 1) * LANES]
    logit_ref[...] = _dot_nt(wr_ref[...], h2) + br_ref[...]


def _merge(xt, ys, u, d_skip, om, xqn, kc, vc, gates, w_glu, w_mo, w_co, w_out, g_ffn, w_router, b_router,
           seq_len, tm=512):
    t, d = xt.shape
    tm = min(tm, seq_len)
    w = d // 2
    m = kc.shape[1]
    ne = w_router.shape[1]
    nt = seq_len // tm
    row = lambda i: (i, 0)
    const = lambda i: (0, 0)
    per_b = lambda i: (i // nt, 0, 0)
    once = dict(pipeline_mode=pl.Buffered(1))
    return pl.pallas_call(
        _merge_kernel,
        grid=(t // tm,),
        in_specs=[
            pl.BlockSpec((tm, d), row), pl.BlockSpec((tm, w), row), pl.BlockSpec((tm, w), row),
            pl.BlockSpec((1, w), const), pl.BlockSpec((tm, w), row), pl.BlockSpec((tm, w), row),
            pl.BlockSpec((1, m, w), per_b), pl.BlockSpec((1, m, w), per_b),
            pl.BlockSpec((tm, N_BRANCH * d), row),
            pl.BlockSpec((w, 2 * d), const, **once), pl.BlockSpec((w, d), const, **once),
            pl.BlockSpec((w, d), const, **once), pl.BlockSpec((d, d), const, **once), pl.BlockSpec((1, d), const),
            pl.BlockSpec((ne, d), const), pl.BlockSpec((ne, 1), const),
        ],
        out_specs=[pl.BlockSpec((tm, d), row), pl.BlockSpec((d // 2 // LANES, tm, LANES), lambda i: (0, i, 0)),
                   pl.BlockSpec((ne, tm), lambda i: (0, i))],
        out_shape=[jax.ShapeDtypeStruct((t, d), F32), jax.ShapeDtypeStruct((d // 2 // LANES, t, LANES), jnp.uint32),
                   jax.ShapeDtypeStruct((ne, t), F32)],
        compiler_params=_params("parallel"),
        name="merge",
    )(xt, ys, u, d_skip.reshape(1, w), om, xqn, kc, vc, gates,
      w_glu.astype(BF16), w_mo.astype(BF16), w_co.astype(BF16), w_out.astype(BF16),
      g_ffn.reshape(1, d), w_router.T.astype(BF16), b_router.reshape(ne, 1))


def _moe_kernel(blk_e_ref, blk_used_ref, xs_ref, wgu_ref, bgu_ref, wd_ref, bd_ref, y_ref, wgu_bf, wd_bf):
    i = pl.program_id(0)
    prev = blk_e_ref[jnp.maximum(i - 1, 0)]

    @pl.when((i == 0) | (blk_e_ref[i] != prev))
    def _():
        wgu_bf[...] = wgu_ref[0].astype(BF16)
        wd_bf[...] = wd_ref[0].astype(BF16)

    @pl.when(blk_used_ref[i] > 0)
    def _():
        de = wd_bf.shape[0]
        words = jnp.concatenate([xs_ref[j] for j in range(xs_ref.shape[0])], axis=1)
        xs = jnp.concatenate(_unpack_bf16_pairs(words), axis=1).astype(BF16)
        gu = _dot(xs, wgu_bf[...]) + bgu_ref[0]
        gate = jnp.minimum(gu[:, :de], SWIGLU_LIMIT)
        up = jnp.clip(gu[:, de:], -SWIGLU_LIMIT, SWIGLU_LIMIT)
        act = gate * _sigmoid(SWIGLU_ALPHA * gate) * (up + 1.0)
        y = _dot(act.astype(BF16), wd_bf[...]) + bd_ref[0]
        packed = _pack_bf16_pairs(y.astype(BF16).astype(F32))
        for j in range(y_ref.shape[0]):
            y_ref[j] = packed[:, j * LANES:(j + 1) * LANES]

    @pl.when(blk_used_ref[i] == 0)
    def _():
        y_ref[...] = jnp.zeros(y_ref.shape, y_ref.dtype)


def _moe_experts(xs, blk_e, blk_used, w_gu, b_gu, w_down, b_down):
    slabs, p, _ = xs.shape
    d = 2 * slabs * LANES
    ne, _, de2 = w_gu.shape
    de = de2 // 2
    nblk = p // EXPERT_ROWS
    row_spec = pl.BlockSpec((slabs, EXPERT_ROWS, LANES), lambda i, e, n: (0, i, 0))
    grid_spec = pltpu.PrefetchScalarGridSpec(
        num_scalar_prefetch=2,
        grid=(nblk,),
        in_specs=[
            row_spec,
            pl.BlockSpec((1, d, de2), lambda i, e, n: (e[i], 0, 0)),
            pl.BlockSpec((1, 1, de2), lambda i, e, n: (e[i], 0, 0)),
            pl.BlockSpec((1, de, d), lambda i, e, n: (e[i], 0, 0)),
            pl.BlockSpec((1, 1, d), lambda i, e, n: (e[i], 0, 0)),
        ],
        out_specs=row_spec,
        scratch_shapes=[pltpu.VMEM((d, de2), BF16), pltpu.VMEM((de, d), BF16)],
    )
    return pl.pallas_call(
        _moe_kernel,
        grid_spec=grid_spec,
        out_shape=jax.ShapeDtypeStruct(xs.shape, jnp.uint32),
        compiler_params=_params("arbitrary"),
        name="moe_experts",
    )(blk_e, blk_used, xs, w_gu, b_gu.reshape(ne, 1, de2), w_down, b_down.reshape(ne, 1, d))


def _router_kernel(lg_ref, tri_ref, e_ref, w_ref, r_ref, cnt_out_ref, cnt_ref):
    t = pl.program_id(0)
    ne, ts = lg_ref.shape

    @pl.when(t == 0)
    def _():
        cnt_ref[...] = jnp.zeros(cnt_ref.shape, F32)

    g = lg_ref[...]
    eid = lax.broadcasted_iota(jnp.int32, g.shape, 0)
    selected = jnp.zeros(g.shape, jnp.bool_)
    picks = []
    for _ in range(TOPK_EXPERTS):
        mx = jnp.max(g, axis=0, keepdims=True)
        idx = jnp.min(jnp.where(g == mx, eid, ne), axis=0, keepdims=True)
        hit = eid == idx
        picks.append((hit, idx, mx))
        selected = selected | hit
        g = jnp.where(hit, -jnp.inf, g)
    exps = [jnp.exp(mx - picks[0][2]) for (_, _, mx) in picks]
    total = sum(exps)
    sel01 = jnp.where(selected, 1.0, 0.0).astype(BF16)
    prefix = _dot(sel01, tri_ref[...]) + jnp.tile(cnt_ref[...], (1, ts // LANES))
    pad_i = [jnp.zeros((1, ts), jnp.int32)] * (8 - TOPK_EXPERTS)
    e_ref[...] = jnp.concatenate([idx for (_, idx, _) in picks] + pad_i, axis=0)
    w_ref[...] = jnp.concatenate([e / total for e in exps] + [jnp.zeros((1, ts), F32)] * (8 - TOPK_EXPERTS), axis=0)
    r_ref[...] = jnp.concatenate(
        [jnp.sum(jnp.where(hit, prefix, 0.0), axis=0, keepdims=True).astype(jnp.int32) for (hit, _, _) in picks]
        + pad_i, axis=0)
    cnt_ref[...] = cnt_ref[...] + _dot(sel01, jnp.ones((ts, LANES), BF16))
    cnt_out_ref[...] = cnt_ref[...]


def _router(logits_t, ts=1024):
    ne, t = logits_t.shape
    ts = min(ts, t)
    tri = (jnp.arange(ts)[:, None] < jnp.arange(ts)[None, :]).astype(BF16)
    rows = pl.BlockSpec((8, ts), lambda i: (0, i))
    return pl.pallas_call(
        _router_kernel,
        grid=(t // ts,),
        in_specs=[pl.BlockSpec((ne, ts), lambda i: (0, i)), pl.BlockSpec((ts, ts), lambda i: (0, 0))],
        out_specs=[rows, rows, rows, pl.BlockSpec((ne, LANES), lambda i: (0, 0))],
        out_shape=[jax.ShapeDtypeStruct((8, t), jnp.int32), jax.ShapeDtypeStruct((8, t), F32),
                   jax.ShapeDtypeStruct((8, t), jnp.int32), jax.ShapeDtypeStruct((ne, LANES), F32)],
        scratch_shapes=[pltpu.VMEM((ne, LANES), F32)],
        compiler_params=_params("arbitrary"),
        name="moe_router",
    )(logits_t, tri)


def _moe_mix_kernel(x1_ref, w_ref, pk_ref, o_ref):
    slabs = pk_ref.shape[0]
    wts = w_ref[...]
    lo = [jnp.zeros((x1_ref.shape[0], LANES), F32)] * slabs
    hi = [jnp.zeros((x1_ref.shape[0], LANES), F32)] * slabs
    for k in range(TOPK_EXPERTS):
        wk = wts[:, k:k + 1]
        for j in range(slabs):
            a, b = _unpack_bf16_pairs(pk_ref[j, k])
            lo[j] = lo[j] + wk * a
            hi[j] = hi[j] + wk * b
    o_ref[...] = x1_ref[...] + jnp.concatenate(lo + hi, axis=1)


def _moe_mix(x1, weights_tk, picked, tm=512):
    t, d = x1.shape
    slabs = picked.shape[0]
    return pl.pallas_call(
        _moe_mix_kernel,
        grid=(t // tm,),
        in_specs=[pl.BlockSpec((tm, d), lambda i: (i, 0)),
                  pl.BlockSpec((tm, TOPK_EXPERTS), lambda i: (i, 0)),
                  pl.BlockSpec((slabs, TOPK_EXPERTS, tm, LANES), lambda i: (0, 0, i, 0))],
        out_specs=pl.BlockSpec((tm, d), lambda i: (i, 0)),
        out_shape=jax.ShapeDtypeStruct((t, d), F32),
        compiler_params=_params("parallel"),
        name="moe_mix",
    )(x1, weights_tk, picked)


def _moe_ffn(x1, h2, logits_t, w_gu, b_gu, w_down, b_down):
    t, d = x1.shape
    tk = t * TOPK_EXPERTS
    e8, w8, r8, cnt = _router(logits_t)
    top_e, weights, rank = e8[:TOPK_EXPERTS], w8[:TOPK_EXPERTS], r8[:TOPK_EXPERTS]
    counts = cnt[:, 0].astype(jnp.int32)
    pcounts = ((counts + EXPERT_ROWS - 1) // EXPERT_ROWS) * EXPERT_ROWS
    pends = jnp.cumsum(pcounts)
    pstarts = pends - pcounts
    is_e = top_e[..., None] == jnp.arange(N_EXPERTS, dtype=jnp.int32)
    dest_by_k = jnp.sum(jnp.where(is_e, pstarts, 0), axis=-1) + rank
    nblk = -(-tk // EXPERT_ROWS) + N_EXPERTS
    blk_start = jnp.arange(nblk, dtype=jnp.int32) * EXPERT_ROWS
    blk_e = jnp.minimum(jnp.sum((pends[None, :] <= blk_start[:, None]).astype(jnp.int32), axis=1), N_EXPERTS - 1)
    blk_used = (blk_start < pends[-1]).astype(jnp.int32)
    slabs = d // 2 // LANES
    p = nblk * EXPERT_ROWS
    slab_off = jnp.arange(slabs, dtype=jnp.int32) * p
    dest_kst = (dest_by_k[:, None, :] + slab_off[None, :, None]).reshape(-1)
    dest_skt = (dest_by_k[None, :, :] + slab_off[:, None, None]).reshape(-1)
    xs = _sc_scatter_rows(h2.reshape(slabs * t, LANES), dest_kst, slabs * p, TOPK_EXPERTS)
    ys = _moe_experts(xs.reshape(slabs, p, LANES), blk_e, blk_used, w_gu, b_gu, w_down, b_down)
    picked = _sc_gather_rows(ys.reshape(slabs * p, LANES), dest_skt).reshape(slabs, TOPK_EXPERTS, t, LANES)
    return _moe_mix(x1, weights.T, picked)


def kernel(x, mem, g_mix, w_in, lam_re, lam_im, log_dt, b_re, b_im, c_re, c_im, d_skip, w_glu, g_q, g_k, w_moba_out, g_mem, w_kv_mem, g_cq, g_ck, w_cross_out, w_out, g_ffn, w_router, b_router, w_gu, b_gu, w_down, b_down):
    bsz, seq_len, d = x.shape
    xt = x.reshape(bsz * seq_len, d)
    for l in range(g_mix.shape[0]):
        u, q2, kv, xqn, gates, qh = _in_proj(xt, g_mix[l], w_in[l], g_q[l], g_k[l], g_cq[l], seq_len)
        qd, dest, tiles = _moba_dispatch(q2, qh, kv, bsz, seq_len)
        mats = _s5_matrices(lam_re[l], lam_im[l], log_dt[l], b_re[l], b_im[l], c_re[l], c_im[l],
                            seq_len // S5_CHUNK)
        ys = _s5(u, mats, bsz, seq_len)
        od, lsed = _moba_diag(q2, kv, bsz, seq_len)
        kc, vc = _mem_kv(mem, g_mem[l], w_kv_mem[l], g_ck[l])
        om = _moba_finish(qd, dest, tiles, od, lsed, kv, bsz, seq_len)
        x1, h2, logits = _merge(xt, ys, u, d_skip[l], om, xqn, kc, vc, gates, w_glu[l], w_moba_out[l],
                                w_cross_out[l], w_out[l], g_ffn[l], w_router[l], b_router[l], seq_len)
        xt = _moe_ffn(x1, h2, logits, w_gu[l], b_gu[l], w_down[l], b_down[l])
    return xt.reshape(bsz, seq_len, d)
```

```python
import functools
import math

import jax
import jax.numpy as jnp
from jax import lax
from jax.experimental import pallas as pl
from jax.experimental.pallas import tpu as pltpu
from jax.experimental.pallas import tpu_sc as plsc

F32 = jnp.float32
BF16 = jnp.bfloat16

EPS = 1e-6
N_BRANCH = 3
SSM_GROUP = 16
SSM_STATE = 64
S5_CHUNK = 16
MOBA_HEAD_DIM = 64
MOBA_BLOCK = 256
MOBA_TOPK = 3
MOBA_MAX_BLOCKS = 64
ROPE_THETA = 10000.0
X_HEADS = 4
N_EXPERTS = 32
TOPK_EXPERTS = 4
SWIGLU_LIMIT = 7.0
SWIGLU_ALPHA = 1.702
EXPERT_ROWS = 512
NEG_BIG = -1e30
LANES = 128
VMEM_LIMIT_BYTES = 56 * 1024 * 1024


def _params(*sem):
    return pltpu.CompilerParams(dimension_semantics=sem, vmem_limit_bytes=VMEM_LIMIT_BYTES)


def _sigmoid(x):
    return 1.0 / (1.0 + jnp.exp(-x))


def _dot(a, b):
    return jnp.dot(a, b, preferred_element_type=F32)


def _pack_bf16_pairs(x):
    n = x.shape[1] // 2
    lo = lax.bitcast_convert_type(x[:, :n], jnp.uint32) >> 16
    hi = lax.bitcast_convert_type(x[:, n:], jnp.uint32) & jnp.uint32(0xFFFF0000)
    return lo | hi


def _unpack_bf16_pairs(w):
    lo = lax.bitcast_convert_type(w << 16, F32)
    hi = lax.bitcast_convert_type(w & jnp.uint32(0xFFFF0000), F32)
    return lo, hi


def _dot_nt(a, b):
    return lax.dot_general(a, b, (((1,), (1,)), ((), ())), preferred_element_type=F32)


def _inproj_kernel(x_ref, gmix_ref, wa_ref, wg_ref, e64_ref, gq_ref, gk_ref, gcq_ref, cos_ref, sin_ref,
                   u_ref, q_ref, kv_ref, xq_ref, g_ref, qh_ref):
    xf = x_ref[...]
    ms = jnp.mean(xf * xf, axis=-1, keepdims=True)
    h = (xf * lax.rsqrt(ms + EPS) * gmix_ref[...]).astype(BF16)
    a = _dot(h, wa_ref[...])
    w = u_ref.shape[1]
    u_ref[...] = a[:, :w]

    cos = jnp.tile(cos_ref[...], (1, w // LANES))
    sin = jnp.tile(sin_ref[...], (1, w // LANES))
    lane = lax.broadcasted_iota(jnp.int32, (xf.shape[0], w), 1)
    first_half = (lane % MOBA_HEAD_DIM) < (MOBA_HEAD_DIM // 2)

    def qk_norm_rope(raw, g):
        ss = _dot((raw * raw).astype(BF16), e64_ref[...])
        n = raw * lax.rsqrt(ss * (1.0 / MOBA_HEAD_DIM) + EPS) * g
        rot = jnp.where(first_half,
                        pltpu.roll(n, w - MOBA_HEAD_DIM // 2, 1),
                        pltpu.roll(n, MOBA_HEAD_DIM // 2, 1))
        return n * cos + rot * sin

    q = qk_norm_rope(a[:, w:2 * w], gq_ref[...]) * (MOBA_HEAD_DIM ** -0.5)
    q_ref[...] = q.astype(BF16)
    in_a = lax.broadcasted_iota(jnp.int32, (xf.shape[0], LANES), 1) < MOBA_HEAD_DIM
    for p in range(w // LANES):
        pair = q[:, p * LANES:(p + 1) * LANES].astype(BF16).astype(F32)
        qh_ref[2 * p] = jnp.where(in_a, pair, 0.0)
        qh_ref[2 * p + 1] = jnp.where(in_a, 0.0, pair)
    k = qk_norm_rope(a[:, 2 * w:3 * w], gk_ref[...]).astype(BF16)
    for p in range(w // LANES):
        kv_ref[:, (2 * p) * LANES:(2 * p + 1) * LANES] = k[:, p * LANES:(p + 1) * LANES]
        kv_ref[:, (2 * p + 1) * LANES:(2 * p + 2) * LANES] = a[:, 3 * w + p * LANES:3 * w + (p + 1) * LANES].astype(BF16)

    xq = a[:, 4 * w:5 * w]
    hd = w // X_HEADS
    for c in range(X_HEADS):
        chunk = xq[:, c * hd:(c + 1) * hd]
        cms = jnp.mean(chunk * chunk, axis=-1, keepdims=True)
        xq_ref[:, c * hd:(c + 1) * hd] = (chunk * lax.rsqrt(cms + EPS) * gcq_ref[...]).astype(BF16)

    d = xf.shape[1]
    for c in range(N_BRANCH):
        z = _dot(h, wg_ref[:, c * d:(c + 1) * d])
        g_ref[:, c * d:(c + 1) * d] = _sigmoid(z).astype(BF16)


def _in_proj(xt, g_mix, w_in, g_q, g_k, g_cq, seq_len, tm=512):
    t, d = xt.shape
    tm = min(tm, seq_len)
    w = d // 2
    wa = w_in[:, :5 * w].astype(BF16)
    wg = w_in[:, 5 * w:].astype(BF16)
    heads = w // MOBA_HEAD_DIM
    e64 = jnp.kron(jnp.eye(heads, dtype=F32), jnp.ones((MOBA_HEAD_DIM, MOBA_HEAD_DIM), F32)).astype(BF16)
    half = MOBA_HEAD_DIM // 2
    inv = ROPE_THETA ** (-jnp.arange(half, dtype=F32) / half)
    ang = jnp.arange(seq_len, dtype=F32)[:, None] * inv[None, :]
    cos = jnp.tile(jnp.cos(ang), (1, LANES // half))
    sin = jnp.tile(jnp.concatenate([-jnp.sin(ang), jnp.sin(ang)], axis=1), (1, LANES // MOBA_HEAD_DIM))
    nt = seq_len // tm
    row = lambda i: (i, 0)
    const = lambda i: (0, 0)
    out_w = jax.ShapeDtypeStruct((t, w), BF16)
    return pl.pallas_call(
        _inproj_kernel,
        grid=(t // tm,),
        in_specs=[
            pl.BlockSpec((tm, d), row),
            pl.BlockSpec((1, d), const),
            pl.BlockSpec((d, 5 * w), const, pipeline_mode=pl.Buffered(1)),
            pl.BlockSpec((d, N_BRANCH * d), const, pipeline_mode=pl.Buffered(1)),
            pl.BlockSpec((w, w), const, pipeline_mode=pl.Buffered(1)),
            pl.BlockSpec((1, w), const),
            pl.BlockSpec((1, w), const),
            pl.BlockSpec((1, w // X_HEADS), const),
            pl.BlockSpec((tm, LANES), lambda i: (i % nt, 0)),
            pl.BlockSpec((tm, LANES), lambda i: (i % nt, 0)),
        ],
        out_specs=[pl.BlockSpec((tm, w), row), pl.BlockSpec((tm, w), row), pl.BlockSpec((tm, 2 * w), row),
                   pl.BlockSpec((tm, w), row), pl.BlockSpec((tm, N_BRANCH * d), row),
                   pl.BlockSpec((heads, tm, LANES), lambda i: (0, i, 0))],
        out_shape=[jax.ShapeDtypeStruct((t, w), F32), out_w, jax.ShapeDtypeStruct((t, 2 * w), BF16), out_w,
                   jax.ShapeDtypeStruct((t, N_BRANCH * d), BF16),
                   jax.ShapeDtypeStruct((heads, t, LANES), F32)],
        compiler_params=_params("parallel"),
        name="in_proj",
    )(xt, g_mix.reshape(1, d), wa, wg, e64,
      jnp.tile(g_q, heads).reshape(1, w), jnp.tile(g_k, heads).reshape(1, w), g_cq.reshape(1, -1), cos, sin)


def _s5_matrices(lam_re, lam_im, log_dt, b_re, b_im, c_re, c_im, n_chunks):
    hp = lax.Precision.HIGHEST
    c = S5_CHUNK
    dt = jnp.exp(log_dt)[:, None]
    mag = jnp.exp(lam_re * dt)
    ar = mag * jnp.cos(lam_im * dt)
    ai = mag * jnp.sin(lam_im * dt)
    nr = ar - 1.0
    den = lam_re * lam_re + lam_im * lam_im
    cr = (nr * lam_re + ai * lam_im) / den
    ci = (ai * lam_re - nr * lam_im) / den
    bbr = cr[..., None] * b_re - ci[..., None] * b_im
    bbi = cr[..., None] * b_im + ci[..., None] * b_re

    def power(n):
        nf = n.astype(F32)[None, :, None]
        m = jnp.exp((lam_re * dt)[:, None, :] * nf)
        th = (lam_im * dt)[:, None, :] * nf
        return m * jnp.cos(th), m * jnp.sin(th)

    pr, pi = power(jnp.arange(c + 1))
    kbr = pr[..., None] * bbr[:, None] - pi[..., None] * bbi[:, None]
    kbi = pr[..., None] * bbi[:, None] + pi[..., None] * bbr[:, None]
    kk = (jnp.einsum('ghp,gtpc->gthc', c_re, kbr, precision=hp)
          - jnp.einsum('ghp,gtpc->gthc', c_im, kbi, precision=hp))
    tq = jnp.arange(c)
    g = kk.shape[0]
    rev = c - 1 - tq
    w_in = jnp.concatenate([kbr[:, rev].transpose(0, 1, 3, 2), kbi[:, rev].transpose(0, 1, 3, 2)], axis=-1)
    w_in = w_in.reshape(g, c * SSM_GROUP, 2 * SSM_STATE)
    prn, pin = pr[:, 1:], pi[:, 1:]
    wo_r = c_re[:, None] * prn[:, :, None, :] - c_im[:, None] * pin[:, :, None, :]
    wo_i = -c_re[:, None] * pin[:, :, None, :] - c_im[:, None] * prn[:, :, None, :]
    w_out = jnp.concatenate([wo_r, wo_i], axis=-1).transpose(0, 3, 1, 2).reshape(g, 2 * SSM_STATE, c * SSM_GROUP)
    n_steps = max(1, int(math.ceil(math.log2(n_chunks))))
    qr, qi = power(c * (2 ** jnp.arange(n_steps)))
    pa = jnp.concatenate([qr, qr], axis=-1)
    pb = jnp.concatenate([-qi, qi], axis=-1)
    return kk.astype(BF16), w_in.astype(BF16), w_out.astype(BF16), pa, pb


S5_SET = LANES // SSM_GROUP
S5_ROWS = 256


def _s5_kernel(u_ref, toep_ref, win_ref, wout_ref, pa_ref, pb_ref, y_ref, s_ref):
    c = S5_CHUNK
    nc = u_ref.shape[0] // c
    rb = min(S5_ROWS, nc)

    def chunk_rows(r0):
        return jnp.concatenate([u_ref[pl.ds(r0 * c + t, rb, stride=c), :] for t in range(c)], axis=1).astype(BF16)

    for blk in range(nc // rb):
        s_ref[blk * rb:(blk + 1) * rb, :] = _dot(chunk_rows(blk * rb), win_ref[0])
    row = lax.broadcasted_iota(jnp.int32, (nc, LANES), 0)
    n_steps = pa_ref.shape[1]
    for g in range(S5_SET):
        lanes = slice(g * LANES, (g + 1) * LANES)
        s = s_ref[:, lanes]
        for k in range(n_steps):
            sh = 1 << k
            if sh >= nc:
                break
            prev = jnp.where(row >= sh, pltpu.roll(s, sh, 0), 0.0)
            s = s + pa_ref[0, k:k + 1, lanes] * prev + pb_ref[0, k:k + 1, lanes] * pltpu.roll(prev, SSM_STATE, 1)
        s_ref[:, lanes] = jnp.where(row >= 1, pltpu.roll(s, 1, 0), 0.0)
    for blk in range(nc // rb):
        rows = slice(blk * rb, (blk + 1) * rb)
        a = chunk_rows(blk * rb)
        carried = _dot(s_ref[rows, :].astype(BF16), wout_ref[0])
        for t in range(c):
            y = _dot(a[:, :(t + 1) * LANES], toep_ref[0, (c - 1 - t) * LANES:, :])
            y_ref[pl.ds(blk * rb * c + t, rb, stride=c), :] = y + carried[:, t * LANES:(t + 1) * LANES]


def _s5_block_diag(mats):
    kk, w_in, w_out, pa, pb = mats
    g, c, h, p2 = kk.shape[0], S5_CHUNK, SSM_GROUP, 2 * SSM_STATE
    ns = g // S5_SET
    eye = jnp.eye(S5_SET, dtype=kk.dtype)
    lag_bd = jnp.einsum('sglhc,gk->slgckh', kk.reshape(ns, S5_SET, c + 1, h, h), eye).reshape(ns, c + 1, LANES, LANES)
    toep_bd = lag_bd[:, c - 1::-1].reshape(ns, c * LANES, LANES)

    def rows_to_states(w):
        wc = w.reshape(ns, S5_SET, c, h, p2).transpose(0, 2, 1, 3, 4).reshape(ns, c, LANES, p2)
        same = (jnp.arange(LANES)[:, None] // h) == (jnp.arange(S5_SET * p2)[None, :] // p2)
        return jnp.where(same, jnp.concatenate([wc] * S5_SET, axis=-1), 0).reshape(ns, c * LANES, S5_SET * p2)

    win_bd = rows_to_states(w_in)
    wout_bd = rows_to_states(w_out.transpose(0, 2, 1)).transpose(0, 2, 1)
    k = pa.shape[1]
    lanes_of = lambda x: x.reshape(ns, S5_SET, k, p2).transpose(0, 2, 1, 3).reshape(ns, k, S5_SET * p2)
    return toep_bd, win_bd, wout_bd, lanes_of(pa), lanes_of(pb)


def _s5(u, mats, bsz, seq_len):
    toep, w_in, w_out, pa, pb = _s5_block_diag(mats)
    ns = toep.shape[0]
    nc = seq_len // S5_CHUNK
    per_set = lambda s, b: (s, 0, 0)
    once = dict(pipeline_mode=pl.Buffered(1))
    io_spec = pl.BlockSpec((seq_len, LANES), lambda s, b: (b, s), **once)
    return pl.pallas_call(
        _s5_kernel,
        grid=(ns, bsz),
        in_specs=[
            io_spec,
            pl.BlockSpec((1,) + toep.shape[1:], per_set, **once),
            pl.BlockSpec((1,) + w_in.shape[1:], per_set, **once),
            pl.BlockSpec((1,) + w_out.shape[1:], per_set, **once),
            pl.BlockSpec((1,) + pa.shape[1:], per_set),
            pl.BlockSpec((1,) + pb.shape[1:], per_set),
        ],
        out_specs=io_spec,
        out_shape=jax.ShapeDtypeStruct(u.shape, F32),
        scratch_shapes=[pltpu.VMEM((nc, S5_SET * 2 * SSM_STATE), F32)],
        compiler_params=_params("arbitrary", "arbitrary"),
        name="s5_scan",
    )(u, toep, w_in, w_out, pa, pb)


def _moba_select_kernel(q_ref, k_ref, tri_ref, sel_ref, cnt_out_ref, km_ref, cnt_ref):
    h = pl.program_id(1)
    t = pl.program_id(2)
    ts = q_ref.shape[0]
    nbk = MOBA_MAX_BLOCKS

    @pl.when(t == 0)
    def _():
        kk = k_ref[...].astype(F32)
        nb = kk.shape[0] // MOBA_BLOCK
        km = jnp.sum(kk.reshape(nb, MOBA_BLOCK, LANES), axis=1) * (1.0 / MOBA_BLOCK)
        if nb < nbk:
            km = jnp.concatenate([km, jnp.zeros((nbk - nb, LANES), F32)], axis=0)
        lane = lax.broadcasted_iota(jnp.int32, (nbk, LANES), 1)
        km_ref[...] = jnp.where((lane // MOBA_HEAD_DIM) == (h % 2), km, 0.0).astype(BF16)
        cnt_ref[...] = jnp.zeros(cnt_ref.shape, F32)

    gate = _dot_nt(km_ref[...], q_ref[...])
    blk = lax.broadcasted_iota(jnp.int32, gate.shape, 0)
    qblk = (t * ts + lax.broadcasted_iota(jnp.int32, gate.shape, 1)) // MOBA_BLOCK
    g = jnp.where(blk < qblk, gate, -jnp.inf)
    selected = jnp.zeros(gate.shape, jnp.bool_)
    picks = []
    for _ in range(MOBA_TOPK):
        mx = jnp.max(g, axis=0, keepdims=True)
        idx = jnp.min(jnp.where(g == mx, blk, nbk), axis=0, keepdims=True)
        hit = blk == idx
        ok = (idx[0:1] < qblk[0:1]) & (mx > -jnp.inf)
        picks.append((hit, idx, ok))
        selected = selected | (hit & ok)
        g = jnp.where(hit, -jnp.inf, g)
    sel01 = jnp.where(selected, 1.0, 0.0)
    chunks = [sel01[:, c * LANES:(c + 1) * LANES] for c in range(ts // LANES)]
    within = _dot(jnp.concatenate(chunks, axis=0).astype(BF16), tri_ref[...])
    base = cnt_ref[...]
    pieces = []
    for c, chunk in enumerate(chunks):
        pieces.append(within[c * nbk:(c + 1) * nbk] + base)
        base = base + jnp.sum(chunk, axis=1, keepdims=True)
    prefix = jnp.concatenate(pieces, axis=1)
    rows = [jnp.where(ok, idx, -1) for (_, idx, ok) in picks]
    rows += [jnp.sum(jnp.where(hit, prefix, 0.0), axis=0, keepdims=True).astype(jnp.int32) for (hit, _, _) in picks]
    rows += [jnp.zeros((1, ts), jnp.int32)] * (8 - 2 * MOBA_TOPK)
    sel_ref[0, 0] = jnp.concatenate(rows, axis=0)
    cnt_ref[...] = base
    cnt_out_ref[0, 0] = base


def _moba_select(q2, kv, bsz, seq_len, ts=1024):
    ts = min(ts, seq_len)
    heads = q2.shape[1] // MOBA_HEAD_DIM
    nt = seq_len // ts
    tri = (jnp.arange(LANES)[:, None] < jnp.arange(LANES)[None, :]).astype(BF16)
    return pl.pallas_call(
        _moba_select_kernel,
        grid=(bsz, heads, nt),
        in_specs=[
            pl.BlockSpec((ts, LANES), lambda b, h, t: (b * nt + t, h // 2)),
            pl.BlockSpec((seq_len, LANES), lambda b, h, t: (b, 2 * (h // 2))),
            pl.BlockSpec((LANES, LANES), lambda b, h, t: (0, 0)),
        ],
        out_specs=[pl.BlockSpec((1, 1, 8, ts), lambda b, h, t: (b, h, 0, t)),
                   pl.BlockSpec((1, 1, MOBA_MAX_BLOCKS, LANES), lambda b, h, t: (b, h, 0, 0))],
        out_shape=[jax.ShapeDtypeStruct((bsz, heads, 8, seq_len), jnp.int32),
                   jax.ShapeDtypeStruct((bsz, heads, MOBA_MAX_BLOCKS, LANES), F32)],
        scratch_shapes=[pltpu.VMEM((MOBA_MAX_BLOCKS, LANES), BF16), pltpu.VMEM((MOBA_MAX_BLOCKS, LANES), F32)],
        compiler_params=_params("parallel", "arbitrary", "arbitrary"),
        name="moba_select",
    )(q2, kv, tri)


MOBA_DIAG_BLOCKS = 4


def _moba_diag_kernel(q_ref, kv_ref, o_ref, lse_ref):
    lane = lax.broadcasted_iota(jnp.int32, (MOBA_BLOCK, LANES), 1)
    is_a = lane < MOBA_HEAD_DIM
    r = lax.broadcasted_iota(jnp.int32, (MOBA_BLOCK, MOBA_BLOCK), 0)
    c = lax.broadcasted_iota(jnp.int32, (MOBA_BLOCK, MOBA_BLOCK), 1)
    for u in range(MOBA_DIAG_BLOCKS):
        rows = pl.ds(u * MOBA_BLOCK, MOBA_BLOCK)
        q = q_ref[rows, :]
        kb = kv_ref[rows, :LANES]
        vb = kv_ref[rows, LANES:]
        pvs, ms = [], []
        for own in (is_a, jnp.logical_not(is_a)):
            s = jnp.where(c <= r, _dot_nt(jnp.where(own, q, jnp.zeros_like(q)), kb), NEG_BIG)
            m = jnp.max(s, axis=1, keepdims=True)
            p = jnp.exp(s - m).astype(BF16)
            pvs.append(_dot(p, jnp.where(own, vb, 1.0)))
            ms.append(m)
        num = jnp.where(is_a, pvs[0], pvs[1])
        den = pltpu.roll(jnp.where(is_a, pvs[1], pvs[0]), MOBA_HEAD_DIM, 1)
        o_ref[rows, :] = num / den
        lse_ref[rows, :] = jnp.where(is_a, ms[0], ms[1]) + jnp.log(den)


def _moba_diag(q2, kv, bsz, seq_len):
    npair = q2.shape[1] // LANES
    rows = MOBA_BLOCK * MOBA_DIAG_BLOCKS
    nb = seq_len // rows
    spec = pl.BlockSpec((rows, LANES), lambda b, p, i: (b * nb + i, p))
    out = jax.ShapeDtypeStruct(q2.shape, F32)
    return pl.pallas_call(
        _moba_diag_kernel,
        grid=(bsz, npair, nb),
        in_specs=[spec, pl.BlockSpec((rows, 2 * LANES), lambda b, p, i: (b * nb + i, p))],
        out_specs=[spec, spec],
        out_shape=[out, out],
        compiler_params=_params("parallel", "parallel", "parallel"),
        name="moba_diag",
    )(q2, kv)


MOBA_TILE = 256
MOBA_TILES_PER_STEP = 16


def _moba_grouped_kernel(tile_row_ref, tile_pair_ref, tile_half_ref, tile_real_ref, qd_ref, *refs):
    n = MOBA_TILES_PER_STEP
    kv_refs, o_ref = refs[:n], refs[n + 1]
    i = pl.program_id(0)
    lane = lax.broadcasted_iota(jnp.int32, (MOBA_TILE, LANES), 1)
    kv_lane = lax.broadcasted_iota(jnp.int32, (MOBA_BLOCK, LANES), 1)

    @pl.when(tile_real_ref[i * n] > 0)
    def _():
        for u in range(n):
            tile = i * n + u
            rows = pl.ds(u * MOBA_TILE, MOBA_TILE)
            own = (lane // MOBA_HEAD_DIM) == tile_half_ref[tile]
            s = _dot_nt(qd_ref[rows, :].astype(BF16), kv_refs[u][:, :LANES])
            m = jnp.max(s, axis=1, keepdims=True)
            p = jnp.exp(s - m).astype(BF16)
            vb = jnp.where((kv_lane // MOBA_HEAD_DIM) == tile_half_ref[tile], kv_refs[u][:, LANES:], 1.0)
            pv = _dot(p, vb)
            row_sum = pltpu.roll(pv, MOBA_HEAD_DIM, 1)
            part = jnp.where(own, pv / row_sum, m + jnp.log(pv))
            o_ref[rows, :] = jnp.where(tile_real_ref[tile] > 0, part, NEG_BIG)

    @pl.when(tile_real_ref[i * n] == 0)
    def _():
        o_ref[...] = jnp.full(o_ref.shape, NEG_BIG, F32)


def _moba_grouped(qd, kv, tile_row, tile_pair, tile_half, tile_real, run_after):
    n = MOBA_TILES_PER_STEP
    n_tiles = qd.shape[0] // MOBA_TILE

    def kv_spec(u):
        return pl.BlockSpec((MOBA_BLOCK, 2 * LANES), lambda i, tr, tp, th, tl: (tr[i * n + u], tp[i * n + u]))

    grid_spec = pltpu.PrefetchScalarGridSpec(
        num_scalar_prefetch=4,
        grid=(n_tiles // n,),
        in_specs=[pl.BlockSpec((n * MOBA_TILE, LANES), lambda i, tr, tp, th, tl: (i, 0))]
        + [kv_spec(u) for u in range(n)]
        + [pl.BlockSpec((8, LANES), lambda i, tr, tp, th, tl: (0, 0))],
        out_specs=pl.BlockSpec((n * MOBA_TILE, LANES), lambda i, tr, tp, th, tl: (i, 0)),
    )
    return pl.pallas_call(
        _moba_grouped_kernel,
        grid_spec=grid_spec,
        out_shape=jax.ShapeDtypeStruct(qd.shape, F32),
        compiler_params=_params("arbitrary"),
        name="moba_grouped",
    )(tile_row, tile_pair, tile_half, tile_real, qd, *([kv] * n), run_after)


def _moba_combine_kernel(od_ref, lsed_ref, *refs):
    g_refs, o_ref = refs[:-1], refs[-1]
    lane = lax.broadcasted_iota(jnp.int32, od_ref.shape, 1)
    is_a = lane < MOBA_HEAD_DIM
    parts = [(od_ref[...], lsed_ref[...])]
    for s in range(MOBA_TOPK):
        xa = g_refs[s][0, 0, 0]
        xb = g_refs[MOBA_TOPK + s][0, 0, 0]
        o = jnp.where(is_a, xa, xb)
        lse = pltpu.roll(jnp.where(is_a, xb, xa), MOBA_HEAD_DIM, 1)
        parts.append((o, lse))
    m = parts[0][1]
    for _, lse in parts[1:]:
        m = jnp.maximum(m, lse)
    num = jnp.zeros(od_ref.shape, F32)
    den = jnp.zeros(od_ref.shape, F32)
    for o, lse in parts:
        w = jnp.exp(lse - m)
        num = num + w * o
        den = den + w
    o_ref[...] = (num / den).astype(BF16)


def _moba_combine(od, lsed, gath, bsz, seq_len, tm=1024):
    tm = min(tm, seq_len)
    npair = od.shape[1] // LANES
    nt = seq_len // tm
    spec = pl.BlockSpec((tm, LANES), lambda b, p, i: (b * nt + i, p))

    def g_spec(e, s):
        return pl.BlockSpec((1, 1, 1, tm, LANES), lambda b, p, i: (b, 2 * p + e, s, i, 0))

    return pl.pallas_call(
        _moba_combine_kernel,
        grid=(bsz, npair, nt),
        in_specs=[spec, spec] + [g_spec(e, s) for e in range(2) for s in range(MOBA_TOPK)],
        out_specs=spec,
        out_shape=jax.ShapeDtypeStruct(od.shape, BF16),
        compiler_params=_params("parallel", "parallel", "parallel"),
        name="moba_combine",
    )(od, lsed, *([gath] * (2 * MOBA_TOPK)))


SC_WINDOW = 128
SC_CORES = 2
SC_SUBCORES = 16


def _sc_mesh():
    return plsc.VectorSubcoreMesh(core_axis_name="core", subcore_axis_name="subcore")


def _sc_gather_rows(table, idx):
    n = idx.shape[0]
    d = table.shape[1]
    window = SC_WINDOW
    assert n % (window * SC_CORES * SC_SUBCORES) == 0
    per_core = n // window // SC_CORES

    @functools.partial(pl.kernel, out_type=jax.ShapeDtypeStruct((n, d), table.dtype), mesh=_sc_mesh())
    def gather_kernel(x_hbm, i_hbm, o_hbm):
        base = lax.axis_index("core") * per_core

        def body(i_vmem, o_vmem):
            pltpu.sync_copy(x_hbm.at[i_vmem.at[0]], o_vmem)

        pltpu.emit_pipeline(
            body,
            grid=(per_core,),
            in_specs=[pl.BlockSpec((1, window), index_map=lambda i: (0, base + i))],
            out_specs=[pl.BlockSpec((window, d), index_map=lambda i: (base + i, 0))],
            core_axis_name="subcore",
            dimension_semantics=(pltpu.PARALLEL,),
            trace_scopes=False,
        )(i_hbm, o_hbm)

    return gather_kernel(table, idx.reshape(1, n))


def _sc_scatter_rows(rows, dest, n_out, repeat):
    n_src, d = rows.shape
    n = dest.shape[0]
    assert n == repeat * n_src
    window = SC_WINDOW
    assert n % (window * SC_CORES * SC_SUBCORES) == 0
    per_core = n // window // SC_CORES
    src_windows = n_src // window

    @functools.partial(pl.kernel, out_type=jax.ShapeDtypeStruct((n_out, d), rows.dtype), mesh=_sc_mesh())
    def scatter_kernel(x_hbm, i_hbm, o_hbm):
        base = lax.axis_index("core") * per_core

        def body(x_vmem, i_vmem):
            pltpu.sync_copy(x_vmem, o_hbm.at[i_vmem.at[0]])

        pltpu.emit_pipeline(
            body,
            grid=(per_core,),
            in_specs=[pl.BlockSpec((window, d), index_map=lambda i: ((base + i) % src_windows, 0)),
                      pl.BlockSpec((1, window), index_map=lambda i: (0, base + i))],
            out_specs=[],
            core_axis_name="subcore",
            dimension_semantics=(pltpu.PARALLEL,),
            trace_scopes=False,
        )(x_hbm, i_hbm)

    return scatter_kernel(rows, dest.reshape(1, n))


def _moba_dispatch(q2, qh, kv, bsz, seq_len):
    heads = q2.shape[1] // MOBA_HEAD_DIM
    nbk = MOBA_MAX_BLOCKS
    sel, cnt = _moba_select(q2, kv, bsz, seq_len)
    counts = cnt[..., 0].astype(jnp.int32).reshape(bsz * heads * nbk)
    pcounts = ((counts + MOBA_TILE - 1) // MOBA_TILE) * MOBA_TILE
    pends = jnp.cumsum(pcounts)
    pstarts = (pends - pcounts).reshape(bsz, heads, 1, 1, nbk)
    n_items = bsz * heads * seq_len * MOBA_TOPK
    step_rows = MOBA_TILE * MOBA_TILES_PER_STEP
    n_rows = -(-(n_items + bsz * heads * nbk * MOBA_TILE) // step_rows) * step_rows
    n_null = step_rows
    idx = sel[:, :, 0:MOBA_TOPK, :]
    rank = sel[:, :, MOBA_TOPK:2 * MOBA_TOPK, :]
    start = jnp.sum(jnp.where(idx[..., None] == jnp.arange(nbk), pstarts, 0), axis=-1)
    null_row = n_rows + jnp.arange(seq_len, dtype=jnp.int32) % n_null
    dest = jnp.where(idx >= 0, start + rank, null_row)
    n_tiles = (n_rows + n_null) // MOBA_TILE
    n_groups = bsz * heads * nbk
    tile_start = jnp.arange(n_tiles, dtype=jnp.int32) * MOBA_TILE
    tile_g = jnp.minimum(jnp.sum((pends[None, :] <= tile_start[:, None]).astype(jnp.int32), axis=1), n_groups - 1)
    tile_real = (tile_start < pends[-1]).astype(jnp.int32)
    tile_head = (tile_g // nbk) % heads
    tile_row = (tile_g // (heads * nbk)) * (seq_len // MOBA_BLOCK) + jnp.minimum(tile_g % nbk, seq_len // MOBA_BLOCK - 1)
    dest_by_slot = dest.transpose(2, 1, 0, 3).reshape(-1)
    qd = _sc_scatter_rows(qh.reshape(-1, LANES), dest_by_slot, n_rows + n_null, MOBA_TOPK)
    return qd, dest, (tile_row, tile_head // 2, tile_head % 2, tile_real)


def _moba_finish(qd, dest, tiles, od, lsed, kv, bsz, seq_len):
    heads = dest.shape[1]
    part = _moba_grouped(qd, kv, *tiles, lsed)
    gath = _sc_gather_rows(part, dest.reshape(-1)).reshape(bsz, heads, MOBA_TOPK, seq_len, LANES)
    return _moba_combine(od, lsed, gath, bsz, seq_len)


def _mem_kv_kernel(mem_ref, gmem_ref, w_ref, gck_ref, k_ref, v_ref):
    xf = mem_ref[0]
    ms = jnp.mean(xf * xf, axis=-1, keepdims=True)
    h = (xf * lax.rsqrt(ms + EPS) * gmem_ref[...]).astype(BF16)
    kv = _dot(h, w_ref[...])
    w = k_ref.shape[2]
    hd = w // X_HEADS
    for c in range(X_HEADS):
        chunk = kv[:, c * hd:(c + 1) * hd]
        cms = jnp.mean(chunk * chunk, axis=-1, keepdims=True)
        k_ref[0, :, c * hd:(c + 1) * hd] = (chunk * lax.rsqrt(cms + EPS) * gck_ref[...]).astype(BF16)
    v_ref[0] = kv[:, w:].astype(BF16)


def _mem_kv(mem, g_mem, w_kv_mem, g_ck):
    bsz, m, d = mem.shape
    w = w_kv_mem.shape[1] // 2
    const = lambda b: (0, 0)
    out = jax.ShapeDtypeStruct((bsz, m, w), BF16)
    return pl.pallas_call(
        _mem_kv_kernel,
        grid=(bsz,),
        in_specs=[pl.BlockSpec((1, m, d), lambda b: (b, 0, 0)), pl.BlockSpec((1, d), const),
                  pl.BlockSpec((d, 2 * w), const), pl.BlockSpec((1, w // X_HEADS), const)],
        out_specs=[pl.BlockSpec((1, m, w), lambda b: (b, 0, 0))] * 2,
        out_shape=[out, out],
        compiler_params=_params("parallel"),
        name="mem_kv",
    )(mem, g_mem.reshape(1, d), w_kv_mem.astype(BF16), g_ck.reshape(1, -1))


def _merge_kernel(x_ref, ys_ref, u_ref, dskip_ref, om_ref, xq_ref, kc_ref, vc_ref, g_ref,
                  wglu_ref, wmo_ref, wco_ref, wout_ref, gffn_ref, wr_ref, br_ref,
                  x1_ref, h2_ref, logit_ref):
    d = x_ref.shape[1]
    y = ys_ref[...].astype(F32) + dskip_ref[...] * u_ref[...].astype(F32)
    ge = 0.5 * y * (1.0 + jnp.tanh(math.sqrt(2.0 / math.pi) * (y + 0.044715 * (y * y * y))))
    z = _dot(ge.astype(BF16), wglu_ref[...])
    merged = g_ref[:, 0:d].astype(F32) * (z[:, :d] * _sigmoid(z[:, d:]))
    merged = merged + g_ref[:, d:2 * d].astype(F32) * _dot(om_ref[...], wmo_ref[...])
    w = xq_ref.shape[1]
    hd = w // X_HEADS
    heads = []
    for c in range(X_HEADS):
        s = _dot_nt(xq_ref[:, c * hd:(c + 1) * hd], kc_ref[0, :, c * hd:(c + 1) * hd]) * (hd ** -0.5)
        p = jnp.exp(s - jnp.max(s, axis=1, keepdims=True))
        p = p / jnp.sum(p, axis=1, keepdims=True)
        heads.append(_dot(p.astype(BF16), vc_ref[0, :, c * hd:(c + 1) * hd]))
    oc = jnp.concatenate(heads, axis=1).astype(BF16)
    merged = merged + g_ref[:, 2 * d:3 * d].astype(F32) * _dot(oc, wco_ref[...])
    x1 = x_ref[...] + _dot(merged.astype(BF16), wout_ref[...])
    x1_ref[...] = x1
    ms = jnp.mean(x1 * x1, axis=-1, keepdims=True)
    h2 = (x1 * lax.rsqrt(ms + EPS) * gffn_ref[...]).astype(BF16)
    packed = _pack_bf16_pairs(h2.astype(F32))
    for j in range(h2_ref.shape[0]):
        h2_ref[j] = packed[:, j * LANES:(j + 1) * LANES]
    logit_ref[...] = _dot_nt(wr_ref[...], h2) + br_ref[...]


def _merge(xt, ys, u, d_skip, om, xqn, kc, vc, gates, w_glu, w_mo, w_co, w_out, g_ffn, w_router, b_router,
           seq_len, tm=512):
    t, d = xt.shape
    tm = min(tm, seq_len)
    w = d // 2
    m = kc.shape[1]
    ne = w_router.shape[1]
    nt = seq_len // tm
    row = lambda i: (i, 0)
    const = lambda i: (0, 0)
    per_b = lambda i: (i // nt, 0, 0)
    once = dict(pipeline_mode=pl.Buffered(1))
    return pl.pallas_call(
        _merge_kernel,
        grid=(t // tm,),
        in_specs=[
            pl.BlockSpec((tm, d), row), pl.BlockSpec((tm, w), row), pl.BlockSpec((tm, w), row),
            pl.BlockSpec((1, w), const), pl.BlockSpec((tm, w), row), pl.BlockSpec((tm, w), row),
            pl.BlockSpec((1, m, w), per_b), pl.BlockSpec((1, m, w), per_b),
            pl.BlockSpec((tm, N_BRANCH * d), row),
            pl.BlockSpec((w, 2 * d), const, **once), pl.BlockSpec((w, d), const, **once),
            pl.BlockSpec((w, d), const, **once), pl.BlockSpec((d, d), const, **once), pl.BlockSpec((1, d), const),
            pl.BlockSpec((ne, d), const), pl.BlockSpec((ne, 1), const),
        ],
        out_specs=[pl.BlockSpec((tm, d), row), pl.BlockSpec((d // 2 // LANES, tm, LANES), lambda i: (0, i, 0)),
                   pl.BlockSpec((ne, tm), lambda i: (0, i))],
        out_shape=[jax.ShapeDtypeStruct((t, d), F32), jax.ShapeDtypeStruct((d // 2 // LANES, t, LANES), jnp.uint32),
                   jax.ShapeDtypeStruct((ne, t), F32)],
        compiler_params=_params("parallel"),
        name="merge",
    )(xt, ys, u, d_skip.reshape(1, w), om, xqn, kc, vc, gates,
      w_glu.astype(BF16), w_mo.astype(BF16), w_co.astype(BF16), w_out.astype(BF16),
      g_ffn.reshape(1, d), w_router.T.astype(BF16), b_router.reshape(ne, 1))


def _moe_kernel(blk_e_ref, blk_used_ref, xs_ref, wgu_ref, bgu_ref, wd_ref, bd_ref, y_ref, wgu_bf, wd_bf):
    i = pl.program_id(0)
    prev = blk_e_ref[jnp.maximum(i - 1, 0)]

    @pl.when((i == 0) | (blk_e_ref[i] != prev))
    def _():
        wgu_bf[...] = wgu_ref[0].astype(BF16)
        wd_bf[...] = wd_ref[0].astype(BF16)

    @pl.when(blk_used_ref[i] > 0)
    def _():
        de = wd_bf.shape[0]
        words = jnp.concatenate([xs_ref[j] for j in range(xs_ref.shape[0])], axis=1)
        xs = jnp.concatenate(_unpack_bf16_pairs(words), axis=1).astype(BF16)
        gu = _dot(xs, wgu_bf[...]) + bgu_ref[0]
        gate = jnp.minimum(gu[:, :de], SWIGLU_LIMIT)
        up = jnp.clip(gu[:, de:], -SWIGLU_LIMIT, SWIGLU_LIMIT)
        act = gate * _sigmoid(SWIGLU_ALPHA * gate) * (up + 1.0)
        y = _dot(act.astype(BF16), wd_bf[...]) + bd_ref[0]
        packed = _pack_bf16_pairs(y.astype(BF16).astype(F32))
        for j in range(y_ref.shape[0]):
            y_ref[j] = packed[:, j * LANES:(j + 1) * LANES]

    @pl.when(blk_used_ref[i] == 0)
    def _():
        y_ref[...] = jnp.zeros(y_ref.shape, y_ref.dtype)


def _moe_experts(xs, blk_e, blk_used, w_gu, b_gu, w_down, b_down):
    slabs, p, _ = xs.shape
    d = 2 * slabs * LANES
    ne, _, de2 = w_gu.shape
    de = de2 // 2
    nblk = p // EXPERT_ROWS
    row_spec = pl.BlockSpec((slabs, EXPERT_ROWS, LANES), lambda i, e, n: (0, i, 0))
    grid_spec = pltpu.PrefetchScalarGridSpec(
        num_scalar_prefetch=2,
        grid=(nblk,),
        in_specs=[
            row_spec,
            pl.BlockSpec((1, d, de2), lambda i, e, n: (e[i], 0, 0)),
            pl.BlockSpec((1, 1, de2), lambda i, e, n: (e[i], 0, 0)),
            pl.BlockSpec((1, de, d), lambda i, e, n: (e[i], 0, 0)),
            pl.BlockSpec((1, 1, d), lambda i, e, n: (e[i], 0, 0)),
        ],
        out_specs=row_spec,
        scratch_shapes=[pltpu.VMEM((d, de2), BF16), pltpu.VMEM((de, d), BF16)],
    )
    return pl.pallas_call(
        _moe_kernel,
        grid_spec=grid_spec,
        out_shape=jax.ShapeDtypeStruct(xs.shape, jnp.uint32),
        compiler_params=_params("arbitrary"),
        name="moe_experts",
    )(blk_e, blk_used, xs, w_gu, b_gu.reshape(ne, 1, de2), w_down, b_down.reshape(ne, 1, d))


def _router_kernel(lg_ref, tri_ref, e_ref, w_ref, r_ref, cnt_out_ref, cnt_ref):
    t = pl.program_id(0)
    ne, ts = lg_ref.shape

    @pl.when(t == 0)
    def _():
        cnt_ref[...] = jnp.zeros(cnt_ref.shape, F32)

    g = lg_ref[...]
    eid = lax.broadcasted_iota(jnp.int32, g.shape, 0)
    selected = jnp.zeros(g.shape, jnp.bool_)
    picks = []
    for _ in range(TOPK_EXPERTS):
        mx = jnp.max(g, axis=0, keepdims=True)
        idx = jnp.min(jnp.where(g == mx, eid, ne), axis=0, keepdims=True)
        hit = eid == idx
        picks.append((hit, idx, mx))
        selected = selected | hit
        g = jnp.where(hit, -jnp.inf, g)
    exps = [jnp.exp(mx - picks[0][2]) for (_, _, mx) in picks]
    total = sum(exps)
    sel01 = jnp.where(selected, 1.0, 0.0).astype(BF16)
    prefix = _dot(sel01, tri_ref[...]) + jnp.tile(cnt_ref[...], (1, ts // LANES))
    pad_i = [jnp.zeros((1, ts), jnp.int32)] * (8 - TOPK_EXPERTS)
    e_ref[...] = jnp.concatenate([idx for (_, idx, _) in picks] + pad_i, axis=0)
    w_ref[...] = jnp.concatenate([e / total for e in exps] + [jnp.zeros((1, ts), F32)] * (8 - TOPK_EXPERTS), axis=0)
    r_ref[...] = jnp.concatenate(
        [jnp.sum(jnp.where(hit, prefix, 0.0), axis=0, keepdims=True).astype(jnp.int32) for (hit, _, _) in picks]
        + pad_i, axis=0)
    cnt_ref[...] = cnt_ref[...] + _dot(sel01, jnp.ones((ts, LANES), BF16))
    cnt_out_ref[...] = cnt_ref[...]


def _router(logits_t, ts=1024):
    ne, t = logits_t.shape
    ts = min(ts, t)
    tri = (jnp.arange(ts)[:, None] < jnp.arange(ts)[None, :]).astype(BF16)
    rows = pl.BlockSpec((8, ts), lambda i: (0, i))
    return pl.pallas_call(
        _router_kernel,
        grid=(t // ts,),
        in_specs=[pl.BlockSpec((ne, ts), lambda i: (0, i)), pl.BlockSpec((ts, ts), lambda i: (0, 0))],
        out_specs=[rows, rows, rows, pl.BlockSpec((ne, LANES), lambda i: (0, 0))],
        out_shape=[jax.ShapeDtypeStruct((8, t), jnp.int32), jax.ShapeDtypeStruct((8, t), F32),
                   jax.ShapeDtypeStruct((8, t), jnp.int32), jax.ShapeDtypeStruct((ne, LANES), F32)],
        scratch_shapes=[pltpu.VMEM((ne, LANES), F32)],
        compiler_params=_params("arbitrary"),
        name="moe_router",
    )(logits_t, tri)


def _moe_mix_kernel(x1_ref, w_ref, pk_ref, o_ref):
    slabs = pk_ref.shape[0]
    wts = w_ref[...]
    lo = [jnp.zeros((x1_ref.shape[0], LANES), F32)] * slabs
    hi = [jnp.zeros((x1_ref.shape[0], LANES), F32)] * slabs
    for k in range(TOPK_EXPERTS):
        wk = wts[:, k:k + 1]
        for j in range(slabs):
            a, b = _unpack_bf16_pairs(pk_ref[j, k])
            lo[j] = lo[j] + wk * a
            hi[j] = hi[j] + wk * b
    o_ref[...] = x1_ref[...] + jnp.concatenate(lo + hi, axis=1)


def _moe_mix(x1, weights_tk, picked, tm=512):
    t, d = x1.shape
    slabs = picked.shape[0]
    return pl.pallas_call(
        _moe_mix_kernel,
        grid=(t // tm,),
        in_specs=[pl.BlockSpec((tm, d), lambda i: (i, 0)),
                  pl.BlockSpec((tm, TOPK_EXPERTS), lambda i: (i, 0)),
                  pl.BlockSpec((slabs, TOPK_EXPERTS, tm, LANES), lambda i: (0, 0, i, 0))],
        out_specs=pl.BlockSpec((tm, d), lambda i: (i, 0)),
        out_shape=jax.ShapeDtypeStruct((t, d), F32),
        compiler_params=_params("parallel"),
        name="moe_mix",
    )(x1, weights_tk, picked)


def _moe_ffn(x1, h2, logits_t, w_gu, b_gu, w_down, b_down):
    t, d = x1.shape
    tk = t * TOPK_EXPERTS
    e8, w8, r8, cnt = _router(logits_t)
    top_e, weights, rank = e8[:TOPK_EXPERTS], w8[:TOPK_EXPERTS], r8[:TOPK_EXPERTS]
    counts = cnt[:, 0].astype(jnp.int32)
    pcounts = ((counts + EXPERT_ROWS - 1) // EXPERT_ROWS) * EXPERT_ROWS
    pends = jnp.cumsum(pcounts)
    pstarts = pends - pcounts
    is_e = top_e[..., None] == jnp.arange(N_EXPERTS, dtype=jnp.int32)
    dest_by_k = jnp.sum(jnp.where(is_e, pstarts, 0), axis=-1) + rank
    nblk = -(-tk // EXPERT_ROWS) + N_EXPERTS
    blk_start = jnp.arange(nblk, dtype=jnp.int32) * EXPERT_ROWS
    blk_e = jnp.minimum(jnp.sum((pends[None, :] <= blk_start[:, None]).astype(jnp.int32), axis=1), N_EXPERTS - 1)
    blk_used = (blk_start < pends[-1]).astype(jnp.int32)
    slabs = d // 2 // LANES
    p = nblk * EXPERT_ROWS
    slab_off = jnp.arange(slabs, dtype=jnp.int32) * p
    dest_kst = (dest_by_k[:, None, :] + slab_off[None, :, None]).reshape(-1)
    dest_skt = (dest_by_k[None, :, :] + slab_off[:, None, None]).reshape(-1)
    xs = _sc_scatter_rows(h2.reshape(slabs * t, LANES), dest_kst, slabs * p, TOPK_EXPERTS)
    ys = _moe_experts(xs.reshape(slabs, p, LANES), blk_e, blk_used, w_gu, b_gu, w_down, b_down)
    picked = _sc_gather_rows(ys.reshape(slabs * p, LANES), dest_skt).reshape(slabs, TOPK_EXPERTS, t, LANES)
    return _moe_mix(x1, weights.T, picked)


def kernel(x, mem, g_mix, w_in, lam_re, lam_im, log_dt, b_re, b_im, c_re, c_im, d_skip, w_glu, g_q, g_k, w_moba_out, g_mem, w_kv_mem, g_cq, g_ck, w_cross_out, w_out, g_ffn, w_router, b_router, w_gu, b_gu, w_down, b_down):
    bsz, seq_len, d = x.shape
    xt = x.reshape(bsz * seq_len, d)
    for l in range(g_mix.shape[0]):
        u, q2, kv, xqn, gates, qh = _in_proj(xt, g_mix[l], w_in[l], g_q[l], g_k[l], g_cq[l], seq_len)
        qd, dest, tiles = _moba_dispatch(q2, qh, kv, bsz, seq_len)
        mats = _s5_matrices(lam_re[l], lam_im[l], log_dt[l], b_re[l], b_im[l], c_re[l], c_im[l],
                            seq_len // S5_CHUNK)
        ys = _s5(u, mats, bsz, seq_len)
        od, lsed = _moba_diag(q2, kv, bsz, seq_len)
        kc, vc = _mem_kv(mem, g_mem[l], w_kv_mem[l], g_ck[l])
        om = _moba_finish(qd, dest, tiles, od, lsed, kv, bsz, seq_len)
        x1, h2, logits = _merge(xt, ys, u, d_skip[l], om, xqn, kc, vc, gates, w_glu[l], w_moba_out[l],
                                w_cross_out[l], w_out[l], g_ffn[l], w_router[l], b_router[l], seq_len)
        xt = _moe_ffn(x1, h2, logits, w_gu[l], b_gu[l], w_down[l], b_down[l])
    return xt.reshape(bsz, seq_len, d)
```

```python
import functools
import math

import jax
import jax.numpy as jnp
from jax import lax
from jax.experimental import pallas as pl
from jax.experimental.pallas import tpu as pltpu
from jax.experimental.pallas import tpu_sc as plsc

F32 = jnp.float32
BF16 = jnp.bfloat16

EPS = 1e-6
N_BRANCH = 3
SSM_GROUP = 16
SSM_STATE = 64
S5_CHUNK = 16
MOBA_HEAD_DIM = 64
MOBA_BLOCK = 256
MOBA_TOPK = 3
MOBA_MAX_BLOCKS = 64
ROPE_THETA = 10000.0
X_HEADS = 4
N_EXPERTS = 32
TOPK_EXPERTS = 4
SWIGLU_LIMIT = 7.0
SWIGLU_ALPHA = 1.702
EXPERT_ROWS = 512
EXPERT_ROW_PARTS = 2
NEG_BIG = -1e30
LANES = 128
VMEM_LIMIT_BYTES = 56 * 1024 * 1024


def _params(*sem):
    return pltpu.CompilerParams(dimension_semantics=sem, vmem_limit_bytes=VMEM_LIMIT_BYTES)


def _sigmoid(x):
    return 1.0 / (1.0 + jnp.exp(-x))


def _dot(a, b):
    return jnp.dot(a, b, preferred_element_type=F32)


def _pack_bf16_pairs(x):
    n = x.shape[1] // 2
    lo = lax.bitcast_convert_type(x[:, :n], jnp.uint32) >> 16
    hi = lax.bitcast_convert_type(x[:, n:], jnp.uint32) & jnp.uint32(0xFFFF0000)
    return lo | hi


def _unpack_bf16_pairs(w):
    lo = lax.bitcast_convert_type(w << 16, F32)
    hi = lax.bitcast_convert_type(w & jnp.uint32(0xFFFF0000), F32)
    return lo, hi


def _dot_nt(a, b):
    return lax.dot_general(a, b, (((1,), (1,)), ((), ())), preferred_element_type=F32)


def _inproj_kernel(x_ref, gmix_ref, wa_ref, wg_ref, e64_ref, gq_ref, gk_ref, gcq_ref, cos_ref, sin_ref,
                   u_ref, q_ref, kv_ref, xq_ref, g_ref, qh_ref):
    xf = x_ref[...]
    ms = jnp.mean(xf * xf, axis=-1, keepdims=True)
    h = (xf * lax.rsqrt(ms + EPS) * gmix_ref[...]).astype(BF16)
    a = _dot(h, wa_ref[...])
    w = u_ref.shape[1]
    u_ref[...] = a[:, :w]

    cos = jnp.tile(cos_ref[...], (1, w // LANES))
    sin = jnp.tile(sin_ref[...], (1, w // LANES))
    lane = lax.broadcasted_iota(jnp.int32, (xf.shape[0], w), 1)
    first_half = (lane % MOBA_HEAD_DIM) < (MOBA_HEAD_DIM // 2)

    def qk_norm_rope(raw, g):
        ss = _dot((raw * raw).astype(BF16), e64_ref[...])
        n = raw * lax.rsqrt(ss * (1.0 / MOBA_HEAD_DIM) + EPS) * g
        rot = jnp.where(first_half,
                        pltpu.roll(n, w - MOBA_HEAD_DIM // 2, 1),
                        pltpu.roll(n, MOBA_HEAD_DIM // 2, 1))
        return n * cos + rot * sin

    q = qk_norm_rope(a[:, w:2 * w], gq_ref[...]) * (MOBA_HEAD_DIM ** -0.5)
    q_ref[...] = q.astype(BF16)
    in_a = lax.broadcasted_iota(jnp.int32, (xf.shape[0], LANES), 1) < MOBA_HEAD_DIM
    for p in range(w // LANES):
        pair = q[:, p * LANES:(p + 1) * LANES].astype(BF16).astype(F32)
        qh_ref[2 * p] = jnp.where(in_a, pair, 0.0)
        qh_ref[2 * p + 1] = jnp.where(in_a, 0.0, pair)
    k = qk_norm_rope(a[:, 2 * w:3 * w], gk_ref[...]).astype(BF16)
    for p in range(w // LANES):
        kv_ref[:, (2 * p) * LANES:(2 * p + 1) * LANES] = k[:, p * LANES:(p + 1) * LANES]
        kv_ref[:, (2 * p + 1) * LANES:(2 * p + 2) * LANES] = a[:, 3 * w + p * LANES:3 * w + (p + 1) * LANES].astype(BF16)

    xq = a[:, 4 * w:5 * w]
    hd = w // X_HEADS
    for c in range(X_HEADS):
        chunk = xq[:, c * hd:(c + 1) * hd]
        cms = jnp.mean(chunk * chunk, axis=-1, keepdims=True)
        xq_ref[:, c * hd:(c + 1) * hd] = (chunk * lax.rsqrt(cms + EPS) * gcq_ref[...]).astype(BF16)

    d = xf.shape[1]
    for c in range(N_BRANCH):
        z = _dot(h, wg_ref[:, c * d:(c + 1) * d])
        g_ref[:, c * d:(c + 1) * d] = _sigmoid(z).astype(BF16)


def _in_proj(xt, g_mix, w_in, g_q, g_k, g_cq, seq_len, tm=512):
    t, d = xt.shape
    tm = min(tm, seq_len)
    w = d // 2
    wa = w_in[:, :5 * w].astype(BF16)
    wg = w_in[:, 5 * w:].astype(BF16)
    heads = w // MOBA_HEAD_DIM
    e64 = jnp.kron(jnp.eye(heads, dtype=F32), jnp.ones((MOBA_HEAD_DIM, MOBA_HEAD_DIM), F32)).astype(BF16)
    half = MOBA_HEAD_DIM // 2
    inv = ROPE_THETA ** (-jnp.arange(half, dtype=F32) / half)
    ang = jnp.arange(seq_len, dtype=F32)[:, None] * inv[None, :]
    cos = jnp.tile(jnp.cos(ang), (1, LANES // half))
    sin = jnp.tile(jnp.concatenate([-jnp.sin(ang), jnp.sin(ang)], axis=1), (1, LANES // MOBA_HEAD_DIM))
    nt = seq_len // tm
    row = lambda i: (i, 0)
    const = lambda i: (0, 0)
    out_w = jax.ShapeDtypeStruct((t, w), BF16)
    return pl.pallas_call(
        _inproj_kernel,
        grid=(t // tm,),
        in_specs=[
            pl.BlockSpec((tm, d), row),
            pl.BlockSpec((1, d), const),
            pl.BlockSpec((d, 5 * w), const, pipeline_mode=pl.Buffered(1)),
            pl.BlockSpec((d, N_BRANCH * d), const, pipeline_mode=pl.Buffered(1)),
            pl.BlockSpec((w, w), const, pipeline_mode=pl.Buffered(1)),
            pl.BlockSpec((1, w), const),
            pl.BlockSpec((1, w), const),
            pl.BlockSpec((1, w // X_HEADS), const),
            pl.BlockSpec((tm, LANES), lambda i: (i % nt, 0)),
            pl.BlockSpec((tm, LANES), lambda i: (i % nt, 0)),
        ],
        out_specs=[pl.BlockSpec((tm, w), row), pl.BlockSpec((tm, w), row), pl.BlockSpec((tm, 2 * w), row),
                   pl.BlockSpec((tm, w), row), pl.BlockSpec((tm, N_BRANCH * d), row),
                   pl.BlockSpec((heads, tm, LANES), lambda i: (0, i, 0))],
        out_shape=[jax.ShapeDtypeStruct((t, w), F32), out_w, jax.ShapeDtypeStruct((t, 2 * w), BF16), out_w,
                   jax.ShapeDtypeStruct((t, N_BRANCH * d), BF16),
                   jax.ShapeDtypeStruct((heads, t, LANES), F32)],
        compiler_params=_params("parallel"),
        name="in_proj",
    )(xt, g_mix.reshape(1, d), wa, wg, e64,
      jnp.tile(g_q, heads).reshape(1, w), jnp.tile(g_k, heads).reshape(1, w), g_cq.reshape(1, -1), cos, sin)


def _s5_matrices(lam_re, lam_im, log_dt, b_re, b_im, c_re, c_im, n_chunks):
    hp = lax.Precision.HIGHEST
    c = S5_CHUNK
    dt = jnp.exp(log_dt)[:, None]
    mag = jnp.exp(lam_re * dt)
    ar = mag * jnp.cos(lam_im * dt)
    ai = mag * jnp.sin(lam_im * dt)
    nr = ar - 1.0
    den = lam_re * lam_re + lam_im * lam_im
    cr = (nr * lam_re + ai * lam_im) / den
    ci = (ai * lam_re - nr * lam_im) / den
    bbr = cr[..., None] * b_re - ci[..., None] * b_im
    bbi = cr[..., None] * b_im + ci[..., None] * b_re

    def power(n):
        nf = n.astype(F32)[None, :, None]
        m = jnp.exp((lam_re * dt)[:, None, :] * nf)
        th = (lam_im * dt)[:, None, :] * nf
        return m * jnp.cos(th), m * jnp.sin(th)

    pr, pi = power(jnp.arange(c + 1))
    kbr = pr[..., None] * bbr[:, None] - pi[..., None] * bbi[:, None]
    kbi = pr[..., None] * bbi[:, None] + pi[..., None] * bbr[:, None]
    kk = (jnp.einsum('ghp,gtpc->gthc', c_re, kbr, precision=hp)
          - jnp.einsum('ghp,gtpc->gthc', c_im, kbi, precision=hp))
    tq = jnp.arange(c)
    g = kk.shape[0]
    rev = c - 1 - tq
    w_in = jnp.concatenate([kbr[:, rev].transpose(0, 1, 3, 2), kbi[:, rev].transpose(0, 1, 3, 2)], axis=-1)
    w_in = w_in.reshape(g, c * SSM_GROUP, 2 * SSM_STATE)
    prn, pin = pr[:, 1:], pi[:, 1:]
    wo_r = c_re[:, None] * prn[:, :, None, :] - c_im[:, None] * pin[:, :, None, :]
    wo_i = -c_re[:, None] * pin[:, :, None, :] - c_im[:, None] * prn[:, :, None, :]
    w_out = jnp.concatenate([wo_r, wo_i], axis=-1).transpose(0, 3, 1, 2).reshape(g, 2 * SSM_STATE, c * SSM_GROUP)
    n_steps = max(1, int(math.ceil(math.log2(n_chunks))))
    qr, qi = power(c * (2 ** jnp.arange(n_steps)))
    pa = jnp.concatenate([qr, qr], axis=-1)
    pb = jnp.concatenate([-qi, qi], axis=-1)
    return kk.astype(BF16), w_in.astype(BF16), w_out.astype(BF16), pa, pb


S5_SET = LANES // SSM_GROUP
S5_ROWS = 256


def _s5_kernel(u_ref, toep_ref, win_ref, wout_ref, pa_ref, pb_ref, y_ref, s_ref):
    c = S5_CHUNK
    nc = u_ref.shape[0] // c
    rb = min(S5_ROWS, nc)

    def chunk_rows(r0):
        return jnp.concatenate([u_ref[pl.ds(r0 * c + t, rb, stride=c), :] for t in range(c)], axis=1).astype(BF16)

    for blk in range(nc // rb):
        s_ref[blk * rb:(blk + 1) * rb, :] = _dot(chunk_rows(blk * rb), win_ref[0])
    row = lax.broadcasted_iota(jnp.int32, (nc, LANES), 0)
    n_steps = pa_ref.shape[1]
    for g in range(S5_SET):
        lanes = slice(g * LANES, (g + 1) * LANES)
        s = s_ref[:, lanes]
        for k in range(n_steps):
            sh = 1 << k
            if sh >= nc:
                break
            prev = jnp.where(row >= sh, pltpu.roll(s, sh, 0), 0.0)
            s = s + pa_ref[0, k:k + 1, lanes] * prev + pb_ref[0, k:k + 1, lanes] * pltpu.roll(prev, SSM_STATE, 1)
        s_ref[:, lanes] = jnp.where(row >= 1, pltpu.roll(s, 1, 0), 0.0)
    for blk in range(nc // rb):
        rows = slice(blk * rb, (blk + 1) * rb)
        a = chunk_rows(blk * rb)
        carried = _dot(s_ref[rows, :].astype(BF16), wout_ref[0])
        for t in range(c):
            y = _dot(a[:, :(t + 1) * LANES], toep_ref[0, (c - 1 - t) * LANES:, :])
            y_ref[pl.ds(blk * rb * c + t, rb, stride=c), :] = y + carried[:, t * LANES:(t + 1) * LANES]


def _s5_block_diag(mats):
    kk, w_in, w_out, pa, pb = mats
    g, c, h, p2 = kk.shape[0], S5_CHUNK, SSM_GROUP, 2 * SSM_STATE
    ns = g // S5_SET
    eye = jnp.eye(S5_SET, dtype=kk.dtype)
    lag_bd = jnp.einsum('sglhc,gk->slgckh', kk.reshape(ns, S5_SET, c + 1, h, h), eye).reshape(ns, c + 1, LANES, LANES)
    toep_bd = lag_bd[:, c - 1::-1].reshape(ns, c * LANES, LANES)

    def rows_to_states(w):
        wc = w.reshape(ns, S5_SET, c, h, p2).transpose(0, 2, 1, 3, 4).reshape(ns, c, LANES, p2)
        same = (jnp.arange(LANES)[:, None] // h) == (jnp.arange(S5_SET * p2)[None, :] // p2)
        return jnp.where(same, jnp.concatenate([wc] * S5_SET, axis=-1), 0).reshape(ns, c * LANES, S5_SET * p2)

    win_bd = rows_to_states(w_in)
    wout_bd = rows_to_states(w_out.transpose(0, 2, 1)).transpose(0, 2, 1)
    k = pa.shape[1]
    lanes_of = lambda x: x.reshape(ns, S5_SET, k, p2).transpose(0, 2, 1, 3).reshape(ns, k, S5_SET * p2)
    return toep_bd, win_bd, wout_bd, lanes_of(pa), lanes_of(pb)


def _s5(u, mats, bsz, seq_len):
    toep, w_in, w_out, pa, pb = _s5_block_diag(mats)
    ns = toep.shape[0]
    nc = seq_len // S5_CHUNK
    per_set = lambda s, b: (s, 0, 0)
    once = dict(pipeline_mode=pl.Buffered(1))
    io_spec = pl.BlockSpec((seq_len, LANES), lambda s, b: (b, s), **once)
    return pl.pallas_call(
        _s5_kernel,
        grid=(ns, bsz),
        in_specs=[
            io_spec,
            pl.BlockSpec((1,) + toep.shape[1:], per_set, **once),
            pl.BlockSpec((1,) + w_in.shape[1:], per_set, **once),
            pl.BlockSpec((1,) + w_out.shape[1:], per_set, **once),
            pl.BlockSpec((1,) + pa.shape[1:], per_set),
            pl.BlockSpec((1,) + pb.shape[1:], per_set),
        ],
        out_specs=io_spec,
        out_shape=jax.ShapeDtypeStruct(u.shape, F32),
        scratch_shapes=[pltpu.VMEM((nc, S5_SET * 2 * SSM_STATE), F32)],
        compiler_params=_params("arbitrary", "arbitrary"),
        name="s5_scan",
    )(u, toep, w_in, w_out, pa, pb)


def _moba_select_kernel(q_ref, k_ref, tri_ref, sel_ref, cnt_out_ref, km_ref, cnt_ref):
    h = pl.program_id(1)
    t = pl.program_id(2)
    ts = q_ref.shape[0]
    nbk = MOBA_MAX_BLOCKS

    @pl.when(t == 0)
    def _():
        kk = k_ref[...].astype(F32)
        nb = kk.shape[0] // MOBA_BLOCK
        km = jnp.sum(kk.reshape(nb, MOBA_BLOCK, LANES), axis=1) * (1.0 / MOBA_BLOCK)
        if nb < nbk:
            km = jnp.concatenate([km, jnp.zeros((nbk - nb, LANES), F32)], axis=0)
        lane = lax.broadcasted_iota(jnp.int32, (nbk, LANES), 1)
        km_ref[...] = jnp.where((lane // MOBA_HEAD_DIM) == (h % 2), km, 0.0).astype(BF16)
        cnt_ref[...] = jnp.zeros(cnt_ref.shape, F32)

    gate = _dot_nt(km_ref[...], q_ref[...])
    blk = lax.broadcasted_iota(jnp.int32, gate.shape, 0)
    qblk = (t * ts + lax.broadcasted_iota(jnp.int32, gate.shape, 1)) // MOBA_BLOCK
    g = jnp.where(blk < qblk, gate, -jnp.inf)
    selected = jnp.zeros(gate.shape, jnp.bool_)
    picks = []
    for _ in range(MOBA_TOPK):
        mx = jnp.max(g, axis=0, keepdims=True)
        idx = jnp.min(jnp.where(g == mx, blk, nbk), axis=0, keepdims=True)
        hit = blk == idx
        ok = (idx[0:1] < qblk[0:1]) & (mx > -jnp.inf)
        picks.append((hit, idx, ok))
        selected = selected | (hit & ok)
        g = jnp.where(hit, -jnp.inf, g)
    sel01 = jnp.where(selected, 1.0, 0.0)
    chunks = [sel01[:, c * LANES:(c + 1) * LANES] for c in range(ts // LANES)]
    within = _dot(jnp.concatenate(chunks, axis=0).astype(BF16), tri_ref[...])
    base = cnt_ref[...]
    pieces = []
    for c, chunk in enumerate(chunks):
        pieces.append(within[c * nbk:(c + 1) * nbk] + base)
        base = base + jnp.sum(chunk, axis=1, keepdims=True)
    prefix = jnp.concatenate(pieces, axis=1)
    rows = [jnp.where(ok, idx, -1) for (_, idx, ok) in picks]
    rows += [jnp.sum(jnp.where(hit, prefix, 0.0), axis=0, keepdims=True).astype(jnp.int32) for (hit, _, _) in picks]
    rows += [jnp.zeros((1, ts), jnp.int32)] * (8 - 2 * MOBA_TOPK)
    sel_ref[0, 0] = jnp.concatenate(rows, axis=0)
    cnt_ref[...] = base
    cnt_out_ref[0, 0] = base


def _moba_select(q2, kv, bsz, seq_len, ts=1024):
    ts = min(ts, seq_len)
    heads = q2.shape[1] // MOBA_HEAD_DIM
    nt = seq_len // ts
    tri = (jnp.arange(LANES)[:, None] < jnp.arange(LANES)[None, :]).astype(BF16)
    return pl.pallas_call(
        _moba_select_kernel,
        grid=(bsz, heads, nt),
        in_specs=[
            pl.BlockSpec((ts, LANES), lambda b, h, t: (b * nt + t, h // 2)),
            pl.BlockSpec((seq_len, LANES), lambda b, h, t: (b, 2 * (h // 2))),
            pl.BlockSpec((LANES, LANES), lambda b, h, t: (0, 0)),
        ],
        out_specs=[pl.BlockSpec((1, 1, 8, ts), lambda b, h, t: (b, h, 0, t)),
                   pl.BlockSpec((1, 1, MOBA_MAX_BLOCKS, LANES), lambda b, h, t: (b, h, 0, 0))],
        out_shape=[jax.ShapeDtypeStruct((bsz, heads, 8, seq_len), jnp.int32),
                   jax.ShapeDtypeStruct((bsz, heads, MOBA_MAX_BLOCKS, LANES), F32)],
        scratch_shapes=[pltpu.VMEM((MOBA_MAX_BLOCKS, LANES), BF16), pltpu.VMEM((MOBA_MAX_BLOCKS, LANES), F32)],
        compiler_params=_params("parallel", "arbitrary", "arbitrary"),
        name="moba_select",
    )(q2, kv, tri)


MOBA_DIAG_BLOCKS = 4


def _moba_diag_kernel(q_ref, kv_ref, o_ref, lse_ref):
    lane = lax.broadcasted_iota(jnp.int32, (MOBA_BLOCK, LANES), 1)
    is_a = lane < MOBA_HEAD_DIM
    r = lax.broadcasted_iota(jnp.int32, (MOBA_BLOCK, MOBA_BLOCK), 0)
    c = lax.broadcasted_iota(jnp.int32, (MOBA_BLOCK, MOBA_BLOCK), 1)
    for u in range(MOBA_DIAG_BLOCKS):
        rows = pl.ds(u * MOBA_BLOCK, MOBA_BLOCK)
        q = q_ref[rows, :]
        kb = kv_ref[rows, :LANES]
        vb = kv_ref[rows, LANES:]
        pvs, ms = [], []
        for own in (is_a, jnp.logical_not(is_a)):
            s = jnp.where(c <= r, _dot_nt(jnp.where(own, q, jnp.zeros_like(q)), kb), NEG_BIG)
            m = jnp.max(s, axis=1, keepdims=True)
            p = jnp.exp(s - m).astype(BF16)
            pvs.append(_dot(p, jnp.where(own, vb, 1.0)))
            ms.append(m)
        num = jnp.where(is_a, pvs[0], pvs[1])
        den = pltpu.roll(jnp.where(is_a, pvs[1], pvs[0]), MOBA_HEAD_DIM, 1)
        o_ref[rows, :] = num / den
        lse_ref[rows, :] = jnp.where(is_a, ms[0], ms[1]) + jnp.log(den)


def _moba_diag(q2, kv, bsz, seq_len):
    npair = q2.shape[1] // LANES
    rows = MOBA_BLOCK * MOBA_DIAG_BLOCKS
    nb = seq_len // rows
    spec = pl.BlockSpec((rows, LANES), lambda b, p, i: (b * nb + i, p))
    out = jax.ShapeDtypeStruct(q2.shape, F32)
    return pl.pallas_call(
        _moba_diag_kernel,
        grid=(bsz, npair, nb),
        in_specs=[spec, pl.BlockSpec((rows, 2 * LANES), lambda b, p, i: (b * nb + i, p))],
        out_specs=[spec, spec],
        out_shape=[out, out],
        compiler_params=_params("parallel", "parallel", "parallel"),
        name="moba_diag",
    )(q2, kv)


MOBA_TILE = 256
MOBA_TILES_PER_STEP = 8


def _moba_grouped_kernel(tile_row_ref, tile_pair_ref, tile_half_ref, tile_real_ref, qd_ref, *refs):
    n = MOBA_TILES_PER_STEP
    kv_refs, o_ref = refs[:n], refs[n + 1]
    i = pl.program_id(0)
    lane = lax.broadcasted_iota(jnp.int32, (MOBA_TILE, LANES), 1)
    kv_lane = lax.broadcasted_iota(jnp.int32, (MOBA_BLOCK, LANES), 1)

    @pl.when(tile_real_ref[i * n] > 0)
    def _():
        for u in range(n):
            tile = i * n + u
            rows = pl.ds(u * MOBA_TILE, MOBA_TILE)
            own = (lane // MOBA_HEAD_DIM) == tile_half_ref[tile]
            s = _dot_nt(qd_ref[rows, :].astype(BF16), kv_refs[u][:, :LANES])
            m = jnp.max(s, axis=1, keepdims=True)
            p = jnp.exp(s - m).astype(BF16)
            vb = jnp.where((kv_lane // MOBA_HEAD_DIM) == tile_half_ref[tile], kv_refs[u][:, LANES:], 1.0)
            pv = _dot(p, vb)
            row_sum = pltpu.roll(pv, MOBA_HEAD_DIM, 1)
            part = jnp.where(own, pv / row_sum, m + jnp.log(pv))
            o_ref[rows, :] = jnp.where(tile_real_ref[tile] > 0, part, NEG_BIG)

    @pl.when(tile_real_ref[i * n] == 0)
    def _():
        o_ref[...] = jnp.full(o_ref.shape, NEG_BIG, F32)


def _moba_grouped(qd, kv, tile_row, tile_pair, tile_half, tile_real, run_after):
    n = MOBA_TILES_PER_STEP
    n_tiles = qd.shape[0] // MOBA_TILE

    def kv_spec(u):
        return pl.BlockSpec((MOBA_BLOCK, 2 * LANES), lambda i, tr, tp, th, tl: (tr[i * n + u], tp[i * n + u]))

    grid_spec = pltpu.PrefetchScalarGridSpec(
        num_scalar_prefetch=4,
        grid=(n_tiles // n,),
        in_specs=[pl.BlockSpec((n * MOBA_TILE, LANES), lambda i, tr, tp, th, tl: (i, 0))]
        + [kv_spec(u) for u in range(n)]
        + [pl.BlockSpec((8, LANES), lambda i, tr, tp, th, tl: (0, 0))],
        out_specs=pl.BlockSpec((n * MOBA_TILE, LANES), lambda i, tr, tp, th, tl: (i, 0)),
    )
    return pl.pallas_call(
        _moba_grouped_kernel,
        grid_spec=grid_spec,
        out_shape=jax.ShapeDtypeStruct(qd.shape, F32),
        compiler_params=_params("arbitrary"),
        name="moba_grouped",
    )(tile_row, tile_pair, tile_half, tile_real, qd, *([kv] * n), run_after)


def _moba_combine_kernel(od_ref, lsed_ref, *refs):
    g_refs, o_ref = refs[:-1], refs[-1]
    lane = lax.broadcasted_iota(jnp.int32, od_ref.shape, 1)
    is_a = lane < MOBA_HEAD_DIM
    parts = [(od_ref[...], lsed_ref[...])]
    for s in range(MOBA_TOPK):
        xa = g_refs[s][0, 0, 0]
        xb = g_refs[MOBA_TOPK + s][0, 0, 0]
        o = jnp.where(is_a, xa, xb)
        lse = pltpu.roll(jnp.where(is_a, xb, xa), MOBA_HEAD_DIM, 1)
        parts.append((o, lse))
    m = parts[0][1]
    for _, lse in parts[1:]:
        m = jnp.maximum(m, lse)
    num = jnp.zeros(od_ref.shape, F32)
    den = jnp.zeros(od_ref.shape, F32)
    for o, lse in parts:
        w = jnp.exp(lse - m)
        num = num + w * o
        den = den + w
    o_ref[...] = (num / den).astype(BF16)


def _moba_combine(od, lsed, gath, bsz, seq_len, tm=1024):
    tm = min(tm, seq_len)
    npair = od.shape[1] // LANES
    nt = seq_len // tm
    spec = pl.BlockSpec((tm, LANES), lambda b, p, i: (b * nt + i, p))

    def g_spec(e, s):
        return pl.BlockSpec((1, 1, 1, tm, LANES), lambda b, p, i: (b, 2 * p + e, s, i, 0))

    return pl.pallas_call(
        _moba_combine_kernel,
        grid=(bsz, npair, nt),
        in_specs=[spec, spec] + [g_spec(e, s) for e in range(2) for s in range(MOBA_TOPK)],
        out_specs=spec,
        out_shape=jax.ShapeDtypeStruct(od.shape, BF16),
        compiler_params=_params("parallel", "parallel", "parallel"),
        name="moba_combine",
    )(od, lsed, *([gath] * (2 * MOBA_TOPK)))


SC_WINDOW = 128
SC_CORES = 2
SC_SUBCORES = 16


def _sc_mesh():
    return plsc.VectorSubcoreMesh(core_axis_name="core", subcore_axis_name="subcore")


def _sc_gather_rows(table, idx):
    n = idx.shape[0]
    d = table.shape[1]
    window = SC_WINDOW
    assert n % (window * SC_CORES * SC_SUBCORES) == 0
    per_core = n // window // SC_CORES

    @functools.partial(pl.kernel, out_type=jax.ShapeDtypeStruct((n, d), table.dtype), mesh=_sc_mesh())
    def gather_kernel(x_hbm, i_hbm, o_hbm):
        base = lax.axis_index("core") * per_core

        def body(i_vmem, o_vmem):
            pltpu.sync_copy(x_hbm.at[i_vmem.at[0]], o_vmem)

        pltpu.emit_pipeline(
            body,
            grid=(per_core,),
            in_specs=[pl.BlockSpec((1, window), index_map=lambda i: (0, base + i))],
            out_specs=[pl.BlockSpec((window, d), index_map=lambda i: (base + i, 0))],
            core_axis_name="subcore",
            dimension_semantics=(pltpu.PARALLEL,),
            trace_scopes=False,
        )(i_hbm, o_hbm)

    return gather_kernel(table, idx.reshape(1, n))


def _sc_scatter_rows(rows, dest, n_out, repeat):
    n_src, d = rows.shape
    n = dest.shape[0]
    assert n == repeat * n_src
    window = SC_WINDOW
    assert n % (window * SC_CORES * SC_SUBCORES) == 0
    per_core = n // window // SC_CORES
    src_windows = n_src // window

    @functools.partial(pl.kernel, out_type=jax.ShapeDtypeStruct((n_out, d), rows.dtype), mesh=_sc_mesh())
    def scatter_kernel(x_hbm, i_hbm, o_hbm):
        base = lax.axis_index("core") * per_core

        def body(x_vmem, i_vmem):
            pltpu.sync_copy(x_vmem, o_hbm.at[i_vmem.at[0]])

        pltpu.emit_pipeline(
            body,
            grid=(per_core,),
            in_specs=[pl.BlockSpec((window, d), index_map=lambda i: ((base + i) % src_windows, 0)),
                      pl.BlockSpec((1, window), index_map=lambda i: (0, base + i))],
            out_specs=[],
            core_axis_name="subcore",
            dimension_semantics=(pltpu.PARALLEL,),
            trace_scopes=False,
        )(x_hbm, i_hbm)

    return scatter_kernel(rows, dest.reshape(1, n))


def _moba_dispatch(q2, qh, kv, bsz, seq_len):
    heads = q2.shape[1] // MOBA_HEAD_DIM
    nbk = MOBA_MAX_BLOCKS
    sel, cnt = _moba_select(q2, kv, bsz, seq_len)
    counts = cnt[..., 0].astype(jnp.int32).reshape(bsz * heads * nbk)
    pcounts = ((counts + MOBA_TILE - 1) // MOBA_TILE) * MOBA_TILE
    pends = jnp.cumsum(pcounts)
    pstarts = (pends - pcounts).reshape(bsz, heads, 1, 1, nbk)
    n_items = bsz * heads * seq_len * MOBA_TOPK
    step_rows = MOBA_TILE * MOBA_TILES_PER_STEP
    n_rows = -(-(n_items + bsz * heads * nbk * MOBA_TILE) // step_rows) * step_rows
    n_null = step_rows
    idx = sel[:, :, 0:MOBA_TOPK, :]
    rank = sel[:, :, MOBA_TOPK:2 * MOBA_TOPK, :]
    start = jnp.sum(jnp.where(idx[..., None] == jnp.arange(nbk), pstarts, 0), axis=-1)
    null_row = n_rows + jnp.arange(seq_len, dtype=jnp.int32) % n_null
    dest = jnp.where(idx >= 0, start + rank, null_row)
    n_tiles = (n_rows + n_null) // MOBA_TILE
    n_groups = bsz * heads * nbk
    tile_start = jnp.arange(n_tiles, dtype=jnp.int32) * MOBA_TILE
    tile_g = jnp.minimum(jnp.sum((pends[None, :] <= tile_start[:, None]).astype(jnp.int32), axis=1), n_groups - 1)
    tile_real = (tile_start < pends[-1]).astype(jnp.int32)
    tile_head = (tile_g // nbk) % heads
    tile_row = (tile_g // (heads * nbk)) * (seq_len // MOBA_BLOCK) + jnp.minimum(tile_g % nbk, seq_len // MOBA_BLOCK - 1)
    dest_by_slot = dest.transpose(2, 1, 0, 3).reshape(-1)
    qd = _sc_scatter_rows(qh.reshape(-1, LANES), dest_by_slot, n_rows + n_null, MOBA_TOPK)
    return qd, dest, (tile_row, tile_head // 2, tile_head % 2, tile_real)


def _moba_finish(qd, dest, tiles, od, lsed, kv, bsz, seq_len):
    heads = dest.shape[1]
    part = _moba_grouped(qd, kv, *tiles, lsed)
    gath = _sc_gather_rows(part, dest.reshape(-1)).reshape(bsz, heads, MOBA_TOPK, seq_len, LANES)
    return _moba_combine(od, lsed, gath, bsz, seq_len)


def _mem_kv_kernel(mem_ref, gmem_ref, w_ref, gck_ref, k_ref, v_ref):
    xf = mem_ref[0]
    ms = jnp.mean(xf * xf, axis=-1, keepdims=True)
    h = (xf * lax.rsqrt(ms + EPS) * gmem_ref[...]).astype(BF16)
    kv = _dot(h, w_ref[...])
    w = k_ref.shape[2]
    hd = w // X_HEADS
    for c in range(X_HEADS):
        chunk = kv[:, c * hd:(c + 1) * hd]
        cms = jnp.mean(chunk * chunk, axis=-1, keepdims=True)
        k_ref[0, :, c * hd:(c + 1) * hd] = (chunk * lax.rsqrt(cms + EPS) * gck_ref[...]).astype(BF16)
    v_ref[0] = kv[:, w:].astype(BF16)


def _mem_kv(mem, g_mem, w_kv_mem, g_ck):
    bsz, m, d = mem.shape
    w = w_kv_mem.shape[1] // 2
    const = lambda b: (0, 0)
    out = jax.ShapeDtypeStruct((bsz, m, w), BF16)
    return pl.pallas_call(
        _mem_kv_kernel,
        grid=(bsz,),
        in_specs=[pl.BlockSpec((1, m, d), lambda b: (b, 0, 0)), pl.BlockSpec((1, d), const),
                  pl.BlockSpec((d, 2 * w), const), pl.BlockSpec((1, w // X_HEADS), const)],
        out_specs=[pl.BlockSpec((1, m, w), lambda b: (b, 0, 0))] * 2,
        out_shape=[out, out],
        compiler_params=_params("parallel"),
        name="mem_kv",
    )(mem, g_mem.reshape(1, d), w_kv_mem.astype(BF16), g_ck.reshape(1, -1))


def _merge_kernel(x_ref, ys_ref, u_ref, dskip_ref, om_ref, xq_ref, kc_ref, vc_ref, g_ref,
                  wglu_ref, wmo_ref, wco_ref, wout_ref, gffn_ref, wr_ref, br_ref,
                  x1_ref, h2_ref, logit_ref):
    d = x_ref.shape[1]
    y = ys_ref[...].astype(F32) + dskip_ref[...] * u_ref[...].astype(F32)
    ge = 0.5 * y * (1.0 + jnp.tanh(math.sqrt(2.0 / math.pi) * (y + 0.044715 * (y * y * y))))
    z = _dot(ge.astype(BF16), wglu_ref[...])
    merged = g_ref[:, 0:d].astype(F32) * (z[:, :d] * _sigmoid(z[:, d:]))
    merged = merged + g_ref[:, d:2 * d].astype(F32) * _dot(om_ref[...], wmo_ref[...])
    w = xq_ref.shape[1]
    hd = w // X_HEADS
    heads = []
    for c in range(X_HEADS):
        s = _dot_nt(xq_ref[:, c * hd:(c + 1) * hd], kc_ref[0, :, c * hd:(c + 1) * hd]) * (hd ** -0.5)
        p = jnp.exp(s - jnp.max(s, axis=1, keepdims=True))
        p = p / jnp.sum(p, axis=1, keepdims=True)
        heads.append(_dot(p.astype(BF16), vc_ref[0, :, c * hd:(c + 1) * hd]))
    oc = jnp.concatenate(heads, axis=1).astype(BF16)
    merged = merged + g_ref[:, 2 * d:3 * d].astype(F32) * _dot(oc, wco_ref[...])
    x1 = x_ref[...] + _dot(merged.astype(BF16), wout_ref[...])
    x1_ref[...] = x1
    ms = jnp.mean(x1 * x1, axis=-1, keepdims=True)
    h2 = (x1 * lax.rsqrt(ms + EPS) * gffn_ref[...]).astype(BF16)
    packed = _pack_bf16_pairs(h2.astype(F32))
    for j in range(h2_ref.shape[0]):
        h2_ref[j] = packed[:, j * LANES:(j + 1) * LANES]
    logit_ref[...] = _dot_nt(wr_ref[...], h2) + br_ref[...]


def _merge(xt, ys, u, d_skip, om, xqn, kc, vc, gates, w_glu, w_mo, w_co, w_out, g_ffn, w_router, b_router,
           seq_len, tm=512):
    t, d = xt.shape
    tm = min(tm, seq_len)
    w = d // 2
    m = kc.shape[1]
    ne = w_router.shape[1]
    nt = seq_len // tm
    row = lambda i: (i, 0)
    const = lambda i: (0, 0)
    per_b = lambda i: (i // nt, 0, 0)
    once = dict(pipeline_mode=pl.Buffered(1))
    return pl.pallas_call(
        _merge_kernel,
        grid=(t // tm,),
        in_specs=[
            pl.BlockSpec((tm, d), row), pl.BlockSpec((tm, w), row), pl.BlockSpec((tm, w), row),
            pl.BlockSpec((1, w), const), pl.BlockSpec((tm, w), row), pl.BlockSpec((tm, w), row),
            pl.BlockSpec((1, m, w), per_b), pl.BlockSpec((1, m, w), per_b),
            pl.BlockSpec((tm, N_BRANCH * d), row),
            pl.BlockSpec((w, 2 * d), const, **once), pl.BlockSpec((w, d), const, **once),
            pl.BlockSpec((w, d), const, **once), pl.BlockSpec((d, d), const, **once), pl.BlockSpec((1, d), const),
            pl.BlockSpec((ne, d), const), pl.BlockSpec((ne, 1), const),
        ],
        out_specs=[pl.BlockSpec((tm, d), row), pl.BlockSpec((d // 2 // LANES, tm, LANES), lambda i: (0, i, 0)),
                   pl.BlockSpec((ne, tm), lambda i: (0, i))],
        out_shape=[jax.ShapeDtypeStruct((t, d), F32), jax.ShapeDtypeStruct((d // 2 // LANES, t, LANES), jnp.uint32),
                   jax.ShapeDtypeStruct((ne, t), F32)],
        compiler_params=_params("parallel"),
        name="merge",
    )(xt, ys, u, d_skip.reshape(1, w), om, xqn, kc, vc, gates,
      w_glu.astype(BF16), w_mo.astype(BF16), w_co.astype(BF16), w_out.astype(BF16),
      g_ffn.reshape(1, d), w_router.T.astype(BF16), b_router.reshape(ne, 1))


def _moe_kernel(blk_e_ref, blk_used_ref, xs_ref, wgu_ref, bgu_ref, wd_ref, bd_ref, y_ref, wgu_bf, wd_bf):
    i = pl.program_id(0)
    prev = blk_e_ref[jnp.maximum(i - 1, 0)]

    @pl.when((i == 0) | (blk_e_ref[i] != prev))
    def _():
        wgu_bf[...] = wgu_ref[0].astype(BF16)
        wd_bf[...] = wd_ref[0].astype(BF16)

    @pl.when(blk_used_ref[i] > 0)
    def _():
        de = wd_bf.shape[0]
        part = xs_ref.shape[1] // EXPERT_ROW_PARTS
        for r in range(EXPERT_ROW_PARTS):
            rows = slice(r * part, (r + 1) * part)
            words = jnp.concatenate([xs_ref[j, rows, :] for j in range(xs_ref.shape[0])], axis=1)
            xs = jnp.concatenate(_unpack_bf16_pairs(words), axis=1).astype(BF16)
            gu = _dot(xs, wgu_bf[...]) + bgu_ref[0]
            gate = jnp.minimum(gu[:, :de], SWIGLU_LIMIT)
            up = jnp.clip(gu[:, de:], -SWIGLU_LIMIT, SWIGLU_LIMIT)
            act = gate * _sigmoid(SWIGLU_ALPHA * gate) * (up + 1.0)
            y = _dot(act.astype(BF16), wd_bf[...]) + bd_ref[0]
            packed = _pack_bf16_pairs(y.astype(BF16).astype(F32))
            for j in range(y_ref.shape[0]):
                y_ref[j, rows, :] = packed[:, j * LANES:(j + 1) * LANES]

    @pl.when(blk_used_ref[i] == 0)
    def _():
        y_ref[...] = jnp.zeros(y_ref.shape, y_ref.dtype)


def _moe_experts(xs, blk_e, blk_used, w_gu, b_gu, w_down, b_down):
    slabs, p, _ = xs.shape
    d = 2 * slabs * LANES
    ne, _, de2 = w_gu.shape
    de = de2 // 2
    nblk = p // EXPERT_ROWS
    row_spec = pl.BlockSpec((slabs, EXPERT_ROWS, LANES), lambda i, e, n: (0, i, 0))
    grid_spec = pltpu.PrefetchScalarGridSpec(
        num_scalar_prefetch=2,
        grid=(nblk,),
        in_specs=[
            row_spec,
            pl.BlockSpec((1, d, de2), lambda i, e, n: (e[i], 0, 0)),
            pl.BlockSpec((1, 1, de2), lambda i, e, n: (e[i], 0, 0)),
            pl.BlockSpec((1, de, d), lambda i, e, n: (e[i], 0, 0)),
            pl.BlockSpec((1, 1, d), lambda i, e, n: (e[i], 0, 0)),
        ],
        out_specs=row_spec,
        scratch_shapes=[pltpu.VMEM((d, de2), BF16), pltpu.VMEM((de, d), BF16)],
    )
    return pl.pallas_call(
        _moe_kernel,
        grid_spec=grid_spec,
        out_shape=jax.ShapeDtypeStruct(xs.shape, jnp.uint32),
        compiler_params=_params("arbitrary"),
        name="moe_experts",
    )(blk_e, blk_used, xs, w_gu, b_gu.reshape(ne, 1, de2), w_down, b_down.reshape(ne, 1, d))


def _router_kernel(lg_ref, tri_ref, e_ref, w_ref, r_ref, cnt_out_ref, cnt_ref):
    t = pl.program_id(0)
    ne, ts = lg_ref.shape

    @pl.when(t == 0)
    def _():
        cnt_ref[...] = jnp.zeros(cnt_ref.shape, F32)

    g = lg_ref[...]
    eid = lax.broadcasted_iota(jnp.int32, g.shape, 0)
    selected = jnp.zeros(g.shape, jnp.bool_)
    picks = []
    for _ in range(TOPK_EXPERTS):
        mx = jnp.max(g, axis=0, keepdims=True)
        idx = jnp.min(jnp.where(g == mx, eid, ne), axis=0, keepdims=True)
        hit = eid == idx
        picks.append((hit, idx, mx))
        selected = selected | hit
        g = jnp.where(hit, -jnp.inf, g)
    exps = [jnp.exp(mx - picks[0][2]) for (_, _, mx) in picks]
    total = sum(exps)
    sel01 = jnp.where(selected, 1.0, 0.0).astype(BF16)
    prefix = _dot(sel01, tri_ref[...]) + jnp.tile(cnt_ref[...], (1, ts // LANES))
    pad_i = [jnp.zeros((1, ts), jnp.int32)] * (8 - TOPK_EXPERTS)
    e_ref[...] = jnp.concatenate([idx for (_, idx, _) in picks] + pad_i, axis=0)
    w_ref[...] = jnp.concatenate([e / total for e in exps] + [jnp.zeros((1, ts), F32)] * (8 - TOPK_EXPERTS), axis=0)
    r_ref[...] = jnp.concatenate(
        [jnp.sum(jnp.where(hit, prefix, 0.0), axis=0, keepdims=True).astype(jnp.int32) for (hit, _, _) in picks]
        + pad_i, axis=0)
    cnt_ref[...] = cnt_ref[...] + _dot(sel01, jnp.ones((ts, LANES), BF16))
    cnt_out_ref[...] = cnt_ref[...]


def _router(logits_t, ts=1024):
    ne, t = logits_t.shape
    ts = min(ts, t)
    tri = (jnp.arange(ts)[:, None] < jnp.arange(ts)[None, :]).astype(BF16)
    rows = pl.BlockSpec((8, ts), lambda i: (0, i))
    return pl.pallas_call(
        _router_kernel,
        grid=(t // ts,),
        in_specs=[pl.BlockSpec((ne, ts), lambda i: (0, i)), pl.BlockSpec((ts, ts), lambda i: (0, 0))],
        out_specs=[rows, rows, rows, pl.BlockSpec((ne, LANES), lambda i: (0, 0))],
        out_shape=[jax.ShapeDtypeStruct((8, t), jnp.int32), jax.ShapeDtypeStruct((8, t), F32),
                   jax.ShapeDtypeStruct((8, t), jnp.int32), jax.ShapeDtypeStruct((ne, LANES), F32)],
        scratch_shapes=[pltpu.VMEM((ne, LANES), F32)],
        compiler_params=_params("arbitrary"),
        name="moe_router",
    )(logits_t, tri)


def _moe_mix_kernel(x1_ref, w_ref, pk_ref, o_ref):
    slabs = pk_ref.shape[0]
    wts = w_ref[...]
    lo = [jnp.zeros((x1_ref.shape[0], LANES), F32)] * slabs
    hi = [jnp.zeros((x1_ref.shape[0], LANES), F32)] * slabs
    for k in range(TOPK_EXPERTS):
        wk = wts[:, k:k + 1]
        for j in range(slabs):
            a, b = _unpack_bf16_pairs(pk_ref[j, k])
            lo[j] = lo[j] + wk * a
            hi[j] = hi[j] + wk * b
    o_ref[...] = x1_ref[...] + jnp.concatenate(lo + hi, axis=1)


def _moe_mix(x1, weights_tk, picked, tm=512):
    t, d = x1.shape
    slabs = picked.shape[0]
    return pl.pallas_call(
        _moe_mix_kernel,
        grid=(t // tm,),
        in_specs=[pl.BlockSpec((tm, d), lambda i: (i, 0)),
                  pl.BlockSpec((tm, TOPK_EXPERTS), lambda i: (i, 0)),
                  pl.BlockSpec((slabs, TOPK_EXPERTS, tm, LANES), lambda i: (0, 0, i, 0))],
        out_specs=pl.BlockSpec((tm, d), lambda i: (i, 0)),
        out_shape=jax.ShapeDtypeStruct((t, d), F32),
        compiler_params=_params("parallel"),
        name="moe_mix",
    )(x1, weights_tk, picked)


def _moe_ffn(x1, h2, logits_t, w_gu, b_gu, w_down, b_down):
    t, d = x1.shape
    tk = t * TOPK_EXPERTS
    e8, w8, r8, cnt = _router(logits_t)
    top_e, weights, rank = e8[:TOPK_EXPERTS], w8[:TOPK_EXPERTS], r8[:TOPK_EXPERTS]
    counts = cnt[:, 0].astype(jnp.int32)
    pcounts = ((counts + EXPERT_ROWS - 1) // EXPERT_ROWS) * EXPERT_ROWS
    pends = jnp.cumsum(pcounts)
    pstarts = pends - pcounts
    is_e = top_e[..., None] == jnp.arange(N_EXPERTS, dtype=jnp.int32)
    dest_by_k = jnp.sum(jnp.where(is_e, pstarts, 0), axis=-1) + rank
    nblk = -(-tk // EXPERT_ROWS) + N_EXPERTS
    blk_start = jnp.arange(nblk, dtype=jnp.int32) * EXPERT_ROWS
    blk_e = jnp.minimum(jnp.sum((pends[None, :] <= blk_start[:, None]).astype(jnp.int32), axis=1), N_EXPERTS - 1)
    blk_used = (blk_start < pends[-1]).astype(jnp.int32)
    slabs = d // 2 // LANES
    p = nblk * EXPERT_ROWS
    slab_off = jnp.arange(slabs, dtype=jnp.int32) * p
    dest_kst = (dest_by_k[:, None, :] + slab_off[None, :, None]).reshape(-1)
    dest_skt = (dest_by_k[None, :, :] + slab_off[:, None, None]).reshape(-1)
    xs = _sc_scatter_rows(h2.reshape(slabs * t, LANES), dest_kst, slabs * p, TOPK_EXPERTS)
    ys = _moe_experts(xs.reshape(slabs, p, LANES), blk_e, blk_used, w_gu, b_gu, w_down, b_down)
    picked = _sc_gather_rows(ys.reshape(slabs * p, LANES), dest_skt).reshape(slabs, TOPK_EXPERTS, t, LANES)
    return _moe_mix(x1, weights.T, picked)


def kernel(x, mem, g_mix, w_in, lam_re, lam_im, log_dt, b_re, b_im, c_re, c_im, d_skip, w_glu, g_q, g_k, w_moba_out, g_mem, w_kv_mem, g_cq, g_ck, w_cross_out, w_out, g_ffn, w_router, b_router, w_gu, b_gu, w_down, b_down):
    bsz, seq_len, d = x.shape
    xt = x.reshape(bsz * seq_len, d)
    for l in range(g_mix.shape[0]):
        u, q2, kv, xqn, gates, qh = _in_proj(xt, g_mix[l], w_in[l], g_q[l], g_k[l], g_cq[l], seq_len)
        qd, dest, tiles = _moba_dispatch(q2, qh, kv, bsz, seq_len)
        mats = _s5_matrices(lam_re[l], lam_im[l], log_dt[l], b_re[l], b_im[l], c_re[l], c_im[l],
                            seq_len // S5_CHUNK)
        ys = _s5(u, mats, bsz, seq_len)
        od, lsed = _moba_diag(q2, kv, bsz, seq_len)
        kc, vc = _mem_kv(mem, g_mem[l], w_kv_mem[l], g_ck[l])
        om = _moba_finish(qd, dest, tiles, od, lsed, kv, bsz, seq_len)
        x1, h2, logits = _merge(xt, ys, u, d_skip[l], om, xqn, kc, vc, gates, w_glu[l], w_moba_out[l],
                                w_cross_out[l], w_out[l], g_ffn[l], w_router[l], b_router[l], seq_len)
        xt = _moe_ffn(x1, h2, logits, w_gu[l], b_gu[l], w_down[l], b_down[l])
    return xt.reshape(bsz, seq_len, d)
```

```python
import functools
import math

import jax
import jax.numpy as jnp
from jax import lax
from jax.experimental import pallas as pl
from jax.experimental.pallas import tpu as pltpu
from jax.experimental.pallas import tpu_sc as plsc

F32 = jnp.float32
BF16 = jnp.bfloat16

EPS = 1e-6
N_BRANCH = 3
SSM_GROUP = 16
SSM_STATE = 64
S5_CHUNK = 16
MOBA_HEAD_DIM = 64
MOBA_BLOCK = 256
MOBA_TOPK = 3
MOBA_MAX_BLOCKS = 64
ROPE_THETA = 10000.0
X_HEADS = 4
N_EXPERTS = 32
TOPK_EXPERTS = 4
SWIGLU_LIMIT = 7.0
SWIGLU_ALPHA = 1.702
EXPERT_ROWS = 512
NEG_BIG = -1e30
LANES = 128
VMEM_LIMIT_BYTES = 56 * 1024 * 1024


def _params(*sem):
    return pltpu.CompilerParams(dimension_semantics=sem, vmem_limit_bytes=VMEM_LIMIT_BYTES)


def _sigmoid(x):
    return 1.0 / (1.0 + jnp.exp(-x))


def _dot(a, b):
    return jnp.dot(a, b, preferred_element_type=F32)


def _pack_bf16_pairs(x):
    n = x.shape[1] // 2
    lo = lax.bitcast_convert_type(x[:, :n], jnp.uint32) >> 16
    hi = lax.bitcast_convert_type(x[:, n:], jnp.uint32) & jnp.uint32(0xFFFF0000)
    return lo | hi


def _unpack_bf16_pairs(w):
    lo = lax.bitcast_convert_type(w << 16, F32)
    hi = lax.bitcast_convert_type(w & jnp.uint32(0xFFFF0000), F32)
    return lo, hi


def _dot_nt(a, b):
    return lax.dot_general(a, b, (((1,), (1,)), ((), ())), preferred_element_type=F32)


def _inproj_kernel(x_ref, gmix_ref, wa_ref, wg_ref, e64_ref, gq_ref, gk_ref, gcq_ref, cos_ref, sin_ref,
                   u_ref, q_ref, kv_ref, xq_ref, g_ref, qh_ref):
    xf = x_ref[...]
    ms = jnp.mean(xf * xf, axis=-1, keepdims=True)
    h = (xf * lax.rsqrt(ms + EPS) * gmix_ref[...]).astype(BF16)
    a = _dot(h, wa_ref[...])
    w = u_ref.shape[1]
    u_ref[...] = a[:, :w]

    cos = jnp.tile(cos_ref[...], (1, w // LANES))
    sin = jnp.tile(sin_ref[...], (1, w // LANES))
    lane = lax.broadcasted_iota(jnp.int32, (xf.shape[0], w), 1)
    first_half = (lane % MOBA_HEAD_DIM) < (MOBA_HEAD_DIM // 2)

    def qk_norm_rope(raw, g):
        ss = _dot((raw * raw).astype(BF16), e64_ref[...])
        n = raw * lax.rsqrt(ss * (1.0 / MOBA_HEAD_DIM) + EPS) * g
        rot = jnp.where(first_half,
                        pltpu.roll(n, w - MOBA_HEAD_DIM // 2, 1),
                        pltpu.roll(n, MOBA_HEAD_DIM // 2, 1))
        return n * cos + rot * sin

    q = qk_norm_rope(a[:, w:2 * w], gq_ref[...]) * (MOBA_HEAD_DIM ** -0.5)
    q_ref[...] = q.astype(BF16)
    in_a = lax.broadcasted_iota(jnp.int32, (xf.shape[0], LANES), 1) < MOBA_HEAD_DIM
    for p in range(w // LANES):
        pair = q[:, p * LANES:(p + 1) * LANES].astype(BF16).astype(F32)
        qh_ref[2 * p] = jnp.where(in_a, pair, 0.0)
        qh_ref[2 * p + 1] = jnp.where(in_a, 0.0, pair)
    k = qk_norm_rope(a[:, 2 * w:3 * w], gk_ref[...]).astype(BF16)
    for p in range(w // LANES):
        kv_ref[:, (2 * p) * LANES:(2 * p + 1) * LANES] = k[:, p * LANES:(p + 1) * LANES]
        kv_ref[:, (2 * p + 1) * LANES:(2 * p + 2) * LANES] = a[:, 3 * w + p * LANES:3 * w + (p + 1) * LANES].astype(BF16)

    xq = a[:, 4 * w:5 * w]
    hd = w // X_HEADS
    for c in range(X_HEADS):
        chunk = xq[:, c * hd:(c + 1) * hd]
        cms = jnp.mean(chunk * chunk, axis=-1, keepdims=True)
        xq_ref[:, c * hd:(c + 1) * hd] = (chunk * lax.rsqrt(cms + EPS) * gcq_ref[...]).astype(BF16)

    d = xf.shape[1]
    for c in range(N_BRANCH):
        z = _dot(h, wg_ref[:, c * d:(c + 1) * d])
        g_ref[:, c * d:(c + 1) * d] = _sigmoid(z).astype(BF16)


def _in_proj(xt, g_mix, w_in, g_q, g_k, g_cq, seq_len, tm=512):
    t, d = xt.shape
    tm = min(tm, seq_len)
    w = d // 2
    wa = w_in[:, :5 * w].astype(BF16)
    wg = w_in[:, 5 * w:].astype(BF16)
    heads = w // MOBA_HEAD_DIM
    e64 = jnp.kron(jnp.eye(heads, dtype=F32), jnp.ones((MOBA_HEAD_DIM, MOBA_HEAD_DIM), F32)).astype(BF16)
    half = MOBA_HEAD_DIM // 2
    inv = ROPE_THETA ** (-jnp.arange(half, dtype=F32) / half)
    ang = jnp.arange(seq_len, dtype=F32)[:, None] * inv[None, :]
    cos = jnp.tile(jnp.cos(ang), (1, LANES // half))
    sin = jnp.tile(jnp.concatenate([-jnp.sin(ang), jnp.sin(ang)], axis=1), (1, LANES // MOBA_HEAD_DIM))
    nt = seq_len // tm
    row = lambda i: (i, 0)
    const = lambda i: (0, 0)
    out_w = jax.ShapeDtypeStruct((t, w), BF16)
    return pl.pallas_call(
        _inproj_kernel,
        grid=(t // tm,),
        in_specs=[
            pl.BlockSpec((tm, d), row),
            pl.BlockSpec((1, d), const),
            pl.BlockSpec((d, 5 * w), const, pipeline_mode=pl.Buffered(1)),
            pl.BlockSpec((d, N_BRANCH * d), const, pipeline_mode=pl.Buffered(1)),
            pl.BlockSpec((w, w), const, pipeline_mode=pl.Buffered(1)),
            pl.BlockSpec((1, w), const),
            pl.BlockSpec((1, w), const),
            pl.BlockSpec((1, w // X_HEADS), const),
            pl.BlockSpec((tm, LANES), lambda i: (i % nt, 0)),
            pl.BlockSpec((tm, LANES), lambda i: (i % nt, 0)),
        ],
        out_specs=[pl.BlockSpec((tm, w), row), pl.BlockSpec((tm, w), row), pl.BlockSpec((tm, 2 * w), row),
                   pl.BlockSpec((tm, w), row), pl.BlockSpec((tm, N_BRANCH * d), row),
                   pl.BlockSpec((heads, tm, LANES), lambda i: (0, i, 0))],
        out_shape=[jax.ShapeDtypeStruct((t, w), F32), out_w, jax.ShapeDtypeStruct((t, 2 * w), BF16), out_w,
                   jax.ShapeDtypeStruct((t, N_BRANCH * d), BF16),
                   jax.ShapeDtypeStruct((heads, t, LANES), F32)],
        compiler_params=_params("parallel"),
        name="in_proj",
    )(xt, g_mix.reshape(1, d), wa, wg, e64,
      jnp.tile(g_q, heads).reshape(1, w), jnp.tile(g_k, heads).reshape(1, w), g_cq.reshape(1, -1), cos, sin)


def _s5_matrices(lam_re, lam_im, log_dt, b_re, b_im, c_re, c_im, n_chunks):
    hp = lax.Precision.HIGHEST
    c = S5_CHUNK
    dt = jnp.exp(log_dt)[:, None]
    mag = jnp.exp(lam_re * dt)
    ar = mag * jnp.cos(lam_im * dt)
    ai = mag * jnp.sin(lam_im * dt)
    nr = ar - 1.0
    den = lam_re * lam_re + lam_im * lam_im
    cr = (nr * lam_re + ai * lam_im) / den
    ci = (ai * lam_re - nr * lam_im) / den
    bbr = cr[..., None] * b_re - ci[..., None] * b_im
    bbi = cr[..., None] * b_im + ci[..., None] * b_re

    def power(n):
        nf = n.astype(F32)[None, :, None]
        m = jnp.exp((lam_re * dt)[:, None, :] * nf)
        th = (lam_im * dt)[:, None, :] * nf
        return m * jnp.cos(th), m * jnp.sin(th)

    pr, pi = power(jnp.arange(c + 1))
    kbr = pr[..., None] * bbr[:, None] - pi[..., None] * bbi[:, None]
    kbi = pr[..., None] * bbi[:, None] + pi[..., None] * bbr[:, None]
    kk = (jnp.einsum('ghp,gtpc->gthc', c_re, kbr, precision=hp)
          - jnp.einsum('ghp,gtpc->gthc', c_im, kbi, precision=hp))
    tq = jnp.arange(c)
    g = kk.shape[0]
    rev = c - 1 - tq
    w_in = jnp.concatenate([kbr[:, rev].transpose(0, 1, 3, 2), kbi[:, rev].transpose(0, 1, 3, 2)], axis=-1)
    w_in = w_in.reshape(g, c * SSM_GROUP, 2 * SSM_STATE)
    prn, pin = pr[:, 1:], pi[:, 1:]
    wo_r = c_re[:, None] * prn[:, :, None, :] - c_im[:, None] * pin[:, :, None, :]
    wo_i = -c_re[:, None] * pin[:, :, None, :] - c_im[:, None] * prn[:, :, None, :]
    w_out = jnp.concatenate([wo_r, wo_i], axis=-1).transpose(0, 3, 1, 2).reshape(g, 2 * SSM_STATE, c * SSM_GROUP)
    n_steps = max(1, int(math.ceil(math.log2(n_chunks))))
    qr, qi = power(c * (2 ** jnp.arange(n_steps)))
    pa = jnp.concatenate([qr, qr], axis=-1)
    pb = jnp.concatenate([-qi, qi], axis=-1)
    return kk.astype(BF16), w_in.astype(BF16), w_out.astype(BF16), pa, pb


S5_SET = LANES // SSM_GROUP
S5_ROWS = 256


def _s5_kernel(u_ref, toep_ref, win_ref, wout_ref, pa_ref, pb_ref, y_ref, s_ref):
    c = S5_CHUNK
    nc = u_ref.shape[0] // c
    rb = min(S5_ROWS, nc)

    def chunk_rows(r0):
        return jnp.concatenate([u_ref[pl.ds(r0 * c + t, rb, stride=c), :] for t in range(c)], axis=1).astype(BF16)

    for blk in range(nc // rb):
        s_ref[blk * rb:(blk + 1) * rb, :] = _dot(chunk_rows(blk * rb), win_ref[0])
    row = lax.broadcasted_iota(jnp.int32, (nc, LANES), 0)
    n_steps = pa_ref.shape[1]
    for g in range(S5_SET):
        lanes = slice(g * LANES, (g + 1) * LANES)
        s = s_ref[:, lanes]
        for k in range(n_steps):
            sh = 1 << k
            if sh >= nc:
                break
            prev = jnp.where(row >= sh, pltpu.roll(s, sh, 0), 0.0)
            s = s + pa_ref[0, k:k + 1, lanes] * prev + pb_ref[0, k:k + 1, lanes] * pltpu.roll(prev, SSM_STATE, 1)
        s_ref[:, lanes] = jnp.where(row >= 1, pltpu.roll(s, 1, 0), 0.0)
    for blk in range(nc // rb):
        rows = slice(blk * rb, (blk + 1) * rb)
        a = chunk_rows(blk * rb)
        carried = _dot(s_ref[rows, :].astype(BF16), wout_ref[0])
        for t in range(c):
            y = _dot(a[:, :(t + 1) * LANES], toep_ref[0, (c - 1 - t) * LANES:, :])
            y_ref[pl.ds(blk * rb * c + t, rb, stride=c), :] = y + carried[:, t * LANES:(t + 1) * LANES]


def _s5_block_diag(mats):
    kk, w_in, w_out, pa, pb = mats
    g, c, h, p2 = kk.shape[0], S5_CHUNK, SSM_GROUP, 2 * SSM_STATE
    ns = g // S5_SET
    eye = jnp.eye(S5_SET, dtype=kk.dtype)
    lag_bd = jnp.einsum('sglhc,gk->slgckh', kk.reshape(ns, S5_SET, c + 1, h, h), eye).reshape(ns, c + 1, LANES, LANES)
    toep_bd = lag_bd[:, c - 1::-1].reshape(ns, c * LANES, LANES)

    def rows_to_states(w):
        wc = w.reshape(ns, S5_SET, c, h, p2).transpose(0, 2, 1, 3, 4).reshape(ns, c, LANES, p2)
        same = (jnp.arange(LANES)[:, None] // h) == (jnp.arange(S5_SET * p2)[None, :] // p2)
        return jnp.where(same, jnp.concatenate([wc] * S5_SET, axis=-1), 0).reshape(ns, c * LANES, S5_SET * p2)

    win_bd = rows_to_states(w_in)
    wout_bd = rows_to_states(w_out.transpose(0, 2, 1)).transpose(0, 2, 1)
    k = pa.shape[1]
    lanes_of = lambda x: x.reshape(ns, S5_SET, k, p2).transpose(0, 2, 1, 3).reshape(ns, k, S5_SET * p2)
    return toep_bd, win_bd, wout_bd, lanes_of(pa), lanes_of(pb)


def _s5(u, mats, bsz, seq_len):
    toep, w_in, w_out, pa, pb = _s5_block_diag(mats)
    ns = toep.shape[0]
    nc = seq_len // S5_CHUNK
    per_set = lambda s, b: (s, 0, 0)
    once = dict(pipeline_mode=pl.Buffered(1))
    io_spec = pl.BlockSpec((seq_len, LANES), lambda s, b: (b, s), **once)
    return pl.pallas_call(
        _s5_kernel,
        grid=(ns, bsz),
        in_specs=[
            io_spec,
            pl.BlockSpec((1,) + toep.shape[1:], per_set, **once),
            pl.BlockSpec((1,) + w_in.shape[1:], per_set, **once),
            pl.BlockSpec((1,) + w_out.shape[1:], per_set, **once),
            pl.BlockSpec((1,) + pa.shape[1:], per_set),
            pl.BlockSpec((1,) + pb.shape[1:], per_set),
        ],
        out_specs=io_spec,
        out_shape=jax.ShapeDtypeStruct(u.shape, F32),
        scratch_shapes=[pltpu.VMEM((nc, S5_SET * 2 * SSM_STATE), F32)],
        compiler_params=_params("arbitrary", "arbitrary"),
        name="s5_scan",
    )(u, toep, w_in, w_out, pa, pb)


def _moba_select_kernel(q_ref, k_ref, tri_ref, sel_ref, cnt_out_ref, km_ref, cnt_ref):
    h = pl.program_id(1)
    t = pl.program_id(2)
    ts = q_ref.shape[0]
    nbk = MOBA_MAX_BLOCKS

    @pl.when(t == 0)
    def _():
        kk = k_ref[...].astype(F32)
        nb = kk.shape[0] // MOBA_BLOCK
        km = jnp.sum(kk.reshape(nb, MOBA_BLOCK, LANES), axis=1) * (1.0 / MOBA_BLOCK)
        if nb < nbk:
            km = jnp.concatenate([km, jnp.zeros((nbk - nb, LANES), F32)], axis=0)
        lane = lax.broadcasted_iota(jnp.int32, (nbk, LANES), 1)
        km_ref[...] = jnp.where((lane // MOBA_HEAD_DIM) == (h % 2), km, 0.0).astype(BF16)
        cnt_ref[...] = jnp.zeros(cnt_ref.shape, F32)

    gate = _dot_nt(km_ref[...], q_ref[...])
    blk = lax.broadcasted_iota(jnp.int32, gate.shape, 0)
    qblk = (t * ts + lax.broadcasted_iota(jnp.int32, gate.shape, 1)) // MOBA_BLOCK
    g = jnp.where(blk < qblk, gate, -jnp.inf)
    selected = jnp.zeros(gate.shape, jnp.bool_)
    picks = []
    for _ in range(MOBA_TOPK):
        mx = jnp.max(g, axis=0, keepdims=True)
        idx = jnp.min(jnp.where(g == mx, blk, nbk), axis=0, keepdims=True)
        hit = blk == idx
        ok = (idx[0:1] < qblk[0:1]) & (mx > -jnp.inf)
        picks.append((hit, idx, ok))
        selected = selected | (hit & ok)
        g = jnp.where(hit, -jnp.inf, g)
    sel01 = jnp.where(selected, 1.0, 0.0)
    chunks = [sel01[:, c * LANES:(c + 1) * LANES] for c in range(ts // LANES)]
    within = _dot(jnp.concatenate(chunks, axis=0).astype(BF16), tri_ref[...])
    base = cnt_ref[...]
    pieces = []
    for c, chunk in enumerate(chunks):
        pieces.append(within[c * nbk:(c + 1) * nbk] + base)
        base = base + jnp.sum(chunk, axis=1, keepdims=True)
    prefix = jnp.concatenate(pieces, axis=1)
    rows = [jnp.where(ok, idx, -1) for (_, idx, ok) in picks]
    rows += [jnp.sum(jnp.where(hit, prefix, 0.0), axis=0, keepdims=True).astype(jnp.int32) for (hit, _, _) in picks]
    rows += [jnp.zeros((1, ts), jnp.int32)] * (8 - 2 * MOBA_TOPK)
    sel_ref[0, 0] = jnp.concatenate(rows, axis=0)
    cnt_ref[...] = base
    cnt_out_ref[0, 0] = base


def _moba_select(q2, kv, bsz, seq_len, ts=1024):
    ts = min(ts, seq_len)
    heads = q2.shape[1] // MOBA_HEAD_DIM
    nt = seq_len // ts
    tri = (jnp.arange(LANES)[:, None] < jnp.arange(LANES)[None, :]).astype(BF16)
    return pl.pallas_call(
        _moba_select_kernel,
        grid=(bsz, heads, nt),
        in_specs=[
            pl.BlockSpec((ts, LANES), lambda b, h, t: (b * nt + t, h // 2)),
            pl.BlockSpec((seq_len, LANES), lambda b, h, t: (b, 2 * (h // 2))),
            pl.BlockSpec((LANES, LANES), lambda b, h, t: (0, 0)),
        ],
        out_specs=[pl.BlockSpec((1, 1, 8, ts), lambda b, h, t: (b, h, 0, t)),
                   pl.BlockSpec((1, 1, MOBA_MAX_BLOCKS, LANES), lambda b, h, t: (b, h, 0, 0))],
        out_shape=[jax.ShapeDtypeStruct((bsz, heads, 8, seq_len), jnp.int32),
                   jax.ShapeDtypeStruct((bsz, heads, MOBA_MAX_BLOCKS, LANES), F32)],
        scratch_shapes=[pltpu.VMEM((MOBA_MAX_BLOCKS, LANES), BF16), pltpu.VMEM((MOBA_MAX_BLOCKS, LANES), F32)],
        compiler_params=_params("parallel", "arbitrary", "arbitrary"),
        name="moba_select",
    )(q2, kv, tri)


MOBA_DIAG_BLOCKS = 4


def _moba_diag_kernel(q_ref, kv_ref, o_ref, lse_ref):
    lane = lax.broadcasted_iota(jnp.int32, (MOBA_BLOCK, LANES), 1)
    is_a = lane < MOBA_HEAD_DIM
    r = lax.broadcasted_iota(jnp.int32, (MOBA_BLOCK, MOBA_BLOCK), 0)
    c = lax.broadcasted_iota(jnp.int32, (MOBA_BLOCK, MOBA_BLOCK), 1)
    for u in range(MOBA_DIAG_BLOCKS):
        rows = pl.ds(u * MOBA_BLOCK, MOBA_BLOCK)
        q = q_ref[rows, :]
        kb = kv_ref[rows, :LANES]
        vb = kv_ref[rows, LANES:]
        pvs, ms = [], []
        for own in (is_a, jnp.logical_not(is_a)):
            s = jnp.where(c <= r, _dot_nt(jnp.where(own, q, jnp.zeros_like(q)), kb), NEG_BIG)
            m = jnp.max(s, axis=1, keepdims=True)
            p = jnp.exp(s - m).astype(BF16)
            pvs.append(_dot(p, jnp.where(own, vb, 1.0)))
            ms.append(m)
        num = jnp.where(is_a, pvs[0], pvs[1])
        den = pltpu.roll(jnp.where(is_a, pvs[1], pvs[0]), MOBA_HEAD_DIM, 1)
        o_ref[rows, :] = num / den
        lse_ref[rows, :] = jnp.where(is_a, ms[0], ms[1]) + jnp.log(den)


def _moba_diag(q2, kv, bsz, seq_len):
    npair = q2.shape[1] // LANES
    rows = MOBA_BLOCK * MOBA_DIAG_BLOCKS
    nb = seq_len // rows
    spec = pl.BlockSpec((rows, LANES), lambda b, p, i: (b * nb + i, p))
    out = jax.ShapeDtypeStruct(q2.shape, F32)
    return pl.pallas_call(
        _moba_diag_kernel,
        grid=(bsz, npair, nb),
        in_specs=[spec, pl.BlockSpec((rows, 2 * LANES), lambda b, p, i: (b * nb + i, p))],
        out_specs=[spec, spec],
        out_shape=[out, out],
        compiler_params=_params("parallel", "parallel", "parallel"),
        name="moba_diag",
    )(q2, kv)


MOBA_TILE = 256
MOBA_TILES_PER_STEP = 8


def _moba_grouped_kernel(tile_row_ref, tile_pair_ref, tile_half_ref, tile_real_ref, qd_ref, *refs):
    n = MOBA_TILES_PER_STEP
    kv_refs, o_ref = refs[:n], refs[n + 1]
    i = pl.program_id(0)
    lane = lax.broadcasted_iota(jnp.int32, (MOBA_TILE, LANES), 1)
    kv_lane = lax.broadcasted_iota(jnp.int32, (MOBA_BLOCK, LANES), 1)

    @pl.when(tile_real_ref[i * n] > 0)
    def _():
        for u in range(n):
            tile = i * n + u
            rows = pl.ds(u * MOBA_TILE, MOBA_TILE)
            own = (lane // MOBA_HEAD_DIM) == tile_half_ref[tile]
            s = _dot_nt(qd_ref[rows, :].astype(BF16), kv_refs[u][:, :LANES])
            m = jnp.max(s, axis=1, keepdims=True)
            p = jnp.exp(s - m).astype(BF16)
            vb = jnp.where((kv_lane // MOBA_HEAD_DIM) == tile_half_ref[tile], kv_refs[u][:, LANES:], 1.0)
            pv = _dot(p, vb)
            row_sum = pltpu.roll(pv, MOBA_HEAD_DIM, 1)
            part = jnp.where(own, pv / row_sum, m + jnp.log(pv))
            o_ref[rows, :] = jnp.where(tile_real_ref[tile] > 0, part, NEG_BIG)

    @pl.when(tile_real_ref[i * n] == 0)
    def _():
        o_ref[...] = jnp.full(o_ref.shape, NEG_BIG, F32)


def _moba_grouped(qd, kv, tile_row, tile_pair, tile_half, tile_real, run_after):
    n = MOBA_TILES_PER_STEP
    n_tiles = qd.shape[0] // MOBA_TILE

    def kv_spec(u):
        return pl.BlockSpec((MOBA_BLOCK, 2 * LANES), lambda i, tr, tp, th, tl: (tr[i * n + u], tp[i * n + u]))

    grid_spec = pltpu.PrefetchScalarGridSpec(
        num_scalar_prefetch=4,
        grid=(n_tiles // n,),
        in_specs=[pl.BlockSpec((n * MOBA_TILE, LANES), lambda i, tr, tp, th, tl: (i, 0))]
        + [kv_spec(u) for u in range(n)]
        + [pl.BlockSpec((8, LANES), lambda i, tr, tp, th, tl: (0, 0))],
        out_specs=pl.BlockSpec((n * MOBA_TILE, LANES), lambda i, tr, tp, th, tl: (i, 0)),
    )
    return pl.pallas_call(
        _moba_grouped_kernel,
        grid_spec=grid_spec,
        out_shape=jax.ShapeDtypeStruct(qd.shape, F32),
        compiler_params=_params("arbitrary"),
        name="moba_grouped",
    )(tile_row, tile_pair, tile_half, tile_real, qd, *([kv] * n), run_after)


def _moba_combine_kernel(od_ref, lsed_ref, *refs):
    g_refs, o_ref = refs[:-1], refs[-1]
    lane = lax.broadcasted_iota(jnp.int32, od_ref.shape, 1)
    is_a = lane < MOBA_HEAD_DIM
    parts = [(od_ref[...], lsed_ref[...])]
    for s in range(MOBA_TOPK):
        xa = g_refs[s][0, 0, 0]
        xb = g_refs[MOBA_TOPK + s][0, 0, 0]
        o = jnp.where(is_a, xa, xb)
        lse = pltpu.roll(jnp.where(is_a, xb, xa), MOBA_HEAD_DIM, 1)
        parts.append((o, lse))
    m = parts[0][1]
    for _, lse in parts[1:]:
        m = jnp.maximum(m, lse)
    num = jnp.zeros(od_ref.shape, F32)
    den = jnp.zeros(od_ref.shape, F32)
    for o, lse in parts:
        w = jnp.exp(lse - m)
        num = num + w * o
        den = den + w
    o_ref[...] = (num / den).astype(BF16)


def _moba_combine(od, lsed, gath, bsz, seq_len, tm=1024):
    tm = min(tm, seq_len)
    npair = od.shape[1] // LANES
    nt = seq_len // tm
    spec = pl.BlockSpec((tm, LANES), lambda b, p, i: (b * nt + i, p))

    def g_spec(e, s):
        return pl.BlockSpec((1, 1, 1, tm, LANES), lambda b, p, i: (b, 2 * p + e, s, i, 0))

    return pl.pallas_call(
        _moba_combine_kernel,
        grid=(bsz, npair, nt),
        in_specs=[spec, spec] + [g_spec(e, s) for e in range(2) for s in range(MOBA_TOPK)],
        out_specs=spec,
        out_shape=jax.ShapeDtypeStruct(od.shape, BF16),
        compiler_params=_params("parallel", "parallel", "parallel"),
        name="moba_combine",
    )(od, lsed, *([gath] * (2 * MOBA_TOPK)))


SC_WINDOW = 128
SC_CORES = 2
SC_SUBCORES = 16


def _sc_mesh():
    return plsc.VectorSubcoreMesh(core_axis_name="core", subcore_axis_name="subcore")


def _sc_gather_rows(table, idx):
    n = idx.shape[0]
    d = table.shape[1]
    window = SC_WINDOW
    assert n % (window * SC_CORES * SC_SUBCORES) == 0
    per_core = n // window // SC_CORES

    @functools.partial(pl.kernel, out_type=jax.ShapeDtypeStruct((n, d), table.dtype), mesh=_sc_mesh())
    def gather_kernel(x_hbm, i_hbm, o_hbm):
        base = lax.axis_index("core") * per_core

        def body(i_vmem, o_vmem):
            pltpu.sync_copy(x_hbm.at[i_vmem.at[0]], o_vmem)

        pltpu.emit_pipeline(
            body,
            grid=(per_core,),
            in_specs=[pl.BlockSpec((1, window), index_map=lambda i: (0, base + i))],
            out_specs=[pl.BlockSpec((window, d), index_map=lambda i: (base + i, 0))],
            core_axis_name="subcore",
            dimension_semantics=(pltpu.PARALLEL,),
            trace_scopes=False,
        )(i_hbm, o_hbm)

    return gather_kernel(table, idx.reshape(1, n))


def _sc_scatter_rows(rows, dest, n_out, repeat):
    n_src, d = rows.shape
    n = dest.shape[0]
    assert n == repeat * n_src
    window = SC_WINDOW
    assert n % (window * SC_CORES * SC_SUBCORES) == 0
    per_core = n // window // SC_CORES
    src_windows = n_src // window

    @functools.partial(pl.kernel, out_type=jax.ShapeDtypeStruct((n_out, d), rows.dtype), mesh=_sc_mesh())
    def scatter_kernel(x_hbm, i_hbm, o_hbm):
        base = lax.axis_index("core") * per_core

        def body(x_vmem, i_vmem):
            pltpu.sync_copy(x_vmem, o_hbm.at[i_vmem.at[0]])

        pltpu.emit_pipeline(
            body,
            grid=(per_core,),
            in_specs=[pl.BlockSpec((window, d), index_map=lambda i: ((base + i) % src_windows, 0)),
                      pl.BlockSpec((1, window), index_map=lambda i: (0, base + i))],
            out_specs=[],
            core_axis_name="subcore",
            dimension_semantics=(pltpu.PARALLEL,),
            trace_scopes=False,
        )(x_hbm, i_hbm)

    return scatter_kernel(rows, dest.reshape(1, n))


def _moba_dispatch(q2, qh, kv, bsz, seq_len):
    heads = q2.shape[1] // MOBA_HEAD_DIM
    nbk = MOBA_MAX_BLOCKS
    sel, cnt = _moba_select(q2, kv, bsz, seq_len)
    counts = cnt[..., 0].astype(jnp.int32).reshape(bsz * heads * nbk)
    pcounts = ((counts + MOBA_TILE - 1) // MOBA_TILE) * MOBA_TILE
    pends = jnp.cumsum(pcounts)
    pstarts = (pends - pcounts).reshape(bsz, heads, 1, 1, nbk)
    n_items = bsz * heads * seq_len * MOBA_TOPK
    step_rows = MOBA_TILE * MOBA_TILES_PER_STEP
    n_rows = -(-(n_items + bsz * heads * nbk * MOBA_TILE) // step_rows) * step_rows
    n_null = step_rows
    idx = sel[:, :, 0:MOBA_TOPK, :]
    rank = sel[:, :, MOBA_TOPK:2 * MOBA_TOPK, :]
    start = jnp.sum(jnp.where(idx[..., None] == jnp.arange(nbk), pstarts, 0), axis=-1)
    null_row = n_rows + jnp.arange(seq_len, dtype=jnp.int32) % n_null
    dest = jnp.where(idx >= 0, start + rank, null_row)
    n_tiles = (n_rows + n_null) // MOBA_TILE
    n_groups = bsz * heads * nbk
    tile_start = jnp.arange(n_tiles, dtype=jnp.int32) * MOBA_TILE
    tile_g = jnp.minimum(jnp.sum((pends[None, :] <= tile_start[:, None]).astype(jnp.int32), axis=1), n_groups - 1)
    tile_real = (tile_start < pends[-1]).astype(jnp.int32)
    tile_head = (tile_g // nbk) % heads
    tile_row = (tile_g // (heads * nbk)) * (seq_len // MOBA_BLOCK) + jnp.minimum(tile_g % nbk, seq_len // MOBA_BLOCK - 1)
    dest_by_slot = dest.transpose(2, 1, 0, 3).reshape(-1)
    qd = _sc_scatter_rows(qh.reshape(-1, LANES), dest_by_slot, n_rows + n_null, MOBA_TOPK)
    return qd, dest, (tile_row, tile_head // 2, tile_head % 2, tile_real)


def _moba_finish(qd, dest, tiles, od, lsed, kv, bsz, seq_len):
    heads = dest.shape[1]
    part = _moba_grouped(qd, kv, *tiles, lsed)
    gath = _sc_gather_rows(part, dest.reshape(-1)).reshape(bsz, heads, MOBA_TOPK, seq_len, LANES)
    return _moba_combine(od, lsed, gath, bsz, seq_len)


def _mem_kv_kernel(mem_ref, gmem_ref, w_ref, gck_ref, k_ref, v_ref):
    xf = mem_ref[0]
    ms = jnp.mean(xf * xf, axis=-1, keepdims=True)
    h = (xf * lax.rsqrt(ms + EPS) * gmem_ref[...]).astype(BF16)
    kv = _dot(h, w_ref[...])
    w = k_ref.shape[2]
    hd = w // X_HEADS
    for c in range(X_HEADS):
        chunk = kv[:, c * hd:(c + 1) * hd]
        cms = jnp.mean(chunk * chunk, axis=-1, keepdims=True)
        k_ref[0, :, c * hd:(c + 1) * hd] = (chunk * lax.rsqrt(cms + EPS) * gck_ref[...]).astype(BF16)
    v_ref[0] = kv[:, w:].astype(BF16)


def _mem_kv(mem, g_mem, w_kv_mem, g_ck):
    bsz, m, d = mem.shape
    w = w_kv_mem.shape[1] // 2
    const = lambda b: (0, 0)
    out = jax.ShapeDtypeStruct((bsz, m, w), BF16)
    return pl.pallas_call(
        _mem_kv_kernel,
        grid=(bsz,),
        in_specs=[pl.BlockSpec((1, m, d), lambda b: (b, 0, 0)), pl.BlockSpec((1, d), const),
                  pl.BlockSpec((d, 2 * w), const), pl.BlockSpec((1, w // X_HEADS), const)],
        out_specs=[pl.BlockSpec((1, m, w), lambda b: (b, 0, 0))] * 2,
        out_shape=[out, out],
        compiler_params=_params("parallel"),
        name="mem_kv",
    )(mem, g_mem.reshape(1, d), w_kv_mem.astype(BF16), g_ck.reshape(1, -1))


def _merge_kernel(x_ref, ys_ref, u_ref, dskip_ref, om_ref, xq_ref, kc_ref, vc_ref, g_ref,
                  wglu_ref, wmo_ref, wco_ref, wout_ref, gffn_ref, wr_ref, br_ref,
                  x1_ref, h2_ref, logit_ref):
    d = x_ref.shape[1]
    y = ys_ref[...].astype(F32) + dskip_ref[...] * u_ref[...].astype(F32)
    ge = 0.5 * y * (1.0 + jnp.tanh(math.sqrt(2.0 / math.pi) * (y + 0.044715 * (y * y * y))))
    z = _dot(ge.astype(BF16), wglu_ref[...])
    merged = g_ref[:, 0:d].astype(F32) * (z[:, :d] * _sigmoid(z[:, d:]))
    merged = merged + g_ref[:, d:2 * d].astype(F32) * _dot(om_ref[...], wmo_ref[...])
    w = xq_ref.shape[1]
    hd = w // X_HEADS
    heads = []
    for c in range(X_HEADS):
        s = _dot_nt(xq_ref[:, c * hd:(c + 1) * hd], kc_ref[0, :, c * hd:(c + 1) * hd]) * (hd ** -0.5)
        p = jnp.exp(s - jnp.max(s, axis=1, keepdims=True))
        p = p / jnp.sum(p, axis=1, keepdims=True)
        heads.append(_dot(p.astype(BF16), vc_ref[0, :, c * hd:(c + 1) * hd]))
    oc = jnp.concatenate(heads, axis=1).astype(BF16)
    merged = merged + g_ref[:, 2 * d:3 * d].astype(F32) * _dot(oc, wco_ref[...])
    x1 = x_ref[...] + _dot(merged.astype(BF16), wout_ref[...])
    x1_ref[...] = x1
    ms = jnp.mean(x1 * x1, axis=-1, keepdims=True)
    h2 = (x1 * lax.rsqrt(ms + EPS) * gffn_ref[...]).astype(BF16)
    packed = _pack_bf16_pairs(h2.astype(F32))
    for j in range(h2_ref.shape[0]):
        h2_ref[j] = packed[:, j * LANES:(j + 1) * LANES]
    logit_ref[...] = _dot_nt(wr_ref[...], h2) + br_ref[...]


def _merge(xt, ys, u, d_skip, om, xqn, kc, vc, gates, w_glu, w_mo, w_co, w_out, g_ffn, w_router, b_router,
           seq_len, tm=512):
    t, d = xt.shape
    tm = min(tm, seq_len)
    w = d // 2
    m = kc.shape[1]
    ne = w_router.shape[1]
    nt = seq_len // tm
    row = lambda i: (i, 0)
    const = lambda i: (0, 0)
    per_b = lambda i: (i // nt, 0, 0)
    once = dict(pipeline_mode=pl.Buffered(1))
    return pl.pallas_call(
        _merge_kernel,
        grid=(t // tm,),
        in_specs=[
            pl.BlockSpec((tm, d), row), pl.BlockSpec((tm, w), row), pl.BlockSpec((tm, w), row),
            pl.BlockSpec((1, w), const), pl.BlockSpec((tm, w), row), pl.BlockSpec((tm, w), row),
            pl.BlockSpec((1, m, w), per_b), pl.BlockSpec((1, m, w), per_b),
            pl.BlockSpec((tm, N_BRANCH * d), row),
            pl.BlockSpec((w, 2 * d), const, **once), pl.BlockSpec((w, d), const, **once),
            pl.BlockSpec((w, d), const, **once), pl.BlockSpec((d, d), const, **once), pl.BlockSpec((1, d), const),
            pl.BlockSpec((ne, d), const), pl.BlockSpec((ne, 1), const),
        ],
        out_specs=[pl.BlockSpec((tm, d), row), pl.BlockSpec((d // 2 // LANES, tm, LANES), lambda i: (0, i, 0)),
                   pl.BlockSpec((ne, tm), lambda i: (0, i))],
        out_shape=[jax.ShapeDtypeStruct((t, d), F32), jax.ShapeDtypeStruct((d // 2 // LANES, t, LANES), jnp.uint32),
                   jax.ShapeDtypeStruct((ne, t), F32)],
        compiler_params=_params("parallel"),
        name="merge",
    )(xt, ys, u, d_skip.reshape(1, w), om, xqn, kc, vc, gates,
      w_glu.astype(BF16), w_mo.astype(BF16), w_co.astype(BF16), w_out.astype(BF16),
      g_ffn.reshape(1, d), w_router.T.astype(BF16), b_router.reshape(ne, 1))


def _moe_kernel(blk_e_ref, blk_used_ref, xs_ref, wgu_ref, bgu_ref, wd_ref, bd_ref, y_ref, wgu_bf, wd_bf):
    i = pl.program_id(0)
    prev = blk_e_ref[jnp.maximum(i - 1, 0)]

    @pl.when((i == 0) | (blk_e_ref[i] != prev))
    def _():
        wgu_bf[...] = wgu_ref[0].astype(BF16)
        wd_bf[...] = wd_ref[0].astype(BF16)

    @pl.when(blk_used_ref[i] > 0)
    def _():
        de = wd_bf.shape[0]
        words = jnp.concatenate([xs_ref[j] for j in range(xs_ref.shape[0])], axis=1)
        xs = jnp.concatenate(_unpack_bf16_pairs(words), axis=1).astype(BF16)
        gu = _dot(xs, wgu_bf[...]) + bgu_ref[0]
        gate = jnp.minimum(gu[:, :de], SWIGLU_LIMIT)
        up = jnp.clip(gu[:, de:], -SWIGLU_LIMIT, SWIGLU_LIMIT)
        act = gate * _sigmoid(SWIGLU_ALPHA * gate) * (up + 1.0)
        y = _dot(act.astype(BF16), wd_bf[...]) + bd_ref[0]
        packed = _pack_bf16_pairs(y.astype(BF16).astype(F32))
        for j in range(y_ref.shape[0]):
            y_ref[j] = packed[:, j * LANES:(j + 1) * LANES]

    @pl.when(blk_used_ref[i] == 0)
    def _():
        y_ref[...] = jnp.zeros(y_ref.shape, y_ref.dtype)


def _moe_experts(xs, blk_e, blk_used, w_gu, b_gu, w_down, b_down):
    slabs, p, _ = xs.shape
    d = 2 * slabs * LANES
    ne, _, de2 = w_gu.shape
    de = de2 // 2
    nblk = p // EXPERT_ROWS
    row_spec = pl.BlockSpec((slabs, EXPERT_ROWS, LANES), lambda i, e, n: (0, i, 0))
    grid_spec = pltpu.PrefetchScalarGridSpec(
        num_scalar_prefetch=2,
        grid=(nblk,),
        in_specs=[
            row_spec,
            pl.BlockSpec((1, d, de2), lambda i, e, n: (e[i], 0, 0)),
            pl.BlockSpec((1, 1, de2), lambda i, e, n: (e[i], 0, 0)),
            pl.BlockSpec((1, de, d), lambda i, e, n: (e[i], 0, 0)),
            pl.BlockSpec((1, 1, d), lambda i, e, n: (e[i], 0, 0)),
        ],
        out_specs=row_spec,
        scratch_shapes=[pltpu.VMEM((d, de2), BF16), pltpu.VMEM((de, d), BF16)],
    )
    return pl.pallas_call(
        _moe_kernel,
        grid_spec=grid_spec,
        out_shape=jax.ShapeDtypeStruct(xs.shape, jnp.uint32),
        compiler_params=_params("arbitrary"),
        name="moe_experts",
    )(blk_e, blk_used, xs, w_gu, b_gu.reshape(ne, 1, de2), w_down, b_down.reshape(ne, 1, d))


def _router_kernel(lg_ref, tri_ref, e_ref, w_ref, r_ref, cnt_out_ref, cnt_ref):
    t = pl.program_id(0)
    ne, ts = lg_ref.shape

    @pl.when(t == 0)
    def _():
        cnt_ref[...] = jnp.zeros(cnt_ref.shape, F32)

    g = lg_ref[...]
    eid = lax.broadcasted_iota(jnp.int32, g.shape, 0)
    selected = jnp.zeros(g.shape, jnp.bool_)
    picks = []
    for _ in range(TOPK_EXPERTS):
        mx = jnp.max(g, axis=0, keepdims=True)
        idx = jnp.min(jnp.where(g == mx, eid, ne), axis=0, keepdims=True)
        hit = eid == idx
        picks.append((hit, idx, mx))
        selected = selected | hit
        g = jnp.where(hit, -jnp.inf, g)
    exps = [jnp.exp(mx - picks[0][2]) for (_, _, mx) in picks]
    total = sum(exps)
    sel01 = jnp.where(selected, 1.0, 0.0).astype(BF16)
    prefix = _dot(sel01, tri_ref[...]) + jnp.tile(cnt_ref[...], (1, ts // LANES))
    pad_i = [jnp.zeros((1, ts), jnp.int32)] * (8 - TOPK_EXPERTS)
    e_ref[...] = jnp.concatenate([idx for (_, idx, _) in picks] + pad_i, axis=0)
    w_ref[...] = jnp.concatenate([e / total for e in exps] + [jnp.zeros((1, ts), F32)] * (8 - TOPK_EXPERTS), axis=0)
    r_ref[...] = jnp.concatenate(
        [jnp.sum(jnp.where(hit, prefix, 0.0), axis=0, keepdims=True).astype(jnp.int32) for (hit, _, _) in picks]
        + pad_i, axis=0)
    cnt_ref[...] = cnt_ref[...] + _dot(sel01, jnp.ones((ts, LANES), BF16))
    cnt_out_ref[...] = cnt_ref[...]


def _router(logits_t, ts=1024):
    ne, t = logits_t.shape
    ts = min(ts, t)
    tri = (jnp.arange(ts)[:, None] < jnp.arange(ts)[None, :]).astype(BF16)
    rows = pl.BlockSpec((8, ts), lambda i: (0, i))
    return pl.pallas_call(
        _router_kernel,
        grid=(t // ts,),
        in_specs=[pl.BlockSpec((ne, ts), lambda i: (0, i)), pl.BlockSpec((ts, ts), lambda i: (0, 0))],
        out_specs=[rows, rows, rows, pl.BlockSpec((ne, LANES), lambda i: (0, 0))],
        out_shape=[jax.ShapeDtypeStruct((8, t), jnp.int32), jax.ShapeDtypeStruct((8, t), F32),
                   jax.ShapeDtypeStruct((8, t), jnp.int32), jax.ShapeDtypeStruct((ne, LANES), F32)],
        scratch_shapes=[pltpu.VMEM((ne, LANES), F32)],
        compiler_params=_params("arbitrary"),
        name="moe_router",
    )(logits_t, tri)


def _moe_mix_kernel(x1_ref, w_ref, pk_ref, o_ref):
    slabs = pk_ref.shape[0]
    wts = w_ref[...]
    lo = [jnp.zeros((x1_ref.shape[0], LANES), F32)] * slabs
    hi = [jnp.zeros((x1_ref.shape[0], LANES), F32)] * slabs
    for k in range(TOPK_EXPERTS):
        wk = wts[:, k:k + 1]
        for j in range(slabs):
            a, b = _unpack_bf16_pairs(pk_ref[j, k])
            lo[j] = lo[j] + wk * a
            hi[j] = hi[j] + wk * b
    o_ref[...] = x1_ref[...] + jnp.concatenate(lo + hi, axis=1)


def _moe_mix(x1, weights_tk, picked, tm=512):
    t, d = x1.shape
    slabs = picked.shape[0]
    return pl.pallas_call(
        _moe_mix_kernel,
        grid=(t // tm,),
        in_specs=[pl.BlockSpec((tm, d), lambda i: (i, 0)),
                  pl.BlockSpec((tm, TOPK_EXPERTS), lambda i: (i, 0)),
                  pl.BlockSpec((slabs, TOPK_EXPERTS, tm, LANES), lambda i: (0, 0, i, 0))],
        out_specs=pl.BlockSpec((tm, d), lambda i: (i, 0)),
        out_shape=jax.ShapeDtypeStruct((t, d), F32),
        compiler_params=_params("parallel"),
        name="moe_mix",
    )(x1, weights_tk, picked)


def _moe_ffn(x1, h2, logits_t, w_gu, b_gu, w_down, b_down):
    t, d = x1.shape
    tk = t * TOPK_EXPERTS
    e8, w8, r8, cnt = _router(logits_t)
    top_e, weights, rank = e8[:TOPK_EXPERTS], w8[:TOPK_EXPERTS], r8[:TOPK_EXPERTS]
    counts = cnt[:, 0].astype(jnp.int32)
    pcounts = ((counts + EXPERT_ROWS - 1) // EXPERT_ROWS) * EXPERT_ROWS
    pends = jnp.cumsum(pcounts)
    pstarts = pends - pcounts
    is_e = top_e[..., None] == jnp.arange(N_EXPERTS, dtype=jnp.int32)
    dest_by_k = jnp.sum(jnp.where(is_e, pstarts, 0), axis=-1) + rank
    nblk = -(-tk // EXPERT_ROWS) + N_EXPERTS
    blk_start = jnp.arange(nblk, dtype=jnp.int32) * EXPERT_ROWS
    blk_e = jnp.minimum(jnp.sum((pends[None, :] <= blk_start[:, None]).astype(jnp.int32), axis=1), N_EXPERTS - 1)
    blk_used = (blk_start < pends[-1]).astype(jnp.int32)
    slabs = d // 2 // LANES
    p = nblk * EXPERT_ROWS
    slab_off = jnp.arange(slabs, dtype=jnp.int32) * p
    dest_kst = (dest_by_k[:, None, :] + slab_off[None, :, None]).reshape(-1)
    dest_skt = (dest_by_k[None, :, :] + slab_off[:, None, None]).reshape(-1)
    xs = _sc_scatter_rows(h2.reshape(slabs * t, LANES), dest_kst, slabs * p, TOPK_EXPERTS)
    ys = _moe_experts(xs.reshape(slabs, p, LANES), blk_e, blk_used, w_gu, b_gu, w_down, b_down)
    picked = _sc_gather_rows(ys.reshape(slabs * p, LANES), dest_skt).reshape(slabs, TOPK_EXPERTS, t, LANES)
    return _moe_mix(x1, weights.T, picked)


def kernel(x, mem, g_mix, w_in, lam_re, lam_im, log_dt, b_re, b_im, c_re, c_im, d_skip, w_glu, g_q, g_k, w_moba_out, g_mem, w_kv_mem, g_cq, g_ck, w_cross_out, w_out, g_ffn, w_router, b_router, w_gu, b_gu, w_down, b_down):
    bsz, seq_len, d = x.shape
    xt = x.reshape(bsz * seq_len, d)
    for l in range(g_mix.shape[0]):
        u, q2, kv, xqn, gates, qh = _in_proj(xt, g_mix[l], w_in[l], g_q[l], g_k[l], g_cq[l], seq_len)
        qd, dest, tiles = _moba_dispatch(q2, qh, kv, bsz, seq_len)
        mats = _s5_matrices(lam_re[l], lam_im[l], log_dt[l], b_re[l], b_im[l], c_re[l], c_im[l],
                            seq_len // S5_CHUNK)
        ys = _s5(u, mats, bsz, seq_len)
        od, lsed = _moba_diag(q2, kv, bsz, seq_len)
        kc, vc = _mem_kv(mem, g_mem[l], w_kv_mem[l], g_ck[l])
        om = _moba_finish(qd, dest, tiles, od, lsed, kv, bsz, seq_len)
        x1, h2, logits = _merge(xt, ys, u, d_skip[l], om, xqn, kc, vc, gates, w_glu[l], w_moba_out[l],
                                w_cross_out[l], w_out[l], g_ffn[l], w_router[l], b_router[l], seq_len)
        xt = _moe_ffn(x1, h2, logits, w_gu[l], b_gu[l], w_down[l], b_down[l])
    return xt.reshape(bsz, seq_len, d)
```

```python
import functools
import math

import jax
import jax.numpy as jnp
from jax import lax
from jax.experimental import pallas as pl
from jax.experimental.pallas import tpu as pltpu
from jax.experimental.pallas import tpu_sc as plsc

F32 = jnp.float32
BF16 = jnp.bfloat16

EPS = 1e-6
N_BRANCH = 3
SSM_GROUP = 16
SSM_STATE = 64
S5_CHUNK = 16
MOBA_HEAD_DIM = 64
MOBA_BLOCK = 256
MOBA_TOPK = 3
MOBA_MAX_BLOCKS = 64
ROPE_THETA = 10000.0
X_HEADS = 4
N_EXPERTS = 32
TOPK_EXPERTS = 4
SWIGLU_LIMIT = 7.0
SWIGLU_ALPHA = 1.702
EXPERT_ROWS = 512
NEG_BIG = -1e30
LANES = 128
VMEM_LIMIT_BYTES = 56 * 1024 * 1024


def _params(*sem):
    return pltpu.CompilerParams(dimension_semantics=sem, vmem_limit_bytes=VMEM_LIMIT_BYTES)


def _sigmoid(x):
    return 1.0 / (1.0 + jnp.exp(-x))


def _dot(a, b):
    return jnp.dot(a, b, preferred_element_type=F32)


def _pack_bf16_pairs(x):
    n = x.shape[1] // 2
    lo = lax.bitcast_convert_type(x[:, :n], jnp.uint32) >> 16
    hi = lax.bitcast_convert_type(x[:, n:], jnp.uint32) & jnp.uint32(0xFFFF0000)
    return lo | hi


def _unpack_bf16_pairs(w):
    lo = lax.bitcast_convert_type(w << 16, F32)
    hi = lax.bitcast_convert_type(w & jnp.uint32(0xFFFF0000), F32)
    return lo, hi


def _dot_nt(a, b):
    return lax.dot_general(a, b, (((1,), (1,)), ((), ())), preferred_element_type=F32)


def _inproj_kernel(x_ref, gmix_ref, wa_ref, wg_ref, e64_ref, gq_ref, gk_ref, gcq_ref, cos_ref, sin_ref,
                   u_ref, q_ref, kv_ref, xq_ref, g_ref, qh_ref):
    xf = x_ref[...]
    ms = jnp.mean(xf * xf, axis=-1, keepdims=True)
    h = (xf * lax.rsqrt(ms + EPS) * gmix_ref[...]).astype(BF16)
    a = _dot(h, wa_ref[...])
    w = u_ref.shape[1]
    u_ref[...] = a[:, :w]

    cos = jnp.tile(cos_ref[...], (1, w // LANES))
    sin = jnp.tile(sin_ref[...], (1, w // LANES))
    lane = lax.broadcasted_iota(jnp.int32, (xf.shape[0], w), 1)
    first_half = (lane % MOBA_HEAD_DIM) < (MOBA_HEAD_DIM // 2)

    def qk_norm_rope(raw, g):
        ss = _dot((raw * raw).astype(BF16), e64_ref[...])
        n = raw * lax.rsqrt(ss * (1.0 / MOBA_HEAD_DIM) + EPS) * g
        rot = jnp.where(first_half,
                        pltpu.roll(n, w - MOBA_HEAD_DIM // 2, 1),
                        pltpu.roll(n, MOBA_HEAD_DIM // 2, 1))
        return n * cos + rot * sin

    q = qk_norm_rope(a[:, w:2 * w], gq_ref[...]) * (MOBA_HEAD_DIM ** -0.5)
    q_ref[...] = q.astype(BF16)
    in_a = lax.broadcasted_iota(jnp.int32, (xf.shape[0], LANES), 1) < MOBA_HEAD_DIM
    for p in range(w // LANES):
        pair = q[:, p * LANES:(p + 1) * LANES].astype(BF16).astype(F32)
        qh_ref[2 * p] = jnp.where(in_a, pair, 0.0)
        qh_ref[2 * p + 1] = jnp.where(in_a, 0.0, pair)
    k = qk_norm_rope(a[:, 2 * w:3 * w], gk_ref[...]).astype(BF16)
    for p in range(w // LANES):
        kv_ref[:, (2 * p) * LANES:(2 * p + 1) * LANES] = k[:, p * LANES:(p + 1) * LANES]
        kv_ref[:, (2 * p + 1) * LANES:(2 * p + 2) * LANES] = a[:, 3 * w + p * LANES:3 * w + (p + 1) * LANES].astype(BF16)

    xq = a[:, 4 * w:5 * w]
    hd = w // X_HEADS
    for c in range(X_HEADS):
        chunk = xq[:, c * hd:(c + 1) * hd]
        cms = jnp.mean(chunk * chunk, axis=-1, keepdims=True)
        xq_ref[:, c * hd:(c + 1) * hd] = (chunk * lax.rsqrt(cms + EPS) * gcq_ref[...]).astype(BF16)

    d = xf.shape[1]
    for c in range(N_BRANCH):
        z = _dot(h, wg_ref[:, c * d:(c + 1) * d])
        g_ref[:, c * d:(c + 1) * d] = _sigmoid(z).astype(BF16)


def _in_proj(xt, g_mix, w_in, g_q, g_k, g_cq, seq_len, tm=512):
    t, d = xt.shape
    tm = min(tm, seq_len)
    w = d // 2
    wa = w_in[:, :5 * w].astype(BF16)
    wg = w_in[:, 5 * w:].astype(BF16)
    heads = w // MOBA_HEAD_DIM
    e64 = jnp.kron(jnp.eye(heads, dtype=F32), jnp.ones((MOBA_HEAD_DIM, MOBA_HEAD_DIM), F32)).astype(BF16)
    half = MOBA_HEAD_DIM // 2
    inv = ROPE_THETA ** (-jnp.arange(half, dtype=F32) / half)
    ang = jnp.arange(seq_len, dtype=F32)[:, None] * inv[None, :]
    cos = jnp.tile(jnp.cos(ang), (1, LANES // half))
    sin = jnp.tile(jnp.concatenate([-jnp.sin(ang), jnp.sin(ang)], axis=1), (1, LANES // MOBA_HEAD_DIM))
    nt = seq_len // tm
    row = lambda i: (i, 0)
    const = lambda i: (0, 0)
    out_w = jax.ShapeDtypeStruct((t, w), BF16)
    return pl.pallas_call(
        _inproj_kernel,
        grid=(t // tm,),
        in_specs=[
            pl.BlockSpec((tm, d), row),
            pl.BlockSpec((1, d), const),
            pl.BlockSpec((d, 5 * w), const, pipeline_mode=pl.Buffered(1)),
            pl.BlockSpec((d, N_BRANCH * d), const, pipeline_mode=pl.Buffered(1)),
            pl.BlockSpec((w, w), const, pipeline_mode=pl.Buffered(1)),
            pl.BlockSpec((1, w), const),
            pl.BlockSpec((1, w), const),
            pl.BlockSpec((1, w // X_HEADS), const),
            pl.BlockSpec((tm, LANES), lambda i: (i % nt, 0)),
            pl.BlockSpec((tm, LANES), lambda i: (i % nt, 0)),
        ],
        out_specs=[pl.BlockSpec((tm, w), row), pl.BlockSpec((tm, w), row), pl.BlockSpec((tm, 2 * w), row),
                   pl.BlockSpec((tm, w), row), pl.BlockSpec((tm, N_BRANCH * d), row),
                   pl.BlockSpec((heads, tm, LANES), lambda i: (0, i, 0))],
        out_shape=[jax.ShapeDtypeStruct((t, w), F32), out_w, jax.ShapeDtypeStruct((t, 2 * w), BF16), out_w,
                   jax.ShapeDtypeStruct((t, N_BRANCH * d), BF16),
                   jax.ShapeDtypeStruct((heads, t, LANES), F32)],
        compiler_params=_params("parallel"),
        name="in_proj",
    )(xt, g_mix.reshape(1, d), wa, wg, e64,
      jnp.tile(g_q, heads).reshape(1, w), jnp.tile(g_k, heads).reshape(1, w), g_cq.reshape(1, -1), cos, sin)


def _s5_matrices(lam_re, lam_im, log_dt, b_re, b_im, c_re, c_im, n_chunks):
    hp = lax.Precision.HIGHEST
    c = S5_CHUNK
    dt = jnp.exp(log_dt)[:, None]
    mag = jnp.exp(lam_re * dt)
    ar = mag * jnp.cos(lam_im * dt)
    ai = mag * jnp.sin(lam_im * dt)
    nr = ar - 1.0
    den = lam_re * lam_re + lam_im * lam_im
    cr = (nr * lam_re + ai * lam_im) / den
    ci = (ai * lam_re - nr * lam_im) / den
    bbr = cr[..., None] * b_re - ci[..., None] * b_im
    bbi = cr[..., None] * b_im + ci[..., None] * b_re

    def power(n):
        nf = n.astype(F32)[None, :, None]
        m = jnp.exp((lam_re * dt)[:, None, :] * nf)
        th = (lam_im * dt)[:, None, :] * nf
        return m * jnp.cos(th), m * jnp.sin(th)

    pr, pi = power(jnp.arange(c + 1))
    kbr = pr[..., None] * bbr[:, None] - pi[..., None] * bbi[:, None]
    kbi = pr[..., None] * bbi[:, None] + pi[..., None] * bbr[:, None]
    kk = (jnp.einsum('ghp,gtpc->gthc', c_re, kbr, precision=hp)
          - jnp.einsum('ghp,gtpc->gthc', c_im, kbi, precision=hp))
    tq = jnp.arange(c)
    g = kk.shape[0]
    rev = c - 1 - tq
    w_in = jnp.concatenate([kbr[:, rev].transpose(0, 1, 3, 2), kbi[:, rev].transpose(0, 1, 3, 2)], axis=-1)
    w_in = w_in.reshape(g, c * SSM_GROUP, 2 * SSM_STATE)
    prn, pin = pr[:, 1:], pi[:, 1:]
    wo_r = c_re[:, None] * prn[:, :, None, :] - c_im[:, None] * pin[:, :, None, :]
    wo_i = -c_re[:, None] * pin[:, :, None, :] - c_im[:, None] * prn[:, :, None, :]
    w_out = jnp.concatenate([wo_r, wo_i], axis=-1).transpose(0, 3, 1, 2).reshape(g, 2 * SSM_STATE, c * SSM_GROUP)
    n_steps = max(1, int(math.ceil(math.log2(n_chunks))))
    qr, qi = power(c * (2 ** jnp.arange(n_steps)))
    pa = jnp.concatenate([qr, qr], axis=-1)
    pb = jnp.concatenate([-qi, qi], axis=-1)
    return kk.astype(BF16), w_in.astype(BF16), w_out.astype(BF16), pa, pb


S5_SET = LANES // SSM_GROUP
S5_ROWS = 256


def _s5_kernel(u_ref, toep_ref, winc_ref, woutc_ref, pa_ref, pb_ref, y_ref, s_ref, win_ref, wout_ref):
    c = S5_CHUNK
    nc = u_ref.shape[0] // c
    rb = min(S5_ROWS, nc)
    p2 = 2 * SSM_STATE

    @pl.when(pl.program_id(1) == 0)
    def _():
        r_group = lax.broadcasted_iota(jnp.int32, (LANES, p2), 0) // SSM_GROUP
        l_group = lax.broadcasted_iota(jnp.int32, (p2, LANES), 1) // SSM_GROUP
        for t in range(c):
            for g in range(S5_SET):
                win_ref[0, t * LANES:(t + 1) * LANES, g * p2:(g + 1) * p2] = jnp.where(
                    r_group == g, winc_ref[0, t], jnp.zeros((LANES, p2), BF16))
                wout_ref[0, g * p2:(g + 1) * p2, t * LANES:(t + 1) * LANES] = jnp.where(
                    l_group == g, woutc_ref[0, t], jnp.zeros((p2, LANES), BF16))

    def chunk_rows(r0):
        return jnp.concatenate([u_ref[pl.ds(r0 * c + t, rb, stride=c), :] for t in range(c)], axis=1).astype(BF16)

    for blk in range(nc // rb):
        s_ref[blk * rb:(blk + 1) * rb, :] = _dot(chunk_rows(blk * rb), win_ref[0])
    row = lax.broadcasted_iota(jnp.int32, (nc, LANES), 0)
    n_steps = pa_ref.shape[1]
    for g in range(S5_SET):
        lanes = slice(g * LANES, (g + 1) * LANES)
        s = s_ref[:, lanes]
        for k in range(n_steps):
            sh = 1 << k
            if sh >= nc:
                break
            prev = jnp.where(row >= sh, pltpu.roll(s, sh, 0), 0.0)
            s = s + pa_ref[0, k:k + 1, lanes] * prev + pb_ref[0, k:k + 1, lanes] * pltpu.roll(prev, SSM_STATE, 1)
        s_ref[:, lanes] = jnp.where(row >= 1, pltpu.roll(s, 1, 0), 0.0)
    for blk in range(nc // rb):
        rows = slice(blk * rb, (blk + 1) * rb)
        a = chunk_rows(blk * rb)
        carried = _dot(s_ref[rows, :].astype(BF16), wout_ref[0])
        for t in range(c):
            y = _dot(a[:, :(t + 1) * LANES], toep_ref[0, (c - 1 - t) * LANES:, :])
            y_ref[pl.ds(blk * rb * c + t, rb, stride=c), :] = y + carried[:, t * LANES:(t + 1) * LANES]


def _s5_block_diag(mats):
    kk, w_in, w_out, pa, pb = mats
    g, c, h, p2 = kk.shape[0], S5_CHUNK, SSM_GROUP, 2 * SSM_STATE
    ns = g // S5_SET
    eye = jnp.eye(S5_SET, dtype=kk.dtype)
    lag_bd = jnp.einsum('sglhc,gk->slgckh', kk.reshape(ns, S5_SET, c + 1, h, h), eye).reshape(ns, c + 1, LANES, LANES)
    toep_bd = lag_bd[:, c - 1::-1].reshape(ns, c * LANES, LANES)

    w_in_c = w_in.reshape(ns, S5_SET, c, h, p2).transpose(0, 2, 1, 3, 4).reshape(ns, c, LANES, p2)
    w_out_c = w_out.reshape(ns, S5_SET, p2, c, h).transpose(0, 3, 2, 1, 4).reshape(ns, c, p2, LANES)
    k = pa.shape[1]
    lanes_of = lambda x: x.reshape(ns, S5_SET, k, p2).transpose(0, 2, 1, 3).reshape(ns, k, S5_SET * p2)
    return toep_bd, w_in_c, w_out_c, lanes_of(pa), lanes_of(pb)


def _s5(u, mats, bsz, seq_len):
    toep, w_in, w_out, pa, pb = _s5_block_diag(mats)
    ns = toep.shape[0]
    nc = seq_len // S5_CHUNK
    per_set = lambda s, b: (s, 0, 0)
    once = dict(pipeline_mode=pl.Buffered(1))
    io_spec = pl.BlockSpec((seq_len, LANES), lambda s, b: (b, s), **once)
    return pl.pallas_call(
        _s5_kernel,
        grid=(ns, bsz),
        in_specs=[
            io_spec,
            pl.BlockSpec((1,) + toep.shape[1:], per_set, **once),
            pl.BlockSpec((1,) + w_in.shape[1:], lambda s, b: (s, 0, 0, 0)),
            pl.BlockSpec((1,) + w_out.shape[1:], lambda s, b: (s, 0, 0, 0)),
            pl.BlockSpec((1,) + pa.shape[1:], per_set),
            pl.BlockSpec((1,) + pb.shape[1:], per_set),
        ],
        out_specs=io_spec,
        out_shape=jax.ShapeDtypeStruct(u.shape, F32),
        scratch_shapes=[pltpu.VMEM((nc, S5_SET * 2 * SSM_STATE), F32),
                        pltpu.VMEM((1, S5_CHUNK * LANES, S5_SET * 2 * SSM_STATE), BF16),
                        pltpu.VMEM((1, S5_SET * 2 * SSM_STATE, S5_CHUNK * LANES), BF16)],
        compiler_params=_params("arbitrary", "arbitrary"),
        name="s5_scan",
    )(u, toep, w_in, w_out, pa, pb)


def _moba_select_kernel(q_ref, k_ref, tri_ref, sel_ref, cnt_out_ref, km_ref, cnt_ref):
    h = pl.program_id(1)
    t = pl.program_id(2)
    ts = q_ref.shape[0]
    nbk = MOBA_MAX_BLOCKS

    @pl.when(t == 0)
    def _():
        kk = k_ref[...].astype(F32)
        nb = kk.shape[0] // MOBA_BLOCK
        km = jnp.sum(kk.reshape(nb, MOBA_BLOCK, LANES), axis=1) * (1.0 / MOBA_BLOCK)
        if nb < nbk:
            km = jnp.concatenate([km, jnp.zeros((nbk - nb, LANES), F32)], axis=0)
        lane = lax.broadcasted_iota(jnp.int32, (nbk, LANES), 1)
        km_ref[...] = jnp.where((lane // MOBA_HEAD_DIM) == (h % 2), km, 0.0).astype(BF16)
        cnt_ref[...] = jnp.zeros(cnt_ref.shape, F32)

    gate = _dot_nt(km_ref[...], q_ref[...])
    blk = lax.broadcasted_iota(jnp.int32, gate.shape, 0)
    qblk = (t * ts + lax.broadcasted_iota(jnp.int32, gate.shape, 1)) // MOBA_BLOCK
    g = jnp.where(blk < qblk, gate, -jnp.inf)
    selected = jnp.zeros(gate.shape, jnp.bool_)
    picks = []
    for _ in range(MOBA_TOPK):
        mx = jnp.max(g, axis=0, keepdims=True)
        idx = jnp.min(jnp.where(g == mx, blk, nbk), axis=0, keepdims=True)
        hit = blk == idx
        ok = (idx[0:1] < qblk[0:1]) & (mx > -jnp.inf)
        picks.append((hit, idx, ok))
        selected = selected | (hit & ok)
        g = jnp.where(hit, -jnp.inf, g)
    sel01 = jnp.where(selected, 1.0, 0.0)
    chunks = [sel01[:, c * LANES:(c + 1) * LANES] for c in range(ts // LANES)]
    within = _dot(jnp.concatenate(chunks, axis=0).astype(BF16), tri_ref[...])
    base = cnt_ref[...]
    pieces = []
    for c, chunk in enumerate(chunks):
        pieces.append(within[c * nbk:(c + 1) * nbk] + base)
        base = base + jnp.sum(chunk, axis=1, keepdims=True)
    prefix = jnp.concatenate(pieces, axis=1)
    rows = [jnp.where(ok, idx, -1) for (_, idx, ok) in picks]
    rows += [jnp.sum(jnp.where(hit, prefix, 0.0), axis=0, keepdims=True).astype(jnp.int32) for (hit, _, _) in picks]
    rows += [jnp.zeros((1, ts), jnp.int32)] * (8 - 2 * MOBA_TOPK)
    sel_ref[0, 0] = jnp.concatenate(rows, axis=0)
    cnt_ref[...] = base
    cnt_out_ref[0, 0] = base


def _moba_select(q2, kv, bsz, seq_len, ts=1024):
    ts = min(ts, seq_len)
    heads = q2.shape[1] // MOBA_HEAD_DIM
    nt = seq_len // ts
    tri = (jnp.arange(LANES)[:, None] < jnp.arange(LANES)[None, :]).astype(BF16)
    return pl.pallas_call(
        _moba_select_kernel,
        grid=(bsz, heads, nt),
        in_specs=[
            pl.BlockSpec((ts, LANES), lambda b, h, t: (b * nt + t, h // 2)),
            pl.BlockSpec((seq_len, LANES), lambda b, h, t: (b, 2 * (h // 2))),
            pl.BlockSpec((LANES, LANES), lambda b, h, t: (0, 0)),
        ],
        out_specs=[pl.BlockSpec((1, 1, 8, ts), lambda b, h, t: (b, h, 0, t)),
                   pl.BlockSpec((1, 1, MOBA_MAX_BLOCKS, LANES), lambda b, h, t: (b, h, 0, 0))],
        out_shape=[jax.ShapeDtypeStruct((bsz, heads, 8, seq_len), jnp.int32),
                   jax.ShapeDtypeStruct((bsz, heads, MOBA_MAX_BLOCKS, LANES), F32)],
        scratch_shapes=[pltpu.VMEM((MOBA_MAX_BLOCKS, LANES), BF16), pltpu.VMEM((MOBA_MAX_BLOCKS, LANES), F32)],
        compiler_params=_params("parallel", "arbitrary", "arbitrary"),
        name="moba_select",
    )(q2, kv, tri)


MOBA_DIAG_BLOCKS = 4


def _moba_diag_kernel(q_ref, kv_ref, o_ref, lse_ref):
    lane = lax.broadcasted_iota(jnp.int32, (MOBA_BLOCK, LANES), 1)
    is_a = lane < MOBA_HEAD_DIM
    r = lax.broadcasted_iota(jnp.int32, (MOBA_BLOCK, MOBA_BLOCK), 0)
    c = lax.broadcasted_iota(jnp.int32, (MOBA_BLOCK, MOBA_BLOCK), 1)
    for u in range(MOBA_DIAG_BLOCKS):
        rows = pl.ds(u * MOBA_BLOCK, MOBA_BLOCK)
        q = q_ref[rows, :]
        kb = kv_ref[rows, :LANES]
        vb = kv_ref[rows, LANES:]
        pvs, ms = [], []
        for own in (is_a, jnp.logical_not(is_a)):
            s = jnp.where(c <= r, _dot_nt(jnp.where(own, q, jnp.zeros_like(q)), kb), NEG_BIG)
            m = jnp.max(s, axis=1, keepdims=True)
            p = jnp.exp(s - m).astype(BF16)
            pvs.append(_dot(p, jnp.where(own, vb, 1.0)))
            ms.append(m)
        num = jnp.where(is_a, pvs[0], pvs[1])
        den = pltpu.roll(jnp.where(is_a, pvs[1], pvs[0]), MOBA_HEAD_DIM, 1)
        o_ref[rows, :] = num / den
        lse_ref[rows, :] = jnp.where(is_a, ms[0], ms[1]) + jnp.log(den)


def _moba_diag(q2, kv, bsz, seq_len):
    npair = q2.shape[1] // LANES
    rows = MOBA_BLOCK * MOBA_DIAG_BLOCKS
    nb = seq_len // rows
    spec = pl.BlockSpec((rows, LANES), lambda b, p, i: (b * nb + i, p))
    out = jax.ShapeDtypeStruct(q2.shape, F32)
    return pl.pallas_call(
        _moba_diag_kernel,
        grid=(bsz, npair, nb),
        in_specs=[spec, pl.BlockSpec((rows, 2 * LANES), lambda b, p, i: (b * nb + i, p))],
        out_specs=[spec, spec],
        out_shape=[out, out],
        compiler_params=_params("parallel", "parallel", "parallel"),
        name="moba_diag",
    )(q2, kv)


MOBA_TILE = 256
MOBA_TILES_PER_STEP = 8


def _moba_grouped_kernel(tile_row_ref, tile_pair_ref, tile_half_ref, tile_real_ref, qd_ref, *refs):
    n = MOBA_TILES_PER_STEP
    kv_refs, o_ref = refs[:n], refs[n + 1]
    i = pl.program_id(0)
    lane = lax.broadcasted_iota(jnp.int32, (MOBA_TILE, LANES), 1)
    kv_lane = lax.broadcasted_iota(jnp.int32, (MOBA_BLOCK, LANES), 1)

    @pl.when(tile_real_ref[i * n] > 0)
    def _():
        for u in range(n):
            tile = i * n + u
            rows = pl.ds(u * MOBA_TILE, MOBA_TILE)
            own = (lane // MOBA_HEAD_DIM) == tile_half_ref[tile]
            s = _dot_nt(qd_ref[rows, :].astype(BF16), kv_refs[u][:, :LANES])
            m = jnp.max(s, axis=1, keepdims=True)
            p = jnp.exp(s - m).astype(BF16)
            vb = jnp.where((kv_lane // MOBA_HEAD_DIM) == tile_half_ref[tile], kv_refs[u][:, LANES:], 1.0)
            pv = _dot(p, vb)
            row_sum = pltpu.roll(pv, MOBA_HEAD_DIM, 1)
            part = jnp.where(own, pv / row_sum, m + jnp.log(pv))
            o_ref[rows, :] = jnp.where(tile_real_ref[tile] > 0, part, NEG_BIG)

    @pl.when(tile_real_ref[i * n] == 0)
    def _():
        o_ref[...] = jnp.full(o_ref.shape, NEG_BIG, F32)


def _moba_grouped(qd, kv, tile_row, tile_pair, tile_half, tile_real, run_after):
    n = MOBA_TILES_PER_STEP
    n_tiles = qd.shape[0] // MOBA_TILE

    def kv_spec(u):
        return pl.BlockSpec((MOBA_BLOCK, 2 * LANES), lambda i, tr, tp, th, tl: (tr[i * n + u], tp[i * n + u]))

    grid_spec = pltpu.PrefetchScalarGridSpec(
        num_scalar_prefetch=4,
        grid=(n_tiles // n,),
        in_specs=[pl.BlockSpec((n * MOBA_TILE, LANES), lambda i, tr, tp, th, tl: (i, 0))]
        + [kv_spec(u) for u in range(n)]
        + [pl.BlockSpec((8, LANES), lambda i, tr, tp, th, tl: (0, 0))],
        out_specs=pl.BlockSpec((n * MOBA_TILE, LANES), lambda i, tr, tp, th, tl: (i, 0)),
    )
    return pl.pallas_call(
        _moba_grouped_kernel,
        grid_spec=grid_spec,
        out_shape=jax.ShapeDtypeStruct(qd.shape, F32),
        compiler_params=_params("arbitrary"),
        name="moba_grouped",
    )(tile_row, tile_pair, tile_half, tile_real, qd, *([kv] * n), run_after)


def _moba_combine_kernel(od_ref, lsed_ref, *refs):
    g_refs, o_ref = refs[:-1], refs[-1]
    lane = lax.broadcasted_iota(jnp.int32, od_ref.shape, 1)
    is_a = lane < MOBA_HEAD_DIM
    parts = [(od_ref[...], lsed_ref[...])]
    for s in range(MOBA_TOPK):
        xa = g_refs[s][0, 0, 0]
        xb = g_refs[MOBA_TOPK + s][0, 0, 0]
        o = jnp.where(is_a, xa, xb)
        lse = pltpu.roll(jnp.where(is_a, xb, xa), MOBA_HEAD_DIM, 1)
        parts.append((o, lse))
    m = parts[0][1]
    for _, lse in parts[1:]:
        m = jnp.maximum(m, lse)
    num = jnp.zeros(od_ref.shape, F32)
    den = jnp.zeros(od_ref.shape, F32)
    for o, lse in parts:
        w = jnp.exp(lse - m)
        num = num + w * o
        den = den + w
    o_ref[...] = (num / den).astype(BF16)


def _moba_combine(od, lsed, gath, bsz, seq_len, tm=1024):
    tm = min(tm, seq_len)
    npair = od.shape[1] // LANES
    nt = seq_len // tm
    spec = pl.BlockSpec((tm, LANES), lambda b, p, i: (b * nt + i, p))

    def g_spec(e, s):
        return pl.BlockSpec((1, 1, 1, tm, LANES), lambda b, p, i: (b, 2 * p + e, s, i, 0))

    return pl.pallas_call(
        _moba_combine_kernel,
        grid=(bsz, npair, nt),
        in_specs=[spec, spec] + [g_spec(e, s) for e in range(2) for s in range(MOBA_TOPK)],
        out_specs=spec,
        out_shape=jax.ShapeDtypeStruct(od.shape, BF16),
        compiler_params=_params("parallel", "parallel", "parallel"),
        name="moba_combine",
    )(od, lsed, *([gath] * (2 * MOBA_TOPK)))


SC_WINDOW = 128
SC_CORES = 2
SC_SUBCORES = 16


def _sc_mesh():
    return plsc.VectorSubcoreMesh(core_axis_name="core", subcore_axis_name="subcore")


def _sc_gather_rows(table, idx):
    n = idx.shape[0]
    d = table.shape[1]
    window = SC_WINDOW
    assert n % (window * SC_CORES * SC_SUBCORES) == 0
    per_core = n // window // SC_CORES

    @functools.partial(pl.kernel, out_type=jax.ShapeDtypeStruct((n, d), table.dtype), mesh=_sc_mesh())
    def gather_kernel(x_hbm, i_hbm, o_hbm):
        base = lax.axis_index("core") * per_core

        def body(i_vmem, o_vmem):
            pltpu.sync_copy(x_hbm.at[i_vmem.at[0]], o_vmem)

        pltpu.emit_pipeline(
            body,
            grid=(per_core,),
            in_specs=[pl.BlockSpec((1, window), index_map=lambda i: (0, base + i))],
            out_specs=[pl.BlockSpec((window, d), index_map=lambda i: (base + i, 0))],
            core_axis_name="subcore",
            dimension_semantics=(pltpu.PARALLEL,),
            trace_scopes=False,
        )(i_hbm, o_hbm)

    return gather_kernel(table, idx.reshape(1, n))


def _sc_scatter_rows(rows, dest, n_out, repeat):
    n_src, d = rows.shape
    n = dest.shape[0]
    assert n == repeat * n_src
    window = SC_WINDOW
    assert n % (window * SC_CORES * SC_SUBCORES) == 0
    per_core = n // window // SC_CORES
    src_windows = n_src // window

    @functools.partial(pl.kernel, out_type=jax.ShapeDtypeStruct((n_out, d), rows.dtype), mesh=_sc_mesh())
    def scatter_kernel(x_hbm, i_hbm, o_hbm):
        base = lax.axis_index("core") * per_core

        def body(x_vmem, i_vmem):
            pltpu.sync_copy(x_vmem, o_hbm.at[i_vmem.at[0]])

        pltpu.emit_pipeline(
            body,
            grid=(per_core,),
            in_specs=[pl.BlockSpec((window, d), index_map=lambda i: ((base + i) % src_windows, 0)),
                      pl.BlockSpec((1, window), index_map=lambda i: (0, base + i))],
            out_specs=[],
            core_axis_name="subcore",
            dimension_semantics=(pltpu.PARALLEL,),
            trace_scopes=False,
        )(x_hbm, i_hbm)

    return scatter_kernel(rows, dest.reshape(1, n))


def _moba_dispatch(q2, qh, kv, bsz, seq_len):
    heads = q2.shape[1] // MOBA_HEAD_DIM
    nbk = MOBA_MAX_BLOCKS
    sel, cnt = _moba_select(q2, kv, bsz, seq_len)
    counts = cnt[..., 0].astype(jnp.int32).reshape(bsz * heads * nbk)
    pcounts = ((counts + MOBA_TILE - 1) // MOBA_TILE) * MOBA_TILE
    pends = jnp.cumsum(pcounts)
    pstarts = (pends - pcounts).reshape(bsz, heads, 1, 1, nbk)
    n_items = bsz * heads * seq_len * MOBA_TOPK
    step_rows = MOBA_TILE * MOBA_TILES_PER_STEP
    n_rows = -(-(n_items + bsz * heads * nbk * MOBA_TILE) // step_rows) * step_rows
    n_null = step_rows
    idx = sel[:, :, 0:MOBA_TOPK, :]
    rank = sel[:, :, MOBA_TOPK:2 * MOBA_TOPK, :]
    start = jnp.sum(jnp.where(idx[..., None] == jnp.arange(nbk), pstarts, 0), axis=-1)
    null_row = n_rows + jnp.arange(seq_len, dtype=jnp.int32) % n_null
    dest = jnp.where(idx >= 0, start + rank, null_row)
    n_tiles = (n_rows + n_null) // MOBA_TILE
    n_groups = bsz * heads * nbk
    tile_start = jnp.arange(n_tiles, dtype=jnp.int32) * MOBA_TILE
    tile_g = jnp.minimum(jnp.sum((pends[None, :] <= tile_start[:, None]).astype(jnp.int32), axis=1), n_groups - 1)
    tile_real = (tile_start < pends[-1]).astype(jnp.int32)
    tile_head = (tile_g // nbk) % heads
    tile_row = (tile_g // (heads * nbk)) * (seq_len // MOBA_BLOCK) + jnp.minimum(tile_g % nbk, seq_len // MOBA_BLOCK - 1)
    dest_by_slot = dest.transpose(2, 1, 0, 3).reshape(-1)
    qd = _sc_scatter_rows(qh.reshape(-1, LANES), dest_by_slot, n_rows + n_null, MOBA_TOPK)
    return qd, dest, (tile_row, tile_head // 2, tile_head % 2, tile_real)


def _moba_finish(qd, dest, tiles, od, lsed, kv, bsz, seq_len):
    heads = dest.shape[1]
    part = _moba_grouped(qd, kv, *tiles, lsed)
    gath = _sc_gather_rows(part, dest.reshape(-1)).reshape(bsz, heads, MOBA_TOPK, seq_len, LANES)
    return _moba_combine(od, lsed, gath, bsz, seq_len)


def _mem_kv_kernel(mem_ref, gmem_ref, w_ref, gck_ref, k_ref, v_ref):
    xf = mem_ref[0]
    ms = jnp.mean(xf * xf, axis=-1, keepdims=True)
    h = (xf * lax.rsqrt(ms + EPS) * gmem_ref[...]).astype(BF16)
    kv = _dot(h, w_ref[...])
    w = k_ref.shape[2]
    hd = w // X_HEADS
    for c in range(X_HEADS):
        chunk = kv[:, c * hd:(c + 1) * hd]
        cms = jnp.mean(chunk * chunk, axis=-1, keepdims=True)
        k_ref[0, :, c * hd:(c + 1) * hd] = (chunk * lax.rsqrt(cms + EPS) * gck_ref[...]).astype(BF16)
    v_ref[0] = kv[:, w:].astype(BF16)


def _mem_kv(mem, g_mem, w_kv_mem, g_ck):
    bsz, m, d = mem.shape
    w = w_kv_mem.shape[1] // 2
    const = lambda b: (0, 0)
    out = jax.ShapeDtypeStruct((bsz, m, w), BF16)
    return pl.pallas_call(
        _mem_kv_kernel,
        grid=(bsz,),
        in_specs=[pl.BlockSpec((1, m, d), lambda b: (b, 0, 0)), pl.BlockSpec((1, d), const),
                  pl.BlockSpec((d, 2 * w), const), pl.BlockSpec((1, w // X_HEADS), const)],
        out_specs=[pl.BlockSpec((1, m, w), lambda b: (b, 0, 0))] * 2,
        out_shape=[out, out],
        compiler_params=_params("parallel"),
        name="mem_kv",
    )(mem, g_mem.reshape(1, d), w_kv_mem.astype(BF16), g_ck.reshape(1, -1))


def _merge_kernel(x_ref, ys_ref, u_ref, dskip_ref, om_ref, xq_ref, kc_ref, vc_ref, g_ref,
                  wglu_ref, wmo_ref, wco_ref, wout_ref, gffn_ref, wr_ref, br_ref,
                  x1_ref, h2_ref, logit_ref):
    d = x_ref.shape[1]
    y = ys_ref[...].astype(F32) + dskip_ref[...] * u_ref[...].astype(F32)
    ge = 0.5 * y * (1.0 + jnp.tanh(math.sqrt(2.0 / math.pi) * (y + 0.044715 * (y * y * y))))
    z = _dot(ge.astype(BF16), wglu_ref[...])
    merged = g_ref[:, 0:d].astype(F32) * (z[:, :d] * _sigmoid(z[:, d:]))
    merged = merged + g_ref[:, d:2 * d].astype(F32) * _dot(om_ref[...], wmo_ref[...])
    w = xq_ref.shape[1]
    hd = w // X_HEADS
    heads = []
    for c in range(X_HEADS):
        s = _dot_nt(xq_ref[:, c * hd:(c + 1) * hd], kc_ref[0, :, c * hd:(c + 1) * hd]) * (hd ** -0.5)
        p = jnp.exp(s - jnp.max(s, axis=1, keepdims=True))
        p = p / jnp.sum(p, axis=1, keepdims=True)
        heads.append(_dot(p.astype(BF16), vc_ref[0, :, c * hd:(c + 1) * hd]))
    oc = jnp.concatenate(heads, axis=1).astype(BF16)
    merged = merged + g_ref[:, 2 * d:3 * d].astype(F32) * _dot(oc, wco_ref[...])
    x1 = x_ref[...] + _dot(merged.astype(BF16), wout_ref[...])
    x1_ref[...] = x1
    ms = jnp.mean(x1 * x1, axis=-1, keepdims=True)
    h2 = (x1 * lax.rsqrt(ms + EPS) * gffn_ref[...]).astype(BF16)
    packed = _pack_bf16_pairs(h2.astype(F32))
    for j in range(h2_ref.shape[0]):
        h2_ref[j] = packed[:, j * LANES:(j + 1) * LANES]
    logit_ref[...] = _dot_nt(wr_ref[...], h2) + br_ref[...]


def _merge(xt, ys, u, d_skip, om, xqn, kc, vc, gates, w_glu, w_mo, w_co, w_out, g_ffn, w_router, b_router,
           seq_len, tm=512):
    t, d = xt.shape
    tm = min(tm, seq_len)
    w = d // 2
    m = kc.shape[1]
    ne = w_router.shape[1]
    nt = seq_len // tm
    row = lambda i: (i, 0)
    const = lambda i: (0, 0)
    per_b = lambda i: (i // nt, 0, 0)
    once = dict(pipeline_mode=pl.Buffered(1))
    return pl.pallas_call(
        _merge_kernel,
        grid=(t // tm,),
        in_specs=[
            pl.BlockSpec((tm, d), row), pl.BlockSpec((tm, w), row), pl.BlockSpec((tm, w), row),
            pl.BlockSpec((1, w), const), pl.BlockSpec((tm, w), row), pl.BlockSpec((tm, w), row),
            pl.BlockSpec((1, m, w), per_b), pl.BlockSpec((1, m, w), per_b),
            pl.BlockSpec((tm, N_BRANCH * d), row),
            pl.BlockSpec((w, 2 * d), const, **once), pl.BlockSpec((w, d), const, **once),
            pl.BlockSpec((w, d), const, **once), pl.BlockSpec((d, d), const, **once), pl.BlockSpec((1, d), const),
            pl.BlockSpec((ne, d), const), pl.BlockSpec((ne, 1), const),
        ],
        out_specs=[pl.BlockSpec((tm, d), row), pl.BlockSpec((d // 2 // LANES, tm, LANES), lambda i: (0, i, 0)),
                   pl.BlockSpec((ne, tm), lambda i: (0, i))],
        out_shape=[jax.ShapeDtypeStruct((t, d), F32), jax.ShapeDtypeStruct((d // 2 // LANES, t, LANES), jnp.uint32),
                   jax.ShapeDtypeStruct((ne, t), F32)],
        compiler_params=_params("parallel"),
        name="merge",
    )(xt, ys, u, d_skip.reshape(1, w), om, xqn, kc, vc, gates,
      w_glu.astype(BF16), w_mo.astype(BF16), w_co.astype(BF16), w_out.astype(BF16),
      g_ffn.reshape(1, d), w_router.T.astype(BF16), b_router.reshape(ne, 1))


def _moe_kernel(blk_e_ref, blk_used_ref, xs_ref, wgu_ref, bgu_ref, wd_ref, bd_ref, y_ref, wgu_bf, wd_bf):
    i = pl.program_id(0)
    prev = blk_e_ref[jnp.maximum(i - 1, 0)]

    @pl.when((i == 0) | (blk_e_ref[i] != prev))
    def _():
        wgu_bf[...] = wgu_ref[0].astype(BF16)
        wd_bf[...] = wd_ref[0].astype(BF16)

    @pl.when(blk_used_ref[i] > 0)
    def _():
        de = wd_bf.shape[0]
        words = jnp.concatenate([xs_ref[j] for j in range(xs_ref.shape[0])], axis=1)
        xs = jnp.concatenate(_unpack_bf16_pairs(words), axis=1).astype(BF16)
        gu = _dot(xs, wgu_bf[...]) + bgu_ref[0]
        gate = jnp.minimum(gu[:, :de], SWIGLU_LIMIT)
        up = jnp.clip(gu[:, de:], -SWIGLU_LIMIT, SWIGLU_LIMIT)
        act = gate * _sigmoid(SWIGLU_ALPHA * gate) * (up + 1.0)
        y = _dot(act.astype(BF16), wd_bf[...]) + bd_ref[0]
        packed = _pack_bf16_pairs(y.astype(BF16).astype(F32))
        for j in range(y_ref.shape[0]):
            y_ref[j] = packed[:, j * LANES:(j + 1) * LANES]

    @pl.when(blk_used_ref[i] == 0)
    def _():
        y_ref[...] = jnp.zeros(y_ref.shape, y_ref.dtype)


def _moe_experts(xs, blk_e, blk_used, w_gu, b_gu, w_down, b_down):
    slabs, p, _ = xs.shape
    d = 2 * slabs * LANES
    ne, _, de2 = w_gu.shape
    de = de2 // 2
    nblk = p // EXPERT_ROWS
    row_spec = pl.BlockSpec((slabs, EXPERT_ROWS, LANES), lambda i, e, n: (0, i, 0))
    grid_spec = pltpu.PrefetchScalarGridSpec(
        num_scalar_prefetch=2,
        grid=(nblk,),
        in_specs=[
            row_spec,
            pl.BlockSpec((1, d, de2), lambda i, e, n: (e[i], 0, 0)),
            pl.BlockSpec((1, 1, de2), lambda i, e, n: (e[i], 0, 0)),
            pl.BlockSpec((1, de, d), lambda i, e, n: (e[i], 0, 0)),
            pl.BlockSpec((1, 1, d), lambda i, e, n: (e[i], 0, 0)),
        ],
        out_specs=row_spec,
        scratch_shapes=[pltpu.VMEM((d, de2), BF16), pltpu.VMEM((de, d), BF16)],
    )
    return pl.pallas_call(
        _moe_kernel,
        grid_spec=grid_spec,
        out_shape=jax.ShapeDtypeStruct(xs.shape, jnp.uint32),
        compiler_params=_params("arbitrary"),
        name="moe_experts",
    )(blk_e, blk_used, xs, w_gu, b_gu.reshape(ne, 1, de2), w_down, b_down.reshape(ne, 1, d))


def _router_kernel(lg_ref, tri_ref, e_ref, w_ref, r_ref, cnt_out_ref, cnt_ref):
    t = pl.program_id(0)
    ne, ts = lg_ref.shape

    @pl.when(t == 0)
    def _():
        cnt_ref[...] = jnp.zeros(cnt_ref.shape, F32)

    g = lg_ref[...]
    eid = lax.broadcasted_iota(jnp.int32, g.shape, 0)
    selected = jnp.zeros(g.shape, jnp.bool_)
    picks = []
    for _ in range(TOPK_EXPERTS):
        mx = jnp.max(g, axis=0, keepdims=True)
        idx = jnp.min(jnp.where(g == mx, eid, ne), axis=0, keepdims=True)
        hit = eid == idx
        picks.append((hit, idx, mx))
        selected = selected | hit
        g = jnp.where(hit, -jnp.inf, g)
    exps = [jnp.exp(mx - picks[0][2]) for (_, _, mx) in picks]
    total = sum(exps)
    sel01 = jnp.where(selected, 1.0, 0.0).astype(BF16)
    prefix = _dot(sel01, tri_ref[...]) + jnp.tile(cnt_ref[...], (1, ts // LANES))
    pad_i = [jnp.zeros((1, ts), jnp.int32)] * (8 - TOPK_EXPERTS)
    e_ref[...] = jnp.concatenate([idx for (_, idx, _) in picks] + pad_i, axis=0)
    w_ref[...] = jnp.concatenate([e / total for e in exps] + [jnp.zeros((1, ts), F32)] * (8 - TOPK_EXPERTS), axis=0)
    r_ref[...] = jnp.concatenate(
        [jnp.sum(jnp.where(hit, prefix, 0.0), axis=0, keepdims=True).astype(jnp.int32) for (hit, _, _) in picks]
        + pad_i, axis=0)
    cnt_ref[...] = cnt_ref[...] + _dot(sel01, jnp.ones((ts, LANES), BF16))
    cnt_out_ref[...] = cnt_ref[...]


def _router(logits_t, ts=1024):
    ne, t = logits_t.shape
    ts = min(ts, t)
    tri = (jnp.arange(ts)[:, None] < jnp.arange(ts)[None, :]).astype(BF16)
    rows = pl.BlockSpec((8, ts), lambda i: (0, i))
    return pl.pallas_call(
        _router_kernel,
        grid=(t // ts,),
        in_specs=[pl.BlockSpec((ne, ts), lambda i: (0, i)), pl.BlockSpec((ts, ts), lambda i: (0, 0))],
        out_specs=[rows, rows, rows, pl.BlockSpec((ne, LANES), lambda i: (0, 0))],
        out_shape=[jax.ShapeDtypeStruct((8, t), jnp.int32), jax.ShapeDtypeStruct((8, t), F32),
                   jax.ShapeDtypeStruct((8, t), jnp.int32), jax.ShapeDtypeStruct((ne, LANES), F32)],
        scratch_shapes=[pltpu.VMEM((ne, LANES), F32)],
        compiler_params=_params("arbitrary"),
        name="moe_router",
    )(logits_t, tri)


def _moe_mix_kernel(x1_ref, w_ref, pk_ref, o_ref):
    slabs = pk_ref.shape[0]
    wts = w_ref[...]
    lo = [jnp.zeros((x1_ref.shape[0], LANES), F32)] * slabs
    hi = [jnp.zeros((x1_ref.shape[0], LANES), F32)] * slabs
    for k in range(TOPK_EXPERTS):
        wk = wts[:, k:k + 1]
        for j in range(slabs):
            a, b = _unpack_bf16_pairs(pk_ref[j, k])
            lo[j] = lo[j] + wk * a
            hi[j] = hi[j] + wk * b
    o_ref[...] = x1_ref[...] + jnp.concatenate(lo + hi, axis=1)


def _moe_mix(x1, weights_tk, picked, tm=512):
    t, d = x1.shape
    slabs = picked.shape[0]
    return pl.pallas_call(
        _moe_mix_kernel,
        grid=(t // tm,),
        in_specs=[pl.BlockSpec((tm, d), lambda i: (i, 0)),
                  pl.BlockSpec((tm, TOPK_EXPERTS), lambda i: (i, 0)),
                  pl.BlockSpec((slabs, TOPK_EXPERTS, tm, LANES), lambda i: (0, 0, i, 0))],
        out_specs=pl.BlockSpec((tm, d), lambda i: (i, 0)),
        out_shape=jax.ShapeDtypeStruct((t, d), F32),
        compiler_params=_params("parallel"),
        name="moe_mix",
    )(x1, weights_tk, picked)


def _moe_ffn(x1, h2, logits_t, w_gu, b_gu, w_down, b_down):
    t, d = x1.shape
    tk = t * TOPK_EXPERTS
    e8, w8, r8, cnt = _router(logits_t)
    top_e, weights, rank = e8[:TOPK_EXPERTS], w8[:TOPK_EXPERTS], r8[:TOPK_EXPERTS]
    counts = cnt[:, 0].astype(jnp.int32)
    pcounts = ((counts + EXPERT_ROWS - 1) // EXPERT_ROWS) * EXPERT_ROWS
    pends = jnp.cumsum(pcounts)
    pstarts = pends - pcounts
    is_e = top_e[..., None] == jnp.arange(N_EXPERTS, dtype=jnp.int32)
    dest_by_k = jnp.sum(jnp.where(is_e, pstarts, 0), axis=-1) + rank
    nblk = -(-tk // EXPERT_ROWS) + N_EXPERTS
    blk_start = jnp.arange(nblk, dtype=jnp.int32) * EXPERT_ROWS
    blk_e = jnp.minimum(jnp.sum((pends[None, :] <= blk_start[:, None]).astype(jnp.int32), axis=1), N_EXPERTS - 1)
    blk_used = (blk_start < pends[-1]).astype(jnp.int32)
    slabs = d // 2 // LANES
    p = nblk * EXPERT_ROWS
    slab_off = jnp.arange(slabs, dtype=jnp.int32) * p
    dest_kst = (dest_by_k[:, None, :] + slab_off[None, :, None]).reshape(-1)
    dest_skt = (dest_by_k[None, :, :] + slab_off[:, None, None]).reshape(-1)
    xs = _sc_scatter_rows(h2.reshape(slabs * t, LANES), dest_kst, slabs * p, TOPK_EXPERTS)
    ys = _moe_experts(xs.reshape(slabs, p, LANES), blk_e, blk_used, w_gu, b_gu, w_down, b_down)
    picked = _sc_gather_rows(ys.reshape(slabs * p, LANES), dest_skt).reshape(slabs, TOPK_EXPERTS, t, LANES)
    return _moe_mix(x1, weights.T, picked)


def kernel(x, mem, g_mix, w_in, lam_re, lam_im, log_dt, b_re, b_im, c_re, c_im, d_skip, w_glu, g_q, g_k, w_moba_out, g_mem, w_kv_mem, g_cq, g_ck, w_cross_out, w_out, g_ffn, w_router, b_router, w_gu, b_gu, w_down, b_down):
    bsz, seq_len, d = x.shape
    xt = x.reshape(bsz * seq_len, d)
    for l in range(g_mix.shape[0]):
        u, q2, kv, xqn, gates, qh = _in_proj(xt, g_mix[l], w_in[l], g_q[l], g_k[l], g_cq[l], seq_len)
        qd, dest, tiles = _moba_dispatch(q2, qh, kv, bsz, seq_len)
        mats = _s5_matrices(lam_re[l], lam_im[l], log_dt[l], b_re[l], b_im[l], c_re[l], c_im[l],
                            seq_len // S5_CHUNK)
        ys = _s5(u, mats, bsz, seq_len)
        od, lsed = _moba_diag(q2, kv, bsz, seq_len)
        kc, vc = _mem_kv(mem, g_mem[l], w_kv_mem[l], g_ck[l])
        om = _moba_finish(qd, dest, tiles, od, lsed, kv, bsz, seq_len)
        x1, h2, logits = _merge(xt, ys, u, d_skip[l], om, xqn, kc, vc, gates, w_glu[l], w_moba_out[l],
                                w_cross_out[l], w_out[l], g_ffn[l], w_router[l], b_router[l], seq_len)
        xt = _moe_ffn(x1, h2, logits, w_gu[l], b_gu[l], w_down[l], b_down[l])
    return xt.reshape(bsz, seq_len, d)
```

```python
import functools
import math

import jax
import jax.numpy as jnp
from jax import lax
from jax.experimental import pallas as pl
from jax.experimental.pallas import tpu as pltpu
from jax.experimental.pallas import tpu_sc as plsc

F32 = jnp.float32
BF16 = jnp.bfloat16

EPS = 1e-6
N_BRANCH = 3
SSM_GROUP = 16
SSM_STATE = 64
S5_CHUNK = 16
MOBA_HEAD_DIM = 64
MOBA_BLOCK = 256
MOBA_TOPK = 3
MOBA_MAX_BLOCKS = 64
ROPE_THETA = 10000.0
X_HEADS = 4
N_EXPERTS = 32
TOPK_EXPERTS = 4
SWIGLU_LIMIT = 7.0
SWIGLU_ALPHA = 1.702
EXPERT_ROWS = 512
NEG_BIG = -1e30
LANES = 128
VMEM_LIMIT_BYTES = 56 * 1024 * 1024


def _params(*sem):
    return pltpu.CompilerParams(dimension_semantics=sem, vmem_limit_bytes=VMEM_LIMIT_BYTES)


def _sigmoid(x):
    return 1.0 / (1.0 + jnp.exp(-x))


def _dot(a, b):
    return jnp.dot(a, b, preferred_element_type=F32)


def _pack_bf16_pairs(x):
    n = x.shape[1] // 2
    lo = lax.bitcast_convert_type(x[:, :n], jnp.uint32) >> 16
    hi = lax.bitcast_convert_type(x[:, n:], jnp.uint32) & jnp.uint32(0xFFFF0000)
    return lo | hi


def _unpack_bf16_pairs(w):
    lo = lax.bitcast_convert_type(w << 16, F32)
    hi = lax.bitcast_convert_type(w & jnp.uint32(0xFFFF0000), F32)
    return lo, hi


def _dot_nt(a, b):
    return lax.dot_general(a, b, (((1,), (1,)), ((), ())), preferred_element_type=F32)


def _inproj_kernel(x_ref, gmix_ref, wa_ref, wg_ref, e64_ref, gq_ref, gk_ref, gcq_ref, cos_ref, sin_ref,
                   u_ref, q_ref, kv_ref, xq_ref, g_ref, qh_ref):
    xf = x_ref[...]
    ms = jnp.mean(xf * xf, axis=-1, keepdims=True)
    h = (xf * lax.rsqrt(ms + EPS) * gmix_ref[...]).astype(BF16)
    w = u_ref.shape[1]
    proj = lambda c: _dot(h, wa_ref[:, c * w:(c + 1) * w])
    u_ref[...] = proj(0)

    cos = jnp.tile(cos_ref[...], (1, w // LANES))
    sin = jnp.tile(sin_ref[...], (1, w // LANES))
    lane = lax.broadcasted_iota(jnp.int32, (xf.shape[0], w), 1)
    first_half = (lane % MOBA_HEAD_DIM) < (MOBA_HEAD_DIM // 2)

    def qk_norm_rope(raw, g):
        ss = _dot((raw * raw).astype(BF16), e64_ref[...])
        n = raw * lax.rsqrt(ss * (1.0 / MOBA_HEAD_DIM) + EPS) * g
        rot = jnp.where(first_half,
                        pltpu.roll(n, w - MOBA_HEAD_DIM // 2, 1),
                        pltpu.roll(n, MOBA_HEAD_DIM // 2, 1))
        return n * cos + rot * sin

    q = qk_norm_rope(proj(1), gq_ref[...]) * (MOBA_HEAD_DIM ** -0.5)
    q_ref[...] = q.astype(BF16)
    in_a = lax.broadcasted_iota(jnp.int32, (xf.shape[0], LANES), 1) < MOBA_HEAD_DIM
    for p in range(w // LANES):
        pair = q[:, p * LANES:(p + 1) * LANES].astype(BF16).astype(F32)
        qh_ref[2 * p] = jnp.where(in_a, pair, 0.0)
        qh_ref[2 * p + 1] = jnp.where(in_a, 0.0, pair)
    k = qk_norm_rope(proj(2), gk_ref[...]).astype(BF16)
    v = proj(3).astype(BF16)
    for p in range(w // LANES):
        kv_ref[:, (2 * p) * LANES:(2 * p + 1) * LANES] = k[:, p * LANES:(p + 1) * LANES]
        kv_ref[:, (2 * p + 1) * LANES:(2 * p + 2) * LANES] = v[:, p * LANES:(p + 1) * LANES]

    xq = proj(4)
    hd = w // X_HEADS
    for c in range(X_HEADS):
        chunk = xq[:, c * hd:(c + 1) * hd]
        cms = jnp.mean(chunk * chunk, axis=-1, keepdims=True)
        xq_ref[:, c * hd:(c + 1) * hd] = (chunk * lax.rsqrt(cms + EPS) * gcq_ref[...]).astype(BF16)

    d = xf.shape[1]
    for c in range(N_BRANCH):
        z = _dot(h, wg_ref[:, c * d:(c + 1) * d])
        g_ref[:, c * d:(c + 1) * d] = _sigmoid(z).astype(BF16)


def _in_proj(xt, g_mix, w_in, g_q, g_k, g_cq, seq_len, tm=512):
    t, d = xt.shape
    tm = min(tm, seq_len)
    w = d // 2
    wa = w_in[:, :5 * w].astype(BF16)
    wg = w_in[:, 5 * w:].astype(BF16)
    heads = w // MOBA_HEAD_DIM
    e64 = jnp.kron(jnp.eye(heads, dtype=F32), jnp.ones((MOBA_HEAD_DIM, MOBA_HEAD_DIM), F32)).astype(BF16)
    half = MOBA_HEAD_DIM // 2
    inv = ROPE_THETA ** (-jnp.arange(half, dtype=F32) / half)
    ang = jnp.arange(seq_len, dtype=F32)[:, None] * inv[None, :]
    cos = jnp.tile(jnp.cos(ang), (1, LANES // half))
    sin = jnp.tile(jnp.concatenate([-jnp.sin(ang), jnp.sin(ang)], axis=1), (1, LANES // MOBA_HEAD_DIM))
    nt = seq_len // tm
    row = lambda i: (i, 0)
    const = lambda i: (0, 0)
    out_w = jax.ShapeDtypeStruct((t, w), BF16)
    return pl.pallas_call(
        _inproj_kernel,
        grid=(t // tm,),
        in_specs=[
            pl.BlockSpec((tm, d), row),
            pl.BlockSpec((1, d), const),
            pl.BlockSpec((d, 5 * w), const, pipeline_mode=pl.Buffered(1)),
            pl.BlockSpec((d, N_BRANCH * d), const, pipeline_mode=pl.Buffered(1)),
            pl.BlockSpec((w, w), const, pipeline_mode=pl.Buffered(1)),
            pl.BlockSpec((1, w), const),
            pl.BlockSpec((1, w), const),
            pl.BlockSpec((1, w // X_HEADS), const),
            pl.BlockSpec((tm, LANES), lambda i: (i % nt, 0)),
            pl.BlockSpec((tm, LANES), lambda i: (i % nt, 0)),
        ],
        out_specs=[pl.BlockSpec((tm, w), row), pl.BlockSpec((tm, w), row), pl.BlockSpec((tm, 2 * w), row),
                   pl.BlockSpec((tm, w), row), pl.BlockSpec((tm, N_BRANCH * d), row),
                   pl.BlockSpec((heads, tm, LANES), lambda i: (0, i, 0))],
        out_shape=[jax.ShapeDtypeStruct((t, w), F32), out_w, jax.ShapeDtypeStruct((t, 2 * w), BF16), out_w,
                   jax.ShapeDtypeStruct((t, N_BRANCH * d), BF16),
                   jax.ShapeDtypeStruct((heads, t, LANES), F32)],
        compiler_params=_params("parallel"),
        name="in_proj",
    )(xt, g_mix.reshape(1, d), wa, wg, e64,
      jnp.tile(g_q, heads).reshape(1, w), jnp.tile(g_k, heads).reshape(1, w), g_cq.reshape(1, -1), cos, sin)


def _s5_matrices(lam_re, lam_im, log_dt, b_re, b_im, c_re, c_im, n_chunks):
    hp = lax.Precision.HIGHEST
    c = S5_CHUNK
    dt = jnp.exp(log_dt)[:, None]
    mag = jnp.exp(lam_re * dt)
    ar = mag * jnp.cos(lam_im * dt)
    ai = mag * jnp.sin(lam_im * dt)
    nr = ar - 1.0
    den = lam_re * lam_re + lam_im * lam_im
    cr = (nr * lam_re + ai * lam_im) / den
    ci = (ai * lam_re - nr * lam_im) / den
    bbr = cr[..., None] * b_re - ci[..., None] * b_im
    bbi = cr[..., None] * b_im + ci[..., None] * b_re

    def power(n):
        nf = n.astype(F32)[None, :, None]
        m = jnp.exp((lam_re * dt)[:, None, :] * nf)
        th = (lam_im * dt)[:, None, :] * nf
        return m * jnp.cos(th), m * jnp.sin(th)

    pr, pi = power(jnp.arange(c + 1))
    kbr = pr[..., None] * bbr[:, None] - pi[..., None] * bbi[:, None]
    kbi = pr[..., None] * bbi[:, None] + pi[..., None] * bbr[:, None]
    kk = (jnp.einsum('ghp,gtpc->gthc', c_re, kbr, precision=hp)
          - jnp.einsum('ghp,gtpc->gthc', c_im, kbi, precision=hp))
    tq = jnp.arange(c)
    g = kk.shape[0]
    rev = c - 1 - tq
    w_in = jnp.concatenate([kbr[:, rev].transpose(0, 1, 3, 2), kbi[:, rev].transpose(0, 1, 3, 2)], axis=-1)
    w_in = w_in.reshape(g, c * SSM_GROUP, 2 * SSM_STATE)
    prn, pin = pr[:, 1:], pi[:, 1:]
    wo_r = c_re[:, None] * prn[:, :, None, :] - c_im[:, None] * pin[:, :, None, :]
    wo_i = -c_re[:, None] * pin[:, :, None, :] - c_im[:, None] * prn[:, :, None, :]
    w_out = jnp.concatenate([wo_r, wo_i], axis=-1).transpose(0, 3, 1, 2).reshape(g, 2 * SSM_STATE, c * SSM_GROUP)
    n_steps = max(1, int(math.ceil(math.log2(n_chunks))))
    qr, qi = power(c * (2 ** jnp.arange(n_steps)))
    pa = jnp.concatenate([qr, qr], axis=-1)
    pb = jnp.concatenate([-qi, qi], axis=-1)
    return kk.astype(BF16), w_in.astype(BF16), w_out.astype(BF16), pa, pb


S5_SET = LANES // SSM_GROUP
S5_ROWS = 256


def _s5_kernel(u_ref, toep_ref, winc_ref, woutc_ref, pa_ref, pb_ref, y_ref, s_ref, win_ref, wout_ref):
    c = S5_CHUNK
    nc = u_ref.shape[0] // c
    rb = min(S5_ROWS, nc)
    p2 = 2 * SSM_STATE

    @pl.when(pl.program_id(1) == 0)
    def _():
        r_group = lax.broadcasted_iota(jnp.int32, (LANES, p2), 0) // SSM_GROUP
        l_group = lax.broadcasted_iota(jnp.int32, (p2, LANES), 1) // SSM_GROUP
        for t in range(c):
            for g in range(S5_SET):
                win_ref[0, t * LANES:(t + 1) * LANES, g * p2:(g + 1) * p2] = jnp.where(
                    r_group == g, winc_ref[0, t], jnp.zeros((LANES, p2), BF16))
                wout_ref[0, g * p2:(g + 1) * p2, t * LANES:(t + 1) * LANES] = jnp.where(
                    l_group == g, woutc_ref[0, t], jnp.zeros((p2, LANES), BF16))

    def chunk_rows(r0):
        return jnp.concatenate([u_ref[pl.ds(r0 * c + t, rb, stride=c), :] for t in range(c)], axis=1).astype(BF16)

    for blk in range(nc // rb):
        s_ref[blk * rb:(blk + 1) * rb, :] = _dot(chunk_rows(blk * rb), win_ref[0])
    row = lax.broadcasted_iota(jnp.int32, (nc, LANES), 0)
    n_steps = pa_ref.shape[1]
    for g in range(S5_SET):
        lanes = slice(g * LANES, (g + 1) * LANES)
        s = s_ref[:, lanes]
        for k in range(n_steps):
            sh = 1 << k
            if sh >= nc:
                break
            prev = jnp.where(row >= sh, pltpu.roll(s, sh, 0), 0.0)
            s = s + pa_ref[0, k:k + 1, lanes] * prev + pb_ref[0, k:k + 1, lanes] * pltpu.roll(prev, SSM_STATE, 1)
        s_ref[:, lanes] = jnp.where(row >= 1, pltpu.roll(s, 1, 0), 0.0)
    for blk in range(nc // rb):
        rows = slice(blk * rb, (blk + 1) * rb)
        a = chunk_rows(blk * rb)
        carried = _dot(s_ref[rows, :].astype(BF16), wout_ref[0])
        for t in range(c):
            y = _dot(a[:, :(t + 1) * LANES], toep_ref[0, (c - 1 - t) * LANES:, :])
            y_ref[pl.ds(blk * rb * c + t, rb, stride=c), :] = y + carried[:, t * LANES:(t + 1) * LANES]


def _s5_block_diag(mats):
    kk, w_in, w_out, pa, pb = mats
    g, c, h, p2 = kk.shape[0], S5_CHUNK, SSM_GROUP, 2 * SSM_STATE
    ns = g // S5_SET
    eye = jnp.eye(S5_SET, dtype=kk.dtype)
    lag_bd = jnp.einsum('sglhc,gk->slgckh', kk.reshape(ns, S5_SET, c + 1, h, h), eye).reshape(ns, c + 1, LANES, LANES)
    toep_bd = lag_bd[:, c - 1::-1].reshape(ns, c * LANES, LANES)

    w_in_c = w_in.reshape(ns, S5_SET, c, h, p2).transpose(0, 2, 1, 3, 4).reshape(ns, c, LANES, p2)
    w_out_c = w_out.reshape(ns, S5_SET, p2, c, h).transpose(0, 3, 2, 1, 4).reshape(ns, c, p2, LANES)
    k = pa.shape[1]
    lanes_of = lambda x: x.reshape(ns, S5_SET, k, p2).transpose(0, 2, 1, 3).reshape(ns, k, S5_SET * p2)
    return toep_bd, w_in_c, w_out_c, lanes_of(pa), lanes_of(pb)


def _s5(u, mats, bsz, seq_len):
    toep, w_in, w_out, pa, pb = _s5_block_diag(mats)
    ns = toep.shape[0]
    nc = seq_len // S5_CHUNK
    per_set = lambda s, b: (s, 0, 0)
    once = dict(pipeline_mode=pl.Buffered(1))
    io_spec = pl.BlockSpec((seq_len, LANES), lambda s, b: (b, s), **once)
    return pl.pallas_call(
        _s5_kernel,
        grid=(ns, bsz),
        in_specs=[
            io_spec,
            pl.BlockSpec((1,) + toep.shape[1:], per_set, **once),
            pl.BlockSpec((1,) + w_in.shape[1:], lambda s, b: (s, 0, 0, 0)),
            pl.BlockSpec((1,) + w_out.shape[1:], lambda s, b: (s, 0, 0, 0)),
            pl.BlockSpec((1,) + pa.shape[1:], per_set),
            pl.BlockSpec((1,) + pb.shape[1:], per_set),
        ],
        out_specs=io_spec,
        out_shape=jax.ShapeDtypeStruct(u.shape, F32),
        scratch_shapes=[pltpu.VMEM((nc, S5_SET * 2 * SSM_STATE), F32),
                        pltpu.VMEM((1, S5_CHUNK * LANES, S5_SET * 2 * SSM_STATE), BF16),
                        pltpu.VMEM((1, S5_SET * 2 * SSM_STATE, S5_CHUNK * LANES), BF16)],
        compiler_params=_params("arbitrary", "arbitrary"),
        name="s5_scan",
    )(u, toep, w_in, w_out, pa, pb)


def _moba_select_kernel(q_ref, k_ref, tri_ref, sel_ref, cnt_out_ref, km_ref, cnt_ref):
    h = pl.program_id(1)
    t = pl.program_id(2)
    ts = q_ref.shape[0]
    nbk = MOBA_MAX_BLOCKS

    @pl.when(t == 0)
    def _():
        kk = k_ref[...].astype(F32)
        nb = kk.shape[0] // MOBA_BLOCK
        km = jnp.sum(kk.reshape(nb, MOBA_BLOCK, LANES), axis=1) * (1.0 / MOBA_BLOCK)
        if nb < nbk:
            km = jnp.concatenate([km, jnp.zeros((nbk - nb, LANES), F32)], axis=0)
        lane = lax.broadcasted_iota(jnp.int32, (nbk, LANES), 1)
        km_ref[...] = jnp.where((lane // MOBA_HEAD_DIM) == (h % 2), km, 0.0).astype(BF16)
        cnt_ref[...] = jnp.zeros(cnt_ref.shape, F32)

    gate = _dot_nt(km_ref[...], q_ref[...])
    blk = lax.broadcasted_iota(jnp.int32, gate.shape, 0)
    qblk = (t * ts + lax.broadcasted_iota(jnp.int32, gate.shape, 1)) // MOBA_BLOCK
    g = jnp.where(blk < qblk, gate, -jnp.inf)
    selected = jnp.zeros(gate.shape, jnp.bool_)
    picks = []
    for _ in range(MOBA_TOPK):
        mx = jnp.max(g, axis=0, keepdims=True)
        idx = jnp.min(jnp.where(g == mx, blk, nbk), axis=0, keepdims=True)
        hit = blk == idx
        ok = (idx[0:1] < qblk[0:1]) & (mx > -jnp.inf)
        picks.append((hit, idx, ok))
        selected = selected | (hit & ok)
        g = jnp.where(hit, -jnp.inf, g)
    sel01 = jnp.where(selected, 1.0, 0.0)
    chunks = [sel01[:, c * LANES:(c + 1) * LANES] for c in range(ts // LANES)]
    within = _dot(jnp.concatenate(chunks, axis=0).astype(BF16), tri_ref[...])
    base = cnt_ref[...]
    pieces = []
    for c, chunk in enumerate(chunks):
        pieces.append(within[c * nbk:(c + 1) * nbk] + base)
        base = base + jnp.sum(chunk, axis=1, keepdims=True)
    prefix = jnp.concatenate(pieces, axis=1)
    rows = [jnp.where(ok, idx, -1) for (_, idx, ok) in picks]
    rows += [jnp.sum(jnp.where(hit, prefix, 0.0), axis=0, keepdims=True).astype(jnp.int32) for (hit, _, _) in picks]
    rows += [jnp.zeros((1, ts), jnp.int32)] * (8 - 2 * MOBA_TOPK)
    sel_ref[0, 0] = jnp.concatenate(rows, axis=0)
    cnt_ref[...] = base
    cnt_out_ref[0, 0] = base


def _moba_select(q2, kv, bsz, seq_len, ts=1024):
    ts = min(ts, seq_len)
    heads = q2.shape[1] // MOBA_HEAD_DIM
    nt = seq_len // ts
    tri = (jnp.arange(LANES)[:, None] < jnp.arange(LANES)[None, :]).astype(BF16)
    return pl.pallas_call(
        _moba_select_kernel,
        grid=(bsz, heads, nt),
        in_specs=[
            pl.BlockSpec((ts, LANES), lambda b, h, t: (b * nt + t, h // 2)),
            pl.BlockSpec((seq_len, LANES), lambda b, h, t: (b, 2 * (h // 2))),
            pl.BlockSpec((LANES, LANES), lambda b, h, t: (0, 0)),
        ],
        out_specs=[pl.BlockSpec((1, 1, 8, ts), lambda b, h, t: (b, h, 0, t)),
                   pl.BlockSpec((1, 1, MOBA_MAX_BLOCKS, LANES), lambda b, h, t: (b, h, 0, 0))],
        out_shape=[jax.ShapeDtypeStruct((bsz, heads, 8, seq_len), jnp.int32),
                   jax.ShapeDtypeStruct((bsz, heads, MOBA_MAX_BLOCKS, LANES), F32)],
        scratch_shapes=[pltpu.VMEM((MOBA_MAX_BLOCKS, LANES), BF16), pltpu.VMEM((MOBA_MAX_BLOCKS, LANES), F32)],
        compiler_params=_params("parallel", "arbitrary", "arbitrary"),
        name="moba_select",
    )(q2, kv, tri)


MOBA_DIAG_BLOCKS = 4


def _moba_diag_kernel(q_ref, kv_ref, o_ref, lse_ref):
    lane = lax.broadcasted_iota(jnp.int32, (MOBA_BLOCK, LANES), 1)
    is_a = lane < MOBA_HEAD_DIM
    r = lax.broadcasted_iota(jnp.int32, (MOBA_BLOCK, MOBA_BLOCK), 0)
    c = lax.broadcasted_iota(jnp.int32, (MOBA_BLOCK, MOBA_BLOCK), 1)
    for u in range(MOBA_DIAG_BLOCKS):
        rows = pl.ds(u * MOBA_BLOCK, MOBA_BLOCK)
        q = q_ref[rows, :]
        kb = kv_ref[rows, :LANES]
        vb = kv_ref[rows, LANES:]
        pvs, ms = [], []
        for own in (is_a, jnp.logical_not(is_a)):
            s = jnp.where(c <= r, _dot_nt(jnp.where(own, q, jnp.zeros_like(q)), kb), NEG_BIG)
            m = jnp.max(s, axis=1, keepdims=True)
            p = jnp.exp(s - m).astype(BF16)
            pvs.append(_dot(p, jnp.where(own, vb, 1.0)))
            ms.append(m)
        num = jnp.where(is_a, pvs[0], pvs[1])
        den = pltpu.roll(jnp.where(is_a, pvs[1], pvs[0]), MOBA_HEAD_DIM, 1)
        o_ref[rows, :] = num / den
        lse_ref[rows, :] = jnp.where(is_a, ms[0], ms[1]) + jnp.log(den)


def _moba_diag(q2, kv, bsz, seq_len):
    npair = q2.shape[1] // LANES
    rows = MOBA_BLOCK * MOBA_DIAG_BLOCKS
    nb = seq_len // rows
    spec = pl.BlockSpec((rows, LANES), lambda b, p, i: (b * nb + i, p))
    out = jax.ShapeDtypeStruct(q2.shape, F32)
    return pl.pallas_call(
        _moba_diag_kernel,
        grid=(bsz, npair, nb),
        in_specs=[spec, pl.BlockSpec((rows, 2 * LANES), lambda b, p, i: (b * nb + i, p))],
        out_specs=[spec, spec],
        out_shape=[out, out],
        compiler_params=_params("parallel", "parallel", "parallel"),
        name="moba_diag",
    )(q2, kv)


MOBA_TILE = 256
MOBA_TILES_PER_STEP = 8


def _moba_grouped_kernel(tile_row_ref, tile_pair_ref, tile_half_ref, tile_real_ref, qd_ref, *refs):
    n = MOBA_TILES_PER_STEP
    kv_refs, o_ref = refs[:n], refs[n + 1]
    i = pl.program_id(0)
    lane = lax.broadcasted_iota(jnp.int32, (MOBA_TILE, LANES), 1)
    kv_lane = lax.broadcasted_iota(jnp.int32, (MOBA_BLOCK, LANES), 1)

    @pl.when(tile_real_ref[i * n] > 0)
    def _():
        for u in range(n):
            tile = i * n + u
            rows = pl.ds(u * MOBA_TILE, MOBA_TILE)
            own = (lane // MOBA_HEAD_DIM) == tile_half_ref[tile]
            s = _dot_nt(qd_ref[rows, :].astype(BF16), kv_refs[u][:, :LANES])
            m = jnp.max(s, axis=1, keepdims=True)
            p = jnp.exp(s - m).astype(BF16)
            vb = jnp.where((kv_lane // MOBA_HEAD_DIM) == tile_half_ref[tile], kv_refs[u][:, LANES:], 1.0)
            pv = _dot(p, vb)
            row_sum = pltpu.roll(pv, MOBA_HEAD_DIM, 1)
            part = jnp.where(own, pv / row_sum, m + jnp.log(pv))
            o_ref[rows, :] = jnp.where(tile_real_ref[tile] > 0, part, NEG_BIG)

    @pl.when(tile_real_ref[i * n] == 0)
    def _():
        o_ref[...] = jnp.full(o_ref.shape, NEG_BIG, F32)


def _moba_grouped(qd, kv, tile_row, tile_pair, tile_half, tile_real, run_after):
    n = MOBA_TILES_PER_STEP
    n_tiles = qd.shape[0] // MOBA_TILE

    def kv_spec(u):
        return pl.BlockSpec((MOBA_BLOCK, 2 * LANES), lambda i, tr, tp, th, tl: (tr[i * n + u], tp[i * n + u]))

    grid_spec = pltpu.PrefetchScalarGridSpec(
        num_scalar_prefetch=4,
        grid=(n_tiles // n,),
        in_specs=[pl.BlockSpec((n * MOBA_TILE, LANES), lambda i, tr, tp, th, tl: (i, 0))]
        + [kv_spec(u) for u in range(n)]
        + [pl.BlockSpec((8, LANES), lambda i, tr, tp, th, tl: (0, 0))],
        out_specs=pl.BlockSpec((n * MOBA_TILE, LANES), lambda i, tr, tp, th, tl: (i, 0)),
    )
    return pl.pallas_call(
        _moba_grouped_kernel,
        grid_spec=grid_spec,
        out_shape=jax.ShapeDtypeStruct(qd.shape, F32),
        compiler_params=_params("arbitrary"),
        name="moba_grouped",
    )(tile_row, tile_pair, tile_half, tile_real, qd, *([kv] * n), run_after)


def _moba_combine_kernel(od_ref, lsed_ref, *refs):
    g_refs, o_ref = refs[:-1], refs[-1]
    lane = lax.broadcasted_iota(jnp.int32, od_ref.shape, 1)
    is_a = lane < MOBA_HEAD_DIM
    parts = [(od_ref[...], lsed_ref[...])]
    for s in range(MOBA_TOPK):
        xa = g_refs[s][0, 0, 0]
        xb = g_refs[MOBA_TOPK + s][0, 0, 0]
        o = jnp.where(is_a, xa, xb)
        lse = pltpu.roll(jnp.where(is_a, xb, xa), MOBA_HEAD_DIM, 1)
        parts.append((o, lse))
    m = parts[0][1]
    for _, lse in parts[1:]:
        m = jnp.maximum(m, lse)
    num = jnp.zeros(od_ref.shape, F32)
    den = jnp.zeros(od_ref.shape, F32)
    for o, lse in parts:
        w = jnp.exp(lse - m)
        num = num + w * o
        den = den + w
    o_ref[...] = (num / den).astype(BF16)


def _moba_combine(od, lsed, gath, bsz, seq_len, tm=1024):
    tm = min(tm, seq_len)
    npair = od.shape[1] // LANES
    nt = seq_len // tm
    spec = pl.BlockSpec((tm, LANES), lambda b, p, i: (b * nt + i, p))

    def g_spec(e, s):
        return pl.BlockSpec((1, 1, 1, tm, LANES), lambda b, p, i: (b, 2 * p + e, s, i, 0))

    return pl.pallas_call(
        _moba_combine_kernel,
        grid=(bsz, npair, nt),
        in_specs=[spec, spec] + [g_spec(e, s) for e in range(2) for s in range(MOBA_TOPK)],
        out_specs=spec,
        out_shape=jax.ShapeDtypeStruct(od.shape, BF16),
        compiler_params=_params("parallel", "parallel", "parallel"),
        name="moba_combine",
    )(od, lsed, *([gath] * (2 * MOBA_TOPK)))


SC_WINDOW = 128
SC_CORES = 2
SC_SUBCORES = 16


def _sc_mesh():
    return plsc.VectorSubcoreMesh(core_axis_name="core", subcore_axis_name="subcore")


def _sc_gather_rows(table, idx):
    n = idx.shape[0]
    d = table.shape[1]
    window = SC_WINDOW
    assert n % (window * SC_CORES * SC_SUBCORES) == 0
    per_core = n // window // SC_CORES

    @functools.partial(pl.kernel, out_type=jax.ShapeDtypeStruct((n, d), table.dtype), mesh=_sc_mesh())
    def gather_kernel(x_hbm, i_hbm, o_hbm):
        base = lax.axis_index("core") * per_core

        def body(i_vmem, o_vmem):
            pltpu.sync_copy(x_hbm.at[i_vmem.at[0]], o_vmem)

        pltpu.emit_pipeline(
            body,
            grid=(per_core,),
            in_specs=[pl.BlockSpec((1, window), index_map=lambda i: (0, base + i))],
            out_specs=[pl.BlockSpec((window, d), index_map=lambda i: (base + i, 0))],
            core_axis_name="subcore",
            dimension_semantics=(pltpu.PARALLEL,),
            trace_scopes=False,
        )(i_hbm, o_hbm)

    return gather_kernel(table, idx.reshape(1, n))


def _sc_scatter_rows(rows, dest, n_out, repeat):
    n_src, d = rows.shape
    n = dest.shape[0]
    assert n == repeat * n_src
    window = SC_WINDOW
    assert n % (window * SC_CORES * SC_SUBCORES) == 0
    per_core = n // window // SC_CORES
    src_windows = n_src // window

    @functools.partial(pl.kernel, out_type=jax.ShapeDtypeStruct((n_out, d), rows.dtype), mesh=_sc_mesh())
    def scatter_kernel(x_hbm, i_hbm, o_hbm):
        base = lax.axis_index("core") * per_core

        def body(x_vmem, i_vmem):
            pltpu.sync_copy(x_vmem, o_hbm.at[i_vmem.at[0]])

        pltpu.emit_pipeline(
            body,
            grid=(per_core,),
            in_specs=[pl.BlockSpec((window, d), index_map=lambda i: ((base + i) % src_windows, 0)),
                      pl.BlockSpec((1, window), index_map=lambda i: (0, base + i))],
            out_specs=[],
            core_axis_name="subcore",
            dimension_semantics=(pltpu.PARALLEL,),
            trace_scopes=False,
        )(x_hbm, i_hbm)

    return scatter_kernel(rows, dest.reshape(1, n))


def _moba_dispatch(q2, qh, kv, bsz, seq_len):
    heads = q2.shape[1] // MOBA_HEAD_DIM
    nbk = MOBA_MAX_BLOCKS
    sel, cnt = _moba_select(q2, kv, bsz, seq_len)
    counts = cnt[..., 0].astype(jnp.int32).reshape(bsz * heads * nbk)
    pcounts = ((counts + MOBA_TILE - 1) // MOBA_TILE) * MOBA_TILE
    pends = jnp.cumsum(pcounts)
    pstarts = (pends - pcounts).reshape(bsz, heads, 1, 1, nbk)
    n_items = bsz * heads * seq_len * MOBA_TOPK
    step_rows = MOBA_TILE * MOBA_TILES_PER_STEP
    n_rows = -(-(n_items + bsz * heads * nbk * MOBA_TILE) // step_rows) * step_rows
    n_null = step_rows
    idx = sel[:, :, 0:MOBA_TOPK, :]
    rank = sel[:, :, MOBA_TOPK:2 * MOBA_TOPK, :]
    start = jnp.sum(jnp.where(idx[..., None] == jnp.arange(nbk), pstarts, 0), axis=-1)
    null_row = n_rows + jnp.arange(seq_len, dtype=jnp.int32) % n_null
    dest = jnp.where(idx >= 0, start + rank, null_row)
    n_tiles = (n_rows + n_null) // MOBA_TILE
    n_groups = bsz * heads * nbk
    tile_start = jnp.arange(n_tiles, dtype=jnp.int32) * MOBA_TILE
    tile_g = jnp.minimum(jnp.sum((pends[None, :] <= tile_start[:, None]).astype(jnp.int32), axis=1), n_groups - 1)
    tile_real = (tile_start < pends[-1]).astype(jnp.int32)
    tile_head = (tile_g // nbk) % heads
    tile_row = (tile_g // (heads * nbk)) * (seq_len // MOBA_BLOCK) + jnp.minimum(tile_g % nbk, seq_len // MOBA_BLOCK - 1)
    dest_by_slot = dest.transpose(2, 1, 0, 3).reshape(-1)
    qd = _sc_scatter_rows(qh.reshape(-1, LANES), dest_by_slot, n_rows + n_null, MOBA_TOPK)
    return qd, dest, (tile_row, tile_head // 2, tile_head % 2, tile_real)


def _moba_finish(qd, dest, tiles, od, lsed, kv, bsz, seq_len):
    heads = dest.shape[1]
    part = _moba_grouped(qd, kv, *tiles, lsed)
    gath = _sc_gather_rows(part, dest.reshape(-1)).reshape(bsz, heads, MOBA_TOPK, seq_len, LANES)
    return _moba_combine(od, lsed, gath, bsz, seq_len)


def _mem_kv_kernel(mem_ref, gmem_ref, w_ref, gck_ref, k_ref, v_ref):
    xf = mem_ref[0]
    ms = jnp.mean(xf * xf, axis=-1, keepdims=True)
    h = (xf * lax.rsqrt(ms + EPS) * gmem_ref[...]).astype(BF16)
    kv = _dot(h, w_ref[...])
    w = k_ref.shape[2]
    hd = w // X_HEADS
    for c in range(X_HEADS):
        chunk = kv[:, c * hd:(c + 1) * hd]
        cms = jnp.mean(chunk * chunk, axis=-1, keepdims=True)
        k_ref[0, :, c * hd:(c + 1) * hd] = (chunk * lax.rsqrt(cms + EPS) * gck_ref[...]).astype(BF16)
    v_ref[0] = kv[:, w:].astype(BF16)


def _mem_kv(mem, g_mem, w_kv_mem, g_ck):
    bsz, m, d = mem.shape
    w = w_kv_mem.shape[1] // 2
    const = lambda b: (0, 0)
    out = jax.ShapeDtypeStruct((bsz, m, w), BF16)
    return pl.pallas_call(
        _mem_kv_kernel,
        grid=(bsz,),
        in_specs=[pl.BlockSpec((1, m, d), lambda b: (b, 0, 0)), pl.BlockSpec((1, d), const),
                  pl.BlockSpec((d, 2 * w), const), pl.BlockSpec((1, w // X_HEADS), const)],
        out_specs=[pl.BlockSpec((1, m, w), lambda b: (b, 0, 0))] * 2,
        out_shape=[out, out],
        compiler_params=_params("parallel"),
        name="mem_kv",
    )(mem, g_mem.reshape(1, d), w_kv_mem.astype(BF16), g_ck.reshape(1, -1))


def _merge_kernel(x_ref, ys_ref, u_ref, dskip_ref, om_ref, xq_ref, kc_ref, vc_ref, g_ref,
                  wglu_ref, wmo_ref, wco_ref, wout_ref, gffn_ref, wr_ref, br_ref,
                  x1_ref, h2_ref, logit_ref):
    d = x_ref.shape[1]
    y = ys_ref[...].astype(F32) + dskip_ref[...] * u_ref[...].astype(F32)
    ge = 0.5 * y * (1.0 + jnp.tanh(math.sqrt(2.0 / math.pi) * (y + 0.044715 * (y * y * y))))
    z = _dot(ge.astype(BF16), wglu_ref[...])
    merged = g_ref[:, 0:d].astype(F32) * (z[:, :d] * _sigmoid(z[:, d:]))
    merged = merged + g_ref[:, d:2 * d].astype(F32) * _dot(om_ref[...], wmo_ref[...])
    w = xq_ref.shape[1]
    hd = w // X_HEADS
    heads = []
    for c in range(X_HEADS):
        s = _dot_nt(xq_ref[:, c * hd:(c + 1) * hd], kc_ref[0, :, c * hd:(c + 1) * hd]) * (hd ** -0.5)
        p = jnp.exp(s - jnp.max(s, axis=1, keepdims=True))
        p = p / jnp.sum(p, axis=1, keepdims=True)
        heads.append(_dot(p.astype(BF16), vc_ref[0, :, c * hd:(c + 1) * hd]))
    oc = jnp.concatenate(heads, axis=1).astype(BF16)
    merged = merged + g_ref[:, 2 * d:3 * d].astype(F32) * _dot(oc, wco_ref[...])
    x1 = x_ref[...] + _dot(merged.astype(BF16), wout_ref[...])
    x1_ref[...] = x1
    ms = jnp.mean(x1 * x1, axis=-1, keepdims=True)
    h2 = (x1 * lax.rsqrt(ms + EPS) * gffn_ref[...]).astype(BF16)
    packed = _pack_bf16_pairs(h2.astype(F32))
    for j in range(h2_ref.shape[0]):
        h2_ref[j] = packed[:, j * LANES:(j + 1) * LANES]
    logit_ref[...] = _dot_nt(wr_ref[...], h2) + br_ref[...]


def _merge(xt, ys, u, d_skip, om, xqn, kc, vc, gates, w_glu, w_mo, w_co, w_out, g_ffn, w_router, b_router,
           seq_len, tm=512):
    t, d = xt.shape
    tm = min(tm, seq_len)
    w = d // 2
    m = kc.shape[1]
    ne = w_router.shape[1]
    nt = seq_len // tm
    row = lambda i: (i, 0)
    const = lambda i: (0, 0)
    per_b = lambda i: (i // nt, 0, 0)
    once = dict(pipeline_mode=pl.Buffered(1))
    return pl.pallas_call(
        _merge_kernel,
        grid=(t // tm,),
        in_specs=[
            pl.BlockSpec((tm, d), row), pl.BlockSpec((tm, w), row), pl.BlockSpec((tm, w), row),
            pl.BlockSpec((1, w), const), pl.BlockSpec((tm, w), row), pl.BlockSpec((tm, w), row),
            pl.BlockSpec((1, m, w), per_b), pl.BlockSpec((1, m, w), per_b),
            pl.BlockSpec((tm, N_BRANCH * d), row),
            pl.BlockSpec((w, 2 * d), const, **once), pl.BlockSpec((w, d), const, **once),
            pl.BlockSpec((w, d), const, **once), pl.BlockSpec((d, d), const, **once), pl.BlockSpec((1, d), const),
            pl.BlockSpec((ne, d), const), pl.BlockSpec((ne, 1), const),
        ],
        out_specs=[pl.BlockSpec((tm, d), row), pl.BlockSpec((d // 2 // LANES, tm, LANES), lambda i: (0, i, 0)),
                   pl.BlockSpec((ne, tm), lambda i: (0, i))],
        out_shape=[jax.ShapeDtypeStruct((t, d), F32), jax.ShapeDtypeStruct((d // 2 // LANES, t, LANES), jnp.uint32),
                   jax.ShapeDtypeStruct((ne, t), F32)],
        compiler_params=_params("parallel"),
        name="merge",
    )(xt, ys, u, d_skip.reshape(1, w), om, xqn, kc, vc, gates,
      w_glu.astype(BF16), w_mo.astype(BF16), w_co.astype(BF16), w_out.astype(BF16),
      g_ffn.reshape(1, d), w_router.T.astype(BF16), b_router.reshape(ne, 1))


def _moe_kernel(blk_e_ref, blk_used_ref, xs_ref, wgu_ref, bgu_ref, wd_ref, bd_ref, y_ref, wgu_bf, wd_bf):
    i = pl.program_id(0)
    prev = blk_e_ref[jnp.maximum(i - 1, 0)]

    @pl.when((i == 0) | (blk_e_ref[i] != prev))
    def _():
        wgu_bf[...] = wgu_ref[0].astype(BF16)
        wd_bf[...] = wd_ref[0].astype(BF16)

    @pl.when(blk_used_ref[i] > 0)
    def _():
        de = wd_bf.shape[0]
        words = jnp.concatenate([xs_ref[j] for j in range(xs_ref.shape[0])], axis=1)
        xs = jnp.concatenate(_unpack_bf16_pairs(words), axis=1).astype(BF16)
        gu = _dot(xs, wgu_bf[...]) + bgu_ref[0]
        gate = jnp.minimum(gu[:, :de], SWIGLU_LIMIT)
        up = jnp.clip(gu[:, de:], -SWIGLU_LIMIT, SWIGLU_LIMIT)
        act = gate * _sigmoid(SWIGLU_ALPHA * gate) * (up + 1.0)
        y = _dot(act.astype(BF16), wd_bf[...]) + bd_ref[0]
        packed = _pack_bf16_pairs(y.astype(BF16).astype(F32))
        for j in range(y_ref.shape[0]):
            y_ref[j] = packed[:, j * LANES:(j + 1) * LANES]

    @pl.when(blk_used_ref[i] == 0)
    def _():
        y_ref[...] = jnp.zeros(y_ref.shape, y_ref.dtype)


def _moe_experts(xs, blk_e, blk_used, w_gu, b_gu, w_down, b_down):
    slabs, p, _ = xs.shape
    d = 2 * slabs * LANES
    ne, _, de2 = w_gu.shape
    de = de2 // 2
    nblk = p // EXPERT_ROWS
    row_spec = pl.BlockSpec((slabs, EXPERT_ROWS, LANES), lambda i, e, n: (0, i, 0))
    grid_spec = pltpu.PrefetchScalarGridSpec(
        num_scalar_prefetch=2,
        grid=(nblk,),
        in_specs=[
            row_spec,
            pl.BlockSpec((1, d, de2), lambda i, e, n: (e[i], 0, 0)),
            pl.BlockSpec((1, 1, de2), lambda i, e, n: (e[i], 0, 0)),
            pl.BlockSpec((1, de, d), lambda i, e, n: (e[i], 0, 0)),
            pl.BlockSpec((1, 1, d), lambda i, e, n: (e[i], 0, 0)),
        ],
        out_specs=row_spec,
        scratch_shapes=[pltpu.VMEM((d, de2), BF16), pltpu.VMEM((de, d), BF16)],
    )
    return pl.pallas_call(
        _moe_kernel,
        grid_spec=grid_spec,
        out_shape=jax.ShapeDtypeStruct(xs.shape, jnp.uint32),
        compiler_params=_params("arbitrary"),
        name="moe_experts",
    )(blk_e, blk_used, xs, w_gu, b_gu.reshape(ne, 1, de2), w_down, b_down.reshape(ne, 1, d))


def _router_kernel(lg_ref, tri_ref, e_ref, w_ref, r_ref, cnt_out_ref, cnt_ref):
    t = pl.program_id(0)
    ne, ts = lg_ref.shape

    @pl.when(t == 0)
    def _():
        cnt_ref[...] = jnp.zeros(cnt_ref.shape, F32)

    g = lg_ref[...]
    eid = lax.broadcasted_iota(jnp.int32, g.shape, 0)
    selected = jnp.zeros(g.shape, jnp.bool_)
    picks = []
    for _ in range(TOPK_EXPERTS):
        mx = jnp.max(g, axis=0, keepdims=True)
        idx = jnp.min(jnp.where(g == mx, eid, ne), axis=0, keepdims=True)
        hit = eid == idx
        picks.append((hit, idx, mx))
        selected = selected | hit
        g = jnp.where(hit, -jnp.inf, g)
    exps = [jnp.exp(mx - picks[0][2]) for (_, _, mx) in picks]
    total = sum(exps)
    sel01 = jnp.where(selected, 1.0, 0.0).astype(BF16)
    prefix = _dot(sel01, tri_ref[...]) + jnp.tile(cnt_ref[...], (1, ts // LANES))
    pad_i = [jnp.zeros((1, ts), jnp.int32)] * (8 - TOPK_EXPERTS)
    e_ref[...] = jnp.concatenate([idx for (_, idx, _) in picks] + pad_i, axis=0)
    w_ref[...] = jnp.concatenate([e / total for e in exps] + [jnp.zeros((1, ts), F32)] * (8 - TOPK_EXPERTS), axis=0)
    r_ref[...] = jnp.concatenate(
        [jnp.sum(jnp.where(hit, prefix, 0.0), axis=0, keepdims=True).astype(jnp.int32) for (hit, _, _) in picks]
        + pad_i, axis=0)
    cnt_ref[...] = cnt_ref[...] + _dot(sel01, jnp.ones((ts, LANES), BF16))
    cnt_out_ref[...] = cnt_ref[...]


def _router(logits_t, ts=1024):
    ne, t = logits_t.shape
    ts = min(ts, t)
    tri = (jnp.arange(ts)[:, None] < jnp.arange(ts)[None, :]).astype(BF16)
    rows = pl.BlockSpec((8, ts), lambda i: (0, i))
    return pl.pallas_call(
        _router_kernel,
        grid=(t // ts,),
        in_specs=[pl.BlockSpec((ne, ts), lambda i: (0, i)), pl.BlockSpec((ts, ts), lambda i: (0, 0))],
        out_specs=[rows, rows, rows, pl.BlockSpec((ne, LANES), lambda i: (0, 0))],
        out_shape=[jax.ShapeDtypeStruct((8, t), jnp.int32), jax.ShapeDtypeStruct((8, t), F32),
                   jax.ShapeDtypeStruct((8, t), jnp.int32), jax.ShapeDtypeStruct((ne, LANES), F32)],
        scratch_shapes=[pltpu.VMEM((ne, LANES), F32)],
        compiler_params=_params("arbitrary"),
        name="moe_router",
    )(logits_t, tri)


def _moe_mix_kernel(x1_ref, w_ref, pk_ref, o_ref):
    slabs = pk_ref.shape[0]
    wts = w_ref[...]
    lo = [jnp.zeros((x1_ref.shape[0], LANES), F32)] * slabs
    hi = [jnp.zeros((x1_ref.shape[0], LANES), F32)] * slabs
    for k in range(TOPK_EXPERTS):
        wk = wts[:, k:k + 1]
        for j in range(slabs):
            a, b = _unpack_bf16_pairs(pk_ref[j, k])
            lo[j] = lo[j] + wk * a
            hi[j] = hi[j] + wk * b
    o_ref[...] = x1_ref[...] + jnp.concatenate(lo + hi, axis=1)


def _moe_mix(x1, weights_tk, picked, tm=512):
    t, d = x1.shape
    slabs = picked.shape[0]
    return pl.pallas_call(
        _moe_mix_kernel,
        grid=(t // tm,),
        in_specs=[pl.BlockSpec((tm, d), lambda i: (i, 0)),
                  pl.BlockSpec((tm, TOPK_EXPERTS), lambda i: (i, 0)),
                  pl.BlockSpec((slabs, TOPK_EXPERTS, tm, LANES), lambda i: (0, 0, i, 0))],
        out_specs=pl.BlockSpec((tm, d), lambda i: (i, 0)),
        out_shape=jax.ShapeDtypeStruct((t, d), F32),
        compiler_params=_params("parallel"),
        name="moe_mix",
    )(x1, weights_tk, picked)


def _moe_ffn(x1, h2, logits_t, w_gu, b_gu, w_down, b_down):
    t, d = x1.shape
    tk = t * TOPK_EXPERTS
    e8, w8, r8, cnt = _router(logits_t)
    top_e, weights, rank = e8[:TOPK_EXPERTS], w8[:TOPK_EXPERTS], r8[:TOPK_EXPERTS]
    counts = cnt[:, 0].astype(jnp.int32)
    pcounts = ((counts + EXPERT_ROWS - 1) // EXPERT_ROWS) * EXPERT_ROWS
    pends = jnp.cumsum(pcounts)
    pstarts = pends - pcounts
    is_e = top_e[..., None] == jnp.arange(N_EXPERTS, dtype=jnp.int32)
    dest_by_k = jnp.sum(jnp.where(is_e, pstarts, 0), axis=-1) + rank
    nblk = -(-tk // EXPERT_ROWS) + N_EXPERTS
    blk_start = jnp.arange(nblk, dtype=jnp.int32) * EXPERT_ROWS
    blk_e = jnp.minimum(jnp.sum((pends[None, :] <= blk_start[:, None]).astype(jnp.int32), axis=1), N_EXPERTS - 1)
    blk_used = (blk_start < pends[-1]).astype(jnp.int32)
    slabs = d // 2 // LANES
    p = nblk * EXPERT_ROWS
    slab_off = jnp.arange(slabs, dtype=jnp.int32) * p
    dest_kst = (dest_by_k[:, None, :] + slab_off[None, :, None]).reshape(-1)
    dest_skt = (dest_by_k[None, :, :] + slab_off[:, None, None]).reshape(-1)
    xs = _sc_scatter_rows(h2.reshape(slabs * t, LANES), dest_kst, slabs * p, TOPK_EXPERTS)
    ys = _moe_experts(xs.reshape(slabs, p, LANES), blk_e, blk_used, w_gu, b_gu, w_down, b_down)
    picked = _sc_gather_rows(ys.reshape(slabs * p, LANES), dest_skt).reshape(slabs, TOPK_EXPERTS, t, LANES)
    return _moe_mix(x1, weights.T, picked)


def kernel(x, mem, g_mix, w_in, lam_re, lam_im, log_dt, b_re, b_im, c_re, c_im, d_skip, w_glu, g_q, g_k, w_moba_out, g_mem, w_kv_mem, g_cq, g_ck, w_cross_out, w_out, g_ffn, w_router, b_router, w_gu, b_gu, w_down, b_down):
    bsz, seq_len, d = x.shape
    xt = x.reshape(bsz * seq_len, d)
    for l in range(g_mix.shape[0]):
        u, q2, kv, xqn, gates, qh = _in_proj(xt, g_mix[l], w_in[l], g_q[l], g_k[l], g_cq[l], seq_len)
        qd, dest, tiles = _moba_dispatch(q2, qh, kv, bsz, seq_len)
        mats = _s5_matrices(lam_re[l], lam_im[l], log_dt[l], b_re[l], b_im[l], c_re[l], c_im[l],
                            seq_len // S5_CHUNK)
        ys = _s5(u, mats, bsz, seq_len)
        od, lsed = _moba_diag(q2, kv, bsz, seq_len)
        kc, vc = _mem_kv(mem, g_mem[l], w_kv_mem[l], g_ck[l])
        om = _moba_finish(qd, dest, tiles, od, lsed, kv, bsz, seq_len)
        x1, h2, logits = _merge(xt, ys, u, d_skip[l], om, xqn, kc, vc, gates, w_glu[l], w_moba_out[l],
                                w_cross_out[l], w_out[l], g_ffn[l], w_router[l], b_router[l], seq_len)
        xt = _moe_ffn(x1, h2, logits, w_gu[l], b_gu[l], w_down[l], b_down[l])
    return xt.reshape(bsz, seq_len, d)
```

```python
import functools
import math

import jax
import jax.numpy as jnp
from jax import lax
from jax.experimental import pallas as pl
from jax.experimental.pallas import tpu as pltpu
from jax.experimental.pallas import tpu_sc as plsc

F32 = jnp.float32
BF16 = jnp.bfloat16

EPS = 1e-6
N_BRANCH = 3
SSM_GROUP = 16
SSM_STATE = 64
S5_CHUNK = 16
MOBA_HEAD_DIM = 64
MOBA_BLOCK = 256
MOBA_TOPK = 3
MOBA_MAX_BLOCKS = 64
ROPE_THETA = 10000.0
X_HEADS = 4
N_EXPERTS = 32
TOPK_EXPERTS = 4
SWIGLU_LIMIT = 7.0
SWIGLU_ALPHA = 1.702
EXPERT_ROWS = 512
NEG_BIG = -1e30
LANES = 128
VMEM_LIMIT_BYTES = 56 * 1024 * 1024


def _params(*sem):
    return pltpu.CompilerParams(dimension_semantics=sem, vmem_limit_bytes=VMEM_LIMIT_BYTES)


def _sigmoid(x):
    return 1.0 / (1.0 + jnp.exp(-x))


def _dot(a, b):
    return jnp.dot(a, b, preferred_element_type=F32)


def _pack_bf16_pairs(x):
    n = x.shape[1] // 2
    lo = lax.bitcast_convert_type(x[:, :n], jnp.uint32) >> 16
    hi = lax.bitcast_convert_type(x[:, n:], jnp.uint32) & jnp.uint32(0xFFFF0000)
    return lo | hi


def _unpack_bf16_pairs(w):
    lo = lax.bitcast_convert_type(w << 16, F32)
    hi = lax.bitcast_convert_type(w & jnp.uint32(0xFFFF0000), F32)
    return lo, hi


def _dot_nt(a, b):
    return lax.dot_general(a, b, (((1,), (1,)), ((), ())), preferred_element_type=F32)


def _inproj_kernel(x_ref, gmix_ref, wa_ref, wg_ref, e64_ref, gq_ref, gk_ref, gcq_ref, cos_ref, sin_ref,
                   u_ref, q_ref, kv_ref, xq_ref, g_ref, qh_ref):
    xf = x_ref[...]
    ms = jnp.mean(xf * xf, axis=-1, keepdims=True)
    h = (xf * lax.rsqrt(ms + EPS) * gmix_ref[...]).astype(BF16)
    w = u_ref.shape[1]
    proj = lambda c: _dot(h, wa_ref[:, c * w:(c + 1) * w])
    u_ref[...] = proj(0)

    cos = jnp.tile(cos_ref[...], (1, w // LANES))
    sin = jnp.tile(sin_ref[...], (1, w // LANES))
    lane = lax.broadcasted_iota(jnp.int32, (xf.shape[0], w), 1)
    first_half = (lane % MOBA_HEAD_DIM) < (MOBA_HEAD_DIM // 2)

    def qk_norm_rope(raw, g):
        ss = _dot((raw * raw).astype(BF16), e64_ref[...])
        n = raw * lax.rsqrt(ss * (1.0 / MOBA_HEAD_DIM) + EPS) * g
        rot = jnp.where(first_half,
                        pltpu.roll(n, w - MOBA_HEAD_DIM // 2, 1),
                        pltpu.roll(n, MOBA_HEAD_DIM // 2, 1))
        return n * cos + rot * sin

    q = qk_norm_rope(proj(1), gq_ref[...]) * (MOBA_HEAD_DIM ** -0.5)
    q_ref[...] = q.astype(BF16)
    in_a = lax.broadcasted_iota(jnp.int32, (xf.shape[0], LANES), 1) < MOBA_HEAD_DIM
    for p in range(w // LANES):
        pair = q[:, p * LANES:(p + 1) * LANES].astype(BF16).astype(F32)
        qh_ref[2 * p] = jnp.where(in_a, pair, 0.0)
        qh_ref[2 * p + 1] = jnp.where(in_a, 0.0, pair)
    k = qk_norm_rope(proj(2), gk_ref[...]).astype(BF16)
    v = proj(3).astype(BF16)
    for p in range(w // LANES):
        kv_ref[:, (2 * p) * LANES:(2 * p + 1) * LANES] = k[:, p * LANES:(p + 1) * LANES]
        kv_ref[:, (2 * p + 1) * LANES:(2 * p + 2) * LANES] = v[:, p * LANES:(p + 1) * LANES]

    xq = proj(4)
    hd = w // X_HEADS
    for c in range(X_HEADS):
        chunk = xq[:, c * hd:(c + 1) * hd]
        cms = jnp.mean(chunk * chunk, axis=-1, keepdims=True)
        xq_ref[:, c * hd:(c + 1) * hd] = (chunk * lax.rsqrt(cms + EPS) * gcq_ref[...]).astype(BF16)

    d = xf.shape[1]
    for c in range(N_BRANCH):
        z = _dot(h, wg_ref[:, c * d:(c + 1) * d])
        g_ref[:, c * d:(c + 1) * d] = _sigmoid(z).astype(BF16)


def _in_proj(xt, g_mix, w_in, g_q, g_k, g_cq, seq_len, tm=512):
    t, d = xt.shape
    tm = min(tm, seq_len)
    w = d // 2
    wa = w_in[:, :5 * w].astype(BF16)
    wg = w_in[:, 5 * w:].astype(BF16)
    heads = w // MOBA_HEAD_DIM
    e64 = jnp.kron(jnp.eye(heads, dtype=F32), jnp.ones((MOBA_HEAD_DIM, MOBA_HEAD_DIM), F32)).astype(BF16)
    half = MOBA_HEAD_DIM // 2
    inv = ROPE_THETA ** (-jnp.arange(half, dtype=F32) / half)
    ang = jnp.arange(seq_len, dtype=F32)[:, None] * inv[None, :]
    cos = jnp.tile(jnp.cos(ang), (1, LANES // half))
    sin = jnp.tile(jnp.concatenate([-jnp.sin(ang), jnp.sin(ang)], axis=1), (1, LANES // MOBA_HEAD_DIM))
    nt = seq_len // tm
    row = lambda i: (i, 0)
    const = lambda i: (0, 0)
    out_w = jax.ShapeDtypeStruct((t, w), BF16)
    return pl.pallas_call(
        _inproj_kernel,
        grid=(t // tm,),
        in_specs=[
            pl.BlockSpec((tm, d), row),
            pl.BlockSpec((1, d), const),
            pl.BlockSpec((d, 5 * w), const, pipeline_mode=pl.Buffered(1)),
            pl.BlockSpec((d, N_BRANCH * d), const, pipeline_mode=pl.Buffered(1)),
            pl.BlockSpec((w, w), const, pipeline_mode=pl.Buffered(1)),
            pl.BlockSpec((1, w), const),
            pl.BlockSpec((1, w), const),
            pl.BlockSpec((1, w // X_HEADS), const),
            pl.BlockSpec((tm, LANES), lambda i: (i % nt, 0)),
            pl.BlockSpec((tm, LANES), lambda i: (i % nt, 0)),
        ],
        out_specs=[pl.BlockSpec((tm, w), row), pl.BlockSpec((tm, w), row), pl.BlockSpec((tm, 2 * w), row),
                   pl.BlockSpec((tm, w), row), pl.BlockSpec((tm, N_BRANCH * d), row),
                   pl.BlockSpec((heads, tm, LANES), lambda i: (0, i, 0))],
        out_shape=[jax.ShapeDtypeStruct((t, w), F32), out_w, jax.ShapeDtypeStruct((t, 2 * w), BF16), out_w,
                   jax.ShapeDtypeStruct((t, N_BRANCH * d), BF16),
                   jax.ShapeDtypeStruct((heads, t, LANES), F32)],
        compiler_params=_params("parallel"),
        name="in_proj",
    )(xt, g_mix.reshape(1, d), wa, wg, e64,
      jnp.tile(g_q, heads).reshape(1, w), jnp.tile(g_k, heads).reshape(1, w), g_cq.reshape(1, -1), cos, sin)


def _s5_matrices(lam_re, lam_im, log_dt, b_re, b_im, c_re, c_im, n_chunks):
    hp = lax.Precision.HIGHEST
    c = S5_CHUNK
    dt = jnp.exp(log_dt)[:, None]
    mag = jnp.exp(lam_re * dt)
    ar = mag * jnp.cos(lam_im * dt)
    ai = mag * jnp.sin(lam_im * dt)
    nr = ar - 1.0
    den = lam_re * lam_re + lam_im * lam_im
    cr = (nr * lam_re + ai * lam_im) / den
    ci = (ai * lam_re - nr * lam_im) / den
    bbr = cr[..., None] * b_re - ci[..., None] * b_im
    bbi = cr[..., None] * b_im + ci[..., None] * b_re

    def power(n):
        nf = n.astype(F32)[None, :, None]
        m = jnp.exp((lam_re * dt)[:, None, :] * nf)
        th = (lam_im * dt)[:, None, :] * nf
        return m * jnp.cos(th), m * jnp.sin(th)

    pr, pi = power(jnp.arange(c + 1))
    kbr = pr[..., None] * bbr[:, None] - pi[..., None] * bbi[:, None]
    kbi = pr[..., None] * bbi[:, None] + pi[..., None] * bbr[:, None]
    kk = jnp.einsum('ghp,gtpc->gthc', jnp.concatenate([c_re, -c_im], axis=-1),
                    jnp.concatenate([kbr, kbi], axis=2), precision=hp)
    tq = jnp.arange(c)
    g = kk.shape[0]
    rev = c - 1 - tq
    w_in = jnp.concatenate([kbr[:, rev].transpose(0, 1, 3, 2), kbi[:, rev].transpose(0, 1, 3, 2)], axis=-1)
    w_in = w_in.reshape(g, c * SSM_GROUP, 2 * SSM_STATE)
    prn, pin = pr[:, 1:], pi[:, 1:]
    wo_r = c_re[:, None] * prn[:, :, None, :] - c_im[:, None] * pin[:, :, None, :]
    wo_i = -c_re[:, None] * pin[:, :, None, :] - c_im[:, None] * prn[:, :, None, :]
    w_out = jnp.concatenate([wo_r, wo_i], axis=-1).transpose(0, 3, 1, 2).reshape(g, 2 * SSM_STATE, c * SSM_GROUP)
    n_steps = max(1, int(math.ceil(math.log2(n_chunks))))
    qr, qi = power(c * (2 ** jnp.arange(n_steps)))
    pa = jnp.concatenate([qr, qr], axis=-1)
    pb = jnp.concatenate([-qi, qi], axis=-1)
    return kk.astype(BF16), w_in.astype(BF16), w_out.astype(BF16), pa, pb


S5_SET = LANES // SSM_GROUP
S5_ROWS = 256


def _s5_kernel(u_ref, toep_ref, winc_ref, woutc_ref, pa_ref, pb_ref, y_ref, s_ref, win_ref, wout_ref):
    c = S5_CHUNK
    nc = u_ref.shape[0] // c
    rb = min(S5_ROWS, nc)
    p2 = 2 * SSM_STATE

    @pl.when(pl.program_id(1) == 0)
    def _():
        r_group = lax.broadcasted_iota(jnp.int32, (LANES, p2), 0) // SSM_GROUP
        l_group = lax.broadcasted_iota(jnp.int32, (p2, LANES), 1) // SSM_GROUP
        for t in range(c):
            for g in range(S5_SET):
                win_ref[0, t * LANES:(t + 1) * LANES, g * p2:(g + 1) * p2] = jnp.where(
                    r_group == g, winc_ref[0, t], jnp.zeros((LANES, p2), BF16))
                wout_ref[0, g * p2:(g + 1) * p2, t * LANES:(t + 1) * LANES] = jnp.where(
                    l_group == g, woutc_ref[0, t], jnp.zeros((p2, LANES), BF16))

    def chunk_rows(r0):
        return jnp.concatenate([u_ref[pl.ds(r0 * c + t, rb, stride=c), :] for t in range(c)], axis=1).astype(BF16)

    for blk in range(nc // rb):
        s_ref[blk * rb:(blk + 1) * rb, :] = _dot(chunk_rows(blk * rb), win_ref[0])
    row = lax.broadcasted_iota(jnp.int32, (nc, LANES), 0)
    n_steps = pa_ref.shape[1]
    for g in range(S5_SET):
        lanes = slice(g * LANES, (g + 1) * LANES)
        s = s_ref[:, lanes]
        for k in range(n_steps):
            sh = 1 << k
            if sh >= nc:
                break
            prev = jnp.where(row >= sh, pltpu.roll(s, sh, 0), 0.0)
            s = s + pa_ref[0, k:k + 1, lanes] * prev + pb_ref[0, k:k + 1, lanes] * pltpu.roll(prev, SSM_STATE, 1)
        s_ref[:, lanes] = jnp.where(row >= 1, pltpu.roll(s, 1, 0), 0.0)
    for blk in range(nc // rb):
        rows = slice(blk * rb, (blk + 1) * rb)
        a = chunk_rows(blk * rb)
        carried = _dot(s_ref[rows, :].astype(BF16), wout_ref[0])
        for t in range(c):
            y = _dot(a[:, :(t + 1) * LANES], toep_ref[0, (c - 1 - t) * LANES:, :])
            y_ref[pl.ds(blk * rb * c + t, rb, stride=c), :] = y + carried[:, t * LANES:(t + 1) * LANES]


def _s5_block_diag(mats):
    kk, w_in, w_out, pa, pb = mats
    g, c, h, p2 = kk.shape[0], S5_CHUNK, SSM_GROUP, 2 * SSM_STATE
    ns = g // S5_SET
    eye = jnp.eye(S5_SET, dtype=kk.dtype)
    lag_bd = jnp.einsum('sglhc,gk->slgckh', kk.reshape(ns, S5_SET, c + 1, h, h), eye).reshape(ns, c + 1, LANES, LANES)
    toep_bd = lag_bd[:, c - 1::-1].reshape(ns, c * LANES, LANES)

    w_in_c = w_in.reshape(ns, S5_SET, c, h, p2).transpose(0, 2, 1, 3, 4).reshape(ns, c, LANES, p2)
    w_out_c = w_out.reshape(ns, S5_SET, p2, c, h).transpose(0, 3, 2, 1, 4).reshape(ns, c, p2, LANES)
    k = pa.shape[1]
    lanes_of = lambda x: x.reshape(ns, S5_SET, k, p2).transpose(0, 2, 1, 3).reshape(ns, k, S5_SET * p2)
    return toep_bd, w_in_c, w_out_c, lanes_of(pa), lanes_of(pb)


def _s5(u, mats, bsz, seq_len):
    toep, w_in, w_out, pa, pb = _s5_block_diag(mats)
    ns = toep.shape[0]
    nc = seq_len // S5_CHUNK
    per_set = lambda s, b: (s, 0, 0)
    once = dict(pipeline_mode=pl.Buffered(1))
    io_spec = pl.BlockSpec((seq_len, LANES), lambda s, b: (b, s), **once)
    return pl.pallas_call(
        _s5_kernel,
        grid=(ns, bsz),
        in_specs=[
            io_spec,
            pl.BlockSpec((1,) + toep.shape[1:], per_set, **once),
            pl.BlockSpec((1,) + w_in.shape[1:], lambda s, b: (s, 0, 0, 0)),
            pl.BlockSpec((1,) + w_out.shape[1:], lambda s, b: (s, 0, 0, 0)),
            pl.BlockSpec((1,) + pa.shape[1:], per_set),
            pl.BlockSpec((1,) + pb.shape[1:], per_set),
        ],
        out_specs=io_spec,
        out_shape=jax.ShapeDtypeStruct(u.shape, F32),
        scratch_shapes=[pltpu.VMEM((nc, S5_SET * 2 * SSM_STATE), F32),
                        pltpu.VMEM((1, S5_CHUNK * LANES, S5_SET * 2 * SSM_STATE), BF16),
                        pltpu.VMEM((1, S5_SET * 2 * SSM_STATE, S5_CHUNK * LANES), BF16)],
        compiler_params=_params("arbitrary", "arbitrary"),
        name="s5_scan",
    )(u, toep, w_in, w_out, pa, pb)


def _moba_select_kernel(q_ref, k_ref, tri_ref, sel_ref, cnt_out_ref, km_ref, cnt_ref):
    h = pl.program_id(1)
    t = pl.program_id(2)
    ts = q_ref.shape[0]
    nbk = MOBA_MAX_BLOCKS

    @pl.when(t == 0)
    def _():
        kk = k_ref[...].astype(F32)
        nb = kk.shape[0] // MOBA_BLOCK
        km = jnp.sum(kk.reshape(nb, MOBA_BLOCK, LANES), axis=1) * (1.0 / MOBA_BLOCK)
        if nb < nbk:
            km = jnp.concatenate([km, jnp.zeros((nbk - nb, LANES), F32)], axis=0)
        lane = lax.broadcasted_iota(jnp.int32, (nbk, LANES), 1)
        km_ref[...] = jnp.where((lane // MOBA_HEAD_DIM) == (h % 2), km, 0.0).astype(BF16)
        cnt_ref[...] = jnp.zeros(cnt_ref.shape, F32)

    gate = _dot_nt(km_ref[...], q_ref[...])
    blk = lax.broadcasted_iota(jnp.int32, gate.shape, 0)
    qblk = (t * ts + lax.broadcasted_iota(jnp.int32, gate.shape, 1)) // MOBA_BLOCK
    g = jnp.where(blk < qblk, gate, -jnp.inf)
    selected = jnp.zeros(gate.shape, jnp.bool_)
    picks = []
    for _ in range(MOBA_TOPK):
        mx = jnp.max(g, axis=0, keepdims=True)
        idx = jnp.min(jnp.where(g == mx, blk, nbk), axis=0, keepdims=True)
        hit = blk == idx
        ok = (idx[0:1] < qblk[0:1]) & (mx > -jnp.inf)
        picks.append((hit, idx, ok))
        selected = selected | (hit & ok)
        g = jnp.where(hit, -jnp.inf, g)
    sel01 = jnp.where(selected, 1.0, 0.0)
    chunks = [sel01[:, c * LANES:(c + 1) * LANES] for c in range(ts // LANES)]
    within = _dot(jnp.concatenate(chunks, axis=0).astype(BF16), tri_ref[...])
    base = cnt_ref[...]
    pieces = []
    for c, chunk in enumerate(chunks):
        pieces.append(within[c * nbk:(c + 1) * nbk] + base)
        base = base + jnp.sum(chunk, axis=1, keepdims=True)
    prefix = jnp.concatenate(pieces, axis=1)
    rows = [jnp.where(ok, idx, -1) for (_, idx, ok) in picks]
    rows += [jnp.sum(jnp.where(hit, prefix, 0.0), axis=0, keepdims=True).astype(jnp.int32) for (hit, _, _) in picks]
    rows += [jnp.zeros((1, ts), jnp.int32)] * (8 - 2 * MOBA_TOPK)
    sel_ref[0, 0] = jnp.concatenate(rows, axis=0)
    cnt_ref[...] = base
    cnt_out_ref[0, 0] = base


def _moba_select(q2, kv, bsz, seq_len, ts=1024):
    ts = min(ts, seq_len)
    heads = q2.shape[1] // MOBA_HEAD_DIM
    nt = seq_len // ts
    tri = (jnp.arange(LANES)[:, None] < jnp.arange(LANES)[None, :]).astype(BF16)
    return pl.pallas_call(
        _moba_select_kernel,
        grid=(bsz, heads, nt),
        in_specs=[
            pl.BlockSpec((ts, LANES), lambda b, h, t: (b * nt + t, h // 2)),
            pl.BlockSpec((seq_len, LANES), lambda b, h, t: (b, 2 * (h // 2))),
            pl.BlockSpec((LANES, LANES), lambda b, h, t: (0, 0)),
        ],
        out_specs=[pl.BlockSpec((1, 1, 8, ts), lambda b, h, t: (b, h, 0, t)),
                   pl.BlockSpec((1, 1, MOBA_MAX_BLOCKS, LANES), lambda b, h, t: (b, h, 0, 0))],
        out_shape=[jax.ShapeDtypeStruct((bsz, heads, 8, seq_len), jnp.int32),
                   jax.ShapeDtypeStruct((bsz, heads, MOBA_MAX_BLOCKS, LANES), F32)],
        scratch_shapes=[pltpu.VMEM((MOBA_MAX_BLOCKS, LANES), BF16), pltpu.VMEM((MOBA_MAX_BLOCKS, LANES), F32)],
        compiler_params=_params("parallel", "arbitrary", "arbitrary"),
        name="moba_select",
    )(q2, kv, tri)


MOBA_DIAG_BLOCKS = 4


def _moba_diag_kernel(q_ref, kv_ref, o_ref, lse_ref):
    lane = lax.broadcasted_iota(jnp.int32, (MOBA_BLOCK, LANES), 1)
    is_a = lane < MOBA_HEAD_DIM
    r = lax.broadcasted_iota(jnp.int32, (MOBA_BLOCK, MOBA_BLOCK), 0)
    c = lax.broadcasted_iota(jnp.int32, (MOBA_BLOCK, MOBA_BLOCK), 1)
    for u in range(MOBA_DIAG_BLOCKS):
        rows = pl.ds(u * MOBA_BLOCK, MOBA_BLOCK)
        q = q_ref[rows, :]
        kb = kv_ref[rows, :LANES]
        vb = kv_ref[rows, LANES:]
        pvs, ms = [], []
        for own in (is_a, jnp.logical_not(is_a)):
            s = jnp.where(c <= r, _dot_nt(jnp.where(own, q, jnp.zeros_like(q)), kb), NEG_BIG)
            m = jnp.max(s, axis=1, keepdims=True)
            p = jnp.exp(s - m).astype(BF16)
            pvs.append(_dot(p, jnp.where(own, vb, 1.0)))
            ms.append(m)
        num = jnp.where(is_a, pvs[0], pvs[1])
        den = pltpu.roll(jnp.where(is_a, pvs[1], pvs[0]), MOBA_HEAD_DIM, 1)
        o_ref[rows, :] = num / den
        lse_ref[rows, :] = jnp.where(is_a, ms[0], ms[1]) + jnp.log(den)


def _moba_diag(q2, kv, bsz, seq_len):
    npair = q2.shape[1] // LANES
    rows = MOBA_BLOCK * MOBA_DIAG_BLOCKS
    nb = seq_len // rows
    spec = pl.BlockSpec((rows, LANES), lambda b, p, i: (b * nb + i, p))
    out = jax.ShapeDtypeStruct(q2.shape, F32)
    return pl.pallas_call(
        _moba_diag_kernel,
        grid=(bsz, npair, nb),
        in_specs=[spec, pl.BlockSpec((rows, 2 * LANES), lambda b, p, i: (b * nb + i, p))],
        out_specs=[spec, spec],
        out_shape=[out, out],
        compiler_params=_params("parallel", "parallel", "parallel"),
        name="moba_diag",
    )(q2, kv)


MOBA_TILE = 256
MOBA_TILES_PER_STEP = 8


def _moba_grouped_kernel(tile_row_ref, tile_pair_ref, tile_half_ref, tile_real_ref, qd_ref, *refs):
    n = MOBA_TILES_PER_STEP
    kv_refs, o_ref = refs[:n], refs[n + 1]
    i = pl.program_id(0)
    lane = lax.broadcasted_iota(jnp.int32, (MOBA_TILE, LANES), 1)
    kv_lane = lax.broadcasted_iota(jnp.int32, (MOBA_BLOCK, LANES), 1)

    @pl.when(tile_real_ref[i * n] > 0)
    def _():
        for u in range(n):
            tile = i * n + u
            rows = pl.ds(u * MOBA_TILE, MOBA_TILE)
            own = (lane // MOBA_HEAD_DIM) == tile_half_ref[tile]
            s = _dot_nt(qd_ref[rows, :].astype(BF16), kv_refs[u][:, :LANES])
            m = jnp.max(s, axis=1, keepdims=True)
            p = jnp.exp(s - m).astype(BF16)
            vb = jnp.where((kv_lane // MOBA_HEAD_DIM) == tile_half_ref[tile], kv_refs[u][:, LANES:], 1.0)
            pv = _dot(p, vb)
            row_sum = pltpu.roll(pv, MOBA_HEAD_DIM, 1)
            part = jnp.where(own, pv / row_sum, m + jnp.log(pv))
            o_ref[rows, :] = jnp.where(tile_real_ref[tile] > 0, part, NEG_BIG)

    @pl.when(tile_real_ref[i * n] == 0)
    def _():
        o_ref[...] = jnp.full(o_ref.shape, NEG_BIG, F32)


def _moba_grouped(qd, kv, tile_row, tile_pair, tile_half, tile_real, run_after):
    n = MOBA_TILES_PER_STEP
    n_tiles = qd.shape[0] // MOBA_TILE

    def kv_spec(u):
        return pl.BlockSpec((MOBA_BLOCK, 2 * LANES), lambda i, tr, tp, th, tl: (tr[i * n + u], tp[i * n + u]))

    grid_spec = pltpu.PrefetchScalarGridSpec(
        num_scalar_prefetch=4,
        grid=(n_tiles // n,),
        in_specs=[pl.BlockSpec((n * MOBA_TILE, LANES), lambda i, tr, tp, th, tl: (i, 0))]
        + [kv_spec(u) for u in range(n)]
        + [pl.BlockSpec((8, LANES), lambda i, tr, tp, th, tl: (0, 0))],
        out_specs=pl.BlockSpec((n * MOBA_TILE, LANES), lambda i, tr, tp, th, tl: (i, 0)),
    )
    return pl.pallas_call(
        _moba_grouped_kernel,
        grid_spec=grid_spec,
        out_shape=jax.ShapeDtypeStruct(qd.shape, F32),
        compiler_params=_params("arbitrary"),
        name="moba_grouped",
    )(tile_row, tile_pair, tile_half, tile_real, qd, *([kv] * n), run_after)


def _moba_combine_kernel(od_ref, lsed_ref, *refs):
    g_refs, o_ref = refs[:-1], refs[-1]
    lane = lax.broadcasted_iota(jnp.int32, od_ref.shape, 1)
    is_a = lane < MOBA_HEAD_DIM
    parts = [(od_ref[...], lsed_ref[...])]
    for s in range(MOBA_TOPK):
        xa = g_refs[s][0, 0, 0]
        xb = g_refs[MOBA_TOPK + s][0, 0, 0]
        o = jnp.where(is_a, xa, xb)
        lse = pltpu.roll(jnp.where(is_a, xb, xa), MOBA_HEAD_DIM, 1)
        parts.append((o, lse))
    m = parts[0][1]
    for _, lse in parts[1:]:
        m = jnp.maximum(m, lse)
    num = jnp.zeros(od_ref.shape, F32)
    den = jnp.zeros(od_ref.shape, F32)
    for o, lse in parts:
        w = jnp.exp(lse - m)
        num = num + w * o
        den = den + w
    o_ref[...] = (num / den).astype(BF16)


def _moba_combine(od, lsed, gath, bsz, seq_len, tm=1024):
    tm = min(tm, seq_len)
    npair = od.shape[1] // LANES
    nt = seq_len // tm
    spec = pl.BlockSpec((tm, LANES), lambda b, p, i: (b * nt + i, p))

    def g_spec(e, s):
        return pl.BlockSpec((1, 1, 1, tm, LANES), lambda b, p, i: (b, 2 * p + e, s, i, 0))

    return pl.pallas_call(
        _moba_combine_kernel,
        grid=(bsz, npair, nt),
        in_specs=[spec, spec] + [g_spec(e, s) for e in range(2) for s in range(MOBA_TOPK)],
        out_specs=spec,
        out_shape=jax.ShapeDtypeStruct(od.shape, BF16),
        compiler_params=_params("parallel", "parallel", "parallel"),
        name="moba_combine",
    )(od, lsed, *([gath] * (2 * MOBA_TOPK)))


SC_WINDOW = 128
SC_CORES = 2
SC_SUBCORES = 16


def _sc_mesh():
    return plsc.VectorSubcoreMesh(core_axis_name="core", subcore_axis_name="subcore")


def _sc_gather_rows(table, idx):
    n = idx.shape[0]
    d = table.shape[1]
    window = SC_WINDOW
    assert n % (window * SC_CORES * SC_SUBCORES) == 0
    per_core = n // window // SC_CORES

    @functools.partial(pl.kernel, out_type=jax.ShapeDtypeStruct((n, d), table.dtype), mesh=_sc_mesh())
    def gather_kernel(x_hbm, i_hbm, o_hbm):
        base = lax.axis_index("core") * per_core

        def body(i_vmem, o_vmem):
            pltpu.sync_copy(x_hbm.at[i_vmem.at[0]], o_vmem)

        pltpu.emit_pipeline(
            body,
            grid=(per_core,),
            in_specs=[pl.BlockSpec((1, window), index_map=lambda i: (0, base + i))],
            out_specs=[pl.BlockSpec((window, d), index_map=lambda i: (base + i, 0))],
            core_axis_name="subcore",
            dimension_semantics=(pltpu.PARALLEL,),
            trace_scopes=False,
        )(i_hbm, o_hbm)

    return gather_kernel(table, idx.reshape(1, n))


def _sc_scatter_rows(rows, dest, n_out, repeat):
    n_src, d = rows.shape
    n = dest.shape[0]
    assert n == repeat * n_src
    window = SC_WINDOW
    assert n % (window * SC_CORES * SC_SUBCORES) == 0
    per_core = n // window // SC_CORES
    src_windows = n_src // window

    @functools.partial(pl.kernel, out_type=jax.ShapeDtypeStruct((n_out, d), rows.dtype), mesh=_sc_mesh())
    def scatter_kernel(x_hbm, i_hbm, o_hbm):
        base = lax.axis_index("core") * per_core

        def body(x_vmem, i_vmem):
            pltpu.sync_copy(x_vmem, o_hbm.at[i_vmem.at[0]])

        pltpu.emit_pipeline(
            body,
            grid=(per_core,),
            in_specs=[pl.BlockSpec((window, d), index_map=lambda i: ((base + i) % src_windows, 0)),
                      pl.BlockSpec((1, window), index_map=lambda i: (0, base + i))],
            out_specs=[],
            core_axis_name="subcore",
            dimension_semantics=(pltpu.PARALLEL,),
            trace_scopes=False,
        )(x_hbm, i_hbm)

    return scatter_kernel(rows, dest.reshape(1, n))


def _moba_dispatch(q2, qh, kv, bsz, seq_len):
    heads = q2.shape[1] // MOBA_HEAD_DIM
    nbk = MOBA_MAX_BLOCKS
    sel, cnt = _moba_select(q2, kv, bsz, seq_len)
    counts = cnt[..., 0].astype(jnp.int32).reshape(bsz * heads * nbk)
    pcounts = ((counts + MOBA_TILE - 1) // MOBA_TILE) * MOBA_TILE
    pends = jnp.cumsum(pcounts)
    pstarts = (pends - pcounts).reshape(bsz, heads, 1, 1, nbk)
    n_items = bsz * heads * seq_len * MOBA_TOPK
    step_rows = MOBA_TILE * MOBA_TILES_PER_STEP
    n_rows = -(-(n_items + bsz * heads * nbk * MOBA_TILE) // step_rows) * step_rows
    n_null = step_rows
    idx = sel[:, :, 0:MOBA_TOPK, :]
    rank = sel[:, :, MOBA_TOPK:2 * MOBA_TOPK, :]
    start = jnp.sum(jnp.where(idx[..., None] == jnp.arange(nbk), pstarts, 0), axis=-1)
    null_row = n_rows + jnp.arange(seq_len, dtype=jnp.int32) % n_null
    dest = jnp.where(idx >= 0, start + rank, null_row)
    n_tiles = (n_rows + n_null) // MOBA_TILE
    n_groups = bsz * heads * nbk
    tile_start = jnp.arange(n_tiles, dtype=jnp.int32) * MOBA_TILE
    tile_g = jnp.minimum(jnp.sum((pends[None, :] <= tile_start[:, None]).astype(jnp.int32), axis=1), n_groups - 1)
    tile_real = (tile_start < pends[-1]).astype(jnp.int32)
    tile_head = (tile_g // nbk) % heads
    tile_row = (tile_g // (heads * nbk)) * (seq_len // MOBA_BLOCK) + jnp.minimum(tile_g % nbk, seq_len // MOBA_BLOCK - 1)
    dest_by_slot = dest.transpose(2, 1, 0, 3).reshape(-1)
    qd = _sc_scatter_rows(qh.reshape(-1, LANES), dest_by_slot, n_rows + n_null, MOBA_TOPK)
    return qd, dest, (tile_row, tile_head // 2, tile_head % 2, tile_real)


def _moba_finish(qd, dest, tiles, od, lsed, kv, bsz, seq_len):
    heads = dest.shape[1]
    part = _moba_grouped(qd, kv, *tiles, lsed)
    gath = _sc_gather_rows(part, dest.reshape(-1)).reshape(bsz, heads, MOBA_TOPK, seq_len, LANES)
    return _moba_combine(od, lsed, gath, bsz, seq_len)


def _mem_kv_kernel(mem_ref, gmem_ref, w_ref, gck_ref, k_ref, v_ref):
    xf = mem_ref[0]
    ms = jnp.mean(xf * xf, axis=-1, keepdims=True)
    h = (xf * lax.rsqrt(ms + EPS) * gmem_ref[...]).astype(BF16)
    kv = _dot(h, w_ref[...])
    w = k_ref.shape[2]
    hd = w // X_HEADS
    for c in range(X_HEADS):
        chunk = kv[:, c * hd:(c + 1) * hd]
        cms = jnp.mean(chunk * chunk, axis=-1, keepdims=True)
        k_ref[0, :, c * hd:(c + 1) * hd] = (chunk * lax.rsqrt(cms + EPS) * gck_ref[...]).astype(BF16)
    v_ref[0] = kv[:, w:].astype(BF16)


def _mem_kv(mem, g_mem, w_kv_mem, g_ck):
    bsz, m, d = mem.shape
    w = w_kv_mem.shape[1] // 2
    const = lambda b: (0, 0)
    out = jax.ShapeDtypeStruct((bsz, m, w), BF16)
    return pl.pallas_call(
        _mem_kv_kernel,
        grid=(bsz,),
        in_specs=[pl.BlockSpec((1, m, d), lambda b: (b, 0, 0)), pl.BlockSpec((1, d), const),
                  pl.BlockSpec((d, 2 * w), const), pl.BlockSpec((1, w // X_HEADS), const)],
        out_specs=[pl.BlockSpec((1, m, w), lambda b: (b, 0, 0))] * 2,
        out_shape=[out, out],
        compiler_params=_params("parallel"),
        name="mem_kv",
    )(mem, g_mem.reshape(1, d), w_kv_mem.astype(BF16), g_ck.reshape(1, -1))


def _merge_kernel(x_ref, ys_ref, u_ref, dskip_ref, om_ref, xq_ref, kc_ref, vc_ref, g_ref,
                  wglu_ref, wmo_ref, wco_ref, wout_ref, gffn_ref, wr_ref, br_ref,
                  x1_ref, h2_ref, logit_ref):
    d = x_ref.shape[1]
    y = ys_ref[...].astype(F32) + dskip_ref[...] * u_ref[...].astype(F32)
    ge = 0.5 * y * (1.0 + jnp.tanh(math.sqrt(2.0 / math.pi) * (y + 0.044715 * (y * y * y))))
    z = _dot(ge.astype(BF16), wglu_ref[...])
    merged = g_ref[:, 0:d].astype(F32) * (z[:, :d] * _sigmoid(z[:, d:]))
    merged = merged + g_ref[:, d:2 * d].astype(F32) * _dot(om_ref[...], wmo_ref[...])
    w = xq_ref.shape[1]
    hd = w // X_HEADS
    heads = []
    for c in range(X_HEADS):
        s = _dot_nt(xq_ref[:, c * hd:(c + 1) * hd], kc_ref[0, :, c * hd:(c + 1) * hd]) * (hd ** -0.5)
        p = jnp.exp(s - jnp.max(s, axis=1, keepdims=True))
        p = p / jnp.sum(p, axis=1, keepdims=True)
        heads.append(_dot(p.astype(BF16), vc_ref[0, :, c * hd:(c + 1) * hd]))
    oc = jnp.concatenate(heads, axis=1).astype(BF16)
    merged = merged + g_ref[:, 2 * d:3 * d].astype(F32) * _dot(oc, wco_ref[...])
    x1 = x_ref[...] + _dot(merged.astype(BF16), wout_ref[...])
    x1_ref[...] = x1
    ms = jnp.mean(x1 * x1, axis=-1, keepdims=True)
    h2 = (x1 * lax.rsqrt(ms + EPS) * gffn_ref[...]).astype(BF16)
    packed = _pack_bf16_pairs(h2.astype(F32))
    for j in range(h2_ref.shape[0]):
        h2_ref[j] = packed[:, j * LANES:(j + 1) * LANES]
    logit_ref[...] = _dot_nt(wr_ref[...], h2) + br_ref[...]


def _merge(xt, ys, u, d_skip, om, xqn, kc, vc, gates, w_glu, w_mo, w_co, w_out, g_ffn, w_router, b_router,
           seq_len, tm=512):
    t, d = xt.shape
    tm = min(tm, seq_len)
    w = d // 2
    m = kc.shape[1]
    ne = w_router.shape[1]
    nt = seq_len // tm
    row = lambda i: (i, 0)
    const = lambda i: (0, 0)
    per_b = lambda i: (i // nt, 0, 0)
    once = dict(pipeline_mode=pl.Buffered(1))
    return pl.pallas_call(
        _merge_kernel,
        grid=(t // tm,),
        in_specs=[
            pl.BlockSpec((tm, d), row), pl.BlockSpec((tm, w), row), pl.BlockSpec((tm, w), row),
            pl.BlockSpec((1, w), const), pl.BlockSpec((tm, w), row), pl.BlockSpec((tm, w), row),
            pl.BlockSpec((1, m, w), per_b), pl.BlockSpec((1, m, w), per_b),
            pl.BlockSpec((tm, N_BRANCH * d), row),
            pl.BlockSpec((w, 2 * d), const, **once), pl.BlockSpec((w, d), const, **once),
            pl.BlockSpec((w, d), const, **once), pl.BlockSpec((d, d), const, **once), pl.BlockSpec((1, d), const),
            pl.BlockSpec((ne, d), const), pl.BlockSpec((ne, 1), const),
        ],
        out_specs=[pl.BlockSpec((tm, d), row), pl.BlockSpec((d // 2 // LANES, tm, LANES), lambda i: (0, i, 0)),
                   pl.BlockSpec((ne, tm), lambda i: (0, i))],
        out_shape=[jax.ShapeDtypeStruct((t, d), F32), jax.ShapeDtypeStruct((d // 2 // LANES, t, LANES), jnp.uint32),
                   jax.ShapeDtypeStruct((ne, t), F32)],
        compiler_params=_params("parallel"),
        name="merge",
    )(xt, ys, u, d_skip.reshape(1, w), om, xqn, kc, vc, gates,
      w_glu.astype(BF16), w_mo.astype(BF16), w_co.astype(BF16), w_out.astype(BF16),
      g_ffn.reshape(1, d), w_router.T.astype(BF16), b_router.reshape(ne, 1))


def _moe_kernel(blk_e_ref, blk_used_ref, xs_ref, wgu_ref, bgu_ref, wd_ref, bd_ref, y_ref, wgu_bf, wd_bf):
    i = pl.program_id(0)
    prev = blk_e_ref[jnp.maximum(i - 1, 0)]

    @pl.when((i == 0) | (blk_e_ref[i] != prev))
    def _():
        wgu_bf[...] = wgu_ref[0].astype(BF16)
        wd_bf[...] = wd_ref[0].astype(BF16)

    @pl.when(blk_used_ref[i] > 0)
    def _():
        de = wd_bf.shape[0]
        words = jnp.concatenate([xs_ref[j] for j in range(xs_ref.shape[0])], axis=1)
        xs = jnp.concatenate(_unpack_bf16_pairs(words), axis=1).astype(BF16)
        gu = _dot(xs, wgu_bf[...]) + bgu_ref[0]
        gate = jnp.minimum(gu[:, :de], SWIGLU_LIMIT)
        up = jnp.clip(gu[:, de:], -SWIGLU_LIMIT, SWIGLU_LIMIT)
        act = gate * _sigmoid(SWIGLU_ALPHA * gate) * (up + 1.0)
        y = _dot(act.astype(BF16), wd_bf[...]) + bd_ref[0]
        packed = _pack_bf16_pairs(y.astype(BF16).astype(F32))
        for j in range(y_ref.shape[0]):
            y_ref[j] = packed[:, j * LANES:(j + 1) * LANES]

    @pl.when(blk_used_ref[i] == 0)
    def _():
        y_ref[...] = jnp.zeros(y_ref.shape, y_ref.dtype)


def _moe_experts(xs, blk_e, blk_used, w_gu, b_gu, w_down, b_down):
    slabs, p, _ = xs.shape
    d = 2 * slabs * LANES
    ne, _, de2 = w_gu.shape
    de = de2 // 2
    nblk = p // EXPERT_ROWS
    row_spec = pl.BlockSpec((slabs, EXPERT_ROWS, LANES), lambda i, e, n: (0, i, 0))
    grid_spec = pltpu.PrefetchScalarGridSpec(
        num_scalar_prefetch=2,
        grid=(nblk,),
        in_specs=[
            row_spec,
            pl.BlockSpec((1, d, de2), lambda i, e, n: (e[i], 0, 0)),
            pl.BlockSpec((1, 1, de2), lambda i, e, n: (e[i], 0, 0)),
            pl.BlockSpec((1, de, d), lambda i, e, n: (e[i], 0, 0)),
            pl.BlockSpec((1, 1, d), lambda i, e, n: (e[i], 0, 0)),
        ],
        out_specs=row_spec,
        scratch_shapes=[pltpu.VMEM((d, de2), BF16), pltpu.VMEM((de, d), BF16)],
    )
    return pl.pallas_call(
        _moe_kernel,
        grid_spec=grid_spec,
        out_shape=jax.ShapeDtypeStruct(xs.shape, jnp.uint32),
        compiler_params=_params("arbitrary"),
        name="moe_experts",
    )(blk_e, blk_used, xs, w_gu, b_gu.reshape(ne, 1, de2), w_down, b_down.reshape(ne, 1, d))


def _router_kernel(lg_ref, tri_ref, e_ref, w_ref, r_ref, cnt_out_ref, cnt_ref):
    t = pl.program_id(0)
    ne, ts = lg_ref.shape

    @pl.when(t == 0)
    def _():
        cnt_ref[...] = jnp.zeros(cnt_ref.shape, F32)

    g = lg_ref[...]
    eid = lax.broadcasted_iota(jnp.int32, g.shape, 0)
    selected = jnp.zeros(g.shape, jnp.bool_)
    picks = []
    for _ in range(TOPK_EXPERTS):
        mx = jnp.max(g, axis=0, keepdims=True)
        idx = jnp.min(jnp.where(g == mx, eid, ne), axis=0, keepdims=True)
        hit = eid == idx
        picks.append((hit, idx, mx))
        selected = selected | hit
        g = jnp.where(hit, -jnp.inf, g)
    exps = [jnp.exp(mx - picks[0][2]) for (_, _, mx) in picks]
    total = sum(exps)
    sel01 = jnp.where(selected, 1.0, 0.0).astype(BF16)
    prefix = _dot(sel01, tri_ref[...]) + jnp.tile(cnt_ref[...], (1, ts // LANES))
    pad_i = [jnp.zeros((1, ts), jnp.int32)] * (8 - TOPK_EXPERTS)
    e_ref[...] = jnp.concatenate([idx for (_, idx, _) in picks] + pad_i, axis=0)
    w_ref[...] = jnp.concatenate([e / total for e in exps] + [jnp.zeros((1, ts), F32)] * (8 - TOPK_EXPERTS), axis=0)
    r_ref[...] = jnp.concatenate(
        [jnp.sum(jnp.where(hit, prefix, 0.0), axis=0, keepdims=True).astype(jnp.int32) for (hit, _, _) in picks]
        + pad_i, axis=0)
    cnt_ref[...] = cnt_ref[...] + _dot(sel01, jnp.ones((ts, LANES), BF16))
    cnt_out_ref[...] = cnt_ref[...]


def _router(logits_t, ts=1024):
    ne, t = logits_t.shape
    ts = min(ts, t)
    tri = (jnp.arange(ts)[:, None] < jnp.arange(ts)[None, :]).astype(BF16)
    rows = pl.BlockSpec((8, ts), lambda i: (0, i))
    return pl.pallas_call(
        _router_kernel,
        grid=(t // ts,),
        in_specs=[pl.BlockSpec((ne, ts), lambda i: (0, i)), pl.BlockSpec((ts, ts), lambda i: (0, 0))],
        out_specs=[rows, rows, rows, pl.BlockSpec((ne, LANES), lambda i: (0, 0))],
        out_shape=[jax.ShapeDtypeStruct((8, t), jnp.int32), jax.ShapeDtypeStruct((8, t), F32),
                   jax.ShapeDtypeStruct((8, t), jnp.int32), jax.ShapeDtypeStruct((ne, LANES), F32)],
        scratch_shapes=[pltpu.VMEM((ne, LANES), F32)],
        compiler_params=_params("arbitrary"),
        name="moe_router",
    )(logits_t, tri)


def _moe_mix_kernel(x1_ref, w_ref, pk_ref, o_ref):
    slabs = pk_ref.shape[0]
    wts = w_ref[...]
    lo = [jnp.zeros((x1_ref.shape[0], LANES), F32)] * slabs
    hi = [jnp.zeros((x1_ref.shape[0], LANES), F32)] * slabs
    for k in range(TOPK_EXPERTS):
        wk = wts[:, k:k + 1]
        for j in range(slabs):
            a, b = _unpack_bf16_pairs(pk_ref[j, k])
            lo[j] = lo[j] + wk * a
            hi[j] = hi[j] + wk * b
    o_ref[...] = x1_ref[...] + jnp.concatenate(lo + hi, axis=1)


def _moe_mix(x1, weights_tk, picked, tm=512):
    t, d = x1.shape
    slabs = picked.shape[0]
    return pl.pallas_call(
        _moe_mix_kernel,
        grid=(t // tm,),
        in_specs=[pl.BlockSpec((tm, d), lambda i: (i, 0)),
                  pl.BlockSpec((tm, TOPK_EXPERTS), lambda i: (i, 0)),
                  pl.BlockSpec((slabs, TOPK_EXPERTS, tm, LANES), lambda i: (0, 0, i, 0))],
        out_specs=pl.BlockSpec((tm, d), lambda i: (i, 0)),
        out_shape=jax.ShapeDtypeStruct((t, d), F32),
        compiler_params=_params("parallel"),
        name="moe_mix",
    )(x1, weights_tk, picked)


def _moe_ffn(x1, h2, logits_t, w_gu, b_gu, w_down, b_down):
    t, d = x1.shape
    tk = t * TOPK_EXPERTS
    e8, w8, r8, cnt = _router(logits_t)
    top_e, weights, rank = e8[:TOPK_EXPERTS], w8[:TOPK_EXPERTS], r8[:TOPK_EXPERTS]
    counts = cnt[:, 0].astype(jnp.int32)
    pcounts = ((counts + EXPERT_ROWS - 1) // EXPERT_ROWS) * EXPERT_ROWS
    pends = jnp.cumsum(pcounts)
    pstarts = pends - pcounts
    is_e = top_e[..., None] == jnp.arange(N_EXPERTS, dtype=jnp.int32)
    dest_by_k = jnp.sum(jnp.where(is_e, pstarts, 0), axis=-1) + rank
    nblk = -(-tk // EXPERT_ROWS) + N_EXPERTS
    blk_start = jnp.arange(nblk, dtype=jnp.int32) * EXPERT_ROWS
    blk_e = jnp.minimum(jnp.sum((pends[None, :] <= blk_start[:, None]).astype(jnp.int32), axis=1), N_EXPERTS - 1)
    blk_used = (blk_start < pends[-1]).astype(jnp.int32)
    slabs = d // 2 // LANES
    p = nblk * EXPERT_ROWS
    slab_off = jnp.arange(slabs, dtype=jnp.int32) * p
    dest_kst = (dest_by_k[:, None, :] + slab_off[None, :, None]).reshape(-1)
    dest_skt = (dest_by_k[None, :, :] + slab_off[:, None, None]).reshape(-1)
    xs = _sc_scatter_rows(h2.reshape(slabs * t, LANES), dest_kst, slabs * p, TOPK_EXPERTS)
    ys = _moe_experts(xs.reshape(slabs, p, LANES), blk_e, blk_used, w_gu, b_gu, w_down, b_down)
    picked = _sc_gather_rows(ys.reshape(slabs * p, LANES), dest_skt).reshape(slabs, TOPK_EXPERTS, t, LANES)
    return _moe_mix(x1, weights.T, picked)


def kernel(x, mem, g_mix, w_in, lam_re, lam_im, log_dt, b_re, b_im, c_re, c_im, d_skip, w_glu, g_q, g_k, w_moba_out, g_mem, w_kv_mem, g_cq, g_ck, w_cross_out, w_out, g_ffn, w_router, b_router, w_gu, b_gu, w_down, b_down):
    bsz, seq_len, d = x.shape
    xt = x.reshape(bsz * seq_len, d)
    for l in range(g_mix.shape[0]):
        u, q2, kv, xqn, gates, qh = _in_proj(xt, g_mix[l], w_in[l], g_q[l], g_k[l], g_cq[l], seq_len)
        qd, dest, tiles = _moba_dispatch(q2, qh, kv, bsz, seq_len)
        mats = _s5_matrices(lam_re[l], lam_im[l], log_dt[l], b_re[l], b_im[l], c_re[l], c_im[l],
                            seq_len // S5_CHUNK)
        ys = _s5(u, mats, bsz, seq_len)
        od, lsed = _moba_diag(q2, kv, bsz, seq_len)
        kc, vc = _mem_kv(mem, g_mem[l], w_kv_mem[l], g_ck[l])
        om = _moba_finish(qd, dest, tiles, od, lsed, kv, bsz, seq_len)
        x1, h2, logits = _merge(xt, ys, u, d_skip[l], om, xqn, kc, vc, gates, w_glu[l], w_moba_out[l],
                                w_cross_out[l], w_out[l], g_ffn[l], w_router[l], b_router[l], seq_len)
        xt = _moe_ffn(x1, h2, logits, w_gu[l], b_gu[l], w_down[l], b_down[l])
    return xt.reshape(bsz, seq_len, d)
```

```python
import functools
import math

import jax
import jax.numpy as jnp
from jax import lax
from jax.experimental import pallas as pl
from jax.experimental.pallas import tpu as pltpu
from jax.experimental.pallas import tpu_sc as plsc

F32 = jnp.float32
BF16 = jnp.bfloat16

EPS = 1e-6
N_BRANCH = 3
SSM_GROUP = 16
SSM_STATE = 64
S5_CHUNK = 16
MOBA_HEAD_DIM = 64
MOBA_BLOCK = 256
MOBA_TOPK = 3
MOBA_MAX_BLOCKS = 64
ROPE_THETA = 10000.0
X_HEADS = 4
N_EXPERTS = 32
TOPK_EXPERTS = 4
SWIGLU_LIMIT = 7.0
SWIGLU_ALPHA = 1.702
EXPERT_ROWS = 512
NEG_BIG = -1e30
LANES = 128
VMEM_LIMIT_BYTES = 56 * 1024 * 1024


def _params(*sem):
    return pltpu.CompilerParams(dimension_semantics=sem, vmem_limit_bytes=VMEM_LIMIT_BYTES)


def _sigmoid(x):
    return 1.0 / (1.0 + jnp.exp(-x))


def _dot(a, b):
    return jnp.dot(a, b, preferred_element_type=F32)


def _pack_bf16_pairs(x):
    n = x.shape[1] // 2
    lo = lax.bitcast_convert_type(x[:, :n], jnp.uint32) >> 16
    hi = lax.bitcast_convert_type(x[:, n:], jnp.uint32) & jnp.uint32(0xFFFF0000)
    return lo | hi


def _unpack_bf16_pairs(w):
    lo = lax.bitcast_convert_type(w << 16, F32)
    hi = lax.bitcast_convert_type(w & jnp.uint32(0xFFFF0000), F32)
    return lo, hi


def _dot_nt(a, b):
    return lax.dot_general(a, b, (((1,), (1,)), ((), ())), preferred_element_type=F32)


def _inproj_kernel(x_ref, gmix_ref, wa_ref, wg_ref, e64_ref, gq_ref, gk_ref, gcq_ref, cos_ref, sin_ref,
                   u_ref, q_ref, kv_ref, xq_ref, g_ref, qh_ref):
    xf = x_ref[...]
    ms = jnp.mean(xf * xf, axis=-1, keepdims=True)
    h = (xf * lax.rsqrt(ms + EPS) * gmix_ref[...]).astype(BF16)
    w = u_ref.shape[1]
    proj = lambda c: _dot(h, wa_ref[:, c * w:(c + 1) * w])
    u_ref[...] = proj(0)

    cos = jnp.tile(cos_ref[...], (1, w // LANES))
    sin = jnp.tile(sin_ref[...], (1, w // LANES))
    lane = lax.broadcasted_iota(jnp.int32, (xf.shape[0], w), 1)
    first_half = (lane % MOBA_HEAD_DIM) < (MOBA_HEAD_DIM // 2)

    def qk_norm_rope(raw, g):
        ss = _dot((raw * raw).astype(BF16), e64_ref[...])
        n = raw * lax.rsqrt(ss * (1.0 / MOBA_HEAD_DIM) + EPS) * g
        rot = jnp.where(first_half,
                        pltpu.roll(n, w - MOBA_HEAD_DIM // 2, 1),
                        pltpu.roll(n, MOBA_HEAD_DIM // 2, 1))
        return n * cos + rot * sin

    q = qk_norm_rope(proj(1), gq_ref[...]) * (MOBA_HEAD_DIM ** -0.5)
    q_ref[...] = q.astype(BF16)
    in_a = lax.broadcasted_iota(jnp.int32, (xf.shape[0], LANES), 1) < MOBA_HEAD_DIM
    for p in range(w // LANES):
        pair = q[:, p * LANES:(p + 1) * LANES].astype(BF16).astype(F32)
        qh_ref[2 * p] = jnp.where(in_a, pair, 0.0)
        qh_ref[2 * p + 1] = jnp.where(in_a, 0.0, pair)
    k = qk_norm_rope(proj(2), gk_ref[...]).astype(BF16)
    v = proj(3).astype(BF16)
    for p in range(w // LANES):
        kv_ref[:, (2 * p) * LANES:(2 * p + 1) * LANES] = k[:, p * LANES:(p + 1) * LANES]
        kv_ref[:, (2 * p + 1) * LANES:(2 * p + 2) * LANES] = v[:, p * LANES:(p + 1) * LANES]

    xq = proj(4)
    hd = w // X_HEADS
    for c in range(X_HEADS):
        chunk = xq[:, c * hd:(c + 1) * hd]
        cms = jnp.mean(chunk * chunk, axis=-1, keepdims=True)
        xq_ref[:, c * hd:(c + 1) * hd] = (chunk * lax.rsqrt(cms + EPS) * gcq_ref[...]).astype(BF16)

    d = xf.shape[1]
    for c in range(N_BRANCH):
        z = _dot(h, wg_ref[:, c * d:(c + 1) * d])
        g_ref[:, c * d:(c + 1) * d] = _sigmoid(z).astype(BF16)


def _in_proj(xt, g_mix, w_in, g_q, g_k, g_cq, seq_len, tm=512):
    t, d = xt.shape
    tm = min(tm, seq_len)
    w = d // 2
    wa = w_in[:, :5 * w].astype(BF16)
    wg = w_in[:, 5 * w:].astype(BF16)
    heads = w // MOBA_HEAD_DIM
    e64 = jnp.kron(jnp.eye(heads, dtype=F32), jnp.ones((MOBA_HEAD_DIM, MOBA_HEAD_DIM), F32)).astype(BF16)
    half = MOBA_HEAD_DIM // 2
    inv = ROPE_THETA ** (-jnp.arange(half, dtype=F32) / half)
    ang = jnp.arange(seq_len, dtype=F32)[:, None] * inv[None, :]
    cos = jnp.tile(jnp.cos(ang), (1, LANES // half))
    sin = jnp.tile(jnp.concatenate([-jnp.sin(ang), jnp.sin(ang)], axis=1), (1, LANES // MOBA_HEAD_DIM))
    nt = seq_len // tm
    row = lambda i: (i, 0)
    const = lambda i: (0, 0)
    out_w = jax.ShapeDtypeStruct((t, w), BF16)
    return pl.pallas_call(
        _inproj_kernel,
        grid=(t // tm,),
        in_specs=[
            pl.BlockSpec((tm, d), row),
            pl.BlockSpec((1, d), const),
            pl.BlockSpec((d, 5 * w), const, pipeline_mode=pl.Buffered(1)),
            pl.BlockSpec((d, N_BRANCH * d), const, pipeline_mode=pl.Buffered(1)),
            pl.BlockSpec((w, w), const, pipeline_mode=pl.Buffered(1)),
            pl.BlockSpec((1, w), const),
            pl.BlockSpec((1, w), const),
            pl.BlockSpec((1, w // X_HEADS), const),
            pl.BlockSpec((tm, LANES), lambda i: (i % nt, 0)),
            pl.BlockSpec((tm, LANES), lambda i: (i % nt, 0)),
        ],
        out_specs=[pl.BlockSpec((tm, w), row), pl.BlockSpec((tm, w), row), pl.BlockSpec((tm, 2 * w), row),
                   pl.BlockSpec((tm, w), row), pl.BlockSpec((tm, N_BRANCH * d), row),
                   pl.BlockSpec((heads, tm, LANES), lambda i: (0, i, 0))],
        out_shape=[jax.ShapeDtypeStruct((t, w), F32), out_w, jax.ShapeDtypeStruct((t, 2 * w), BF16), out_w,
                   jax.ShapeDtypeStruct((t, N_BRANCH * d), BF16),
                   jax.ShapeDtypeStruct((heads, t, LANES), F32)],
        compiler_params=_params("parallel"),
        name="in_proj",
    )(xt, g_mix.reshape(1, d), wa, wg, e64,
      jnp.tile(g_q, heads).reshape(1, w), jnp.tile(g_k, heads).reshape(1, w), g_cq.reshape(1, -1), cos, sin)


def _s5_matrices(lam_re, lam_im, log_dt, b_re, b_im, c_re, c_im, n_chunks):
    hp = lax.Precision.HIGHEST
    c = S5_CHUNK
    dt = jnp.exp(log_dt)[:, None]
    mag = jnp.exp(lam_re * dt)
    ar = mag * jnp.cos(lam_im * dt)
    ai = mag * jnp.sin(lam_im * dt)
    nr = ar - 1.0
    den = lam_re * lam_re + lam_im * lam_im
    cr = (nr * lam_re + ai * lam_im) / den
    ci = (ai * lam_re - nr * lam_im) / den
    bbr = cr[..., None] * b_re - ci[..., None] * b_im
    bbi = cr[..., None] * b_im + ci[..., None] * b_re

    def power(n):
        nf = n.astype(F32)[None, :, None]
        m = jnp.exp((lam_re * dt)[:, None, :] * nf)
        th = (lam_im * dt)[:, None, :] * nf
        return m * jnp.cos(th), m * jnp.sin(th)

    pr, pi = power(jnp.arange(c + 1))
    kbr = pr[..., None] * bbr[:, None] - pi[..., None] * bbi[:, None]
    kbi = pr[..., None] * bbi[:, None] + pi[..., None] * bbr[:, None]
    kk = jnp.einsum('ghp,gtpc->gthc', jnp.concatenate([c_re, -c_im], axis=-1),
                    jnp.concatenate([kbr, kbi], axis=2), precision=hp)
    tq = jnp.arange(c)
    g = kk.shape[0]
    rev = c - 1 - tq
    w_in = jnp.concatenate([kbr[:, rev].transpose(0, 1, 3, 2), kbi[:, rev].transpose(0, 1, 3, 2)], axis=-1)
    w_in = w_in.reshape(g, c * SSM_GROUP, 2 * SSM_STATE)
    prn, pin = pr[:, 1:], pi[:, 1:]
    wo_r = c_re[:, None] * prn[:, :, None, :] - c_im[:, None] * pin[:, :, None, :]
    wo_i = -c_re[:, None] * pin[:, :, None, :] - c_im[:, None] * prn[:, :, None, :]
    w_out = jnp.concatenate([wo_r, wo_i], axis=-1).transpose(0, 3, 1, 2).reshape(g, 2 * SSM_STATE, c * SSM_GROUP)
    n_steps = max(1, int(math.ceil(math.log2(n_chunks))))
    qr, qi = power(c * (2 ** jnp.arange(n_steps)))
    pa = jnp.concatenate([qr, qr], axis=-1)
    pb = jnp.concatenate([-qi, qi], axis=-1)
    return kk.astype(BF16), w_in.astype(BF16), w_out.astype(BF16), pa, pb


S5_SET = LANES // SSM_GROUP
S5_ROWS = 256


def _s5_kernel(u_ref, toep_ref, winc_ref, woutc_ref, pa_ref, pb_ref, y_ref, s_ref, win_ref, wout_ref):
    c = S5_CHUNK
    nc = u_ref.shape[0] // c
    rb = min(S5_ROWS, nc)
    p2 = 2 * SSM_STATE

    @pl.when(pl.program_id(1) == 0)
    def _():
        r_group = lax.broadcasted_iota(jnp.int32, (LANES, p2), 0) // SSM_GROUP
        l_group = lax.broadcasted_iota(jnp.int32, (p2, LANES), 1) // SSM_GROUP
        for t in range(c):
            for g in range(S5_SET):
                win_ref[0, t * LANES:(t + 1) * LANES, g * p2:(g + 1) * p2] = jnp.where(
                    r_group == g, winc_ref[0, t], jnp.zeros((LANES, p2), BF16))
                wout_ref[0, g * p2:(g + 1) * p2, t * LANES:(t + 1) * LANES] = jnp.where(
                    l_group == g, woutc_ref[0, t], jnp.zeros((p2, LANES), BF16))

    def chunk_rows(r0):
        return jnp.concatenate([u_ref[pl.ds(r0 * c + t, rb, stride=c), :] for t in range(c)], axis=1).astype(BF16)

    for blk in range(nc // rb):
        s_ref[blk * rb:(blk + 1) * rb, :] = _dot(chunk_rows(blk * rb), win_ref[0])
    row = lax.broadcasted_iota(jnp.int32, (nc, LANES), 0)
    n_steps = pa_ref.shape[1]
    for g in range(S5_SET):
        lanes = slice(g * LANES, (g + 1) * LANES)
        s = s_ref[:, lanes]
        for k in range(n_steps):
            sh = 1 << k
            if sh >= nc:
                break
            prev = jnp.where(row >= sh, pltpu.roll(s, sh, 0), 0.0)
            s = s + pa_ref[0, k:k + 1, lanes] * prev + pb_ref[0, k:k + 1, lanes] * pltpu.roll(prev, SSM_STATE, 1)
        s_ref[:, lanes] = jnp.where(row >= 1, pltpu.roll(s, 1, 0), 0.0)
    for blk in range(nc // rb):
        rows = slice(blk * rb, (blk + 1) * rb)
        a = chunk_rows(blk * rb)
        carried = _dot(s_ref[rows, :].astype(BF16), wout_ref[0])
        for t in range(c):
            y = _dot(a[:, :(t + 1) * LANES], toep_ref[0, (c - 1 - t) * LANES:, :])
            y_ref[pl.ds(blk * rb * c + t, rb, stride=c), :] = y + carried[:, t * LANES:(t + 1) * LANES]


def _s5_block_diag(mats):
    kk, w_in, w_out, pa, pb = mats
    g, c, h, p2 = kk.shape[0], S5_CHUNK, SSM_GROUP, 2 * SSM_STATE
    ns = g // S5_SET
    eye = jnp.eye(S5_SET, dtype=kk.dtype)
    lag_bd = jnp.einsum('sglhc,gk->slgckh', kk.reshape(ns, S5_SET, c + 1, h, h), eye).reshape(ns, c + 1, LANES, LANES)
    toep_bd = lag_bd[:, c - 1::-1].reshape(ns, c * LANES, LANES)

    w_in_c = w_in.reshape(ns, S5_SET, c, h, p2).transpose(0, 2, 1, 3, 4).reshape(ns, c, LANES, p2)
    w_out_c = w_out.reshape(ns, S5_SET, p2, c, h).transpose(0, 3, 2, 1, 4).reshape(ns, c, p2, LANES)
    k = pa.shape[1]
    lanes_of = lambda x: x.reshape(ns, S5_SET, k, p2).transpose(0, 2, 1, 3).reshape(ns, k, S5_SET * p2)
    return toep_bd, w_in_c, w_out_c, lanes_of(pa), lanes_of(pb)


def _s5(u, mats, bsz, seq_len):
    toep, w_in, w_out, pa, pb = _s5_block_diag(mats)
    ns = toep.shape[0]
    nc = seq_len // S5_CHUNK
    per_set = lambda s, b: (s, 0, 0)
    once = dict(pipeline_mode=pl.Buffered(1))
    io_spec = pl.BlockSpec((seq_len, LANES), lambda s, b: (b, s), **once)
    return pl.pallas_call(
        _s5_kernel,
        grid=(ns, bsz),
        in_specs=[
            io_spec,
            pl.BlockSpec((1,) + toep.shape[1:], per_set, **once),
            pl.BlockSpec((1,) + w_in.shape[1:], lambda s, b: (s, 0, 0, 0)),
            pl.BlockSpec((1,) + w_out.shape[1:], lambda s, b: (s, 0, 0, 0)),
            pl.BlockSpec((1,) + pa.shape[1:], per_set),
            pl.BlockSpec((1,) + pb.shape[1:], per_set),
        ],
        out_specs=io_spec,
        out_shape=jax.ShapeDtypeStruct(u.shape, F32),
        scratch_shapes=[pltpu.VMEM((nc, S5_SET * 2 * SSM_STATE), F32),
                        pltpu.VMEM((1, S5_CHUNK * LANES, S5_SET * 2 * SSM_STATE), BF16),
                        pltpu.VMEM((1, S5_SET * 2 * SSM_STATE, S5_CHUNK * LANES), BF16)],
        compiler_params=_params("arbitrary", "arbitrary"),
        name="s5_scan",
    )(u, toep, w_in, w_out, pa, pb)


def _moba_select_kernel(q_ref, k_ref, tri_ref, sel_ref, cnt_out_ref, km_ref, cnt_ref):
    h = pl.program_id(1)
    t = pl.program_id(2)
    ts = q_ref.shape[0]
    nbk = MOBA_MAX_BLOCKS

    @pl.when(t == 0)
    def _():
        kk = k_ref[...].astype(F32)
        nb = kk.shape[0] // MOBA_BLOCK
        km = jnp.sum(kk.reshape(nb, MOBA_BLOCK, LANES), axis=1) * (1.0 / MOBA_BLOCK)
        if nb < nbk:
            km = jnp.concatenate([km, jnp.zeros((nbk - nb, LANES), F32)], axis=0)
        lane = lax.broadcasted_iota(jnp.int32, (nbk, LANES), 1)
        km_ref[...] = jnp.where((lane // MOBA_HEAD_DIM) == (h % 2), km, 0.0).astype(BF16)
        cnt_ref[...] = jnp.zeros(cnt_ref.shape, F32)

    gate = _dot_nt(km_ref[...], q_ref[...])
    blk = lax.broadcasted_iota(jnp.int32, gate.shape, 0)
    qblk = (t * ts + lax.broadcasted_iota(jnp.int32, gate.shape, 1)) // MOBA_BLOCK
    g = jnp.where(blk < qblk, gate, -jnp.inf)
    selected = jnp.zeros(gate.shape, jnp.bool_)
    picks = []
    for _ in range(MOBA_TOPK):
        mx = jnp.max(g, axis=0, keepdims=True)
        idx = jnp.min(jnp.where(g == mx, blk, nbk), axis=0, keepdims=True)
        hit = blk == idx
        ok = (idx[0:1] < qblk[0:1]) & (mx > -jnp.inf)
        picks.append((hit, idx, ok))
        selected = selected | (hit & ok)
        g = jnp.where(hit, -jnp.inf, g)
    sel01 = jnp.where(selected, 1.0, 0.0)
    chunks = [sel01[:, c * LANES:(c + 1) * LANES] for c in range(ts // LANES)]
    within = _dot(jnp.concatenate(chunks, axis=0).astype(BF16), tri_ref[...])
    base = cnt_ref[...]
    pieces = []
    for c, chunk in enumerate(chunks):
        pieces.append(within[c * nbk:(c + 1) * nbk] + base)
        base = base + jnp.sum(chunk, axis=1, keepdims=True)
    prefix = jnp.concatenate(pieces, axis=1)
    rows = [jnp.where(ok, idx, -1) for (_, idx, ok) in picks]
    rows += [jnp.sum(jnp.where(hit, prefix, 0.0), axis=0, keepdims=True).astype(jnp.int32) for (hit, _, _) in picks]
    rows += [jnp.zeros((1, ts), jnp.int32)] * (8 - 2 * MOBA_TOPK)
    sel_ref[0, 0] = jnp.concatenate(rows, axis=0)
    cnt_ref[...] = base
    cnt_out_ref[0, 0] = base


def _moba_select(q2, kv, bsz, seq_len, ts=1024):
    ts = min(ts, seq_len)
    heads = q2.shape[1] // MOBA_HEAD_DIM
    nt = seq_len // ts
    tri = (jnp.arange(LANES)[:, None] < jnp.arange(LANES)[None, :]).astype(BF16)
    return pl.pallas_call(
        _moba_select_kernel,
        grid=(bsz, heads, nt),
        in_specs=[
            pl.BlockSpec((ts, LANES), lambda b, h, t: (b * nt + t, h // 2)),
            pl.BlockSpec((seq_len, LANES), lambda b, h, t: (b, 2 * (h // 2))),
            pl.BlockSpec((LANES, LANES), lambda b, h, t: (0, 0)),
        ],
        out_specs=[pl.BlockSpec((1, 1, 8, ts), lambda b, h, t: (b, h, 0, t)),
                   pl.BlockSpec((1, 1, MOBA_MAX_BLOCKS, LANES), lambda b, h, t: (b, h, 0, 0))],
        out_shape=[jax.ShapeDtypeStruct((bsz, heads, 8, seq_len), jnp.int32),
                   jax.ShapeDtypeStruct((bsz, heads, MOBA_MAX_BLOCKS, LANES), F32)],
        scratch_shapes=[pltpu.VMEM((MOBA_MAX_BLOCKS, LANES), BF16), pltpu.VMEM((MOBA_MAX_BLOCKS, LANES), F32)],
        compiler_params=_params("parallel", "arbitrary", "arbitrary"),
        name="moba_select",
    )(q2, kv, tri)


MOBA_DIAG_BLOCKS = 4


def _moba_diag_kernel(q_ref, kv_ref, o_ref, lse_ref):
    lane = lax.broadcasted_iota(jnp.int32, (MOBA_BLOCK, LANES), 1)
    is_a = lane < MOBA_HEAD_DIM
    r = lax.broadcasted_iota(jnp.int32, (MOBA_BLOCK, MOBA_BLOCK), 0)
    c = lax.broadcasted_iota(jnp.int32, (MOBA_BLOCK, MOBA_BLOCK), 1)
    for u in range(MOBA_DIAG_BLOCKS):
        rows = pl.ds(u * MOBA_BLOCK, MOBA_BLOCK)
        q = q_ref[rows, :]
        kb = kv_ref[rows, :LANES]
        vb = kv_ref[rows, LANES:]
        pvs, ms = [], []
        for own in (is_a, jnp.logical_not(is_a)):
            s = jnp.where(c <= r, _dot_nt(jnp.where(own, q, jnp.zeros_like(q)), kb), NEG_BIG)
            m = jnp.max(s, axis=1, keepdims=True)
            p = jnp.exp(s - m).astype(BF16)
            pvs.append(_dot(p, jnp.where(own, vb, 1.0)))
            ms.append(m)
        num = jnp.where(is_a, pvs[0], pvs[1])
        den = pltpu.roll(jnp.where(is_a, pvs[1], pvs[0]), MOBA_HEAD_DIM, 1)
        o_ref[rows, :] = num / den
        lse_ref[rows, :] = jnp.where(is_a, ms[0], ms[1]) + jnp.log(den)


def _moba_diag(q2, kv, bsz, seq_len):
    npair = q2.shape[1] // LANES
    rows = MOBA_BLOCK * MOBA_DIAG_BLOCKS
    nb = seq_len // rows
    spec = pl.BlockSpec((rows, LANES), lambda b, p, i: (b * nb + i, p))
    out = jax.ShapeDtypeStruct(q2.shape, F32)
    return pl.pallas_call(
        _moba_diag_kernel,
        grid=(bsz, npair, nb),
        in_specs=[spec, pl.BlockSpec((rows, 2 * LANES), lambda b, p, i: (b * nb + i, p))],
        out_specs=[spec, spec],
        out_shape=[out, out],
        compiler_params=_params("parallel", "parallel", "parallel"),
        name="moba_diag",
    )(q2, kv)


MOBA_TILE = 256
MOBA_TILES_PER_STEP = 8


def _moba_grouped_kernel(tile_row_ref, tile_pair_ref, tile_half_ref, tile_real_ref, qd_ref, *refs):
    n = MOBA_TILES_PER_STEP
    kv_refs, o_ref = refs[:n], refs[n + 1]
    i = pl.program_id(0)
    lane = lax.broadcasted_iota(jnp.int32, (MOBA_TILE, LANES), 1)
    kv_lane = lax.broadcasted_iota(jnp.int32, (MOBA_BLOCK, LANES), 1)

    @pl.when(tile_real_ref[i * n] > 0)
    def _():
        for u in range(n):
            tile = i * n + u
            rows = pl.ds(u * MOBA_TILE, MOBA_TILE)
            own = (lane // MOBA_HEAD_DIM) == tile_half_ref[tile]
            s = _dot_nt(qd_ref[rows, :].astype(BF16), kv_refs[u][:, :LANES])
            m = jnp.max(s, axis=1, keepdims=True)
            p = jnp.exp(s - m).astype(BF16)
            vb = jnp.where((kv_lane // MOBA_HEAD_DIM) == tile_half_ref[tile], kv_refs[u][:, LANES:], 1.0)
            pv = _dot(p, vb)
            row_sum = pltpu.roll(pv, MOBA_HEAD_DIM, 1)
            part = jnp.where(own, pv / row_sum, m + jnp.log(pv))
            o_ref[rows, :] = jnp.where(tile_real_ref[tile] > 0, part, NEG_BIG)

    @pl.when(tile_real_ref[i * n] == 0)
    def _():
        o_ref[...] = jnp.full(o_ref.shape, NEG_BIG, F32)


def _moba_grouped(qd, kv, tile_row, tile_pair, tile_half, tile_real, run_after):
    n = MOBA_TILES_PER_STEP
    n_tiles = qd.shape[0] // MOBA_TILE

    def kv_spec(u):
        return pl.BlockSpec((MOBA_BLOCK, 2 * LANES), lambda i, tr, tp, th, tl: (tr[i * n + u], tp[i * n + u]))

    grid_spec = pltpu.PrefetchScalarGridSpec(
        num_scalar_prefetch=4,
        grid=(n_tiles // n,),
        in_specs=[pl.BlockSpec((n * MOBA_TILE, LANES), lambda i, tr, tp, th, tl: (i, 0))]
        + [kv_spec(u) for u in range(n)]
        + [pl.BlockSpec((8, LANES), lambda i, tr, tp, th, tl: (0, 0))],
        out_specs=pl.BlockSpec((n * MOBA_TILE, LANES), lambda i, tr, tp, th, tl: (i, 0)),
    )
    return pl.pallas_call(
        _moba_grouped_kernel,
        grid_spec=grid_spec,
        out_shape=jax.ShapeDtypeStruct(qd.shape, F32),
        compiler_params=_params("arbitrary"),
        name="moba_grouped",
    )(tile_row, tile_pair, tile_half, tile_real, qd, *([kv] * n), run_after)


def _moba_combine_kernel(od_ref, lsed_ref, *refs):
    g_refs, o_ref = refs[:-1], refs[-1]
    lane = lax.broadcasted_iota(jnp.int32, od_ref.shape, 1)
    is_a = lane < MOBA_HEAD_DIM
    parts = [(od_ref[...], lsed_ref[...])]
    for s in range(MOBA_TOPK):
        xa = g_refs[s][0, 0, 0]
        xb = g_refs[MOBA_TOPK + s][0, 0, 0]
        o = jnp.where(is_a, xa, xb)
        lse = pltpu.roll(jnp.where(is_a, xb, xa), MOBA_HEAD_DIM, 1)
        parts.append((o, lse))
    m = parts[0][1]
    for _, lse in parts[1:]:
        m = jnp.maximum(m, lse)
    num = jnp.zeros(od_ref.shape, F32)
    den = jnp.zeros(od_ref.shape, F32)
    for o, lse in parts:
        w = jnp.exp(lse - m)
        num = num + w * o
        den = den + w
    o_ref[...] = (num / den).astype(BF16)


def _moba_combine(od, lsed, gath, bsz, seq_len, tm=1024):
    tm = min(tm, seq_len)
    npair = od.shape[1] // LANES
    nt = seq_len // tm
    spec = pl.BlockSpec((tm, LANES), lambda b, p, i: (b * nt + i, p))

    def g_spec(e, s):
        return pl.BlockSpec((1, 1, 1, tm, LANES), lambda b, p, i: (b, 2 * p + e, s, i, 0))

    return pl.pallas_call(
        _moba_combine_kernel,
        grid=(bsz, npair, nt),
        in_specs=[spec, spec] + [g_spec(e, s) for e in range(2) for s in range(MOBA_TOPK)],
        out_specs=spec,
        out_shape=jax.ShapeDtypeStruct(od.shape, BF16),
        compiler_params=_params("parallel", "parallel", "parallel"),
        name="moba_combine",
    )(od, lsed, *([gath] * (2 * MOBA_TOPK)))


SC_WINDOW = 128
SC_CORES = 2
SC_SUBCORES = 16


def _sc_mesh():
    return plsc.VectorSubcoreMesh(core_axis_name="core", subcore_axis_name="subcore")


def _sc_gather_rows(table, idx):
    n = idx.shape[0]
    d = table.shape[1]
    window = SC_WINDOW
    assert n % (window * SC_CORES * SC_SUBCORES) == 0
    per_core = n // window // SC_CORES

    @functools.partial(pl.kernel, out_type=jax.ShapeDtypeStruct((n, d), table.dtype), mesh=_sc_mesh())
    def gather_kernel(x_hbm, i_hbm, o_hbm):
        base = lax.axis_index("core") * per_core

        def body(i_vmem, o_vmem):
            pltpu.sync_copy(x_hbm.at[i_vmem.at[0]], o_vmem)

        pltpu.emit_pipeline(
            body,
            grid=(per_core,),
            in_specs=[pl.BlockSpec((1, window), index_map=lambda i: (0, base + i))],
            out_specs=[pl.BlockSpec((window, d), index_map=lambda i: (base + i, 0))],
            core_axis_name="subcore",
            dimension_semantics=(pltpu.PARALLEL,),
            trace_scopes=False,
        )(i_hbm, o_hbm)

    return gather_kernel(table, idx.reshape(1, n))


def _sc_scatter_rows(rows, dest, n_out, repeat):
    n_src, d = rows.shape
    n = dest.shape[0]
    assert n == repeat * n_src
    window = SC_WINDOW
    assert n % (window * SC_CORES * SC_SUBCORES) == 0
    per_core = n // window // SC_CORES
    src_windows = n_src // window

    @functools.partial(pl.kernel, out_type=jax.ShapeDtypeStruct((n_out, d), rows.dtype), mesh=_sc_mesh())
    def scatter_kernel(x_hbm, i_hbm, o_hbm):
        base = lax.axis_index("core") * per_core

        def body(x_vmem, i_vmem):
            pltpu.sync_copy(x_vmem, o_hbm.at[i_vmem.at[0]])

        pltpu.emit_pipeline(
            body,
            grid=(per_core,),
            in_specs=[pl.BlockSpec((window, d), index_map=lambda i: ((base + i) % src_windows, 0)),
                      pl.BlockSpec((1, window), index_map=lambda i: (0, base + i))],
            out_specs=[],
            core_axis_name="subcore",
            dimension_semantics=(pltpu.PARALLEL,),
            trace_scopes=False,
        )(x_hbm, i_hbm)

    return scatter_kernel(rows, dest.reshape(1, n))


def _moba_dispatch(q2, qh, kv, bsz, seq_len):
    heads = q2.shape[1] // MOBA_HEAD_DIM
    nbk = MOBA_MAX_BLOCKS
    sel, cnt = _moba_select(q2, kv, bsz, seq_len)
    counts = cnt[..., 0].astype(jnp.int32).reshape(bsz * heads * nbk)
    pcounts = ((counts + MOBA_TILE - 1) // MOBA_TILE) * MOBA_TILE
    pends = jnp.cumsum(pcounts)
    pstarts = (pends - pcounts).reshape(bsz, heads, 1, 1, nbk)
    n_items = bsz * heads * seq_len * MOBA_TOPK
    step_rows = MOBA_TILE * MOBA_TILES_PER_STEP
    n_rows = -(-(n_items + bsz * heads * nbk * MOBA_TILE) // step_rows) * step_rows
    n_null = step_rows
    idx = sel[:, :, 0:MOBA_TOPK, :]
    rank = sel[:, :, MOBA_TOPK:2 * MOBA_TOPK, :]
    start = jnp.sum(jnp.where(idx[..., None] == jnp.arange(nbk), pstarts, 0), axis=-1)
    null_row = n_rows + jnp.arange(seq_len, dtype=jnp.int32) % n_null
    dest = jnp.where(idx >= 0, start + rank, null_row)
    n_tiles = (n_rows + n_null) // MOBA_TILE
    n_groups = bsz * heads * nbk
    tile_start = jnp.arange(n_tiles, dtype=jnp.int32) * MOBA_TILE
    tile_g = jnp.minimum(jnp.sum((pends[None, :] <= tile_start[:, None]).astype(jnp.int32), axis=1), n_groups - 1)
    tile_real = (tile_start < pends[-1]).astype(jnp.int32)
    tile_head = (tile_g // nbk) % heads
    tile_row = (tile_g // (heads * nbk)) * (seq_len // MOBA_BLOCK) + jnp.minimum(tile_g % nbk, seq_len // MOBA_BLOCK - 1)
    dest_by_slot = dest.transpose(2, 1, 0, 3).reshape(-1)
    qd = _sc_scatter_rows(qh.reshape(-1, LANES), dest_by_slot, n_rows + n_null, MOBA_TOPK)
    return qd, dest, (tile_row, tile_head // 2, tile_head % 2, tile_real)


def _moba_finish(qd, dest, tiles, od, lsed, kv, bsz, seq_len):
    heads = dest.shape[1]
    part = _moba_grouped(qd, kv, *tiles, lsed)
    gath = _sc_gather_rows(part, dest.reshape(-1)).reshape(bsz, heads, MOBA_TOPK, seq_len, LANES)
    return _moba_combine(od, lsed, gath, bsz, seq_len)


def _mem_kv_kernel(mem_ref, gmem_ref, w_ref, gck_ref, k_ref, v_ref):
    xf = mem_ref[0]
    ms = jnp.mean(xf * xf, axis=-1, keepdims=True)
    h = (xf * lax.rsqrt(ms + EPS) * gmem_ref[...]).astype(BF16)
    kv = _dot(h, w_ref[...])
    w = k_ref.shape[2]
    hd = w // X_HEADS
    for c in range(X_HEADS):
        chunk = kv[:, c * hd:(c + 1) * hd]
        cms = jnp.mean(chunk * chunk, axis=-1, keepdims=True)
        k_ref[0, :, c * hd:(c + 1) * hd] = (chunk * lax.rsqrt(cms + EPS) * gck_ref[...]).astype(BF16)
    v_ref[0] = kv[:, w:].astype(BF16)


def _mem_kv(mem, g_mem, w_kv_mem, g_ck):
    bsz, m, d = mem.shape
    w = w_kv_mem.shape[1] // 2
    const = lambda b: (0, 0)
    out = jax.ShapeDtypeStruct((bsz, m, w), BF16)
    return pl.pallas_call(
        _mem_kv_kernel,
        grid=(bsz,),
        in_specs=[pl.BlockSpec((1, m, d), lambda b: (b, 0, 0)), pl.BlockSpec((1, d), const),
                  pl.BlockSpec((d, 2 * w), const), pl.BlockSpec((1, w // X_HEADS), const)],
        out_specs=[pl.BlockSpec((1, m, w), lambda b: (b, 0, 0))] * 2,
        out_shape=[out, out],
        compiler_params=_params("parallel"),
        name="mem_kv",
    )(mem, g_mem.reshape(1, d), w_kv_mem.astype(BF16), g_ck.reshape(1, -1))


def _merge_kernel(x_ref, ys_ref, u_ref, dskip_ref, om_ref, xq_ref, kc_ref, vc_ref, g_ref,
                  wglu_ref, wmo_ref, wco_ref, wout_ref, gffn_ref, wr_ref, br_ref,
                  x1_ref, h2_ref, logit_ref):
    d = x_ref.shape[1]
    y = ys_ref[...].astype(F32) + dskip_ref[...] * u_ref[...].astype(F32)
    ge = 0.5 * y * (1.0 + jnp.tanh(math.sqrt(2.0 / math.pi) * (y + 0.044715 * (y * y * y))))
    z = _dot(ge.astype(BF16), wglu_ref[...])
    merged = g_ref[:, 0:d].astype(F32) * (z[:, :d] * _sigmoid(z[:, d:]))
    merged = merged + g_ref[:, d:2 * d].astype(F32) * _dot(om_ref[...], wmo_ref[...])
    w = xq_ref.shape[1]
    hd = w // X_HEADS
    heads = []
    for c in range(X_HEADS):
        s = _dot_nt(xq_ref[:, c * hd:(c + 1) * hd], kc_ref[0, :, c * hd:(c + 1) * hd]) * (hd ** -0.5)
        p = jnp.exp(s - jnp.max(s, axis=1, keepdims=True))
        p = p / jnp.sum(p, axis=1, keepdims=True)
        heads.append(_dot(p.astype(BF16), vc_ref[0, :, c * hd:(c + 1) * hd]))
    oc = jnp.concatenate(heads, axis=1).astype(BF16)
    merged = merged + g_ref[:, 2 * d:3 * d].astype(F32) * _dot(oc, wco_ref[...])
    x1 = x_ref[...] + _dot(merged.astype(BF16), wout_ref[...])
    x1_ref[...] = x1
    ms = jnp.mean(x1 * x1, axis=-1, keepdims=True)
    h2 = (x1 * lax.rsqrt(ms + EPS) * gffn_ref[...]).astype(BF16)
    packed = _pack_bf16_pairs(h2.astype(F32))
    for j in range(h2_ref.shape[0]):
        h2_ref[j] = packed[:, j * LANES:(j + 1) * LANES]
    logit_ref[...] = _dot_nt(wr_ref[...], h2) + br_ref[...]


def _merge(xt, ys, u, d_skip, om, xqn, kc, vc, gates, w_glu, w_mo, w_co, w_out, g_ffn, w_router, b_router,
           seq_len, tm=512):
    t, d = xt.shape
    tm = min(tm, seq_len)
    w = d // 2
    m = kc.shape[1]
    ne = w_router.shape[1]
    nt = seq_len // tm
    row = lambda i: (i, 0)
    const = lambda i: (0, 0)
    per_b = lambda i: (i // nt, 0, 0)
    once = dict(pipeline_mode=pl.Buffered(1))
    return pl.pallas_call(
        _merge_kernel,
        grid=(t // tm,),
        in_specs=[
            pl.BlockSpec((tm, d), row), pl.BlockSpec((tm, w), row), pl.BlockSpec((tm, w), row),
            pl.BlockSpec((1, w), const), pl.BlockSpec((tm, w), row), pl.BlockSpec((tm, w), row),
            pl.BlockSpec((1, m, w), per_b), pl.BlockSpec((1, m, w), per_b),
            pl.BlockSpec((tm, N_BRANCH * d), row),
            pl.BlockSpec((w, 2 * d), const, **once), pl.BlockSpec((w, d), const, **once),
            pl.BlockSpec((w, d), const, **once), pl.BlockSpec((d, d), const, **once), pl.BlockSpec((1, d), const),
            pl.BlockSpec((ne, d), const), pl.BlockSpec((ne, 1), const),
        ],
        out_specs=[pl.BlockSpec((tm, d), row), pl.BlockSpec((d // 2 // LANES, tm, LANES), lambda i: (0, i, 0)),
                   pl.BlockSpec((ne, tm), lambda i: (0, i))],
        out_shape=[jax.ShapeDtypeStruct((t, d), F32), jax.ShapeDtypeStruct((d // 2 // LANES, t, LANES), jnp.uint32),
                   jax.ShapeDtypeStruct((ne, t), F32)],
        compiler_params=_params("parallel"),
        name="merge",
    )(xt, ys, u, d_skip.reshape(1, w), om, xqn, kc, vc, gates,
      w_glu.astype(BF16), w_mo.astype(BF16), w_co.astype(BF16), w_out.astype(BF16),
      g_ffn.reshape(1, d), w_router.T.astype(BF16), b_router.reshape(ne, 1))


def _cast_weights_kernel(wgu_ref, wd_ref, after_ref, wgu_out, wd_out):
    wgu_out[...] = wgu_ref[...].astype(BF16)
    wd_out[...] = wd_ref[...].astype(BF16)


def _cast_expert_weights(w_gu, w_down, run_after):
    ne = w_gu.shape[0]
    spec = lambda a: pl.BlockSpec((1,) + a.shape[1:], lambda e: (e, 0, 0))
    return pl.pallas_call(
        _cast_weights_kernel,
        grid=(ne,),
        in_specs=[spec(w_gu), spec(w_down), pl.BlockSpec((8, LANES), lambda e: (0, 0))],
        out_specs=[spec(w_gu), spec(w_down)],
        out_shape=[jax.ShapeDtypeStruct(w_gu.shape, BF16), jax.ShapeDtypeStruct(w_down.shape, BF16)],
        compiler_params=_params("parallel"),
        name="moe_weight_cast",
    )(w_gu, w_down, run_after)


def _moe_kernel(blk_e_ref, blk_used_ref, xs_ref, wgu_ref, bgu_ref, wd_ref, bd_ref, y_ref):
    i = pl.program_id(0)

    @pl.when(blk_used_ref[i] > 0)
    def _():
        de = wd_ref.shape[1]
        words = jnp.concatenate([xs_ref[j] for j in range(xs_ref.shape[0])], axis=1)
        xs = jnp.concatenate(_unpack_bf16_pairs(words), axis=1).astype(BF16)
        gu = _dot(xs, wgu_ref[0]) + bgu_ref[0]
        gate = jnp.minimum(gu[:, :de], SWIGLU_LIMIT)
        up = jnp.clip(gu[:, de:], -SWIGLU_LIMIT, SWIGLU_LIMIT)
        act = gate * _sigmoid(SWIGLU_ALPHA * gate) * (up + 1.0)
        y = _dot(act.astype(BF16), wd_ref[0]) + bd_ref[0]
        packed = _pack_bf16_pairs(y.astype(BF16).astype(F32))
        for j in range(y_ref.shape[0]):
            y_ref[j] = packed[:, j * LANES:(j + 1) * LANES]

    @pl.when(blk_used_ref[i] == 0)
    def _():
        y_ref[...] = jnp.zeros(y_ref.shape, y_ref.dtype)


def _moe_experts(xs, blk_e, blk_used, w_gu, b_gu, w_down, b_down):
    slabs, p, _ = xs.shape
    d = 2 * slabs * LANES
    ne, _, de2 = w_gu.shape
    de = de2 // 2
    nblk = p // EXPERT_ROWS
    row_spec = pl.BlockSpec((slabs, EXPERT_ROWS, LANES), lambda i, e, n: (0, i, 0))
    grid_spec = pltpu.PrefetchScalarGridSpec(
        num_scalar_prefetch=2,
        grid=(nblk,),
        in_specs=[
            row_spec,
            pl.BlockSpec((1, d, de2), lambda i, e, n: (e[i], 0, 0)),
            pl.BlockSpec((1, 1, de2), lambda i, e, n: (e[i], 0, 0)),
            pl.BlockSpec((1, de, d), lambda i, e, n: (e[i], 0, 0)),
            pl.BlockSpec((1, 1, d), lambda i, e, n: (e[i], 0, 0)),
        ],
        out_specs=row_spec,
    )
    return pl.pallas_call(
        _moe_kernel,
        grid_spec=grid_spec,
        out_shape=jax.ShapeDtypeStruct(xs.shape, jnp.uint32),
        compiler_params=_params("arbitrary"),
        name="moe_experts",
    )(blk_e, blk_used, xs, w_gu, b_gu.reshape(ne, 1, de2), w_down, b_down.reshape(ne, 1, d))


def _router_kernel(lg_ref, tri_ref, e_ref, w_ref, r_ref, cnt_out_ref, cnt_ref):
    t = pl.program_id(0)
    ne, ts = lg_ref.shape

    @pl.when(t == 0)
    def _():
        cnt_ref[...] = jnp.zeros(cnt_ref.shape, F32)

    g = lg_ref[...]
    eid = lax.broadcasted_iota(jnp.int32, g.shape, 0)
    selected = jnp.zeros(g.shape, jnp.bool_)
    picks = []
    for _ in range(TOPK_EXPERTS):
        mx = jnp.max(g, axis=0, keepdims=True)
        idx = jnp.min(jnp.where(g == mx, eid, ne), axis=0, keepdims=True)
        hit = eid == idx
        picks.append((hit, idx, mx))
        selected = selected | hit
        g = jnp.where(hit, -jnp.inf, g)
    exps = [jnp.exp(mx - picks[0][2]) for (_, _, mx) in picks]
    total = sum(exps)
    sel01 = jnp.where(selected, 1.0, 0.0).astype(BF16)
    prefix = _dot(sel01, tri_ref[...]) + jnp.tile(cnt_ref[...], (1, ts // LANES))
    pad_i = [jnp.zeros((1, ts), jnp.int32)] * (8 - TOPK_EXPERTS)
    e_ref[...] = jnp.concatenate([idx for (_, idx, _) in picks] + pad_i, axis=0)
    w_ref[...] = jnp.concatenate([e / total for e in exps] + [jnp.zeros((1, ts), F32)] * (8 - TOPK_EXPERTS), axis=0)
    r_ref[...] = jnp.concatenate(
        [jnp.sum(jnp.where(hit, prefix, 0.0), axis=0, keepdims=True).astype(jnp.int32) for (hit, _, _) in picks]
        + pad_i, axis=0)
    cnt_ref[...] = cnt_ref[...] + _dot(sel01, jnp.ones((ts, LANES), BF16))
    cnt_out_ref[...] = cnt_ref[...]


def _router(logits_t, ts=1024):
    ne, t = logits_t.shape
    ts = min(ts, t)
    tri = (jnp.arange(ts)[:, None] < jnp.arange(ts)[None, :]).astype(BF16)
    rows = pl.BlockSpec((8, ts), lambda i: (0, i))
    return pl.pallas_call(
        _router_kernel,
        grid=(t // ts,),
        in_specs=[pl.BlockSpec((ne, ts), lambda i: (0, i)), pl.BlockSpec((ts, ts), lambda i: (0, 0))],
        out_specs=[rows, rows, rows, pl.BlockSpec((ne, LANES), lambda i: (0, 0))],
        out_shape=[jax.ShapeDtypeStruct((8, t), jnp.int32), jax.ShapeDtypeStruct((8, t), F32),
                   jax.ShapeDtypeStruct((8, t), jnp.int32), jax.ShapeDtypeStruct((ne, LANES), F32)],
        scratch_shapes=[pltpu.VMEM((ne, LANES), F32)],
        compiler_params=_params("arbitrary"),
        name="moe_router",
    )(logits_t, tri)


def _moe_mix_kernel(x1_ref, w_ref, pk_ref, o_ref):
    slabs = pk_ref.shape[0]
    wts = w_ref[...]
    lo = [jnp.zeros((x1_ref.shape[0], LANES), F32)] * slabs
    hi = [jnp.zeros((x1_ref.shape[0], LANES), F32)] * slabs
    for k in range(TOPK_EXPERTS):
        wk = wts[:, k:k + 1]
        for j in range(slabs):
            a, b = _unpack_bf16_pairs(pk_ref[j, k])
            lo[j] = lo[j] + wk * a
            hi[j] = hi[j] + wk * b
    o_ref[...] = x1_ref[...] + jnp.concatenate(lo + hi, axis=1)


def _moe_mix(x1, weights_tk, picked, tm=512):
    t, d = x1.shape
    slabs = picked.shape[0]
    return pl.pallas_call(
        _moe_mix_kernel,
        grid=(t // tm,),
        in_specs=[pl.BlockSpec((tm, d), lambda i: (i, 0)),
                  pl.BlockSpec((tm, TOPK_EXPERTS), lambda i: (i, 0)),
                  pl.BlockSpec((slabs, TOPK_EXPERTS, tm, LANES), lambda i: (0, 0, i, 0))],
        out_specs=pl.BlockSpec((tm, d), lambda i: (i, 0)),
        out_shape=jax.ShapeDtypeStruct((t, d), F32),
        compiler_params=_params("parallel"),
        name="moe_mix",
    )(x1, weights_tk, picked)


def _moe_ffn(x1, h2, logits_t, w_gu, b_gu, w_down, b_down):
    t, d = x1.shape
    tk = t * TOPK_EXPERTS
    e8, w8, r8, cnt = _router(logits_t)
    top_e, weights, rank = e8[:TOPK_EXPERTS], w8[:TOPK_EXPERTS], r8[:TOPK_EXPERTS]
    counts = cnt[:, 0].astype(jnp.int32)
    pcounts = ((counts + EXPERT_ROWS - 1) // EXPERT_ROWS) * EXPERT_ROWS
    pends = jnp.cumsum(pcounts)
    pstarts = pends - pcounts
    is_e = top_e[..., None] == jnp.arange(N_EXPERTS, dtype=jnp.int32)
    dest_by_k = jnp.sum(jnp.where(is_e, pstarts, 0), axis=-1) + rank
    nblk = -(-tk // EXPERT_ROWS) + N_EXPERTS
    blk_start = jnp.arange(nblk, dtype=jnp.int32) * EXPERT_ROWS
    blk_e = jnp.minimum(jnp.sum((pends[None, :] <= blk_start[:, None]).astype(jnp.int32), axis=1), N_EXPERTS - 1)
    blk_used = (blk_start < pends[-1]).astype(jnp.int32)
    slabs = d // 2 // LANES
    p = nblk * EXPERT_ROWS
    slab_off = jnp.arange(slabs, dtype=jnp.int32) * p
    dest_kst = (dest_by_k[:, None, :] + slab_off[None, :, None]).reshape(-1)
    dest_skt = (dest_by_k[None, :, :] + slab_off[:, None, None]).reshape(-1)
    xs = _sc_scatter_rows(h2.reshape(slabs * t, LANES), dest_kst, slabs * p, TOPK_EXPERTS)
    w_gu_bf, w_down_bf = _cast_expert_weights(w_gu, w_down, cnt)
    ys = _moe_experts(xs.reshape(slabs, p, LANES), blk_e, blk_used, w_gu_bf, b_gu, w_down_bf, b_down)
    picked = _sc_gather_rows(ys.reshape(slabs * p, LANES), dest_skt).reshape(slabs, TOPK_EXPERTS, t, LANES)
    return _moe_mix(x1, weights.T, picked)


def kernel(x, mem, g_mix, w_in, lam_re, lam_im, log_dt, b_re, b_im, c_re, c_im, d_skip, w_glu, g_q, g_k, w_moba_out, g_mem, w_kv_mem, g_cq, g_ck, w_cross_out, w_out, g_ffn, w_router, b_router, w_gu, b_gu, w_down, b_down):
    bsz, seq_len, d = x.shape
    xt = x.reshape(bsz * seq_len, d)
    for l in range(g_mix.shape[0]):
        u, q2, kv, xqn, gates, qh = _in_proj(xt, g_mix[l], w_in[l], g_q[l], g_k[l], g_cq[l], seq_len)
        qd, dest, tiles = _moba_dispatch(q2, qh, kv, bsz, seq_len)
        mats = _s5_matrices(lam_re[l], lam_im[l], log_dt[l], b_re[l], b_im[l], c_re[l], c_im[l],
                            seq_len // S5_CHUNK)
        ys = _s5(u, mats, bsz, seq_len)
        od, lsed = _moba_diag(q2, kv, bsz, seq_len)
        kc, vc = _mem_kv(mem, g_mem[l], w_kv_mem[l], g_ck[l])
        om = _moba_finish(qd, dest, tiles, od, lsed, kv, bsz, seq_len)
        x1, h2, logits = _merge(xt, ys, u, d_skip[l], om, xqn, kc, vc, gates, w_glu[l], w_moba_out[l],
                                w_cross_out[l], w_out[l], g_ffn[l], w_router[l], b_router[l], seq_len)
        xt = _moe_ffn(x1, h2, logits, w_gu[l], b_gu[l], w_down[l], b_down[l])
    return xt.reshape(bsz, seq_len, d)
```

```python
import functools
import math

import jax
import jax.numpy as jnp
from jax import lax
from jax.experimental import pallas as pl
from jax.experimental.pallas import tpu as pltpu
from jax.experimental.pallas import tpu_sc as plsc

F32 = jnp.float32
BF16 = jnp.bfloat16

EPS = 1e-6
N_BRANCH = 3
SSM_GROUP = 16
SSM_STATE = 64
S5_CHUNK = 16
MOBA_HEAD_DIM = 64
MOBA_BLOCK = 256
MOBA_TOPK = 3
MOBA_MAX_BLOCKS = 64
ROPE_THETA = 10000.0
X_HEADS = 4
N_EXPERTS = 32
TOPK_EXPERTS = 4
SWIGLU_LIMIT = 7.0
SWIGLU_ALPHA = 1.702
EXPERT_ROWS = 512
NEG_BIG = -1e30
LANES = 128
VMEM_LIMIT_BYTES = 56 * 1024 * 1024


def _params(*sem):
    return pltpu.CompilerParams(dimension_semantics=sem, vmem_limit_bytes=VMEM_LIMIT_BYTES)


def _sigmoid(x):
    return 1.0 / (1.0 + jnp.exp(-x))


def _dot(a, b):
    return jnp.dot(a, b, preferred_element_type=F32)


def _pack_bf16_pairs(x):
    n = x.shape[1] // 2
    lo = lax.bitcast_convert_type(x[:, :n], jnp.uint32) >> 16
    hi = lax.bitcast_convert_type(x[:, n:], jnp.uint32) & jnp.uint32(0xFFFF0000)
    return lo | hi


def _unpack_bf16_pairs(w):
    lo = lax.bitcast_convert_type(w << 16, F32)
    hi = lax.bitcast_convert_type(w & jnp.uint32(0xFFFF0000), F32)
    return lo, hi


def _dot_nt(a, b):
    return lax.dot_general(a, b, (((1,), (1,)), ((), ())), preferred_element_type=F32)


def _inproj_kernel(x_ref, gmix_ref, wa_ref, wg_ref, e64_ref, gq_ref, gk_ref, gcq_ref, cos_ref, sin_ref,
                   u_ref, q_ref, kv_ref, xq_ref, g_ref, qh_ref):
    xf = x_ref[...]
    ms = jnp.mean(xf * xf, axis=-1, keepdims=True)
    h = (xf * lax.rsqrt(ms + EPS) * gmix_ref[...]).astype(BF16)
    w = u_ref.shape[1]
    proj = lambda c: _dot(h, wa_ref[:, c * w:(c + 1) * w])
    u_ref[...] = proj(0)

    cos = jnp.tile(cos_ref[...], (1, w // LANES))
    sin = jnp.tile(sin_ref[...], (1, w // LANES))
    lane = lax.broadcasted_iota(jnp.int32, (xf.shape[0], w), 1)
    first_half = (lane % MOBA_HEAD_DIM) < (MOBA_HEAD_DIM // 2)

    def qk_norm_rope(raw, g):
        ss = _dot((raw * raw).astype(BF16), e64_ref[...])
        n = raw * lax.rsqrt(ss * (1.0 / MOBA_HEAD_DIM) + EPS) * g
        rot = jnp.where(first_half,
                        pltpu.roll(n, w - MOBA_HEAD_DIM // 2, 1),
                        pltpu.roll(n, MOBA_HEAD_DIM // 2, 1))
        return n * cos + rot * sin

    q = qk_norm_rope(proj(1), gq_ref[...]) * (MOBA_HEAD_DIM ** -0.5)
    q_ref[...] = q.astype(BF16)
    in_a = lax.broadcasted_iota(jnp.int32, (xf.shape[0], LANES), 1) < MOBA_HEAD_DIM
    for p in range(w // LANES):
        pair = q[:, p * LANES:(p + 1) * LANES].astype(BF16).astype(F32)
        qh_ref[2 * p] = jnp.where(in_a, pair, 0.0)
        qh_ref[2 * p + 1] = jnp.where(in_a, 0.0, pair)
    k = qk_norm_rope(proj(2), gk_ref[...]).astype(BF16)
    v = proj(3).astype(BF16)
    for p in range(w // LANES):
        kv_ref[:, (2 * p) * LANES:(2 * p + 1) * LANES] = k[:, p * LANES:(p + 1) * LANES]
        kv_ref[:, (2 * p + 1) * LANES:(2 * p + 2) * LANES] = v[:, p * LANES:(p + 1) * LANES]

    xq = proj(4)
    hd = w // X_HEADS
    for c in range(X_HEADS):
        chunk = xq[:, c * hd:(c + 1) * hd]
        cms = jnp.mean(chunk * chunk, axis=-1, keepdims=True)
        xq_ref[:, c * hd:(c + 1) * hd] = (chunk * lax.rsqrt(cms + EPS) * gcq_ref[...]).astype(BF16)

    d = xf.shape[1]
    for c in range(N_BRANCH):
        z = _dot(h, wg_ref[:, c * d:(c + 1) * d])
        g_ref[:, c * d:(c + 1) * d] = _sigmoid(z).astype(BF16)


def _in_proj(xt, g_mix, w_in, g_q, g_k, g_cq, seq_len, tm=512):
    t, d = xt.shape
    tm = min(tm, seq_len)
    w = d // 2
    wa = w_in[:, :5 * w].astype(BF16)
    wg = w_in[:, 5 * w:].astype(BF16)
    heads = w // MOBA_HEAD_DIM
    e64 = jnp.kron(jnp.eye(heads, dtype=F32), jnp.ones((MOBA_HEAD_DIM, MOBA_HEAD_DIM), F32)).astype(BF16)
    half = MOBA_HEAD_DIM // 2
    inv = ROPE_THETA ** (-jnp.arange(half, dtype=F32) / half)
    ang = jnp.arange(seq_len, dtype=F32)[:, None] * inv[None, :]
    cos = jnp.tile(jnp.cos(ang), (1, LANES // half))
    sin = jnp.tile(jnp.concatenate([-jnp.sin(ang), jnp.sin(ang)], axis=1), (1, LANES // MOBA_HEAD_DIM))
    nt = seq_len // tm
    row = lambda i: (i, 0)
    const = lambda i: (0, 0)
    out_w = jax.ShapeDtypeStruct((t, w), BF16)
    return pl.pallas_call(
        _inproj_kernel,
        grid=(t // tm,),
        in_specs=[
            pl.BlockSpec((tm, d), row),
            pl.BlockSpec((1, d), const),
            pl.BlockSpec((d, 5 * w), const, pipeline_mode=pl.Buffered(1)),
            pl.BlockSpec((d, N_BRANCH * d), const, pipeline_mode=pl.Buffered(1)),
            pl.BlockSpec((w, w), const, pipeline_mode=pl.Buffered(1)),
            pl.BlockSpec((1, w), const),
            pl.BlockSpec((1, w), const),
            pl.BlockSpec((1, w // X_HEADS), const),
            pl.BlockSpec((tm, LANES), lambda i: (i % nt, 0)),
            pl.BlockSpec((tm, LANES), lambda i: (i % nt, 0)),
        ],
        out_specs=[pl.BlockSpec((tm, w), row), pl.BlockSpec((tm, w), row), pl.BlockSpec((tm, 2 * w), row),
                   pl.BlockSpec((tm, w), row), pl.BlockSpec((tm, N_BRANCH * d), row),
                   pl.BlockSpec((heads, tm, LANES), lambda i: (0, i, 0))],
        out_shape=[jax.ShapeDtypeStruct((t, w), F32), out_w, jax.ShapeDtypeStruct((t, 2 * w), BF16), out_w,
                   jax.ShapeDtypeStruct((t, N_BRANCH * d), BF16),
                   jax.ShapeDtypeStruct((heads, t, LANES), F32)],
        compiler_params=_params("parallel"),
        name="in_proj",
    )(xt, g_mix.reshape(1, d), wa, wg, e64,
      jnp.tile(g_q, heads).reshape(1, w), jnp.tile(g_k, heads).reshape(1, w), g_cq.reshape(1, -1), cos, sin)


def _s5_matrices(lam_re, lam_im, log_dt, b_re, b_im, c_re, c_im, n_chunks):
    hp = lax.Precision.HIGHEST
    c = S5_CHUNK
    dt = jnp.exp(log_dt)[:, None]
    mag = jnp.exp(lam_re * dt)
    ar = mag * jnp.cos(lam_im * dt)
    ai = mag * jnp.sin(lam_im * dt)
    nr = ar - 1.0
    den = lam_re * lam_re + lam_im * lam_im
    cr = (nr * lam_re + ai * lam_im) / den
    ci = (ai * lam_re - nr * lam_im) / den
    bbr = cr[..., None] * b_re - ci[..., None] * b_im
    bbi = cr[..., None] * b_im + ci[..., None] * b_re

    def power(n):
        nf = n.astype(F32)[None, :, None]
        m = jnp.exp((lam_re * dt)[:, None, :] * nf)
        th = (lam_im * dt)[:, None, :] * nf
        return m * jnp.cos(th), m * jnp.sin(th)

    pr, pi = power(jnp.arange(c + 1))
    kbr = pr[..., None] * bbr[:, None] - pi[..., None] * bbi[:, None]
    kbi = pr[..., None] * bbi[:, None] + pi[..., None] * bbr[:, None]
    kk = jnp.einsum('ghp,gtpc->gthc', jnp.concatenate([c_re, -c_im], axis=-1),
                    jnp.concatenate([kbr, kbi], axis=2), precision=hp)
    tq = jnp.arange(c)
    g = kk.shape[0]
    rev = c - 1 - tq
    w_in = jnp.concatenate([kbr[:, rev].transpose(0, 1, 3, 2), kbi[:, rev].transpose(0, 1, 3, 2)], axis=-1)
    w_in = w_in.reshape(g, c * SSM_GROUP, 2 * SSM_STATE)
    prn, pin = pr[:, 1:], pi[:, 1:]
    wo_r = c_re[:, None] * prn[:, :, None, :] - c_im[:, None] * pin[:, :, None, :]
    wo_i = -c_re[:, None] * pin[:, :, None, :] - c_im[:, None] * prn[:, :, None, :]
    w_out = jnp.concatenate([wo_r, wo_i], axis=-1).transpose(0, 3, 1, 2).reshape(g, 2 * SSM_STATE, c * SSM_GROUP)
    n_steps = max(1, int(math.ceil(math.log2(n_chunks))))
    qr, qi = power(c * (2 ** jnp.arange(n_steps)))
    pa = jnp.concatenate([qr, qr], axis=-1)
    pb = jnp.concatenate([-qi, qi], axis=-1)
    return kk.astype(BF16), w_in.astype(BF16), w_out.astype(BF16), pa, pb


S5_SET = LANES // SSM_GROUP
S5_ROWS = 512


def _s5_kernel(u_ref, toep_ref, winc_ref, woutc_ref, pa_ref, pb_ref, y_ref, s_ref, win_ref, wout_ref):
    c = S5_CHUNK
    nc = u_ref.shape[0] // c
    rb = min(S5_ROWS, nc)
    p2 = 2 * SSM_STATE

    @pl.when(pl.program_id(1) == 0)
    def _():
        r_group = lax.broadcasted_iota(jnp.int32, (LANES, p2), 0) // SSM_GROUP
        l_group = lax.broadcasted_iota(jnp.int32, (p2, LANES), 1) // SSM_GROUP
        for t in range(c):
            for g in range(S5_SET):
                win_ref[0, t * LANES:(t + 1) * LANES, g * p2:(g + 1) * p2] = jnp.where(
                    r_group == g, winc_ref[0, t], jnp.zeros((LANES, p2), BF16))
                wout_ref[0, g * p2:(g + 1) * p2, t * LANES:(t + 1) * LANES] = jnp.where(
                    l_group == g, woutc_ref[0, t], jnp.zeros((p2, LANES), BF16))

    def chunk_rows(r0):
        return jnp.concatenate([u_ref[pl.ds(r0 * c + t, rb, stride=c), :] for t in range(c)], axis=1).astype(BF16)

    for blk in range(nc // rb):
        s_ref[blk * rb:(blk + 1) * rb, :] = _dot(chunk_rows(blk * rb), win_ref[0])
    row = lax.broadcasted_iota(jnp.int32, (nc, LANES), 0)
    n_steps = pa_ref.shape[1]
    for g in range(S5_SET):
        lanes = slice(g * LANES, (g + 1) * LANES)
        s = s_ref[:, lanes]
        for k in range(n_steps):
            sh = 1 << k
            if sh >= nc:
                break
            prev = jnp.where(row >= sh, pltpu.roll(s, sh, 0), 0.0)
            s = s + pa_ref[0, k:k + 1, lanes] * prev + pb_ref[0, k:k + 1, lanes] * pltpu.roll(prev, SSM_STATE, 1)
        s_ref[:, lanes] = jnp.where(row >= 1, pltpu.roll(s, 1, 0), 0.0)
    for blk in range(nc // rb):
        rows = slice(blk * rb, (blk + 1) * rb)
        a = chunk_rows(blk * rb)
        carried = _dot(s_ref[rows, :].astype(BF16), wout_ref[0])
        for t in range(c):
            y = _dot(a[:, :(t + 1) * LANES], toep_ref[0, (c - 1 - t) * LANES:, :])
            y_ref[pl.ds(blk * rb * c + t, rb, stride=c), :] = y + carried[:, t * LANES:(t + 1) * LANES]


def _s5_block_diag(mats):
    kk, w_in, w_out, pa, pb = mats
    g, c, h, p2 = kk.shape[0], S5_CHUNK, SSM_GROUP, 2 * SSM_STATE
    ns = g // S5_SET
    eye = jnp.eye(S5_SET, dtype=kk.dtype)
    lag_bd = jnp.einsum('sglhc,gk->slgckh', kk.reshape(ns, S5_SET, c + 1, h, h), eye).reshape(ns, c + 1, LANES, LANES)
    toep_bd = lag_bd[:, c - 1::-1].reshape(ns, c * LANES, LANES)

    w_in_c = w_in.reshape(ns, S5_SET, c, h, p2).transpose(0, 2, 1, 3, 4).reshape(ns, c, LANES, p2)
    w_out_c = w_out.reshape(ns, S5_SET, p2, c, h).transpose(0, 3, 2, 1, 4).reshape(ns, c, p2, LANES)
    k = pa.shape[1]
    lanes_of = lambda x: x.reshape(ns, S5_SET, k, p2).transpose(0, 2, 1, 3).reshape(ns, k, S5_SET * p2)
    return toep_bd, w_in_c, w_out_c, lanes_of(pa), lanes_of(pb)


def _s5(u, mats, bsz, seq_len):
    toep, w_in, w_out, pa, pb = _s5_block_diag(mats)
    ns = toep.shape[0]
    nc = seq_len // S5_CHUNK
    per_set = lambda s, b: (s, 0, 0)
    once = dict(pipeline_mode=pl.Buffered(1))
    io_spec = pl.BlockSpec((seq_len, LANES), lambda s, b: (b, s), **once)
    return pl.pallas_call(
        _s5_kernel,
        grid=(ns, bsz),
        in_specs=[
            io_spec,
            pl.BlockSpec((1,) + toep.shape[1:], per_set, **once),
            pl.BlockSpec((1,) + w_in.shape[1:], lambda s, b: (s, 0, 0, 0)),
            pl.BlockSpec((1,) + w_out.shape[1:], lambda s, b: (s, 0, 0, 0)),
            pl.BlockSpec((1,) + pa.shape[1:], per_set),
            pl.BlockSpec((1,) + pb.shape[1:], per_set),
        ],
        out_specs=io_spec,
        out_shape=jax.ShapeDtypeStruct(u.shape, F32),
        scratch_shapes=[pltpu.VMEM((nc, S5_SET * 2 * SSM_STATE), F32),
                        pltpu.VMEM((1, S5_CHUNK * LANES, S5_SET * 2 * SSM_STATE), BF16),
                        pltpu.VMEM((1, S5_SET * 2 * SSM_STATE, S5_CHUNK * LANES), BF16)],
        compiler_params=_params("arbitrary", "arbitrary"),
        name="s5_scan",
    )(u, toep, w_in, w_out, pa, pb)


def _moba_select_kernel(q_ref, k_ref, tri_ref, sel_ref, cnt_out_ref, km_ref, cnt_ref):
    h = pl.program_id(1)
    t = pl.program_id(2)
    ts = q_ref.shape[0]
    nbk = MOBA_MAX_BLOCKS

    @pl.when(t == 0)
    def _():
        kk = k_ref[...].astype(F32)
        nb = kk.shape[0] // MOBA_BLOCK
        km = jnp.sum(kk.reshape(nb, MOBA_BLOCK, LANES), axis=1) * (1.0 / MOBA_BLOCK)
        if nb < nbk:
            km = jnp.concatenate([km, jnp.zeros((nbk - nb, LANES), F32)], axis=0)
        lane = lax.broadcasted_iota(jnp.int32, (nbk, LANES), 1)
        km_ref[...] = jnp.where((lane // MOBA_HEAD_DIM) == (h % 2), km, 0.0).astype(BF16)
        cnt_ref[...] = jnp.zeros(cnt_ref.shape, F32)

    gate = _dot_nt(km_ref[...], q_ref[...])
    blk = lax.broadcasted_iota(jnp.int32, gate.shape, 0)
    qblk = (t * ts + lax.broadcasted_iota(jnp.int32, gate.shape, 1)) // MOBA_BLOCK
    g = jnp.where(blk < qblk, gate, -jnp.inf)
    selected = jnp.zeros(gate.shape, jnp.bool_)
    picks = []
    for _ in range(MOBA_TOPK):
        mx = jnp.max(g, axis=0, keepdims=True)
        idx = jnp.min(jnp.where(g == mx, blk, nbk), axis=0, keepdims=True)
        hit = blk == idx
        ok = (idx[0:1] < qblk[0:1]) & (mx > -jnp.inf)
        picks.append((hit, idx, ok))
        selected = selected | (hit & ok)
        g = jnp.where(hit, -jnp.inf, g)
    sel01 = jnp.where(selected, 1.0, 0.0)
    chunks = [sel01[:, c * LANES:(c + 1) * LANES] for c in range(ts // LANES)]
    within = _dot(jnp.concatenate(chunks, axis=0).astype(BF16), tri_ref[...])
    base = cnt_ref[...]
    pieces = []
    for c, chunk in enumerate(chunks):
        pieces.append(within[c * nbk:(c + 1) * nbk] + base)
        base = base + jnp.sum(chunk, axis=1, keepdims=True)
    prefix = jnp.concatenate(pieces, axis=1)
    rows = [jnp.where(ok, idx, -1) for (_, idx, ok) in picks]
    rows += [jnp.sum(jnp.where(hit, prefix, 0.0), axis=0, keepdims=True).astype(jnp.int32) for (hit, _, _) in picks]
    rows += [jnp.zeros((1, ts), jnp.int32)] * (8 - 2 * MOBA_TOPK)
    sel_ref[0, 0] = jnp.concatenate(rows, axis=0)
    cnt_ref[...] = base
    cnt_out_ref[0, 0] = base


def _moba_select(q2, kv, bsz, seq_len, ts=1024):
    ts = min(ts, seq_len)
    heads = q2.shape[1] // MOBA_HEAD_DIM
    nt = seq_len // ts
    tri = (jnp.arange(LANES)[:, None] < jnp.arange(LANES)[None, :]).astype(BF16)
    return pl.pallas_call(
        _moba_select_kernel,
        grid=(bsz, heads, nt),
        in_specs=[
            pl.BlockSpec((ts, LANES), lambda b, h, t: (b * nt + t, h // 2)),
            pl.BlockSpec((seq_len, LANES), lambda b, h, t: (b, 2 * (h // 2))),
            pl.BlockSpec((LANES, LANES), lambda b, h, t: (0, 0)),
        ],
        out_specs=[pl.BlockSpec((1, 1, 8, ts), lambda b, h, t: (b, h, 0, t)),
                   pl.BlockSpec((1, 1, MOBA_MAX_BLOCKS, LANES), lambda b, h, t: (b, h, 0, 0))],
        out_shape=[jax.ShapeDtypeStruct((bsz, heads, 8, seq_len), jnp.int32),
                   jax.ShapeDtypeStruct((bsz, heads, MOBA_MAX_BLOCKS, LANES), F32)],
        scratch_shapes=[pltpu.VMEM((MOBA_MAX_BLOCKS, LANES), BF16), pltpu.VMEM((MOBA_MAX_BLOCKS, LANES), F32)],
        compiler_params=_params("parallel", "arbitrary", "arbitrary"),
        name="moba_select",
    )(q2, kv, tri)


MOBA_DIAG_BLOCKS = 4


def _moba_diag_kernel(q_ref, kv_ref, o_ref, lse_ref):
    lane = lax.broadcasted_iota(jnp.int32, (MOBA_BLOCK, LANES), 1)
    is_a = lane < MOBA_HEAD_DIM
    r = lax.broadcasted_iota(jnp.int32, (MOBA_BLOCK, MOBA_BLOCK), 0)
    c = lax.broadcasted_iota(jnp.int32, (MOBA_BLOCK, MOBA_BLOCK), 1)
    for u in range(MOBA_DIAG_BLOCKS):
        rows = pl.ds(u * MOBA_BLOCK, MOBA_BLOCK)
        q = q_ref[rows, :]
        kb = kv_ref[rows, :LANES]
        vb = kv_ref[rows, LANES:]
        pvs, ms = [], []
        for own in (is_a, jnp.logical_not(is_a)):
            s = jnp.where(c <= r, _dot_nt(jnp.where(own, q, jnp.zeros_like(q)), kb), NEG_BIG)
            m = jnp.max(s, axis=1, keepdims=True)
            p = jnp.exp(s - m).astype(BF16)
            pvs.append(_dot(p, jnp.where(own, vb, 1.0)))
            ms.append(m)
        num = jnp.where(is_a, pvs[0], pvs[1])
        den = pltpu.roll(jnp.where(is_a, pvs[1], pvs[0]), MOBA_HEAD_DIM, 1)
        o_ref[rows, :] = num / den
        lse_ref[rows, :] = jnp.where(is_a, ms[0], ms[1]) + jnp.log(den)


def _moba_diag(q2, kv, bsz, seq_len):
    npair = q2.shape[1] // LANES
    rows = MOBA_BLOCK * MOBA_DIAG_BLOCKS
    nb = seq_len // rows
    spec = pl.BlockSpec((rows, LANES), lambda b, p, i: (b * nb + i, p))
    out = jax.ShapeDtypeStruct(q2.shape, F32)
    return pl.pallas_call(
        _moba_diag_kernel,
        grid=(bsz, npair, nb),
        in_specs=[spec, pl.BlockSpec((rows, 2 * LANES), lambda b, p, i: (b * nb + i, p))],
        out_specs=[spec, spec],
        out_shape=[out, out],
        compiler_params=_params("parallel", "parallel", "parallel"),
        name="moba_diag",
    )(q2, kv)


MOBA_TILE = 256
MOBA_TILES_PER_STEP = 8


def _moba_grouped_kernel(tile_row_ref, tile_pair_ref, tile_half_ref, tile_real_ref, qd_ref, *refs):
    n = MOBA_TILES_PER_STEP
    kv_refs, o_ref = refs[:n], refs[n + 1]
    i = pl.program_id(0)
    lane = lax.broadcasted_iota(jnp.int32, (MOBA_TILE, LANES), 1)
    kv_lane = lax.broadcasted_iota(jnp.int32, (MOBA_BLOCK, LANES), 1)

    @pl.when(tile_real_ref[i * n] > 0)
    def _():
        for u in range(n):
            tile = i * n + u
            rows = pl.ds(u * MOBA_TILE, MOBA_TILE)
            own = (lane // MOBA_HEAD_DIM) == tile_half_ref[tile]
            s = _dot_nt(qd_ref[rows, :].astype(BF16), kv_refs[u][:, :LANES])
            m = jnp.max(s, axis=1, keepdims=True)
            p = jnp.exp(s - m).astype(BF16)
            vb = jnp.where((kv_lane // MOBA_HEAD_DIM) == tile_half_ref[tile], kv_refs[u][:, LANES:], 1.0)
            pv = _dot(p, vb)
            row_sum = pltpu.roll(pv, MOBA_HEAD_DIM, 1)
            part = jnp.where(own, pv / row_sum, m + jnp.log(pv))
            o_ref[rows, :] = jnp.where(tile_real_ref[tile] > 0, part, NEG_BIG)

    @pl.when(tile_real_ref[i * n] == 0)
    def _():
        o_ref[...] = jnp.full(o_ref.shape, NEG_BIG, F32)


def _moba_grouped(qd, kv, tile_row, tile_pair, tile_half, tile_real, run_after):
    n = MOBA_TILES_PER_STEP
    n_tiles = qd.shape[0] // MOBA_TILE

    def kv_spec(u):
        return pl.BlockSpec((MOBA_BLOCK, 2 * LANES), lambda i, tr, tp, th, tl: (tr[i * n + u], tp[i * n + u]))

    grid_spec = pltpu.PrefetchScalarGridSpec(
        num_scalar_prefetch=4,
        grid=(n_tiles // n,),
        in_specs=[pl.BlockSpec((n * MOBA_TILE, LANES), lambda i, tr, tp, th, tl: (i, 0))]
        + [kv_spec(u) for u in range(n)]
        + [pl.BlockSpec((8, LANES), lambda i, tr, tp, th, tl: (0, 0))],
        out_specs=pl.BlockSpec((n * MOBA_TILE, LANES), lambda i, tr, tp, th, tl: (i, 0)),
    )
    return pl.pallas_call(
        _moba_grouped_kernel,
        grid_spec=grid_spec,
        out_shape=jax.ShapeDtypeStruct(qd.shape, F32),
        compiler_params=_params("arbitrary"),
        name="moba_grouped",
    )(tile_row, tile_pair, tile_half, tile_real, qd, *([kv] * n), run_after)


def _moba_combine_kernel(od_ref, lsed_ref, *refs):
    g_refs, o_ref = refs[:-1], refs[-1]
    lane = lax.broadcasted_iota(jnp.int32, od_ref.shape, 1)
    is_a = lane < MOBA_HEAD_DIM
    parts = [(od_ref[...], lsed_ref[...])]
    for s in range(MOBA_TOPK):
        xa = g_refs[s][0, 0, 0]
        xb = g_refs[MOBA_TOPK + s][0, 0, 0]
        o = jnp.where(is_a, xa, xb)
        lse = pltpu.roll(jnp.where(is_a, xb, xa), MOBA_HEAD_DIM, 1)
        parts.append((o, lse))
    m = parts[0][1]
    for _, lse in parts[1:]:
        m = jnp.maximum(m, lse)
    num = jnp.zeros(od_ref.shape, F32)
    den = jnp.zeros(od_ref.shape, F32)
    for o, lse in parts:
        w = jnp.exp(lse - m)
        num = num + w * o
        den = den + w
    o_ref[...] = (num / den).astype(BF16)


def _moba_combine(od, lsed, gath, bsz, seq_len, tm=1024):
    tm = min(tm, seq_len)
    npair = od.shape[1] // LANES
    nt = seq_len // tm
    spec = pl.BlockSpec((tm, LANES), lambda b, p, i: (b * nt + i, p))

    def g_spec(e, s):
        return pl.BlockSpec((1, 1, 1, tm, LANES), lambda b, p, i: (b, 2 * p + e, s, i, 0))

    return pl.pallas_call(
        _moba_combine_kernel,
        grid=(bsz, npair, nt),
        in_specs=[spec, spec] + [g_spec(e, s) for e in range(2) for s in range(MOBA_TOPK)],
        out_specs=spec,
        out_shape=jax.ShapeDtypeStruct(od.shape, BF16),
        compiler_params=_params("parallel", "parallel", "parallel"),
        name="moba_combine",
    )(od, lsed, *([gath] * (2 * MOBA_TOPK)))


SC_WINDOW = 128
SC_CORES = 2
SC_SUBCORES = 16


def _sc_mesh():
    return plsc.VectorSubcoreMesh(core_axis_name="core", subcore_axis_name="subcore")


def _sc_gather_rows(table, idx):
    n = idx.shape[0]
    d = table.shape[1]
    window = SC_WINDOW
    assert n % (window * SC_CORES * SC_SUBCORES) == 0
    per_core = n // window // SC_CORES

    @functools.partial(pl.kernel, out_type=jax.ShapeDtypeStruct((n, d), table.dtype), mesh=_sc_mesh())
    def gather_kernel(x_hbm, i_hbm, o_hbm):
        base = lax.axis_index("core") * per_core

        def body(i_vmem, o_vmem):
            pltpu.sync_copy(x_hbm.at[i_vmem.at[0]], o_vmem)

        pltpu.emit_pipeline(
            body,
            grid=(per_core,),
            in_specs=[pl.BlockSpec((1, window), index_map=lambda i: (0, base + i))],
            out_specs=[pl.BlockSpec((window, d), index_map=lambda i: (base + i, 0))],
            core_axis_name="subcore",
            dimension_semantics=(pltpu.PARALLEL,),
            trace_scopes=False,
        )(i_hbm, o_hbm)

    return gather_kernel(table, idx.reshape(1, n))


def _sc_scatter_rows(rows, dest, n_out, repeat):
    n_src, d = rows.shape
    n = dest.shape[0]
    assert n == repeat * n_src
    window = SC_WINDOW
    assert n % (window * SC_CORES * SC_SUBCORES) == 0
    per_core = n // window // SC_CORES
    src_windows = n_src // window

    @functools.partial(pl.kernel, out_type=jax.ShapeDtypeStruct((n_out, d), rows.dtype), mesh=_sc_mesh())
    def scatter_kernel(x_hbm, i_hbm, o_hbm):
        base = lax.axis_index("core") * per_core

        def body(x_vmem, i_vmem):
            pltpu.sync_copy(x_vmem, o_hbm.at[i_vmem.at[0]])

        pltpu.emit_pipeline(
            body,
            grid=(per_core,),
            in_specs=[pl.BlockSpec((window, d), index_map=lambda i: ((base + i) % src_windows, 0)),
                      pl.BlockSpec((1, window), index_map=lambda i: (0, base + i))],
            out_specs=[],
            core_axis_name="subcore",
            dimension_semantics=(pltpu.PARALLEL,),
            trace_scopes=False,
        )(x_hbm, i_hbm)

    return scatter_kernel(rows, dest.reshape(1, n))


def _moba_dispatch(q2, qh, kv, bsz, seq_len):
    heads = q2.shape[1] // MOBA_HEAD_DIM
    nbk = MOBA_MAX_BLOCKS
    sel, cnt = _moba_select(q2, kv, bsz, seq_len)
    counts = cnt[..., 0].astype(jnp.int32).reshape(bsz * heads * nbk)
    pcounts = ((counts + MOBA_TILE - 1) // MOBA_TILE) * MOBA_TILE
    pends = jnp.cumsum(pcounts)
    pstarts = (pends - pcounts).reshape(bsz, heads, 1, 1, nbk)
    n_items = bsz * heads * seq_len * MOBA_TOPK
    step_rows = MOBA_TILE * MOBA_TILES_PER_STEP
    n_rows = -(-(n_items + bsz * heads * nbk * MOBA_TILE) // step_rows) * step_rows
    n_null = step_rows
    idx = sel[:, :, 0:MOBA_TOPK, :]
    rank = sel[:, :, MOBA_TOPK:2 * MOBA_TOPK, :]
    start = jnp.sum(jnp.where(idx[..., None] == jnp.arange(nbk), pstarts, 0), axis=-1)
    null_row = n_rows + jnp.arange(seq_len, dtype=jnp.int32) % n_null
    dest = jnp.where(idx >= 0, start + rank, null_row)
    n_tiles = (n_rows + n_null) // MOBA_TILE
    n_groups = bsz * heads * nbk
    tile_start = jnp.arange(n_tiles, dtype=jnp.int32) * MOBA_TILE
    tile_g = jnp.minimum(jnp.sum((pends[None, :] <= tile_start[:, None]).astype(jnp.int32), axis=1), n_groups - 1)
    tile_real = (tile_start < pends[-1]).astype(jnp.int32)
    tile_head = (tile_g // nbk) % heads
    tile_row = (tile_g // (heads * nbk)) * (seq_len // MOBA_BLOCK) + jnp.minimum(tile_g % nbk, seq_len // MOBA_BLOCK - 1)
    dest_by_slot = dest.transpose(2, 1, 0, 3).reshape(-1)
    qd = _sc_scatter_rows(qh.reshape(-1, LANES), dest_by_slot, n_rows + n_null, MOBA_TOPK)
    return qd, dest, (tile_row, tile_head // 2, tile_head % 2, tile_real)


def _moba_finish(qd, dest, tiles, od, lsed, kv, bsz, seq_len):
    heads = dest.shape[1]
    part = _moba_grouped(qd, kv, *tiles, lsed)
    gath = _sc_gather_rows(part, dest.reshape(-1)).reshape(bsz, heads, MOBA_TOPK, seq_len, LANES)
    return _moba_combine(od, lsed, gath, bsz, seq_len)


def _mem_kv_kernel(mem_ref, gmem_ref, w_ref, gck_ref, k_ref, v_ref):
    xf = mem_ref[0]
    ms = jnp.mean(xf * xf, axis=-1, keepdims=True)
    h = (xf * lax.rsqrt(ms + EPS) * gmem_ref[...]).astype(BF16)
    kv = _dot(h, w_ref[...])
    w = k_ref.shape[2]
    hd = w // X_HEADS
    for c in range(X_HEADS):
        chunk = kv[:, c * hd:(c + 1) * hd]
        cms = jnp.mean(chunk * chunk, axis=-1, keepdims=True)
        k_ref[0, :, c * hd:(c + 1) * hd] = (chunk * lax.rsqrt(cms + EPS) * gck_ref[...]).astype(BF16)
    v_ref[0] = kv[:, w:].astype(BF16)


def _mem_kv(mem, g_mem, w_kv_mem, g_ck):
    bsz, m, d = mem.shape
    w = w_kv_mem.shape[1] // 2
    const = lambda b: (0, 0)
    out = jax.ShapeDtypeStruct((bsz, m, w), BF16)
    return pl.pallas_call(
        _mem_kv_kernel,
        grid=(bsz,),
        in_specs=[pl.BlockSpec((1, m, d), lambda b: (b, 0, 0)), pl.BlockSpec((1, d), const),
                  pl.BlockSpec((d, 2 * w), const), pl.BlockSpec((1, w // X_HEADS), const)],
        out_specs=[pl.BlockSpec((1, m, w), lambda b: (b, 0, 0))] * 2,
        out_shape=[out, out],
        compiler_params=_params("parallel"),
        name="mem_kv",
    )(mem, g_mem.reshape(1, d), w_kv_mem.astype(BF16), g_ck.reshape(1, -1))


def _merge_kernel(x_ref, ys_ref, u_ref, dskip_ref, om_ref, xq_ref, kc_ref, vc_ref, g_ref,
                  wglu_ref, wmo_ref, wco_ref, wout_ref, gffn_ref, wr_ref, br_ref,
                  x1_ref, h2_ref, logit_ref):
    d = x_ref.shape[1]
    y = ys_ref[...].astype(F32) + dskip_ref[...] * u_ref[...].astype(F32)
    ge = 0.5 * y * (1.0 + jnp.tanh(math.sqrt(2.0 / math.pi) * (y + 0.044715 * (y * y * y))))
    z = _dot(ge.astype(BF16), wglu_ref[...])
    merged = g_ref[:, 0:d].astype(F32) * (z[:, :d] * _sigmoid(z[:, d:]))
    merged = merged + g_ref[:, d:2 * d].astype(F32) * _dot(om_ref[...], wmo_ref[...])
    w = xq_ref.shape[1]
    hd = w // X_HEADS
    heads = []
    for c in range(X_HEADS):
        s = _dot_nt(xq_ref[:, c * hd:(c + 1) * hd], kc_ref[0, :, c * hd:(c + 1) * hd]) * (hd ** -0.5)
        p = jnp.exp(s - jnp.max(s, axis=1, keepdims=True))
        p = p / jnp.sum(p, axis=1, keepdims=True)
        heads.append(_dot(p.astype(BF16), vc_ref[0, :, c * hd:(c + 1) * hd]))
    oc = jnp.concatenate(heads, axis=1).astype(BF16)
    merged = merged + g_ref[:, 2 * d:3 * d].astype(F32) * _dot(oc, wco_ref[...])
    x1 = x_ref[...] + _dot(merged.astype(BF16), wout_ref[...])
    x1_ref[...] = x1
    ms = jnp.mean(x1 * x1, axis=-1, keepdims=True)
    h2 = (x1 * lax.rsqrt(ms + EPS) * gffn_ref[...]).astype(BF16)
    packed = _pack_bf16_pairs(h2.astype(F32))
    for j in range(h2_ref.shape[0]):
        h2_ref[j] = packed[:, j * LANES:(j + 1) * LANES]
    logit_ref[...] = _dot_nt(wr_ref[...], h2) + br_ref[...]


def _merge(xt, ys, u, d_skip, om, xqn, kc, vc, gates, w_glu, w_mo, w_co, w_out, g_ffn, w_router, b_router,
           seq_len, tm=512):
    t, d = xt.shape
    tm = min(tm, seq_len)
    w = d // 2
    m = kc.shape[1]
    ne = w_router.shape[1]
    nt = seq_len // tm
    row = lambda i: (i, 0)
    const = lambda i: (0, 0)
    per_b = lambda i: (i // nt, 0, 0)
    once = dict(pipeline_mode=pl.Buffered(1))
    return pl.pallas_call(
        _merge_kernel,
        grid=(t // tm,),
        in_specs=[
            pl.BlockSpec((tm, d), row), pl.BlockSpec((tm, w), row), pl.BlockSpec((tm, w), row),
            pl.BlockSpec((1, w), const), pl.BlockSpec((tm, w), row), pl.BlockSpec((tm, w), row),
            pl.BlockSpec((1, m, w), per_b), pl.BlockSpec((1, m, w), per_b),
            pl.BlockSpec((tm, N_BRANCH * d), row),
            pl.BlockSpec((w, 2 * d), const, **once), pl.BlockSpec((w, d), const, **once),
            pl.BlockSpec((w, d), const, **once), pl.BlockSpec((d, d), const, **once), pl.BlockSpec((1, d), const),
            pl.BlockSpec((ne, d), const), pl.BlockSpec((ne, 1), const),
        ],
        out_specs=[pl.BlockSpec((tm, d), row), pl.BlockSpec((d // 2 // LANES, tm, LANES), lambda i: (0, i, 0)),
                   pl.BlockSpec((ne, tm), lambda i: (0, i))],
        out_shape=[jax.ShapeDtypeStruct((t, d), F32), jax.ShapeDtypeStruct((d // 2 // LANES, t, LANES), jnp.uint32),
                   jax.ShapeDtypeStruct((ne, t), F32)],
        compiler_params=_params("parallel"),
        name="merge",
    )(xt, ys, u, d_skip.reshape(1, w), om, xqn, kc, vc, gates,
      w_glu.astype(BF16), w_mo.astype(BF16), w_co.astype(BF16), w_out.astype(BF16),
      g_ffn.reshape(1, d), w_router.T.astype(BF16), b_router.reshape(ne, 1))


def _moe_kernel(blk_e_ref, blk_used_ref, xs_ref, wgu_ref, bgu_ref, wd_ref, bd_ref, y_ref, wgu_bf, wd_bf):
    i = pl.program_id(0)
    prev = blk_e_ref[jnp.maximum(i - 1, 0)]

    @pl.when((i == 0) | (blk_e_ref[i] != prev))
    def _():
        wgu_bf[...] = wgu_ref[0].astype(BF16)
        wd_bf[...] = wd_ref[0].astype(BF16)

    @pl.when(blk_used_ref[i] > 0)
    def _():
        de = wd_bf.shape[0]
        words = jnp.concatenate([xs_ref[j] for j in range(xs_ref.shape[0])], axis=1)
        xs = jnp.concatenate(_unpack_bf16_pairs(words), axis=1).astype(BF16)
        gu = _dot(xs, wgu_bf[...]) + bgu_ref[0]
        gate = jnp.minimum(gu[:, :de], SWIGLU_LIMIT)
        up = jnp.clip(gu[:, de:], -SWIGLU_LIMIT, SWIGLU_LIMIT)
        act = gate * _sigmoid(SWIGLU_ALPHA * gate) * (up + 1.0)
        y = _dot(act.astype(BF16), wd_bf[...]) + bd_ref[0]
        packed = _pack_bf16_pairs(y.astype(BF16).astype(F32))
        for j in range(y_ref.shape[0]):
            y_ref[j] = packed[:, j * LANES:(j + 1) * LANES]

    @pl.when(blk_used_ref[i] == 0)
    def _():
        y_ref[...] = jnp.zeros(y_ref.shape, y_ref.dtype)


def _moe_experts(xs, blk_e, blk_used, w_gu, b_gu, w_down, b_down):
    slabs, p, _ = xs.shape
    d = 2 * slabs * LANES
    ne, _, de2 = w_gu.shape
    de = de2 // 2
    nblk = p // EXPERT_ROWS
    row_spec = pl.BlockSpec((slabs, EXPERT_ROWS, LANES), lambda i, e, n: (0, i, 0))
    grid_spec = pltpu.PrefetchScalarGridSpec(
        num_scalar_prefetch=2,
        grid=(nblk,),
        in_specs=[
            row_spec,
            pl.BlockSpec((1, d, de2), lambda i, e, n: (e[i], 0, 0)),
            pl.BlockSpec((1, 1, de2), lambda i, e, n: (e[i], 0, 0)),
            pl.BlockSpec((1, de, d), lambda i, e, n: (e[i], 0, 0)),
            pl.BlockSpec((1, 1, d), lambda i, e, n: (e[i], 0, 0)),
        ],
        out_specs=row_spec,
        scratch_shapes=[pltpu.VMEM((d, de2), BF16), pltpu.VMEM((de, d), BF16)],
    )
    return pl.pallas_call(
        _moe_kernel,
        grid_spec=grid_spec,
        out_shape=jax.ShapeDtypeStruct(xs.shape, jnp.uint32),
        compiler_params=_params("arbitrary"),
        name="moe_experts",
    )(blk_e, blk_used, xs, w_gu, b_gu.reshape(ne, 1, de2), w_down, b_down.reshape(ne, 1, d))


def _router_kernel(lg_ref, tri_ref, e_ref, w_ref, r_ref, cnt_out_ref, cnt_ref):
    t = pl.program_id(0)
    ne, ts = lg_ref.shape

    @pl.when(t == 0)
    def _():
        cnt_ref[...] = jnp.zeros(cnt_ref.shape, F32)

    g = lg_ref[...]
    eid = lax.broadcasted_iota(jnp.int32, g.shape, 0)
    selected = jnp.zeros(g.shape, jnp.bool_)
    picks = []
    for _ in range(TOPK_EXPERTS):
        mx = jnp.max(g, axis=0, keepdims=True)
        idx = jnp.min(jnp.where(g == mx, eid, ne), axis=0, keepdims=True)
        hit = eid == idx
        picks.append((hit, idx, mx))
        selected = selected | hit
        g = jnp.where(hit, -jnp.inf, g)
    exps = [jnp.exp(mx - picks[0][2]) for (_, _, mx) in picks]
    total = sum(exps)
    sel01 = jnp.where(selected, 1.0, 0.0).astype(BF16)
    prefix = _dot(sel01, tri_ref[...]) + jnp.tile(cnt_ref[...], (1, ts // LANES))
    pad_i = [jnp.zeros((1, ts), jnp.int32)] * (8 - TOPK_EXPERTS)
    e_ref[...] = jnp.concatenate([idx for (_, idx, _) in picks] + pad_i, axis=0)
    w_ref[...] = jnp.concatenate([e / total for e in exps] + [jnp.zeros((1, ts), F32)] * (8 - TOPK_EXPERTS), axis=0)
    r_ref[...] = jnp.concatenate(
        [jnp.sum(jnp.where(hit, prefix, 0.0), axis=0, keepdims=True).astype(jnp.int32) for (hit, _, _) in picks]
        + pad_i, axis=0)
    cnt_ref[...] = cnt_ref[...] + _dot(sel01, jnp.ones((ts, LANES), BF16))
    cnt_out_ref[...] = cnt_ref[...]


def _router(logits_t, ts=1024):
    ne, t = logits_t.shape
    ts = min(ts, t)
    tri = (jnp.arange(ts)[:, None] < jnp.arange(ts)[None, :]).astype(BF16)
    rows = pl.BlockSpec((8, ts), lambda i: (0, i))
    return pl.pallas_call(
        _router_kernel,
        grid=(t // ts,),
        in_specs=[pl.BlockSpec((ne, ts), lambda i: (0, i)), pl.BlockSpec((ts, ts), lambda i: (0, 0))],
        out_specs=[rows, rows, rows, pl.BlockSpec((ne, LANES), lambda i: (0, 0))],
        out_shape=[jax.ShapeDtypeStruct((8, t), jnp.int32), jax.ShapeDtypeStruct((8, t), F32),
                   jax.ShapeDtypeStruct((8, t), jnp.int32), jax.ShapeDtypeStruct((ne, LANES), F32)],
        scratch_shapes=[pltpu.VMEM((ne, LANES), F32)],
        compiler_params=_params("arbitrary"),
        name="moe_router",
    )(logits_t, tri)


def _moe_mix_kernel(x1_ref, w_ref, pk_ref, o_ref):
    slabs = pk_ref.shape[0]
    wts = w_ref[...]
    lo = [jnp.zeros((x1_ref.shape[0], LANES), F32)] * slabs
    hi = [jnp.zeros((x1_ref.shape[0], LANES), F32)] * slabs
    for k in range(TOPK_EXPERTS):
        wk = wts[:, k:k + 1]
        for j in range(slabs):
            a, b = _unpack_bf16_pairs(pk_ref[j, k])
            lo[j] = lo[j] + wk * a
            hi[j] = hi[j] + wk * b
    o_ref[...] = x1_ref[...] + jnp.concatenate(lo + hi, axis=1)


def _moe_mix(x1, weights_tk, picked, tm=512):
    t, d = x1.shape
    slabs = picked.shape[0]
    return pl.pallas_call(
        _moe_mix_kernel,
        grid=(t // tm,),
        in_specs=[pl.BlockSpec((tm, d), lambda i: (i, 0)),
                  pl.BlockSpec((tm, TOPK_EXPERTS), lambda i: (i, 0)),
                  pl.BlockSpec((slabs, TOPK_EXPERTS, tm, LANES), lambda i: (0, 0, i, 0))],
        out_specs=pl.BlockSpec((tm, d), lambda i: (i, 0)),
        out_shape=jax.ShapeDtypeStruct((t, d), F32),
        compiler_params=_params("parallel"),
        name="moe_mix",
    )(x1, weights_tk, picked)


def _moe_ffn(x1, h2, logits_t, w_gu, b_gu, w_down, b_down):
    t, d = x1.shape
    tk = t * TOPK_EXPERTS
    e8, w8, r8, cnt = _router(logits_t)
    top_e, weights, rank = e8[:TOPK_EXPERTS], w8[:TOPK_EXPERTS], r8[:TOPK_EXPERTS]
    counts = cnt[:, 0].astype(jnp.int32)
    pcounts = ((counts + EXPERT_ROWS - 1) // EXPERT_ROWS) * EXPERT_ROWS
    pends = jnp.cumsum(pcounts)
    pstarts = pends - pcounts
    is_e = top_e[..., None] == jnp.arange(N_EXPERTS, dtype=jnp.int32)
    dest_by_k = jnp.sum(jnp.where(is_e, pstarts, 0), axis=-1) + rank
    nblk = -(-tk // EXPERT_ROWS) + N_EXPERTS
    blk_start = jnp.arange(nblk, dtype=jnp.int32) * EXPERT_ROWS
    blk_e = jnp.minimum(jnp.sum((pends[None, :] <= blk_start[:, None]).astype(jnp.int32), axis=1), N_EXPERTS - 1)
    blk_used = (blk_start < pends[-1]).astype(jnp.int32)
    slabs = d // 2 // LANES
    p = nblk * EXPERT_ROWS
    slab_off = jnp.arange(slabs, dtype=jnp.int32) * p
    dest_kst = (dest_by_k[:, None, :] + slab_off[None, :, None]).reshape(-1)
    dest_skt = (dest_by_k[None, :, :] + slab_off[:, None, None]).reshape(-1)
    xs = _sc_scatter_rows(h2.reshape(slabs * t, LANES), dest_kst, slabs * p, TOPK_EXPERTS)
    ys = _moe_experts(xs.reshape(slabs, p, LANES), blk_e, blk_used, w_gu, b_gu, w_down, b_down)
    picked = _sc_gather_rows(ys.reshape(slabs * p, LANES), dest_skt).reshape(slabs, TOPK_EXPERTS, t, LANES)
    return _moe_mix(x1, weights.T, picked)


def kernel(x, mem, g_mix, w_in, lam_re, lam_im, log_dt, b_re, b_im, c_re, c_im, d_skip, w_glu, g_q, g_k, w_moba_out, g_mem, w_kv_mem, g_cq, g_ck, w_cross_out, w_out, g_ffn, w_router, b_router, w_gu, b_gu, w_down, b_down):
    bsz, seq_len, d = x.shape
    xt = x.reshape(bsz * seq_len, d)
    for l in range(g_mix.shape[0]):
        u, q2, kv, xqn, gates, qh = _in_proj(xt, g_mix[l], w_in[l], g_q[l], g_k[l], g_cq[l], seq_len)
        qd, dest, tiles = _moba_dispatch(q2, qh, kv, bsz, seq_len)
        mats = _s5_matrices(lam_re[l], lam_im[l], log_dt[l], b_re[l], b_im[l], c_re[l], c_im[l],
                            seq_len // S5_CHUNK)
        ys = _s5(u, mats, bsz, seq_len)
        od, lsed = _moba_diag(q2, kv, bsz, seq_len)
        kc, vc = _mem_kv(mem, g_mem[l], w_kv_mem[l], g_ck[l])
        om = _moba_finish(qd, dest, tiles, od, lsed, kv, bsz, seq_len)
        x1, h2, logits = _merge(xt, ys, u, d_skip[l], om, xqn, kc, vc, gates, w_glu[l], w_moba_out[l],
                                w_cross_out[l], w_out[l], g_ffn[l], w_router[l], b_router[l], seq_len)
        xt = _moe_ffn(x1, h2, logits, w_gu[l], b_gu[l], w_down[l], b_down[l])
    return xt.reshape(bsz, seq_len, d)
```

```python
import functools
import math

import jax
import jax.numpy as jnp
from jax import lax
from jax.experimental import pallas as pl
from jax.experimental.pallas import tpu as pltpu
from jax.experimental.pallas import tpu_sc as plsc

F32 = jnp.float32
BF16 = jnp.bfloat16

EPS = 1e-6
N_BRANCH = 3
SSM_GROUP = 16
SSM_STATE = 64
S5_CHUNK = 16
MOBA_HEAD_DIM = 64
MOBA_BLOCK = 256
MOBA_TOPK = 3
MOBA_MAX_BLOCKS = 64
ROPE_THETA = 10000.0
X_HEADS = 4
N_EXPERTS = 32
TOPK_EXPERTS = 4
SWIGLU_LIMIT = 7.0
SWIGLU_ALPHA = 1.702
EXPERT_ROWS = 512
NEG_BIG = -1e30
LANES = 128
VMEM_LIMIT_BYTES = 56 * 1024 * 1024


def _params(*sem):
    return pltpu.CompilerParams(dimension_semantics=sem, vmem_limit_bytes=VMEM_LIMIT_BYTES)


def _sigmoid(x):
    return 1.0 / (1.0 + jnp.exp(-x))


def _dot(a, b):
    return jnp.dot(a, b, preferred_element_type=F32)


def _pack_bf16_pairs(x):
    n = x.shape[1] // 2
    lo = lax.bitcast_convert_type(x[:, :n], jnp.uint32) >> 16
    hi = lax.bitcast_convert_type(x[:, n:], jnp.uint32) & jnp.uint32(0xFFFF0000)
    return lo | hi


def _unpack_bf16_pairs(w):
    lo = lax.bitcast_convert_type(w << 16, F32)
    hi = lax.bitcast_convert_type(w & jnp.uint32(0xFFFF0000), F32)
    return lo, hi


def _dot_nt(a, b):
    return lax.dot_general(a, b, (((1,), (1,)), ((), ())), preferred_element_type=F32)


def _inproj_kernel(x_ref, gmix_ref, wa_ref, wg_ref, e64_ref, gq_ref, gk_ref, gcq_ref, cos_ref, sin_ref,
                   u_ref, q_ref, kv_ref, xq_ref, g_ref, qh_ref):
    xf = x_ref[...]
    ms = jnp.mean(xf * xf, axis=-1, keepdims=True)
    h = (xf * lax.rsqrt(ms + EPS) * gmix_ref[...]).astype(BF16)
    w = u_ref.shape[1]
    proj = lambda c: _dot(h, wa_ref[:, c * w:(c + 1) * w])
    u_ref[...] = proj(0)

    cos = jnp.tile(cos_ref[...], (1, w // LANES))
    sin = jnp.tile(sin_ref[...], (1, w // LANES))
    lane = lax.broadcasted_iota(jnp.int32, (xf.shape[0], w), 1)
    first_half = (lane % MOBA_HEAD_DIM) < (MOBA_HEAD_DIM // 2)

    def qk_norm_rope(raw, g):
        ss = _dot((raw * raw).astype(BF16), e64_ref[...])
        n = raw * lax.rsqrt(ss * (1.0 / MOBA_HEAD_DIM) + EPS) * g
        rot = jnp.where(first_half,
                        pltpu.roll(n, w - MOBA_HEAD_DIM // 2, 1),
                        pltpu.roll(n, MOBA_HEAD_DIM // 2, 1))
        return n * cos + rot * sin

    q = qk_norm_rope(proj(1), gq_ref[...]) * (MOBA_HEAD_DIM ** -0.5)
    q_ref[...] = q.astype(BF16)
    in_a = lax.broadcasted_iota(jnp.int32, (xf.shape[0], LANES), 1) < MOBA_HEAD_DIM
    for p in range(w // LANES):
        pair = q[:, p * LANES:(p + 1) * LANES].astype(BF16).astype(F32)
        qh_ref[2 * p] = jnp.where(in_a, pair, 0.0)
        qh_ref[2 * p + 1] = jnp.where(in_a, 0.0, pair)
    k = qk_norm_rope(proj(2), gk_ref[...]).astype(BF16)
    v = proj(3).astype(BF16)
    for p in range(w // LANES):
        kv_ref[:, (2 * p) * LANES:(2 * p + 1) * LANES] = k[:, p * LANES:(p + 1) * LANES]
        kv_ref[:, (2 * p + 1) * LANES:(2 * p + 2) * LANES] = v[:, p * LANES:(p + 1) * LANES]

    xq = proj(4)
    hd = w // X_HEADS
    for c in range(X_HEADS):
        chunk = xq[:, c * hd:(c + 1) * hd]
        cms = jnp.mean(chunk * chunk, axis=-1, keepdims=True)
        xq_ref[:, c * hd:(c + 1) * hd] = (chunk * lax.rsqrt(cms + EPS) * gcq_ref[...]).astype(BF16)

    d = xf.shape[1]
    for c in range(N_BRANCH):
        z = _dot(h, wg_ref[:, c * d:(c + 1) * d])
        g_ref[:, c * d:(c + 1) * d] = _sigmoid(z).astype(BF16)


def _in_proj(xt, g_mix, w_in, g_q, g_k, g_cq, seq_len, tm=512):
    t, d = xt.shape
    tm = min(tm, seq_len)
    w = d // 2
    wa = w_in[:, :5 * w].astype(BF16)
    wg = w_in[:, 5 * w:].astype(BF16)
    heads = w // MOBA_HEAD_DIM
    e64 = jnp.kron(jnp.eye(heads, dtype=F32), jnp.ones((MOBA_HEAD_DIM, MOBA_HEAD_DIM), F32)).astype(BF16)
    half = MOBA_HEAD_DIM // 2
    inv = ROPE_THETA ** (-jnp.arange(half, dtype=F32) / half)
    ang = jnp.arange(seq_len, dtype=F32)[:, None] * inv[None, :]
    cos = jnp.tile(jnp.cos(ang), (1, LANES // half))
    sin = jnp.tile(jnp.concatenate([-jnp.sin(ang), jnp.sin(ang)], axis=1), (1, LANES // MOBA_HEAD_DIM))
    nt = seq_len // tm
    row = lambda i: (i, 0)
    const = lambda i: (0, 0)
    out_w = jax.ShapeDtypeStruct((t, w), BF16)
    return pl.pallas_call(
        _inproj_kernel,
        grid=(t // tm,),
        in_specs=[
            pl.BlockSpec((tm, d), row),
            pl.BlockSpec((1, d), const),
            pl.BlockSpec((d, 5 * w), const, pipeline_mode=pl.Buffered(1)),
            pl.BlockSpec((d, N_BRANCH * d), const, pipeline_mode=pl.Buffered(1)),
            pl.BlockSpec((w, w), const, pipeline_mode=pl.Buffered(1)),
            pl.BlockSpec((1, w), const),
            pl.BlockSpec((1, w), const),
            pl.BlockSpec((1, w // X_HEADS), const),
            pl.BlockSpec((tm, LANES), lambda i: (i % nt, 0)),
            pl.BlockSpec((tm, LANES), lambda i: (i % nt, 0)),
        ],
        out_specs=[pl.BlockSpec((tm, w), row), pl.BlockSpec((tm, w), row), pl.BlockSpec((tm, 2 * w), row),
                   pl.BlockSpec((tm, w), row), pl.BlockSpec((tm, N_BRANCH * d), row),
                   pl.BlockSpec((heads, tm, LANES), lambda i: (0, i, 0))],
        out_shape=[jax.ShapeDtypeStruct((t, w), F32), out_w, jax.ShapeDtypeStruct((t, 2 * w), BF16), out_w,
                   jax.ShapeDtypeStruct((t, N_BRANCH * d), BF16),
                   jax.ShapeDtypeStruct((heads, t, LANES), F32)],
        compiler_params=_params("parallel"),
        name="in_proj",
    )(xt, g_mix.reshape(1, d), wa, wg, e64,
      jnp.tile(g_q, heads).reshape(1, w), jnp.tile(g_k, heads).reshape(1, w), g_cq.reshape(1, -1), cos, sin)


def _s5_matrices(lam_re, lam_im, log_dt, b_re, b_im, c_re, c_im, n_chunks):
    hp = lax.Precision.HIGHEST
    c = S5_CHUNK
    dt = jnp.exp(log_dt)[:, None]
    mag = jnp.exp(lam_re * dt)
    ar = mag * jnp.cos(lam_im * dt)
    ai = mag * jnp.sin(lam_im * dt)
    nr = ar - 1.0
    den = lam_re * lam_re + lam_im * lam_im
    cr = (nr * lam_re + ai * lam_im) / den
    ci = (ai * lam_re - nr * lam_im) / den
    bbr = cr[..., None] * b_re - ci[..., None] * b_im
    bbi = cr[..., None] * b_im + ci[..., None] * b_re

    def power(n):
        nf = n.astype(F32)[None, :, None]
        m = jnp.exp((lam_re * dt)[:, None, :] * nf)
        th = (lam_im * dt)[:, None, :] * nf
        return m * jnp.cos(th), m * jnp.sin(th)

    pr, pi = power(jnp.arange(c + 1))
    kbr = pr[..., None] * bbr[:, None] - pi[..., None] * bbi[:, None]
    kbi = pr[..., None] * bbi[:, None] + pi[..., None] * bbr[:, None]
    kk = jnp.einsum('ghp,gtpc->gthc', jnp.concatenate([c_re, -c_im], axis=-1),
                    jnp.concatenate([kbr, kbi], axis=2), precision=hp)
    tq = jnp.arange(c)
    g = kk.shape[0]
    rev = c - 1 - tq
    w_in = jnp.concatenate([kbr[:, rev].transpose(0, 1, 3, 2), kbi[:, rev].transpose(0, 1, 3, 2)], axis=-1)
    w_in = w_in.reshape(g, c * SSM_GROUP, 2 * SSM_STATE)
    prn, pin = pr[:, 1:], pi[:, 1:]
    wo_r = c_re[:, None] * prn[:, :, None, :] - c_im[:, None] * pin[:, :, None, :]
    wo_i = -c_re[:, None] * pin[:, :, None, :] - c_im[:, None] * prn[:, :, None, :]
    w_out = jnp.concatenate([wo_r, wo_i], axis=-1).transpose(0, 3, 1, 2).reshape(g, 2 * SSM_STATE, c * SSM_GROUP)
    n_steps = max(1, int(math.ceil(math.log2(n_chunks))))
    qr, qi = power(c * (2 ** jnp.arange(n_steps)))
    pa = jnp.concatenate([qr, qr], axis=-1)
    pb = jnp.concatenate([-qi, qi], axis=-1)
    return kk.astype(BF16), w_in.astype(BF16), w_out.astype(BF16), pa, pb


S5_SET = LANES // SSM_GROUP
S5_ROWS = 256


def _s5_kernel(u_ref, toep_ref, winc_ref, woutc_ref, pa_ref, pb_ref, y_ref, s_ref, win_ref, wout_ref):
    c = S5_CHUNK
    nc = u_ref.shape[0] // c
    rb = min(S5_ROWS, nc)
    p2 = 2 * SSM_STATE

    @pl.when(pl.program_id(1) == 0)
    def _():
        r_group = lax.broadcasted_iota(jnp.int32, (LANES, p2), 0) // SSM_GROUP
        l_group = lax.broadcasted_iota(jnp.int32, (p2, LANES), 1) // SSM_GROUP
        for t in range(c):
            for g in range(S5_SET):
                win_ref[0, t * LANES:(t + 1) * LANES, g * p2:(g + 1) * p2] = jnp.where(
                    r_group == g, winc_ref[0, t], jnp.zeros((LANES, p2), BF16))
                wout_ref[0, g * p2:(g + 1) * p2, t * LANES:(t + 1) * LANES] = jnp.where(
                    l_group == g, woutc_ref[0, t], jnp.zeros((p2, LANES), BF16))

    def chunk_rows(r0):
        return jnp.concatenate([u_ref[pl.ds(r0 * c + t, rb, stride=c), :] for t in range(c)], axis=1).astype(BF16)

    for blk in range(nc // rb):
        s_ref[blk * rb:(blk + 1) * rb, :] = _dot(chunk_rows(blk * rb), win_ref[0])
    row = lax.broadcasted_iota(jnp.int32, (nc, LANES), 0)
    n_steps = pa_ref.shape[1]
    for g in range(S5_SET):
        lanes = slice(g * LANES, (g + 1) * LANES)
        s = s_ref[:, lanes]
        for k in range(n_steps):
            sh = 1 << k
            if sh >= nc:
                break
            prev = jnp.where(row >= sh, pltpu.roll(s, sh, 0), 0.0)
            s = s + pa_ref[0, k:k + 1, lanes] * prev + pb_ref[0, k:k + 1, lanes] * pltpu.roll(prev, SSM_STATE, 1)
        s_ref[:, lanes] = jnp.where(row >= 1, pltpu.roll(s, 1, 0), 0.0)
    for blk in range(nc // rb):
        rows = slice(blk * rb, (blk + 1) * rb)
        a = chunk_rows(blk * rb)
        carried = _dot(s_ref[rows, :].astype(BF16), wout_ref[0])
        for t in range(c):
            y = _dot(a[:, :(t + 1) * LANES], toep_ref[0, (c - 1 - t) * LANES:, :])
            y_ref[pl.ds(blk * rb * c + t, rb, stride=c), :] = y + carried[:, t * LANES:(t + 1) * LANES]


def _s5_block_diag(mats):
    kk, w_in, w_out, pa, pb = mats
    g, c, h, p2 = kk.shape[0], S5_CHUNK, SSM_GROUP, 2 * SSM_STATE
    ns = g // S5_SET
    eye = jnp.eye(S5_SET, dtype=kk.dtype)
    lag_bd = jnp.einsum('sglhc,gk->slgckh', kk.reshape(ns, S5_SET, c + 1, h, h), eye).reshape(ns, c + 1, LANES, LANES)
    toep_bd = lag_bd[:, c - 1::-1].reshape(ns, c * LANES, LANES)

    w_in_c = w_in.reshape(ns, S5_SET, c, h, p2).transpose(0, 2, 1, 3, 4).reshape(ns, c, LANES, p2)
    w_out_c = w_out.reshape(ns, S5_SET, p2, c, h).transpose(0, 3, 2, 1, 4).reshape(ns, c, p2, LANES)
    k = pa.shape[1]
    lanes_of = lambda x: x.reshape(ns, S5_SET, k, p2).transpose(0, 2, 1, 3).reshape(ns, k, S5_SET * p2)
    return toep_bd, w_in_c, w_out_c, lanes_of(pa), lanes_of(pb)


def _s5(u, mats, bsz, seq_len):
    toep, w_in, w_out, pa, pb = _s5_block_diag(mats)
    ns = toep.shape[0]
    nc = seq_len // S5_CHUNK
    per_set = lambda s, b: (s, 0, 0)
    once = dict(pipeline_mode=pl.Buffered(1))
    io_spec = pl.BlockSpec((seq_len, LANES), lambda s, b: (b, s), **once)
    return pl.pallas_call(
        _s5_kernel,
        grid=(ns, bsz),
        in_specs=[
            io_spec,
            pl.BlockSpec((1,) + toep.shape[1:], per_set, **once),
            pl.BlockSpec((1,) + w_in.shape[1:], lambda s, b: (s, 0, 0, 0)),
            pl.BlockSpec((1,) + w_out.shape[1:], lambda s, b: (s, 0, 0, 0)),
            pl.BlockSpec((1,) + pa.shape[1:], per_set),
            pl.BlockSpec((1,) + pb.shape[1:], per_set),
        ],
        out_specs=io_spec,
        out_shape=jax.ShapeDtypeStruct(u.shape, F32),
        scratch_shapes=[pltpu.VMEM((nc, S5_SET * 2 * SSM_STATE), F32),
                        pltpu.VMEM((1, S5_CHUNK * LANES, S5_SET * 2 * SSM_STATE), BF16),
                        pltpu.VMEM((1, S5_SET * 2 * SSM_STATE, S5_CHUNK * LANES), BF16)],
        compiler_params=_params("arbitrary", "arbitrary"),
        name="s5_scan",
    )(u, toep, w_in, w_out, pa, pb)


def _moba_select_kernel(q_ref, k_ref, tri_ref, sel_ref, cnt_out_ref, km_ref, cnt_ref):
    h = pl.program_id(1)
    t = pl.program_id(2)
    ts = q_ref.shape[0]
    nbk = MOBA_MAX_BLOCKS

    @pl.when(t == 0)
    def _():
        kk = k_ref[...].astype(F32)
        nb = kk.shape[0] // MOBA_BLOCK
        km = jnp.sum(kk.reshape(nb, MOBA_BLOCK, LANES), axis=1) * (1.0 / MOBA_BLOCK)
        if nb < nbk:
            km = jnp.concatenate([km, jnp.zeros((nbk - nb, LANES), F32)], axis=0)
        lane = lax.broadcasted_iota(jnp.int32, (nbk, LANES), 1)
        km_ref[...] = jnp.where((lane // MOBA_HEAD_DIM) == (h % 2), km, 0.0).astype(BF16)
        cnt_ref[...] = jnp.zeros(cnt_ref.shape, F32)

    gate = _dot_nt(km_ref[...], q_ref[...])
    blk = lax.broadcasted_iota(jnp.int32, gate.shape, 0)
    qblk = (t * ts + lax.broadcasted_iota(jnp.int32, gate.shape, 1)) // MOBA_BLOCK
    g = jnp.where(blk < qblk, gate, -jnp.inf)
    selected = jnp.zeros(gate.shape, jnp.bool_)
    picks = []
    for _ in range(MOBA_TOPK):
        mx = jnp.max(g, axis=0, keepdims=True)
        idx = jnp.min(jnp.where(g == mx, blk, nbk), axis=0, keepdims=True)
        hit = blk == idx
        ok = (idx[0:1] < qblk[0:1]) & (mx > -jnp.inf)
        picks.append((hit, idx, ok))
        selected = selected | (hit & ok)
        g = jnp.where(hit, -jnp.inf, g)
    sel01 = jnp.where(selected, 1.0, 0.0)
    chunks = [sel01[:, c * LANES:(c + 1) * LANES] for c in range(ts // LANES)]
    within = _dot(jnp.concatenate(chunks, axis=0).astype(BF16), tri_ref[...])
    base = cnt_ref[...]
    pieces = []
    for c, chunk in enumerate(chunks):
        pieces.append(within[c * nbk:(c + 1) * nbk] + base)
        base = base + jnp.sum(chunk, axis=1, keepdims=True)
    prefix = jnp.concatenate(pieces, axis=1)
    rows = [jnp.where(ok, idx, -1) for (_, idx, ok) in picks]
    rows += [jnp.sum(jnp.where(hit, prefix, 0.0), axis=0, keepdims=True).astype(jnp.int32) for (hit, _, _) in picks]
    rows += [jnp.zeros((1, ts), jnp.int32)] * (8 - 2 * MOBA_TOPK)
    sel_ref[0, 0] = jnp.concatenate(rows, axis=0)
    cnt_ref[...] = base
    cnt_out_ref[0, 0] = base


def _moba_select(q2, kv, bsz, seq_len, ts=1024):
    ts = min(ts, seq_len)
    heads = q2.shape[1] // MOBA_HEAD_DIM
    nt = seq_len // ts
    tri = (jnp.arange(LANES)[:, None] < jnp.arange(LANES)[None, :]).astype(BF16)
    return pl.pallas_call(
        _moba_select_kernel,
        grid=(bsz, heads, nt),
        in_specs=[
            pl.BlockSpec((ts, LANES), lambda b, h, t: (b * nt + t, h // 2)),
            pl.BlockSpec((seq_len, LANES), lambda b, h, t: (b, 2 * (h // 2))),
            pl.BlockSpec((LANES, LANES), lambda b, h, t: (0, 0)),
        ],
        out_specs=[pl.BlockSpec((1, 1, 8, ts), lambda b, h, t: (b, h, 0, t)),
                   pl.BlockSpec((1, 1, MOBA_MAX_BLOCKS, LANES), lambda b, h, t: (b, h, 0, 0))],
        out_shape=[jax.ShapeDtypeStruct((bsz, heads, 8, seq_len), jnp.int32),
                   jax.ShapeDtypeStruct((bsz, heads, MOBA_MAX_BLOCKS, LANES), F32)],
        scratch_shapes=[pltpu.VMEM((MOBA_MAX_BLOCKS, LANES), BF16), pltpu.VMEM((MOBA_MAX_BLOCKS, LANES), F32)],
        compiler_params=_params("parallel", "arbitrary", "arbitrary"),
        name="moba_select",
    )(q2, kv, tri)


MOBA_DIAG_BLOCKS = 2


def _moba_diag_kernel(q_ref, kv_ref, o_ref, lse_ref):
    lane = lax.broadcasted_iota(jnp.int32, (MOBA_BLOCK, LANES), 1)
    is_a = lane < MOBA_HEAD_DIM
    r = lax.broadcasted_iota(jnp.int32, (MOBA_BLOCK, MOBA_BLOCK), 0)
    c = lax.broadcasted_iota(jnp.int32, (MOBA_BLOCK, MOBA_BLOCK), 1)
    for u in range(MOBA_DIAG_BLOCKS):
        rows = pl.ds(u * MOBA_BLOCK, MOBA_BLOCK)
        q = q_ref[rows, :]
        kb = kv_ref[rows, :LANES]
        vb = kv_ref[rows, LANES:]
        pvs, ms = [], []
        for own in (is_a, jnp.logical_not(is_a)):
            s = jnp.where(c <= r, _dot_nt(jnp.where(own, q, jnp.zeros_like(q)), kb), NEG_BIG)
            m = jnp.max(s, axis=1, keepdims=True)
            p = jnp.exp(s - m).astype(BF16)
            pvs.append(_dot(p, jnp.where(own, vb, 1.0)))
            ms.append(m)
        num = jnp.where(is_a, pvs[0], pvs[1])
        den = pltpu.roll(jnp.where(is_a, pvs[1], pvs[0]), MOBA_HEAD_DIM, 1)
        o_ref[rows, :] = num / den
        lse_ref[rows, :] = jnp.where(is_a, ms[0], ms[1]) + jnp.log(den)


def _moba_diag(q2, kv, bsz, seq_len):
    npair = q2.shape[1] // LANES
    rows = MOBA_BLOCK * MOBA_DIAG_BLOCKS
    nb = seq_len // rows
    spec = pl.BlockSpec((rows, LANES), lambda b, p, i: (b * nb + i, p))
    out = jax.ShapeDtypeStruct(q2.shape, F32)
    return pl.pallas_call(
        _moba_diag_kernel,
        grid=(bsz, npair, nb),
        in_specs=[spec, pl.BlockSpec((rows, 2 * LANES), lambda b, p, i: (b * nb + i, p))],
        out_specs=[spec, spec],
        out_shape=[out, out],
        compiler_params=_params("parallel", "parallel", "parallel"),
        name="moba_diag",
    )(q2, kv)


MOBA_TILE = 256
MOBA_TILES_PER_STEP = 8


def _moba_grouped_kernel(tile_row_ref, tile_pair_ref, tile_half_ref, tile_real_ref, qd_ref, *refs):
    n = MOBA_TILES_PER_STEP
    kv_refs, o_ref = refs[:n], refs[n + 1]
    i = pl.program_id(0)
    lane = lax.broadcasted_iota(jnp.int32, (MOBA_TILE, LANES), 1)
    kv_lane = lax.broadcasted_iota(jnp.int32, (MOBA_BLOCK, LANES), 1)

    @pl.when(tile_real_ref[i * n] > 0)
    def _():
        for u in range(n):
            tile = i * n + u
            rows = pl.ds(u * MOBA_TILE, MOBA_TILE)
            own = (lane // MOBA_HEAD_DIM) == tile_half_ref[tile]
            s = _dot_nt(qd_ref[rows, :].astype(BF16), kv_refs[u][:, :LANES])
            m = jnp.max(s, axis=1, keepdims=True)
            p = jnp.exp(s - m).astype(BF16)
            vb = jnp.where((kv_lane // MOBA_HEAD_DIM) == tile_half_ref[tile], kv_refs[u][:, LANES:], 1.0)
            pv = _dot(p, vb)
            row_sum = pltpu.roll(pv, MOBA_HEAD_DIM, 1)
            part = jnp.where(own, pv / row_sum, m + jnp.log(pv))
            o_ref[rows, :] = jnp.where(tile_real_ref[tile] > 0, part, NEG_BIG)

    @pl.when(tile_real_ref[i * n] == 0)
    def _():
        o_ref[...] = jnp.full(o_ref.shape, NEG_BIG, F32)


def _moba_grouped(qd, kv, tile_row, tile_pair, tile_half, tile_real, run_after):
    n = MOBA_TILES_PER_STEP
    n_tiles = qd.shape[0] // MOBA_TILE

    def kv_spec(u):
        return pl.BlockSpec((MOBA_BLOCK, 2 * LANES), lambda i, tr, tp, th, tl: (tr[i * n + u], tp[i * n + u]))

    grid_spec = pltpu.PrefetchScalarGridSpec(
        num_scalar_prefetch=4,
        grid=(n_tiles // n,),
        in_specs=[pl.BlockSpec((n * MOBA_TILE, LANES), lambda i, tr, tp, th, tl: (i, 0))]
        + [kv_spec(u) for u in range(n)]
        + [pl.BlockSpec((8, LANES), lambda i, tr, tp, th, tl: (0, 0))],
        out_specs=pl.BlockSpec((n * MOBA_TILE, LANES), lambda i, tr, tp, th, tl: (i, 0)),
    )
    return pl.pallas_call(
        _moba_grouped_kernel,
        grid_spec=grid_spec,
        out_shape=jax.ShapeDtypeStruct(qd.shape, F32),
        compiler_params=_params("arbitrary"),
        name="moba_grouped",
    )(tile_row, tile_pair, tile_half, tile_real, qd, *([kv] * n), run_after)


def _moba_combine_kernel(od_ref, lsed_ref, *refs):
    g_refs, o_ref = refs[:-1], refs[-1]
    lane = lax.broadcasted_iota(jnp.int32, od_ref.shape, 1)
    is_a = lane < MOBA_HEAD_DIM
    parts = [(od_ref[...], lsed_ref[...])]
    for s in range(MOBA_TOPK):
        xa = g_refs[s][0, 0, 0]
        xb = g_refs[MOBA_TOPK + s][0, 0, 0]
        o = jnp.where(is_a, xa, xb)
        lse = pltpu.roll(jnp.where(is_a, xb, xa), MOBA_HEAD_DIM, 1)
        parts.append((o, lse))
    m = parts[0][1]
    for _, lse in parts[1:]:
        m = jnp.maximum(m, lse)
    num = jnp.zeros(od_ref.shape, F32)
    den = jnp.zeros(od_ref.shape, F32)
    for o, lse in parts:
        w = jnp.exp(lse - m)
        num = num + w * o
        den = den + w
    o_ref[...] = (num / den).astype(BF16)


def _moba_combine(od, lsed, gath, bsz, seq_len, tm=1024):
    tm = min(tm, seq_len)
    npair = od.shape[1] // LANES
    nt = seq_len // tm
    spec = pl.BlockSpec((tm, LANES), lambda b, p, i: (b * nt + i, p))

    def g_spec(e, s):
        return pl.BlockSpec((1, 1, 1, tm, LANES), lambda b, p, i: (b, 2 * p + e, s, i, 0))

    return pl.pallas_call(
        _moba_combine_kernel,
        grid=(bsz, npair, nt),
        in_specs=[spec, spec] + [g_spec(e, s) for e in range(2) for s in range(MOBA_TOPK)],
        out_specs=spec,
        out_shape=jax.ShapeDtypeStruct(od.shape, BF16),
        compiler_params=_params("parallel", "parallel", "parallel"),
        name="moba_combine",
    )(od, lsed, *([gath] * (2 * MOBA_TOPK)))


SC_WINDOW = 128
SC_CORES = 2
SC_SUBCORES = 16


def _sc_mesh():
    return plsc.VectorSubcoreMesh(core_axis_name="core", subcore_axis_name="subcore")


def _sc_gather_rows(table, idx):
    n = idx.shape[0]
    d = table.shape[1]
    window = SC_WINDOW
    assert n % (window * SC_CORES * SC_SUBCORES) == 0
    per_core = n // window // SC_CORES

    @functools.partial(pl.kernel, out_type=jax.ShapeDtypeStruct((n, d), table.dtype), mesh=_sc_mesh())
    def gather_kernel(x_hbm, i_hbm, o_hbm):
        base = lax.axis_index("core") * per_core

        def body(i_vmem, o_vmem):
            pltpu.sync_copy(x_hbm.at[i_vmem.at[0]], o_vmem)

        pltpu.emit_pipeline(
            body,
            grid=(per_core,),
            in_specs=[pl.BlockSpec((1, window), index_map=lambda i: (0, base + i))],
            out_specs=[pl.BlockSpec((window, d), index_map=lambda i: (base + i, 0))],
            core_axis_name="subcore",
            dimension_semantics=(pltpu.PARALLEL,),
            trace_scopes=False,
        )(i_hbm, o_hbm)

    return gather_kernel(table, idx.reshape(1, n))


def _sc_scatter_rows(rows, dest, n_out, repeat):
    n_src, d = rows.shape
    n = dest.shape[0]
    assert n == repeat * n_src
    window = SC_WINDOW
    assert n % (window * SC_CORES * SC_SUBCORES) == 0
    per_core = n // window // SC_CORES
    src_windows = n_src // window

    @functools.partial(pl.kernel, out_type=jax.ShapeDtypeStruct((n_out, d), rows.dtype), mesh=_sc_mesh())
    def scatter_kernel(x_hbm, i_hbm, o_hbm):
        base = lax.axis_index("core") * per_core

        def body(x_vmem, i_vmem):
            pltpu.sync_copy(x_vmem, o_hbm.at[i_vmem.at[0]])

        pltpu.emit_pipeline(
            body,
            grid=(per_core,),
            in_specs=[pl.BlockSpec((window, d), index_map=lambda i: ((base + i) % src_windows, 0)),
                      pl.BlockSpec((1, window), index_map=lambda i: (0, base + i))],
            out_specs=[],
            core_axis_name="subcore",
            dimension_semantics=(pltpu.PARALLEL,),
            trace_scopes=False,
        )(x_hbm, i_hbm)

    return scatter_kernel(rows, dest.reshape(1, n))


def _moba_dispatch(q2, qh, kv, bsz, seq_len):
    heads = q2.shape[1] // MOBA_HEAD_DIM
    nbk = MOBA_MAX_BLOCKS
    sel, cnt = _moba_select(q2, kv, bsz, seq_len)
    counts = cnt[..., 0].astype(jnp.int32).reshape(bsz * heads * nbk)
    pcounts = ((counts + MOBA_TILE - 1) // MOBA_TILE) * MOBA_TILE
    pends = jnp.cumsum(pcounts)
    pstarts = (pends - pcounts).reshape(bsz, heads, 1, 1, nbk)
    n_items = bsz * heads * seq_len * MOBA_TOPK
    step_rows = MOBA_TILE * MOBA_TILES_PER_STEP
    n_rows = -(-(n_items + bsz * heads * nbk * MOBA_TILE) // step_rows) * step_rows
    n_null = step_rows
    idx = sel[:, :, 0:MOBA_TOPK, :]
    rank = sel[:, :, MOBA_TOPK:2 * MOBA_TOPK, :]
    start = jnp.sum(jnp.where(idx[..., None] == jnp.arange(nbk), pstarts, 0), axis=-1)
    null_row = n_rows + jnp.arange(seq_len, dtype=jnp.int32) % n_null
    dest = jnp.where(idx >= 0, start + rank, null_row)
    n_tiles = (n_rows + n_null) // MOBA_TILE
    n_groups = bsz * heads * nbk
    tile_start = jnp.arange(n_tiles, dtype=jnp.int32) * MOBA_TILE
    tile_g = jnp.minimum(jnp.sum((pends[None, :] <= tile_start[:, None]).astype(jnp.int32), axis=1), n_groups - 1)
    tile_real = (tile_start < pends[-1]).astype(jnp.int32)
    tile_head = (tile_g // nbk) % heads
    tile_row = (tile_g // (heads * nbk)) * (seq_len // MOBA_BLOCK) + jnp.minimum(tile_g % nbk, seq_len // MOBA_BLOCK - 1)
    dest_by_slot = dest.transpose(2, 1, 0, 3).reshape(-1)
    qd = _sc_scatter_rows(qh.reshape(-1, LANES), dest_by_slot, n_rows + n_null, MOBA_TOPK)
    return qd, dest, (tile_row, tile_head // 2, tile_head % 2, tile_real)


def _moba_finish(qd, dest, tiles, od, lsed, kv, bsz, seq_len):
    heads = dest.shape[1]
    part = _moba_grouped(qd, kv, *tiles, lsed)
    gath = _sc_gather_rows(part, dest.reshape(-1)).reshape(bsz, heads, MOBA_TOPK, seq_len, LANES)
    return _moba_combine(od, lsed, gath, bsz, seq_len)


def _mem_kv_kernel(mem_ref, gmem_ref, w_ref, gck_ref, k_ref, v_ref):
    xf = mem_ref[0]
    ms = jnp.mean(xf * xf, axis=-1, keepdims=True)
    h = (xf * lax.rsqrt(ms + EPS) * gmem_ref[...]).astype(BF16)
    kv = _dot(h, w_ref[...])
    w = k_ref.shape[2]
    hd = w // X_HEADS
    for c in range(X_HEADS):
        chunk = kv[:, c * hd:(c + 1) * hd]
        cms = jnp.mean(chunk * chunk, axis=-1, keepdims=True)
        k_ref[0, :, c * hd:(c + 1) * hd] = (chunk * lax.rsqrt(cms + EPS) * gck_ref[...]).astype(BF16)
    v_ref[0] = kv[:, w:].astype(BF16)


def _mem_kv(mem, g_mem, w_kv_mem, g_ck):
    bsz, m, d = mem.shape
    w = w_kv_mem.shape[1] // 2
    const = lambda b: (0, 0)
    out = jax.ShapeDtypeStruct((bsz, m, w), BF16)
    return pl.pallas_call(
        _mem_kv_kernel,
        grid=(bsz,),
        in_specs=[pl.BlockSpec((1, m, d), lambda b: (b, 0, 0)), pl.BlockSpec((1, d), const),
                  pl.BlockSpec((d, 2 * w), const), pl.BlockSpec((1, w // X_HEADS), const)],
        out_specs=[pl.BlockSpec((1, m, w), lambda b: (b, 0, 0))] * 2,
        out_shape=[out, out],
        compiler_params=_params("parallel"),
        name="mem_kv",
    )(mem, g_mem.reshape(1, d), w_kv_mem.astype(BF16), g_ck.reshape(1, -1))


def _merge_kernel(x_ref, ys_ref, u_ref, dskip_ref, om_ref, xq_ref, kc_ref, vc_ref, g_ref,
                  wglu_ref, wmo_ref, wco_ref, wout_ref, gffn_ref, wr_ref, br_ref,
                  x1_ref, h2_ref, logit_ref):
    d = x_ref.shape[1]
    y = ys_ref[...].astype(F32) + dskip_ref[...] * u_ref[...].astype(F32)
    ge = 0.5 * y * (1.0 + jnp.tanh(math.sqrt(2.0 / math.pi) * (y + 0.044715 * (y * y * y))))
    z = _dot(ge.astype(BF16), wglu_ref[...])
    merged = g_ref[:, 0:d].astype(F32) * (z[:, :d] * _sigmoid(z[:, d:]))
    merged = merged + g_ref[:, d:2 * d].astype(F32) * _dot(om_ref[...], wmo_ref[...])
    w = xq_ref.shape[1]
    hd = w // X_HEADS
    heads = []
    for c in range(X_HEADS):
        s = _dot_nt(xq_ref[:, c * hd:(c + 1) * hd], kc_ref[0, :, c * hd:(c + 1) * hd]) * (hd ** -0.5)
        p = jnp.exp(s - jnp.max(s, axis=1, keepdims=True))
        p = p / jnp.sum(p, axis=1, keepdims=True)
        heads.append(_dot(p.astype(BF16), vc_ref[0, :, c * hd:(c + 1) * hd]))
    oc = jnp.concatenate(heads, axis=1).astype(BF16)
    merged = merged + g_ref[:, 2 * d:3 * d].astype(F32) * _dot(oc, wco_ref[...])
    x1 = x_ref[...] + _dot(merged.astype(BF16), wout_ref[...])
    x1_ref[...] = x1
    ms = jnp.mean(x1 * x1, axis=-1, keepdims=True)
    h2 = (x1 * lax.rsqrt(ms + EPS) * gffn_ref[...]).astype(BF16)
    packed = _pack_bf16_pairs(h2.astype(F32))
    for j in range(h2_ref.shape[0]):
        h2_ref[j] = packed[:, j * LANES:(j + 1) * LANES]
    logit_ref[...] = _dot_nt(wr_ref[...], h2) + br_ref[...]


def _merge(xt, ys, u, d_skip, om, xqn, kc, vc, gates, w_glu, w_mo, w_co, w_out, g_ffn, w_router, b_router,
           seq_len, tm=512):
    t, d = xt.shape
    tm = min(tm, seq_len)
    w = d // 2
    m = kc.shape[1]
    ne = w_router.shape[1]
    nt = seq_len // tm
    row = lambda i: (i, 0)
    const = lambda i: (0, 0)
    per_b = lambda i: (i // nt, 0, 0)
    once = dict(pipeline_mode=pl.Buffered(1))
    return pl.pallas_call(
        _merge_kernel,
        grid=(t // tm,),
        in_specs=[
            pl.BlockSpec((tm, d), row), pl.BlockSpec((tm, w), row), pl.BlockSpec((tm, w), row),
            pl.BlockSpec((1, w), const), pl.BlockSpec((tm, w), row), pl.BlockSpec((tm, w), row),
            pl.BlockSpec((1, m, w), per_b), pl.BlockSpec((1, m, w), per_b),
            pl.BlockSpec((tm, N_BRANCH * d), row),
            pl.BlockSpec((w, 2 * d), const, **once), pl.BlockSpec((w, d), const, **once),
            pl.BlockSpec((w, d), const, **once), pl.BlockSpec((d, d), const, **once), pl.BlockSpec((1, d), const),
            pl.BlockSpec((ne, d), const), pl.BlockSpec((ne, 1), const),
        ],
        out_specs=[pl.BlockSpec((tm, d), row), pl.BlockSpec((d // 2 // LANES, tm, LANES), lambda i: (0, i, 0)),
                   pl.BlockSpec((ne, tm), lambda i: (0, i))],
        out_shape=[jax.ShapeDtypeStruct((t, d), F32), jax.ShapeDtypeStruct((d // 2 // LANES, t, LANES), jnp.uint32),
                   jax.ShapeDtypeStruct((ne, t), F32)],
        compiler_params=_params("parallel"),
        name="merge",
    )(xt, ys, u, d_skip.reshape(1, w), om, xqn, kc, vc, gates,
      w_glu.astype(BF16), w_mo.astype(BF16), w_co.astype(BF16), w_out.astype(BF16),
      g_ffn.reshape(1, d), w_router.T.astype(BF16), b_router.reshape(ne, 1))


def _moe_kernel(blk_e_ref, blk_used_ref, xs_ref, wgu_ref, bgu_ref, wd_ref, bd_ref, y_ref, wgu_bf, wd_bf):
    i = pl.program_id(0)
    prev = blk_e_ref[jnp.maximum(i - 1, 0)]

    @pl.when((i == 0) | (blk_e_ref[i] != prev))
    def _():
        wgu_bf[...] = wgu_ref[0].astype(BF16)
        wd_bf[...] = wd_ref[0].astype(BF16)

    @pl.when(blk_used_ref[i] > 0)
    def _():
        de = wd_bf.shape[0]
        words = jnp.concatenate([xs_ref[j] for j in range(xs_ref.shape[0])], axis=1)
        xs = jnp.concatenate(_unpack_bf16_pairs(words), axis=1).astype(BF16)
        gu = _dot(xs, wgu_bf[...]) + bgu_ref[0]
        gate = jnp.minimum(gu[:, :de], SWIGLU_LIMIT)
        up = jnp.clip(gu[:, de:], -SWIGLU_LIMIT, SWIGLU_LIMIT)
        act = gate * _sigmoid(SWIGLU_ALPHA * gate) * (up + 1.0)
        y = _dot(act.astype(BF16), wd_bf[...]) + bd_ref[0]
        packed = _pack_bf16_pairs(y.astype(BF16).astype(F32))
        for j in range(y_ref.shape[0]):
            y_ref[j] = packed[:, j * LANES:(j + 1) * LANES]

    @pl.when(blk_used_ref[i] == 0)
    def _():
        y_ref[...] = jnp.zeros(y_ref.shape, y_ref.dtype)


def _moe_experts(xs, blk_e, blk_used, w_gu, b_gu, w_down, b_down):
    slabs, p, _ = xs.shape
    d = 2 * slabs * LANES
    ne, _, de2 = w_gu.shape
    de = de2 // 2
    nblk = p // EXPERT_ROWS
    row_spec = pl.BlockSpec((slabs, EXPERT_ROWS, LANES), lambda i, e, n: (0, i, 0))
    grid_spec = pltpu.PrefetchScalarGridSpec(
        num_scalar_prefetch=2,
        grid=(nblk,),
        in_specs=[
            row_spec,
            pl.BlockSpec((1, d, de2), lambda i, e, n: (e[i], 0, 0)),
            pl.BlockSpec((1, 1, de2), lambda i, e, n: (e[i], 0, 0)),
            pl.BlockSpec((1, de, d), lambda i, e, n: (e[i], 0, 0)),
            pl.BlockSpec((1, 1, d), lambda i, e, n: (e[i], 0, 0)),
        ],
        out_specs=row_spec,
        scratch_shapes=[pltpu.VMEM((d, de2), BF16), pltpu.VMEM((de, d), BF16)],
    )
    return pl.pallas_call(
        _moe_kernel,
        grid_spec=grid_spec,
        out_shape=jax.ShapeDtypeStruct(xs.shape, jnp.uint32),
        compiler_params=_params("arbitrary"),
        name="moe_experts",
    )(blk_e, blk_used, xs, w_gu, b_gu.reshape(ne, 1, de2), w_down, b_down.reshape(ne, 1, d))


def _router_kernel(lg_ref, tri_ref, e_ref, w_ref, r_ref, cnt_out_ref, cnt_ref):
    t = pl.program_id(0)
    ne, ts = lg_ref.shape

    @pl.when(t == 0)
    def _():
        cnt_ref[...] = jnp.zeros(cnt_ref.shape, F32)

    g = lg_ref[...]
    eid = lax.broadcasted_iota(jnp.int32, g.shape, 0)
    selected = jnp.zeros(g.shape, jnp.bool_)
    picks = []
    for _ in range(TOPK_EXPERTS):
        mx = jnp.max(g, axis=0, keepdims=True)
        idx = jnp.min(jnp.where(g == mx, eid, ne), axis=0, keepdims=True)
        hit = eid == idx
        picks.append((hit, idx, mx))
        selected = selected | hit
        g = jnp.where(hit, -jnp.inf, g)
    exps = [jnp.exp(mx - picks[0][2]) for (_, _, mx) in picks]
    total = sum(exps)
    sel01 = jnp.where(selected, 1.0, 0.0).astype(BF16)
    prefix = _dot(sel01, tri_ref[...]) + jnp.tile(cnt_ref[...], (1, ts // LANES))
    pad_i = [jnp.zeros((1, ts), jnp.int32)] * (8 - TOPK_EXPERTS)
    e_ref[...] = jnp.concatenate([idx for (_, idx, _) in picks] + pad_i, axis=0)
    w_ref[...] = jnp.concatenate([e / total for e in exps] + [jnp.zeros((1, ts), F32)] * (8 - TOPK_EXPERTS), axis=0)
    r_ref[...] = jnp.concatenate(
        [jnp.sum(jnp.where(hit, prefix, 0.0), axis=0, keepdims=True).astype(jnp.int32) for (hit, _, _) in picks]
        + pad_i, axis=0)
    cnt_ref[...] = cnt_ref[...] + _dot(sel01, jnp.ones((ts, LANES), BF16))
    cnt_out_ref[...] = cnt_ref[...]


def _router(logits_t, ts=1024):
    ne, t = logits_t.shape
    ts = min(ts, t)
    tri = (jnp.arange(ts)[:, None] < jnp.arange(ts)[None, :]).astype(BF16)
    rows = pl.BlockSpec((8, ts), lambda i: (0, i))
    return pl.pallas_call(
        _router_kernel,
        grid=(t // ts,),
        in_specs=[pl.BlockSpec((ne, ts), lambda i: (0, i)), pl.BlockSpec((ts, ts), lambda i: (0, 0))],
        out_specs=[rows, rows, rows, pl.BlockSpec((ne, LANES), lambda i: (0, 0))],
        out_shape=[jax.ShapeDtypeStruct((8, t), jnp.int32), jax.ShapeDtypeStruct((8, t), F32),
                   jax.ShapeDtypeStruct((8, t), jnp.int32), jax.ShapeDtypeStruct((ne, LANES), F32)],
        scratch_shapes=[pltpu.VMEM((ne, LANES), F32)],
        compiler_params=_params("arbitrary"),
        name="moe_router",
    )(logits_t, tri)


def _moe_mix_kernel(x1_ref, w_ref, pk_ref, o_ref):
    slabs = pk_ref.shape[0]
    wts = w_ref[...]
    lo = [jnp.zeros((x1_ref.shape[0], LANES), F32)] * slabs
    hi = [jnp.zeros((x1_ref.shape[0], LANES), F32)] * slabs
    for k in range(TOPK_EXPERTS):
        wk = wts[:, k:k + 1]
        for j in range(slabs):
            a, b = _unpack_bf16_pairs(pk_ref[j, k])
            lo[j] = lo[j] + wk * a
            hi[j] = hi[j] + wk * b
    o_ref[...] = x1_ref[...] + jnp.concatenate(lo + hi, axis=1)


def _moe_mix(x1, weights_tk, picked, tm=512):
    t, d = x1.shape
    slabs = picked.shape[0]
    return pl.pallas_call(
        _moe_mix_kernel,
        grid=(t // tm,),
        in_specs=[pl.BlockSpec((tm, d), lambda i: (i, 0)),
                  pl.BlockSpec((tm, TOPK_EXPERTS), lambda i: (i, 0)),
                  pl.BlockSpec((slabs, TOPK_EXPERTS, tm, LANES), lambda i: (0, 0, i, 0))],
        out_specs=pl.BlockSpec((tm, d), lambda i: (i, 0)),
        out_shape=jax.ShapeDtypeStruct((t, d), F32),
        compiler_params=_params("parallel"),
        name="moe_mix",
    )(x1, weights_tk, picked)


def _moe_ffn(x1, h2, logits_t, w_gu, b_gu, w_down, b_down):
    t, d = x1.shape
    tk = t * TOPK_EXPERTS
    e8, w8, r8, cnt = _router(logits_t)
    top_e, weights, rank = e8[:TOPK_EXPERTS], w8[:TOPK_EXPERTS], r8[:TOPK_EXPERTS]
    counts = cnt[:, 0].astype(jnp.int32)
    pcounts = ((counts + EXPERT_ROWS - 1) // EXPERT_ROWS) * EXPERT_ROWS
    pends = jnp.cumsum(pcounts)
    pstarts = pends - pcounts
    is_e = top_e[..., None] == jnp.arange(N_EXPERTS, dtype=jnp.int32)
    dest_by_k = jnp.sum(jnp.where(is_e, pstarts, 0), axis=-1) + rank
    nblk = -(-tk // EXPERT_ROWS) + N_EXPERTS
    blk_start = jnp.arange(nblk, dtype=jnp.int32) * EXPERT_ROWS
    blk_e = jnp.minimum(jnp.sum((pends[None, :] <= blk_start[:, None]).astype(jnp.int32), axis=1), N_EXPERTS - 1)
    blk_used = (blk_start < pends[-1]).astype(jnp.int32)
    slabs = d // 2 // LANES
    p = nblk * EXPERT_ROWS
    slab_off = jnp.arange(slabs, dtype=jnp.int32) * p
    dest_kst = (dest_by_k[:, None, :] + slab_off[None, :, None]).reshape(-1)
    dest_skt = (dest_by_k[None, :, :] + slab_off[:, None, None]).reshape(-1)
    xs = _sc_scatter_rows(h2.reshape(slabs * t, LANES), dest_kst, slabs * p, TOPK_EXPERTS)
    ys = _moe_experts(xs.reshape(slabs, p, LANES), blk_e, blk_used, w_gu, b_gu, w_down, b_down)
    picked = _sc_gather_rows(ys.reshape(slabs * p, LANES), dest_skt).reshape(slabs, TOPK_EXPERTS, t, LANES)
    return _moe_mix(x1, weights.T, picked)


def kernel(x, mem, g_mix, w_in, lam_re, lam_im, log_dt, b_re, b_im, c_re, c_im, d_skip, w_glu, g_q, g_k, w_moba_out, g_mem, w_kv_mem, g_cq, g_ck, w_cross_out, w_out, g_ffn, w_router, b_router, w_gu, b_gu, w_down, b_down):
    bsz, seq_len, d = x.shape
    xt = x.reshape(bsz * seq_len, d)
    for l in range(g_mix.shape[0]):
        u, q2, kv, xqn, gates, qh = _in_proj(xt, g_mix[l], w_in[l], g_q[l], g_k[l], g_cq[l], seq_len)
        qd, dest, tiles = _moba_dispatch(q2, qh, kv, bsz, seq_len)
        mats = _s5_matrices(lam_re[l], lam_im[l], log_dt[l], b_re[l], b_im[l], c_re[l], c_im[l],
                            seq_len // S5_CHUNK)
        ys = _s5(u, mats, bsz, seq_len)
        od, lsed = _moba_diag(q2, kv, bsz, seq_len)
        kc, vc = _mem_kv(mem, g_mem[l], w_kv_mem[l], g_ck[l])
        om = _moba_finish(qd, dest, tiles, od, lsed, kv, bsz, seq_len)
        x1, h2, logits = _merge(xt, ys, u, d_skip[l], om, xqn, kc, vc, gates, w_glu[l], w_moba_out[l],
                                w_cross_out[l], w_out[l], g_ffn[l], w_router[l], b_router[l], seq_len)
        xt = _moe_ffn(x1, h2, logits, w_gu[l], b_gu[l], w_down[l], b_down[l])
    return xt.reshape(bsz, seq_len, d)
```

```python
import functools
import math

import jax
import jax.numpy as jnp
from jax import lax
from jax.experimental import pallas as pl
from jax.experimental.pallas import tpu as pltpu
from jax.experimental.pallas import tpu_sc as plsc

F32 = jnp.float32
BF16 = jnp.bfloat16

EPS = 1e-6
N_BRANCH = 3
SSM_GROUP = 16
SSM_STATE = 64
S5_CHUNK = 16
MOBA_HEAD_DIM = 64
MOBA_BLOCK = 256
MOBA_TOPK = 3
MOBA_MAX_BLOCKS = 64
ROPE_THETA = 10000.0
X_HEADS = 4
N_EXPERTS = 32
TOPK_EXPERTS = 4
SWIGLU_LIMIT = 7.0
SWIGLU_ALPHA = 1.702
EXPERT_ROWS = 512
NEG_BIG = -1e30
LANES = 128
VMEM_LIMIT_BYTES = 56 * 1024 * 1024


def _params(*sem):
    return pltpu.CompilerParams(dimension_semantics=sem, vmem_limit_bytes=VMEM_LIMIT_BYTES)


def _sigmoid(x):
    return 1.0 / (1.0 + jnp.exp(-x))


def _dot(a, b):
    return jnp.dot(a, b, preferred_element_type=F32)


def _pack_bf16_pairs(x):
    n = x.shape[1] // 2
    lo = lax.bitcast_convert_type(x[:, :n], jnp.uint32) >> 16
    hi = lax.bitcast_convert_type(x[:, n:], jnp.uint32) & jnp.uint32(0xFFFF0000)
    return lo | hi


def _unpack_bf16_pairs(w):
    lo = lax.bitcast_convert_type(w << 16, F32)
    hi = lax.bitcast_convert_type(w & jnp.uint32(0xFFFF0000), F32)
    return lo, hi


def _dot_nt(a, b):
    return lax.dot_general(a, b, (((1,), (1,)), ((), ())), preferred_element_type=F32)


def _inproj_kernel(x_ref, gmix_ref, wa_ref, wg_ref, e64_ref, gq_ref, gk_ref, gcq_ref, cos_ref, sin_ref,
                   u_ref, q_ref, kv_ref, xq_ref, g_ref, qh_ref):
    xf = x_ref[...]
    ms = jnp.mean(xf * xf, axis=-1, keepdims=True)
    h = (xf * lax.rsqrt(ms + EPS) * gmix_ref[...]).astype(BF16)
    w = u_ref.shape[1]
    proj = lambda c: _dot(h, wa_ref[:, c * w:(c + 1) * w])
    u_ref[...] = proj(0)

    cos = jnp.tile(cos_ref[...], (1, w // LANES))
    sin = jnp.tile(sin_ref[...], (1, w // LANES))
    lane = lax.broadcasted_iota(jnp.int32, (xf.shape[0], w), 1)
    first_half = (lane % MOBA_HEAD_DIM) < (MOBA_HEAD_DIM // 2)

    def qk_norm_rope(raw, g):
        ss = _dot((raw * raw).astype(BF16), e64_ref[...])
        n = raw * lax.rsqrt(ss * (1.0 / MOBA_HEAD_DIM) + EPS) * g
        rot = jnp.where(first_half,
                        pltpu.roll(n, w - MOBA_HEAD_DIM // 2, 1),
                        pltpu.roll(n, MOBA_HEAD_DIM // 2, 1))
        return n * cos + rot * sin

    q = qk_norm_rope(proj(1), gq_ref[...]) * (MOBA_HEAD_DIM ** -0.5)
    q_ref[...] = q.astype(BF16)
    in_a = lax.broadcasted_iota(jnp.int32, (xf.shape[0], LANES), 1) < MOBA_HEAD_DIM
    for p in range(w // LANES):
        pair = q[:, p * LANES:(p + 1) * LANES].astype(BF16).astype(F32)
        qh_ref[2 * p] = jnp.where(in_a, pair, 0.0)
        qh_ref[2 * p + 1] = jnp.where(in_a, 0.0, pair)
    k = qk_norm_rope(proj(2), gk_ref[...]).astype(BF16)
    v = proj(3).astype(BF16)
    for p in range(w // LANES):
        kv_ref[:, (2 * p) * LANES:(2 * p + 1) * LANES] = k[:, p * LANES:(p + 1) * LANES]
        kv_ref[:, (2 * p + 1) * LANES:(2 * p + 2) * LANES] = v[:, p * LANES:(p + 1) * LANES]

    xq = proj(4)
    hd = w // X_HEADS
    for c in range(X_HEADS):
        chunk = xq[:, c * hd:(c + 1) * hd]
        cms = jnp.mean(chunk * chunk, axis=-1, keepdims=True)
        xq_ref[:, c * hd:(c + 1) * hd] = (chunk * lax.rsqrt(cms + EPS) * gcq_ref[...]).astype(BF16)

    d = xf.shape[1]
    for c in range(N_BRANCH):
        z = _dot(h, wg_ref[:, c * d:(c + 1) * d])
        g_ref[:, c * d:(c + 1) * d] = _sigmoid(z).astype(BF16)


def _in_proj(xt, g_mix, w_in, g_q, g_k, g_cq, seq_len, tm=512):
    t, d = xt.shape
    tm = min(tm, seq_len)
    w = d // 2
    wa = w_in[:, :5 * w].astype(BF16)
    wg = w_in[:, 5 * w:].astype(BF16)
    heads = w // MOBA_HEAD_DIM
    e64 = jnp.kron(jnp.eye(heads, dtype=F32), jnp.ones((MOBA_HEAD_DIM, MOBA_HEAD_DIM), F32)).astype(BF16)
    half = MOBA_HEAD_DIM // 2
    inv = ROPE_THETA ** (-jnp.arange(half, dtype=F32) / half)
    ang = jnp.arange(seq_len, dtype=F32)[:, None] * inv[None, :]
    cos = jnp.tile(jnp.cos(ang), (1, LANES // half))
    sin = jnp.tile(jnp.concatenate([-jnp.sin(ang), jnp.sin(ang)], axis=1), (1, LANES // MOBA_HEAD_DIM))
    nt = seq_len // tm
    row = lambda i: (i, 0)
    const = lambda i: (0, 0)
    out_w = jax.ShapeDtypeStruct((t, w), BF16)
    return pl.pallas_call(
        _inproj_kernel,
        grid=(t // tm,),
        in_specs=[
            pl.BlockSpec((tm, d), row),
            pl.BlockSpec((1, d), const),
            pl.BlockSpec((d, 5 * w), const, pipeline_mode=pl.Buffered(1)),
            pl.BlockSpec((d, N_BRANCH * d), const, pipeline_mode=pl.Buffered(1)),
            pl.BlockSpec((w, w), const, pipeline_mode=pl.Buffered(1)),
            pl.BlockSpec((1, w), const),
            pl.BlockSpec((1, w), const),
            pl.BlockSpec((1, w // X_HEADS), const),
            pl.BlockSpec((tm, LANES), lambda i: (i % nt, 0)),
            pl.BlockSpec((tm, LANES), lambda i: (i % nt, 0)),
        ],
        out_specs=[pl.BlockSpec((tm, w), row), pl.BlockSpec((tm, w), row), pl.BlockSpec((tm, 2 * w), row),
                   pl.BlockSpec((tm, w), row), pl.BlockSpec((tm, N_BRANCH * d), row),
                   pl.BlockSpec((heads, tm, LANES), lambda i: (0, i, 0))],
        out_shape=[jax.ShapeDtypeStruct((t, w), F32), out_w, jax.ShapeDtypeStruct((t, 2 * w), BF16), out_w,
                   jax.ShapeDtypeStruct((t, N_BRANCH * d), BF16),
                   jax.ShapeDtypeStruct((heads, t, LANES), F32)],
        compiler_params=_params("parallel"),
        name="in_proj",
    )(xt, g_mix.reshape(1, d), wa, wg, e64,
      jnp.tile(g_q, heads).reshape(1, w), jnp.tile(g_k, heads).reshape(1, w), g_cq.reshape(1, -1), cos, sin)


def _s5_matrices(lam_re, lam_im, log_dt, b_re, b_im, c_re, c_im, n_chunks):
    hp = lax.Precision.HIGHEST
    c = S5_CHUNK
    dt = jnp.exp(log_dt)[:, None]
    mag = jnp.exp(lam_re * dt)
    ar = mag * jnp.cos(lam_im * dt)
    ai = mag * jnp.sin(lam_im * dt)
    nr = ar - 1.0
    den = lam_re * lam_re + lam_im * lam_im
    cr = (nr * lam_re + ai * lam_im) / den
    ci = (ai * lam_re - nr * lam_im) / den
    bbr = cr[..., None] * b_re - ci[..., None] * b_im
    bbi = cr[..., None] * b_im + ci[..., None] * b_re

    def power(n):
        nf = n.astype(F32)[None, :, None]
        m = jnp.exp((lam_re * dt)[:, None, :] * nf)
        th = (lam_im * dt)[:, None, :] * nf
        return m * jnp.cos(th), m * jnp.sin(th)

    pr, pi = power(jnp.arange(c + 1))
    kbr = pr[..., None] * bbr[:, None] - pi[..., None] * bbi[:, None]
    kbi = pr[..., None] * bbi[:, None] + pi[..., None] * bbr[:, None]
    kk = jnp.einsum('ghp,gtpc->gthc', jnp.concatenate([c_re, -c_im], axis=-1),
                    jnp.concatenate([kbr, kbi], axis=2), precision=hp)
    tq = jnp.arange(c)
    g = kk.shape[0]
    rev = c - 1 - tq
    w_in = jnp.concatenate([kbr[:, rev].transpose(0, 1, 3, 2), kbi[:, rev].transpose(0, 1, 3, 2)], axis=-1)
    w_in = w_in.reshape(g, c * SSM_GROUP, 2 * SSM_STATE)
    prn, pin = pr[:, 1:], pi[:, 1:]
    wo_r = c_re[:, None] * prn[:, :, None, :] - c_im[:, None] * pin[:, :, None, :]
    wo_i = -c_re[:, None] * pin[:, :, None, :] - c_im[:, None] * prn[:, :, None, :]
    w_out = jnp.concatenate([wo_r, wo_i], axis=-1).transpose(0, 3, 1, 2).reshape(g, 2 * SSM_STATE, c * SSM_GROUP)
    n_steps = max(1, int(math.ceil(math.log2(n_chunks))))
    qr, qi = power(c * (2 ** jnp.arange(n_steps)))
    pa = jnp.concatenate([qr, qr], axis=-1)
    pb = jnp.concatenate([-qi, qi], axis=-1)
    return kk.astype(BF16), w_in.astype(BF16), w_out.astype(BF16), pa, pb


S5_SET = LANES // SSM_GROUP
S5_ROWS = 256


def _s5_kernel(u_ref, toep_ref, winc_ref, woutc_ref, pa_ref, pb_ref, y_ref, s_ref, win_ref, wout_ref):
    c = S5_CHUNK
    nc = u_ref.shape[0] // c
    rb = min(S5_ROWS, nc)
    p2 = 2 * SSM_STATE

    @pl.when(pl.program_id(1) == 0)
    def _():
        r_group = lax.broadcasted_iota(jnp.int32, (LANES, p2), 0) // SSM_GROUP
        l_group = lax.broadcasted_iota(jnp.int32, (p2, LANES), 1) // SSM_GROUP
        for t in range(c):
            for g in range(S5_SET):
                win_ref[0, t * LANES:(t + 1) * LANES, g * p2:(g + 1) * p2] = jnp.where(
                    r_group == g, winc_ref[0, t], jnp.zeros((LANES, p2), BF16))
                wout_ref[0, g * p2:(g + 1) * p2, t * LANES:(t + 1) * LANES] = jnp.where(
                    l_group == g, woutc_ref[0, t], jnp.zeros((p2, LANES), BF16))

    def chunk_rows(r0):
        return jnp.concatenate([u_ref[pl.ds(r0 * c + t, rb, stride=c), :] for t in range(c)], axis=1).astype(BF16)

    for blk in range(nc // rb):
        s_ref[blk * rb:(blk + 1) * rb, :] = _dot(chunk_rows(blk * rb), win_ref[0])
    row = lax.broadcasted_iota(jnp.int32, (nc, LANES), 0)
    n_steps = pa_ref.shape[1]
    for g in range(S5_SET):
        lanes = slice(g * LANES, (g + 1) * LANES)
        s = s_ref[:, lanes]
        for k in range(n_steps):
            sh = 1 << k
            if sh >= nc:
                break
            prev = jnp.where(row >= sh, pltpu.roll(s, sh, 0), 0.0)
            s = s + pa_ref[0, k:k + 1, lanes] * prev + pb_ref[0, k:k + 1, lanes] * pltpu.roll(prev, SSM_STATE, 1)
        s_ref[:, lanes] = jnp.where(row >= 1, pltpu.roll(s, 1, 0), 0.0)
    for blk in range(nc // rb):
        rows = slice(blk * rb, (blk + 1) * rb)
        a = chunk_rows(blk * rb)
        carried = _dot(s_ref[rows, :].astype(BF16), wout_ref[0])
        for t in range(c):
            y = _dot(a[:, :(t + 1) * LANES], toep_ref[0, (c - 1 - t) * LANES:, :])
            y_ref[pl.ds(blk * rb * c + t, rb, stride=c), :] = y + carried[:, t * LANES:(t + 1) * LANES]


def _s5_block_diag(mats):
    kk, w_in, w_out, pa, pb = mats
    g, c, h, p2 = kk.shape[0], S5_CHUNK, SSM_GROUP, 2 * SSM_STATE
    ns = g // S5_SET
    eye = jnp.eye(S5_SET, dtype=kk.dtype)
    lag_bd = jnp.einsum('sglhc,gk->slgckh', kk.reshape(ns, S5_SET, c + 1, h, h), eye).reshape(ns, c + 1, LANES, LANES)
    toep_bd = lag_bd[:, c - 1::-1].reshape(ns, c * LANES, LANES)

    w_in_c = w_in.reshape(ns, S5_SET, c, h, p2).transpose(0, 2, 1, 3, 4).reshape(ns, c, LANES, p2)
    w_out_c = w_out.reshape(ns, S5_SET, p2, c, h).transpose(0, 3, 2, 1, 4).reshape(ns, c, p2, LANES)
    k = pa.shape[1]
    lanes_of = lambda x: x.reshape(ns, S5_SET, k, p2).transpose(0, 2, 1, 3).reshape(ns, k, S5_SET * p2)
    return toep_bd, w_in_c, w_out_c, lanes_of(pa), lanes_of(pb)


def _s5(u, mats, bsz, seq_len):
    toep, w_in, w_out, pa, pb = _s5_block_diag(mats)
    ns = toep.shape[0]
    nc = seq_len // S5_CHUNK
    per_set = lambda s, b: (s, 0, 0)
    once = dict(pipeline_mode=pl.Buffered(1))
    io_spec = pl.BlockSpec((seq_len, LANES), lambda s, b: (b, s), **once)
    return pl.pallas_call(
        _s5_kernel,
        grid=(ns, bsz),
        in_specs=[
            io_spec,
            pl.BlockSpec((1,) + toep.shape[1:], per_set, **once),
            pl.BlockSpec((1,) + w_in.shape[1:], lambda s, b: (s, 0, 0, 0)),
            pl.BlockSpec((1,) + w_out.shape[1:], lambda s, b: (s, 0, 0, 0)),
            pl.BlockSpec((1,) + pa.shape[1:], per_set),
            pl.BlockSpec((1,) + pb.shape[1:], per_set),
        ],
        out_specs=io_spec,
        out_shape=jax.ShapeDtypeStruct(u.shape, F32),
        scratch_shapes=[pltpu.VMEM((nc, S5_SET * 2 * SSM_STATE), F32),
                        pltpu.VMEM((1, S5_CHUNK * LANES, S5_SET * 2 * SSM_STATE), BF16),
                        pltpu.VMEM((1, S5_SET * 2 * SSM_STATE, S5_CHUNK * LANES), BF16)],
        compiler_params=_params("arbitrary", "arbitrary"),
        name="s5_scan",
    )(u, toep, w_in, w_out, pa, pb)


def _moba_select_kernel(q_ref, k_ref, tri_ref, sel_ref, cnt_out_ref, km_ref, cnt_ref):
    h = pl.program_id(1)
    t = pl.program_id(2)
    ts = q_ref.shape[0]
    nbk = MOBA_MAX_BLOCKS

    @pl.when(t == 0)
    def _():
        kk = k_ref[...].astype(F32)
        nb = kk.shape[0] // MOBA_BLOCK
        km = jnp.sum(kk.reshape(nb, MOBA_BLOCK, LANES), axis=1) * (1.0 / MOBA_BLOCK)
        if nb < nbk:
            km = jnp.concatenate([km, jnp.zeros((nbk - nb, LANES), F32)], axis=0)
        lane = lax.broadcasted_iota(jnp.int32, (nbk, LANES), 1)
        km_ref[...] = jnp.where((lane // MOBA_HEAD_DIM) == (h % 2), km, 0.0).astype(BF16)
        cnt_ref[...] = jnp.zeros(cnt_ref.shape, F32)

    gate = _dot_nt(km_ref[...], q_ref[...])
    blk = lax.broadcasted_iota(jnp.int32, gate.shape, 0)
    qblk = (t * ts + lax.broadcasted_iota(jnp.int32, gate.shape, 1)) // MOBA_BLOCK
    g = jnp.where(blk < qblk, gate, -jnp.inf)
    selected = jnp.zeros(gate.shape, jnp.bool_)
    picks = []
    for _ in range(MOBA_TOPK):
        mx = jnp.max(g, axis=0, keepdims=True)
        idx = jnp.min(jnp.where(g == mx, blk, nbk), axis=0, keepdims=True)
        hit = blk == idx
        ok = (idx[0:1] < qblk[0:1]) & (mx > -jnp.inf)
        picks.append((hit, idx, ok))
        selected = selected | (hit & ok)
        g = jnp.where(hit, -jnp.inf, g)
    sel01 = jnp.where(selected, 1.0, 0.0)
    chunks = [sel01[:, c * LANES:(c + 1) * LANES] for c in range(ts // LANES)]
    within = _dot(jnp.concatenate(chunks, axis=0).astype(BF16), tri_ref[...])
    base = cnt_ref[...]
    pieces = []
    for c, chunk in enumerate(chunks):
        pieces.append(within[c * nbk:(c + 1) * nbk] + base)
        base = base + jnp.sum(chunk, axis=1, keepdims=True)
    prefix = jnp.concatenate(pieces, axis=1)
    rows = [jnp.where(ok, idx, -1) for (_, idx, ok) in picks]
    rows += [jnp.sum(jnp.where(hit, prefix, 0.0), axis=0, keepdims=True).astype(jnp.int32) for (hit, _, _) in picks]
    rows += [jnp.zeros((1, ts), jnp.int32)] * (8 - 2 * MOBA_TOPK)
    sel_ref[0, 0] = jnp.concatenate(rows, axis=0)
    cnt_ref[...] = base
    cnt_out_ref[0, 0] = base


def _moba_select(q2, kv, bsz, seq_len, ts=1024):
    ts = min(ts, seq_len)
    heads = q2.shape[1] // MOBA_HEAD_DIM
    nt = seq_len // ts
    tri = (jnp.arange(LANES)[:, None] < jnp.arange(LANES)[None, :]).astype(BF16)
    return pl.pallas_call(
        _moba_select_kernel,
        grid=(bsz, heads, nt),
        in_specs=[
            pl.BlockSpec((ts, LANES), lambda b, h, t: (b * nt + t, h // 2)),
            pl.BlockSpec((seq_len, LANES), lambda b, h, t: (b, 2 * (h // 2))),
            pl.BlockSpec((LANES, LANES), lambda b, h, t: (0, 0)),
        ],
        out_specs=[pl.BlockSpec((1, 1, 8, ts), lambda b, h, t: (b, h, 0, t)),
                   pl.BlockSpec((1, 1, MOBA_MAX_BLOCKS, LANES), lambda b, h, t: (b, h, 0, 0))],
        out_shape=[jax.ShapeDtypeStruct((bsz, heads, 8, seq_len), jnp.int32),
                   jax.ShapeDtypeStruct((bsz, heads, MOBA_MAX_BLOCKS, LANES), F32)],
        scratch_shapes=[pltpu.VMEM((MOBA_MAX_BLOCKS, LANES), BF16), pltpu.VMEM((MOBA_MAX_BLOCKS, LANES), F32)],
        compiler_params=_params("parallel", "arbitrary", "arbitrary"),
        name="moba_select",
    )(q2, kv, tri)


MOBA_DIAG_BLOCKS = 8


def _moba_diag_kernel(q_ref, kv_ref, o_ref, lse_ref):
    lane = lax.broadcasted_iota(jnp.int32, (MOBA_BLOCK, LANES), 1)
    is_a = lane < MOBA_HEAD_DIM
    r = lax.broadcasted_iota(jnp.int32, (MOBA_BLOCK, MOBA_BLOCK), 0)
    c = lax.broadcasted_iota(jnp.int32, (MOBA_BLOCK, MOBA_BLOCK), 1)
    for u in range(MOBA_DIAG_BLOCKS):
        rows = pl.ds(u * MOBA_BLOCK, MOBA_BLOCK)
        q = q_ref[rows, :]
        kb = kv_ref[rows, :LANES]
        vb = kv_ref[rows, LANES:]
        pvs, ms = [], []
        for own in (is_a, jnp.logical_not(is_a)):
            s = jnp.where(c <= r, _dot_nt(jnp.where(own, q, jnp.zeros_like(q)), kb), NEG_BIG)
            m = jnp.max(s, axis=1, keepdims=True)
            p = jnp.exp(s - m).astype(BF16)
            pvs.append(_dot(p, jnp.where(own, vb, 1.0)))
            ms.append(m)
        num = jnp.where(is_a, pvs[0], pvs[1])
        den = pltpu.roll(jnp.where(is_a, pvs[1], pvs[0]), MOBA_HEAD_DIM, 1)
        o_ref[rows, :] = num / den
        lse_ref[rows, :] = jnp.where(is_a, ms[0], ms[1]) + jnp.log(den)


def _moba_diag(q2, kv, bsz, seq_len):
    npair = q2.shape[1] // LANES
    rows = MOBA_BLOCK * MOBA_DIAG_BLOCKS
    nb = seq_len // rows
    spec = pl.BlockSpec((rows, LANES), lambda b, p, i: (b * nb + i, p))
    out = jax.ShapeDtypeStruct(q2.shape, F32)
    return pl.pallas_call(
        _moba_diag_kernel,
        grid=(bsz, npair, nb),
        in_specs=[spec, pl.BlockSpec((rows, 2 * LANES), lambda b, p, i: (b * nb + i, p))],
        out_specs=[spec, spec],
        out_shape=[out, out],
        compiler_params=_params("parallel", "parallel", "parallel"),
        name="moba_diag",
    )(q2, kv)


MOBA_TILE = 256
MOBA_TILES_PER_STEP = 8


def _moba_grouped_kernel(tile_row_ref, tile_pair_ref, tile_half_ref, tile_real_ref, qd_ref, *refs):
    n = MOBA_TILES_PER_STEP
    kv_refs, o_ref = refs[:n], refs[n + 1]
    i = pl.program_id(0)
    lane = lax.broadcasted_iota(jnp.int32, (MOBA_TILE, LANES), 1)
    kv_lane = lax.broadcasted_iota(jnp.int32, (MOBA_BLOCK, LANES), 1)

    @pl.when(tile_real_ref[i * n] > 0)
    def _():
        for u in range(n):
            tile = i * n + u
            rows = pl.ds(u * MOBA_TILE, MOBA_TILE)
            own = (lane // MOBA_HEAD_DIM) == tile_half_ref[tile]
            s = _dot_nt(qd_ref[rows, :].astype(BF16), kv_refs[u][:, :LANES])
            m = jnp.max(s, axis=1, keepdims=True)
            p = jnp.exp(s - m).astype(BF16)
            vb = jnp.where((kv_lane // MOBA_HEAD_DIM) == tile_half_ref[tile], kv_refs[u][:, LANES:], 1.0)
            pv = _dot(p, vb)
            row_sum = pltpu.roll(pv, MOBA_HEAD_DIM, 1)
            part = jnp.where(own, pv / row_sum, m + jnp.log(pv))
            o_ref[rows, :] = jnp.where(tile_real_ref[tile] > 0, part, NEG_BIG)

    @pl.when(tile_real_ref[i * n] == 0)
    def _():
        o_ref[...] = jnp.full(o_ref.shape, NEG_BIG, F32)


def _moba_grouped(qd, kv, tile_row, tile_pair, tile_half, tile_real, run_after):
    n = MOBA_TILES_PER_STEP
    n_tiles = qd.shape[0] // MOBA_TILE

    def kv_spec(u):
        return pl.BlockSpec((MOBA_BLOCK, 2 * LANES), lambda i, tr, tp, th, tl: (tr[i * n + u], tp[i * n + u]))

    grid_spec = pltpu.PrefetchScalarGridSpec(
        num_scalar_prefetch=4,
        grid=(n_tiles // n,),
        in_specs=[pl.BlockSpec((n * MOBA_TILE, LANES), lambda i, tr, tp, th, tl: (i, 0))]
        + [kv_spec(u) for u in range(n)]
        + [pl.BlockSpec((8, LANES), lambda i, tr, tp, th, tl: (0, 0))],
        out_specs=pl.BlockSpec((n * MOBA_TILE, LANES), lambda i, tr, tp, th, tl: (i, 0)),
    )
    return pl.pallas_call(
        _moba_grouped_kernel,
        grid_spec=grid_spec,
        out_shape=jax.ShapeDtypeStruct(qd.shape, F32),
        compiler_params=_params("arbitrary"),
        name="moba_grouped",
    )(tile_row, tile_pair, tile_half, tile_real, qd, *([kv] * n), run_after)


def _moba_combine_kernel(od_ref, lsed_ref, *refs):
    g_refs, o_ref = refs[:-1], refs[-1]
    lane = lax.broadcasted_iota(jnp.int32, od_ref.shape, 1)
    is_a = lane < MOBA_HEAD_DIM
    parts = [(od_ref[...], lsed_ref[...])]
    for s in range(MOBA_TOPK):
        xa = g_refs[s][0, 0, 0]
        xb = g_refs[MOBA_TOPK + s][0, 0, 0]
        o = jnp.where(is_a, xa, xb)
        lse = pltpu.roll(jnp.where(is_a, xb, xa), MOBA_HEAD_DIM, 1)
        parts.append((o, lse))
    m = parts[0][1]
    for _, lse in parts[1:]:
        m = jnp.maximum(m, lse)
    num = jnp.zeros(od_ref.shape, F32)
    den = jnp.zeros(od_ref.shape, F32)
    for o, lse in parts:
        w = jnp.exp(lse - m)
        num = num + w * o
        den = den + w
    o_ref[...] = (num / den).astype(BF16)


def _moba_combine(od, lsed, gath, bsz, seq_len, tm=1024):
    tm = min(tm, seq_len)
    npair = od.shape[1] // LANES
    nt = seq_len // tm
    spec = pl.BlockSpec((tm, LANES), lambda b, p, i: (b * nt + i, p))

    def g_spec(e, s):
        return pl.BlockSpec((1, 1, 1, tm, LANES), lambda b, p, i: (b, 2 * p + e, s, i, 0))

    return pl.pallas_call(
        _moba_combine_kernel,
        grid=(bsz, npair, nt),
        in_specs=[spec, spec] + [g_spec(e, s) for e in range(2) for s in range(MOBA_TOPK)],
        out_specs=spec,
        out_shape=jax.ShapeDtypeStruct(od.shape, BF16),
        compiler_params=_params("parallel", "parallel", "parallel"),
        name="moba_combine",
    )(od, lsed, *([gath] * (2 * MOBA_TOPK)))


SC_WINDOW = 128
SC_CORES = 2
SC_SUBCORES = 16


def _sc_mesh():
    return plsc.VectorSubcoreMesh(core_axis_name="core", subcore_axis_name="subcore")


def _sc_gather_rows(table, idx):
    n = idx.shape[0]
    d = table.shape[1]
    window = SC_WINDOW
    assert n % (window * SC_CORES * SC_SUBCORES) == 0
    per_core = n // window // SC_CORES

    @functools.partial(pl.kernel, out_type=jax.ShapeDtypeStruct((n, d), table.dtype), mesh=_sc_mesh())
    def gather_kernel(x_hbm, i_hbm, o_hbm):
        base = lax.axis_index("core") * per_core

        def body(i_vmem, o_vmem):
            pltpu.sync_copy(x_hbm.at[i_vmem.at[0]], o_vmem)

        pltpu.emit_pipeline(
            body,
            grid=(per_core,),
            in_specs=[pl.BlockSpec((1, window), index_map=lambda i: (0, base + i))],
            out_specs=[pl.BlockSpec((window, d), index_map=lambda i: (base + i, 0))],
            core_axis_name="subcore",
            dimension_semantics=(pltpu.PARALLEL,),
            trace_scopes=False,
        )(i_hbm, o_hbm)

    return gather_kernel(table, idx.reshape(1, n))


def _sc_scatter_rows(rows, dest, n_out, repeat):
    n_src, d = rows.shape
    n = dest.shape[0]
    assert n == repeat * n_src
    window = SC_WINDOW
    assert n % (window * SC_CORES * SC_SUBCORES) == 0
    per_core = n // window // SC_CORES
    src_windows = n_src // window

    @functools.partial(pl.kernel, out_type=jax.ShapeDtypeStruct((n_out, d), rows.dtype), mesh=_sc_mesh())
    def scatter_kernel(x_hbm, i_hbm, o_hbm):
        base = lax.axis_index("core") * per_core

        def body(x_vmem, i_vmem):
            pltpu.sync_copy(x_vmem, o_hbm.at[i_vmem.at[0]])

        pltpu.emit_pipeline(
            body,
            grid=(per_core,),
            in_specs=[pl.BlockSpec((window, d), index_map=lambda i: ((base + i) % src_windows, 0)),
                      pl.BlockSpec((1, window), index_map=lambda i: (0, base + i))],
            out_specs=[],
            core_axis_name="subcore",
            dimension_semantics=(pltpu.PARALLEL,),
            trace_scopes=False,
        )(x_hbm, i_hbm)

    return scatter_kernel(rows, dest.reshape(1, n))


def _moba_dispatch(q2, qh, kv, bsz, seq_len):
    heads = q2.shape[1] // MOBA_HEAD_DIM
    nbk = MOBA_MAX_BLOCKS
    sel, cnt = _moba_select(q2, kv, bsz, seq_len)
    counts = cnt[..., 0].astype(jnp.int32).reshape(bsz * heads * nbk)
    pcounts = ((counts + MOBA_TILE - 1) // MOBA_TILE) * MOBA_TILE
    pends = jnp.cumsum(pcounts)
    pstarts = (pends - pcounts).reshape(bsz, heads, 1, 1, nbk)
    n_items = bsz * heads * seq_len * MOBA_TOPK
    step_rows = MOBA_TILE * MOBA_TILES_PER_STEP
    n_rows = -(-(n_items + bsz * heads * nbk * MOBA_TILE) // step_rows) * step_rows
    n_null = step_rows
    idx = sel[:, :, 0:MOBA_TOPK, :]
    rank = sel[:, :, MOBA_TOPK:2 * MOBA_TOPK, :]
    start = jnp.sum(jnp.where(idx[..., None] == jnp.arange(nbk), pstarts, 0), axis=-1)
    null_row = n_rows + jnp.arange(seq_len, dtype=jnp.int32) % n_null
    dest = jnp.where(idx >= 0, start + rank, null_row)
    n_tiles = (n_rows + n_null) // MOBA_TILE
    n_groups = bsz * heads * nbk
    tile_start = jnp.arange(n_tiles, dtype=jnp.int32) * MOBA_TILE
    tile_g = jnp.minimum(jnp.sum((pends[None, :] <= tile_start[:, None]).astype(jnp.int32), axis=1), n_groups - 1)
    tile_real = (tile_start < pends[-1]).astype(jnp.int32)
    tile_head = (tile_g // nbk) % heads
    tile_row = (tile_g // (heads * nbk)) * (seq_len // MOBA_BLOCK) + jnp.minimum(tile_g % nbk, seq_len // MOBA_BLOCK - 1)
    dest_by_slot = dest.transpose(2, 1, 0, 3).reshape(-1)
    qd = _sc_scatter_rows(qh.reshape(-1, LANES), dest_by_slot, n_rows + n_null, MOBA_TOPK)
    return qd, dest, (tile_row, tile_head // 2, tile_head % 2, tile_real)


def _moba_finish(qd, dest, tiles, od, lsed, kv, bsz, seq_len):
    heads = dest.shape[1]
    part = _moba_grouped(qd, kv, *tiles, lsed)
    gath = _sc_gather_rows(part, dest.reshape(-1)).reshape(bsz, heads, MOBA_TOPK, seq_len, LANES)
    return _moba_combine(od, lsed, gath, bsz, seq_len)


def _mem_kv_kernel(mem_ref, gmem_ref, w_ref, gck_ref, k_ref, v_ref):
    xf = mem_ref[0]
    ms = jnp.mean(xf * xf, axis=-1, keepdims=True)
    h = (xf * lax.rsqrt(ms + EPS) * gmem_ref[...]).astype(BF16)
    kv = _dot(h, w_ref[...])
    w = k_ref.shape[2]
    hd = w // X_HEADS
    for c in range(X_HEADS):
        chunk = kv[:, c * hd:(c + 1) * hd]
        cms = jnp.mean(chunk * chunk, axis=-1, keepdims=True)
        k_ref[0, :, c * hd:(c + 1) * hd] = (chunk * lax.rsqrt(cms + EPS) * gck_ref[...]).astype(BF16)
    v_ref[0] = kv[:, w:].astype(BF16)


def _mem_kv(mem, g_mem, w_kv_mem, g_ck):
    bsz, m, d = mem.shape
    w = w_kv_mem.shape[1] // 2
    const = lambda b: (0, 0)
    out = jax.ShapeDtypeStruct((bsz, m, w), BF16)
    return pl.pallas_call(
        _mem_kv_kernel,
        grid=(bsz,),
        in_specs=[pl.BlockSpec((1, m, d), lambda b: (b, 0, 0)), pl.BlockSpec((1, d), const),
                  pl.BlockSpec((d, 2 * w), const), pl.BlockSpec((1, w // X_HEADS), const)],
        out_specs=[pl.BlockSpec((1, m, w), lambda b: (b, 0, 0))] * 2,
        out_shape=[out, out],
        compiler_params=_params("parallel"),
        name="mem_kv",
    )(mem, g_mem.reshape(1, d), w_kv_mem.astype(BF16), g_ck.reshape(1, -1))


def _merge_kernel(x_ref, ys_ref, u_ref, dskip_ref, om_ref, xq_ref, kc_ref, vc_ref, g_ref,
                  wglu_ref, wmo_ref, wco_ref, wout_ref, gffn_ref, wr_ref, br_ref,
                  x1_ref, h2_ref, logit_ref):
    d = x_ref.shape[1]
    y = ys_ref[...].astype(F32) + dskip_ref[...] * u_ref[...].astype(F32)
    ge = 0.5 * y * (1.0 + jnp.tanh(math.sqrt(2.0 / math.pi) * (y + 0.044715 * (y * y * y))))
    z = _dot(ge.astype(BF16), wglu_ref[...])
    merged = g_ref[:, 0:d].astype(F32) * (z[:, :d] * _sigmoid(z[:, d:]))
    merged = merged + g_ref[:, d:2 * d].astype(F32) * _dot(om_ref[...], wmo_ref[...])
    w = xq_ref.shape[1]
    hd = w // X_HEADS
    heads = []
    for c in range(X_HEADS):
        s = _dot_nt(xq_ref[:, c * hd:(c + 1) * hd], kc_ref[0, :, c * hd:(c + 1) * hd]) * (hd ** -0.5)
        p = jnp.exp(s - jnp.max(s, axis=1, keepdims=True))
        p = p / jnp.sum(p, axis=1, keepdims=True)
        heads.append(_dot(p.astype(BF16), vc_ref[0, :, c * hd:(c + 1) * hd]))
    oc = jnp.concatenate(heads, axis=1).astype(BF16)
    merged = merged + g_ref[:, 2 * d:3 * d].astype(F32) * _dot(oc, wco_ref[...])
    x1 = x_ref[...] + _dot(merged.astype(BF16), wout_ref[...])
    x1_ref[...] = x1
    ms = jnp.mean(x1 * x1, axis=-1, keepdims=True)
    h2 = (x1 * lax.rsqrt(ms + EPS) * gffn_ref[...]).astype(BF16)
    packed = _pack_bf16_pairs(h2.astype(F32))
    for j in range(h2_ref.shape[0]):
        h2_ref[j] = packed[:, j * LANES:(j + 1) * LANES]
    logit_ref[...] = _dot_nt(wr_ref[...], h2) + br_ref[...]


def _merge(xt, ys, u, d_skip, om, xqn, kc, vc, gates, w_glu, w_mo, w_co, w_out, g_ffn, w_router, b_router,
           seq_len, tm=512):
    t, d = xt.shape
    tm = min(tm, seq_len)
    w = d // 2
    m = kc.shape[1]
    ne = w_router.shape[1]
    nt = seq_len // tm
    row = lambda i: (i, 0)
    const = lambda i: (0, 0)
    per_b = lambda i: (i // nt, 0, 0)
    once = dict(pipeline_mode=pl.Buffered(1))
    return pl.pallas_call(
        _merge_kernel,
        grid=(t // tm,),
        in_specs=[
            pl.BlockSpec((tm, d), row), pl.BlockSpec((tm, w), row), pl.BlockSpec((tm, w), row),
            pl.BlockSpec((1, w), const), pl.BlockSpec((tm, w), row), pl.BlockSpec((tm, w), row),
            pl.BlockSpec((1, m, w), per_b), pl.BlockSpec((1, m, w), per_b),
            pl.BlockSpec((tm, N_BRANCH * d), row),
            pl.BlockSpec((w, 2 * d), const, **once), pl.BlockSpec((w, d), const, **once),
            pl.BlockSpec((w, d), const, **once), pl.BlockSpec((d, d), const, **once), pl.BlockSpec((1, d), const),
            pl.BlockSpec((ne, d), const), pl.BlockSpec((ne, 1), const),
        ],
        out_specs=[pl.BlockSpec((tm, d), row), pl.BlockSpec((d // 2 // LANES, tm, LANES), lambda i: (0, i, 0)),
                   pl.BlockSpec((ne, tm), lambda i: (0, i))],
        out_shape=[jax.ShapeDtypeStruct((t, d), F32), jax.ShapeDtypeStruct((d // 2 // LANES, t, LANES), jnp.uint32),
                   jax.ShapeDtypeStruct((ne, t), F32)],
        compiler_params=_params("parallel"),
        name="merge",
    )(xt, ys, u, d_skip.reshape(1, w), om, xqn, kc, vc, gates,
      w_glu.astype(BF16), w_mo.astype(BF16), w_co.astype(BF16), w_out.astype(BF16),
      g_ffn.reshape(1, d), w_router.T.astype(BF16), b_router.reshape(ne, 1))


def _moe_kernel(blk_e_ref, blk_used_ref, xs_ref, wgu_ref, bgu_ref, wd_ref, bd_ref, y_ref, wgu_bf, wd_bf):
    i = pl.program_id(0)
    prev = blk_e_ref[jnp.maximum(i - 1, 0)]

    @pl.when((i == 0) | (blk_e_ref[i] != prev))
    def _():
        wgu_bf[...] = wgu_ref[0].astype(BF16)
        wd_bf[...] = wd_ref[0].astype(BF16)

    @pl.when(blk_used_ref[i] > 0)
    def _():
        de = wd_bf.shape[0]
        words = jnp.concatenate([xs_ref[j] for j in range(xs_ref.shape[0])], axis=1)
        xs = jnp.concatenate(_unpack_bf16_pairs(words), axis=1).astype(BF16)
        gu = _dot(xs, wgu_bf[...]) + bgu_ref[0]
        gate = jnp.minimum(gu[:, :de], SWIGLU_LIMIT)
        up = jnp.clip(gu[:, de:], -SWIGLU_LIMIT, SWIGLU_LIMIT)
        act = gate * _sigmoid(SWIGLU_ALPHA * gate) * (up + 1.0)
        y = _dot(act.astype(BF16), wd_bf[...]) + bd_ref[0]
        packed = _pack_bf16_pairs(y.astype(BF16).astype(F32))
        for j in range(y_ref.shape[0]):
            y_ref[j] = packed[:, j * LANES:(j + 1) * LANES]

    @pl.when(blk_used_ref[i] == 0)
    def _():
        y_ref[...] = jnp.zeros(y_ref.shape, y_ref.dtype)


def _moe_experts(xs, blk_e, blk_used, w_gu, b_gu, w_down, b_down):
    slabs, p, _ = xs.shape
    d = 2 * slabs * LANES
    ne, _, de2 = w_gu.shape
    de = de2 // 2
    nblk = p // EXPERT_ROWS
    row_spec = pl.BlockSpec((slabs, EXPERT_ROWS, LANES), lambda i, e, n: (0, i, 0))
    grid_spec = pltpu.PrefetchScalarGridSpec(
        num_scalar_prefetch=2,
        grid=(nblk,),
        in_specs=[
            row_spec,
            pl.BlockSpec((1, d, de2), lambda i, e, n: (e[i], 0, 0)),
            pl.BlockSpec((1, 1, de2), lambda i, e, n: (e[i], 0, 0)),
            pl.BlockSpec((1, de, d), lambda i, e, n: (e[i], 0, 0)),
            pl.BlockSpec((1, 1, d), lambda i, e, n: (e[i], 0, 0)),
        ],
        out_specs=row_spec,
        scratch_shapes=[pltpu.VMEM((d, de2), BF16), pltpu.VMEM((de, d), BF16)],
    )
    return pl.pallas_call(
        _moe_kernel,
        grid_spec=grid_spec,
        out_shape=jax.ShapeDtypeStruct(xs.shape, jnp.uint32),
        compiler_params=_params("arbitrary"),
        name="moe_experts",
    )(blk_e, blk_used, xs, w_gu, b_gu.reshape(ne, 1, de2), w_down, b_down.reshape(ne, 1, d))


def _router_kernel(lg_ref, tri_ref, e_ref, w_ref, r_ref, cnt_out_ref, cnt_ref):
    t = pl.program_id(0)
    ne, ts = lg_ref.shape

    @pl.when(t == 0)
    def _():
        cnt_ref[...] = jnp.zeros(cnt_ref.shape, F32)

    g = lg_ref[...]
    eid = lax.broadcasted_iota(jnp.int32, g.shape, 0)
    selected = jnp.zeros(g.shape, jnp.bool_)
    picks = []
    for _ in range(TOPK_EXPERTS):
        mx = jnp.max(g, axis=0, keepdims=True)
        idx = jnp.min(jnp.where(g == mx, eid, ne), axis=0, keepdims=True)
        hit = eid == idx
        picks.append((hit, idx, mx))
        selected = selected | hit
        g = jnp.where(hit, -jnp.inf, g)
    exps = [jnp.exp(mx - picks[0][2]) for (_, _, mx) in picks]
    total = sum(exps)
    sel01 = jnp.where(selected, 1.0, 0.0).astype(BF16)
    prefix = _dot(sel01, tri_ref[...]) + jnp.tile(cnt_ref[...], (1, ts // LANES))
    pad_i = [jnp.zeros((1, ts), jnp.int32)] * (8 - TOPK_EXPERTS)
    e_ref[...] = jnp.concatenate([idx for (_, idx, _) in picks] + pad_i, axis=0)
    w_ref[...] = jnp.concatenate([e / total for e in exps] + [jnp.zeros((1, ts), F32)] * (8 - TOPK_EXPERTS), axis=0)
    r_ref[...] = jnp.concatenate(
        [jnp.sum(jnp.where(hit, prefix, 0.0), axis=0, keepdims=True).astype(jnp.int32) for (hit, _, _) in picks]
        + pad_i, axis=0)
    cnt_ref[...] = cnt_ref[...] + _dot(sel01, jnp.ones((ts, LANES), BF16))
    cnt_out_ref[...] = cnt_ref[...]


def _router(logits_t, ts=1024):
    ne, t = logits_t.shape
    ts = min(ts, t)
    tri = (jnp.arange(ts)[:, None] < jnp.arange(ts)[None, :]).astype(BF16)
    rows = pl.BlockSpec((8, ts), lambda i: (0, i))
    return pl.pallas_call(
        _router_kernel,
        grid=(t // ts,),
        in_specs=[pl.BlockSpec((ne, ts), lambda i: (0, i)), pl.BlockSpec((ts, ts), lambda i: (0, 0))],
        out_specs=[rows, rows, rows, pl.BlockSpec((ne, LANES), lambda i: (0, 0))],
        out_shape=[jax.ShapeDtypeStruct((8, t), jnp.int32), jax.ShapeDtypeStruct((8, t), F32),
                   jax.ShapeDtypeStruct((8, t), jnp.int32), jax.ShapeDtypeStruct((ne, LANES), F32)],
        scratch_shapes=[pltpu.VMEM((ne, LANES), F32)],
        compiler_params=_params("arbitrary"),
        name="moe_router",
    )(logits_t, tri)


def _moe_mix_kernel(x1_ref, w_ref, pk_ref, o_ref):
    slabs = pk_ref.shape[0]
    wts = w_ref[...]
    lo = [jnp.zeros((x1_ref.shape[0], LANES), F32)] * slabs
    hi = [jnp.zeros((x1_ref.shape[0], LANES), F32)] * slabs
    for k in range(TOPK_EXPERTS):
        wk = wts[:, k:k + 1]
        for j in range(slabs):
            a, b = _unpack_bf16_pairs(pk_ref[j, k])
            lo[j] = lo[j] + wk * a
            hi[j] = hi[j] + wk * b
    o_ref[...] = x1_ref[...] + jnp.concatenate(lo + hi, axis=1)


def _moe_mix(x1, weights_tk, picked, tm=512):
    t, d = x1.shape
    slabs = picked.shape[0]
    return pl.pallas_call(
        _moe_mix_kernel,
        grid=(t // tm,),
        in_specs=[pl.BlockSpec((tm, d), lambda i: (i, 0)),
                  pl.BlockSpec((tm, TOPK_EXPERTS), lambda i: (i, 0)),
                  pl.BlockSpec((slabs, TOPK_EXPERTS, tm, LANES), lambda i: (0, 0, i, 0))],
        out_specs=pl.BlockSpec((tm, d), lambda i: (i, 0)),
        out_shape=jax.ShapeDtypeStruct((t, d), F32),
        compiler_params=_params("parallel"),
        name="moe_mix",
    )(x1, weights_tk, picked)


def _moe_ffn(x1, h2, logits_t, w_gu, b_gu, w_down, b_down):
    t, d = x1.shape
    tk = t * TOPK_EXPERTS
    e8, w8, r8, cnt = _router(logits_t)
    top_e, weights, rank = e8[:TOPK_EXPERTS], w8[:TOPK_EXPERTS], r8[:TOPK_EXPERTS]
    counts = cnt[:, 0].astype(jnp.int32)
    pcounts = ((counts + EXPERT_ROWS - 1) // EXPERT_ROWS) * EXPERT_ROWS
    pends = jnp.cumsum(pcounts)
    pstarts = pends - pcounts
    is_e = top_e[..., None] == jnp.arange(N_EXPERTS, dtype=jnp.int32)
    dest_by_k = jnp.sum(jnp.where(is_e, pstarts, 0), axis=-1) + rank
    nblk = -(-tk // EXPERT_ROWS) + N_EXPERTS
    blk_start = jnp.arange(nblk, dtype=jnp.int32) * EXPERT_ROWS
    blk_e = jnp.minimum(jnp.sum((pends[None, :] <= blk_start[:, None]).astype(jnp.int32), axis=1), N_EXPERTS - 1)
    blk_used = (blk_start < pends[-1]).astype(jnp.int32)
    slabs = d // 2 // LANES
    p = nblk * EXPERT_ROWS
    slab_off = jnp.arange(slabs, dtype=jnp.int32) * p
    dest_kst = (dest_by_k[:, None, :] + slab_off[None, :, None]).reshape(-1)
    dest_skt = (dest_by_k[None, :, :] + slab_off[:, None, None]).reshape(-1)
    xs = _sc_scatter_rows(h2.reshape(slabs * t, LANES), dest_kst, slabs * p, TOPK_EXPERTS)
    ys = _moe_experts(xs.reshape(slabs, p, LANES), blk_e, blk_used, w_gu, b_gu, w_down, b_down)
    picked = _sc_gather_rows(ys.reshape(slabs * p, LANES), dest_skt).reshape(slabs, TOPK_EXPERTS, t, LANES)
    return _moe_mix(x1, weights.T, picked)


def kernel(x, mem, g_mix, w_in, lam_re, lam_im, log_dt, b_re, b_im, c_re, c_im, d_skip, w_glu, g_q, g_k, w_moba_out, g_mem, w_kv_mem, g_cq, g_ck, w_cross_out, w_out, g_ffn, w_router, b_router, w_gu, b_gu, w_down, b_down):
    bsz, seq_len, d = x.shape
    xt = x.reshape(bsz * seq_len, d)
    for l in range(g_mix.shape[0]):
        u, q2, kv, xqn, gates, qh = _in_proj(xt, g_mix[l], w_in[l], g_q[l], g_k[l], g_cq[l], seq_len)
        qd, dest, tiles = _moba_dispatch(q2, qh, kv, bsz, seq_len)
        mats = _s5_matrices(lam_re[l], lam_im[l], log_dt[l], b_re[l], b_im[l], c_re[l], c_im[l],
                            seq_len // S5_CHUNK)
        ys = _s5(u, mats, bsz, seq_len)
        od, lsed = _moba_diag(q2, kv, bsz, seq_len)
        kc, vc = _mem_kv(mem, g_mem[l], w_kv_mem[l], g_ck[l])
        om = _moba_finish(qd, dest, tiles, od, lsed, kv, bsz, seq_len)
        x1, h2, logits = _merge(xt, ys, u, d_skip[l], om, xqn, kc, vc, gates, w_glu[l], w_moba_out[l],
                                w_cross_out[l], w_out[l], g_ffn[l], w_router[l], b_router[l], seq_len)
        xt = _moe_ffn(x1, h2, logits, w_gu[l], b_gu[l], w_down[l], b_down[l])
    return xt.reshape(bsz, seq_len, d)
```

```python
import functools
import math

import jax
import jax.numpy as jnp
from jax import lax
from jax.experimental import pallas as pl
from jax.experimental.pallas import tpu as pltpu
from jax.experimental.pallas import tpu_sc as plsc

F32 = jnp.float32
BF16 = jnp.bfloat16

EPS = 1e-6
N_BRANCH = 3
SSM_GROUP = 16
SSM_STATE = 64
S5_CHUNK = 16
MOBA_HEAD_DIM = 64
MOBA_BLOCK = 256
MOBA_TOPK = 3
MOBA_MAX_BLOCKS = 64
ROPE_THETA = 10000.0
X_HEADS = 4
N_EXPERTS = 32
TOPK_EXPERTS = 4
SWIGLU_LIMIT = 7.0
SWIGLU_ALPHA = 1.702
EXPERT_ROWS = 512
NEG_BIG = -1e30
LANES = 128
VMEM_LIMIT_BYTES = 56 * 1024 * 1024


def _params(*sem):
    return pltpu.CompilerParams(dimension_semantics=sem, vmem_limit_bytes=VMEM_LIMIT_BYTES)


def _sigmoid(x):
    return 1.0 / (1.0 + jnp.exp(-x))


def _dot(a, b):
    return jnp.dot(a, b, preferred_element_type=F32)


def _pack_bf16_pairs(x):
    n = x.shape[1] // 2
    lo = lax.bitcast_convert_type(x[:, :n], jnp.uint32) >> 16
    hi = lax.bitcast_convert_type(x[:, n:], jnp.uint32) & jnp.uint32(0xFFFF0000)
    return lo | hi


def _unpack_bf16_pairs(w):
    lo = lax.bitcast_convert_type(w << 16, F32)
    hi = lax.bitcast_convert_type(w & jnp.uint32(0xFFFF0000), F32)
    return lo, hi


def _dot_nt(a, b):
    return lax.dot_general(a, b, (((1,), (1,)), ((), ())), preferred_element_type=F32)


def _inproj_kernel(x_ref, gmix_ref, wa_ref, wg_ref, e64_ref, gq_ref, gk_ref, gcq_ref, cos_ref, sin_ref,
                   u_ref, q_ref, kv_ref, xq_ref, g_ref, qh_ref):
    xf = x_ref[...]
    ms = jnp.mean(xf * xf, axis=-1, keepdims=True)
    h = (xf * lax.rsqrt(ms + EPS) * gmix_ref[...]).astype(BF16)
    w = u_ref.shape[1]
    proj = lambda c: _dot(h, wa_ref[:, c * w:(c + 1) * w])
    u_ref[...] = proj(0)

    cos = jnp.tile(cos_ref[...], (1, w // LANES))
    sin = jnp.tile(sin_ref[...], (1, w // LANES))
    lane = lax.broadcasted_iota(jnp.int32, (xf.shape[0], w), 1)
    first_half = (lane % MOBA_HEAD_DIM) < (MOBA_HEAD_DIM // 2)

    def qk_norm_rope(raw, g):
        ss = _dot((raw * raw).astype(BF16), e64_ref[...])
        n = raw * lax.rsqrt(ss * (1.0 / MOBA_HEAD_DIM) + EPS) * g
        rot = jnp.where(first_half,
                        pltpu.roll(n, w - MOBA_HEAD_DIM // 2, 1),
                        pltpu.roll(n, MOBA_HEAD_DIM // 2, 1))
        return n * cos + rot * sin

    q = qk_norm_rope(proj(1), gq_ref[...]) * (MOBA_HEAD_DIM ** -0.5)
    q_ref[...] = q.astype(BF16)
    in_a = lax.broadcasted_iota(jnp.int32, (xf.shape[0], LANES), 1) < MOBA_HEAD_DIM
    for p in range(w // LANES):
        pair = q[:, p * LANES:(p + 1) * LANES].astype(BF16).astype(F32)
        qh_ref[2 * p] = jnp.where(in_a, pair, 0.0)
        qh_ref[2 * p + 1] = jnp.where(in_a, 0.0, pair)
    k = qk_norm_rope(proj(2), gk_ref[...]).astype(BF16)
    v = proj(3).astype(BF16)
    for p in range(w // LANES):
        kv_ref[:, (2 * p) * LANES:(2 * p + 1) * LANES] = k[:, p * LANES:(p + 1) * LANES]
        kv_ref[:, (2 * p + 1) * LANES:(2 * p + 2) * LANES] = v[:, p * LANES:(p + 1) * LANES]

    xq = proj(4)
    hd = w // X_HEADS
    for c in range(X_HEADS):
        chunk = xq[:, c * hd:(c + 1) * hd]
        cms = jnp.mean(chunk * chunk, axis=-1, keepdims=True)
        xq_ref[:, c * hd:(c + 1) * hd] = (chunk * lax.rsqrt(cms + EPS) * gcq_ref[...]).astype(BF16)

    d = xf.shape[1]
    for c in range(N_BRANCH):
        z = _dot(h, wg_ref[:, c * d:(c + 1) * d])
        g_ref[:, c * d:(c + 1) * d] = _sigmoid(z).astype(BF16)


def _in_proj(xt, g_mix, w_in, g_q, g_k, g_cq, seq_len, tm=512):
    t, d = xt.shape
    tm = min(tm, seq_len)
    w = d // 2
    wa = w_in[:, :5 * w].astype(BF16)
    wg = w_in[:, 5 * w:].astype(BF16)
    heads = w // MOBA_HEAD_DIM
    e64 = jnp.kron(jnp.eye(heads, dtype=F32), jnp.ones((MOBA_HEAD_DIM, MOBA_HEAD_DIM), F32)).astype(BF16)
    half = MOBA_HEAD_DIM // 2
    inv = ROPE_THETA ** (-jnp.arange(half, dtype=F32) / half)
    ang = jnp.arange(seq_len, dtype=F32)[:, None] * inv[None, :]
    cos = jnp.tile(jnp.cos(ang), (1, LANES // half))
    sin = jnp.tile(jnp.concatenate([-jnp.sin(ang), jnp.sin(ang)], axis=1), (1, LANES // MOBA_HEAD_DIM))
    nt = seq_len // tm
    row = lambda i: (i, 0)
    const = lambda i: (0, 0)
    out_w = jax.ShapeDtypeStruct((t, w), BF16)
    return pl.pallas_call(
        _inproj_kernel,
        grid=(t // tm,),
        in_specs=[
            pl.BlockSpec((tm, d), row),
            pl.BlockSpec((1, d), const),
            pl.BlockSpec((d, 5 * w), const, pipeline_mode=pl.Buffered(1)),
            pl.BlockSpec((d, N_BRANCH * d), const, pipeline_mode=pl.Buffered(1)),
            pl.BlockSpec((w, w), const, pipeline_mode=pl.Buffered(1)),
            pl.BlockSpec((1, w), const),
            pl.BlockSpec((1, w), const),
            pl.BlockSpec((1, w // X_HEADS), const),
            pl.BlockSpec((tm, LANES), lambda i: (i % nt, 0)),
            pl.BlockSpec((tm, LANES), lambda i: (i % nt, 0)),
        ],
        out_specs=[pl.BlockSpec((tm, w), row), pl.BlockSpec((tm, w), row), pl.BlockSpec((tm, 2 * w), row),
                   pl.BlockSpec((tm, w), row), pl.BlockSpec((tm, N_BRANCH * d), row),
                   pl.BlockSpec((heads, tm, LANES), lambda i: (0, i, 0))],
        out_shape=[jax.ShapeDtypeStruct((t, w), F32), out_w, jax.ShapeDtypeStruct((t, 2 * w), BF16), out_w,
                   jax.ShapeDtypeStruct((t, N_BRANCH * d), BF16),
                   jax.ShapeDtypeStruct((heads, t, LANES), F32)],
        compiler_params=_params("parallel"),
        name="in_proj",
    )(xt, g_mix.reshape(1, d), wa, wg, e64,
      jnp.tile(g_q, heads).reshape(1, w), jnp.tile(g_k, heads).reshape(1, w), g_cq.reshape(1, -1), cos, sin)


def _s5_matrices(lam_re, lam_im, log_dt, b_re, b_im, c_re, c_im, n_chunks):
    hp = lax.Precision.HIGHEST
    c = S5_CHUNK
    dt = jnp.exp(log_dt)[:, None]
    mag = jnp.exp(lam_re * dt)
    ar = mag * jnp.cos(lam_im * dt)
    ai = mag * jnp.sin(lam_im * dt)
    nr = ar - 1.0
    den = lam_re * lam_re + lam_im * lam_im
    cr = (nr * lam_re + ai * lam_im) / den
    ci = (ai * lam_re - nr * lam_im) / den
    bbr = cr[..., None] * b_re - ci[..., None] * b_im
    bbi = cr[..., None] * b_im + ci[..., None] * b_re

    def power(n):
        nf = n.astype(F32)[None, :, None]
        m = jnp.exp((lam_re * dt)[:, None, :] * nf)
        th = (lam_im * dt)[:, None, :] * nf
        return m * jnp.cos(th), m * jnp.sin(th)

    pr, pi = power(jnp.arange(c + 1))
    kbr = pr[..., None] * bbr[:, None] - pi[..., None] * bbi[:, None]
    kbi = pr[..., None] * bbi[:, None] + pi[..., None] * bbr[:, None]
    kk = jnp.einsum('ghp,gtpc->gthc', jnp.concatenate([c_re, -c_im], axis=-1),
                    jnp.concatenate([kbr, kbi], axis=2), precision=hp)
    tq = jnp.arange(c)
    g = kk.shape[0]
    rev = c - 1 - tq
    w_in = jnp.concatenate([kbr[:, rev].transpose(0, 1, 3, 2), kbi[:, rev].transpose(0, 1, 3, 2)], axis=-1)
    w_in = w_in.reshape(g, c * SSM_GROUP, 2 * SSM_STATE)
    prn, pin = pr[:, 1:], pi[:, 1:]
    wo_r = c_re[:, None] * prn[:, :, None, :] - c_im[:, None] * pin[:, :, None, :]
    wo_i = -c_re[:, None] * pin[:, :, None, :] - c_im[:, None] * prn[:, :, None, :]
    w_out = jnp.concatenate([wo_r, wo_i], axis=-1).transpose(0, 3, 1, 2).reshape(g, 2 * SSM_STATE, c * SSM_GROUP)
    n_steps = max(1, int(math.ceil(math.log2(n_chunks))))
    qr, qi = power(c * (2 ** jnp.arange(n_steps)))
    pa = jnp.concatenate([qr, qr], axis=-1)
    pb = jnp.concatenate([-qi, qi], axis=-1)
    return kk.astype(BF16), w_in.astype(BF16), w_out.astype(BF16), pa, pb


S5_SET = LANES // SSM_GROUP
S5_ROWS = 256


def _s5_kernel(u_ref, toep_ref, winc_ref, woutc_ref, pa_ref, pb_ref, y_ref, s_ref, win_ref, wout_ref):
    c = S5_CHUNK
    nc = u_ref.shape[0] // c
    rb = min(S5_ROWS, nc)
    p2 = 2 * SSM_STATE

    @pl.when(pl.program_id(1) == 0)
    def _():
        r_group = lax.broadcasted_iota(jnp.int32, (LANES, p2), 0) // SSM_GROUP
        l_group = lax.broadcasted_iota(jnp.int32, (p2, LANES), 1) // SSM_GROUP
        for t in range(c):
            for g in range(S5_SET):
                win_ref[0, t * LANES:(t + 1) * LANES, g * p2:(g + 1) * p2] = jnp.where(
                    r_group == g, winc_ref[0, t], jnp.zeros((LANES, p2), BF16))
                wout_ref[0, g * p2:(g + 1) * p2, t * LANES:(t + 1) * LANES] = jnp.where(
                    l_group == g, woutc_ref[0, t], jnp.zeros((p2, LANES), BF16))

    def chunk_rows(r0):
        return jnp.concatenate([u_ref[pl.ds(r0 * c + t, rb, stride=c), :] for t in range(c)], axis=1).astype(BF16)

    for blk in range(nc // rb):
        s_ref[blk * rb:(blk + 1) * rb, :] = _dot(chunk_rows(blk * rb), win_ref[0])
    row = lax.broadcasted_iota(jnp.int32, (nc, LANES), 0)
    n_steps = pa_ref.shape[1]
    for g in range(S5_SET):
        lanes = slice(g * LANES, (g + 1) * LANES)
        s = s_ref[:, lanes]
        for k in range(n_steps):
            sh = 1 << k
            if sh >= nc:
                break
            prev = jnp.where(row >= sh, pltpu.roll(s, sh, 0), 0.0)
            s = s + pa_ref[0, k:k + 1, lanes] * prev + pb_ref[0, k:k + 1, lanes] * pltpu.roll(prev, SSM_STATE, 1)
        s_ref[:, lanes] = jnp.where(row >= 1, pltpu.roll(s, 1, 0), 0.0)
    for blk in range(nc // rb):
        rows = slice(blk * rb, (blk + 1) * rb)
        a = chunk_rows(blk * rb)
        carried = _dot(s_ref[rows, :].astype(BF16), wout_ref[0])
        for t in range(c):
            y = _dot(a[:, :(t + 1) * LANES], toep_ref[0, (c - 1 - t) * LANES:, :])
            y_ref[pl.ds(blk * rb * c + t, rb, stride=c), :] = y + carried[:, t * LANES:(t + 1) * LANES]


def _s5_block_diag(mats):
    kk, w_in, w_out, pa, pb = mats
    g, c, h, p2 = kk.shape[0], S5_CHUNK, SSM_GROUP, 2 * SSM_STATE
    ns = g // S5_SET
    eye = jnp.eye(S5_SET, dtype=kk.dtype)
    lag_bd = jnp.einsum('sglhc,gk->slgckh', kk.reshape(ns, S5_SET, c + 1, h, h), eye).reshape(ns, c + 1, LANES, LANES)
    toep_bd = lag_bd[:, c - 1::-1].reshape(ns, c * LANES, LANES)

    w_in_c = w_in.reshape(ns, S5_SET, c, h, p2).transpose(0, 2, 1, 3, 4).reshape(ns, c, LANES, p2)
    w_out_c = w_out.reshape(ns, S5_SET, p2, c, h).transpose(0, 3, 2, 1, 4).reshape(ns, c, p2, LANES)
    k = pa.shape[1]
    lanes_of = lambda x: x.reshape(ns, S5_SET, k, p2).transpose(0, 2, 1, 3).reshape(ns, k, S5_SET * p2)
    return toep_bd, w_in_c, w_out_c, lanes_of(pa), lanes_of(pb)


def _s5(u, mats, bsz, seq_len):
    toep, w_in, w_out, pa, pb = _s5_block_diag(mats)
    ns = toep.shape[0]
    nc = seq_len // S5_CHUNK
    per_set = lambda s, b: (s, 0, 0)
    once = dict(pipeline_mode=pl.Buffered(1))
    io_spec = pl.BlockSpec((seq_len, LANES), lambda s, b: (b, s), **once)
    return pl.pallas_call(
        _s5_kernel,
        grid=(ns, bsz),
        in_specs=[
            io_spec,
            pl.BlockSpec((1,) + toep.shape[1:], per_set, **once),
            pl.BlockSpec((1,) + w_in.shape[1:], lambda s, b: (s, 0, 0, 0)),
            pl.BlockSpec((1,) + w_out.shape[1:], lambda s, b: (s, 0, 0, 0)),
            pl.BlockSpec((1,) + pa.shape[1:], per_set),
            pl.BlockSpec((1,) + pb.shape[1:], per_set),
        ],
        out_specs=io_spec,
        out_shape=jax.ShapeDtypeStruct(u.shape, F32),
        scratch_shapes=[pltpu.VMEM((nc, S5_SET * 2 * SSM_STATE), F32),
                        pltpu.VMEM((1, S5_CHUNK * LANES, S5_SET * 2 * SSM_STATE), BF16),
                        pltpu.VMEM((1, S5_SET * 2 * SSM_STATE, S5_CHUNK * LANES), BF16)],
        compiler_params=_params("arbitrary", "arbitrary"),
        name="s5_scan",
    )(u, toep, w_in, w_out, pa, pb)


def _moba_select_kernel(q_ref, k_ref, tri_ref, sel_ref, cnt_out_ref, km_ref, cnt_ref):
    h = pl.program_id(1)
    t = pl.program_id(2)
    ts = q_ref.shape[0]
    nbk = MOBA_MAX_BLOCKS

    @pl.when(t == 0)
    def _():
        kk = k_ref[...].astype(F32)
        nb = kk.shape[0] // MOBA_BLOCK
        km = jnp.sum(kk.reshape(nb, MOBA_BLOCK, LANES), axis=1) * (1.0 / MOBA_BLOCK)
        if nb < nbk:
            km = jnp.concatenate([km, jnp.zeros((nbk - nb, LANES), F32)], axis=0)
        lane = lax.broadcasted_iota(jnp.int32, (nbk, LANES), 1)
        km_ref[...] = jnp.where((lane // MOBA_HEAD_DIM) == (h % 2), km, 0.0).astype(BF16)
        cnt_ref[...] = jnp.zeros(cnt_ref.shape, F32)

    gate = _dot_nt(km_ref[...], q_ref[...])
    blk = lax.broadcasted_iota(jnp.int32, gate.shape, 0)
    qblk = (t * ts + lax.broadcasted_iota(jnp.int32, gate.shape, 1)) // MOBA_BLOCK
    g = jnp.where(blk < qblk, gate, -jnp.inf)
    selected = jnp.zeros(gate.shape, jnp.bool_)
    picks = []
    for _ in range(MOBA_TOPK):
        mx = jnp.max(g, axis=0, keepdims=True)
        idx = jnp.min(jnp.where(g == mx, blk, nbk), axis=0, keepdims=True)
        hit = blk == idx
        ok = (idx[0:1] < qblk[0:1]) & (mx > -jnp.inf)
        picks.append((hit, idx, ok))
        selected = selected | (hit & ok)
        g = jnp.where(hit, -jnp.inf, g)
    sel01 = jnp.where(selected, 1.0, 0.0)
    chunks = [sel01[:, c * LANES:(c + 1) * LANES] for c in range(ts // LANES)]
    within = _dot(jnp.concatenate(chunks, axis=0).astype(BF16), tri_ref[...])
    base = cnt_ref[...]
    pieces = []
    for c, chunk in enumerate(chunks):
        pieces.append(within[c * nbk:(c + 1) * nbk] + base)
        base = base + jnp.sum(chunk, axis=1, keepdims=True)
    prefix = jnp.concatenate(pieces, axis=1)
    rows = [jnp.where(ok, idx, -1) for (_, idx, ok) in picks]
    rows += [jnp.sum(jnp.where(hit, prefix, 0.0), axis=0, keepdims=True).astype(jnp.int32) for (hit, _, _) in picks]
    rows += [jnp.zeros((1, ts), jnp.int32)] * (8 - 2 * MOBA_TOPK)
    sel_ref[0, 0] = jnp.concatenate(rows, axis=0)
    cnt_ref[...] = base
    cnt_out_ref[0, 0] = base


def _moba_select(q2, kv, bsz, seq_len, ts=1024):
    ts = min(ts, seq_len)
    heads = q2.shape[1] // MOBA_HEAD_DIM
    nt = seq_len // ts
    tri = (jnp.arange(LANES)[:, None] < jnp.arange(LANES)[None, :]).astype(BF16)
    return pl.pallas_call(
        _moba_select_kernel,
        grid=(bsz, heads, nt),
        in_specs=[
            pl.BlockSpec((ts, LANES), lambda b, h, t: (b * nt + t, h // 2)),
            pl.BlockSpec((seq_len, LANES), lambda b, h, t: (b, 2 * (h // 2))),
            pl.BlockSpec((LANES, LANES), lambda b, h, t: (0, 0)),
        ],
        out_specs=[pl.BlockSpec((1, 1, 8, ts), lambda b, h, t: (b, h, 0, t)),
                   pl.BlockSpec((1, 1, MOBA_MAX_BLOCKS, LANES), lambda b, h, t: (b, h, 0, 0))],
        out_shape=[jax.ShapeDtypeStruct((bsz, heads, 8, seq_len), jnp.int32),
                   jax.ShapeDtypeStruct((bsz, heads, MOBA_MAX_BLOCKS, LANES), F32)],
        scratch_shapes=[pltpu.VMEM((MOBA_MAX_BLOCKS, LANES), BF16), pltpu.VMEM((MOBA_MAX_BLOCKS, LANES), F32)],
        compiler_params=_params("parallel", "arbitrary", "arbitrary"),
        name="moba_select",
    )(q2, kv, tri)


MOBA_DIAG_BLOCKS = 8


def _moba_diag_kernel(q_ref, kv_ref, o_ref, lse_ref):
    lane = lax.broadcasted_iota(jnp.int32, (MOBA_BLOCK, LANES), 1)
    is_a = lane < MOBA_HEAD_DIM
    r = lax.broadcasted_iota(jnp.int32, (MOBA_BLOCK, MOBA_BLOCK), 0)
    c = lax.broadcasted_iota(jnp.int32, (MOBA_BLOCK, MOBA_BLOCK), 1)
    for u in range(MOBA_DIAG_BLOCKS):
        rows = pl.ds(u * MOBA_BLOCK, MOBA_BLOCK)
        q = q_ref[rows, :]
        kb = kv_ref[rows, :LANES]
        vb = kv_ref[rows, LANES:]
        pvs, ms = [], []
        for own in (is_a, jnp.logical_not(is_a)):
            s = jnp.where(c <= r, _dot_nt(jnp.where(own, q, jnp.zeros_like(q)), kb), NEG_BIG)
            m = jnp.max(s, axis=1, keepdims=True)
            p = jnp.exp(s - m).astype(BF16)
            pvs.append(_dot(p, jnp.where(own, vb, 1.0)))
            ms.append(m)
        num = jnp.where(is_a, pvs[0], pvs[1])
        den = pltpu.roll(jnp.where(is_a, pvs[1], pvs[0]), MOBA_HEAD_DIM, 1)
        o_ref[rows, :] = num / den
        lse_ref[rows, :] = jnp.where(is_a, ms[0], ms[1]) + jnp.log(den)


def _moba_diag(q2, kv, bsz, seq_len):
    npair = q2.shape[1] // LANES
    rows = MOBA_BLOCK * MOBA_DIAG_BLOCKS
    nb = seq_len // rows
    spec = pl.BlockSpec((rows, LANES), lambda b, p, i: (b * nb + i, p))
    out = jax.ShapeDtypeStruct(q2.shape, F32)
    return pl.pallas_call(
        _moba_diag_kernel,
        grid=(bsz, npair, nb),
        in_specs=[spec, pl.BlockSpec((rows, 2 * LANES), lambda b, p, i: (b * nb + i, p))],
        out_specs=[spec, spec],
        out_shape=[out, out],
        compiler_params=_params("parallel", "parallel", "parallel"),
        name="moba_diag",
    )(q2, kv)


MOBA_TILE = 256
MOBA_TILES_PER_STEP = 8


def _moba_grouped_kernel(tile_row_ref, tile_pair_ref, tile_half_ref, tile_real_ref, qd_ref, *refs):
    n = MOBA_TILES_PER_STEP
    kv_refs, o_ref = refs[:n], refs[n + 1]
    i = pl.program_id(0)
    lane = lax.broadcasted_iota(jnp.int32, (MOBA_TILE, LANES), 1)
    kv_lane = lax.broadcasted_iota(jnp.int32, (MOBA_BLOCK, LANES), 1)

    @pl.when(tile_real_ref[i * n] > 0)
    def _():
        for u in range(n):
            tile = i * n + u
            rows = pl.ds(u * MOBA_TILE, MOBA_TILE)
            own = (lane // MOBA_HEAD_DIM) == tile_half_ref[tile]
            s = _dot_nt(qd_ref[rows, :].astype(BF16), kv_refs[u][:, :LANES])
            m = jnp.max(s, axis=1, keepdims=True)
            p = jnp.exp(s - m).astype(BF16)
            vb = jnp.where((kv_lane // MOBA_HEAD_DIM) == tile_half_ref[tile], kv_refs[u][:, LANES:], 1.0)
            pv = _dot(p, vb)
            row_sum = pltpu.roll(pv, MOBA_HEAD_DIM, 1)
            part = jnp.where(own, pv / row_sum, m + jnp.log(pv))
            o_ref[rows, :] = jnp.where(tile_real_ref[tile] > 0, part, NEG_BIG)

    @pl.when(tile_real_ref[i * n] == 0)
    def _():
        o_ref[...] = jnp.full(o_ref.shape, NEG_BIG, F32)


def _moba_grouped(qd, kv, tile_row, tile_pair, tile_half, tile_real, run_after):
    n = MOBA_TILES_PER_STEP
    n_tiles = qd.shape[0] // MOBA_TILE

    def kv_spec(u):
        return pl.BlockSpec((MOBA_BLOCK, 2 * LANES), lambda i, tr, tp, th, tl: (tr[i * n + u], tp[i * n + u]))

    grid_spec = pltpu.PrefetchScalarGridSpec(
        num_scalar_prefetch=4,
        grid=(n_tiles // n,),
        in_specs=[pl.BlockSpec((n * MOBA_TILE, LANES), lambda i, tr, tp, th, tl: (i, 0))]
        + [kv_spec(u) for u in range(n)]
        + [pl.BlockSpec((8, LANES), lambda i, tr, tp, th, tl: (0, 0))],
        out_specs=pl.BlockSpec((n * MOBA_TILE, LANES), lambda i, tr, tp, th, tl: (i, 0)),
    )
    return pl.pallas_call(
        _moba_grouped_kernel,
        grid_spec=grid_spec,
        out_shape=jax.ShapeDtypeStruct(qd.shape, F32),
        compiler_params=_params("arbitrary"),
        name="moba_grouped",
    )(tile_row, tile_pair, tile_half, tile_real, qd, *([kv] * n), run_after)


def _moba_combine_kernel(od_ref, lsed_ref, *refs):
    g_refs, o_ref = refs[:-1], refs[-1]
    lane = lax.broadcasted_iota(jnp.int32, od_ref.shape, 1)
    is_a = lane < MOBA_HEAD_DIM
    parts = [(od_ref[...], lsed_ref[...])]
    for s in range(MOBA_TOPK):
        xa = g_refs[s][0, 0, 0]
        xb = g_refs[MOBA_TOPK + s][0, 0, 0]
        o = jnp.where(is_a, xa, xb)
        lse = pltpu.roll(jnp.where(is_a, xb, xa), MOBA_HEAD_DIM, 1)
        parts.append((o, lse))
    m = parts[0][1]
    for _, lse in parts[1:]:
        m = jnp.maximum(m, lse)
    num = jnp.zeros(od_ref.shape, F32)
    den = jnp.zeros(od_ref.shape, F32)
    for o, lse in parts:
        w = jnp.exp(lse - m)
        num = num + w * o
        den = den + w
    o_ref[...] = (num / den).astype(BF16)


def _moba_combine(od, lsed, gath, bsz, seq_len, tm=2048):
    tm = min(tm, seq_len)
    npair = od.shape[1] // LANES
    nt = seq_len // tm
    spec = pl.BlockSpec((tm, LANES), lambda b, p, i: (b * nt + i, p))

    def g_spec(e, s):
        return pl.BlockSpec((1, 1, 1, tm, LANES), lambda b, p, i: (b, 2 * p + e, s, i, 0))

    return pl.pallas_call(
        _moba_combine_kernel,
        grid=(bsz, npair, nt),
        in_specs=[spec, spec] + [g_spec(e, s) for e in range(2) for s in range(MOBA_TOPK)],
        out_specs=spec,
        out_shape=jax.ShapeDtypeStruct(od.shape, BF16),
        compiler_params=_params("parallel", "parallel", "parallel"),
        name="moba_combine",
    )(od, lsed, *([gath] * (2 * MOBA_TOPK)))


SC_WINDOW = 128
SC_CORES = 2
SC_SUBCORES = 16


def _sc_mesh():
    return plsc.VectorSubcoreMesh(core_axis_name="core", subcore_axis_name="subcore")


def _sc_gather_rows(table, idx):
    n = idx.shape[0]
    d = table.shape[1]
    window = SC_WINDOW
    assert n % (window * SC_CORES * SC_SUBCORES) == 0
    per_core = n // window // SC_CORES

    @functools.partial(pl.kernel, out_type=jax.ShapeDtypeStruct((n, d), table.dtype), mesh=_sc_mesh())
    def gather_kernel(x_hbm, i_hbm, o_hbm):
        base = lax.axis_index("core") * per_core

        def body(i_vmem, o_vmem):
            pltpu.sync_copy(x_hbm.at[i_vmem.at[0]], o_vmem)

        pltpu.emit_pipeline(
            body,
            grid=(per_core,),
            in_specs=[pl.BlockSpec((1, window), index_map=lambda i: (0, base + i))],
            out_specs=[pl.BlockSpec((window, d), index_map=lambda i: (base + i, 0))],
            core_axis_name="subcore",
            dimension_semantics=(pltpu.PARALLEL,),
            trace_scopes=False,
        )(i_hbm, o_hbm)

    return gather_kernel(table, idx.reshape(1, n))


def _sc_scatter_rows(rows, dest, n_out, repeat):
    n_src, d = rows.shape
    n = dest.shape[0]
    assert n == repeat * n_src
    window = SC_WINDOW
    assert n % (window * SC_CORES * SC_SUBCORES) == 0
    per_core = n // window // SC_CORES
    src_windows = n_src // window

    @functools.partial(pl.kernel, out_type=jax.ShapeDtypeStruct((n_out, d), rows.dtype), mesh=_sc_mesh())
    def scatter_kernel(x_hbm, i_hbm, o_hbm):
        base = lax.axis_index("core") * per_core

        def body(x_vmem, i_vmem):
            pltpu.sync_copy(x_vmem, o_hbm.at[i_vmem.at[0]])

        pltpu.emit_pipeline(
            body,
            grid=(per_core,),
            in_specs=[pl.BlockSpec((window, d), index_map=lambda i: ((base + i) % src_windows, 0)),
                      pl.BlockSpec((1, window), index_map=lambda i: (0, base + i))],
            out_specs=[],
            core_axis_name="subcore",
            dimension_semantics=(pltpu.PARALLEL,),
            trace_scopes=False,
        )(x_hbm, i_hbm)

    return scatter_kernel(rows, dest.reshape(1, n))


def _moba_dispatch(q2, qh, kv, bsz, seq_len):
    heads = q2.shape[1] // MOBA_HEAD_DIM
    nbk = MOBA_MAX_BLOCKS
    sel, cnt = _moba_select(q2, kv, bsz, seq_len)
    counts = cnt[..., 0].astype(jnp.int32).reshape(bsz * heads * nbk)
    pcounts = ((counts + MOBA_TILE - 1) // MOBA_TILE) * MOBA_TILE
    pends = jnp.cumsum(pcounts)
    pstarts = (pends - pcounts).reshape(bsz, heads, 1, 1, nbk)
    n_items = bsz * heads * seq_len * MOBA_TOPK
    step_rows = MOBA_TILE * MOBA_TILES_PER_STEP
    n_rows = -(-(n_items + bsz * heads * nbk * MOBA_TILE) // step_rows) * step_rows
    n_null = step_rows
    idx = sel[:, :, 0:MOBA_TOPK, :]
    rank = sel[:, :, MOBA_TOPK:2 * MOBA_TOPK, :]
    start = jnp.sum(jnp.where(idx[..., None] == jnp.arange(nbk), pstarts, 0), axis=-1)
    null_row = n_rows + jnp.arange(seq_len, dtype=jnp.int32) % n_null
    dest = jnp.where(idx >= 0, start + rank, null_row)
    n_tiles = (n_rows + n_null) // MOBA_TILE
    n_groups = bsz * heads * nbk
    tile_start = jnp.arange(n_tiles, dtype=jnp.int32) * MOBA_TILE
    tile_g = jnp.minimum(jnp.sum((pends[None, :] <= tile_start[:, None]).astype(jnp.int32), axis=1), n_groups - 1)
    tile_real = (tile_start < pends[-1]).astype(jnp.int32)
    tile_head = (tile_g // nbk) % heads
    tile_row = (tile_g // (heads * nbk)) * (seq_len // MOBA_BLOCK) + jnp.minimum(tile_g % nbk, seq_len // MOBA_BLOCK - 1)
    dest_by_slot = dest.transpose(2, 1, 0, 3).reshape(-1)
    qd = _sc_scatter_rows(qh.reshape(-1, LANES), dest_by_slot, n_rows + n_null, MOBA_TOPK)
    return qd, dest, (tile_row, tile_head // 2, tile_head % 2, tile_real)


def _moba_finish(qd, dest, tiles, od, lsed, kv, bsz, seq_len):
    heads = dest.shape[1]
    part = _moba_grouped(qd, kv, *tiles, lsed)
    gath = _sc_gather_rows(part, dest.reshape(-1)).reshape(bsz, heads, MOBA_TOPK, seq_len, LANES)
    return _moba_combine(od, lsed, gath, bsz, seq_len)


def _mem_kv_kernel(mem_ref, gmem_ref, w_ref, gck_ref, k_ref, v_ref):
    xf = mem_ref[0]
    ms = jnp.mean(xf * xf, axis=-1, keepdims=True)
    h = (xf * lax.rsqrt(ms + EPS) * gmem_ref[...]).astype(BF16)
    kv = _dot(h, w_ref[...])
    w = k_ref.shape[2]
    hd = w // X_HEADS
    for c in range(X_HEADS):
        chunk = kv[:, c * hd:(c + 1) * hd]
        cms = jnp.mean(chunk * chunk, axis=-1, keepdims=True)
        k_ref[0, :, c * hd:(c + 1) * hd] = (chunk * lax.rsqrt(cms + EPS) * gck_ref[...]).astype(BF16)
    v_ref[0] = kv[:, w:].astype(BF16)


def _mem_kv(mem, g_mem, w_kv_mem, g_ck):
    bsz, m, d = mem.shape
    w = w_kv_mem.shape[1] // 2
    const = lambda b: (0, 0)
    out = jax.ShapeDtypeStruct((bsz, m, w), BF16)
    return pl.pallas_call(
        _mem_kv_kernel,
        grid=(bsz,),
        in_specs=[pl.BlockSpec((1, m, d), lambda b: (b, 0, 0)), pl.BlockSpec((1, d), const),
                  pl.BlockSpec((d, 2 * w), const), pl.BlockSpec((1, w // X_HEADS), const)],
        out_specs=[pl.BlockSpec((1, m, w), lambda b: (b, 0, 0))] * 2,
        out_shape=[out, out],
        compiler_params=_params("parallel"),
        name="mem_kv",
    )(mem, g_mem.reshape(1, d), w_kv_mem.astype(BF16), g_ck.reshape(1, -1))


def _merge_kernel(x_ref, ys_ref, u_ref, dskip_ref, om_ref, xq_ref, kc_ref, vc_ref, g_ref,
                  wglu_ref, wmo_ref, wco_ref, wout_ref, gffn_ref, wr_ref, br_ref,
                  x1_ref, h2_ref, logit_ref):
    d = x_ref.shape[1]
    y = ys_ref[...].astype(F32) + dskip_ref[...] * u_ref[...].astype(F32)
    ge = 0.5 * y * (1.0 + jnp.tanh(math.sqrt(2.0 / math.pi) * (y + 0.044715 * (y * y * y))))
    z = _dot(ge.astype(BF16), wglu_ref[...])
    merged = g_ref[:, 0:d].astype(F32) * (z[:, :d] * _sigmoid(z[:, d:]))
    merged = merged + g_ref[:, d:2 * d].astype(F32) * _dot(om_ref[...], wmo_ref[...])
    w = xq_ref.shape[1]
    hd = w // X_HEADS
    heads = []
    for c in range(X_HEADS):
        s = _dot_nt(xq_ref[:, c * hd:(c + 1) * hd], kc_ref[0, :, c * hd:(c + 1) * hd]) * (hd ** -0.5)
        p = jnp.exp(s - jnp.max(s, axis=1, keepdims=True))
        p = p / jnp.sum(p, axis=1, keepdims=True)
        heads.append(_dot(p.astype(BF16), vc_ref[0, :, c * hd:(c + 1) * hd]))
    oc = jnp.concatenate(heads, axis=1).astype(BF16)
    merged = merged + g_ref[:, 2 * d:3 * d].astype(F32) * _dot(oc, wco_ref[...])
    x1 = x_ref[...] + _dot(merged.astype(BF16), wout_ref[...])
    x1_ref[...] = x1
    ms = jnp.mean(x1 * x1, axis=-1, keepdims=True)
    h2 = (x1 * lax.rsqrt(ms + EPS) * gffn_ref[...]).astype(BF16)
    packed = _pack_bf16_pairs(h2.astype(F32))
    for j in range(h2_ref.shape[0]):
        h2_ref[j] = packed[:, j * LANES:(j + 1) * LANES]
    logit_ref[...] = _dot_nt(wr_ref[...], h2) + br_ref[...]


def _merge(xt, ys, u, d_skip, om, xqn, kc, vc, gates, w_glu, w_mo, w_co, w_out, g_ffn, w_router, b_router,
           seq_len, tm=512):
    t, d = xt.shape
    tm = min(tm, seq_len)
    w = d // 2
    m = kc.shape[1]
    ne = w_router.shape[1]
    nt = seq_len // tm
    row = lambda i: (i, 0)
    const = lambda i: (0, 0)
    per_b = lambda i: (i // nt, 0, 0)
    once = dict(pipeline_mode=pl.Buffered(1))
    return pl.pallas_call(
        _merge_kernel,
        grid=(t // tm,),
        in_specs=[
            pl.BlockSpec((tm, d), row), pl.BlockSpec((tm, w), row), pl.BlockSpec((tm, w), row),
            pl.BlockSpec((1, w), const), pl.BlockSpec((tm, w), row), pl.BlockSpec((tm, w), row),
            pl.BlockSpec((1, m, w), per_b), pl.BlockSpec((1, m, w), per_b),
            pl.BlockSpec((tm, N_BRANCH * d), row),
            pl.BlockSpec((w, 2 * d), const, **once), pl.BlockSpec((w, d), const, **once),
            pl.BlockSpec((w, d), const, **once), pl.BlockSpec((d, d), const, **once), pl.BlockSpec((1, d), const),
            pl.BlockSpec((ne, d), const), pl.BlockSpec((ne, 1), const),
        ],
        out_specs=[pl.BlockSpec((tm, d), row), pl.BlockSpec((d // 2 // LANES, tm, LANES), lambda i: (0, i, 0)),
                   pl.BlockSpec((ne, tm), lambda i: (0, i))],
        out_shape=[jax.ShapeDtypeStruct((t, d), F32), jax.ShapeDtypeStruct((d // 2 // LANES, t, LANES), jnp.uint32),
                   jax.ShapeDtypeStruct((ne, t), F32)],
        compiler_params=_params("parallel"),
        name="merge",
    )(xt, ys, u, d_skip.reshape(1, w), om, xqn, kc, vc, gates,
      w_glu.astype(BF16), w_mo.astype(BF16), w_co.astype(BF16), w_out.astype(BF16),
      g_ffn.reshape(1, d), w_router.T.astype(BF16), b_router.reshape(ne, 1))


def _moe_kernel(blk_e_ref, blk_used_ref, xs_ref, wgu_ref, bgu_ref, wd_ref, bd_ref, y_ref, wgu_bf, wd_bf):
    i = pl.program_id(0)
    prev = blk_e_ref[jnp.maximum(i - 1, 0)]

    @pl.when((i == 0) | (blk_e_ref[i] != prev))
    def _():
        wgu_bf[...] = wgu_ref[0].astype(BF16)
        wd_bf[...] = wd_ref[0].astype(BF16)

    @pl.when(blk_used_ref[i] > 0)
    def _():
        de = wd_bf.shape[0]
        words = jnp.concatenate([xs_ref[j] for j in range(xs_ref.shape[0])], axis=1)
        xs = jnp.concatenate(_unpack_bf16_pairs(words), axis=1).astype(BF16)
        gu = _dot(xs, wgu_bf[...]) + bgu_ref[0]
        gate = jnp.minimum(gu[:, :de], SWIGLU_LIMIT)
        up = jnp.clip(gu[:, de:], -SWIGLU_LIMIT, SWIGLU_LIMIT)
        act = gate * _sigmoid(SWIGLU_ALPHA * gate) * (up + 1.0)
        y = _dot(act.astype(BF16), wd_bf[...]) + bd_ref[0]
        packed = _pack_bf16_pairs(y.astype(BF16).astype(F32))
        for j in range(y_ref.shape[0]):
            y_ref[j] = packed[:, j * LANES:(j + 1) * LANES]

    @pl.when(blk_used_ref[i] == 0)
    def _():
        y_ref[...] = jnp.zeros(y_ref.shape, y_ref.dtype)


def _moe_experts(xs, blk_e, blk_used, w_gu, b_gu, w_down, b_down):
    slabs, p, _ = xs.shape
    d = 2 * slabs * LANES
    ne, _, de2 = w_gu.shape
    de = de2 // 2
    nblk = p // EXPERT_ROWS
    row_spec = pl.BlockSpec((slabs, EXPERT_ROWS, LANES), lambda i, e, n: (0, i, 0))
    grid_spec = pltpu.PrefetchScalarGridSpec(
        num_scalar_prefetch=2,
        grid=(nblk,),
        in_specs=[
            row_spec,
            pl.BlockSpec((1, d, de2), lambda i, e, n: (e[i], 0, 0)),
            pl.BlockSpec((1, 1, de2), lambda i, e, n: (e[i], 0, 0)),
            pl.BlockSpec((1, de, d), lambda i, e, n: (e[i], 0, 0)),
            pl.BlockSpec((1, 1, d), lambda i, e, n: (e[i], 0, 0)),
        ],
        out_specs=row_spec,
        scratch_shapes=[pltpu.VMEM((d, de2), BF16), pltpu.VMEM((de, d), BF16)],
    )
    return pl.pallas_call(
        _moe_kernel,
        grid_spec=grid_spec,
        out_shape=jax.ShapeDtypeStruct(xs.shape, jnp.uint32),
        compiler_params=_params("arbitrary"),
        name="moe_experts",
    )(blk_e, blk_used, xs, w_gu, b_gu.reshape(ne, 1, de2), w_down, b_down.reshape(ne, 1, d))


def _router_kernel(lg_ref, tri_ref, e_ref, w_ref, r_ref, cnt_out_ref, cnt_ref):
    t = pl.program_id(0)
    ne, ts = lg_ref.shape

    @pl.when(t == 0)
    def _():
        cnt_ref[...] = jnp.zeros(cnt_ref.shape, F32)

    g = lg_ref[...]
    eid = lax.broadcasted_iota(jnp.int32, g.shape, 0)
    selected = jnp.zeros(g.shape, jnp.bool_)
    picks = []
    for _ in range(TOPK_EXPERTS):
        mx = jnp.max(g, axis=0, keepdims=True)
        idx = jnp.min(jnp.where(g == mx, eid, ne), axis=0, keepdims=True)
        hit = eid == idx
        picks.append((hit, idx, mx))
        selected = selected | hit
        g = jnp.where(hit, -jnp.inf, g)
    exps = [jnp.exp(mx - picks[0][2]) for (_, _, mx) in picks]
    total = sum(exps)
    sel01 = jnp.where(selected, 1.0, 0.0).astype(BF16)
    prefix = _dot(sel01, tri_ref[...]) + jnp.tile(cnt_ref[...], (1, ts // LANES))
    pad_i = [jnp.zeros((1, ts), jnp.int32)] * (8 - TOPK_EXPERTS)
    e_ref[...] = jnp.concatenate([idx for (_, idx, _) in picks] + pad_i, axis=0)
    w_ref[...] = jnp.concatenate([e / total for e in exps] + [jnp.zeros((1, ts), F32)] * (8 - TOPK_EXPERTS), axis=0)
    r_ref[...] = jnp.concatenate(
        [jnp.sum(jnp.where(hit, prefix, 0.0), axis=0, keepdims=True).astype(jnp.int32) for (hit, _, _) in picks]
        + pad_i, axis=0)
    cnt_ref[...] = cnt_ref[...] + _dot(sel01, jnp.ones((ts, LANES), BF16))
    cnt_out_ref[...] = cnt_ref[...]


def _router(logits_t, ts=1024):
    ne, t = logits_t.shape
    ts = min(ts, t)
    tri = (jnp.arange(ts)[:, None] < jnp.arange(ts)[None, :]).astype(BF16)
    rows = pl.BlockSpec((8, ts), lambda i: (0, i))
    return pl.pallas_call(
        _router_kernel,
        grid=(t // ts,),
        in_specs=[pl.BlockSpec((ne, ts), lambda i: (0, i)), pl.BlockSpec((ts, ts), lambda i: (0, 0))],
        out_specs=[rows, rows, rows, pl.BlockSpec((ne, LANES), lambda i: (0, 0))],
        out_shape=[jax.ShapeDtypeStruct((8, t), jnp.int32), jax.ShapeDtypeStruct((8, t), F32),
                   jax.ShapeDtypeStruct((8, t), jnp.int32), jax.ShapeDtypeStruct((ne, LANES), F32)],
        scratch_shapes=[pltpu.VMEM((ne, LANES), F32)],
        compiler_params=_params("arbitrary"),
        name="moe_router",
    )(logits_t, tri)


def _moe_mix_kernel(x1_ref, w_ref, pk_ref, o_ref):
    slabs = pk_ref.shape[0]
    wts = w_ref[...]
    lo = [jnp.zeros((x1_ref.shape[0], LANES), F32)] * slabs
    hi = [jnp.zeros((x1_ref.shape[0], LANES), F32)] * slabs
    for k in range(TOPK_EXPERTS):
        wk = wts[:, k:k + 1]
        for j in range(slabs):
            a, b = _unpack_bf16_pairs(pk_ref[j, k])
            lo[j] = lo[j] + wk * a
            hi[j] = hi[j] + wk * b
    o_ref[...] = x1_ref[...] + jnp.concatenate(lo + hi, axis=1)


def _moe_mix(x1, weights_tk, picked, tm=1024):
    t, d = x1.shape
    slabs = picked.shape[0]
    return pl.pallas_call(
        _moe_mix_kernel,
        grid=(t // tm,),
        in_specs=[pl.BlockSpec((tm, d), lambda i: (i, 0)),
                  pl.BlockSpec((tm, TOPK_EXPERTS), lambda i: (i, 0)),
                  pl.BlockSpec((slabs, TOPK_EXPERTS, tm, LANES), lambda i: (0, 0, i, 0))],
        out_specs=pl.BlockSpec((tm, d), lambda i: (i, 0)),
        out_shape=jax.ShapeDtypeStruct((t, d), F32),
        compiler_params=_params("parallel"),
        name="moe_mix",
    )(x1, weights_tk, picked)


def _moe_ffn(x1, h2, logits_t, w_gu, b_gu, w_down, b_down):
    t, d = x1.shape
    tk = t * TOPK_EXPERTS
    e8, w8, r8, cnt = _router(logits_t)
    top_e, weights, rank = e8[:TOPK_EXPERTS], w8[:TOPK_EXPERTS], r8[:TOPK_EXPERTS]
    counts = cnt[:, 0].astype(jnp.int32)
    pcounts = ((counts + EXPERT_ROWS - 1) // EXPERT_ROWS) * EXPERT_ROWS
    pends = jnp.cumsum(pcounts)
    pstarts = pends - pcounts
    is_e = top_e[..., None] == jnp.arange(N_EXPERTS, dtype=jnp.int32)
    dest_by_k = jnp.sum(jnp.where(is_e, pstarts, 0), axis=-1) + rank
    nblk = -(-tk // EXPERT_ROWS) + N_EXPERTS
    blk_start = jnp.arange(nblk, dtype=jnp.int32) * EXPERT_ROWS
    blk_e = jnp.minimum(jnp.sum((pends[None, :] <= blk_start[:, None]).astype(jnp.int32), axis=1), N_EXPERTS - 1)
    blk_used = (blk_start < pends[-1]).astype(jnp.int32)
    slabs = d // 2 // LANES
    p = nblk * EXPERT_ROWS
    slab_off = jnp.arange(slabs, dtype=jnp.int32) * p
    dest_kst = (dest_by_k[:, None, :] + slab_off[None, :, None]).reshape(-1)
    dest_skt = (dest_by_k[None, :, :] + slab_off[:, None, None]).reshape(-1)
    xs = _sc_scatter_rows(h2.reshape(slabs * t, LANES), dest_kst, slabs * p, TOPK_EXPERTS)
    ys = _moe_experts(xs.reshape(slabs, p, LANES), blk_e, blk_used, w_gu, b_gu, w_down, b_down)
    picked = _sc_gather_rows(ys.reshape(slabs * p, LANES), dest_skt).reshape(slabs, TOPK_EXPERTS, t, LANES)
    return _moe_mix(x1, weights.T, picked)


def kernel(x, mem, g_mix, w_in, lam_re, lam_im, log_dt, b_re, b_im, c_re, c_im, d_skip, w_glu, g_q, g_k, w_moba_out, g_mem, w_kv_mem, g_cq, g_ck, w_cross_out, w_out, g_ffn, w_router, b_router, w_gu, b_gu, w_down, b_down):
    bsz, seq_len, d = x.shape
    xt = x.reshape(bsz * seq_len, d)
    for l in range(g_mix.shape[0]):
        u, q2, kv, xqn, gates, qh = _in_proj(xt, g_mix[l], w_in[l], g_q[l], g_k[l], g_cq[l], seq_len)
        qd, dest, tiles = _moba_dispatch(q2, qh, kv, bsz, seq_len)
        mats = _s5_matrices(lam_re[l], lam_im[l], log_dt[l], b_re[l], b_im[l], c_re[l], c_im[l],
                            seq_len // S5_CHUNK)
        ys = _s5(u, mats, bsz, seq_len)
        od, lsed = _moba_diag(q2, kv, bsz, seq_len)
        kc, vc = _mem_kv(mem, g_mem[l], w_kv_mem[l], g_ck[l])
        om = _moba_finish(qd, dest, tiles, od, lsed, kv, bsz, seq_len)
        x1, h2, logits = _merge(xt, ys, u, d_skip[l], om, xqn, kc, vc, gates, w_glu[l], w_moba_out[l],
                                w_cross_out[l], w_out[l], g_ffn[l], w_router[l], b_router[l], seq_len)
        xt = _moe_ffn(x1, h2, logits, w_gu[l], b_gu[l], w_down[l], b_down[l])
    return xt.reshape(bsz, seq_len, d)
```
